```python
import jax, jax.numpy as jnp
from jax import lax
import numpy as np

D_MODEL = 1024
BATCH = 8
SEQ = 4096
DEPTH = 2

N_MEM = 256
MIX_WIDTH = 2 * D_MODEL
XA_HEADS = 4
XA_WIDTH = MIX_WIDTH // 4
XA_HEAD_DIM = XA_WIDTH // XA_HEADS
BRANCH_WIDTH = (MIX_WIDTH - XA_WIDTH) // 2
CHUNK = 128
A_HEADS = 4
A_HEAD_DIM = BRANCH_WIDTH // A_HEADS
SHORT_CONV = 3
POOL_WINDOWS = (2, 4, 8, 16)
C_GROUP = BRANCH_WIDTH // len(POOL_WINDOWS)
CONF_CONV = 31
EPS = 1e-6

N_EVEN = (DEPTH + 1) // 2
N_ODD = DEPTH // 2
EVEN_SPLITS = (BRANCH_WIDTH, BRANCH_WIDTH, BRANCH_WIDTH, BRANCH_WIDTH, BRANCH_WIDTH, XA_WIDTH, MIX_WIDTH)
ODD_SPLITS = (BRANCH_WIDTH, BRANCH_WIDTH, BRANCH_WIDTH, XA_WIDTH, MIX_WIDTH)
EVEN_IN = sum(EVEN_SPLITS)
ODD_IN = sum(ODD_SPLITS)

kernel_name = "hybrid_sgu_shortconv_pool_conformer_memxattn"


def _split(p, sizes):
    idx = [int(v) for v in np.cumsum(sizes)[:-1]]
    return jnp.split(p, idx, axis=-1)


def rms_norm(x, g):
    xf = x.astype(jnp.float32)
    y = xf * lax.rsqrt(jnp.mean(xf * xf, axis=-1, keepdims=True) + EPS)
    return (y * g.astype(jnp.float32)).astype(x.dtype)


def layer_norm(x, g, b):
    xf = x.astype(jnp.float32)
    mu = jnp.mean(xf, axis=-1, keepdims=True)
    var = jnp.mean(jnp.square(xf - mu), axis=-1, keepdims=True)
    y = (xf - mu) * lax.rsqrt(var + EPS)
    return (y * g.astype(jnp.float32) + b.astype(jnp.float32)).astype(x.dtype)


def causal_dwconv(x, w):
    k = w.shape[0]
    return lax.conv_general_dilated(
        x, w[:, None, :].astype(x.dtype), window_strides=(1,), padding=[(k - 1, 0)],
        dimension_numbers=("NWC", "WIO", "NWC"), feature_group_count=x.shape[-1])


def memory_cross_attention(q, mem_n, w_kv):
    bsz, s, _ = q.shape
    k, v = _split(jnp.einsum("bmd,de->bme", mem_n, w_kv), (XA_WIDTH, XA_WIDTH))
    q = q.reshape(bsz, s, XA_HEADS, XA_HEAD_DIM)
    k = k.reshape(bsz, -1, XA_HEADS, XA_HEAD_DIM)
    v = v.reshape(bsz, -1, XA_HEADS, XA_HEAD_DIM)
    scores = jnp.einsum("bshd,bmhd->bhsm", q, k).astype(jnp.float32) * (XA_HEAD_DIM ** -0.5)
    probs = jax.nn.softmax(scores, axis=-1).astype(v.dtype)
    return jnp.einsum("bhsm,bmhd->bshd", probs, v).reshape(bsz, s, XA_WIDTH)


def spatial_gating(u, v, ln_g, ln_b, w_s, b_s):
    bsz, s, _ = v.shape
    v = layer_norm(v, ln_g, ln_b).reshape(bsz, s // CHUNK, CHUNK, A_HEADS, A_HEAD_DIM)
    causal = jnp.tril(jnp.ones((CHUNK, CHUNK), dtype=bool))
    w = jnp.where(causal[None], w_s, 0.0).astype(v.dtype)
    sg = jnp.einsum("hts,bnshd->bnthd", w, v) + b_s.T[None, None, :, :, None]
    return u * sg.reshape(bsz, s, BRANCH_WIDTH)


def short_gated_conv(bg, cg, xin, w_conv):
    return bg * causal_dwconv(cg * xin, w_conv)


def multiscale_pool(z, w_grp, scale):
    s = z.shape[1]
    zf = z.astype(jnp.float32)
    csum = jnp.pad(jnp.cumsum(zf, axis=1), ((0, 0), (1, 0), (0, 0)))
    t = jnp.arange(1, s + 1)
    outs = []
    for g, win in enumerate(POOL_WINDOWS):
        sl = slice(g * C_GROUP, (g + 1) * C_GROUP)
        c = csum[..., sl]
        lo = jnp.pad(c, ((0, 0), (win, 0), (0, 0)))[:, : s + 1]
        cnt = jnp.minimum(t, win).astype(jnp.float32)[None, :, None]
        pooled = (c[:, 1:] - lo[:, 1:]) / cnt - zf[..., sl]
        outs.append(jnp.einsum("bsc,cd->bsd", pooled.astype(z.dtype), w_grp[g]))
    return jnp.concatenate(outs, axis=-1) * scale


def conformer_conv(a, b, w_dw, b_dw, ln_g, ln_b, w_pw, b_pw):
    z = a * jax.nn.sigmoid(b)
    z = causal_dwconv(z, w_dw) + b_dw
    z = jax.nn.silu(layer_norm(z, ln_g, ln_b))
    return jnp.einsum("bsc,cd->bsd", z, w_pw) + b_pw


def even_layer(x, mem, pre_g, w_in, a_ln_g, a_ln_b, a_ws, a_bs, b_conv, mem_g, w_kv, w_out, post_g):
    h = rms_norm(x, pre_g)
    p = jnp.einsum("bsd,de->bse", h, w_in)
    u, v, bg, cg, xin, q, gate = _split(p, EVEN_SPLITS)
    ya = spatial_gating(u, v, a_ln_g, a_ln_b, a_ws, a_bs)
    yb = short_gated_conv(bg, cg, xin, b_conv)
    yx = memory_cross_attention(q, rms_norm(mem, mem_g), w_kv)
    y = jnp.concatenate([ya, yb, yx], axis=-1) * jax.nn.silu(gate)
    return x + rms_norm(jnp.einsum("bse,ed->bsd", y, w_out), post_g)


def odd_layer(x, mem, pre_g, w_in, c_wgrp, c_scale, d_dw_w, d_dw_b, d_ln_g, d_ln_b, d_pw_w, d_pw_b,
              mem_g, w_kv, w_out, post_g):
    h = rms_norm(x, pre_g)
    p = jnp.einsum("bsd,de->bse", h, w_in)
    zc, ga, gb, q, gate = _split(p, ODD_SPLITS)
    yc = multiscale_pool(zc, c_wgrp, c_scale)
    yd = conformer_conv(ga, gb, d_dw_w, d_dw_b, d_ln_g, d_ln_b, d_pw_w, d_pw_b)
    yx = memory_cross_attention(q, rms_norm(mem, mem_g), w_kv)
    y = jnp.concatenate([yc, yd, yx], axis=-1) * jax.nn.silu(gate)
    return x + rms_norm(jnp.einsum("bse,ed->bsd", y, w_out), post_g)


def _fwd_setup_inputs(seed: int = 0) -> dict:
    key = jax.random.key(seed)
    ks = iter(jax.random.split(key, 40))
    f32 = jnp.float32

    def nrm(shape, scale):
        return jax.random.normal(next(ks), shape, f32) * scale

    def gain(shape):
        return 1.0 + 0.05 * jax.random.normal(next(ks), shape, f32)

    ne, no = N_EVEN, N_ODD
    bw = BRANCH_WIDTH
    return {
        "x": jax.random.normal(next(ks), (BATCH, SEQ, D_MODEL), f32),
        "mem": jax.random.normal(next(ks), (BATCH, N_MEM, D_MODEL), f32),
        "even_pre_g": gain((ne, D_MODEL)),
        "even_w_in": nrm((ne, D_MODEL, EVEN_IN), D_MODEL ** -0.5),
        "even_a_ln_g": gain((ne, bw)),
        "even_a_ln_b": nrm((ne, bw), 0.02),
        "even_a_ws": nrm((ne, A_HEADS, CHUNK, CHUNK), CHUNK ** -0.5),
        "even_a_bs": nrm((ne, A_HEADS, CHUNK), 0.02),
        "even_b_conv": nrm((ne, SHORT_CONV, bw), SHORT_CONV ** -0.5),
        "even_mem_g": gain((ne, D_MODEL)),
        "even_w_kv": nrm((ne, D_MODEL, 2 * XA_WIDTH), D_MODEL ** -0.5),
        "even_w_out": nrm((ne, MIX_WIDTH, D_MODEL), MIX_WIDTH ** -0.5),
        "even_post_g": gain((ne, D_MODEL)),
        "odd_pre_g": gain((no, D_MODEL)),
        "odd_w_in": nrm((no, D_MODEL, ODD_IN), D_MODEL ** -0.5),
        "odd_c_wgrp": nrm((no, len(POOL_WINDOWS), C_GROUP, C_GROUP), C_GROUP ** -0.5),
        "odd_c_scale": gain((no, bw)),
        "odd_d_dw_w": nrm((no, CONF_CONV, bw), CONF_CONV ** -0.5),
        "odd_d_dw_b": nrm((no, bw), 0.02),
        "odd_d_ln_g": gain((no, bw)),
        "odd_d_ln_b": nrm((no, bw), 0.02),
        "odd_d_pw_w": nrm((no, bw, bw), bw ** -0.5),
        "odd_d_pw_b": nrm((no, bw), 0.02),
        "odd_mem_g": gain((no, D_MODEL)),
        "odd_w_kv": nrm((no, D_MODEL, 2 * XA_WIDTH), D_MODEL ** -0.5),
        "odd_w_out": nrm((no, MIX_WIDTH, D_MODEL), MIX_WIDTH ** -0.5),
        "odd_post_g": gain((no, D_MODEL)),
    }


def _fwd_reference(x, mem,
              even_pre_g, even_w_in, even_a_ln_g, even_a_ln_b, even_a_ws, even_a_bs, even_b_conv,
              even_mem_g, even_w_kv, even_w_out, even_post_g,
              odd_pre_g, odd_w_in, odd_c_wgrp, odd_c_scale, odd_d_dw_w, odd_d_dw_b, odd_d_ln_g,
              odd_d_ln_b, odd_d_pw_w, odd_d_pw_b, odd_mem_g, odd_w_kv, odd_w_out, odd_post_g):
    for layer in range(DEPTH):
        i = layer // 2
        if layer % 2 == 0:
            x = even_layer(x, mem, even_pre_g[i], even_w_in[i], even_a_ln_g[i], even_a_ln_b[i],
                           even_a_ws[i], even_a_bs[i], even_b_conv[i], even_mem_g[i], even_w_kv[i],
                           even_w_out[i], even_post_g[i])
        else:
            x = odd_layer(x, mem, odd_pre_g[i], odd_w_in[i], odd_c_wgrp[i], odd_c_scale[i],
                          odd_d_dw_w[i], odd_d_dw_b[i], odd_d_ln_g[i], odd_d_ln_b[i], odd_d_pw_w[i],
                          odd_d_pw_b[i], odd_mem_g[i], odd_w_kv[i], odd_w_out[i], odd_post_g[i])
    return x


import jax as _jax
import jax.numpy as _jnp

TWIN_FORMAT = 'train_step'
FWD_PARAMS = ['x', 'mem', 'even_pre_g', 'even_w_in', 'even_a_ln_g', 'even_a_ln_b', 'even_a_ws', 'even_a_bs', 'even_b_conv', 'even_mem_g', 'even_w_kv', 'even_w_out', 'even_post_g', 'odd_pre_g', 'odd_w_in', 'odd_c_wgrp', 'odd_c_scale', 'odd_d_dw_w', 'odd_d_dw_b', 'odd_d_ln_g', 'odd_d_ln_b', 'odd_d_pw_w', 'odd_d_pw_b', 'odd_mem_g', 'odd_w_kv', 'odd_w_out', 'odd_post_g']
TWIN_WEIGHTS = ['even_pre_g', 'even_w_in', 'even_a_ln_g', 'even_a_ln_b', 'even_a_ws', 'even_a_bs', 'even_b_conv', 'even_mem_g', 'even_w_kv', 'even_w_out', 'even_post_g', 'odd_pre_g', 'odd_w_in', 'odd_c_wgrp', 'odd_c_scale', 'odd_d_dw_w', 'odd_d_dw_b', 'odd_d_ln_g', 'odd_d_ln_b', 'odd_d_pw_w', 'odd_d_pw_b', 'odd_mem_g', 'odd_w_kv', 'odd_w_out', 'odd_post_g']
TWIN_DIFF_INPUT = 'x'
TWIN_INPUTS = ['x', 'mem', 'even_pre_g', 'even_w_in', 'even_a_ln_g', 'even_a_ln_b', 'even_a_ws', 'even_a_bs', 'even_b_conv', 'even_mem_g', 'even_w_kv', 'even_w_out', 'even_post_g', 'odd_pre_g', 'odd_w_in', 'odd_c_wgrp', 'odd_c_scale', 'odd_d_dw_w', 'odd_d_dw_b', 'odd_d_ln_g', 'odd_d_ln_b', 'odd_d_pw_w', 'odd_d_pw_b', 'odd_mem_g', 'odd_w_kv', 'odd_w_out', 'odd_post_g', 'loss_target', 'm_even_pre_g', 'm_even_w_in', 'm_even_a_ln_g', 'm_even_a_ln_b', 'm_even_a_ws', 'm_even_a_bs', 'm_even_b_conv', 'm_even_mem_g', 'm_even_w_kv', 'm_even_w_out', 'm_even_post_g', 'm_odd_pre_g', 'm_odd_w_in', 'm_odd_c_wgrp', 'm_odd_c_scale', 'm_odd_d_dw_w', 'm_odd_d_dw_b', 'm_odd_d_ln_g', 'm_odd_d_ln_b', 'm_odd_d_pw_w', 'm_odd_d_pw_b', 'm_odd_mem_g', 'm_odd_w_kv', 'm_odd_w_out', 'm_odd_post_g', 'v_even_pre_g', 'v_even_w_in', 'v_even_a_ln_g', 'v_even_a_ln_b', 'v_even_a_ws', 'v_even_a_bs', 'v_even_b_conv', 'v_even_mem_g', 'v_even_w_kv', 'v_even_w_out', 'v_even_post_g', 'v_odd_pre_g', 'v_odd_w_in', 'v_odd_c_wgrp', 'v_odd_c_scale', 'v_odd_d_dw_w', 'v_odd_d_dw_b', 'v_odd_d_ln_g', 'v_odd_d_ln_b', 'v_odd_d_pw_w', 'v_odd_d_pw_b', 'v_odd_mem_g', 'v_odd_w_kv', 'v_odd_w_out', 'v_odd_post_g']
TWIN_OUTPUTS = ['loss', 'grad_x', 'grad_even_pre_g', 'grad_even_w_in', 'grad_even_a_ln_g', 'grad_even_a_ln_b', 'grad_even_a_ws', 'grad_even_a_bs', 'grad_even_b_conv', 'grad_even_mem_g', 'grad_even_w_kv', 'grad_even_w_out', 'grad_even_post_g', 'grad_odd_pre_g', 'grad_odd_w_in', 'grad_odd_c_wgrp', 'grad_odd_c_scale', 'grad_odd_d_dw_w', 'grad_odd_d_dw_b', 'grad_odd_d_ln_g', 'grad_odd_d_ln_b', 'grad_odd_d_pw_w', 'grad_odd_d_pw_b', 'grad_odd_mem_g', 'grad_odd_w_kv', 'grad_odd_w_out', 'grad_odd_post_g', 'delta_even_pre_g', 'delta_even_w_in', 'delta_even_a_ln_g', 'delta_even_a_ln_b', 'delta_even_a_ws', 'delta_even_a_bs', 'delta_even_b_conv', 'delta_even_mem_g', 'delta_even_w_kv', 'delta_even_w_out', 'delta_even_post_g', 'delta_odd_pre_g', 'delta_odd_w_in', 'delta_odd_c_wgrp', 'delta_odd_c_scale', 'delta_odd_d_dw_w', 'delta_odd_d_dw_b', 'delta_odd_d_ln_g', 'delta_odd_d_ln_b', 'delta_odd_d_pw_w', 'delta_odd_d_pw_b', 'delta_odd_mem_g', 'delta_odd_w_kv', 'delta_odd_w_out', 'delta_odd_post_g', 'new_m_even_pre_g', 'new_m_even_w_in', 'new_m_even_a_ln_g', 'new_m_even_a_ln_b', 'new_m_even_a_ws', 'new_m_even_a_bs', 'new_m_even_b_conv', 'new_m_even_mem_g', 'new_m_even_w_kv', 'new_m_even_w_out', 'new_m_even_post_g', 'new_m_odd_pre_g', 'new_m_odd_w_in', 'new_m_odd_c_wgrp', 'new_m_odd_c_scale', 'new_m_odd_d_dw_w', 'new_m_odd_d_dw_b', 'new_m_odd_d_ln_g', 'new_m_odd_d_ln_b', 'new_m_odd_d_pw_w', 'new_m_odd_d_pw_b', 'new_m_odd_mem_g', 'new_m_odd_w_kv', 'new_m_odd_w_out', 'new_m_odd_post_g', 'new_v_even_pre_g', 'new_v_even_w_in', 'new_v_even_a_ln_g', 'new_v_even_a_ln_b', 'new_v_even_a_ws', 'new_v_even_a_bs', 'new_v_even_b_conv', 'new_v_even_mem_g', 'new_v_even_w_kv', 'new_v_even_w_out', 'new_v_even_post_g', 'new_v_odd_pre_g', 'new_v_odd_w_in', 'new_v_odd_c_wgrp', 'new_v_odd_c_scale', 'new_v_odd_d_dw_w', 'new_v_odd_d_dw_b', 'new_v_odd_d_ln_g', 'new_v_odd_d_ln_b', 'new_v_odd_d_pw_w', 'new_v_odd_d_pw_b', 'new_v_odd_mem_g', 'new_v_odd_w_kv', 'new_v_odd_w_out', 'new_v_odd_post_g']
TWIN_LEAF_KINDS = {'loss': 'loss', 'grad_x': 'grad_x', 'grad_even_pre_g': 'grad_w', 'grad_even_w_in': 'grad_w', 'grad_even_a_ln_g': 'grad_w', 'grad_even_a_ln_b': 'grad_w', 'grad_even_a_ws': 'grad_w', 'grad_even_a_bs': 'grad_w', 'grad_even_b_conv': 'grad_w', 'grad_even_mem_g': 'grad_w', 'grad_even_w_kv': 'grad_w', 'grad_even_w_out': 'grad_w', 'grad_even_post_g': 'grad_w', 'grad_odd_pre_g': 'grad_w', 'grad_odd_w_in': 'grad_w', 'grad_odd_c_wgrp': 'grad_w', 'grad_odd_c_scale': 'grad_w', 'grad_odd_d_dw_w': 'grad_w', 'grad_odd_d_dw_b': 'grad_w', 'grad_odd_d_ln_g': 'grad_w', 'grad_odd_d_ln_b': 'grad_w', 'grad_odd_d_pw_w': 'grad_w', 'grad_odd_d_pw_b': 'grad_w', 'grad_odd_mem_g': 'grad_w', 'grad_odd_w_kv': 'grad_w', 'grad_odd_w_out': 'grad_w', 'grad_odd_post_g': 'grad_w', 'delta_even_pre_g': 'delta_w', 'delta_even_w_in': 'delta_w', 'delta_even_a_ln_g': 'delta_w', 'delta_even_a_ln_b': 'delta_w', 'delta_even_a_ws': 'delta_w', 'delta_even_a_bs': 'delta_w', 'delta_even_b_conv': 'delta_w', 'delta_even_mem_g': 'delta_w', 'delta_even_w_kv': 'delta_w', 'delta_even_w_out': 'delta_w', 'delta_even_post_g': 'delta_w', 'delta_odd_pre_g': 'delta_w', 'delta_odd_w_in': 'delta_w', 'delta_odd_c_wgrp': 'delta_w', 'delta_odd_c_scale': 'delta_w', 'delta_odd_d_dw_w': 'delta_w', 'delta_odd_d_dw_b': 'delta_w', 'delta_odd_d_ln_g': 'delta_w', 'delta_odd_d_ln_b': 'delta_w', 'delta_odd_d_pw_w': 'delta_w', 'delta_odd_d_pw_b': 'delta_w', 'delta_odd_mem_g': 'delta_w', 'delta_odd_w_kv': 'delta_w', 'delta_odd_w_out': 'delta_w', 'delta_odd_post_g': 'delta_w', 'new_m_even_pre_g': 'new_m', 'new_m_even_w_in': 'new_m', 'new_m_even_a_ln_g': 'new_m', 'new_m_even_a_ln_b': 'new_m', 'new_m_even_a_ws': 'new_m', 'new_m_even_a_bs': 'new_m', 'new_m_even_b_conv': 'new_m', 'new_m_even_mem_g': 'new_m', 'new_m_even_w_kv': 'new_m', 'new_m_even_w_out': 'new_m', 'new_m_even_post_g': 'new_m', 'new_m_odd_pre_g': 'new_m', 'new_m_odd_w_in': 'new_m', 'new_m_odd_c_wgrp': 'new_m', 'new_m_odd_c_scale': 'new_m', 'new_m_odd_d_dw_w': 'new_m', 'new_m_odd_d_dw_b': 'new_m', 'new_m_odd_d_ln_g': 'new_m', 'new_m_odd_d_ln_b': 'new_m', 'new_m_odd_d_pw_w': 'new_m', 'new_m_odd_d_pw_b': 'new_m', 'new_m_odd_mem_g': 'new_m', 'new_m_odd_w_kv': 'new_m', 'new_m_odd_w_out': 'new_m', 'new_m_odd_post_g': 'new_m', 'new_v_even_pre_g': 'new_v', 'new_v_even_w_in': 'new_v', 'new_v_even_a_ln_g': 'new_v', 'new_v_even_a_ln_b': 'new_v', 'new_v_even_a_ws': 'new_v', 'new_v_even_a_bs': 'new_v', 'new_v_even_b_conv': 'new_v', 'new_v_even_mem_g': 'new_v', 'new_v_even_w_kv': 'new_v', 'new_v_even_w_out': 'new_v', 'new_v_even_post_g': 'new_v', 'new_v_odd_pre_g': 'new_v', 'new_v_odd_w_in': 'new_v', 'new_v_odd_c_wgrp': 'new_v', 'new_v_odd_c_scale': 'new_v', 'new_v_odd_d_dw_w': 'new_v', 'new_v_odd_d_dw_b': 'new_v', 'new_v_odd_d_ln_g': 'new_v', 'new_v_odd_d_ln_b': 'new_v', 'new_v_odd_d_pw_w': 'new_v', 'new_v_odd_d_pw_b': 'new_v', 'new_v_odd_mem_g': 'new_v', 'new_v_odd_w_kv': 'new_v', 'new_v_odd_w_out': 'new_v', 'new_v_odd_post_g': 'new_v'}


def _forward(args):
    return _fwd_reference(*[args[k] for k in FWD_PARAMS])


def _output_shape():
    out = _jax.eval_shape(lambda: _forward(_fwd_setup_inputs(0)))
    return out.shape, out.dtype

N_MICROBATCH = 1
ADAM_LR = 0.001
ADAM_B1 = 0.9
ADAM_B2 = 0.999
ADAM_EPS = 1e-08
ADAM_WD = 0.01
ADAM_STEP = 10
PER_EXAMPLE_BATCH_AXIS = {'x': 0, 'mem': 0, 'loss_target': 0}
SHARED_INPUTS = []
_WEIGHT_DTYPES = {'even_pre_g': _jnp.float32, 'even_w_in': _jnp.float32, 'even_a_ln_g': _jnp.float32, 'even_a_ln_b': _jnp.float32, 'even_a_ws': _jnp.float32, 'even_a_bs': _jnp.float32, 'even_b_conv': _jnp.float32, 'even_mem_g': _jnp.float32, 'even_w_kv': _jnp.float32, 'even_w_out': _jnp.float32, 'even_post_g': _jnp.float32, 'odd_pre_g': _jnp.float32, 'odd_w_in': _jnp.float32, 'odd_c_wgrp': _jnp.float32, 'odd_c_scale': _jnp.float32, 'odd_d_dw_w': _jnp.float32, 'odd_d_dw_b': _jnp.float32, 'odd_d_ln_g': _jnp.float32, 'odd_d_ln_b': _jnp.float32, 'odd_d_pw_w': _jnp.float32, 'odd_d_pw_b': _jnp.float32, 'odd_mem_g': _jnp.float32, 'odd_w_kv': _jnp.float32, 'odd_w_out': _jnp.float32, 'odd_post_g': _jnp.float32}
MOMENT_SCALE = {'even_pre_g': 9.190626e-01, 'even_w_in': 3.491501e-01, 'even_a_ln_g': 3.316886e-01, 'even_a_ln_b': 3.366513e-01, 'even_a_ws': 3.649468e-01, 'even_a_bs': 5.797506e-01, 'even_b_conv': 4.618760e-01, 'even_mem_g': 5.201518e-02, 'even_w_kv': 5.040677e-02, 'even_w_out': 4.765826e-01, 'even_post_g': 3.205809e+01, 'odd_pre_g': 4.691101e-01, 'odd_w_in': 2.401324e-01, 'odd_c_wgrp': 3.922344e-01, 'odd_c_scale': 4.047247e-01, 'odd_d_dw_w': 2.140784e-01, 'odd_d_dw_b': 8.242782e-01, 'odd_d_ln_g': 3.784543e-01, 'odd_d_ln_b': 5.325035e-01, 'odd_d_pw_w': 2.598236e-01, 'odd_d_pw_b': 1.006265e+00, 'odd_mem_g': 4.429641e-02, 'odd_w_kv': 4.071958e-02, 'odd_w_out': 4.181613e-01, 'odd_post_g': 3.194195e+01}


def _to_microbatches(a, axis):
    t = _jnp.moveaxis(a, axis, 0)
    t = t.reshape((N_MICROBATCH, t.shape[0] // N_MICROBATCH) + t.shape[1:])
    return _jnp.moveaxis(t, 1, axis + 1)


def setup_inputs(seed: int = 0) -> dict:
    inp = _fwd_setup_inputs(seed)
    key = _jax.random.fold_in(_jax.random.key(seed), 7919)
    shape, _ = _output_shape()
    out = dict(inp)
    out["loss_target"] = _jax.random.normal(_jax.random.fold_in(key, 0), shape, _jnp.float32)
    for i, name in enumerate(TWIN_WEIGHTS):
        w = inp[name].astype(_jnp.float32)
        if MOMENT_SCALE is None:
            s = _jnp.sqrt(_jnp.mean(_jnp.square(w)) + 1e-30)
        else:
            s = MOMENT_SCALE[name]
        km, kv = _jax.random.split(_jax.random.fold_in(key, i + 1))
        out[name] = w
        out["m_" + name] = s * _jax.random.normal(km, w.shape, _jnp.float32)
        out["v_" + name] = (s * s) * _jax.random.uniform(kv, w.shape, _jnp.float32, 0.5, 1.5)
    if N_MICROBATCH > 1:
        for name, axis in PER_EXAMPLE_BATCH_AXIS.items():
            out[name] = _to_microbatches(out[name], axis)
    return {'x': out['x'], 'mem': out['mem'], 'even_pre_g': out['even_pre_g'], 'even_w_in': out['even_w_in'], 'even_a_ln_g': out['even_a_ln_g'], 'even_a_ln_b': out['even_a_ln_b'], 'even_a_ws': out['even_a_ws'], 'even_a_bs': out['even_a_bs'], 'even_b_conv': out['even_b_conv'], 'even_mem_g': out['even_mem_g'], 'even_w_kv': out['even_w_kv'], 'even_w_out': out['even_w_out'], 'even_post_g': out['even_post_g'], 'odd_pre_g': out['odd_pre_g'], 'odd_w_in': out['odd_w_in'], 'odd_c_wgrp': out['odd_c_wgrp'], 'odd_c_scale': out['odd_c_scale'], 'odd_d_dw_w': out['odd_d_dw_w'], 'odd_d_dw_b': out['odd_d_dw_b'], 'odd_d_ln_g': out['odd_d_ln_g'], 'odd_d_ln_b': out['odd_d_ln_b'], 'odd_d_pw_w': out['odd_d_pw_w'], 'odd_d_pw_b': out['odd_d_pw_b'], 'odd_mem_g': out['odd_mem_g'], 'odd_w_kv': out['odd_w_kv'], 'odd_w_out': out['odd_w_out'], 'odd_post_g': out['odd_post_g'], 'loss_target': out['loss_target'], 'm_even_pre_g': out['m_even_pre_g'], 'm_even_w_in': out['m_even_w_in'], 'm_even_a_ln_g': out['m_even_a_ln_g'], 'm_even_a_ln_b': out['m_even_a_ln_b'], 'm_even_a_ws': out['m_even_a_ws'], 'm_even_a_bs': out['m_even_a_bs'], 'm_even_b_conv': out['m_even_b_conv'], 'm_even_mem_g': out['m_even_mem_g'], 'm_even_w_kv': out['m_even_w_kv'], 'm_even_w_out': out['m_even_w_out'], 'm_even_post_g': out['m_even_post_g'], 'm_odd_pre_g': out['m_odd_pre_g'], 'm_odd_w_in': out['m_odd_w_in'], 'm_odd_c_wgrp': out['m_odd_c_wgrp'], 'm_odd_c_scale': out['m_odd_c_scale'], 'm_odd_d_dw_w': out['m_odd_d_dw_w'], 'm_odd_d_dw_b': out['m_odd_d_dw_b'], 'm_odd_d_ln_g': out['m_odd_d_ln_g'], 'm_odd_d_ln_b': out['m_odd_d_ln_b'], 'm_odd_d_pw_w': out['m_odd_d_pw_w'], 'm_odd_d_pw_b': out['m_odd_d_pw_b'], 'm_odd_mem_g': out['m_odd_mem_g'], 'm_odd_w_kv': out['m_odd_w_kv'], 'm_odd_w_out': out['m_odd_w_out'], 'm_odd_post_g': out['m_odd_post_g'], 'v_even_pre_g': out['v_even_pre_g'], 'v_even_w_in': out['v_even_w_in'], 'v_even_a_ln_g': out['v_even_a_ln_g'], 'v_even_a_ln_b': out['v_even_a_ln_b'], 'v_even_a_ws': out['v_even_a_ws'], 'v_even_a_bs': out['v_even_a_bs'], 'v_even_b_conv': out['v_even_b_conv'], 'v_even_mem_g': out['v_even_mem_g'], 'v_even_w_kv': out['v_even_w_kv'], 'v_even_w_out': out['v_even_w_out'], 'v_even_post_g': out['v_even_post_g'], 'v_odd_pre_g': out['v_odd_pre_g'], 'v_odd_w_in': out['v_odd_w_in'], 'v_odd_c_wgrp': out['v_odd_c_wgrp'], 'v_odd_c_scale': out['v_odd_c_scale'], 'v_odd_d_dw_w': out['v_odd_d_dw_w'], 'v_odd_d_dw_b': out['v_odd_d_dw_b'], 'v_odd_d_ln_g': out['v_odd_d_ln_g'], 'v_odd_d_ln_b': out['v_odd_d_ln_b'], 'v_odd_d_pw_w': out['v_odd_d_pw_w'], 'v_odd_d_pw_b': out['v_odd_d_pw_b'], 'v_odd_mem_g': out['v_odd_mem_g'], 'v_odd_w_kv': out['v_odd_w_kv'], 'v_odd_w_out': out['v_odd_w_out'], 'v_odd_post_g': out['v_odd_post_g']}


def _loss(weights, diff, rest, loss_target):
    with _jax.named_scope("forward"):
        args = {**rest, TWIN_DIFF_INPUT: diff, **{k: w.astype(_WEIGHT_DTYPES[k]) for k, w in weights.items()}}
        y = _forward(args)
    with _jax.named_scope("loss_head"):
        err = _jnp.square(y.astype(_jnp.float32) - loss_target)
        return 0.5 * _jnp.sum(_jnp.mean(err, axis=-1)) if err.ndim else 0.5 * err


def _adamw(w, g, m, v):
    m = ADAM_B1 * m + (1.0 - ADAM_B1) * g
    v = ADAM_B2 * v + (1.0 - ADAM_B2) * _jnp.square(g)
    m_hat = m / (1.0 - ADAM_B1 ** ADAM_STEP)
    v_hat = v / (1.0 - ADAM_B2 ** ADAM_STEP)
    delta = -ADAM_LR * (m_hat / (_jnp.sqrt(v_hat) + ADAM_EPS) + ADAM_WD * w)
    return delta, m, v


def reference(x, mem, even_pre_g, even_w_in, even_a_ln_g, even_a_ln_b, even_a_ws, even_a_bs, even_b_conv, even_mem_g, even_w_kv, even_w_out, even_post_g, odd_pre_g, odd_w_in, odd_c_wgrp, odd_c_scale, odd_d_dw_w, odd_d_dw_b, odd_d_ln_g, odd_d_ln_b, odd_d_pw_w, odd_d_pw_b, odd_mem_g, odd_w_kv, odd_w_out, odd_post_g, loss_target, m_even_pre_g, m_even_w_in, m_even_a_ln_g, m_even_a_ln_b, m_even_a_ws, m_even_a_bs, m_even_b_conv, m_even_mem_g, m_even_w_kv, m_even_w_out, m_even_post_g, m_odd_pre_g, m_odd_w_in, m_odd_c_wgrp, m_odd_c_scale, m_odd_d_dw_w, m_odd_d_dw_b, m_odd_d_ln_g, m_odd_d_ln_b, m_odd_d_pw_w, m_odd_d_pw_b, m_odd_mem_g, m_odd_w_kv, m_odd_w_out, m_odd_post_g, v_even_pre_g, v_even_w_in, v_even_a_ln_g, v_even_a_ln_b, v_even_a_ws, v_even_a_bs, v_even_b_conv, v_even_mem_g, v_even_w_kv, v_even_w_out, v_even_post_g, v_odd_pre_g, v_odd_w_in, v_odd_c_wgrp, v_odd_c_scale, v_odd_d_dw_w, v_odd_d_dw_b, v_odd_d_ln_g, v_odd_d_ln_b, v_odd_d_pw_w, v_odd_d_pw_b, v_odd_mem_g, v_odd_w_kv, v_odd_w_out, v_odd_post_g):
    given = dict(x=x, mem=mem, even_pre_g=even_pre_g, even_w_in=even_w_in, even_a_ln_g=even_a_ln_g, even_a_ln_b=even_a_ln_b, even_a_ws=even_a_ws, even_a_bs=even_a_bs, even_b_conv=even_b_conv, even_mem_g=even_mem_g, even_w_kv=even_w_kv, even_w_out=even_w_out, even_post_g=even_post_g, odd_pre_g=odd_pre_g, odd_w_in=odd_w_in, odd_c_wgrp=odd_c_wgrp, odd_c_scale=odd_c_scale, odd_d_dw_w=odd_d_dw_w, odd_d_dw_b=odd_d_dw_b, odd_d_ln_g=odd_d_ln_g, odd_d_ln_b=odd_d_ln_b, odd_d_pw_w=odd_d_pw_w, odd_d_pw_b=odd_d_pw_b, odd_mem_g=odd_mem_g, odd_w_kv=odd_w_kv, odd_w_out=odd_w_out, odd_post_g=odd_post_g, loss_target=loss_target, m_even_pre_g=m_even_pre_g, m_even_w_in=m_even_w_in, m_even_a_ln_g=m_even_a_ln_g, m_even_a_ln_b=m_even_a_ln_b, m_even_a_ws=m_even_a_ws, m_even_a_bs=m_even_a_bs, m_even_b_conv=m_even_b_conv, m_even_mem_g=m_even_mem_g, m_even_w_kv=m_even_w_kv, m_even_w_out=m_even_w_out, m_even_post_g=m_even_post_g, m_odd_pre_g=m_odd_pre_g, m_odd_w_in=m_odd_w_in, m_odd_c_wgrp=m_odd_c_wgrp, m_odd_c_scale=m_odd_c_scale, m_odd_d_dw_w=m_odd_d_dw_w, m_odd_d_dw_b=m_odd_d_dw_b, m_odd_d_ln_g=m_odd_d_ln_g, m_odd_d_ln_b=m_odd_d_ln_b, m_odd_d_pw_w=m_odd_d_pw_w, m_odd_d_pw_b=m_odd_d_pw_b, m_odd_mem_g=m_odd_mem_g, m_odd_w_kv=m_odd_w_kv, m_odd_w_out=m_odd_w_out, m_odd_post_g=m_odd_post_g, v_even_pre_g=v_even_pre_g, v_even_w_in=v_even_w_in, v_even_a_ln_g=v_even_a_ln_g, v_even_a_ln_b=v_even_a_ln_b, v_even_a_ws=v_even_a_ws, v_even_a_bs=v_even_a_bs, v_even_b_conv=v_even_b_conv, v_even_mem_g=v_even_mem_g, v_even_w_kv=v_even_w_kv, v_even_w_out=v_even_w_out, v_even_post_g=v_even_post_g, v_odd_pre_g=v_odd_pre_g, v_odd_w_in=v_odd_w_in, v_odd_c_wgrp=v_odd_c_wgrp, v_odd_c_scale=v_odd_c_scale, v_odd_d_dw_w=v_odd_d_dw_w, v_odd_d_dw_b=v_odd_d_dw_b, v_odd_d_ln_g=v_odd_d_ln_g, v_odd_d_ln_b=v_odd_d_ln_b, v_odd_d_pw_w=v_odd_d_pw_w, v_odd_d_pw_b=v_odd_d_pw_b, v_odd_mem_g=v_odd_mem_g, v_odd_w_kv=v_odd_w_kv, v_odd_w_out=v_odd_w_out, v_odd_post_g=v_odd_post_g)
    weights = {n: given[n] for n in TWIN_WEIGHTS}
    shared = {n: given[n] for n in SHARED_INPUTS}
    per_example = {n: given[n] for n in ['x', 'mem']}
    grad_fn = _jax.value_and_grad(_loss, argnums=(0, 1))

    def one_microbatch(ex, loss_target):
        ex = dict(ex)
        diff = ex.pop(TWIN_DIFF_INPUT)
        return grad_fn(weights, diff, {**shared, **ex}, loss_target)

    if N_MICROBATCH == 1:
        loss, (grad_w, grad_x) = one_microbatch(per_example, given["loss_target"])
    else:
        def body(carry, xs):
            loss_sum, grad_sum = carry
            l_k, (gw_k, gx_k) = one_microbatch(xs[0], xs[1])
            with _jax.named_scope("update"):
                return (loss_sum + l_k, _jax.tree.map(_jnp.add, grad_sum, gw_k)), gx_k

        init = (_jnp.zeros((), _jnp.float32), _jax.tree.map(_jnp.zeros_like, weights))
        (loss, grad_w), grad_x = _jax.lax.scan(body, init, (per_example, given["loss_target"]))
    with _jax.named_scope("update"):
        delta_w, new_m, new_v = {}, {}, {}
        for n in TWIN_WEIGHTS:
            delta_w[n], new_m[n], new_v[n] = _adamw(weights[n], grad_w[n], given["m_" + n], given["v_" + n])
    return (loss, grad_x, *[grad_w[n] for n in TWIN_WEIGHTS], *[delta_w[n] for n in TWIN_WEIGHTS],
            *[new_m[n] for n in TWIN_WEIGHTS], *[new_v[n] for n in TWIN_WEIGHTS])
```

```python
import functools

import jax
import jax.numpy as jnp
from jax import lax
from jax.experimental import pallas as pl
from jax.experimental.pallas import tpu as pltpu

F32 = jnp.float32
BF16 = jnp.bfloat16
S = jax.ShapeDtypeStruct
MESH = pl.DeviceIdType.MESH
AXES = ("x", "y", "c")
N_DEV = 8

D = 1024
BW = 768
XA = 512
HD = 128
NH = 4
MIX = 2048
CHUNK = 128
GRP = 192
N_MEM = 256
CONF_K = 31
EPS = 1e-6
HALO = 32
POOL_WINDOWS = (2, 4, 8, 16)
TM_FWD = 256
TM_BWD = 128

E_U, E_V, E_BG, E_CG, E_XIN, E_Q, E_GATE = 0, 768, 1536, 2304, 3072, 3840, 4352
EVEN_IN = 6400
O_ZC, O_GA, O_GB, O_Q, O_GATE = 0, 768, 1536, 2304, 2816
ODD_IN = 4864

ADAM_LR, ADAM_B1, ADAM_B2, ADAM_EPS, ADAM_WD, ADAM_STEP = 0.001, 0.9, 0.999, 1e-08, 0.01, 10

VMEM_LIMIT_V7X = 56 * 1024 * 1024


def _params(sem=None):
    return pltpu.CompilerParams(dimension_semantics=sem, vmem_limit_bytes=VMEM_LIMIT_V7X)


def _dot(a, b):
    return jnp.dot(a, b, preferred_element_type=F32)


def _dot_nt(a, b):
    return lax.dot_general(a, b, (((1,), (1,)), ((), ())), preferred_element_type=F32)


def _dot_tn(a, b):
    return lax.dot_general(a, b, (((0,), (0,)), ((), ())), preferred_element_type=F32)


def _sigmoid(z):
    return 1.0 / (1.0 + jnp.exp(-z))


def _rowmean(a):
    return jnp.mean(a, axis=-1, keepdims=True)


def _colsum(a):
    return jnp.sum(a, axis=0, keepdims=True)


def _ln_stats(v):
    mu = _rowmean(v)
    vc = v - mu
    rs = lax.rsqrt(_rowmean(vc * vc) + EPS)
    return vc * rs, rs


def _ln_bwd(dn, vh, rs, g):
    dvh = dn * g
    return rs * (dvh - _rowmean(dvh) - vh * _rowmean(dvh * vh))


def _group_masks():
    col = lax.broadcasted_iota(jnp.int32, (1, BW), 1)
    return [((col >= GRP * h) & (col < GRP * (h + 1))).astype(F32) for h in range(NH)]


def _full(shape):
    nd = len(shape)
    return pl.BlockSpec(shape, lambda *_: (0,) * nd)


def _whole():
    return pl.BlockSpec(memory_space=pltpu.VMEM)


def _rms_matmul(x, g, w, *, tm, tn, name):
    t, d = x.shape
    n = w.shape[1]

    def body(x_ref, g_ref, w_ref, p_ref, h_ref):
        @pl.when(pl.program_id(1) == 0)
        def _():
            xv = x_ref[...]
            r = lax.rsqrt(_rowmean(xv * xv) + EPS)
            h_ref[...] = (xv * r * g_ref[...]).astype(BF16)

        p_ref[...] = _dot(h_ref[...], w_ref[...])

    return pl.pallas_call(
        body, name=name, grid=(t // tm, n // tn),
        in_specs=[pl.BlockSpec((tm, d), lambda i, j: (i, 0)), pl.BlockSpec((1, d), lambda i, j: (0, 0)),
                  pl.BlockSpec((d, tn), lambda i, j: (0, j))],
        out_specs=[pl.BlockSpec((tm, tn), lambda i, j: (i, j)), pl.BlockSpec((tm, d), lambda i, j: (i, 0))],
        out_shape=[S((t, n), F32), S((t, d), BF16)],
        compiler_params=_params(("arbitrary", "arbitrary")),
    )(x, g, w)


def _nt_matmul_rms_bwd(dp, w, x, g, dres, *, tm, name):
    t, n = dp.shape
    d = x.shape[1]

    def body(dp_ref, w_ref, x_ref, g_ref, dres_ref, dx_ref, dg_ref):
        @pl.when(pl.program_id(0) == 0)
        def _():
            dg_ref[...] = jnp.zeros_like(dg_ref)

        dh = _dot_nt(dp_ref[...], w_ref[...])
        xv = x_ref[...]
        r = lax.rsqrt(_rowmean(xv * xv) + EPS)
        xh = xv * r
        dg_ref[...] += _colsum(dh * xh)
        dxh = dh * g_ref[...]
        dx_ref[...] = dres_ref[...] + r * (dxh - xh * _rowmean(dxh * xh))

    return pl.pallas_call(
        body, name=name, grid=(t // tm,),
        in_specs=[pl.BlockSpec((tm, n), lambda i: (i, 0)), _whole(), pl.BlockSpec((tm, d), lambda i: (i, 0)),
                  _whole(), pl.BlockSpec((tm, d), lambda i: (i, 0))],
        out_specs=[pl.BlockSpec((tm, d), lambda i: (i, 0)), pl.BlockSpec((1, d), lambda i: (0, 0))],
        out_shape=[S((t, d), F32), S((1, d), F32)],
        compiler_params=_params(("arbitrary",)),
    )(dp, w, x, g, dres)


def _tn_matmul(a, b, *, tn, tk, out_dtype, name):
    t, m = a.shape
    n = b.shape[1]
    nk = t // tk

    def body(a_ref, b_ref, o_ref, acc_ref):
        k = pl.program_id(1)

        @pl.when(k == 0)
        def _():
            acc_ref[...] = jnp.zeros_like(acc_ref)

        acc_ref[...] += _dot_tn(a_ref[...], b_ref[...])

        @pl.when(k == nk - 1)
        def _():
            o_ref[...] = acc_ref[...].astype(out_dtype)

    return pl.pallas_call(
        body, name=name, grid=(n // tn, nk),
        in_specs=[pl.BlockSpec((tk, m), lambda j, k: (k, 0)), pl.BlockSpec((tk, tn), lambda j, k: (k, j))],
        out_specs=pl.BlockSpec((m, tn), lambda j, k: (0, j)),
        out_shape=S((m, n), out_dtype),
        scratch_shapes=[pltpu.VMEM((m, tn), F32)],
        compiler_params=_params(("arbitrary", "arbitrary")),
    )(a, b)


def _silu_parts(gt):
    sg = _sigmoid(gt)
    return gt * sg, sg * (1.0 + gt * (1.0 - sg))


def _attn_head(q_b, k_b, v_b):
    s = _dot_nt(q_b, k_b) * (HD ** -0.5)
    e = jnp.exp(s - jnp.max(s, axis=-1, keepdims=True))
    prob = e / jnp.sum(e, axis=-1, keepdims=True)
    return prob, _dot(prob.astype(BF16), v_b)


def _rms_residual(x, o, g):
    r = lax.rsqrt(_rowmean(o * o) + EPS)
    return x + o * r * g


def _rms_post_bwd(dres, o, g):
    r = lax.rsqrt(_rowmean(o * o) + EPS)
    oh = o * r
    doh = dres * g
    return r * (doh - oh * _rowmean(doh * oh)), _colsum(dres * oh)


def _sgu_chunk(vn_b, ws_ref, bmap_ref, masks):
    sg = bmap_ref[...]
    for h in range(NH):
        sg = sg + masks[h] * _dot(ws_ref[h], vn_b)
    return sg


def _taps_causal(buf, w_ref, taps, bias):
    rows = buf.shape[0] - HALO
    outs = []
    for rb in range(rows // 32):
        acc = None
        for k in range(taps):
            term = w_ref[k:k + 1, :] * buf[pl.ds(rb * 32 + HALO - (taps - 1 - k), 32), :]
            acc = term if acc is None else acc + term
        outs.append(acc if bias is None else acc + bias)
    return outs


def _taps_anticausal(buf, w_ref, taps):
    rows = buf.shape[0] - HALO
    outs = []
    for rb in range(rows // 32):
        acc = None
        for k in range(taps):
            term = w_ref[k:k + 1, :] * buf[pl.ds(rb * 32 + (taps - 1 - k), 32), :]
            acc = term if acc is None else acc + term
        outs.append(acc)
    return outs


def _fold8(a):
    return a[0:8] + a[8:16] + a[16:24] + a[24:32]


def _tap_grads(d_ref, buf, acc_ref, taps):
    rows = buf.shape[0] - HALO
    for rb in range(rows // 32):
        dv = d_ref[rb * 32:(rb + 1) * 32, :]
        for k in range(taps):
            prod = dv * buf[pl.ds(rb * 32 + HALO - (taps - 1 - k), 32), :]
            acc_ref[k * 8:(k + 1) * 8, :] += _fold8(prod)


def _halo_spec(n, nt, reverse, tm):
    per = tm // HALO
    if reverse:
        return pl.BlockSpec((HALO, n), lambda i: (jnp.maximum((nt - 1 - i) * per - 1, 0), 0))
    return pl.BlockSpec((HALO, n), lambda i: (jnp.maximum(i * per - 1, 0), 0))


def _even_fwd(x, p, lng, lnb, ws, bmap, wc, kv, wout, pg):
    t = x.shape[0]
    tm = min(TM_FWD, t)
    nt = t // tm

    def body(x_ref, p_ref, ph_ref, lng_ref, lnb_ref, ws_ref, bmap_ref, wc_ref, kv_ref, wout_ref, pg_ref,
             o_ref, x1_ref, ybuf, cbuf):
        i = pl.program_id(0)
        masks = _group_masks()
        vh, _ = _ln_stats(p_ref[:, E_V:E_V + BW])
        vn = vh * lng_ref[...] + lnb_ref[...]
        for c in range(tm // CHUNK):
            sl = slice(c * CHUNK, (c + 1) * CHUNK)
            sg = _sgu_chunk(vn[sl].astype(BF16), ws_ref, bmap_ref, masks)
            gate, _ = _silu_parts(p_ref[sl, E_GATE:E_GATE + BW])
            ybuf[sl, 0:BW] = (p_ref[sl, E_U:E_U + BW] * sg * gate).astype(BF16)

        cbuf[0:HALO] = jnp.where(i > 0, ph_ref[:, E_CG:E_CG + BW] * ph_ref[:, E_XIN:E_XIN + BW], 0.0)
        cbuf[HALO:HALO + tm] = p_ref[:, E_CG:E_CG + BW] * p_ref[:, E_XIN:E_XIN + BW]
        conv = _taps_causal(cbuf, wc_ref, 3, None)
        for rb, cv in enumerate(conv):
            sl = slice(rb * 32, (rb + 1) * 32)
            gate, _ = _silu_parts(p_ref[sl, E_GATE + BW:E_GATE + 2 * BW])
            ybuf[sl, BW:2 * BW] = (p_ref[sl, E_BG:E_BG + BW] * cv * gate).astype(BF16)

        for h in range(NH):
            qs = slice(E_Q + h * HD, E_Q + (h + 1) * HD)
            _, yx = _attn_head(p_ref[:, qs].astype(BF16), kv_ref[:, h * HD:(h + 1) * HD],
                               kv_ref[:, XA + h * HD:XA + (h + 1) * HD])
            gs = slice(E_GATE + 2 * BW + h * HD, E_GATE + 2 * BW + (h + 1) * HD)
            gate, _ = _silu_parts(p_ref[:, gs])
            ybuf[:, 2 * BW + h * HD:2 * BW + (h + 1) * HD] = (yx * gate).astype(BF16)

        o = _dot(ybuf[...], wout_ref[...])
        o_ref[...] = o
        x1_ref[...] = _rms_residual(x_ref[...], o, pg_ref[...])

    tile = lambda n: pl.BlockSpec((tm, n), lambda i: (i, 0))
    return pl.pallas_call(
        body, name="even_fwd", grid=(nt,),
        in_specs=[tile(D), tile(EVEN_IN), _halo_spec(EVEN_IN, nt, False, tm)] + [_whole()] * 8,
        out_specs=[tile(D), tile(D)],
        out_shape=[S((t, D), F32), S((t, D), F32)],
        scratch_shapes=[pltpu.VMEM((tm, MIX), BF16), pltpu.VMEM((tm + HALO, BW), F32)],
        compiler_params=_params(("arbitrary",)),
    )(x, p, p, lng, lnb, ws, bmap, wc, kv, wout, pg)


def _even_bwd(dres, o, p, lng, lnb, ws, wst, bmap, wc, kv, wout, pg):
    t = dres.shape[0]
    tm = min(TM_BWD, t)
    nt = t // tm

    def body(dres_ref, o_ref, p_ref, ph_ref, lng_ref, lnb_ref, ws_ref, wst_ref, bmap_ref, wc_ref, kv_ref, wout_ref,
             pg_ref, dp_ref, y_ref, do_ref, dpg_ref, dws_ref, dbs_ref, dlng_ref, dlnb_ref, dwc_ref, dkv_ref,
             dy, cbuf, gbuf, dconv, carry, dvn, dbmap, wacc):
        i = pl.program_id(0)
        ti = nt - 1 - i
        masks = _group_masks()

        @pl.when(i == 0)
        def _():
            for ref in (dpg_ref, dws_ref, dlng_ref, dlnb_ref, dkv_ref, dbmap, wacc):
                ref[...] = jnp.zeros_like(ref)

        do, dpg = _rms_post_bwd(dres_ref[...], o_ref[...], pg_ref[...])
        dpg_ref[...] += dpg
        do_b = do.astype(BF16)
        do_ref[...] = do_b
        dy[...] = _dot_nt(do_b, wout_ref[...])

        vh, rs = _ln_stats(p_ref[:, E_V:E_V + BW])
        vn = vh * lng_ref[...] + lnb_ref[...]
        for c in range(tm // CHUNK):
            sl = slice(c * CHUNK, (c + 1) * CHUNK)
            vn_b = vn[sl].astype(BF16)
            sg = _sgu_chunk(vn_b, ws_ref, bmap_ref, masks)
            u = p_ref[sl, E_U:E_U + BW]
            gate, dgate = _silu_parts(p_ref[sl, E_GATE:E_GATE + BW])
            dyc = dy[sl, 0:BW]
            ya = u * sg
            y_ref[sl, 0:BW] = (ya * gate).astype(BF16)
            dp_ref[sl, E_GATE:E_GATE + BW] = (dyc * ya * dgate).astype(BF16)
            dya = dyc * gate
            dp_ref[sl, E_U:E_U + BW] = (dya * sg).astype(BF16)
            dsg = dya * u
            dbmap[...] += dsg
            dsg_b = dsg.astype(BF16)
            acc = jnp.zeros((CHUNK, BW), F32)
            for h in range(NH):
                dws_ref[h] += _dot_nt((dsg * masks[h]).astype(BF16), vn_b)
                acc = acc + masks[h] * _dot(wst_ref[h], dsg_b)
            dvn[sl, :] = acc
        dn = dvn[...]
        dlng_ref[...] += _colsum(dn * vh)
        dlnb_ref[...] += _colsum(dn)
        dp_ref[:, E_V:E_V + BW] = _ln_bwd(dn, vh, rs, lng_ref[...]).astype(BF16)

        cbuf[0:HALO] = jnp.where(ti > 0, ph_ref[:, E_CG:E_CG + BW] * ph_ref[:, E_XIN:E_XIN + BW], 0.0)
        cbuf[HALO:HALO + tm] = p_ref[:, E_CG:E_CG + BW] * p_ref[:, E_XIN:E_XIN + BW]
        conv = _taps_causal(cbuf, wc_ref, 3, None)
        for rb, cv in enumerate(conv):
            sl = slice(rb * 32, (rb + 1) * 32)
            gate, dgate = _silu_parts(p_ref[sl, E_GATE + BW:E_GATE + 2 * BW])
            bg = p_ref[sl, E_BG:E_BG + BW]
            dyc = dy[sl, BW:2 * BW]
            yb = bg * cv
            y_ref[sl, BW:2 * BW] = (yb * gate).astype(BF16)
            dp_ref[sl, E_GATE + BW:E_GATE + 2 * BW] = (dyc * yb * dgate).astype(BF16)
            dyb = dyc * gate
            dp_ref[sl, E_BG:E_BG + BW] = (dyb * cv).astype(BF16)
            dconv[sl, :] = dyb * bg
        gbuf[0:tm] = dconv[...]
        gbuf[tm:tm + HALO] = jnp.where(i > 0, carry[...], 0.0)
        carry[...] = dconv[0:HALO]
        _tap_grads(dconv, cbuf, wacc, 3)
        for rb, dc in enumerate(_taps_anticausal(gbuf, wc_ref, 3)):
            sl = slice(rb * 32, (rb + 1) * 32)
            dp_ref[sl, E_CG:E_CG + BW] = (dc * p_ref[sl, E_XIN:E_XIN + BW]).astype(BF16)
            dp_ref[sl, E_XIN:E_XIN + BW] = (dc * p_ref[sl, E_CG:E_CG + BW]).astype(BF16)

        for h in range(NH):
            qs = slice(E_Q + h * HD, E_Q + (h + 1) * HD)
            ks = slice(h * HD, (h + 1) * HD)
            vs = slice(XA + h * HD, XA + (h + 1) * HD)
            gs = slice(E_GATE + 2 * BW + h * HD, E_GATE + 2 * BW + (h + 1) * HD)
            ys = slice(2 * BW + h * HD, 2 * BW + (h + 1) * HD)
            q_b = p_ref[:, qs].astype(BF16)
            prob, yx = _attn_head(q_b, kv_ref[:, ks], kv_ref[:, vs])
            gate, dgate = _silu_parts(p_ref[:, gs])
            dyc = dy[:, ys]
            y_ref[:, ys] = (yx * gate).astype(BF16)
            dp_ref[:, gs] = (dyc * yx * dgate).astype(BF16)
            dyx_b = (dyc * gate).astype(BF16)
            dprob = _dot_nt(dyx_b, kv_ref[:, vs])
            dkv_ref[:, vs] += _dot_tn(prob.astype(BF16), dyx_b)
            ds_b = (prob * (dprob - jnp.sum(dprob * prob, axis=-1, keepdims=True)) * (HD ** -0.5)).astype(BF16)
            dp_ref[:, qs] = _dot(ds_b, kv_ref[:, ks]).astype(BF16)
            dkv_ref[:, ks] += _dot_tn(ds_b, q_b)

        @pl.when(i == nt - 1)
        def _():
            for h in range(NH):
                dbs_ref[:, h * HD:(h + 1) * HD] = jnp.broadcast_to(
                    jnp.sum(dbmap[...] * masks[h], axis=-1, keepdims=True), (CHUNK, HD))
            for k in range(3):
                dwc_ref[k:k + 1, :] = _colsum(wacc[k * 8:(k + 1) * 8, :])
            dwc_ref[3:8, :] = jnp.zeros((5, BW), F32)
            causal = (lax.broadcasted_iota(jnp.int32, (CHUNK, CHUNK), 0)
                      >= lax.broadcasted_iota(jnp.int32, (CHUNK, CHUNK), 1))
            for h in range(NH):
                dws_ref[h] = jnp.where(causal, dws_ref[h], 0.0)

    rtile = lambda n: pl.BlockSpec((tm, n), lambda i: (nt - 1 - i, 0))
    outs = [S((t, EVEN_IN), BF16), S((t, MIX), BF16), S((t, D), BF16), S((1, D), F32), S((NH, CHUNK, CHUNK), F32),
            S((CHUNK, NH * HD), F32), S((1, BW), F32), S((1, BW), F32), S((8, BW), F32), S((N_MEM, 2 * XA), F32)]
    return pl.pallas_call(
        body, name="even_bwd", grid=(nt,),
        in_specs=[rtile(D), rtile(D), rtile(EVEN_IN), _halo_spec(EVEN_IN, nt, True, tm)] + [_whole()] * 9,
        out_specs=[rtile(EVEN_IN), rtile(MIX), rtile(D)] + [_full(s.shape) for s in outs[3:]],
        out_shape=outs,
        scratch_shapes=[pltpu.VMEM((tm, MIX), F32), pltpu.VMEM((tm + HALO, BW), F32), pltpu.VMEM((tm + HALO, BW), F32),
                        pltpu.VMEM((tm, BW), F32), pltpu.VMEM((HALO, BW), F32), pltpu.VMEM((tm, BW), F32),
                        pltpu.VMEM((CHUNK, BW), F32), pltpu.VMEM((3 * 8, BW), F32)],
        compiler_params=_params(("arbitrary",)),
    )(dres, o, p, p, lng, lnb, ws, wst, bmap, wc, kv, wout, pg)


def _pool_causal(za, zb, zc, zd, tm):
    n = tm + HALO
    zb[pl.ds(8, n - 8), :] = za[pl.ds(8, n - 8), :] + za[pl.ds(7, n - 8), :]
    zc[pl.ds(16, n - 16), :] = zb[pl.ds(16, n - 16), :] + zb[pl.ds(14, n - 16), :]
    zd[pl.ds(24, n - 24), :] = zc[pl.ds(24, n - 24), :] + zc[pl.ds(20, n - 24), :]
    s16 = zd[pl.ds(HALO, tm), :] + zd[pl.ds(HALO - 8, tm), :]
    return zb[pl.ds(HALO, tm), :], zc[pl.ds(HALO, tm), :], zd[pl.ds(HALO, tm), :], s16


def _pool_anticausal(ea, eb, ec, ed, tm):
    n = tm + HALO
    eb[pl.ds(0, n - 8), :] = ea[pl.ds(0, n - 8), :] + ea[pl.ds(1, n - 8), :]
    ec[pl.ds(0, n - 16), :] = eb[pl.ds(0, n - 16), :] + eb[pl.ds(2, n - 16), :]
    ed[pl.ds(0, n - 24), :] = ec[pl.ds(0, n - 24), :] + ec[pl.ds(4, n - 24), :]
    a16 = ed[pl.ds(0, tm), :] + ed[pl.ds(8, tm), :]
    return eb[pl.ds(0, tm), :], ec[pl.ds(0, tm), :], ed[pl.ds(0, tm), :], a16


def _pool_weights(ti, masks, tm):
    tf = (ti * tm + lax.broadcasted_iota(jnp.int32, (tm, 1), 0) + 1).astype(F32)
    inv = None
    for g, win in enumerate(POOL_WINDOWS):
        term = masks[g] * (1.0 / jnp.minimum(tf, float(win)))
        inv = term if inv is None else inv + term
    return inv


def _mix4(masks, parts):
    return masks[0] * parts[0] + masks[1] * parts[1] + masks[2] * parts[2] + masks[3] * parts[3]


def _odd_fwd(x1, tgt, p, wbd, cscale, dww, dwb, lng, lnb, wpw, pwb, kv, wout, pg):
    t = x1.shape[0]
    tm = min(TM_FWD, t)
    nt = t // tm

    def body(x_ref, tgt_ref, p_ref, ph_ref, wbd_ref, cs_ref, dww_ref, dwb_ref, lng_ref, lnb_ref, wpw_ref, pwb_ref,
             kv_ref, wout_ref, pg_ref, o_ref, dres_ref, loss_ref, ybuf, za, zb, zc, zd, gbuf, tmp, lacc):
        i = pl.program_id(0)
        masks = _group_masks()

        @pl.when(i == 0)
        def _():
            lacc[...] = jnp.zeros_like(lacc)

        z = p_ref[:, O_ZC:O_ZC + BW]
        za[0:HALO] = jnp.where(i > 0, ph_ref[:, O_ZC:O_ZC + BW], 0.0)
        za[HALO:HALO + tm] = z
        pooled = _mix4(masks, _pool_causal(za, zb, zc, zd, tm)) * _pool_weights(i, masks, tm) - z
        gate, _ = _silu_parts(p_ref[:, O_GATE:O_GATE + BW])
        ybuf[:, 0:BW] = (_dot(pooled.astype(BF16), wbd_ref[...]) * cs_ref[...] * gate).astype(BF16)

        gbuf[0:HALO] = jnp.where(i > 0, ph_ref[:, O_GA:O_GA + BW] * _sigmoid(ph_ref[:, O_GB:O_GB + BW]), 0.0)
        gbuf[HALO:HALO + tm] = p_ref[:, O_GA:O_GA + BW] * _sigmoid(p_ref[:, O_GB:O_GB + BW])
        for rb, blk in enumerate(_taps_causal(gbuf, dww_ref, CONF_K, dwb_ref[...])):
            tmp[rb * 32:(rb + 1) * 32, :] = blk
        zh, _ = _ln_stats(tmp[...])
        zn = zh * lng_ref[...] + lnb_ref[...]
        yd = _dot((zn * _sigmoid(zn)).astype(BF16), wpw_ref[...]) + pwb_ref[...]
        gate, _ = _silu_parts(p_ref[:, O_GATE + BW:O_GATE + 2 * BW])
        ybuf[:, BW:2 * BW] = (yd * gate).astype(BF16)

        for h in range(NH):
            qs = slice(O_Q + h * HD, O_Q + (h + 1) * HD)
            _, yx = _attn_head(p_ref[:, qs].astype(BF16), kv_ref[:, h * HD:(h + 1) * HD],
                               kv_ref[:, XA + h * HD:XA + (h + 1) * HD])
            gs = slice(O_GATE + 2 * BW + h * HD, O_GATE + 2 * BW + (h + 1) * HD)
            gate, _ = _silu_parts(p_ref[:, gs])
            ybuf[:, 2 * BW + h * HD:2 * BW + (h + 1) * HD] = (yx * gate).astype(BF16)

        o = _dot(ybuf[...], wout_ref[...])
        o_ref[...] = o
        err = _rms_residual(x_ref[...], o, pg_ref[...]) - tgt_ref[...]
        lacc[...] += _colsum(err * err)
        dres_ref[...] = err * (1.0 / D)

        @pl.when(i == nt - 1)
        def _():
            loss_ref[...] = jnp.full((1, HD), jnp.sum(lacc[...]) * (0.5 / D), F32)

    tile = lambda n: pl.BlockSpec((tm, n), lambda i: (i, 0))
    ext = pltpu.VMEM((tm + HALO, BW), F32)
    return pl.pallas_call(
        body, name="odd_fwd", grid=(nt,),
        in_specs=[tile(D), tile(D), tile(ODD_IN), _halo_spec(ODD_IN, nt, False, tm)] + [_whole()] * 11,
        out_specs=[tile(D), tile(D), _full((1, HD))],
        out_shape=[S((t, D), F32), S((t, D), F32), S((1, HD), F32)],
        scratch_shapes=[pltpu.VMEM((tm, MIX), BF16), ext, ext, ext, ext, ext, pltpu.VMEM((tm, BW), F32),
                        pltpu.VMEM((1, D), F32)],
        compiler_params=_params(("arbitrary",)),
    )(x1, tgt, p, p, wbd, cscale, dww, dwb, lng, lnb, wpw, pwb, kv, wout, pg)


def _odd_bwd(dres, o, p, wbd, cscale, dww, dwb, lng, lnb, wpw, pwb, kv, wout, pg):
    t = dres.shape[0]
    tm = min(TM_BWD, t)
    nt = t // tm

    def body(dres_ref, o_ref, p_ref, ph_ref, wbd_ref, cs_ref, dww_ref, dwb_ref, lng_ref, lnb_ref, wpw_ref, pwb_ref,
             kv_ref, wout_ref, pg_ref, dp_ref, y_ref, do_ref, dpg_ref, dwbd_ref, dcs_ref, ddww_ref, ddwb_ref,
             dlng_ref, dlnb_ref, dwpw_ref, dpwb_ref, dkv_ref,
             dy, za, zb, zc, zd, gbuf, hbuf, tmp, carry_e, carry_d, wacc):
        i = pl.program_id(0)
        ti = nt - 1 - i
        masks = _group_masks()

        @pl.when(i == 0)
        def _():
            for ref in (dpg_ref, dwbd_ref, dcs_ref, ddwb_ref, dlng_ref, dlnb_ref, dwpw_ref, dpwb_ref, dkv_ref, wacc):
                ref[...] = jnp.zeros_like(ref)

        do, dpg = _rms_post_bwd(dres_ref[...], o_ref[...], pg_ref[...])
        dpg_ref[...] += dpg
        do_b = do.astype(BF16)
        do_ref[...] = do_b
        dy[...] = _dot_nt(do_b, wout_ref[...])

        z = p_ref[:, O_ZC:O_ZC + BW]
        za[0:HALO] = jnp.where(ti > 0, ph_ref[:, O_ZC:O_ZC + BW], 0.0)
        za[HALO:HALO + tm] = z
        inv = _pool_weights(ti, masks, tm)
        pooled_b = (_mix4(masks, _pool_causal(za, zb, zc, zd, tm)) * inv - z).astype(BF16)
        pm = _dot(pooled_b, wbd_ref[...])
        gate, dgate = _silu_parts(p_ref[:, O_GATE:O_GATE + BW])
        dyc = dy[:, 0:BW]
        yc = pm * cs_ref[...]
        y_ref[:, 0:BW] = (yc * gate).astype(BF16)
        dp_ref[:, O_GATE:O_GATE + BW] = (dyc * yc * dgate).astype(BF16)
        dyc = dyc * gate
        dcs_ref[...] += _colsum(dyc * pm)
        dpm_b = (dyc * cs_ref[...]).astype(BF16)
        dwbd_ref[...] += _dot_tn(pooled_b, dpm_b)
        dpool = _dot_nt(dpm_b, wbd_ref[...])
        e = dpool * inv
        za[0:tm] = e
        za[tm:tm + HALO] = jnp.where(i > 0, carry_e[...], 0.0)
        carry_e[...] = e[0:HALO]
        dp_ref[:, O_ZC:O_ZC + BW] = (_mix4(masks, _pool_anticausal(za, zb, zc, zd, tm)) - dpool).astype(BF16)

        gbuf[0:HALO] = jnp.where(ti > 0, ph_ref[:, O_GA:O_GA + BW] * _sigmoid(ph_ref[:, O_GB:O_GB + BW]), 0.0)
        gbuf[HALO:HALO + tm] = p_ref[:, O_GA:O_GA + BW] * _sigmoid(p_ref[:, O_GB:O_GB + BW])
        for rb, blk in enumerate(_taps_causal(gbuf, dww_ref, CONF_K, dwb_ref[...])):
            tmp[rb * 32:(rb + 1) * 32, :] = blk
        zh, rs = _ln_stats(tmp[...])
        zn = zh * lng_ref[...] + lnb_ref[...]
        zs, dsilu = _silu_parts(zn)
        zs_b = zs.astype(BF16)
        yd = _dot(zs_b, wpw_ref[...]) + pwb_ref[...]
        gate, dgate = _silu_parts(p_ref[:, O_GATE + BW:O_GATE + 2 * BW])
        dyc = dy[:, BW:2 * BW]
        y_ref[:, BW:2 * BW] = (yd * gate).astype(BF16)
        dp_ref[:, O_GATE + BW:O_GATE + 2 * BW] = (dyc * yd * dgate).astype(BF16)
        dyd = dyc * gate
        dyd_b = dyd.astype(BF16)
        dpwb_ref[...] += _colsum(dyd)
        dwpw_ref[...] += _dot_tn(zs_b, dyd_b)
        dzn = _dot_nt(dyd_b, wpw_ref[...]) * dsilu
        dlng_ref[...] += _colsum(dzn * zh)
        dlnb_ref[...] += _colsum(dzn)
        dzd = _ln_bwd(dzn, zh, rs, lng_ref[...])
        ddwb_ref[...] += _colsum(dzd)
        tmp[...] = dzd
        hbuf[0:tm] = dzd
        hbuf[tm:tm + HALO] = jnp.where(i > 0, carry_d[...], 0.0)
        carry_d[...] = dzd[0:HALO]
        _tap_grads(tmp, gbuf, wacc, CONF_K)
        for rb, dzg in enumerate(_taps_anticausal(hbuf, dww_ref, CONF_K)):
            sl = slice(rb * 32, (rb + 1) * 32)
            sgb = _sigmoid(p_ref[sl, O_GB:O_GB + BW])
            dp_ref[sl, O_GA:O_GA + BW] = (dzg * sgb).astype(BF16)
            dp_ref[sl, O_GB:O_GB + BW] = (dzg * p_ref[sl, O_GA:O_GA + BW] * sgb * (1.0 - sgb)).astype(BF16)

        for h in range(NH):
            qs = slice(O_Q + h * HD, O_Q + (h + 1) * HD)
            ks = slice(h * HD, (h + 1) * HD)
            vs = slice(XA + h * HD, XA + (h + 1) * HD)
            gs = slice(O_GATE + 2 * BW + h * HD, O_GATE + 2 * BW + (h + 1) * HD)
            ys = slice(2 * BW + h * HD, 2 * BW + (h + 1) * HD)
            q_b = p_ref[:, qs].astype(BF16)
            prob, yx = _attn_head(q_b, kv_ref[:, ks], kv_ref[:, vs])
            gate, dgate = _silu_parts(p_ref[:, gs])
            dyc = dy[:, ys]
            y_ref[:, ys] = (yx * gate).astype(BF16)
            dp_ref[:, gs] = (dyc * yx * dgate).astype(BF16)
            dyx_b = (dyc * gate).astype(BF16)
            dprob = _dot_nt(dyx_b, kv_ref[:, vs])
            dkv_ref[:, vs] += _dot_tn(prob.astype(BF16), dyx_b)
            ds_b = (prob * (dprob - jnp.sum(dprob * prob, axis=-1, keepdims=True)) * (HD ** -0.5)).astype(BF16)
            dp_ref[:, qs] = _dot(ds_b, kv_ref[:, ks]).astype(BF16)
            dkv_ref[:, ks] += _dot_tn(ds_b, q_b)

        @pl.when(i == nt - 1)
        def _():
            for k in range(CONF_K):
                ddww_ref[k:k + 1, :] = _colsum(wacc[k * 8:(k + 1) * 8, :])
            ddww_ref[CONF_K:CONF_K + 1, :] = jnp.zeros((1, BW), F32)

    rtile = lambda n: pl.BlockSpec((tm, n), lambda i: (nt - 1 - i, 0))
    outs = [S((t, ODD_IN), BF16), S((t, MIX), BF16), S((t, D), BF16), S((1, D), F32), S((BW, BW), F32),
            S((1, BW), F32), S((CONF_K + 1, BW), F32), S((1, BW), F32), S((1, BW), F32), S((1, BW), F32),
            S((BW, BW), F32), S((1, BW), F32), S((N_MEM, 2 * XA), F32)]
    ext = pltpu.VMEM((tm + HALO, BW), F32)
    return pl.pallas_call(
        body, name="odd_bwd", grid=(nt,),
        in_specs=[rtile(D), rtile(D), rtile(ODD_IN), _halo_spec(ODD_IN, nt, True, tm)] + [_whole()] * 11,
        out_specs=[rtile(ODD_IN), rtile(MIX), rtile(D)] + [_full(s.shape) for s in outs[3:]],
        out_shape=outs,
        scratch_shapes=[pltpu.VMEM((tm, MIX), F32), ext, ext, ext, ext, ext, ext, pltpu.VMEM((tm, BW), F32),
                        pltpu.VMEM((HALO, BW), F32), pltpu.VMEM((HALO, BW), F32), pltpu.VMEM((CONF_K * 8, BW), F32)],
        compiler_params=_params(("arbitrary",)),
    )(dres, o, p, p, wbd, cscale, dww, dwb, lng, lnb, wpw, pwb, kv, wout, pg)


def _pick_tn(n):
    for tn in (640, 2432, 1024, 768):
        if n % tn == 0:
            return tn
    return n


def _pad_rows(a, rows):
    return jnp.pad(a, ((0, rows - a.shape[0]), (0, 0)))


def _local_step(x, mem, tgt, w):
    t = x.shape[0]
    tm = min(512, t)
    causal = jnp.tril(jnp.ones((CHUNK, CHUNK), bool))
    ws = jnp.where(causal[None], w["even_a_ws"], 0.0).astype(BF16)
    wst = jnp.transpose(ws, (0, 2, 1))
    bmap = jnp.repeat(w["even_a_bs"].T, GRP, axis=1)
    wc = _pad_rows(w["even_b_conv"], 8)
    wbd = jax.scipy.linalg.block_diag(*[w["odd_c_wgrp"][g] for g in range(NH)]).astype(BF16)
    dww = _pad_rows(w["odd_d_dw_w"], CONF_K + 1)
    g = {}

    kv_e, memn_e = _rms_matmul(mem, w["even_mem_g"], w["even_w_kv"], tm=N_MEM, tn=2 * XA, name="kv_even")
    kv_o, memn_o = _rms_matmul(mem, w["odd_mem_g"], w["odd_w_kv"], tm=N_MEM, tn=2 * XA, name="kv_odd")
    kv_e, kv_o = kv_e.astype(BF16), kv_o.astype(BF16)

    p_e, h_e = _rms_matmul(x, w["even_pre_g"], w["even_w_in"], tm=tm, tn=_pick_tn(EVEN_IN), name="in_even")
    even_args = (w["even_a_ln_g"], w["even_a_ln_b"], ws)
    o_e, x1 = _even_fwd(x, p_e, *even_args, bmap, wc, kv_e, w["even_w_out"], w["even_post_g"])
    p_o, h_o = _rms_matmul(x1, w["odd_pre_g"], w["odd_w_in"], tm=tm, tn=_pick_tn(ODD_IN), name="in_odd")
    odd_args = (wbd, w["odd_c_scale"], dww, w["odd_d_dw_b"], w["odd_d_ln_g"], w["odd_d_ln_b"], w["odd_d_pw_w"],
                w["odd_d_pw_b"], kv_o, w["odd_w_out"], w["odd_post_g"])
    o_o, dres, loss = _odd_fwd(x1, tgt, p_o, *odd_args)

    tk = min(1024, t)
    (dp_o, y_o, do_o, g["odd_post_g"], dwbd, g["odd_c_scale"], ddww, g["odd_d_dw_b"], g["odd_d_ln_g"],
     g["odd_d_ln_b"], g["odd_d_pw_w"], g["odd_d_pw_b"], dkv_o) = _odd_bwd(dres, o_o, p_o, *odd_args)
    g["odd_c_wgrp"] = jnp.stack([dwbd[i * GRP:(i + 1) * GRP, i * GRP:(i + 1) * GRP] for i in range(NH)])
    g["odd_d_dw_w"] = ddww[:CONF_K]
    dx1, g["odd_pre_g"] = _nt_matmul_rms_bwd(dp_o, w["odd_w_in"], x1, w["odd_pre_g"], dres, tm=min(256, t),
                                             name="dx_odd")
    g["odd_w_in"] = _tn_matmul(h_o, dp_o, tn=_pick_tn(ODD_IN), tk=tk, out_dtype=BF16, name="dw_in_odd")
    g["odd_w_out"] = _tn_matmul(y_o, do_o, tn=D, tk=tk, out_dtype=BF16, name="dw_out_odd")
    dkv_o = dkv_o.astype(BF16)
    g["odd_w_kv"] = _tn_matmul(memn_o, dkv_o, tn=2 * XA, tk=N_MEM, out_dtype=BF16, name="dw_kv_odd")
    zeros = jnp.zeros_like(mem)
    _, g["odd_mem_g"] = _nt_matmul_rms_bwd(dkv_o, w["odd_w_kv"], mem, w["odd_mem_g"], zeros, tm=N_MEM,
                                           name="dmem_odd")

    (dp_e, y_e, do_e, g["even_post_g"], g["even_a_ws"], dbs, g["even_a_ln_g"], g["even_a_ln_b"], dwc,
     dkv_e) = _even_bwd(dx1, o_e, p_e, *even_args, wst, bmap, wc, kv_e, w["even_w_out"], w["even_post_g"])
    g["even_a_bs"] = dbs[:, ::HD].T
    g["even_b_conv"] = dwc[:3]
    grad_x, g["even_pre_g"] = _nt_matmul_rms_bwd(dp_e, w["even_w_in"], x, w["even_pre_g"], dx1, tm=min(256, t),
                                                 name="dx_even")
    g["even_w_in"] = _tn_matmul(h_e, dp_e, tn=_pick_tn(EVEN_IN), tk=tk, out_dtype=BF16, name="dw_in_even")
    g["even_w_out"] = _tn_matmul(y_e, do_e, tn=D, tk=tk, out_dtype=BF16, name="dw_out_even")
    dkv_e = dkv_e.astype(BF16)
    g["even_w_kv"] = _tn_matmul(memn_e, dkv_e, tn=2 * XA, tk=N_MEM, out_dtype=BF16, name="dw_kv_even")
    _, g["even_mem_g"] = _nt_matmul_rms_bwd(dkv_e, w["even_w_kv"], mem, w["even_mem_g"], zeros, tm=N_MEM,
                                            name="dmem_even")
    return loss, grad_x, g


def _place():
    return lax.axis_index("x"), lax.axis_index("y"), lax.axis_index("c")


def _index(px, py, pc):
    return 4 * px + 2 * py + pc


_ANY = pl.BlockSpec(memory_space=pl.ANY)
_COPIES = N_DEV - 1


def _all_gather(arrs, name):
    n = len(arrs)

    def body(*refs):
        ins, outs = refs[:n], refs[n:2 * n]
        send_sems, recv_sems, local_sems = refs[2 * n:]
        x, y, c = _place()
        me, sibling = (x, y, c), (x, y, 1 - c)
        chips = [(1 - x, y), (x, 1 - y), (1 - x, 1 - y)]

        def copy(a, k, block, to, src=None):
            dst = outs[a].at[_index(*block)]
            return pltpu.make_async_remote_copy(
                src_ref=dst if src is None else src, dst_ref=dst, send_sem=send_sems.at[a * _COPIES + k],
                recv_sem=recv_sems.at[a * _COPIES + k], device_id=to, device_id_type=MESH)

        mine = [pltpu.make_async_copy(ins[a], outs[a].at[_index(*me)], local_sems.at[a]) for a in range(n)]
        first = []
        for a in range(n):
            mine[a].start()
            first.append(copy(a, 0, me, sibling, src=ins[a]))
            first += [copy(a, 1 + j, me, (*chip, c), src=ins[a]) for j, chip in enumerate(chips)]
        for cp in first:
            cp.start()
        passed = []
        for j, chip in enumerate(chips):
            for a in range(n):
                copy(a, 1 + j, (*chip, c), me).wait_recv()
                passed.append(copy(a, 4 + j, (*chip, c), sibling))
                passed[-1].start()
        for a in range(n):
            copy(a, 0, sibling, me).wait_recv()
            for j, chip in enumerate(chips):
                copy(a, 4 + j, (*chip, 1 - c), me).wait_recv()
        for cp in first + passed:
            cp.wait_send()
        for cp in mine:
            cp.wait()

    return pl.pallas_call(
        body, name=name, in_specs=[_ANY] * n, out_specs=[_ANY] * n,
        out_shape=[S((N_DEV,) + a.shape, a.dtype) for a in arrs],
        scratch_shapes=[pltpu.SemaphoreType.DMA((n * _COPIES,)), pltpu.SemaphoreType.DMA((n * _COPIES,)),
                        pltpu.SemaphoreType.DMA((n,))],
    )(*arrs)


def _all_to_all(arrs, name):
    n = len(arrs)

    def body(*refs):
        ins, outs = refs[:n], refs[n:2 * n]
        send_sems, recv_sems, local_sems = refs[2 * n:]
        x, y, c = _place()
        mine = _index(x, y, c)
        flips = [(k >> 2 & 1, k >> 1 & 1, k & 1) for k in range(1, N_DEV)]
        peers = [(1 - x if fx else x, 1 - y if fy else y, 1 - c if fc else c) for fx, fy, fc in flips]

        def copy(a, k, dst_block):
            return pltpu.make_async_remote_copy(
                src_ref=ins[a].at[_index(*peers[k])], dst_ref=outs[a].at[dst_block],
                send_sem=send_sems.at[a * _COPIES + k], recv_sem=recv_sems.at[a * _COPIES + k],
                device_id=peers[k], device_id_type=MESH)

        local = [pltpu.make_async_copy(ins[a].at[mine], outs[a].at[mine], local_sems.at[a]) for a in range(n)]
        for a in range(n):
            local[a].start()
            for k in range(_COPIES):
                copy(a, k, mine).start()
        for a in range(n):
            for k in range(_COPIES):
                arrival = copy(a, k, _index(*peers[k]))
                arrival.wait_send()
                arrival.wait_recv()
            local[a].wait()

    return pl.pallas_call(
        body, name=name, in_specs=[_ANY] * n, out_specs=[_ANY] * n,
        out_shape=[S(a.shape, a.dtype) for a in arrs],
        scratch_shapes=[pltpu.SemaphoreType.DMA((n * _COPIES,)), pltpu.SemaphoreType.DMA((n * _COPIES,)),
                        pltpu.SemaphoreType.DMA((n,))],
    )(*arrs)


def _adamw(w, g, m, v):
    m = ADAM_B1 * m + (1.0 - ADAM_B1) * g
    v = ADAM_B2 * v + (1.0 - ADAM_B2) * (g * g)
    m_hat = m / (1.0 - ADAM_B1 ** ADAM_STEP)
    v_hat = v / (1.0 - ADAM_B2 ** ADAM_STEP)
    return -ADAM_LR * (m_hat / (jnp.sqrt(v_hat) + ADAM_EPS) + ADAM_WD * w), m, v


def _sum_devices(ref, rows):
    total = ref[0, rows, :].astype(F32)
    for s in range(1, N_DEV):
        total = total + ref[s, rows, :].astype(F32)
    return total


def _adam_big(recv, w, m, v, *, tr, name):
    r, c = w.shape

    def body(recv_ref, w_ref, m_ref, v_ref, g_ref, d_ref, m2_ref, v2_ref):
        g = _sum_devices(recv_ref, slice(None))
        g_ref[...] = g
        d_ref[...], m2_ref[...], v2_ref[...] = _adamw(w_ref[...], g, m_ref[...], v_ref[...])

    blk = pl.BlockSpec((tr, c), lambda i: (i, 0))
    return pl.pallas_call(
        body, name=name, grid=(r // tr,),
        in_specs=[pl.BlockSpec((N_DEV, tr, c), lambda i: (0, i, 0)), blk, blk, blk],
        out_specs=[blk] * 4, out_shape=[S((r, c), F32)] * 4,
        compiler_params=_params(("arbitrary",)),
    )(recv, w, m, v)


_REPLICATED = {"even_pre_g": (0, 0, 1), "even_mem_g": (0, 8, 1), "even_post_g": (0, 16, 1),
               "even_a_ln_g": (1, 0, 1), "even_a_ln_b": (1, 8, 1),
               "even_a_ws": (2, 0, NH * CHUNK), "even_a_bs": (2, NH * CHUNK, NH),
               "odd_c_wgrp": (3, 0, NH * GRP)}
_SHARDED = {"odd_pre_g": (4, 0, 1), "odd_mem_g": (4, 8, 1), "odd_post_g": (4, 16, 1),
            "even_b_conv": (5, 0, 3), "odd_c_scale": (5, 8, 1), "odd_d_dw_w": (5, 16, CONF_K),
            "odd_d_dw_b": (5, 48, 1), "odd_d_ln_g": (5, 56, 1), "odd_d_ln_b": (5, 64, 1), "odd_d_pw_b": (5, 72, 1)}
_SMALL = {**_REPLICATED, **_SHARDED}
_SMALL_ROWS = {0: 24, 1: 16, 2: NH * CHUNK + 8, 3: NH * GRP, 4: 24, 5: 80}


def _adam_small(sources, wmv):
    names = list(_SMALL)
    ns = len(sources)

    def body(*refs):
        src = refs[:ns]
        ins = refs[ns:ns + 3 * len(names)]
        outs = refs[ns + 3 * len(names):]
        for i, nm in enumerate(names):
            a, row0, rows = _SMALL[nm]
            g = _sum_devices(src[a], slice(row0, row0 + rows))
            w_ref, m_ref, v_ref = ins[3 * i:3 * i + 3]
            g_ref, d_ref, m2_ref, v2_ref = outs[4 * i:4 * i + 4]
            g_ref[...] = g
            d_ref[...], m2_ref[...], v2_ref[...] = _adamw(w_ref[...], g, m_ref[...], v_ref[...])

    flat = [t for nm in names for t in wmv[nm]]
    out_shape = [S(wmv[nm][0].shape, F32) for nm in names for _ in range(4)]
    res = pl.pallas_call(
        body, name="adam_small", in_specs=[_whole()] * (ns + len(flat)), out_specs=[_whole()] * len(out_shape),
        out_shape=out_shape, compiler_params=_params(),
    )(*sources, *flat)
    return {nm: tuple(res[4 * i:4 * i + 4]) for i, nm in enumerate(names)}


_WEIGHTS = ["even_pre_g", "even_w_in", "even_a_ln_g", "even_a_ln_b", "even_a_ws", "even_a_bs", "even_b_conv",
            "even_mem_g", "even_w_kv", "even_w_out", "even_post_g", "odd_pre_g", "odd_w_in", "odd_c_wgrp",
            "odd_c_scale", "odd_d_dw_w", "odd_d_dw_b", "odd_d_ln_g", "odd_d_ln_b", "odd_d_pw_w", "odd_d_pw_b",
            "odd_mem_g", "odd_w_kv", "odd_w_out", "odd_post_g"]
_BIG_COLS = ["even_w_in", "odd_w_in"]
_BIG_ROWS = ["even_w_kv", "even_w_out", "odd_w_kv", "odd_w_out", "odd_d_pw_w"]
_BIG = _BIG_COLS + _BIG_ROWS
_BIG_TILE_ROWS = {"even_w_in": 256, "odd_w_in": 256, "even_w_kv": 128, "even_w_out": 128, "odd_w_kv": 128,
                  "odd_w_out": 128, "odd_d_pw_w": 96}


def _view2d(a):
    a = a[0]
    if a.ndim == 1:
        return a[None]
    return a.reshape(-1, a.shape[-1])


def _rows8(a):
    return _pad_rows(a, -(-a.shape[0] // 8) * 8)


def _pack_rows(parts):
    return jnp.concatenate([_rows8(p) for p in parts], axis=0)


def _unshard_cols(a):
    return jnp.transpose(a, (1, 0, 2)).reshape(a.shape[1], N_DEV * a.shape[2])


def _shard_cols(a):
    return jnp.transpose(a.reshape(a.shape[0], N_DEV, a.shape[1] // N_DEV), (1, 0, 2))


def kernel(x, mem, even_pre_g, even_w_in, even_a_ln_g, even_a_ln_b, even_a_ws, even_a_bs, even_b_conv, even_mem_g, even_w_kv, even_w_out, even_post_g, odd_pre_g, odd_w_in, odd_c_wgrp, odd_c_scale, odd_d_dw_w, odd_d_dw_b, odd_d_ln_g, odd_d_ln_b, odd_d_pw_w, odd_d_pw_b, odd_mem_g, odd_w_kv, odd_w_out, odd_post_g, loss_target, m_even_pre_g, m_even_w_in, m_even_a_ln_g, m_even_a_ln_b, m_even_a_ws, m_even_a_bs, m_even_b_conv, m_even_mem_g, m_even_w_kv, m_even_w_out, m_even_post_g, m_odd_pre_g, m_odd_w_in, m_odd_c_wgrp, m_odd_c_scale, m_odd_d_dw_w, m_odd_d_dw_b, m_odd_d_ln_g, m_odd_d_ln_b, m_odd_d_pw_w, m_odd_d_pw_b, m_odd_mem_g, m_odd_w_kv, m_odd_w_out, m_odd_post_g, v_even_pre_g, v_even_w_in, v_even_a_ln_g, v_even_a_ln_b, v_even_a_ws, v_even_a_bs, v_even_b_conv, v_even_mem_g, v_even_w_kv, v_even_w_out, v_even_post_g, v_odd_pre_g, v_odd_w_in, v_odd_c_wgrp, v_odd_c_scale, v_odd_d_dw_w, v_odd_d_dw_b, v_odd_d_ln_g, v_odd_d_ln_b, v_odd_d_pw_w, v_odd_d_pw_b, v_odd_mem_g, v_odd_w_kv, v_odd_w_out, v_odd_post_g):
    given = dict(locals())
    shard = {nm: _view2d(given[nm]) for nm in _WEIGHTS}
    wmv = {nm: (shard[nm], _view2d(given["m_" + nm]), _view2d(given["v_" + nm])) for nm in _WEIGHTS}

    packs = [_pack_rows([shard[nm] for nm in _SHARDED if _SHARDED[nm][0] == a]) for a in (4, 5)]
    gathered = _all_gather([shard[nm].astype(BF16) for nm in _BIG] + packs, "gather_weights")
    w = {nm: shard[nm] for nm in _REPLICATED}
    for nm, full in zip(_BIG, gathered):
        w[nm] = _unshard_cols(full) if nm in _BIG_COLS else full.reshape(-1, full.shape[-1])
    full_packs = {4: _unshard_cols(gathered[-2]), 5: _unshard_cols(gathered[-1])}
    for nm, (a, row0, rows) in _SHARDED.items():
        w[nm] = full_packs[a][row0:row0 + rows]
    w["even_a_ws"] = w["even_a_ws"].reshape(NH, CHUNK, CHUNK)
    w["odd_c_wgrp"] = w["odd_c_wgrp"].reshape(NH, GRP, GRP)

    loss, grad_x, g = _local_step(x[0], mem[0], loss_target[0], w)
    g["even_a_ws"] = g["even_a_ws"].reshape(NH * CHUNK, CHUNK)
    g["odd_c_wgrp"] = g["odd_c_wgrp"].reshape(NH * GRP, GRP)

    to_owner = [_shard_cols(g[nm]) if nm in _BIG_COLS else g[nm].astype(BF16).reshape(N_DEV, -1, g[nm].shape[-1])
                for nm in _BIG]
    to_owner += [_shard_cols(_pack_rows([g[nm] for nm in _SHARDED if _SHARDED[nm][0] == a])) for a in (4, 5)]
    received = _all_to_all(to_owner, "scatter_grads")
    to_all = [_pack_rows([g[nm] for nm in _REPLICATED if _REPLICATED[nm][0] == a]) for a in range(4)]
    everyone = _all_gather(to_all, "gather_small_grads")

    res = _adam_small(list(everyone) + list(received[-2:]), {nm: wmv[nm] for nm in _SMALL})
    for nm, recv in zip(_BIG, received):
        res[nm] = _adam_big(recv, *wmv[nm], tr=_BIG_TILE_ROWS[nm], name="adam_" + nm)

    total = lax.psum(loss[0, 0], AXES)
    outs = [[res[nm][i].reshape(given[nm].shape) for nm in _WEIGHTS] for i in range(4)]
    return (total, grad_x[None], *outs[0], *outs[1], *outs[2], *outs[3])
```

```python
import functools

import jax
import jax.numpy as jnp
from jax import lax
from jax.experimental import pallas as pl
from jax.experimental.pallas import tpu as pltpu

F32 = jnp.float32
BF16 = jnp.bfloat16
S = jax.ShapeDtypeStruct
MESH = pl.DeviceIdType.MESH
AXES = ("x", "y", "c")
N_DEV = 8

D = 1024
BW = 768
XA = 512
HD = 128
NH = 4
MIX = 2048
CHUNK = 128
GRP = 192
N_MEM = 256
CONF_K = 31
EPS = 1e-6
HALO = 32
POOL_WINDOWS = (2, 4, 8, 16)
TM_FWD = 256
TM_BWD = 128

E_U, E_V, E_BG, E_CG, E_XIN, E_Q, E_GATE = 0, 768, 1536, 2304, 3072, 3840, 4352
EVEN_IN = 6400
O_ZC, O_GA, O_GB, O_Q, O_GATE = 0, 768, 1536, 2304, 2816
ODD_IN = 4864

ADAM_LR, ADAM_B1, ADAM_B2, ADAM_EPS, ADAM_WD, ADAM_STEP = 0.001, 0.9, 0.999, 1e-08, 0.01, 10

VMEM_LIMIT_V7X = 56 * 1024 * 1024


def _params(sem=None):
    return pltpu.CompilerParams(dimension_semantics=sem, vmem_limit_bytes=VMEM_LIMIT_V7X)


def _dot(a, b):
    return jnp.dot(a, b, preferred_element_type=F32)


def _dot_nt(a, b):
    return lax.dot_general(a, b, (((1,), (1,)), ((), ())), preferred_element_type=F32)


def _dot_tn(a, b):
    return lax.dot_general(a, b, (((0,), (0,)), ((), ())), preferred_element_type=F32)


def _sigmoid(z):
    return 1.0 / (1.0 + jnp.exp(-z))


def _rowmean(a):
    return jnp.mean(a, axis=-1, keepdims=True)


def _colsum(a):
    return jnp.sum(a, axis=0, keepdims=True)


def _ln_stats(v):
    mu = _rowmean(v)
    vc = v - mu
    rs = lax.rsqrt(_rowmean(vc * vc) + EPS)
    return vc * rs, rs


def _ln_bwd(dn, vh, rs, g):
    dvh = dn * g
    return rs * (dvh - _rowmean(dvh) - vh * _rowmean(dvh * vh))


def _group_masks():
    col = lax.broadcasted_iota(jnp.int32, (1, BW), 1)
    return [((col >= GRP * h) & (col < GRP * (h + 1))).astype(F32) for h in range(NH)]


def _full(shape):
    nd = len(shape)
    return pl.BlockSpec(shape, lambda *_: (0,) * nd)


def _whole():
    return pl.BlockSpec(memory_space=pltpu.VMEM)


_ANY = pl.BlockSpec(memory_space=pl.ANY)


def _after(body, n_in, deps):
    def ordered(*refs):
        return body(*refs[:n_in], *refs[n_in + len(deps):])
    return ordered


def _rms_matmul(x, g, w, *, tm, tn, name, deps=()):
    t, d = x.shape
    n = w.shape[1]

    def body(x_ref, g_ref, w_ref, p_ref, h_ref):
        @pl.when(pl.program_id(1) == 0)
        def _():
            xv = x_ref[...]
            r = lax.rsqrt(_rowmean(xv * xv) + EPS)
            h_ref[...] = (xv * r * g_ref[...]).astype(BF16)

        p_ref[...] = _dot(h_ref[...], w_ref[...])

    return pl.pallas_call(
        _after(body, 3, deps), name=name, grid=(t // tm, n // tn),
        in_specs=[pl.BlockSpec((tm, d), lambda i, j: (i, 0)), pl.BlockSpec((1, d), lambda i, j: (0, 0)),
                  pl.BlockSpec((d, tn), lambda i, j: (0, j))] + [_ANY] * len(deps),
        out_specs=[pl.BlockSpec((tm, tn), lambda i, j: (i, j)), pl.BlockSpec((tm, d), lambda i, j: (i, 0))],
        out_shape=[S((t, n), F32), S((t, d), BF16)],
        compiler_params=_params(("arbitrary", "arbitrary")),
    )(x, g, w, *deps)


def _nt_matmul_rms_bwd(dp, w, x, g, dres, *, tm, name, deps=()):
    t, n = dp.shape
    d = x.shape[1]

    def body(dp_ref, w_ref, x_ref, g_ref, dres_ref, dx_ref, dg_ref):
        @pl.when(pl.program_id(0) == 0)
        def _():
            dg_ref[...] = jnp.zeros_like(dg_ref)

        dh = _dot_nt(dp_ref[...], w_ref[...])
        xv = x_ref[...]
        r = lax.rsqrt(_rowmean(xv * xv) + EPS)
        xh = xv * r
        dg_ref[...] += _colsum(dh * xh)
        dxh = dh * g_ref[...]
        dx_ref[...] = dres_ref[...] + r * (dxh - xh * _rowmean(dxh * xh))

    return pl.pallas_call(
        _after(body, 5, deps), name=name, grid=(t // tm,),
        in_specs=[pl.BlockSpec((tm, n), lambda i: (i, 0)), _whole(), pl.BlockSpec((tm, d), lambda i: (i, 0)),
                  _whole(), pl.BlockSpec((tm, d), lambda i: (i, 0))] + [_ANY] * len(deps),
        out_specs=[pl.BlockSpec((tm, d), lambda i: (i, 0)), pl.BlockSpec((1, d), lambda i: (0, 0))],
        out_shape=[S((t, d), F32), S((1, d), F32)],
        compiler_params=_params(("arbitrary",)),
    )(dp, w, x, g, dres, *deps)


def _tn_matmul(a, b, *, tn, tk, out_dtype, name, deps=()):
    t, m = a.shape
    n = b.shape[1]
    nk = t // tk

    def body(a_ref, b_ref, o_ref, acc_ref):
        k = pl.program_id(1)

        @pl.when(k == 0)
        def _():
            acc_ref[...] = jnp.zeros_like(acc_ref)

        acc_ref[...] += _dot_tn(a_ref[...], b_ref[...])

        @pl.when(k == nk - 1)
        def _():
            o_ref[...] = acc_ref[...].astype(out_dtype)

    return pl.pallas_call(
        _after(body, 2, deps), name=name, grid=(n // tn, nk),
        in_specs=[pl.BlockSpec((tk, m), lambda j, k: (k, 0)), pl.BlockSpec((tk, tn), lambda j, k: (k, j))]
        + [_ANY] * len(deps),
        out_specs=pl.BlockSpec((m, tn), lambda j, k: (0, j)),
        out_shape=S((m, n), out_dtype),
        scratch_shapes=[pltpu.VMEM((m, tn), F32)],
        compiler_params=_params(("arbitrary", "arbitrary")),
    )(a, b, *deps)


def _silu_parts(gt):
    sg = _sigmoid(gt)
    return gt * sg, sg * (1.0 + gt * (1.0 - sg))


def _attn_head(q_b, k_b, v_b):
    s = _dot_nt(q_b, k_b) * (HD ** -0.5)
    e = jnp.exp(s - jnp.max(s, axis=-1, keepdims=True))
    prob = e / jnp.sum(e, axis=-1, keepdims=True)
    return prob, _dot(prob.astype(BF16), v_b)


def _rms_residual(x, o, g):
    r = lax.rsqrt(_rowmean(o * o) + EPS)
    return x + o * r * g


def _rms_post_bwd(dres, o, g):
    r = lax.rsqrt(_rowmean(o * o) + EPS)
    oh = o * r
    doh = dres * g
    return r * (doh - oh * _rowmean(doh * oh)), _colsum(dres * oh)


def _sgu_chunk(vn_b, ws_ref, bmap_ref, masks):
    sg = bmap_ref[...]
    for h in range(NH):
        sg = sg + masks[h] * _dot(ws_ref[h], vn_b)
    return sg


def _taps_causal(buf, w_ref, taps, bias):
    rows = buf.shape[0] - HALO
    outs = []
    for rb in range(rows // 32):
        acc = None
        for k in range(taps):
            term = w_ref[k:k + 1, :] * buf[pl.ds(rb * 32 + HALO - (taps - 1 - k), 32), :]
            acc = term if acc is None else acc + term
        outs.append(acc if bias is None else acc + bias)
    return outs


def _taps_anticausal(buf, w_ref, taps):
    rows = buf.shape[0] - HALO
    outs = []
    for rb in range(rows // 32):
        acc = None
        for k in range(taps):
            term = w_ref[k:k + 1, :] * buf[pl.ds(rb * 32 + (taps - 1 - k), 32), :]
            acc = term if acc is None else acc + term
        outs.append(acc)
    return outs


def _fold8(a):
    return a[0:8] + a[8:16] + a[16:24] + a[24:32]


def _tap_grads(d_ref, buf, acc_ref, taps):
    rows = buf.shape[0] - HALO
    for rb in range(rows // 32):
        dv = d_ref[rb * 32:(rb + 1) * 32, :]
        for k in range(taps):
            prod = dv * buf[pl.ds(rb * 32 + HALO - (taps - 1 - k), 32), :]
            acc_ref[k * 8:(k + 1) * 8, :] += _fold8(prod)


def _halo_spec(n, nt, reverse, tm):
    per = tm // HALO
    if reverse:
        return pl.BlockSpec((HALO, n), lambda i: (jnp.maximum((nt - 1 - i) * per - 1, 0), 0))
    return pl.BlockSpec((HALO, n), lambda i: (jnp.maximum(i * per - 1, 0), 0))


def _even_fwd(x, p, lng, lnb, ws, bmap, wc, kv, wout, pg):
    t = x.shape[0]
    tm = min(TM_FWD, t)
    nt = t // tm

    def body(x_ref, p_ref, ph_ref, lng_ref, lnb_ref, ws_ref, bmap_ref, wc_ref, kv_ref, wout_ref, pg_ref,
             o_ref, x1_ref, ybuf, cbuf):
        i = pl.program_id(0)
        masks = _group_masks()
        vh, _ = _ln_stats(p_ref[:, E_V:E_V + BW])
        vn = vh * lng_ref[...] + lnb_ref[...]
        for c in range(tm // CHUNK):
            sl = slice(c * CHUNK, (c + 1) * CHUNK)
            sg = _sgu_chunk(vn[sl].astype(BF16), ws_ref, bmap_ref, masks)
            gate, _ = _silu_parts(p_ref[sl, E_GATE:E_GATE + BW])
            ybuf[sl, 0:BW] = (p_ref[sl, E_U:E_U + BW] * sg * gate).astype(BF16)

        cbuf[0:HALO] = jnp.where(i > 0, ph_ref[:, E_CG:E_CG + BW] * ph_ref[:, E_XIN:E_XIN + BW], 0.0)
        cbuf[HALO:HALO + tm] = p_ref[:, E_CG:E_CG + BW] * p_ref[:, E_XIN:E_XIN + BW]
        conv = _taps_causal(cbuf, wc_ref, 3, None)
        for rb, cv in enumerate(conv):
            sl = slice(rb * 32, (rb + 1) * 32)
            gate, _ = _silu_parts(p_ref[sl, E_GATE + BW:E_GATE + 2 * BW])
            ybuf[sl, BW:2 * BW] = (p_ref[sl, E_BG:E_BG + BW] * cv * gate).astype(BF16)

        for h in range(NH):
            qs = slice(E_Q + h * HD, E_Q + (h + 1) * HD)
            _, yx = _attn_head(p_ref[:, qs].astype(BF16), kv_ref[:, h * HD:(h + 1) * HD],
                               kv_ref[:, XA + h * HD:XA + (h + 1) * HD])
            gs = slice(E_GATE + 2 * BW + h * HD, E_GATE + 2 * BW + (h + 1) * HD)
            gate, _ = _silu_parts(p_ref[:, gs])
            ybuf[:, 2 * BW + h * HD:2 * BW + (h + 1) * HD] = (yx * gate).astype(BF16)

        o = _dot(ybuf[...], wout_ref[...])
        o_ref[...] = o
        x1_ref[...] = _rms_residual(x_ref[...], o, pg_ref[...])

    tile = lambda n: pl.BlockSpec((tm, n), lambda i: (i, 0))
    return pl.pallas_call(
        body, name="even_fwd", grid=(nt,),
        in_specs=[tile(D), tile(EVEN_IN), _halo_spec(EVEN_IN, nt, False, tm)] + [_whole()] * 8,
        out_specs=[tile(D), tile(D)],
        out_shape=[S((t, D), F32), S((t, D), F32)],
        scratch_shapes=[pltpu.VMEM((tm, MIX), BF16), pltpu.VMEM((tm + HALO, BW), F32)],
        compiler_params=_params(("arbitrary",)),
    )(x, p, p, lng, lnb, ws, bmap, wc, kv, wout, pg)


def _even_bwd(dres, o, p, lng, lnb, ws, wst, bmap, wc, kv, wout, pg):
    t = dres.shape[0]
    tm = min(TM_BWD, t)
    nt = t // tm

    def body(dres_ref, o_ref, p_ref, ph_ref, lng_ref, lnb_ref, ws_ref, wst_ref, bmap_ref, wc_ref, kv_ref, wout_ref,
             pg_ref, dp_ref, y_ref, do_ref, dpg_ref, dws_ref, dbs_ref, dlng_ref, dlnb_ref, dwc_ref, dkv_ref,
             dy, cbuf, gbuf, dconv, carry, dvn, dbmap, wacc):
        i = pl.program_id(0)
        ti = nt - 1 - i
        masks = _group_masks()

        @pl.when(i == 0)
        def _():
            for ref in (dpg_ref, dws_ref, dlng_ref, dlnb_ref, dkv_ref, dbmap, wacc):
                ref[...] = jnp.zeros_like(ref)

        do, dpg = _rms_post_bwd(dres_ref[...], o_ref[...], pg_ref[...])
        dpg_ref[...] += dpg
        do_b = do.astype(BF16)
        do_ref[...] = do_b
        dy[...] = _dot_nt(do_b, wout_ref[...])

        vh, rs = _ln_stats(p_ref[:, E_V:E_V + BW])
        vn = vh * lng_ref[...] + lnb_ref[...]
        for c in range(tm // CHUNK):
            sl = slice(c * CHUNK, (c + 1) * CHUNK)
            vn_b = vn[sl].astype(BF16)
            sg = _sgu_chunk(vn_b, ws_ref, bmap_ref, masks)
            u = p_ref[sl, E_U:E_U + BW]
            gate, dgate = _silu_parts(p_ref[sl, E_GATE:E_GATE + BW])
            dyc = dy[sl, 0:BW]
            ya = u * sg
            y_ref[sl, 0:BW] = (ya * gate).astype(BF16)
            dp_ref[sl, E_GATE:E_GATE + BW] = (dyc * ya * dgate).astype(BF16)
            dya = dyc * gate
            dp_ref[sl, E_U:E_U + BW] = (dya * sg).astype(BF16)
            dsg = dya * u
            dbmap[...] += dsg
            dsg_b = dsg.astype(BF16)
            acc = jnp.zeros((CHUNK, BW), F32)
            for h in range(NH):
                dws_ref[h] += _dot_nt((dsg * masks[h]).astype(BF16), vn_b)
                acc = acc + masks[h] * _dot(wst_ref[h], dsg_b)
            dvn[sl, :] = acc
        dn = dvn[...]
        dlng_ref[...] += _colsum(dn * vh)
        dlnb_ref[...] += _colsum(dn)
        dp_ref[:, E_V:E_V + BW] = _ln_bwd(dn, vh, rs, lng_ref[...]).astype(BF16)

        cbuf[0:HALO] = jnp.where(ti > 0, ph_ref[:, E_CG:E_CG + BW] * ph_ref[:, E_XIN:E_XIN + BW], 0.0)
        cbuf[HALO:HALO + tm] = p_ref[:, E_CG:E_CG + BW] * p_ref[:, E_XIN:E_XIN + BW]
        conv = _taps_causal(cbuf, wc_ref, 3, None)
        for rb, cv in enumerate(conv):
            sl = slice(rb * 32, (rb + 1) * 32)
            gate, dgate = _silu_parts(p_ref[sl, E_GATE + BW:E_GATE + 2 * BW])
            bg = p_ref[sl, E_BG:E_BG + BW]
            dyc = dy[sl, BW:2 * BW]
            yb = bg * cv
            y_ref[sl, BW:2 * BW] = (yb * gate).astype(BF16)
            dp_ref[sl, E_GATE + BW:E_GATE + 2 * BW] = (dyc * yb * dgate).astype(BF16)
            dyb = dyc * gate
            dp_ref[sl, E_BG:E_BG + BW] = (dyb * cv).astype(BF16)
            dconv[sl, :] = dyb * bg
        gbuf[0:tm] = dconv[...]
        gbuf[tm:tm + HALO] = jnp.where(i > 0, carry[...], 0.0)
        carry[...] = dconv[0:HALO]
        _tap_grads(dconv, cbuf, wacc, 3)
        for rb, dc in enumerate(_taps_anticausal(gbuf, wc_ref, 3)):
            sl = slice(rb * 32, (rb + 1) * 32)
            dp_ref[sl, E_CG:E_CG + BW] = (dc * p_ref[sl, E_XIN:E_XIN + BW]).astype(BF16)
            dp_ref[sl, E_XIN:E_XIN + BW] = (dc * p_ref[sl, E_CG:E_CG + BW]).astype(BF16)

        for h in range(NH):
            qs = slice(E_Q + h * HD, E_Q + (h + 1) * HD)
            ks = slice(h * HD, (h + 1) * HD)
            vs = slice(XA + h * HD, XA + (h + 1) * HD)
            gs = slice(E_GATE + 2 * BW + h * HD, E_GATE + 2 * BW + (h + 1) * HD)
            ys = slice(2 * BW + h * HD, 2 * BW + (h + 1) * HD)
            q_b = p_ref[:, qs].astype(BF16)
            prob, yx = _attn_head(q_b, kv_ref[:, ks], kv_ref[:, vs])
            gate, dgate = _silu_parts(p_ref[:, gs])
            dyc = dy[:, ys]
            y_ref[:, ys] = (yx * gate).astype(BF16)
            dp_ref[:, gs] = (dyc * yx * dgate).astype(BF16)
            dyx_b = (dyc * gate).astype(BF16)
            dprob = _dot_nt(dyx_b, kv_ref[:, vs])
            dkv_ref[:, vs] += _dot_tn(prob.astype(BF16), dyx_b)
            ds_b = (prob * (dprob - jnp.sum(dprob * prob, axis=-1, keepdims=True)) * (HD ** -0.5)).astype(BF16)
            dp_ref[:, qs] = _dot(ds_b, kv_ref[:, ks]).astype(BF16)
            dkv_ref[:, ks] += _dot_tn(ds_b, q_b)

        @pl.when(i == nt - 1)
        def _():
            for h in range(NH):
                dbs_ref[:, h * HD:(h + 1) * HD] = jnp.broadcast_to(
                    jnp.sum(dbmap[...] * masks[h], axis=-1, keepdims=True), (CHUNK, HD))
            for k in range(3):
                dwc_ref[k:k + 1, :] = _colsum(wacc[k * 8:(k + 1) * 8, :])
            dwc_ref[3:8, :] = jnp.zeros((5, BW), F32)
            causal = (lax.broadcasted_iota(jnp.int32, (CHUNK, CHUNK), 0)
                      >= lax.broadcasted_iota(jnp.int32, (CHUNK, CHUNK), 1))
            for h in range(NH):
                dws_ref[h] = jnp.where(causal, dws_ref[h], 0.0)

    rtile = lambda n: pl.BlockSpec((tm, n), lambda i: (nt - 1 - i, 0))
    outs = [S((t, EVEN_IN), BF16), S((t, MIX), BF16), S((t, D), BF16), S((1, D), F32), S((NH, CHUNK, CHUNK), F32),
            S((CHUNK, NH * HD), F32), S((1, BW), F32), S((1, BW), F32), S((8, BW), F32), S((N_MEM, 2 * XA), F32)]
    return pl.pallas_call(
        body, name="even_bwd", grid=(nt,),
        in_specs=[rtile(D), rtile(D), rtile(EVEN_IN), _halo_spec(EVEN_IN, nt, True, tm)] + [_whole()] * 9,
        out_specs=[rtile(EVEN_IN), rtile(MIX), rtile(D)] + [_full(s.shape) for s in outs[3:]],
        out_shape=outs,
        scratch_shapes=[pltpu.VMEM((tm, MIX), F32), pltpu.VMEM((tm + HALO, BW), F32), pltpu.VMEM((tm + HALO, BW), F32),
                        pltpu.VMEM((tm, BW), F32), pltpu.VMEM((HALO, BW), F32), pltpu.VMEM((tm, BW), F32),
                        pltpu.VMEM((CHUNK, BW), F32), pltpu.VMEM((3 * 8, BW), F32)],
        compiler_params=_params(("arbitrary",)),
    )(dres, o, p, p, lng, lnb, ws, wst, bmap, wc, kv, wout, pg)


def _pool_causal(za, zb, zc, zd, tm):
    n = tm + HALO
    zb[pl.ds(8, n - 8), :] = za[pl.ds(8, n - 8), :] + za[pl.ds(7, n - 8), :]
    zc[pl.ds(16, n - 16), :] = zb[pl.ds(16, n - 16), :] + zb[pl.ds(14, n - 16), :]
    zd[pl.ds(24, n - 24), :] = zc[pl.ds(24, n - 24), :] + zc[pl.ds(20, n - 24), :]
    s16 = zd[pl.ds(HALO, tm), :] + zd[pl.ds(HALO - 8, tm), :]
    return zb[pl.ds(HALO, tm), :], zc[pl.ds(HALO, tm), :], zd[pl.ds(HALO, tm), :], s16


def _pool_anticausal(ea, eb, ec, ed, tm):
    n = tm + HALO
    eb[pl.ds(0, n - 8), :] = ea[pl.ds(0, n - 8), :] + ea[pl.ds(1, n - 8), :]
    ec[pl.ds(0, n - 16), :] = eb[pl.ds(0, n - 16), :] + eb[pl.ds(2, n - 16), :]
    ed[pl.ds(0, n - 24), :] = ec[pl.ds(0, n - 24), :] + ec[pl.ds(4, n - 24), :]
    a16 = ed[pl.ds(0, tm), :] + ed[pl.ds(8, tm), :]
    return eb[pl.ds(0, tm), :], ec[pl.ds(0, tm), :], ed[pl.ds(0, tm), :], a16


def _pool_weights(ti, masks, tm):
    tf = (ti * tm + lax.broadcasted_iota(jnp.int32, (tm, 1), 0) + 1).astype(F32)
    inv = None
    for g, win in enumerate(POOL_WINDOWS):
        term = masks[g] * (1.0 / jnp.minimum(tf, float(win)))
        inv = term if inv is None else inv + term
    return inv


def _mix4(masks, parts):
    return masks[0] * parts[0] + masks[1] * parts[1] + masks[2] * parts[2] + masks[3] * parts[3]


def _odd_fwd(x1, tgt, p, wbd, cscale, dww, dwb, lng, lnb, wpw, pwb, kv, wout, pg):
    t = x1.shape[0]
    tm = min(TM_FWD, t)
    nt = t // tm

    def body(x_ref, tgt_ref, p_ref, ph_ref, wbd_ref, cs_ref, dww_ref, dwb_ref, lng_ref, lnb_ref, wpw_ref, pwb_ref,
             kv_ref, wout_ref, pg_ref, o_ref, dres_ref, loss_ref, ybuf, za, zb, zc, zd, gbuf, tmp, lacc):
        i = pl.program_id(0)
        masks = _group_masks()

        @pl.when(i == 0)
        def _():
            lacc[...] = jnp.zeros_like(lacc)

        z = p_ref[:, O_ZC:O_ZC + BW]
        za[0:HALO] = jnp.where(i > 0, ph_ref[:, O_ZC:O_ZC + BW], 0.0)
        za[HALO:HALO + tm] = z
        pooled = _mix4(masks, _pool_causal(za, zb, zc, zd, tm)) * _pool_weights(i, masks, tm) - z
        gate, _ = _silu_parts(p_ref[:, O_GATE:O_GATE + BW])
        ybuf[:, 0:BW] = (_dot(pooled.astype(BF16), wbd_ref[...]) * cs_ref[...] * gate).astype(BF16)

        gbuf[0:HALO] = jnp.where(i > 0, ph_ref[:, O_GA:O_GA + BW] * _sigmoid(ph_ref[:, O_GB:O_GB + BW]), 0.0)
        gbuf[HALO:HALO + tm] = p_ref[:, O_GA:O_GA + BW] * _sigmoid(p_ref[:, O_GB:O_GB + BW])
        for rb, blk in enumerate(_taps_causal(gbuf, dww_ref, CONF_K, dwb_ref[...])):
            tmp[rb * 32:(rb + 1) * 32, :] = blk
        zh, _ = _ln_stats(tmp[...])
        zn = zh * lng_ref[...] + lnb_ref[...]
        yd = _dot((zn * _sigmoid(zn)).astype(BF16), wpw_ref[...]) + pwb_ref[...]
        gate, _ = _silu_parts(p_ref[:, O_GATE + BW:O_GATE + 2 * BW])
        ybuf[:, BW:2 * BW] = (yd * gate).astype(BF16)

        for h in range(NH):
            qs = slice(O_Q + h * HD, O_Q + (h + 1) * HD)
            _, yx = _attn_head(p_ref[:, qs].astype(BF16), kv_ref[:, h * HD:(h + 1) * HD],
                               kv_ref[:, XA + h * HD:XA + (h + 1) * HD])
            gs = slice(O_GATE + 2 * BW + h * HD, O_GATE + 2 * BW + (h + 1) * HD)
            gate, _ = _silu_parts(p_ref[:, gs])
            ybuf[:, 2 * BW + h * HD:2 * BW + (h + 1) * HD] = (yx * gate).astype(BF16)

        o = _dot(ybuf[...], wout_ref[...])
        o_ref[...] = o
        err = _rms_residual(x_ref[...], o, pg_ref[...]) - tgt_ref[...]
        lacc[...] += _colsum(err * err)
        dres_ref[...] = err * (1.0 / D)

        @pl.when(i == nt - 1)
        def _():
            loss_ref[...] = jnp.full((1, HD), jnp.sum(lacc[...]) * (0.5 / D), F32)

    tile = lambda n: pl.BlockSpec((tm, n), lambda i: (i, 0))
    ext = pltpu.VMEM((tm + HALO, BW), F32)
    return pl.pallas_call(
        body, name="odd_fwd", grid=(nt,),
        in_specs=[tile(D), tile(D), tile(ODD_IN), _halo_spec(ODD_IN, nt, False, tm)] + [_whole()] * 11,
        out_specs=[tile(D), tile(D), _full((1, HD))],
        out_shape=[S((t, D), F32), S((t, D), F32), S((1, HD), F32)],
        scratch_shapes=[pltpu.VMEM((tm, MIX), BF16), ext, ext, ext, ext, ext, pltpu.VMEM((tm, BW), F32),
                        pltpu.VMEM((1, D), F32)],
        compiler_params=_params(("arbitrary",)),
    )(x1, tgt, p, p, wbd, cscale, dww, dwb, lng, lnb, wpw, pwb, kv, wout, pg)


def _odd_bwd(dres, o, p, wbd, cscale, dww, dwb, lng, lnb, wpw, pwb, kv, wout, pg):
    t = dres.shape[0]
    tm = min(TM_BWD, t)
    nt = t // tm

    def body(dres_ref, o_ref, p_ref, ph_ref, wbd_ref, cs_ref, dww_ref, dwb_ref, lng_ref, lnb_ref, wpw_ref, pwb_ref,
             kv_ref, wout_ref, pg_ref, dp_ref, y_ref, do_ref, dpg_ref, dwbd_ref, dcs_ref, ddww_ref, ddwb_ref,
             dlng_ref, dlnb_ref, dwpw_ref, dpwb_ref, dkv_ref,
             dy, za, zb, zc, zd, gbuf, hbuf, tmp, carry_e, carry_d, wacc):
        i = pl.program_id(0)
        ti = nt - 1 - i
        masks = _group_masks()

        @pl.when(i == 0)
        def _():
            for ref in (dpg_ref, dwbd_ref, dcs_ref, ddwb_ref, dlng_ref, dlnb_ref, dwpw_ref, dpwb_ref, dkv_ref, wacc):
                ref[...] = jnp.zeros_like(ref)

        do, dpg = _rms_post_bwd(dres_ref[...], o_ref[...], pg_ref[...])
        dpg_ref[...] += dpg
        do_b = do.astype(BF16)
        do_ref[...] = do_b
        dy[...] = _dot_nt(do_b, wout_ref[...])

        z = p_ref[:, O_ZC:O_ZC + BW]
        za[0:HALO] = jnp.where(ti > 0, ph_ref[:, O_ZC:O_ZC + BW], 0.0)
        za[HALO:HALO + tm] = z
        inv = _pool_weights(ti, masks, tm)
        pooled_b = (_mix4(masks, _pool_causal(za, zb, zc, zd, tm)) * inv - z).astype(BF16)
        pm = _dot(pooled_b, wbd_ref[...])
        gate, dgate = _silu_parts(p_ref[:, O_GATE:O_GATE + BW])
        dyc = dy[:, 0:BW]
        yc = pm * cs_ref[...]
        y_ref[:, 0:BW] = (yc * gate).astype(BF16)
        dp_ref[:, O_GATE:O_GATE + BW] = (dyc * yc * dgate).astype(BF16)
        dyc = dyc * gate
        dcs_ref[...] += _colsum(dyc * pm)
        dpm_b = (dyc * cs_ref[...]).astype(BF16)
        dwbd_ref[...] += _dot_tn(pooled_b, dpm_b)
        dpool = _dot_nt(dpm_b, wbd_ref[...])
        e = dpool * inv
        za[0:tm] = e
        za[tm:tm + HALO] = jnp.where(i > 0, carry_e[...], 0.0)
        carry_e[...] = e[0:HALO]
        dp_ref[:, O_ZC:O_ZC + BW] = (_mix4(masks, _pool_anticausal(za, zb, zc, zd, tm)) - dpool).astype(BF16)

        gbuf[0:HALO] = jnp.where(ti > 0, ph_ref[:, O_GA:O_GA + BW] * _sigmoid(ph_ref[:, O_GB:O_GB + BW]), 0.0)
        gbuf[HALO:HALO + tm] = p_ref[:, O_GA:O_GA + BW] * _sigmoid(p_ref[:, O_GB:O_GB + BW])
        for rb, blk in enumerate(_taps_causal(gbuf, dww_ref, CONF_K, dwb_ref[...])):
            tmp[rb * 32:(rb + 1) * 32, :] = blk
        zh, rs = _ln_stats(tmp[...])
        zn = zh * lng_ref[...] + lnb_ref[...]
        zs, dsilu = _silu_parts(zn)
        zs_b = zs.astype(BF16)
        yd = _dot(zs_b, wpw_ref[...]) + pwb_ref[...]
        gate, dgate = _silu_parts(p_ref[:, O_GATE + BW:O_GATE + 2 * BW])
        dyc = dy[:, BW:2 * BW]
        y_ref[:, BW:2 * BW] = (yd * gate).astype(BF16)
        dp_ref[:, O_GATE + BW:O_GATE + 2 * BW] = (dyc * yd * dgate).astype(BF16)
        dyd = dyc * gate
        dyd_b = dyd.astype(BF16)
        dpwb_ref[...] += _colsum(dyd)
        dwpw_ref[...] += _dot_tn(zs_b, dyd_b)
        dzn = _dot_nt(dyd_b, wpw_ref[...]) * dsilu
        dlng_ref[...] += _colsum(dzn * zh)
        dlnb_ref[...] += _colsum(dzn)
        dzd = _ln_bwd(dzn, zh, rs, lng_ref[...])
        ddwb_ref[...] += _colsum(dzd)
        tmp[...] = dzd
        hbuf[0:tm] = dzd
        hbuf[tm:tm + HALO] = jnp.where(i > 0, carry_d[...], 0.0)
        carry_d[...] = dzd[0:HALO]
        _tap_grads(tmp, gbuf, wacc, CONF_K)
        for rb, dzg in enumerate(_taps_anticausal(hbuf, dww_ref, CONF_K)):
            sl = slice(rb * 32, (rb + 1) * 32)
            sgb = _sigmoid(p_ref[sl, O_GB:O_GB + BW])
            dp_ref[sl, O_GA:O_GA + BW] = (dzg * sgb).astype(BF16)
            dp_ref[sl, O_GB:O_GB + BW] = (dzg * p_ref[sl, O_GA:O_GA + BW] * sgb * (1.0 - sgb)).astype(BF16)

        for h in range(NH):
            qs = slice(O_Q + h * HD, O_Q + (h + 1) * HD)
            ks = slice(h * HD, (h + 1) * HD)
            vs = slice(XA + h * HD, XA + (h + 1) * HD)
            gs = slice(O_GATE + 2 * BW + h * HD, O_GATE + 2 * BW + (h + 1) * HD)
            ys = slice(2 * BW + h * HD, 2 * BW + (h + 1) * HD)
            q_b = p_ref[:, qs].astype(BF16)
            prob, yx = _attn_head(q_b, kv_ref[:, ks], kv_ref[:, vs])
            gate, dgate = _silu_parts(p_ref[:, gs])
            dyc = dy[:, ys]
            y_ref[:, ys] = (yx * gate).astype(BF16)
            dp_ref[:, gs] = (dyc * yx * dgate).astype(BF16)
            dyx_b = (dyc * gate).astype(BF16)
            dprob = _dot_nt(dyx_b, kv_ref[:, vs])
            dkv_ref[:, vs] += _dot_tn(prob.astype(BF16), dyx_b)
            ds_b = (prob * (dprob - jnp.sum(dprob * prob, axis=-1, keepdims=True)) * (HD ** -0.5)).astype(BF16)
            dp_ref[:, qs] = _dot(ds_b, kv_ref[:, ks]).astype(BF16)
            dkv_ref[:, ks] += _dot_tn(ds_b, q_b)

        @pl.when(i == nt - 1)
        def _():
            for k in range(CONF_K):
                ddww_ref[k:k + 1, :] = _colsum(wacc[k * 8:(k + 1) * 8, :])
            ddww_ref[CONF_K:CONF_K + 1, :] = jnp.zeros((1, BW), F32)

    rtile = lambda n: pl.BlockSpec((tm, n), lambda i: (nt - 1 - i, 0))
    outs = [S((t, ODD_IN), BF16), S((t, MIX), BF16), S((t, D), BF16), S((1, D), F32), S((BW, BW), F32),
            S((1, BW), F32), S((CONF_K + 1, BW), F32), S((1, BW), F32), S((1, BW), F32), S((1, BW), F32),
            S((BW, BW), F32), S((1, BW), F32), S((N_MEM, 2 * XA), F32)]
    ext = pltpu.VMEM((tm + HALO, BW), F32)
    return pl.pallas_call(
        body, name="odd_bwd", grid=(nt,),
        in_specs=[rtile(D), rtile(D), rtile(ODD_IN), _halo_spec(ODD_IN, nt, True, tm)] + [_whole()] * 11,
        out_specs=[rtile(ODD_IN), rtile(MIX), rtile(D)] + [_full(s.shape) for s in outs[3:]],
        out_shape=outs,
        scratch_shapes=[pltpu.VMEM((tm, MIX), F32), ext, ext, ext, ext, ext, ext, pltpu.VMEM((tm, BW), F32),
                        pltpu.VMEM((HALO, BW), F32), pltpu.VMEM((HALO, BW), F32), pltpu.VMEM((CONF_K * 8, BW), F32)],
        compiler_params=_params(("arbitrary",)),
    )(dres, o, p, p, wbd, cscale, dww, dwb, lng, lnb, wpw, pwb, kv, wout, pg)


def _pick_tn(n):
    for tn in (640, 2432, 1024, 768):
        if n % tn == 0:
            return tn
    return n


def _pad_rows(a, rows):
    return jnp.pad(a, ((0, rows - a.shape[0]), (0, 0)))


def _step(x, mem, tgt, ex):
    t = x.shape[0]
    tm = min(512, t)
    w, deps = ex.first()
    causal = jnp.tril(jnp.ones((CHUNK, CHUNK), bool))
    ws = jnp.where(causal[None], w["even_a_ws"], 0.0).astype(BF16)
    wst = jnp.transpose(ws, (0, 2, 1))
    bmap = jnp.repeat(w["even_a_bs"].T, GRP, axis=1)
    wc = _pad_rows(w["even_b_conv"], 8)
    wbd = jax.scipy.linalg.block_diag(*[w["odd_c_wgrp"][g] for g in range(NH)]).astype(BF16)
    dww = _pad_rows(w["odd_d_dw_w"], CONF_K + 1)
    tk = min(1024, t)
    zeros = jnp.zeros_like(mem)

    p_e, h_e = _rms_matmul(x, w["even_pre_g"], w["even_w_in"], tm=tm, tn=_pick_tn(EVEN_IN), name="in_even",
                           deps=deps)
    w.update(ex.even_rest(h_e))
    kv_e, memn_e = _rms_matmul(mem, w["even_mem_g"], w["even_w_kv"], tm=N_MEM, tn=2 * XA, name="kv_even")
    kv_e = kv_e.astype(BF16)
    even_args = (w["even_a_ln_g"], w["even_a_ln_b"], ws)
    o_e, x1 = _even_fwd(x, p_e, *even_args, bmap, wc, kv_e, w["even_w_out"], w["even_post_g"])
    w.update(ex.odd(o_e))
    kv_o, memn_o = _rms_matmul(mem, w["odd_mem_g"], w["odd_w_kv"], tm=N_MEM, tn=2 * XA, name="kv_odd")
    kv_o = kv_o.astype(BF16)
    p_o, h_o = _rms_matmul(x1, w["odd_pre_g"], w["odd_w_in"], tm=tm, tn=_pick_tn(ODD_IN), name="in_odd")
    odd_args = (wbd, w["odd_c_scale"], dww, w["odd_d_dw_b"], w["odd_d_ln_g"], w["odd_d_ln_b"], w["odd_d_pw_w"],
                w["odd_d_pw_b"], kv_o, w["odd_w_out"], w["odd_post_g"])
    o_o, dres, loss = _odd_fwd(x1, tgt, p_o, *odd_args)

    g = {}
    (dp_o, y_o, do_o, post_g_o, dwbd, g["odd_c_scale"], ddww, g["odd_d_dw_b"], g["odd_d_ln_g"], g["odd_d_ln_b"],
     dwpw, g["odd_d_pw_b"], dkv_o) = _odd_bwd(dres, o_o, p_o, *odd_args)
    g["odd_post_g"] = post_g_o
    g["odd_d_dw_w"] = ddww[:CONF_K]
    dkv_o = dkv_o.astype(BF16)
    deps = ex.send("odd_rest", {
        "odd_w_out": _tn_matmul(y_o, do_o, tn=D, tk=tk, out_dtype=BF16, name="dw_out_odd"),
        "odd_w_kv": _tn_matmul(memn_o, dkv_o, tn=2 * XA, tk=N_MEM, out_dtype=BF16, name="dw_kv_odd"),
        "odd_d_pw_w": dwpw.astype(BF16),
        "odd_c_wgrp": jnp.concatenate([dwbd[i * GRP:(i + 1) * GRP, i * GRP:(i + 1) * GRP] for i in range(NH)])})
    deps = ex.send("odd_in", {"odd_w_in": _tn_matmul(h_o, dp_o, tn=_pick_tn(ODD_IN), tk=tk, out_dtype=BF16,
                                                     name="dw_in_odd", deps=deps)})
    dx1, g["odd_pre_g"] = _nt_matmul_rms_bwd(dp_o, w["odd_w_in"], x1, w["odd_pre_g"], dres, tm=min(256, t),
                                             name="dx_odd", deps=deps)
    _, g["odd_mem_g"] = _nt_matmul_rms_bwd(dkv_o, w["odd_w_kv"], mem, w["odd_mem_g"], zeros, tm=N_MEM,
                                           name="dmem_odd")

    (dp_e, y_e, do_e, post_g_e, dws, dbs, ln_g_e, ln_b_e, dwc, dkv_e) = _even_bwd(
        dx1, o_e, p_e, *even_args, wst, bmap, wc, kv_e, w["even_w_out"], w["even_post_g"])
    g["even_b_conv"] = dwc[:3]
    dkv_e = dkv_e.astype(BF16)
    deps = ex.send("even_rest", {
        "even_w_out": _tn_matmul(y_e, do_e, tn=D, tk=tk, out_dtype=BF16, name="dw_out_even"),
        "even_w_kv": _tn_matmul(memn_e, dkv_e, tn=2 * XA, tk=N_MEM, out_dtype=BF16, name="dw_kv_even"),
        "even_a_ln_g": ln_g_e, "even_a_ln_b": ln_b_e,
        "even_a_ws": dws.reshape(NH * CHUNK, CHUNK), "even_a_bs": dbs[:, ::HD].T})
    g["even_w_in"] = _tn_matmul(h_e, dp_e, tn=_pick_tn(EVEN_IN), tk=tk, out_dtype=BF16, name="dw_in_even", deps=deps)
    deps = ex.send("even_in", g)
    grad_x, pre_g_e = _nt_matmul_rms_bwd(dp_e, w["even_w_in"], x, w["even_pre_g"], dx1, tm=min(256, t),
                                         name="dx_even", deps=deps)
    _, mem_g_e = _nt_matmul_rms_bwd(dkv_e, w["even_w_kv"], mem, w["even_mem_g"], zeros, tm=N_MEM, name="dmem_even")
    ex.send("even_gains", {"even_pre_g": pre_g_e, "even_mem_g": mem_g_e, "even_post_g": post_g_e})
    return loss, grad_x


def _place():
    return lax.axis_index("x"), lax.axis_index("y"), lax.axis_index("c")


def _index(px, py, pc):
    return 4 * px + 2 * py + pc


_COPIES = N_DEV - 1


def _all_gather(arrs, name):
    n = len(arrs)

    def body(*refs):
        ins, outs = refs[:n], refs[n:2 * n]
        send_sems, recv_sems, local_sems = refs[2 * n:]
        x, y, c = _place()
        me, sibling = (x, y, c), (x, y, 1 - c)
        chips = [(1 - x, y), (x, 1 - y), (1 - x, 1 - y)]

        def copy(a, k, block, to, src=None):
            dst = outs[a].at[_index(*block)]
            return pltpu.make_async_remote_copy(
                src_ref=dst if src is None else src, dst_ref=dst, send_sem=send_sems.at[a * _COPIES + k],
                recv_sem=recv_sems.at[a * _COPIES + k], device_id=to, device_id_type=MESH)

        mine = [pltpu.make_async_copy(ins[a], outs[a].at[_index(*me)], local_sems.at[a]) for a in range(n)]
        first = []
        for a in range(n):
            mine[a].start()
            first.append(copy(a, 0, me, sibling, src=ins[a]))
            first += [copy(a, 1 + j, me, (*chip, c), src=ins[a]) for j, chip in enumerate(chips)]
        for cp in first:
            cp.start()
        passed = []
        for j, chip in enumerate(chips):
            for a in range(n):
                copy(a, 1 + j, (*chip, c), me).wait_recv()
                passed.append(copy(a, 4 + j, (*chip, c), sibling))
                passed[-1].start()
        for a in range(n):
            copy(a, 0, sibling, me).wait_recv()
            for j, chip in enumerate(chips):
                copy(a, 4 + j, (*chip, 1 - c), me).wait_recv()
        for cp in first + passed:
            cp.wait_send()
        for cp in mine:
            cp.wait()

    return pl.pallas_call(
        body, name=name, in_specs=[_ANY] * n, out_specs=[_ANY] * n,
        out_shape=[S((N_DEV,) + a.shape, a.dtype) for a in arrs],
        scratch_shapes=[pltpu.SemaphoreType.DMA((n * _COPIES,)), pltpu.SemaphoreType.DMA((n * _COPIES,)),
                        pltpu.SemaphoreType.DMA((n,))],
    )(*arrs)


_HBM = pl.BlockSpec(memory_space=pltpu.HBM)
_SEM = pl.BlockSpec(memory_space=pltpu.SEMAPHORE)
_EFFECT = pltpu.SideEffectType.DATAFLOW_SIDE_EFFECTING


def _exchange_copies(kinds, srcs, lands, send_sems, recv_sems, local_sems, arriving):
    x, y, c = _place()
    mine = _index(x, y, c)
    flips = [(k >> 2 & 1, k >> 1 & 1, k & 1) for k in range(1, N_DEV)]
    peers = [(1 - x if fx else x, 1 - y if fy else y, 1 - c if fc else c) for fx, fy, fc in flips]
    remote, local = [], []
    for a, kind in enumerate(kinds):
        own = srcs[a] if kind == "gather" else srcs[a].at[mine]
        local.append(pltpu.make_async_copy(own, lands[a].at[mine], local_sems.at[a]))
        for k, peer in enumerate(peers):
            there = _index(*peer)
            remote.append(pltpu.make_async_remote_copy(
                src_ref=srcs[a] if kind == "gather" else srcs[a].at[there],
                dst_ref=lands[a].at[there if arriving else mine],
                send_sem=send_sems.at[a * _COPIES + k], recv_sem=recv_sems.at[a * _COPIES + k],
                device_id=peer, device_id_type=MESH))
    return remote, local


def _exchange_start(items, name, deps=()):
    kinds = [kind for kind, _ in items]
    srcs = [a for _, a in items]
    n = len(items)
    lands = [lax.empty((N_DEV,) + (a.shape if kind == "gather" else a.shape[1:]), a.dtype) for kind, a in items]

    def body(*refs):
        send_sems, recv_sems, local_sems = refs[2 * n + len(deps):2 * n + len(deps) + 3]
        remote, local = _exchange_copies(kinds, refs[:n], refs[n:2 * n], send_sems, recv_sems, local_sems, False)
        for cp in local + remote:
            cp.start()
        refs[-1][...] = jnp.zeros_like(refs[-1])

    held = [pltpu.HBM(a.shape, a.dtype) for a in srcs + lands]
    res = pl.pallas_call(
        body, name=name,
        out_shape=(pltpu.SemaphoreType.DMA((n * _COPIES,)), pltpu.SemaphoreType.DMA((n * _COPIES,)),
                   pltpu.SemaphoreType.DMA((n,)), *held, S((8, 128), F32)),
        in_specs=[_HBM] * (2 * n) + [_ANY] * len(deps),
        out_specs=(_SEM, _SEM, _SEM, *[_HBM] * (2 * n), _whole()),
        input_output_aliases={i: 3 + i for i in range(2 * n)},
        compiler_params=pltpu.CompilerParams(has_side_effects=_EFFECT),
    )(*[pltpu.with_memory_space_constraint(a, pltpu.HBM) for a in srcs + lands], *deps)
    return (kinds, res[:3], res[3:3 + 2 * n]), res[-1]


def _exchange_wait(handle, after, name):
    kinds, sems, held = handle
    n = len(kinds)

    def body(*refs):
        send_sems, recv_sems, local_sems = refs[2 * n:2 * n + 3]
        remote, local = _exchange_copies(kinds, refs[:n], refs[n:2 * n], send_sems, recv_sems, local_sems, True)
        for cp in remote:
            cp.wait_send()
            cp.wait_recv()
        for cp in local:
            cp.wait()

    res = pl.pallas_call(
        body, name=name, out_shape=[pltpu.HBM(a.shape, a.dtype) for a in held],
        in_specs=[_HBM] * (2 * n) + [_SEM] * 3 + [_ANY] * len(after), out_specs=[_HBM] * (2 * n),
        input_output_aliases={i: i for i in range(2 * n)},
        compiler_params=pltpu.CompilerParams(has_side_effects=_EFFECT),
    )(*held, *sems, *after)
    return res[n:]


def _adamw(w, g, m, v):
    m = ADAM_B1 * m + (1.0 - ADAM_B1) * g
    v = ADAM_B2 * v + (1.0 - ADAM_B2) * (g * g)
    m_hat = m / (1.0 - ADAM_B1 ** ADAM_STEP)
    v_hat = v / (1.0 - ADAM_B2 ** ADAM_STEP)
    return -ADAM_LR * (m_hat / (jnp.sqrt(v_hat) + ADAM_EPS) + ADAM_WD * w), m, v


def _sum_devices(ref, rows):
    total = ref[0, rows, :].astype(F32)
    for s in range(1, N_DEV):
        total = total + ref[s, rows, :].astype(F32)
    return total


def _adam_big(recv, w, m, v, *, tr, name):
    r, c = w.shape

    def body(recv_ref, w_ref, m_ref, v_ref, g_ref, d_ref, m2_ref, v2_ref):
        g = _sum_devices(recv_ref, slice(None))
        g_ref[...] = g
        d_ref[...], m2_ref[...], v2_ref[...] = _adamw(w_ref[...], g, m_ref[...], v_ref[...])

    blk = pl.BlockSpec((tr, c), lambda i: (i, 0))
    return pl.pallas_call(
        body, name=name, grid=(r // tr,),
        in_specs=[pl.BlockSpec((N_DEV, tr, c), lambda i: (0, i, 0)), blk, blk, blk],
        out_specs=[blk] * 4, out_shape=[S((r, c), F32)] * 4,
        compiler_params=_params(("arbitrary",)),
    )(recv, w, m, v)


_REPLICATED = {"even_pre_g": (0, 0, 1), "even_mem_g": (0, 8, 1), "even_post_g": (0, 16, 1),
               "even_a_ln_g": (1, 0, 1), "even_a_ln_b": (1, 8, 1),
               "even_a_ws": (2, 0, NH * CHUNK), "even_a_bs": (2, NH * CHUNK, NH),
               "odd_c_wgrp": (3, 0, NH * GRP)}
_SHARDED = {"odd_pre_g": (4, 0, 1), "odd_mem_g": (4, 8, 1), "odd_post_g": (4, 16, 1),
            "even_b_conv": (5, 0, 3), "odd_c_scale": (5, 8, 1), "odd_d_dw_w": (5, 16, CONF_K),
            "odd_d_dw_b": (5, 48, 1), "odd_d_ln_g": (5, 56, 1), "odd_d_ln_b": (5, 64, 1), "odd_d_pw_b": (5, 72, 1)}
_SMALL = {**_REPLICATED, **_SHARDED}
_SMALL_ROWS = {0: 24, 1: 16, 2: NH * CHUNK + 8, 3: NH * GRP, 4: 24, 5: 80}


def _adam_small(sources, wmv):
    names = list(_SMALL)
    ns = len(sources)

    def body(*refs):
        src = refs[:ns]
        ins = refs[ns:ns + 3 * len(names)]
        outs = refs[ns + 3 * len(names):]
        for i, nm in enumerate(names):
            a, row0, rows = _SMALL[nm]
            g = _sum_devices(src[a], slice(row0, row0 + rows))
            w_ref, m_ref, v_ref = ins[3 * i:3 * i + 3]
            g_ref, d_ref, m2_ref, v2_ref = outs[4 * i:4 * i + 4]
            g_ref[...] = g
            d_ref[...], m2_ref[...], v2_ref[...] = _adamw(w_ref[...], g, m_ref[...], v_ref[...])

    flat = [t for nm in names for t in wmv[nm]]
    out_shape = [S(wmv[nm][0].shape, F32) for nm in names for _ in range(4)]
    res = pl.pallas_call(
        body, name="adam_small", in_specs=[_whole()] * (ns + len(flat)), out_specs=[_whole()] * len(out_shape),
        out_shape=out_shape, compiler_params=_params(),
    )(*sources, *flat)
    return {nm: tuple(res[4 * i:4 * i + 4]) for i, nm in enumerate(names)}


_WEIGHTS = ["even_pre_g", "even_w_in", "even_a_ln_g", "even_a_ln_b", "even_a_ws", "even_a_bs", "even_b_conv",
            "even_mem_g", "even_w_kv", "even_w_out", "even_post_g", "odd_pre_g", "odd_w_in", "odd_c_wgrp",
            "odd_c_scale", "odd_d_dw_w", "odd_d_dw_b", "odd_d_ln_g", "odd_d_ln_b", "odd_d_pw_w", "odd_d_pw_b",
            "odd_mem_g", "odd_w_kv", "odd_w_out", "odd_post_g"]
_BIG_COLS = ["even_w_in", "odd_w_in"]
_BIG_ROWS = ["even_w_kv", "even_w_out", "odd_w_kv", "odd_w_out", "odd_d_pw_w"]
_BIG = _BIG_COLS + _BIG_ROWS
_BIG_TILE_ROWS = {"even_w_in": 256, "odd_w_in": 256, "even_w_kv": 128, "even_w_out": 128, "odd_w_kv": 128,
                  "odd_w_out": 128, "odd_d_pw_w": 96}


def _view2d(a):
    a = a[0]
    if a.ndim == 1:
        return a[None]
    return a.reshape(-1, a.shape[-1])


def _rows8(a):
    return _pad_rows(a, -(-a.shape[0] // 8) * 8)


def _pack_rows(parts):
    return jnp.concatenate([_rows8(p) for p in parts], axis=0)


def _unshard_cols(a):
    return jnp.transpose(a, (1, 0, 2)).reshape(a.shape[1], N_DEV * a.shape[2])


def _shard_cols(a):
    return jnp.transpose(a.reshape(a.shape[0], N_DEV, a.shape[1] // N_DEV), (1, 0, 2))


def _rows_of(a):
    return a.reshape(-1, a.shape[-1])


_GROUPS = {"odd_rest": (["odd_w_out", "odd_w_kv", "odd_d_pw_w"], [3], []),
           "odd_in": (["odd_w_in"], [], []),
           "even_rest": (["even_w_out", "even_w_kv"], [1, 2], []),
           "even_in": (["even_w_in"], [], [4, 5]),
           "even_gains": ([], [0], [])}


class _MeshExchange:
    def __init__(self, shard):
        self.shard = shard
        self.handles = {}

    def first(self):
        shard = self.shard
        packs = [_pack_rows([shard[nm] for nm in _SHARDED if _SHARDED[nm][0] == a]) for a in (4, 5)]
        w_in, p128, p96 = _all_gather([shard["even_w_in"].astype(BF16)] + packs, "gather_first")
        w = {nm: shard[nm] for nm in _REPLICATED}
        w["even_a_ws"] = w["even_a_ws"].reshape(NH, CHUNK, CHUNK)
        w["odd_c_wgrp"] = w["odd_c_wgrp"].reshape(NH, GRP, GRP)
        w["even_w_in"] = _unshard_cols(w_in)
        full_packs = {4: _unshard_cols(p128), 5: _unshard_cols(p96)}
        for nm, (a, row0, rows) in _SHARDED.items():
            w[nm] = full_packs[a][row0:row0 + rows]
        later = lambda names: [("gather", shard[nm].astype(BF16)) for nm in names]
        self.handles["w_even"], token = _exchange_start(later(["even_w_kv", "even_w_out"]), "gather_even_start",
                                                        deps=(w_in,))
        self.handles["w_odd"], token = _exchange_start(later(["odd_w_in", "odd_w_kv", "odd_w_out", "odd_d_pw_w"]),
                                                       "gather_odd_start", deps=(token,))
        return w, (token,)

    def even_rest(self, after):
        kv, out = _exchange_wait(self.handles.pop("w_even"), (after,), "gather_even_wait")
        return {"even_w_kv": _rows_of(kv), "even_w_out": _rows_of(out)}

    def odd(self, after):
        w_in, kv, out, pw = _exchange_wait(self.handles.pop("w_odd"), (after,), "gather_odd_wait")
        return {"odd_w_in": _unshard_cols(w_in), "odd_w_kv": _rows_of(kv), "odd_w_out": _rows_of(out),
                "odd_d_pw_w": _rows_of(pw)}

    def send(self, group, g):
        big, replicated, sharded = _GROUPS[group]
        items = [("scatter", _shard_cols(g[nm]) if nm in _BIG_COLS else g[nm].reshape(N_DEV, -1, g[nm].shape[-1]))
                 for nm in big]
        items += [("gather", _pack_rows([g[nm] for nm in _REPLICATED if _REPLICATED[nm][0] == a]))
                  for a in replicated]
        items += [("scatter", _shard_cols(_pack_rows([g[nm] for nm in _SHARDED if _SHARDED[nm][0] == a])))
                  for a in sharded]
        self.handles[group], token = _exchange_start(items, "send_" + group + "_start")
        return (token,)

    def receive(self, group, after):
        return _exchange_wait(self.handles.pop(group), (after,), "send_" + group + "_wait")


def kernel(x, mem, even_pre_g, even_w_in, even_a_ln_g, even_a_ln_b, even_a_ws, even_a_bs, even_b_conv, even_mem_g, even_w_kv, even_w_out, even_post_g, odd_pre_g, odd_w_in, odd_c_wgrp, odd_c_scale, odd_d_dw_w, odd_d_dw_b, odd_d_ln_g, odd_d_ln_b, odd_d_pw_w, odd_d_pw_b, odd_mem_g, odd_w_kv, odd_w_out, odd_post_g, loss_target, m_even_pre_g, m_even_w_in, m_even_a_ln_g, m_even_a_ln_b, m_even_a_ws, m_even_a_bs, m_even_b_conv, m_even_mem_g, m_even_w_kv, m_even_w_out, m_even_post_g, m_odd_pre_g, m_odd_w_in, m_odd_c_wgrp, m_odd_c_scale, m_odd_d_dw_w, m_odd_d_dw_b, m_odd_d_ln_g, m_odd_d_ln_b, m_odd_d_pw_w, m_odd_d_pw_b, m_odd_mem_g, m_odd_w_kv, m_odd_w_out, m_odd_post_g, v_even_pre_g, v_even_w_in, v_even_a_ln_g, v_even_a_ln_b, v_even_a_ws, v_even_a_bs, v_even_b_conv, v_even_mem_g, v_even_w_kv, v_even_w_out, v_even_post_g, v_odd_pre_g, v_odd_w_in, v_odd_c_wgrp, v_odd_c_scale, v_odd_d_dw_w, v_odd_d_dw_b, v_odd_d_ln_g, v_odd_d_ln_b, v_odd_d_pw_w, v_odd_d_pw_b, v_odd_mem_g, v_odd_w_kv, v_odd_w_out, v_odd_post_g):
    given = dict(locals())
    shard = {nm: _view2d(given[nm]) for nm in _WEIGHTS}
    wmv = {nm: (shard[nm], _view2d(given["m_" + nm]), _view2d(given["v_" + nm])) for nm in _WEIGHTS}

    ex = _MeshExchange(shard)
    loss, grad_x = _step(x[0], mem[0], loss_target[0], ex)

    res = {}

    def update(group, after):
        names = _GROUPS[group][0]
        landed = ex.receive(group, after)
        for nm, recv in zip(names, landed):
            res[nm] = _adam_big(recv, *wmv[nm], tr=_BIG_TILE_ROWS[nm], name="adam_" + nm)
        return landed[len(names):]

    (c192,) = update("odd_rest", grad_x)
    update("odd_in", res["odd_d_pw_w"][0])
    c768, c128 = update("even_rest", res["odd_w_in"][0])
    a128, a96 = update("even_in", res["even_w_kv"][0])
    (c1024,) = update("even_gains", res["even_w_in"][0])
    res.update(_adam_small([c1024, c768, c128, c192, a128, a96], {nm: wmv[nm] for nm in _SMALL}))

    total = lax.psum(loss[0, 0], AXES)
    outs = [[res[nm][i].reshape(given[nm].shape) for nm in _WEIGHTS] for i in range(4)]
    return (total, grad_x[None], *outs[0], *outs[1], *outs[2], *outs[3])
```

```python
import functools

import jax
import jax.numpy as jnp
from jax import lax
from jax.experimental import pallas as pl
from jax.experimental.pallas import tpu as pltpu

F32 = jnp.float32
BF16 = jnp.bfloat16
S = jax.ShapeDtypeStruct
MESH = pl.DeviceIdType.MESH
AXES = ("x", "y", "c")
N_DEV = 8

D = 1024
BW = 768
XA = 512
HD = 128
NH = 4
MIX = 2048
CHUNK = 128
GRP = 192
N_MEM = 256
CONF_K = 31
EPS = 1e-6
HALO = 32
POOL_WINDOWS = (2, 4, 8, 16)
TM_FWD = 256
TM_BWD = 128

E_U, E_V, E_BG, E_CG, E_XIN, E_Q, E_GATE = 0, 768, 1536, 2304, 3072, 3840, 4352
EVEN_IN = 6400
O_ZC, O_GA, O_GB, O_Q, O_GATE = 0, 768, 1536, 2304, 2816
ODD_IN = 4864

ADAM_LR, ADAM_B1, ADAM_B2, ADAM_EPS, ADAM_WD, ADAM_STEP = 0.001, 0.9, 0.999, 1e-08, 0.01, 10

VMEM_LIMIT_V7X = 56 * 1024 * 1024


def _params(sem=None):
    return pltpu.CompilerParams(dimension_semantics=sem, vmem_limit_bytes=VMEM_LIMIT_V7X)


def _dot(a, b):
    return jnp.dot(a, b, preferred_element_type=F32)


def _dot_nt(a, b):
    return lax.dot_general(a, b, (((1,), (1,)), ((), ())), preferred_element_type=F32)


def _dot_tn(a, b):
    return lax.dot_general(a, b, (((0,), (0,)), ((), ())), preferred_element_type=F32)


def _sigmoid(z):
    return 1.0 / (1.0 + jnp.exp(-z))


def _rowmean(a):
    return jnp.mean(a, axis=-1, keepdims=True)


def _colsum(a):
    return jnp.sum(a, axis=0, keepdims=True)


def _ln_stats(v):
    mu = _rowmean(v)
    vc = v - mu
    rs = lax.rsqrt(_rowmean(vc * vc) + EPS)
    return vc * rs, rs


def _ln_bwd(dn, vh, rs, g):
    dvh = dn * g
    return rs * (dvh - _rowmean(dvh) - vh * _rowmean(dvh * vh))


def _group_masks():
    col = lax.broadcasted_iota(jnp.int32, (1, BW), 1)
    return [((col >= GRP * h) & (col < GRP * (h + 1))).astype(F32) for h in range(NH)]


def _full(shape):
    nd = len(shape)
    return pl.BlockSpec(shape, lambda *_: (0,) * nd)


def _whole():
    return pl.BlockSpec(memory_space=pltpu.VMEM)


_ANY = pl.BlockSpec(memory_space=pl.ANY)


def _after(body, n_in, deps):
    def ordered(*refs):
        return body(*refs[:n_in], *refs[n_in + len(deps):])
    return ordered


def _rms_matmul(x, g, w, *, tm, name, deps=()):
    t, d = x.shape
    n = w.shape[1]

    def body(x_ref, g_ref, w_ref, p_ref, h_ref):
        xv = x_ref[...]
        r = lax.rsqrt(_rowmean(xv * xv) + EPS)
        h = (xv * r * g_ref[...]).astype(BF16)
        h_ref[...] = h
        p_ref[...] = _dot(h, w_ref[...])

    return pl.pallas_call(
        _after(body, 3, deps), name=name, grid=(t // tm,),
        in_specs=[pl.BlockSpec((tm, d), lambda i: (i, 0)), _whole(), _whole()] + [_ANY] * len(deps),
        out_specs=[pl.BlockSpec((tm, n), lambda i: (i, 0)), pl.BlockSpec((tm, d), lambda i: (i, 0))],
        out_shape=[S((t, n), F32), S((t, d), BF16)],
        compiler_params=_params(("arbitrary",)),
    )(x, g, w, *deps)


def _nt_matmul_rms_bwd(dp, w, x, g, dres, *, tm, name, deps=()):
    t, n = dp.shape
    d = x.shape[1]

    def body(dp_ref, w_ref, x_ref, g_ref, dres_ref, dx_ref, dg_ref):
        @pl.when(pl.program_id(0) == 0)
        def _():
            dg_ref[...] = jnp.zeros_like(dg_ref)

        dh = _dot_nt(dp_ref[...], w_ref[...])
        xv = x_ref[...]
        r = lax.rsqrt(_rowmean(xv * xv) + EPS)
        xh = xv * r
        dg_ref[...] += _colsum(dh * xh)
        dxh = dh * g_ref[...]
        dx_ref[...] = dres_ref[...] + r * (dxh - xh * _rowmean(dxh * xh))

    return pl.pallas_call(
        _after(body, 5, deps), name=name, grid=(t // tm,),
        in_specs=[pl.BlockSpec((tm, n), lambda i: (i, 0)), _whole(), pl.BlockSpec((tm, d), lambda i: (i, 0)),
                  _whole(), pl.BlockSpec((tm, d), lambda i: (i, 0))] + [_ANY] * len(deps),
        out_specs=[pl.BlockSpec((tm, d), lambda i: (i, 0)), pl.BlockSpec((1, d), lambda i: (0, 0))],
        out_shape=[S((t, d), F32), S((1, d), F32)],
        compiler_params=_params(("arbitrary",)),
    )(dp, w, x, g, dres, *deps)


def _tn_matmul(a, b, *, tn, tk, out_dtype, name, deps=()):
    t, m = a.shape
    n = b.shape[1]
    nk = t // tk

    def body(a_ref, b_ref, o_ref, acc_ref):
        k = pl.program_id(1)

        @pl.when(k == 0)
        def _():
            acc_ref[...] = jnp.zeros_like(acc_ref)

        acc_ref[...] += _dot_tn(a_ref[...], b_ref[...])

        @pl.when(k == nk - 1)
        def _():
            o_ref[...] = acc_ref[...].astype(out_dtype)

    return pl.pallas_call(
        _after(body, 2, deps), name=name, grid=(n // tn, nk),
        in_specs=[pl.BlockSpec((tk, m), lambda j, k: (k, 0)), pl.BlockSpec((tk, tn), lambda j, k: (k, j))]
        + [_ANY] * len(deps),
        out_specs=pl.BlockSpec((m, tn), lambda j, k: (0, j)),
        out_shape=S((m, n), out_dtype),
        scratch_shapes=[pltpu.VMEM((m, tn), F32)],
        compiler_params=_params(("arbitrary", "arbitrary")),
    )(a, b, *deps)


def _silu_parts(gt):
    sg = _sigmoid(gt)
    return gt * sg, sg * (1.0 + gt * (1.0 - sg))


def _attn_head(q_b, k_b, v_b):
    s = _dot_nt(q_b, k_b) * (HD ** -0.5)
    e = jnp.exp(s - jnp.max(s, axis=-1, keepdims=True))
    prob = e / jnp.sum(e, axis=-1, keepdims=True)
    return prob, _dot(prob.astype(BF16), v_b)


def _rms_residual(x, o, g):
    r = lax.rsqrt(_rowmean(o * o) + EPS)
    return x + o * r * g


def _rms_post_bwd(dres, o, g):
    r = lax.rsqrt(_rowmean(o * o) + EPS)
    oh = o * r
    doh = dres * g
    return r * (doh - oh * _rowmean(doh * oh)), _colsum(dres * oh)


def _sgu_chunk(vn_b, ws_ref, bmap_ref, masks):
    sg = bmap_ref[...]
    for h in range(NH):
        sg = sg + masks[h] * _dot(ws_ref[h], vn_b)
    return sg


def _shift_copies(buf, sh):
    n = buf.shape[0] - 8
    for b in range(1, 8):
        sh[b - 1, pl.ds(0, n), :] = buf[pl.ds(b, n), :]


def _rows_at(buf, sh, off):
    b = off % 8
    if b == 0 or sh is None:
        return buf[pl.ds(off, 32), :]
    return sh[b - 1, pl.ds(off - b, 32), :]


def _taps_causal(buf, sh, w_ref, taps, bias):
    rows = buf.shape[0] - HALO
    outs = []
    for rb in range(rows // 32):
        acc = None
        for k in range(taps):
            term = w_ref[k:k + 1, :] * _rows_at(buf, sh, rb * 32 + HALO - (taps - 1 - k))
            acc = term if acc is None else acc + term
        outs.append(acc if bias is None else acc + bias)
    return outs


def _taps_anticausal(buf, sh, w_ref, taps):
    rows = buf.shape[0] - HALO
    outs = []
    for rb in range(rows // 32):
        acc = None
        for k in range(taps):
            term = w_ref[k:k + 1, :] * _rows_at(buf, sh, rb * 32 + (taps - 1 - k))
            acc = term if acc is None else acc + term
        outs.append(acc)
    return outs


def _fold8(a):
    return a[0:8] + a[8:16] + a[16:24] + a[24:32]


def _tap_grads(d_ref, buf, sh, acc_ref, taps):
    rows = buf.shape[0] - HALO
    for rb in range(rows // 32):
        dv = d_ref[rb * 32:(rb + 1) * 32, :]
        for k in range(taps):
            prod = dv * _rows_at(buf, sh, rb * 32 + HALO - (taps - 1 - k))
            acc_ref[k * 8:(k + 1) * 8, :] += _fold8(prod)


def _halo_spec(n, nt, reverse, tm):
    per = tm // HALO
    if reverse:
        return pl.BlockSpec((HALO, n), lambda i: (jnp.maximum((nt - 1 - i) * per - 1, 0), 0))
    return pl.BlockSpec((HALO, n), lambda i: (jnp.maximum(i * per - 1, 0), 0))


def _even_fwd(x, p, lng, lnb, ws, bmap, wc, kv, wout, pg):
    t = x.shape[0]
    tm = min(TM_FWD, t)
    nt = t // tm

    def body(x_ref, p_ref, ph_ref, lng_ref, lnb_ref, ws_ref, bmap_ref, wc_ref, kv_ref, wout_ref, pg_ref,
             o_ref, x1_ref, ybuf, cbuf):
        i = pl.program_id(0)
        masks = _group_masks()
        vh, _ = _ln_stats(p_ref[:, E_V:E_V + BW])
        vn = vh * lng_ref[...] + lnb_ref[...]
        for c in range(tm // CHUNK):
            sl = slice(c * CHUNK, (c + 1) * CHUNK)
            sg = _sgu_chunk(vn[sl].astype(BF16), ws_ref, bmap_ref, masks)
            gate, _ = _silu_parts(p_ref[sl, E_GATE:E_GATE + BW])
            ybuf[sl, 0:BW] = (p_ref[sl, E_U:E_U + BW] * sg * gate).astype(BF16)

        cbuf[0:HALO] = jnp.where(i > 0, ph_ref[:, E_CG:E_CG + BW] * ph_ref[:, E_XIN:E_XIN + BW], 0.0)
        cbuf[HALO:HALO + tm] = p_ref[:, E_CG:E_CG + BW] * p_ref[:, E_XIN:E_XIN + BW]
        conv = _taps_causal(cbuf, None, wc_ref, 3, None)
        for rb, cv in enumerate(conv):
            sl = slice(rb * 32, (rb + 1) * 32)
            gate, _ = _silu_parts(p_ref[sl, E_GATE + BW:E_GATE + 2 * BW])
            ybuf[sl, BW:2 * BW] = (p_ref[sl, E_BG:E_BG + BW] * cv * gate).astype(BF16)

        for h in range(NH):
            qs = slice(E_Q + h * HD, E_Q + (h + 1) * HD)
            _, yx = _attn_head(p_ref[:, qs].astype(BF16), kv_ref[:, h * HD:(h + 1) * HD],
                               kv_ref[:, XA + h * HD:XA + (h + 1) * HD])
            gs = slice(E_GATE + 2 * BW + h * HD, E_GATE + 2 * BW + (h + 1) * HD)
            gate, _ = _silu_parts(p_ref[:, gs])
            ybuf[:, 2 * BW + h * HD:2 * BW + (h + 1) * HD] = (yx * gate).astype(BF16)

        o = _dot(ybuf[...], wout_ref[...])
        o_ref[...] = o
        x1_ref[...] = _rms_residual(x_ref[...], o, pg_ref[...])

    tile = lambda n: pl.BlockSpec((tm, n), lambda i: (i, 0))
    return pl.pallas_call(
        body, name="even_fwd", grid=(nt,),
        in_specs=[tile(D), tile(EVEN_IN), _halo_spec(EVEN_IN, nt, False, tm)] + [_whole()] * 8,
        out_specs=[tile(D), tile(D)],
        out_shape=[S((t, D), F32), S((t, D), F32)],
        scratch_shapes=[pltpu.VMEM((tm, MIX), BF16), pltpu.VMEM((tm + HALO, BW), F32)],
        compiler_params=_params(("arbitrary",)),
    )(x, p, p, lng, lnb, ws, bmap, wc, kv, wout, pg)


def _even_bwd(dres, o, p, lng, lnb, ws, wst, bmap, wc, kv, wout, pg):
    t = dres.shape[0]
    tm = min(TM_BWD, t)
    nt = t // tm

    def body(dres_ref, o_ref, p_ref, ph_ref, lng_ref, lnb_ref, ws_ref, wst_ref, bmap_ref, wc_ref, kv_ref, wout_ref,
             pg_ref, dp_ref, y_ref, do_ref, dpg_ref, dws_ref, dbs_ref, dlng_ref, dlnb_ref, dwc_ref, dkv_ref,
             dy, cbuf, gbuf, dconv, carry, dvn, dbmap, wacc):
        i = pl.program_id(0)
        ti = nt - 1 - i
        masks = _group_masks()

        @pl.when(i == 0)
        def _():
            for ref in (dpg_ref, dws_ref, dlng_ref, dlnb_ref, dkv_ref, dbmap, wacc):
                ref[...] = jnp.zeros_like(ref)

        do, dpg = _rms_post_bwd(dres_ref[...], o_ref[...], pg_ref[...])
        dpg_ref[...] += dpg
        do_b = do.astype(BF16)
        do_ref[...] = do_b
        dy[...] = _dot_nt(do_b, wout_ref[...])

        vh, rs = _ln_stats(p_ref[:, E_V:E_V + BW])
        vn = vh * lng_ref[...] + lnb_ref[...]
        for c in range(tm // CHUNK):
            sl = slice(c * CHUNK, (c + 1) * CHUNK)
            vn_b = vn[sl].astype(BF16)
            sg = _sgu_chunk(vn_b, ws_ref, bmap_ref, masks)
            u = p_ref[sl, E_U:E_U + BW]
            gate, dgate = _silu_parts(p_ref[sl, E_GATE:E_GATE + BW])
            dyc = dy[sl, 0:BW]
            ya = u * sg
            y_ref[sl, 0:BW] = (ya * gate).astype(BF16)
            dp_ref[sl, E_GATE:E_GATE + BW] = (dyc * ya * dgate).astype(BF16)
            dya = dyc * gate
            dp_ref[sl, E_U:E_U + BW] = (dya * sg).astype(BF16)
            dsg = dya * u
            dbmap[...] += dsg
            dsg_b = dsg.astype(BF16)
            acc = jnp.zeros((CHUNK, BW), F32)
            for h in range(NH):
                dws_ref[h] += _dot_nt((dsg * masks[h]).astype(BF16), vn_b)
                acc = acc + masks[h] * _dot(wst_ref[h], dsg_b)
            dvn[sl, :] = acc
        dn = dvn[...]
        dlng_ref[...] += _colsum(dn * vh)
        dlnb_ref[...] += _colsum(dn)
        dp_ref[:, E_V:E_V + BW] = _ln_bwd(dn, vh, rs, lng_ref[...]).astype(BF16)

        cbuf[0:HALO] = jnp.where(ti > 0, ph_ref[:, E_CG:E_CG + BW] * ph_ref[:, E_XIN:E_XIN + BW], 0.0)
        cbuf[HALO:HALO + tm] = p_ref[:, E_CG:E_CG + BW] * p_ref[:, E_XIN:E_XIN + BW]
        conv = _taps_causal(cbuf, None, wc_ref, 3, None)
        for rb, cv in enumerate(conv):
            sl = slice(rb * 32, (rb + 1) * 32)
            gate, dgate = _silu_parts(p_ref[sl, E_GATE + BW:E_GATE + 2 * BW])
            bg = p_ref[sl, E_BG:E_BG + BW]
            dyc = dy[sl, BW:2 * BW]
            yb = bg * cv
            y_ref[sl, BW:2 * BW] = (yb * gate).astype(BF16)
            dp_ref[sl, E_GATE + BW:E_GATE + 2 * BW] = (dyc * yb * dgate).astype(BF16)
            dyb = dyc * gate
            dp_ref[sl, E_BG:E_BG + BW] = (dyb * cv).astype(BF16)
            dconv[sl, :] = dyb * bg
        gbuf[0:tm] = dconv[...]
        gbuf[tm:tm + HALO] = jnp.where(i > 0, carry[...], 0.0)
        carry[...] = dconv[0:HALO]
        _tap_grads(dconv, cbuf, None, wacc, 3)
        for rb, dc in enumerate(_taps_anticausal(gbuf, None, wc_ref, 3)):
            sl = slice(rb * 32, (rb + 1) * 32)
            dp_ref[sl, E_CG:E_CG + BW] = (dc * p_ref[sl, E_XIN:E_XIN + BW]).astype(BF16)
            dp_ref[sl, E_XIN:E_XIN + BW] = (dc * p_ref[sl, E_CG:E_CG + BW]).astype(BF16)

        for h in range(NH):
            qs = slice(E_Q + h * HD, E_Q + (h + 1) * HD)
            ks = slice(h * HD, (h + 1) * HD)
            vs = slice(XA + h * HD, XA + (h + 1) * HD)
            gs = slice(E_GATE + 2 * BW + h * HD, E_GATE + 2 * BW + (h + 1) * HD)
            ys = slice(2 * BW + h * HD, 2 * BW + (h + 1) * HD)
            q_b = p_ref[:, qs].astype(BF16)
            prob, yx = _attn_head(q_b, kv_ref[:, ks], kv_ref[:, vs])
            gate, dgate = _silu_parts(p_ref[:, gs])
            dyc = dy[:, ys]
            y_ref[:, ys] = (yx * gate).astype(BF16)
            dp_ref[:, gs] = (dyc * yx * dgate).astype(BF16)
            dyx_b = (dyc * gate).astype(BF16)
            dprob = _dot_nt(dyx_b, kv_ref[:, vs])
            dkv_ref[:, vs] += _dot_tn(prob.astype(BF16), dyx_b)
            ds_b = (prob * (dprob - jnp.sum(dprob * prob, axis=-1, keepdims=True)) * (HD ** -0.5)).astype(BF16)
            dp_ref[:, qs] = _dot(ds_b, kv_ref[:, ks]).astype(BF16)
            dkv_ref[:, ks] += _dot_tn(ds_b, q_b)

        @pl.when(i == nt - 1)
        def _():
            for h in range(NH):
                dbs_ref[:, h * HD:(h + 1) * HD] = jnp.broadcast_to(
                    jnp.sum(dbmap[...] * masks[h], axis=-1, keepdims=True), (CHUNK, HD))
            for k in range(3):
                dwc_ref[k:k + 1, :] = _colsum(wacc[k * 8:(k + 1) * 8, :])
            dwc_ref[3:8, :] = jnp.zeros((5, BW), F32)
            causal = (lax.broadcasted_iota(jnp.int32, (CHUNK, CHUNK), 0)
                      >= lax.broadcasted_iota(jnp.int32, (CHUNK, CHUNK), 1))
            for h in range(NH):
                dws_ref[h] = jnp.where(causal, dws_ref[h], 0.0)

    rtile = lambda n: pl.BlockSpec((tm, n), lambda i: (nt - 1 - i, 0))
    outs = [S((t, EVEN_IN), BF16), S((t, MIX), BF16), S((t, D), BF16), S((1, D), F32), S((NH, CHUNK, CHUNK), F32),
            S((CHUNK, NH * HD), F32), S((1, BW), F32), S((1, BW), F32), S((8, BW), F32), S((N_MEM, 2 * XA), F32)]
    return pl.pallas_call(
        body, name="even_bwd", grid=(nt,),
        in_specs=[rtile(D), rtile(D), rtile(EVEN_IN), _halo_spec(EVEN_IN, nt, True, tm)] + [_whole()] * 9,
        out_specs=[rtile(EVEN_IN), rtile(MIX), rtile(D)] + [_full(s.shape) for s in outs[3:]],
        out_shape=outs,
        scratch_shapes=[pltpu.VMEM((tm, MIX), F32), pltpu.VMEM((tm + HALO, BW), F32), pltpu.VMEM((tm + HALO, BW), F32),
                        pltpu.VMEM((tm, BW), F32), pltpu.VMEM((HALO, BW), F32), pltpu.VMEM((tm, BW), F32),
                        pltpu.VMEM((CHUNK, BW), F32), pltpu.VMEM((3 * 8, BW), F32)],
        compiler_params=_params(("arbitrary",)),
    )(dres, o, p, p, lng, lnb, ws, wst, bmap, wc, kv, wout, pg)


def _pool_causal(za, zb, zc, zd, tm):
    n = tm + HALO
    zb[pl.ds(8, n - 8), :] = za[pl.ds(8, n - 8), :] + za[pl.ds(7, n - 8), :]
    zc[pl.ds(16, n - 16), :] = zb[pl.ds(16, n - 16), :] + zb[pl.ds(14, n - 16), :]
    zd[pl.ds(24, n - 24), :] = zc[pl.ds(24, n - 24), :] + zc[pl.ds(20, n - 24), :]
    s16 = zd[pl.ds(HALO, tm), :] + zd[pl.ds(HALO - 8, tm), :]
    return zb[pl.ds(HALO, tm), :], zc[pl.ds(HALO, tm), :], zd[pl.ds(HALO, tm), :], s16


def _pool_anticausal(ea, eb, ec, ed, tm):
    n = tm + HALO
    eb[pl.ds(0, n - 8), :] = ea[pl.ds(0, n - 8), :] + ea[pl.ds(1, n - 8), :]
    ec[pl.ds(0, n - 16), :] = eb[pl.ds(0, n - 16), :] + eb[pl.ds(2, n - 16), :]
    ed[pl.ds(0, n - 24), :] = ec[pl.ds(0, n - 24), :] + ec[pl.ds(4, n - 24), :]
    a16 = ed[pl.ds(0, tm), :] + ed[pl.ds(8, tm), :]
    return eb[pl.ds(0, tm), :], ec[pl.ds(0, tm), :], ed[pl.ds(0, tm), :], a16


def _pool_weights(ti, masks, tm):
    tf = (ti * tm + lax.broadcasted_iota(jnp.int32, (tm, 1), 0) + 1).astype(F32)
    inv = None
    for g, win in enumerate(POOL_WINDOWS):
        term = masks[g] * (1.0 / jnp.minimum(tf, float(win)))
        inv = term if inv is None else inv + term
    return inv


def _mix4(masks, parts):
    return masks[0] * parts[0] + masks[1] * parts[1] + masks[2] * parts[2] + masks[3] * parts[3]


def _odd_fwd(x1, tgt, p, wbd, cscale, dww, dwb, lng, lnb, wpw, pwb, kv, wout, pg):
    t = x1.shape[0]
    tm = min(TM_FWD, t)
    nt = t // tm

    def body(x_ref, tgt_ref, p_ref, ph_ref, wbd_ref, cs_ref, dww_ref, dwb_ref, lng_ref, lnb_ref, wpw_ref, pwb_ref,
             kv_ref, wout_ref, pg_ref, o_ref, dres_ref, loss_ref, ybuf, za, zb, zc, zd, gbuf, tmp, lacc, gsh):
        i = pl.program_id(0)
        masks = _group_masks()

        @pl.when(i == 0)
        def _():
            lacc[...] = jnp.zeros_like(lacc)

        z = p_ref[:, O_ZC:O_ZC + BW]
        za[0:HALO] = jnp.where(i > 0, ph_ref[:, O_ZC:O_ZC + BW], 0.0)
        za[HALO:HALO + tm] = z
        pooled = _mix4(masks, _pool_causal(za, zb, zc, zd, tm)) * _pool_weights(i, masks, tm) - z
        gate, _ = _silu_parts(p_ref[:, O_GATE:O_GATE + BW])
        ybuf[:, 0:BW] = (_dot(pooled.astype(BF16), wbd_ref[...]) * cs_ref[...] * gate).astype(BF16)

        gbuf[0:HALO] = jnp.where(i > 0, ph_ref[:, O_GA:O_GA + BW] * _sigmoid(ph_ref[:, O_GB:O_GB + BW]), 0.0)
        gbuf[HALO:HALO + tm] = p_ref[:, O_GA:O_GA + BW] * _sigmoid(p_ref[:, O_GB:O_GB + BW])
        _shift_copies(gbuf, gsh)
        for rb, blk in enumerate(_taps_causal(gbuf, gsh, dww_ref, CONF_K, dwb_ref[...])):
            tmp[rb * 32:(rb + 1) * 32, :] = blk
        zh, _ = _ln_stats(tmp[...])
        zn = zh * lng_ref[...] + lnb_ref[...]
        yd = _dot((zn * _sigmoid(zn)).astype(BF16), wpw_ref[...]) + pwb_ref[...]
        gate, _ = _silu_parts(p_ref[:, O_GATE + BW:O_GATE + 2 * BW])
        ybuf[:, BW:2 * BW] = (yd * gate).astype(BF16)

        for h in range(NH):
            qs = slice(O_Q + h * HD, O_Q + (h + 1) * HD)
            _, yx = _attn_head(p_ref[:, qs].astype(BF16), kv_ref[:, h * HD:(h + 1) * HD],
                               kv_ref[:, XA + h * HD:XA + (h + 1) * HD])
            gs = slice(O_GATE + 2 * BW + h * HD, O_GATE + 2 * BW + (h + 1) * HD)
            gate, _ = _silu_parts(p_ref[:, gs])
            ybuf[:, 2 * BW + h * HD:2 * BW + (h + 1) * HD] = (yx * gate).astype(BF16)

        o = _dot(ybuf[...], wout_ref[...])
        o_ref[...] = o
        err = _rms_residual(x_ref[...], o, pg_ref[...]) - tgt_ref[...]
        lacc[...] += _colsum(err * err)
        dres_ref[...] = err * (1.0 / D)

        @pl.when(i == nt - 1)
        def _():
            loss_ref[...] = jnp.full((1, HD), jnp.sum(lacc[...]) * (0.5 / D), F32)

    tile = lambda n: pl.BlockSpec((tm, n), lambda i: (i, 0))
    ext = pltpu.VMEM((tm + HALO, BW), F32)
    return pl.pallas_call(
        body, name="odd_fwd", grid=(nt,),
        in_specs=[tile(D), tile(D), tile(ODD_IN), _halo_spec(ODD_IN, nt, False, tm)] + [_whole()] * 11,
        out_specs=[tile(D), tile(D), _full((1, HD))],
        out_shape=[S((t, D), F32), S((t, D), F32), S((1, HD), F32)],
        scratch_shapes=[pltpu.VMEM((tm, MIX), BF16), ext, ext, ext, ext, ext, pltpu.VMEM((tm, BW), F32),
                        pltpu.VMEM((1, D), F32), pltpu.VMEM((7, tm + HALO, BW), F32)],
        compiler_params=_params(("arbitrary",)),
    )(x1, tgt, p, p, wbd, cscale, dww, dwb, lng, lnb, wpw, pwb, kv, wout, pg)


def _odd_bwd(dres, o, p, wbd, cscale, dww, dwb, lng, lnb, wpw, pwb, kv, wout, pg):
    t = dres.shape[0]
    tm = min(TM_BWD, t)
    nt = t // tm

    def body(dres_ref, o_ref, p_ref, ph_ref, wbd_ref, cs_ref, dww_ref, dwb_ref, lng_ref, lnb_ref, wpw_ref, pwb_ref,
             kv_ref, wout_ref, pg_ref, dp_ref, y_ref, do_ref, dpg_ref, dwbd_ref, dcs_ref, ddww_ref, ddwb_ref,
             dlng_ref, dlnb_ref, dwpw_ref, dpwb_ref, dkv_ref,
             dy, za, zb, zc, zd, gbuf, hbuf, tmp, carry_e, carry_d, wacc, gsh, hsh):
        i = pl.program_id(0)
        ti = nt - 1 - i
        masks = _group_masks()

        @pl.when(i == 0)
        def _():
            for ref in (dpg_ref, dwbd_ref, dcs_ref, ddwb_ref, dlng_ref, dlnb_ref, dwpw_ref, dpwb_ref, dkv_ref, wacc):
                ref[...] = jnp.zeros_like(ref)

        do, dpg = _rms_post_bwd(dres_ref[...], o_ref[...], pg_ref[...])
        dpg_ref[...] += dpg
        do_b = do.astype(BF16)
        do_ref[...] = do_b
        dy[...] = _dot_nt(do_b, wout_ref[...])

        z = p_ref[:, O_ZC:O_ZC + BW]
        za[0:HALO] = jnp.where(ti > 0, ph_ref[:, O_ZC:O_ZC + BW], 0.0)
        za[HALO:HALO + tm] = z
        inv = _pool_weights(ti, masks, tm)
        pooled_b = (_mix4(masks, _pool_causal(za, zb, zc, zd, tm)) * inv - z).astype(BF16)
        pm = _dot(pooled_b, wbd_ref[...])
        gate, dgate = _silu_parts(p_ref[:, O_GATE:O_GATE + BW])
        dyc = dy[:, 0:BW]
        yc = pm * cs_ref[...]
        y_ref[:, 0:BW] = (yc * gate).astype(BF16)
        dp_ref[:, O_GATE:O_GATE + BW] = (dyc * yc * dgate).astype(BF16)
        dyc = dyc * gate
        dcs_ref[...] += _colsum(dyc * pm)
        dpm_b = (dyc * cs_ref[...]).astype(BF16)
        dwbd_ref[...] += _dot_tn(pooled_b, dpm_b)
        dpool = _dot_nt(dpm_b, wbd_ref[...])
        e = dpool * inv
        za[0:tm] = e
        za[tm:tm + HALO] = jnp.where(i > 0, carry_e[...], 0.0)
        carry_e[...] = e[0:HALO]
        dp_ref[:, O_ZC:O_ZC + BW] = (_mix4(masks, _pool_anticausal(za, zb, zc, zd, tm)) - dpool).astype(BF16)

        gbuf[0:HALO] = jnp.where(ti > 0, ph_ref[:, O_GA:O_GA + BW] * _sigmoid(ph_ref[:, O_GB:O_GB + BW]), 0.0)
        gbuf[HALO:HALO + tm] = p_ref[:, O_GA:O_GA + BW] * _sigmoid(p_ref[:, O_GB:O_GB + BW])
        _shift_copies(gbuf, gsh)
        for rb, blk in enumerate(_taps_causal(gbuf, gsh, dww_ref, CONF_K, dwb_ref[...])):
            tmp[rb * 32:(rb + 1) * 32, :] = blk
        zh, rs = _ln_stats(tmp[...])
        zn = zh * lng_ref[...] + lnb_ref[...]
        zs, dsilu = _silu_parts(zn)
        zs_b = zs.astype(BF16)
        yd = _dot(zs_b, wpw_ref[...]) + pwb_ref[...]
        gate, dgate = _silu_parts(p_ref[:, O_GATE + BW:O_GATE + 2 * BW])
        dyc = dy[:, BW:2 * BW]
        y_ref[:, BW:2 * BW] = (yd * gate).astype(BF16)
        dp_ref[:, O_GATE + BW:O_GATE + 2 * BW] = (dyc * yd * dgate).astype(BF16)
        dyd = dyc * gate
        dyd_b = dyd.astype(BF16)
        dpwb_ref[...] += _colsum(dyd)
        dwpw_ref[...] += _dot_tn(zs_b, dyd_b)
        dzn = _dot_nt(dyd_b, wpw_ref[...]) * dsilu
        dlng_ref[...] += _colsum(dzn * zh)
        dlnb_ref[...] += _colsum(dzn)
        dzd = _ln_bwd(dzn, zh, rs, lng_ref[...])
        ddwb_ref[...] += _colsum(dzd)
        tmp[...] = dzd
        hbuf[0:tm] = dzd
        hbuf[tm:tm + HALO] = jnp.where(i > 0, carry_d[...], 0.0)
        carry_d[...] = dzd[0:HALO]
        _shift_copies(hbuf, hsh)
        _tap_grads(tmp, gbuf, gsh, wacc, CONF_K)
        for rb, dzg in enumerate(_taps_anticausal(hbuf, hsh, dww_ref, CONF_K)):
            sl = slice(rb * 32, (rb + 1) * 32)
            sgb = _sigmoid(p_ref[sl, O_GB:O_GB + BW])
            dp_ref[sl, O_GA:O_GA + BW] = (dzg * sgb).astype(BF16)
            dp_ref[sl, O_GB:O_GB + BW] = (dzg * p_ref[sl, O_GA:O_GA + BW] * sgb * (1.0 - sgb)).astype(BF16)

        for h in range(NH):
            qs = slice(O_Q + h * HD, O_Q + (h + 1) * HD)
            ks = slice(h * HD, (h + 1) * HD)
            vs = slice(XA + h * HD, XA + (h + 1) * HD)
            gs = slice(O_GATE + 2 * BW + h * HD, O_GATE + 2 * BW + (h + 1) * HD)
            ys = slice(2 * BW + h * HD, 2 * BW + (h + 1) * HD)
            q_b = p_ref[:, qs].astype(BF16)
            prob, yx = _attn_head(q_b, kv_ref[:, ks], kv_ref[:, vs])
            gate, dgate = _silu_parts(p_ref[:, gs])
            dyc = dy[:, ys]
            y_ref[:, ys] = (yx * gate).astype(BF16)
            dp_ref[:, gs] = (dyc * yx * dgate).astype(BF16)
            dyx_b = (dyc * gate).astype(BF16)
            dprob = _dot_nt(dyx_b, kv_ref[:, vs])
            dkv_ref[:, vs] += _dot_tn(prob.astype(BF16), dyx_b)
            ds_b = (prob * (dprob - jnp.sum(dprob * prob, axis=-1, keepdims=True)) * (HD ** -0.5)).astype(BF16)
            dp_ref[:, qs] = _dot(ds_b, kv_ref[:, ks]).astype(BF16)
            dkv_ref[:, ks] += _dot_tn(ds_b, q_b)

        @pl.when(i == nt - 1)
        def _():
            for k in range(CONF_K):
                ddww_ref[k:k + 1, :] = _colsum(wacc[k * 8:(k + 1) * 8, :])
            ddww_ref[CONF_K:CONF_K + 1, :] = jnp.zeros((1, BW), F32)

    rtile = lambda n: pl.BlockSpec((tm, n), lambda i: (nt - 1 - i, 0))
    outs = [S((t, ODD_IN), BF16), S((t, MIX), BF16), S((t, D), BF16), S((1, D), F32), S((BW, BW), F32),
            S((1, BW), F32), S((CONF_K + 1, BW), F32), S((1, BW), F32), S((1, BW), F32), S((1, BW), F32),
            S((BW, BW), F32), S((1, BW), F32), S((N_MEM, 2 * XA), F32)]
    ext = pltpu.VMEM((tm + HALO, BW), F32)
    return pl.pallas_call(
        body, name="odd_bwd", grid=(nt,),
        in_specs=[rtile(D), rtile(D), rtile(ODD_IN), _halo_spec(ODD_IN, nt, True, tm)] + [_whole()] * 11,
        out_specs=[rtile(ODD_IN), rtile(MIX), rtile(D)] + [_full(s.shape) for s in outs[3:]],
        out_shape=outs,
        scratch_shapes=[pltpu.VMEM((tm, MIX), F32), ext, ext, ext, ext, ext, ext, pltpu.VMEM((tm, BW), F32),
                        pltpu.VMEM((HALO, BW), F32), pltpu.VMEM((HALO, BW), F32), pltpu.VMEM((CONF_K * 8, BW), F32),
                        pltpu.VMEM((7, tm + HALO, BW), F32), pltpu.VMEM((7, tm + HALO, BW), F32)],
        compiler_params=_params(("arbitrary",)),
    )(dres, o, p, p, wbd, cscale, dww, dwb, lng, lnb, wpw, pwb, kv, wout, pg)


def _pick_tn(n):
    for tn in (640, 2432, 1024, 768):
        if n % tn == 0:
            return tn
    return n


def _pad_rows(a, rows):
    return jnp.pad(a, ((0, rows - a.shape[0]), (0, 0)))


def _step(x, mem, tgt, ex):
    t = x.shape[0]
    tm = min(256, t)
    w, deps = ex.first()
    causal = jnp.tril(jnp.ones((CHUNK, CHUNK), bool))
    ws = jnp.where(causal[None], w["even_a_ws"], 0.0).astype(BF16)
    wst = jnp.transpose(ws, (0, 2, 1))
    bmap = jnp.repeat(w["even_a_bs"].T, GRP, axis=1)
    wc = _pad_rows(w["even_b_conv"], 8)
    wbd = jax.scipy.linalg.block_diag(*[w["odd_c_wgrp"][g] for g in range(NH)]).astype(BF16)
    dww = _pad_rows(w["odd_d_dw_w"], CONF_K + 1)
    tk = min(1024, t)
    zeros = jnp.zeros_like(mem)

    p_e, h_e = _rms_matmul(x, w["even_pre_g"], w["even_w_in"], tm=tm, name="in_even",
                           deps=deps)
    w.update(ex.even_rest(h_e))
    kv_e, memn_e = _rms_matmul(mem, w["even_mem_g"], w["even_w_kv"], tm=N_MEM, name="kv_even")
    kv_e = kv_e.astype(BF16)
    even_args = (w["even_a_ln_g"], w["even_a_ln_b"], ws)
    o_e, x1 = _even_fwd(x, p_e, *even_args, bmap, wc, kv_e, w["even_w_out"], w["even_post_g"])
    w.update(ex.odd(o_e))
    kv_o, memn_o = _rms_matmul(mem, w["odd_mem_g"], w["odd_w_kv"], tm=N_MEM, name="kv_odd")
    kv_o = kv_o.astype(BF16)
    p_o, h_o = _rms_matmul(x1, w["odd_pre_g"], w["odd_w_in"], tm=tm, name="in_odd")
    odd_args = (wbd, w["odd_c_scale"], dww, w["odd_d_dw_b"], w["odd_d_ln_g"], w["odd_d_ln_b"], w["odd_d_pw_w"],
                w["odd_d_pw_b"], kv_o, w["odd_w_out"], w["odd_post_g"])
    o_o, dres, loss = _odd_fwd(x1, tgt, p_o, *odd_args)

    g = {}
    (dp_o, y_o, do_o, post_g_o, dwbd, g["odd_c_scale"], ddww, g["odd_d_dw_b"], g["odd_d_ln_g"], g["odd_d_ln_b"],
     dwpw, g["odd_d_pw_b"], dkv_o) = _odd_bwd(dres, o_o, p_o, *odd_args)
    g["odd_post_g"] = post_g_o
    g["odd_d_dw_w"] = ddww[:CONF_K]
    dkv_o = dkv_o.astype(BF16)
    deps = ex.send("odd_rest", {
        "odd_w_out": _tn_matmul(y_o, do_o, tn=D, tk=tk, out_dtype=BF16, name="dw_out_odd"),
        "odd_w_kv": _tn_matmul(memn_o, dkv_o, tn=2 * XA, tk=N_MEM, out_dtype=BF16, name="dw_kv_odd"),
        "odd_d_pw_w": dwpw.astype(BF16), "loss": loss,
        "odd_c_wgrp": jnp.concatenate([dwbd[i * GRP:(i + 1) * GRP, i * GRP:(i + 1) * GRP] for i in range(NH)])})
    deps = ex.send("odd_in", {"odd_w_in": _tn_matmul(h_o, dp_o, tn=_pick_tn(ODD_IN), tk=tk, out_dtype=BF16,
                                                     name="dw_in_odd", deps=deps)})
    dx1, g["odd_pre_g"] = _nt_matmul_rms_bwd(dp_o, w["odd_w_in"], x1, w["odd_pre_g"], dres, tm=min(256, t),
                                             name="dx_odd", deps=deps)
    _, g["odd_mem_g"] = _nt_matmul_rms_bwd(dkv_o, w["odd_w_kv"], mem, w["odd_mem_g"], zeros, tm=N_MEM,
                                           name="dmem_odd")

    (dp_e, y_e, do_e, post_g_e, dws, dbs, ln_g_e, ln_b_e, dwc, dkv_e) = _even_bwd(
        dx1, o_e, p_e, *even_args, wst, bmap, wc, kv_e, w["even_w_out"], w["even_post_g"])
    g["even_b_conv"] = dwc[:3]
    dkv_e = dkv_e.astype(BF16)
    g["even_w_in"] = _tn_matmul(h_e, dp_e, tn=_pick_tn(EVEN_IN), tk=tk, out_dtype=BF16, name="dw_in_even")
    deps = ex.send("even_in", g)
    grad_x, pre_g_e = _nt_matmul_rms_bwd(dp_e, w["even_w_in"], x, w["even_pre_g"], dx1, tm=min(256, t),
                                         name="dx_even", deps=deps)
    deps = ex.send("even_rest", {
        "even_w_out": _tn_matmul(y_e, do_e, tn=D, tk=tk, out_dtype=BF16, name="dw_out_even", deps=(grad_x,)),
        "even_w_kv": _tn_matmul(memn_e, dkv_e, tn=2 * XA, tk=N_MEM, out_dtype=BF16, name="dw_kv_even"),
        "even_a_ln_g": ln_g_e, "even_a_ln_b": ln_b_e,
        "even_a_ws": dws.reshape(NH * CHUNK, CHUNK), "even_a_bs": dbs[:, ::HD].T})
    _, mem_g_e = _nt_matmul_rms_bwd(dkv_e, w["even_w_kv"], mem, w["even_mem_g"], zeros, tm=N_MEM, name="dmem_even",
                                    deps=deps)
    ex.send("even_gains", {"even_pre_g": pre_g_e, "even_mem_g": mem_g_e, "even_post_g": post_g_e})
    return grad_x, mem_g_e


def _place():
    return lax.axis_index("x"), lax.axis_index("y"), lax.axis_index("c")


def _index(px, py, pc):
    return 4 * px + 2 * py + pc


_COPIES = N_DEV - 1


def _all_gather(arrs, name):
    n = len(arrs)

    def body(*refs):
        ins, outs = refs[:n], refs[n:2 * n]
        send_sems, recv_sems, local_sems = refs[2 * n:]
        x, y, c = _place()
        me, sibling = (x, y, c), (x, y, 1 - c)
        chips = [(1 - x, y), (x, 1 - y), (1 - x, 1 - y)]

        def copy(a, k, block, to, src=None):
            dst = outs[a].at[_index(*block)]
            return pltpu.make_async_remote_copy(
                src_ref=dst if src is None else src, dst_ref=dst, send_sem=send_sems.at[a * _COPIES + k],
                recv_sem=recv_sems.at[a * _COPIES + k], device_id=to, device_id_type=MESH)

        mine = [pltpu.make_async_copy(ins[a], outs[a].at[_index(*me)], local_sems.at[a]) for a in range(n)]
        first = []
        for a in range(n):
            mine[a].start()
            first.append(copy(a, 0, me, sibling, src=ins[a]))
            first += [copy(a, 1 + j, me, (*chip, c), src=ins[a]) for j, chip in enumerate(chips)]
        for cp in first:
            cp.start()
        passed = []
        for j, chip in enumerate(chips):
            for a in range(n):
                copy(a, 1 + j, (*chip, c), me).wait_recv()
                passed.append(copy(a, 4 + j, (*chip, c), sibling))
                passed[-1].start()
        for a in range(n):
            copy(a, 0, sibling, me).wait_recv()
            for j, chip in enumerate(chips):
                copy(a, 4 + j, (*chip, 1 - c), me).wait_recv()
        for cp in first + passed:
            cp.wait_send()
        for cp in mine:
            cp.wait()

    return pl.pallas_call(
        body, name=name, in_specs=[_ANY] * n, out_specs=[_ANY] * n,
        out_shape=[S((N_DEV,) + a.shape, a.dtype) for a in arrs],
        scratch_shapes=[pltpu.SemaphoreType.DMA((n * _COPIES,)), pltpu.SemaphoreType.DMA((n * _COPIES,)),
                        pltpu.SemaphoreType.DMA((n,))],
    )(*arrs)


_HBM = pl.BlockSpec(memory_space=pltpu.HBM)
_SEM = pl.BlockSpec(memory_space=pltpu.SEMAPHORE)
_EFFECT = pltpu.SideEffectType.DATAFLOW_SIDE_EFFECTING


def _exchange_copies(kinds, srcs, lands, send_sems, recv_sems, local_sems, arriving):
    x, y, c = _place()
    mine = _index(x, y, c)
    flips = [(k >> 2 & 1, k >> 1 & 1, k & 1) for k in range(1, N_DEV)]
    peers = [(1 - x if fx else x, 1 - y if fy else y, 1 - c if fc else c) for fx, fy, fc in flips]
    remote, local = [], []
    for a, kind in enumerate(kinds):
        own = srcs[a] if kind == "gather" else srcs[a].at[mine]
        local.append(pltpu.make_async_copy(own, lands[a].at[mine], local_sems.at[a]))
        for k, peer in enumerate(peers):
            there = _index(*peer)
            remote.append(pltpu.make_async_remote_copy(
                src_ref=srcs[a] if kind == "gather" else srcs[a].at[there],
                dst_ref=lands[a].at[there if arriving else mine],
                send_sem=send_sems.at[a * _COPIES + k], recv_sem=recv_sems.at[a * _COPIES + k],
                device_id=peer, device_id_type=MESH))
    return remote, local


def _exchange_start(items, name, deps=()):
    kinds = [kind for kind, _ in items]
    srcs = [a for _, a in items]
    n = len(items)
    lands = [lax.empty((N_DEV,) + (a.shape if kind == "gather" else a.shape[1:]), a.dtype) for kind, a in items]

    def body(*refs):
        send_sems, recv_sems, local_sems = refs[2 * n + len(deps):2 * n + len(deps) + 3]
        remote, local = _exchange_copies(kinds, refs[:n], refs[n:2 * n], send_sems, recv_sems, local_sems, False)
        for cp in local + remote:
            cp.start()
        refs[-1][...] = jnp.zeros_like(refs[-1])

    held = [pltpu.HBM(a.shape, a.dtype) for a in srcs + lands]
    res = pl.pallas_call(
        body, name=name,
        out_shape=(pltpu.SemaphoreType.DMA((n * _COPIES,)), pltpu.SemaphoreType.DMA((n * _COPIES,)),
                   pltpu.SemaphoreType.DMA((n,)), *held, S((8, 128), F32)),
        in_specs=[_HBM] * (2 * n) + [_ANY] * len(deps),
        out_specs=(_SEM, _SEM, _SEM, *[_HBM] * (2 * n), _whole()),
        input_output_aliases={i: 3 + i for i in range(2 * n)},
        compiler_params=pltpu.CompilerParams(has_side_effects=_EFFECT),
    )(*[pltpu.with_memory_space_constraint(a, pltpu.HBM) for a in srcs + lands], *deps)
    return (kinds, res[:3], res[3:3 + 2 * n]), res[-1]


def _exchange_wait(handle, after, name):
    kinds, sems, held = handle
    n = len(kinds)

    def body(*refs):
        send_sems, recv_sems, local_sems = refs[2 * n:2 * n + 3]
        remote, local = _exchange_copies(kinds, refs[:n], refs[n:2 * n], send_sems, recv_sems, local_sems, True)
        for cp in remote:
            cp.wait_send()
            cp.wait_recv()
        for cp in local:
            cp.wait()

    res = pl.pallas_call(
        body, name=name, out_shape=[pltpu.HBM(a.shape, a.dtype) for a in held],
        in_specs=[_HBM] * (2 * n) + [_SEM] * 3 + [_ANY] * len(after), out_specs=[_HBM] * (2 * n),
        input_output_aliases={i: i for i in range(2 * n)},
        compiler_params=pltpu.CompilerParams(has_side_effects=_EFFECT),
    )(*held, *sems, *after)
    return res[n:]


def _adamw(w, g, m, v):
    m = ADAM_B1 * m + (1.0 - ADAM_B1) * g
    v = ADAM_B2 * v + (1.0 - ADAM_B2) * (g * g)
    m_hat = m / (1.0 - ADAM_B1 ** ADAM_STEP)
    v_hat = v / (1.0 - ADAM_B2 ** ADAM_STEP)
    return -ADAM_LR * (m_hat / (jnp.sqrt(v_hat) + ADAM_EPS) + ADAM_WD * w), m, v


def _sum_devices(ref, rows):
    total = ref[0, rows, :].astype(F32)
    for s in range(1, N_DEV):
        total = total + ref[s, rows, :].astype(F32)
    return total


def _adam_big(recv, w, m, v, *, tr, name):
    r, c = w.shape

    def body(recv_ref, w_ref, m_ref, v_ref, g_ref, d_ref, m2_ref, v2_ref):
        g = _sum_devices(recv_ref, slice(None))
        g_ref[...] = g
        d_ref[...], m2_ref[...], v2_ref[...] = _adamw(w_ref[...], g, m_ref[...], v_ref[...])

    blk = pl.BlockSpec((tr, c), lambda i: (i, 0))
    return pl.pallas_call(
        body, name=name, grid=(r // tr,),
        in_specs=[pl.BlockSpec((N_DEV, tr, c), lambda i: (0, i, 0)), blk, blk, blk],
        out_specs=[blk] * 4, out_shape=[S((r, c), F32)] * 4,
        compiler_params=_params(("arbitrary",)),
    )(recv, w, m, v)


_REPLICATED = {"even_pre_g": (0, 0, 1), "even_mem_g": (0, 8, 1), "even_post_g": (0, 16, 1),
               "even_a_ln_g": (1, 0, 1), "even_a_ln_b": (1, 8, 1),
               "even_a_ws": (2, 0, NH * CHUNK), "even_a_bs": (2, NH * CHUNK, NH),
               "odd_c_wgrp": (3, 0, NH * GRP)}
_SHARDED = {"odd_pre_g": (4, 0, 1), "odd_mem_g": (4, 8, 1), "odd_post_g": (4, 16, 1),
            "even_b_conv": (5, 0, 3), "odd_c_scale": (5, 8, 1), "odd_d_dw_w": (5, 16, CONF_K),
            "odd_d_dw_b": (5, 48, 1), "odd_d_ln_g": (5, 56, 1), "odd_d_ln_b": (5, 64, 1), "odd_d_pw_b": (5, 72, 1)}
_SMALL = {**_REPLICATED, **_SHARDED}
_SMALL_ROWS = {0: 24, 1: 16, 2: NH * CHUNK + 8, 3: NH * GRP, 4: 24, 5: 80}


def _adam_small(sources, wmv):
    names = list(_SMALL)
    ns = len(sources)

    def body(*refs):
        src = refs[:ns]
        ins = refs[ns:ns + 3 * len(names)]
        outs = refs[ns + 3 * len(names):]
        outs[-1][...] = _sum_devices(src[-1], slice(0, 1))
        for i, nm in enumerate(names):
            a, row0, rows = _SMALL[nm]
            g = _sum_devices(src[a], slice(row0, row0 + rows))
            w_ref, m_ref, v_ref = ins[3 * i:3 * i + 3]
            g_ref, d_ref, m2_ref, v2_ref = outs[4 * i:4 * i + 4]
            g_ref[...] = g
            d_ref[...], m2_ref[...], v2_ref[...] = _adamw(w_ref[...], g, m_ref[...], v_ref[...])

    flat = [t for nm in names for t in wmv[nm]]
    out_shape = [S(wmv[nm][0].shape, F32) for nm in names for _ in range(4)] + [S((1, HD), F32)]
    res = pl.pallas_call(
        body, name="adam_small", in_specs=[_whole()] * (ns + len(flat)), out_specs=[_whole()] * len(out_shape),
        out_shape=out_shape, compiler_params=_params(),
    )(*sources, *flat)
    return {nm: tuple(res[4 * i:4 * i + 4]) for i, nm in enumerate(names)}, res[-1]


_WEIGHTS = ["even_pre_g", "even_w_in", "even_a_ln_g", "even_a_ln_b", "even_a_ws", "even_a_bs", "even_b_conv",
            "even_mem_g", "even_w_kv", "even_w_out", "even_post_g", "odd_pre_g", "odd_w_in", "odd_c_wgrp",
            "odd_c_scale", "odd_d_dw_w", "odd_d_dw_b", "odd_d_ln_g", "odd_d_ln_b", "odd_d_pw_w", "odd_d_pw_b",
            "odd_mem_g", "odd_w_kv", "odd_w_out", "odd_post_g"]
_BIG_COLS = ["even_w_in", "odd_w_in"]
_BIG_ROWS = ["even_w_kv", "even_w_out", "odd_w_kv", "odd_w_out", "odd_d_pw_w"]
_BIG = _BIG_COLS + _BIG_ROWS
_BIG_TILE_ROWS = {"even_w_in": 256, "odd_w_in": 256, "even_w_kv": 128, "even_w_out": 128, "odd_w_kv": 128,
                  "odd_w_out": 128, "odd_d_pw_w": 96}


def _view2d(a):
    a = a[0]
    if a.ndim == 1:
        return a[None]
    return a.reshape(-1, a.shape[-1])


def _rows8(a):
    return _pad_rows(a, -(-a.shape[0] // 8) * 8)


def _pack_rows(parts):
    return jnp.concatenate([_rows8(p) for p in parts], axis=0)


def _unshard_cols(a):
    return jnp.transpose(a, (1, 0, 2)).reshape(a.shape[1], N_DEV * a.shape[2])


def _shard_cols(a):
    return jnp.transpose(a.reshape(a.shape[0], N_DEV, a.shape[1] // N_DEV), (1, 0, 2))


def _rows_of(a):
    return a.reshape(-1, a.shape[-1])


_GROUPS = {"odd_rest": (["odd_w_out", "odd_w_kv", "odd_d_pw_w"], [3], []),
           "odd_in": (["odd_w_in"], [], []),
           "even_in": (["even_w_in"], [], [4, 5]),
           "even_rest": (["even_w_out", "even_w_kv"], [1, 2], []),
           "even_gains": ([], [0], [])}


class _MeshExchange:
    def __init__(self, shard):
        self.shard = shard
        self.handles = {}

    def first(self):
        shard = self.shard
        packs = [_pack_rows([shard[nm] for nm in _SHARDED if _SHARDED[nm][0] == a]) for a in (4, 5)]
        w_in, p128, p96 = _all_gather([shard["even_w_in"].astype(BF16)] + packs, "gather_first")
        w = {nm: shard[nm] for nm in _REPLICATED}
        w["even_a_ws"] = w["even_a_ws"].reshape(NH, CHUNK, CHUNK)
        w["odd_c_wgrp"] = w["odd_c_wgrp"].reshape(NH, GRP, GRP)
        w["even_w_in"] = _unshard_cols(w_in)
        full_packs = {4: _unshard_cols(p128), 5: _unshard_cols(p96)}
        for nm, (a, row0, rows) in _SHARDED.items():
            w[nm] = full_packs[a][row0:row0 + rows]
        later = lambda names: [("gather", shard[nm].astype(BF16)) for nm in names]
        self.handles["w_even"], token = _exchange_start(later(["even_w_kv", "even_w_out"]), "gather_even_start",
                                                        deps=(w_in,))
        self.handles["w_odd"], token = _exchange_start(later(["odd_w_in", "odd_w_kv", "odd_w_out", "odd_d_pw_w"]),
                                                       "gather_odd_start", deps=(token,))
        return w, (token,)

    def even_rest(self, after):
        kv, out = _exchange_wait(self.handles.pop("w_even"), (after,), "gather_even_wait")
        return {"even_w_kv": _rows_of(kv), "even_w_out": _rows_of(out)}

    def odd(self, after):
        w_in, kv, out, pw = _exchange_wait(self.handles.pop("w_odd"), (after,), "gather_odd_wait")
        return {"odd_w_in": _unshard_cols(w_in), "odd_w_kv": _rows_of(kv), "odd_w_out": _rows_of(out),
                "odd_d_pw_w": _rows_of(pw)}

    def send(self, group, g):
        big, replicated, sharded = _GROUPS[group]
        items = [("scatter", _shard_cols(g[nm]) if nm in _BIG_COLS else g[nm].reshape(N_DEV, -1, g[nm].shape[-1]))
                 for nm in big]
        items += [("gather", _pack_rows([g[nm] for nm in _REPLICATED if _REPLICATED[nm][0] == a]))
                  for a in replicated]
        items += [("scatter", _shard_cols(_pack_rows([g[nm] for nm in _SHARDED if _SHARDED[nm][0] == a])))
                  for a in sharded]
        if group == "odd_rest":
            items.append(("gather", _rows8(g["loss"])))
        self.handles[group], token = _exchange_start(items, "send_" + group + "_start")
        return (token,)

    def receive(self, group, after):
        return _exchange_wait(self.handles.pop(group), (after,), "send_" + group + "_wait")


def kernel(x, mem, even_pre_g, even_w_in, even_a_ln_g, even_a_ln_b, even_a_ws, even_a_bs, even_b_conv, even_mem_g, even_w_kv, even_w_out, even_post_g, odd_pre_g, odd_w_in, odd_c_wgrp, odd_c_scale, odd_d_dw_w, odd_d_dw_b, odd_d_ln_g, odd_d_ln_b, odd_d_pw_w, odd_d_pw_b, odd_mem_g, odd_w_kv, odd_w_out, odd_post_g, loss_target, m_even_pre_g, m_even_w_in, m_even_a_ln_g, m_even_a_ln_b, m_even_a_ws, m_even_a_bs, m_even_b_conv, m_even_mem_g, m_even_w_kv, m_even_w_out, m_even_post_g, m_odd_pre_g, m_odd_w_in, m_odd_c_wgrp, m_odd_c_scale, m_odd_d_dw_w, m_odd_d_dw_b, m_odd_d_ln_g, m_odd_d_ln_b, m_odd_d_pw_w, m_odd_d_pw_b, m_odd_mem_g, m_odd_w_kv, m_odd_w_out, m_odd_post_g, v_even_pre_g, v_even_w_in, v_even_a_ln_g, v_even_a_ln_b, v_even_a_ws, v_even_a_bs, v_even_b_conv, v_even_mem_g, v_even_w_kv, v_even_w_out, v_even_post_g, v_odd_pre_g, v_odd_w_in, v_odd_c_wgrp, v_odd_c_scale, v_odd_d_dw_w, v_odd_d_dw_b, v_odd_d_ln_g, v_odd_d_ln_b, v_odd_d_pw_w, v_odd_d_pw_b, v_odd_mem_g, v_odd_w_kv, v_odd_w_out, v_odd_post_g):
    given = dict(locals())
    shard = {nm: _view2d(given[nm]) for nm in _WEIGHTS}
    wmv = {nm: (shard[nm], _view2d(given["m_" + nm]), _view2d(given["v_" + nm])) for nm in _WEIGHTS}

    ex = _MeshExchange(shard)
    grad_x, last = _step(x[0], mem[0], loss_target[0], ex)

    res = {}

    def update(group, after):
        names = _GROUPS[group][0]
        landed = ex.receive(group, after)
        for nm, recv in zip(names, landed):
            res[nm] = _adam_big(recv, *wmv[nm], tr=_BIG_TILE_ROWS[nm], name="adam_" + nm)
        return landed[len(names):]

    c192, losses = update("odd_rest", last)
    update("odd_in", res["odd_d_pw_w"][0])
    a128, a96 = update("even_in", res["odd_w_in"][0])
    c768, c128 = update("even_rest", res["even_w_in"][0])
    (c1024,) = update("even_gains", res["even_w_kv"][0])
    small, loss = _adam_small([c1024, c768, c128, c192, a128, a96, losses], {nm: wmv[nm] for nm in _SMALL})
    res.update(small)
    total = loss[0, 0]
    outs = [[res[nm][i].reshape(given[nm].shape) for nm in _WEIGHTS] for i in range(4)]
    return (total, grad_x[None], *outs[0], *outs[1], *outs[2], *outs[3])
```

```python
import functools

import jax
import jax.numpy as jnp
from jax import lax
from jax.experimental import pallas as pl
from jax.experimental.pallas import tpu as pltpu

F32 = jnp.float32
BF16 = jnp.bfloat16
S = jax.ShapeDtypeStruct
MESH = pl.DeviceIdType.MESH
AXES = ("x", "y", "c")
N_DEV = 8

D = 1024
BW = 768
XA = 512
HD = 128
NH = 4
MIX = 2048
CHUNK = 128
GRP = 192
N_MEM = 256
CONF_K = 31
EPS = 1e-6
HALO = 32
POOL_WINDOWS = (2, 4, 8, 16)
TM_FWD = 256
TM_BWD = 128

E_U, E_V, E_BG, E_CG, E_XIN, E_Q, E_GATE = 0, 768, 1536, 2304, 3072, 3840, 4352
EVEN_IN = 6400
O_ZC, O_GA, O_GB, O_Q, O_GATE = 0, 768, 1536, 2304, 2816
ODD_IN = 4864

ADAM_LR, ADAM_B1, ADAM_B2, ADAM_EPS, ADAM_WD, ADAM_STEP = 0.001, 0.9, 0.999, 1e-08, 0.01, 10

VMEM_LIMIT_V7X = 56 * 1024 * 1024


def _params(sem=None):
    return pltpu.CompilerParams(dimension_semantics=sem, vmem_limit_bytes=VMEM_LIMIT_V7X)


def _dot(a, b):
    return jnp.dot(a, b, preferred_element_type=F32)


def _dot_nt(a, b):
    return lax.dot_general(a, b, (((1,), (1,)), ((), ())), preferred_element_type=F32)


def _dot_tn(a, b):
    return lax.dot_general(a, b, (((0,), (0,)), ((), ())), preferred_element_type=F32)


def _sigmoid(z):
    return 1.0 / (1.0 + jnp.exp(-z))


def _rowmean(a):
    return jnp.mean(a, axis=-1, keepdims=True)


def _colsum(a):
    return jnp.sum(a, axis=0, keepdims=True)


def _ln_stats(v):
    mu = _rowmean(v)
    vc = v - mu
    rs = lax.rsqrt(_rowmean(vc * vc) + EPS)
    return vc * rs, rs


def _ln_bwd(dn, vh, rs, g):
    dvh = dn * g
    return rs * (dvh - _rowmean(dvh) - vh * _rowmean(dvh * vh))


def _group_masks():
    col = lax.broadcasted_iota(jnp.int32, (1, BW), 1)
    return [((col >= GRP * h) & (col < GRP * (h + 1))).astype(F32) for h in range(NH)]


def _full(shape):
    nd = len(shape)
    return pl.BlockSpec(shape, lambda *_: (0,) * nd)


def _whole():
    return pl.BlockSpec(memory_space=pltpu.VMEM)


_ANY = pl.BlockSpec(memory_space=pl.ANY)


def _after(body, n_in, deps):
    def ordered(*refs):
        return body(*refs[:n_in], *refs[n_in + len(deps):])
    return ordered


def _rms_matmul(x, g, w, *, tm, name, transposed=False, deps=()):
    t, d = x.shape
    n = w.shape[0] if transposed else w.shape[1]

    def body(x_ref, g_ref, w_ref, p_ref, h_ref):
        xv = x_ref[...]
        r = lax.rsqrt(_rowmean(xv * xv) + EPS)
        h = (xv * r * g_ref[...]).astype(BF16)
        h_ref[...] = h
        p_ref[...] = _dot_nt(h, w_ref[...]) if transposed else _dot(h, w_ref[...])

    return pl.pallas_call(
        _after(body, 3, deps), name=name, grid=(t // tm,),
        in_specs=[pl.BlockSpec((tm, d), lambda i: (i, 0)), _whole(), _whole()] + [_ANY] * len(deps),
        out_specs=[pl.BlockSpec((tm, n), lambda i: (i, 0)), pl.BlockSpec((tm, d), lambda i: (i, 0))],
        out_shape=[S((t, n), F32), S((t, d), BF16)],
        compiler_params=_params(("arbitrary",)),
    )(x, g, w, *deps)


def _nt_matmul_rms_bwd(dp, w, x, g, dres, *, tm, name, transposed=False, deps=()):
    t, n = dp.shape
    d = x.shape[1]

    def body(dp_ref, w_ref, x_ref, g_ref, dres_ref, dx_ref, dg_ref):
        @pl.when(pl.program_id(0) == 0)
        def _():
            dg_ref[...] = jnp.zeros_like(dg_ref)

        dh = _dot(dp_ref[...], w_ref[...]) if transposed else _dot_nt(dp_ref[...], w_ref[...])
        xv = x_ref[...]
        r = lax.rsqrt(_rowmean(xv * xv) + EPS)
        xh = xv * r
        dg_ref[...] += _colsum(dh * xh)
        dxh = dh * g_ref[...]
        dx_ref[...] = dres_ref[...] + r * (dxh - xh * _rowmean(dxh * xh))

    return pl.pallas_call(
        _after(body, 5, deps), name=name, grid=(t // tm,),
        in_specs=[pl.BlockSpec((tm, n), lambda i: (i, 0)), _whole(), pl.BlockSpec((tm, d), lambda i: (i, 0)),
                  _whole(), pl.BlockSpec((tm, d), lambda i: (i, 0))] + [_ANY] * len(deps),
        out_specs=[pl.BlockSpec((tm, d), lambda i: (i, 0)), pl.BlockSpec((1, d), lambda i: (0, 0))],
        out_shape=[S((t, d), F32), S((1, d), F32)],
        compiler_params=_params(("arbitrary",)),
    )(dp, w, x, g, dres, *deps)


def _tn_matmul(a, b, *, tmc, tk, out_dtype, name, deps=()):
    t, m = a.shape
    n = b.shape[1]
    nk = t // tk

    def body(a_ref, b_ref, o_ref, acc_ref):
        k = pl.program_id(1)

        @pl.when(k == 0)
        def _():
            acc_ref[...] = jnp.zeros_like(acc_ref)

        acc_ref[...] += _dot_tn(a_ref[...], b_ref[...])

        @pl.when(k == nk - 1)
        def _():
            o_ref[...] = acc_ref[...].astype(out_dtype)

    return pl.pallas_call(
        _after(body, 2, deps), name=name, grid=(m // tmc, nk),
        in_specs=[pl.BlockSpec((tk, tmc), lambda j, k: (k, j)), pl.BlockSpec((tk, n), lambda j, k: (k, 0))]
        + [_ANY] * len(deps),
        out_specs=pl.BlockSpec((tmc, n), lambda j, k: (j, 0)),
        out_shape=S((m, n), out_dtype),
        scratch_shapes=[pltpu.VMEM((tmc, n), F32)],
        compiler_params=_params(("arbitrary", "arbitrary")),
    )(a, b, *deps)


def _silu_parts(gt):
    sg = _sigmoid(gt)
    return gt * sg, sg * (1.0 + gt * (1.0 - sg))


def _attn_head(q_b, k_b, v_b):
    s = _dot_nt(q_b, k_b) * (HD ** -0.5)
    e = jnp.exp(s - jnp.max(s, axis=-1, keepdims=True))
    prob = e / jnp.sum(e, axis=-1, keepdims=True)
    return prob, _dot(prob.astype(BF16), v_b)


def _rms_residual(x, o, g):
    r = lax.rsqrt(_rowmean(o * o) + EPS)
    return x + o * r * g


def _rms_post_bwd(dres, o, g):
    r = lax.rsqrt(_rowmean(o * o) + EPS)
    oh = o * r
    doh = dres * g
    return r * (doh - oh * _rowmean(doh * oh)), _colsum(dres * oh)


def _sgu_chunk(vn_b, ws_ref, bmap_ref, masks):
    sg = bmap_ref[...]
    for h in range(NH):
        sg = sg + masks[h] * _dot(ws_ref[h], vn_b)
    return sg


def _shift_copies(buf, sh):
    n = buf.shape[0] - 8
    for b in range(1, 8):
        sh[b - 1, pl.ds(0, n), :] = buf[pl.ds(b, n), :]


def _rows_at(buf, sh, off):
    b = off % 8
    if b == 0 or sh is None:
        return buf[pl.ds(off, 32), :]
    return sh[b - 1, pl.ds(off - b, 32), :]


def _taps_causal(buf, sh, w_ref, taps, bias):
    rows = buf.shape[0] - HALO
    outs = []
    for rb in range(rows // 32):
        acc = None
        for k in range(taps):
            term = w_ref[k:k + 1, :] * _rows_at(buf, sh, rb * 32 + HALO - (taps - 1 - k))
            acc = term if acc is None else acc + term
        outs.append(acc if bias is None else acc + bias)
    return outs


def _taps_anticausal(buf, sh, w_ref, taps):
    rows = buf.shape[0] - HALO
    outs = []
    for rb in range(rows // 32):
        acc = None
        for k in range(taps):
            term = w_ref[k:k + 1, :] * _rows_at(buf, sh, rb * 32 + (taps - 1 - k))
            acc = term if acc is None else acc + term
        outs.append(acc)
    return outs


def _fold8(a):
    return a[0:8] + a[8:16] + a[16:24] + a[24:32]


def _tap_grads(d_ref, buf, sh, acc_ref, taps):
    rows = buf.shape[0] - HALO
    for rb in range(rows // 32):
        dv = d_ref[rb * 32:(rb + 1) * 32, :]
        for k in range(taps):
            prod = dv * _rows_at(buf, sh, rb * 32 + HALO - (taps - 1 - k))
            acc_ref[k * 8:(k + 1) * 8, :] += _fold8(prod)


def _halo_spec(n, nt, reverse, tm):
    per = tm // HALO
    if reverse:
        return pl.BlockSpec((HALO, n), lambda i: (jnp.maximum((nt - 1 - i) * per - 1, 0), 0))
    return pl.BlockSpec((HALO, n), lambda i: (jnp.maximum(i * per - 1, 0), 0))


def _even_fwd(x, p, lng, lnb, ws, bmap, wc, kv, wout, pg):
    t = x.shape[0]
    tm = min(TM_FWD, t)
    nt = t // tm

    def body(x_ref, p_ref, ph_ref, lng_ref, lnb_ref, ws_ref, bmap_ref, wc_ref, kv_ref, wout_ref, pg_ref,
             o_ref, x1_ref, ybuf, cbuf):
        i = pl.program_id(0)
        masks = _group_masks()
        vh, _ = _ln_stats(p_ref[:, E_V:E_V + BW])
        vn = vh * lng_ref[...] + lnb_ref[...]
        for c in range(tm // CHUNK):
            sl = slice(c * CHUNK, (c + 1) * CHUNK)
            sg = _sgu_chunk(vn[sl].astype(BF16), ws_ref, bmap_ref, masks)
            gate, _ = _silu_parts(p_ref[sl, E_GATE:E_GATE + BW])
            ybuf[sl, 0:BW] = (p_ref[sl, E_U:E_U + BW] * sg * gate).astype(BF16)

        cbuf[0:HALO] = jnp.where(i > 0, ph_ref[:, E_CG:E_CG + BW] * ph_ref[:, E_XIN:E_XIN + BW], 0.0)
        cbuf[HALO:HALO + tm] = p_ref[:, E_CG:E_CG + BW] * p_ref[:, E_XIN:E_XIN + BW]
        conv = _taps_causal(cbuf, None, wc_ref, 3, None)
        for rb, cv in enumerate(conv):
            sl = slice(rb * 32, (rb + 1) * 32)
            gate, _ = _silu_parts(p_ref[sl, E_GATE + BW:E_GATE + 2 * BW])
            ybuf[sl, BW:2 * BW] = (p_ref[sl, E_BG:E_BG + BW] * cv * gate).astype(BF16)

        for h in range(NH):
            qs = slice(E_Q + h * HD, E_Q + (h + 1) * HD)
            _, yx = _attn_head(p_ref[:, qs].astype(BF16), kv_ref[:, h * HD:(h + 1) * HD],
                               kv_ref[:, XA + h * HD:XA + (h + 1) * HD])
            gs = slice(E_GATE + 2 * BW + h * HD, E_GATE + 2 * BW + (h + 1) * HD)
            gate, _ = _silu_parts(p_ref[:, gs])
            ybuf[:, 2 * BW + h * HD:2 * BW + (h + 1) * HD] = (yx * gate).astype(BF16)

        o = _dot(ybuf[...], wout_ref[...])
        o_ref[...] = o
        x1_ref[...] = _rms_residual(x_ref[...], o, pg_ref[...])

    tile = lambda n: pl.BlockSpec((tm, n), lambda i: (i, 0))
    return pl.pallas_call(
        body, name="even_fwd", grid=(nt,),
        in_specs=[tile(D), tile(EVEN_IN), _halo_spec(EVEN_IN, nt, False, tm)] + [_whole()] * 8,
        out_specs=[tile(D), tile(D)],
        out_shape=[S((t, D), F32), S((t, D), F32)],
        scratch_shapes=[pltpu.VMEM((tm, MIX), BF16), pltpu.VMEM((tm + HALO, BW), F32)],
        compiler_params=_params(("arbitrary",)),
    )(x, p, p, lng, lnb, ws, bmap, wc, kv, wout, pg)


def _even_bwd(dres, o, p, lng, lnb, ws, wst, bmap, wc, kv, wout, pg):
    t = dres.shape[0]
    tm = min(TM_BWD, t)
    nt = t // tm

    def body(dres_ref, o_ref, p_ref, ph_ref, lng_ref, lnb_ref, ws_ref, wst_ref, bmap_ref, wc_ref, kv_ref, wout_ref,
             pg_ref, dp_ref, y_ref, do_ref, dpg_ref, dws_ref, dbs_ref, dlng_ref, dlnb_ref, dwc_ref, dkv_ref,
             dy, cbuf, gbuf, dconv, carry, dvn, dbmap, wacc):
        i = pl.program_id(0)
        ti = nt - 1 - i
        masks = _group_masks()

        @pl.when(i == 0)
        def _():
            for ref in (dpg_ref, dws_ref, dlng_ref, dlnb_ref, dkv_ref, dbmap, wacc):
                ref[...] = jnp.zeros_like(ref)

        do, dpg = _rms_post_bwd(dres_ref[...], o_ref[...], pg_ref[...])
        dpg_ref[...] += dpg
        do_b = do.astype(BF16)
        do_ref[...] = do_b
        dy[...] = _dot_nt(do_b, wout_ref[...])

        vh, rs = _ln_stats(p_ref[:, E_V:E_V + BW])
        vn = vh * lng_ref[...] + lnb_ref[...]
        for c in range(tm // CHUNK):
            sl = slice(c * CHUNK, (c + 1) * CHUNK)
            vn_b = vn[sl].astype(BF16)
            sg = _sgu_chunk(vn_b, ws_ref, bmap_ref, masks)
            u = p_ref[sl, E_U:E_U + BW]
            gate, dgate = _silu_parts(p_ref[sl, E_GATE:E_GATE + BW])
            dyc = dy[sl, 0:BW]
            ya = u * sg
            y_ref[sl, 0:BW] = (ya * gate).astype(BF16)
            dp_ref[sl, E_GATE:E_GATE + BW] = (dyc * ya * dgate).astype(BF16)
            dya = dyc * gate
            dp_ref[sl, E_U:E_U + BW] = (dya * sg).astype(BF16)
            dsg = dya * u
            dbmap[...] += dsg
            dsg_b = dsg.astype(BF16)
            acc = jnp.zeros((CHUNK, BW), F32)
            for h in range(NH):
                dws_ref[h] += _dot_nt((dsg * masks[h]).astype(BF16), vn_b)
                acc = acc + masks[h] * _dot(wst_ref[h], dsg_b)
            dvn[sl, :] = acc
        dn = dvn[...]
        dlng_ref[...] += _colsum(dn * vh)
        dlnb_ref[...] += _colsum(dn)
        dp_ref[:, E_V:E_V + BW] = _ln_bwd(dn, vh, rs, lng_ref[...]).astype(BF16)

        cbuf[0:HALO] = jnp.where(ti > 0, ph_ref[:, E_CG:E_CG + BW] * ph_ref[:, E_XIN:E_XIN + BW], 0.0)
        cbuf[HALO:HALO + tm] = p_ref[:, E_CG:E_CG + BW] * p_ref[:, E_XIN:E_XIN + BW]
        conv = _taps_causal(cbuf, None, wc_ref, 3, None)
        for rb, cv in enumerate(conv):
            sl = slice(rb * 32, (rb + 1) * 32)
            gate, dgate = _silu_parts(p_ref[sl, E_GATE + BW:E_GATE + 2 * BW])
            bg = p_ref[sl, E_BG:E_BG + BW]
            dyc = dy[sl, BW:2 * BW]
            yb = bg * cv
            y_ref[sl, BW:2 * BW] = (yb * gate).astype(BF16)
            dp_ref[sl, E_GATE + BW:E_GATE + 2 * BW] = (dyc * yb * dgate).astype(BF16)
            dyb = dyc * gate
            dp_ref[sl, E_BG:E_BG + BW] = (dyb * cv).astype(BF16)
            dconv[sl, :] = dyb * bg
        gbuf[0:tm] = dconv[...]
        gbuf[tm:tm + HALO] = jnp.where(i > 0, carry[...], 0.0)
        carry[...] = dconv[0:HALO]
        _tap_grads(dconv, cbuf, None, wacc, 3)
        for rb, dc in enumerate(_taps_anticausal(gbuf, None, wc_ref, 3)):
            sl = slice(rb * 32, (rb + 1) * 32)
            dp_ref[sl, E_CG:E_CG + BW] = (dc * p_ref[sl, E_XIN:E_XIN + BW]).astype(BF16)
            dp_ref[sl, E_XIN:E_XIN + BW] = (dc * p_ref[sl, E_CG:E_CG + BW]).astype(BF16)

        for h in range(NH):
            qs = slice(E_Q + h * HD, E_Q + (h + 1) * HD)
            ks = slice(h * HD, (h + 1) * HD)
            vs = slice(XA + h * HD, XA + (h + 1) * HD)
            gs = slice(E_GATE + 2 * BW + h * HD, E_GATE + 2 * BW + (h + 1) * HD)
            ys = slice(2 * BW + h * HD, 2 * BW + (h + 1) * HD)
            q_b = p_ref[:, qs].astype(BF16)
            prob, yx = _attn_head(q_b, kv_ref[:, ks], kv_ref[:, vs])
            gate, dgate = _silu_parts(p_ref[:, gs])
            dyc = dy[:, ys]
            y_ref[:, ys] = (yx * gate).astype(BF16)
            dp_ref[:, gs] = (dyc * yx * dgate).astype(BF16)
            dyx_b = (dyc * gate).astype(BF16)
            dprob = _dot_nt(dyx_b, kv_ref[:, vs])
            dkv_ref[:, vs] += _dot_tn(prob.astype(BF16), dyx_b)
            ds_b = (prob * (dprob - jnp.sum(dprob * prob, axis=-1, keepdims=True)) * (HD ** -0.5)).astype(BF16)
            dp_ref[:, qs] = _dot(ds_b, kv_ref[:, ks]).astype(BF16)
            dkv_ref[:, ks] += _dot_tn(ds_b, q_b)

        @pl.when(i == nt - 1)
        def _():
            for h in range(NH):
                dbs_ref[:, h * HD:(h + 1) * HD] = jnp.broadcast_to(
                    jnp.sum(dbmap[...] * masks[h], axis=-1, keepdims=True), (CHUNK, HD))
            for k in range(3):
                dwc_ref[k:k + 1, :] = _colsum(wacc[k * 8:(k + 1) * 8, :])
            dwc_ref[3:8, :] = jnp.zeros((5, BW), F32)
            causal = (lax.broadcasted_iota(jnp.int32, (CHUNK, CHUNK), 0)
                      >= lax.broadcasted_iota(jnp.int32, (CHUNK, CHUNK), 1))
            for h in range(NH):
                dws_ref[h] = jnp.where(causal, dws_ref[h], 0.0)

    rtile = lambda n: pl.BlockSpec((tm, n), lambda i: (nt - 1 - i, 0))
    outs = [S((t, EVEN_IN), BF16), S((t, MIX), BF16), S((t, D), BF16), S((1, D), F32), S((NH, CHUNK, CHUNK), F32),
            S((CHUNK, NH * HD), F32), S((1, BW), F32), S((1, BW), F32), S((8, BW), F32), S((N_MEM, 2 * XA), F32)]
    return pl.pallas_call(
        body, name="even_bwd", grid=(nt,),
        in_specs=[rtile(D), rtile(D), rtile(EVEN_IN), _halo_spec(EVEN_IN, nt, True, tm)] + [_whole()] * 9,
        out_specs=[rtile(EVEN_IN), rtile(MIX), rtile(D)] + [_full(s.shape) for s in outs[3:]],
        out_shape=outs,
        scratch_shapes=[pltpu.VMEM((tm, MIX), F32), pltpu.VMEM((tm + HALO, BW), F32), pltpu.VMEM((tm + HALO, BW), F32),
                        pltpu.VMEM((tm, BW), F32), pltpu.VMEM((HALO, BW), F32), pltpu.VMEM((tm, BW), F32),
                        pltpu.VMEM((CHUNK, BW), F32), pltpu.VMEM((3 * 8, BW), F32)],
        compiler_params=_params(("arbitrary",)),
    )(dres, o, p, p, lng, lnb, ws, wst, bmap, wc, kv, wout, pg)


def _pool_causal(za, zb, zc, zd, tm):
    n = tm + HALO
    zb[pl.ds(8, n - 8), :] = za[pl.ds(8, n - 8), :] + za[pl.ds(7, n - 8), :]
    zc[pl.ds(16, n - 16), :] = zb[pl.ds(16, n - 16), :] + zb[pl.ds(14, n - 16), :]
    zd[pl.ds(24, n - 24), :] = zc[pl.ds(24, n - 24), :] + zc[pl.ds(20, n - 24), :]
    s16 = zd[pl.ds(HALO, tm), :] + zd[pl.ds(HALO - 8, tm), :]
    return zb[pl.ds(HALO, tm), :], zc[pl.ds(HALO, tm), :], zd[pl.ds(HALO, tm), :], s16


def _pool_anticausal(ea, eb, ec, ed, tm):
    n = tm + HALO
    eb[pl.ds(0, n - 8), :] = ea[pl.ds(0, n - 8), :] + ea[pl.ds(1, n - 8), :]
    ec[pl.ds(0, n - 16), :] = eb[pl.ds(0, n - 16), :] + eb[pl.ds(2, n - 16), :]
    ed[pl.ds(0, n - 24), :] = ec[pl.ds(0, n - 24), :] + ec[pl.ds(4, n - 24), :]
    a16 = ed[pl.ds(0, tm), :] + ed[pl.ds(8, tm), :]
    return eb[pl.ds(0, tm), :], ec[pl.ds(0, tm), :], ed[pl.ds(0, tm), :], a16


def _pool_weights(ti, masks, tm):
    tf = (ti * tm + lax.broadcasted_iota(jnp.int32, (tm, 1), 0) + 1).astype(F32)
    inv = None
    for g, win in enumerate(POOL_WINDOWS):
        term = masks[g] * (1.0 / jnp.minimum(tf, float(win)))
        inv = term if inv is None else inv + term
    return inv


def _mix4(masks, parts):
    return masks[0] * parts[0] + masks[1] * parts[1] + masks[2] * parts[2] + masks[3] * parts[3]


def _odd_fwd(x1, tgt, p, wbd, cscale, dww, dwb, lng, lnb, wpw, pwb, kv, wout, pg):
    t = x1.shape[0]
    tm = min(TM_FWD, t)
    nt = t // tm

    def body(x_ref, tgt_ref, p_ref, ph_ref, wbd_ref, cs_ref, dww_ref, dwb_ref, lng_ref, lnb_ref, wpw_ref, pwb_ref,
             kv_ref, wout_ref, pg_ref, o_ref, dres_ref, loss_ref, conv_ref, ybuf, za, zb, zc, zd, gbuf, lacc, gsh):
        i = pl.program_id(0)
        masks = _group_masks()

        @pl.when(i == 0)
        def _():
            lacc[...] = jnp.zeros_like(lacc)

        z = p_ref[:, O_ZC:O_ZC + BW]
        za[0:HALO] = jnp.where(i > 0, ph_ref[:, O_ZC:O_ZC + BW], 0.0)
        za[HALO:HALO + tm] = z
        pooled = _mix4(masks, _pool_causal(za, zb, zc, zd, tm)) * _pool_weights(i, masks, tm) - z
        gate, _ = _silu_parts(p_ref[:, O_GATE:O_GATE + BW])
        ybuf[:, 0:BW] = (_dot(pooled.astype(BF16), wbd_ref[...]) * cs_ref[...] * gate).astype(BF16)

        gbuf[0:HALO] = jnp.where(i > 0, ph_ref[:, O_GA:O_GA + BW] * _sigmoid(ph_ref[:, O_GB:O_GB + BW]), 0.0)
        gbuf[HALO:HALO + tm] = p_ref[:, O_GA:O_GA + BW] * _sigmoid(p_ref[:, O_GB:O_GB + BW])
        _shift_copies(gbuf, gsh)
        for rb, blk in enumerate(_taps_causal(gbuf, gsh, dww_ref, CONF_K, dwb_ref[...])):
            conv_ref[rb * 32:(rb + 1) * 32, :] = blk
        zh, _ = _ln_stats(conv_ref[...])
        zn = zh * lng_ref[...] + lnb_ref[...]
        yd = _dot((zn * _sigmoid(zn)).astype(BF16), wpw_ref[...]) + pwb_ref[...]
        gate, _ = _silu_parts(p_ref[:, O_GATE + BW:O_GATE + 2 * BW])
        ybuf[:, BW:2 * BW] = (yd * gate).astype(BF16)

        for h in range(NH):
            qs = slice(O_Q + h * HD, O_Q + (h + 1) * HD)
            _, yx = _attn_head(p_ref[:, qs].astype(BF16), kv_ref[:, h * HD:(h + 1) * HD],
                               kv_ref[:, XA + h * HD:XA + (h + 1) * HD])
            gs = slice(O_GATE + 2 * BW + h * HD, O_GATE + 2 * BW + (h + 1) * HD)
            gate, _ = _silu_parts(p_ref[:, gs])
            ybuf[:, 2 * BW + h * HD:2 * BW + (h + 1) * HD] = (yx * gate).astype(BF16)

        o = _dot(ybuf[...], wout_ref[...])
        o_ref[...] = o
        err = _rms_residual(x_ref[...], o, pg_ref[...]) - tgt_ref[...]
        lacc[...] += _colsum(err * err)
        dres_ref[...] = err * (1.0 / D)

        @pl.when(i == nt - 1)
        def _():
            loss_ref[...] = jnp.full((1, HD), jnp.sum(lacc[...]) * (0.5 / D), F32)

    tile = lambda n: pl.BlockSpec((tm, n), lambda i: (i, 0))
    ext = pltpu.VMEM((tm + HALO, BW), F32)
    return pl.pallas_call(
        body, name="odd_fwd", grid=(nt,),
        in_specs=[tile(D), tile(D), tile(ODD_IN), _halo_spec(ODD_IN, nt, False, tm)] + [_whole()] * 11,
        out_specs=[tile(D), tile(D), _full((1, HD)), tile(BW)],
        out_shape=[S((t, D), F32), S((t, D), F32), S((1, HD), F32), S((t, BW), F32)],
        scratch_shapes=[pltpu.VMEM((tm, MIX), BF16), ext, ext, ext, ext, ext,
                        pltpu.VMEM((1, D), F32), pltpu.VMEM((7, tm + HALO, BW), F32)],
        compiler_params=_params(("arbitrary",)),
    )(x1, tgt, p, p, wbd, cscale, dww, dwb, lng, lnb, wpw, pwb, kv, wout, pg)


def _odd_bwd(dres, o, p, conv, wbd, cscale, dww, dwb, lng, lnb, wpw, pwb, kv, wout, pg):
    t = dres.shape[0]
    tm = min(TM_BWD, t)
    nt = t // tm

    def body(dres_ref, o_ref, p_ref, ph_ref, conv_ref, wbd_ref, cs_ref, dww_ref, dwb_ref, lng_ref, lnb_ref, wpw_ref,
             pwb_ref, kv_ref, wout_ref, pg_ref, dp_ref, y_ref, do_ref, dpg_ref, dwbd_ref, dcs_ref, ddww_ref, ddwb_ref,
             dlng_ref, dlnb_ref, dwpw_ref, dpwb_ref, dkv_ref,
             dy, za, zb, zc, zd, gbuf, hbuf, tmp, carry_e, carry_d, wacc, gsh, hsh):
        i = pl.program_id(0)
        ti = nt - 1 - i
        masks = _group_masks()

        @pl.when(i == 0)
        def _():
            for ref in (dpg_ref, dwbd_ref, dcs_ref, ddwb_ref, dlng_ref, dlnb_ref, dwpw_ref, dpwb_ref, dkv_ref, wacc):
                ref[...] = jnp.zeros_like(ref)

        do, dpg = _rms_post_bwd(dres_ref[...], o_ref[...], pg_ref[...])
        dpg_ref[...] += dpg
        do_b = do.astype(BF16)
        do_ref[...] = do_b
        dy[...] = _dot_nt(do_b, wout_ref[...])

        z = p_ref[:, O_ZC:O_ZC + BW]
        za[0:HALO] = jnp.where(ti > 0, ph_ref[:, O_ZC:O_ZC + BW], 0.0)
        za[HALO:HALO + tm] = z
        inv = _pool_weights(ti, masks, tm)
        pooled_b = (_mix4(masks, _pool_causal(za, zb, zc, zd, tm)) * inv - z).astype(BF16)
        pm = _dot(pooled_b, wbd_ref[...])
        gate, dgate = _silu_parts(p_ref[:, O_GATE:O_GATE + BW])
        dyc = dy[:, 0:BW]
        yc = pm * cs_ref[...]
        y_ref[:, 0:BW] = (yc * gate).astype(BF16)
        dp_ref[:, O_GATE:O_GATE + BW] = (dyc * yc * dgate).astype(BF16)
        dyc = dyc * gate
        dcs_ref[...] += _colsum(dyc * pm)
        dpm_b = (dyc * cs_ref[...]).astype(BF16)
        dwbd_ref[...] += _dot_tn(pooled_b, dpm_b)
        dpool = _dot_nt(dpm_b, wbd_ref[...])
        e = dpool * inv
        za[0:tm] = e
        za[tm:tm + HALO] = jnp.where(i > 0, carry_e[...], 0.0)
        carry_e[...] = e[0:HALO]
        dp_ref[:, O_ZC:O_ZC + BW] = (_mix4(masks, _pool_anticausal(za, zb, zc, zd, tm)) - dpool).astype(BF16)

        gbuf[0:HALO] = jnp.where(ti > 0, ph_ref[:, O_GA:O_GA + BW] * _sigmoid(ph_ref[:, O_GB:O_GB + BW]), 0.0)
        gbuf[HALO:HALO + tm] = p_ref[:, O_GA:O_GA + BW] * _sigmoid(p_ref[:, O_GB:O_GB + BW])
        _shift_copies(gbuf, gsh)
        zh, rs = _ln_stats(conv_ref[...])
        zn = zh * lng_ref[...] + lnb_ref[...]
        zs, dsilu = _silu_parts(zn)
        zs_b = zs.astype(BF16)
        yd = _dot(zs_b, wpw_ref[...]) + pwb_ref[...]
        gate, dgate = _silu_parts(p_ref[:, O_GATE + BW:O_GATE + 2 * BW])
        dyc = dy[:, BW:2 * BW]
        y_ref[:, BW:2 * BW] = (yd * gate).astype(BF16)
        dp_ref[:, O_GATE + BW:O_GATE + 2 * BW] = (dyc * yd * dgate).astype(BF16)
        dyd = dyc * gate
        dyd_b = dyd.astype(BF16)
        dpwb_ref[...] += _colsum(dyd)
        dwpw_ref[...] += _dot_tn(zs_b, dyd_b)
        dzn = _dot_nt(dyd_b, wpw_ref[...]) * dsilu
        dlng_ref[...] += _colsum(dzn * zh)
        dlnb_ref[...] += _colsum(dzn)
        dzd = _ln_bwd(dzn, zh, rs, lng_ref[...])
        ddwb_ref[...] += _colsum(dzd)
        tmp[...] = dzd
        hbuf[0:tm] = dzd
        hbuf[tm:tm + HALO] = jnp.where(i > 0, carry_d[...], 0.0)
        carry_d[...] = dzd[0:HALO]
        _shift_copies(hbuf, hsh)
        _tap_grads(tmp, gbuf, gsh, wacc, CONF_K)
        for rb, dzg in enumerate(_taps_anticausal(hbuf, hsh, dww_ref, CONF_K)):
            sl = slice(rb * 32, (rb + 1) * 32)
            sgb = _sigmoid(p_ref[sl, O_GB:O_GB + BW])
            dp_ref[sl, O_GA:O_GA + BW] = (dzg * sgb).astype(BF16)
            dp_ref[sl, O_GB:O_GB + BW] = (dzg * p_ref[sl, O_GA:O_GA + BW] * sgb * (1.0 - sgb)).astype(BF16)

        for h in range(NH):
            qs = slice(O_Q + h * HD, O_Q + (h + 1) * HD)
            ks = slice(h * HD, (h + 1) * HD)
            vs = slice(XA + h * HD, XA + (h + 1) * HD)
            gs = slice(O_GATE + 2 * BW + h * HD, O_GATE + 2 * BW + (h + 1) * HD)
            ys = slice(2 * BW + h * HD, 2 * BW + (h + 1) * HD)
            q_b = p_ref[:, qs].astype(BF16)
            prob, yx = _attn_head(q_b, kv_ref[:, ks], kv_ref[:, vs])
            gate, dgate = _silu_parts(p_ref[:, gs])
            dyc = dy[:, ys]
            y_ref[:, ys] = (yx * gate).astype(BF16)
            dp_ref[:, gs] = (dyc * yx * dgate).astype(BF16)
            dyx_b = (dyc * gate).astype(BF16)
            dprob = _dot_nt(dyx_b, kv_ref[:, vs])
            dkv_ref[:, vs] += _dot_tn(prob.astype(BF16), dyx_b)
            ds_b = (prob * (dprob - jnp.sum(dprob * prob, axis=-1, keepdims=True)) * (HD ** -0.5)).astype(BF16)
            dp_ref[:, qs] = _dot(ds_b, kv_ref[:, ks]).astype(BF16)
            dkv_ref[:, ks] += _dot_tn(ds_b, q_b)

        @pl.when(i == nt - 1)
        def _():
            for k in range(CONF_K):
                ddww_ref[k:k + 1, :] = _colsum(wacc[k * 8:(k + 1) * 8, :])
            ddww_ref[CONF_K:CONF_K + 1, :] = jnp.zeros((1, BW), F32)

    rtile = lambda n: pl.BlockSpec((tm, n), lambda i: (nt - 1 - i, 0))
    outs = [S((t, ODD_IN), BF16), S((t, MIX), BF16), S((t, D), BF16), S((1, D), F32), S((BW, BW), F32),
            S((1, BW), F32), S((CONF_K + 1, BW), F32), S((1, BW), F32), S((1, BW), F32), S((1, BW), F32),
            S((BW, BW), F32), S((1, BW), F32), S((N_MEM, 2 * XA), F32)]
    ext = pltpu.VMEM((tm + HALO, BW), F32)
    return pl.pallas_call(
        body, name="odd_bwd", grid=(nt,),
        in_specs=[rtile(D), rtile(D), rtile(ODD_IN), _halo_spec(ODD_IN, nt, True, tm), rtile(BW)] + [_whole()] * 11,
        out_specs=[rtile(ODD_IN), rtile(MIX), rtile(D)] + [_full(s.shape) for s in outs[3:]],
        out_shape=outs,
        scratch_shapes=[pltpu.VMEM((tm, MIX), F32), ext, ext, ext, ext, ext, ext, pltpu.VMEM((tm, BW), F32),
                        pltpu.VMEM((HALO, BW), F32), pltpu.VMEM((HALO, BW), F32), pltpu.VMEM((CONF_K * 8, BW), F32),
                        pltpu.VMEM((7, tm + HALO, BW), F32), pltpu.VMEM((7, tm + HALO, BW), F32)],
        compiler_params=_params(("arbitrary",)),
    )(dres, o, p, p, conv, wbd, cscale, dww, dwb, lng, lnb, wpw, pwb, kv, wout, pg)


def _pick_rows(n):
    for rows in (640, 2432, 1024, 768):
        if n % rows == 0:
            return rows
    return n


def _pad_rows(a, rows):
    return jnp.pad(a, ((0, rows - a.shape[0]), (0, 0)))


def _step(x, mem, tgt, ex):
    t = x.shape[0]
    tm = min(256, t)
    w, deps = ex.first()
    causal = jnp.tril(jnp.ones((CHUNK, CHUNK), bool))
    ws = jnp.where(causal[None], w["even_a_ws"], 0.0).astype(BF16)
    wst = jnp.transpose(ws, (0, 2, 1))
    bmap = jnp.repeat(w["even_a_bs"].T, GRP, axis=1)
    wc = _pad_rows(w["even_b_conv"], 8)
    wbd = jax.scipy.linalg.block_diag(*[w["odd_c_wgrp"][g] for g in range(NH)]).astype(BF16)
    dww = _pad_rows(w["odd_d_dw_w"], CONF_K + 1)
    tk = min(1024, t)
    zeros = jnp.zeros_like(mem)

    p_e, h_e = _rms_matmul(x, w["even_pre_g"], w["even_w_in"], tm=tm, name="in_even", transposed=True, deps=deps)
    w.update(ex.even_rest(h_e))
    kv_e, memn_e = _rms_matmul(mem, w["even_mem_g"], w["even_w_kv"], tm=N_MEM, name="kv_even")
    kv_e = kv_e.astype(BF16)
    even_args = (w["even_a_ln_g"], w["even_a_ln_b"], ws)
    o_e, x1 = _even_fwd(x, p_e, *even_args, bmap, wc, kv_e, w["even_w_out"], w["even_post_g"])
    w.update(ex.odd(o_e))
    kv_o, memn_o = _rms_matmul(mem, w["odd_mem_g"], w["odd_w_kv"], tm=N_MEM, name="kv_odd")
    kv_o = kv_o.astype(BF16)
    p_o, h_o = _rms_matmul(x1, w["odd_pre_g"], w["odd_w_in"], tm=tm, name="in_odd", transposed=True)
    odd_args = (wbd, w["odd_c_scale"], dww, w["odd_d_dw_b"], w["odd_d_ln_g"], w["odd_d_ln_b"], w["odd_d_pw_w"],
                w["odd_d_pw_b"], kv_o, w["odd_w_out"], w["odd_post_g"])
    o_o, dres, loss, conv_o = _odd_fwd(x1, tgt, p_o, *odd_args)

    g = {}
    (dp_o, y_o, do_o, post_g_o, dwbd, g["odd_c_scale"], ddww, g["odd_d_dw_b"], g["odd_d_ln_g"], g["odd_d_ln_b"],
     dwpw, g["odd_d_pw_b"], dkv_o) = _odd_bwd(dres, o_o, p_o, conv_o, *odd_args)
    g["odd_post_g"] = post_g_o
    g["odd_d_dw_w"] = ddww[:CONF_K]
    dkv_o = dkv_o.astype(BF16)
    deps = ex.send("odd_rest", {
        "odd_w_out": _tn_matmul(y_o, do_o, tmc=MIX, tk=tk, out_dtype=BF16, name="dw_out_odd"),
        "odd_w_kv": _tn_matmul(memn_o, dkv_o, tmc=D, tk=N_MEM, out_dtype=BF16, name="dw_kv_odd"),
        "odd_d_pw_w": dwpw.astype(BF16), "loss": loss,
        "odd_c_wgrp": jnp.concatenate([dwbd[i * GRP:(i + 1) * GRP, i * GRP:(i + 1) * GRP] for i in range(NH)])})
    deps = ex.send("odd_in", {"odd_w_in": _tn_matmul(dp_o, h_o, tmc=_pick_rows(ODD_IN), tk=tk, out_dtype=BF16,
                                                     name="dw_in_odd", deps=deps)})
    dx1, g["odd_pre_g"] = _nt_matmul_rms_bwd(dp_o, w["odd_w_in"], x1, w["odd_pre_g"], dres, tm=min(256, t),
                                             name="dx_odd", transposed=True, deps=deps)
    _, g["odd_mem_g"] = _nt_matmul_rms_bwd(dkv_o, w["odd_w_kv"], mem, w["odd_mem_g"], zeros, tm=N_MEM,
                                           name="dmem_odd")

    (dp_e, y_e, do_e, post_g_e, dws, dbs, ln_g_e, ln_b_e, dwc, dkv_e) = _even_bwd(
        dx1, o_e, p_e, *even_args, wst, bmap, wc, kv_e, w["even_w_out"], w["even_post_g"])
    g["even_b_conv"] = dwc[:3]
    dkv_e = dkv_e.astype(BF16)
    g["even_w_in"] = _tn_matmul(dp_e, h_e, tmc=_pick_rows(EVEN_IN), tk=tk, out_dtype=BF16, name="dw_in_even")
    deps = ex.send("even_in", g)
    grad_x, pre_g_e = _nt_matmul_rms_bwd(dp_e, w["even_w_in"], x, w["even_pre_g"], dx1, tm=min(256, t),
                                         name="dx_even", transposed=True, deps=deps)
    deps = ex.send("even_rest", {
        "even_w_out": _tn_matmul(y_e, do_e, tmc=MIX, tk=tk, out_dtype=BF16, name="dw_out_even", deps=(grad_x,)),
        "even_w_kv": _tn_matmul(memn_e, dkv_e, tmc=D, tk=N_MEM, out_dtype=BF16, name="dw_kv_even"),
        "even_a_ln_g": ln_g_e, "even_a_ln_b": ln_b_e,
        "even_a_ws": dws.reshape(NH * CHUNK, CHUNK), "even_a_bs": dbs[:, ::HD].T})
    _, mem_g_e = _nt_matmul_rms_bwd(dkv_e, w["even_w_kv"], mem, w["even_mem_g"], zeros, tm=N_MEM, name="dmem_even",
                                    deps=deps)
    ex.send("even_gains", {"even_pre_g": pre_g_e, "even_mem_g": mem_g_e, "even_post_g": post_g_e})
    return grad_x, mem_g_e


def _place():
    return lax.axis_index("x"), lax.axis_index("y"), lax.axis_index("c")


def _index(px, py, pc):
    return 4 * px + 2 * py + pc


_COPIES = N_DEV - 1


def _all_gather(arrs, name):
    n = len(arrs)

    def body(*refs):
        ins, outs = refs[:n], refs[n:2 * n]
        send_sems, recv_sems, local_sems = refs[2 * n:]
        x, y, c = _place()
        me, sibling = (x, y, c), (x, y, 1 - c)
        chips = [(1 - x, y), (x, 1 - y), (1 - x, 1 - y)]

        def copy(a, k, block, to, src=None):
            dst = outs[a].at[_index(*block)]
            return pltpu.make_async_remote_copy(
                src_ref=dst if src is None else src, dst_ref=dst, send_sem=send_sems.at[a * _COPIES + k],
                recv_sem=recv_sems.at[a * _COPIES + k], device_id=to, device_id_type=MESH)

        mine = [pltpu.make_async_copy(ins[a], outs[a].at[_index(*me)], local_sems.at[a]) for a in range(n)]
        first = []
        for a in range(n):
            mine[a].start()
            first.append(copy(a, 0, me, sibling, src=ins[a]))
            first += [copy(a, 1 + j, me, (*chip, c), src=ins[a]) for j, chip in enumerate(chips)]
        for cp in first:
            cp.start()
        passed = []
        for j, chip in enumerate(chips):
            for a in range(n):
                copy(a, 1 + j, (*chip, c), me).wait_recv()
                passed.append(copy(a, 4 + j, (*chip, c), sibling))
                passed[-1].start()
        for a in range(n):
            copy(a, 0, sibling, me).wait_recv()
            for j, chip in enumerate(chips):
                copy(a, 4 + j, (*chip, 1 - c), me).wait_recv()
        for cp in first + passed:
            cp.wait_send()
        for cp in mine:
            cp.wait()

    return pl.pallas_call(
        body, name=name, in_specs=[_ANY] * n, out_specs=[_ANY] * n,
        out_shape=[S((N_DEV,) + a.shape, a.dtype) for a in arrs],
        scratch_shapes=[pltpu.SemaphoreType.DMA((n * _COPIES,)), pltpu.SemaphoreType.DMA((n * _COPIES,)),
                        pltpu.SemaphoreType.DMA((n,))],
    )(*arrs)


_HBM = pl.BlockSpec(memory_space=pltpu.HBM)
_SEM = pl.BlockSpec(memory_space=pltpu.SEMAPHORE)
_EFFECT = pltpu.SideEffectType.DATAFLOW_SIDE_EFFECTING


def _exchange_copies(kinds, srcs, lands, send_sems, recv_sems, local_sems, arriving):
    x, y, c = _place()
    mine = _index(x, y, c)
    flips = [(k >> 2 & 1, k >> 1 & 1, k & 1) for k in range(1, N_DEV)]
    peers = [(1 - x if fx else x, 1 - y if fy else y, 1 - c if fc else c) for fx, fy, fc in flips]
    remote, local = [], []
    for a, kind in enumerate(kinds):
        own = srcs[a] if kind == "gather" else srcs[a].at[mine]
        local.append(pltpu.make_async_copy(own, lands[a].at[mine], local_sems.at[a]))
        for k, peer in enumerate(peers):
            there = _index(*peer)
            remote.append(pltpu.make_async_remote_copy(
                src_ref=srcs[a] if kind == "gather" else srcs[a].at[there],
                dst_ref=lands[a].at[there if arriving else mine],
                send_sem=send_sems.at[a * _COPIES + k], recv_sem=recv_sems.at[a * _COPIES + k],
                device_id=peer, device_id_type=MESH))
    return remote, local


def _exchange_start(items, name, deps=()):
    kinds = [kind for kind, _ in items]
    srcs = [a for _, a in items]
    n = len(items)
    lands = [lax.empty((N_DEV,) + (a.shape if kind == "gather" else a.shape[1:]), a.dtype) for kind, a in items]

    def body(*refs):
        send_sems, recv_sems, local_sems = refs[2 * n + len(deps):2 * n + len(deps) + 3]
        remote, local = _exchange_copies(kinds, refs[:n], refs[n:2 * n], send_sems, recv_sems, local_sems, False)
        for cp in local + remote:
            cp.start()
        refs[-1][...] = jnp.zeros_like(refs[-1])

    held = [pltpu.HBM(a.shape, a.dtype) for a in srcs + lands]
    res = pl.pallas_call(
        body, name=name,
        out_shape=(pltpu.SemaphoreType.DMA((n * _COPIES,)), pltpu.SemaphoreType.DMA((n * _COPIES,)),
                   pltpu.SemaphoreType.DMA((n,)), *held, S((8, 128), F32)),
        in_specs=[_HBM] * (2 * n) + [_ANY] * len(deps),
        out_specs=(_SEM, _SEM, _SEM, *[_HBM] * (2 * n), _whole()),
        input_output_aliases={i: 3 + i for i in range(2 * n)},
        compiler_params=pltpu.CompilerParams(has_side_effects=_EFFECT),
    )(*[pltpu.with_memory_space_constraint(a, pltpu.HBM) for a in srcs + lands], *deps)
    return (kinds, res[:3], res[3:3 + 2 * n]), res[-1]


def _exchange_wait(handle, after, name):
    kinds, sems, held = handle
    n = len(kinds)

    def body(*refs):
        send_sems, recv_sems, local_sems = refs[2 * n:2 * n + 3]
        remote, local = _exchange_copies(kinds, refs[:n], refs[n:2 * n], send_sems, recv_sems, local_sems, True)
        for cp in remote:
            cp.wait_send()
            cp.wait_recv()
        for cp in local:
            cp.wait()

    res = pl.pallas_call(
        body, name=name, out_shape=[pltpu.HBM(a.shape, a.dtype) for a in held],
        in_specs=[_HBM] * (2 * n) + [_SEM] * 3 + [_ANY] * len(after), out_specs=[_HBM] * (2 * n),
        input_output_aliases={i: i for i in range(2 * n)},
        compiler_params=pltpu.CompilerParams(has_side_effects=_EFFECT),
    )(*held, *sems, *after)
    return res[n:]


def _adamw(w, g, m, v):
    m = ADAM_B1 * m + (1.0 - ADAM_B1) * g
    v = ADAM_B2 * v + (1.0 - ADAM_B2) * (g * g)
    m_hat = m / (1.0 - ADAM_B1 ** ADAM_STEP)
    v_hat = v / (1.0 - ADAM_B2 ** ADAM_STEP)
    return -ADAM_LR * (m_hat / (jnp.sqrt(v_hat) + ADAM_EPS) + ADAM_WD * w), m, v


def _sum_devices(ref, rows):
    total = ref[0, rows, :].astype(F32)
    for s in range(1, N_DEV):
        total = total + ref[s, rows, :].astype(F32)
    return total


def _adam_big(recv, w, m, v, *, tr, name):
    r, c = w.shape

    def body(recv_ref, w_ref, m_ref, v_ref, g_ref, d_ref, m2_ref, v2_ref):
        g = _sum_devices(recv_ref, slice(None))
        g_ref[...] = g
        d_ref[...], m2_ref[...], v2_ref[...] = _adamw(w_ref[...], g, m_ref[...], v_ref[...])

    blk = pl.BlockSpec((tr, c), lambda i: (i, 0))
    return pl.pallas_call(
        body, name=name, grid=(r // tr,),
        in_specs=[pl.BlockSpec((N_DEV, tr, c), lambda i: (0, i, 0)), blk, blk, blk],
        out_specs=[blk] * 4, out_shape=[S((r, c), F32)] * 4,
        compiler_params=_params(("arbitrary",)),
    )(recv, w, m, v)


_REPLICATED = {"even_pre_g": (0, 0, 1), "even_mem_g": (0, 8, 1), "even_post_g": (0, 16, 1),
               "even_a_ln_g": (1, 0, 1), "even_a_ln_b": (1, 8, 1),
               "even_a_ws": (2, 0, NH * CHUNK), "even_a_bs": (2, NH * CHUNK, NH),
               "odd_c_wgrp": (3, 0, NH * GRP)}
_SHARDED = {"odd_pre_g": (4, 0, 1), "odd_mem_g": (4, 8, 1), "odd_post_g": (4, 16, 1),
            "even_b_conv": (5, 0, 3), "odd_c_scale": (5, 8, 1), "odd_d_dw_w": (5, 16, CONF_K),
            "odd_d_dw_b": (5, 48, 1), "odd_d_ln_g": (5, 56, 1), "odd_d_ln_b": (5, 64, 1), "odd_d_pw_b": (5, 72, 1)}
_SMALL = {**_REPLICATED, **_SHARDED}
_SMALL_ROWS = {0: 24, 1: 16, 2: NH * CHUNK + 8, 3: NH * GRP, 4: 24, 5: 80}


def _adam_small(sources, wmv):
    names = list(_SMALL)
    ns = len(sources)

    def body(*refs):
        src = refs[:ns]
        ins = refs[ns:ns + 3 * len(names)]
        outs = refs[ns + 3 * len(names):]
        outs[-1][...] = _sum_devices(src[-1], slice(0, 1))
        for i, nm in enumerate(names):
            a, row0, rows = _SMALL[nm]
            g = _sum_devices(src[a], slice(row0, row0 + rows))
            w_ref, m_ref, v_ref = ins[3 * i:3 * i + 3]
            g_ref, d_ref, m2_ref, v2_ref = outs[4 * i:4 * i + 4]
            g_ref[...] = g
            d_ref[...], m2_ref[...], v2_ref[...] = _adamw(w_ref[...], g, m_ref[...], v_ref[...])

    flat = [t for nm in names for t in wmv[nm]]
    out_shape = [S(wmv[nm][0].shape, F32) for nm in names for _ in range(4)] + [S((1, HD), F32)]
    res = pl.pallas_call(
        body, name="adam_small", in_specs=[_whole()] * (ns + len(flat)), out_specs=[_whole()] * len(out_shape),
        out_shape=out_shape, compiler_params=_params(),
    )(*sources, *flat)
    return {nm: tuple(res[4 * i:4 * i + 4]) for i, nm in enumerate(names)}, res[-1]


_WEIGHTS = ["even_pre_g", "even_w_in", "even_a_ln_g", "even_a_ln_b", "even_a_ws", "even_a_bs", "even_b_conv",
            "even_mem_g", "even_w_kv", "even_w_out", "even_post_g", "odd_pre_g", "odd_w_in", "odd_c_wgrp",
            "odd_c_scale", "odd_d_dw_w", "odd_d_dw_b", "odd_d_ln_g", "odd_d_ln_b", "odd_d_pw_w", "odd_d_pw_b",
            "odd_mem_g", "odd_w_kv", "odd_w_out", "odd_post_g"]
_TRANSPOSED = ["even_w_in", "odd_w_in"]
_BIG = _TRANSPOSED + ["even_w_kv", "even_w_out", "odd_w_kv", "odd_w_out", "odd_d_pw_w"]
_BIG_TILE_ROWS = {"even_w_in": 400, "odd_w_in": 304, "even_w_kv": 128, "even_w_out": 128, "odd_w_kv": 128,
                  "odd_w_out": 128, "odd_d_pw_w": 96}


def _view2d(a, transposed):
    a = a[0]
    if a.ndim == 1:
        return a[None]
    if transposed:
        return a.T
    return a.reshape(-1, a.shape[-1])


def _rows8(a):
    return _pad_rows(a, -(-a.shape[0] // 8) * 8)


def _pack_rows(parts):
    return jnp.concatenate([_rows8(p) for p in parts], axis=0)


def _unshard_cols(a):
    return jnp.transpose(a, (1, 0, 2)).reshape(a.shape[1], N_DEV * a.shape[2])


def _shard_cols(a):
    return jnp.transpose(a.reshape(a.shape[0], N_DEV, a.shape[1] // N_DEV), (1, 0, 2))


def _rows_of(a):
    return a.reshape(-1, a.shape[-1])


_GROUPS = {"odd_rest": (["odd_w_out", "odd_w_kv", "odd_d_pw_w"], [3], []),
           "odd_in": (["odd_w_in"], [], []),
           "even_in": (["even_w_in"], [], [4, 5]),
           "even_rest": (["even_w_out", "even_w_kv"], [1, 2], []),
           "even_gains": ([], [0], [])}


class _MeshExchange:
    def __init__(self, shard):
        self.shard = shard
        self.handles = {}

    def first(self):
        shard = self.shard
        packs = [_pack_rows([shard[nm] for nm in _SHARDED if _SHARDED[nm][0] == a]) for a in (4, 5)]
        w_in, p128, p96 = _all_gather([shard["even_w_in"].astype(BF16)] + packs, "gather_first")
        w = {nm: shard[nm] for nm in _REPLICATED}
        w["even_a_ws"] = w["even_a_ws"].reshape(NH, CHUNK, CHUNK)
        w["odd_c_wgrp"] = w["odd_c_wgrp"].reshape(NH, GRP, GRP)
        w["even_w_in"] = _rows_of(w_in)
        full_packs = {4: _unshard_cols(p128), 5: _unshard_cols(p96)}
        for nm, (a, row0, rows) in _SHARDED.items():
            w[nm] = full_packs[a][row0:row0 + rows]
        later = lambda names: [("gather", shard[nm].astype(BF16)) for nm in names]
        self.handles["w_even"], token = _exchange_start(later(["even_w_kv", "even_w_out"]), "gather_even_start",
                                                        deps=(w_in,))
        self.handles["w_odd"], token = _exchange_start(later(["odd_w_in", "odd_w_kv", "odd_w_out", "odd_d_pw_w"]),
                                                       "gather_odd_start", deps=(token,))
        return w, (token,)

    def even_rest(self, after):
        kv, out = _exchange_wait(self.handles.pop("w_even"), (after,), "gather_even_wait")
        return {"even_w_kv": _rows_of(kv), "even_w_out": _rows_of(out)}

    def odd(self, after):
        w_in, kv, out, pw = _exchange_wait(self.handles.pop("w_odd"), (after,), "gather_odd_wait")
        return {"odd_w_in": _rows_of(w_in), "odd_w_kv": _rows_of(kv), "odd_w_out": _rows_of(out),
                "odd_d_pw_w": _rows_of(pw)}

    def send(self, group, g):
        big, replicated, sharded = _GROUPS[group]
        items = [("scatter", g[nm].reshape(N_DEV, -1, g[nm].shape[-1])) for nm in big]
        items += [("gather", _pack_rows([g[nm] for nm in _REPLICATED if _REPLICATED[nm][0] == a]))
                  for a in replicated]
        items += [("scatter", _shard_cols(_pack_rows([g[nm] for nm in _SHARDED if _SHARDED[nm][0] == a])))
                  for a in sharded]
        if group == "odd_rest":
            items.append(("gather", _rows8(g["loss"])))
        self.handles[group], token = _exchange_start(items, "send_" + group + "_start")
        return (token,)

    def receive(self, group, after):
        return _exchange_wait(self.handles.pop(group), (after,), "send_" + group + "_wait")


def kernel(x, mem, even_pre_g, even_w_in, even_a_ln_g, even_a_ln_b, even_a_ws, even_a_bs, even_b_conv, even_mem_g, even_w_kv, even_w_out, even_post_g, odd_pre_g, odd_w_in, odd_c_wgrp, odd_c_scale, odd_d_dw_w, odd_d_dw_b, odd_d_ln_g, odd_d_ln_b, odd_d_pw_w, odd_d_pw_b, odd_mem_g, odd_w_kv, odd_w_out, odd_post_g, loss_target, m_even_pre_g, m_even_w_in, m_even_a_ln_g, m_even_a_ln_b, m_even_a_ws, m_even_a_bs, m_even_b_conv, m_even_mem_g, m_even_w_kv, m_even_w_out, m_even_post_g, m_odd_pre_g, m_odd_w_in, m_odd_c_wgrp, m_odd_c_scale, m_odd_d_dw_w, m_odd_d_dw_b, m_odd_d_ln_g, m_odd_d_ln_b, m_odd_d_pw_w, m_odd_d_pw_b, m_odd_mem_g, m_odd_w_kv, m_odd_w_out, m_odd_post_g, v_even_pre_g, v_even_w_in, v_even_a_ln_g, v_even_a_ln_b, v_even_a_ws, v_even_a_bs, v_even_b_conv, v_even_mem_g, v_even_w_kv, v_even_w_out, v_even_post_g, v_odd_pre_g, v_odd_w_in, v_odd_c_wgrp, v_odd_c_scale, v_odd_d_dw_w, v_odd_d_dw_b, v_odd_d_ln_g, v_odd_d_ln_b, v_odd_d_pw_w, v_odd_d_pw_b, v_odd_mem_g, v_odd_w_kv, v_odd_w_out, v_odd_post_g):
    given = dict(locals())
    view = lambda nm, kind: _view2d(given[kind + nm], nm in _TRANSPOSED)
    shard = {nm: view(nm, "") for nm in _WEIGHTS}
    wmv = {nm: (shard[nm], view(nm, "m_"), view(nm, "v_")) for nm in _WEIGHTS}

    ex = _MeshExchange(shard)
    grad_x, last = _step(x[0], mem[0], loss_target[0], ex)

    res = {}

    def update(group, after):
        names = _GROUPS[group][0]
        landed = ex.receive(group, after)
        for nm, recv in zip(names, landed):
            res[nm] = _adam_big(recv, *wmv[nm], tr=_BIG_TILE_ROWS[nm], name="adam_" + nm)
        return landed[len(names):]

    c192, losses = update("odd_rest", last)
    update("odd_in", res["odd_d_pw_w"][0])
    a128, a96 = update("even_in", res["odd_w_in"][0])
    c768, c128 = update("even_rest", res["even_w_in"][0])
    (c1024,) = update("even_gains", res["even_w_kv"][0])
    small, loss = _adam_small([c1024, c768, c128, c192, a128, a96, losses], {nm: wmv[nm] for nm in _SMALL})
    res.update(small)
    total = loss[0, 0]
    back = lambda nm, a: (a.T if nm in _TRANSPOSED else a).reshape(given[nm].shape)
    outs = [[back(nm, res[nm][i]) for nm in _WEIGHTS] for i in range(4)]
    return (total, grad_x[None], *outs[0], *outs[1], *outs[2], *outs[3])
```

```python
import functools

import jax
import jax.numpy as jnp
from jax import lax
from jax.experimental import pallas as pl
from jax.experimental.pallas import tpu as pltpu

F32 = jnp.float32
BF16 = jnp.bfloat16
S = jax.ShapeDtypeStruct
MESH = pl.DeviceIdType.MESH
AXES = ("x", "y", "c")
N_DEV = 8

D = 1024
BW = 768
XA = 512
HD = 128
NH = 4
MIX = 2048
CHUNK = 128
GRP = 192
N_MEM = 256
CONF_K = 31
EPS = 1e-6
HALO = 32
POOL_WINDOWS = (2, 4, 8, 16)
TM_FWD = 256
TM_BWD = 128

E_U, E_V, E_BG, E_CG, E_XIN, E_Q, E_GATE = 0, 768, 1536, 2304, 3072, 3840, 4352
EVEN_IN = 6400
O_ZC, O_GA, O_GB, O_Q, O_GATE = 0, 768, 1536, 2304, 2816
ODD_IN = 4864

ADAM_LR, ADAM_B1, ADAM_B2, ADAM_EPS, ADAM_WD, ADAM_STEP = 0.001, 0.9, 0.999, 1e-08, 0.01, 10

VMEM_LIMIT_V7X = 56 * 1024 * 1024


def _params(sem=None):
    return pltpu.CompilerParams(dimension_semantics=sem, vmem_limit_bytes=VMEM_LIMIT_V7X)


def _dot(a, b):
    return jnp.dot(a, b, preferred_element_type=F32)


def _dot_nt(a, b):
    return lax.dot_general(a, b, (((1,), (1,)), ((), ())), preferred_element_type=F32)


def _dot_tn(a, b):
    return lax.dot_general(a, b, (((0,), (0,)), ((), ())), preferred_element_type=F32)


def _sigmoid(z):
    return 1.0 / (1.0 + jnp.exp(-z))


def _rowmean(a):
    return jnp.mean(a, axis=-1, keepdims=True)


def _colsum(a):
    return jnp.sum(a, axis=0, keepdims=True)


def _ln_stats(v):
    mu = _rowmean(v)
    vc = v - mu
    rs = lax.rsqrt(_rowmean(vc * vc) + EPS)
    return vc * rs, rs


def _ln_bwd(dn, vh, rs, g):
    dvh = dn * g
    return rs * (dvh - _rowmean(dvh) - vh * _rowmean(dvh * vh))


def _group_masks():
    col = lax.broadcasted_iota(jnp.int32, (1, BW), 1)
    return [((col >= GRP * h) & (col < GRP * (h + 1))).astype(F32) for h in range(NH)]


def _full(shape):
    nd = len(shape)
    return pl.BlockSpec(shape, lambda *_: (0,) * nd)


def _whole():
    return pl.BlockSpec(memory_space=pltpu.VMEM)


_ANY = pl.BlockSpec(memory_space=pl.ANY)


def _after(body, n_in, deps):
    def ordered(*refs):
        return body(*refs[:n_in], *refs[n_in + len(deps):])
    return ordered


def _rms_matmul(x, g, w, *, tm, name, transposed=False, deps=()):
    t, d = x.shape
    n = w.shape[0] if transposed else w.shape[1]

    def body(x_ref, g_ref, w_ref, p_ref, h_ref):
        xv = x_ref[...]
        r = lax.rsqrt(_rowmean(xv * xv) + EPS)
        h = (xv * r * g_ref[...]).astype(BF16)
        h_ref[...] = h
        p_ref[...] = _dot_nt(h, w_ref[...]) if transposed else _dot(h, w_ref[...])

    return pl.pallas_call(
        _after(body, 3, deps), name=name, grid=(t // tm,),
        in_specs=[pl.BlockSpec((tm, d), lambda i: (i, 0)), _whole(), _whole()] + [_ANY] * len(deps),
        out_specs=[pl.BlockSpec((tm, n), lambda i: (i, 0)), pl.BlockSpec((tm, d), lambda i: (i, 0))],
        out_shape=[S((t, n), F32), S((t, d), BF16)],
        compiler_params=_params(("arbitrary",)),
    )(x, g, w, *deps)


def _nt_matmul_rms_bwd(dp, w, x, g, dres, *, tm, name, transposed=False, deps=()):
    t, n = dp.shape
    d = x.shape[1]

    def body(dp_ref, w_ref, x_ref, g_ref, dres_ref, dx_ref, dg_ref):
        @pl.when(pl.program_id(0) == 0)
        def _():
            dg_ref[...] = jnp.zeros_like(dg_ref)

        dh = _dot(dp_ref[...], w_ref[...]) if transposed else _dot_nt(dp_ref[...], w_ref[...])
        xv = x_ref[...]
        r = lax.rsqrt(_rowmean(xv * xv) + EPS)
        xh = xv * r
        dg_ref[...] += _colsum(dh * xh)
        dxh = dh * g_ref[...]
        dx_ref[...] = dres_ref[...] + r * (dxh - xh * _rowmean(dxh * xh))

    return pl.pallas_call(
        _after(body, 5, deps), name=name, grid=(t // tm,),
        in_specs=[pl.BlockSpec((tm, n), lambda i: (i, 0)), _whole(), pl.BlockSpec((tm, d), lambda i: (i, 0)),
                  _whole(), pl.BlockSpec((tm, d), lambda i: (i, 0))] + [_ANY] * len(deps),
        out_specs=[pl.BlockSpec((tm, d), lambda i: (i, 0)), pl.BlockSpec((1, d), lambda i: (0, 0))],
        out_shape=[S((t, d), F32), S((1, d), F32)],
        compiler_params=_params(("arbitrary",)),
    )(dp, w, x, g, dres, *deps)


def _tn_matmul(a, b, *, tmc, tk, out_dtype, name, deps=()):
    t, m = a.shape
    n = b.shape[1]
    nk = t // tk

    def body(a_ref, b_ref, o_ref, acc_ref):
        k = pl.program_id(1)

        @pl.when(k == 0)
        def _():
            acc_ref[...] = jnp.zeros_like(acc_ref)

        acc_ref[...] += _dot_tn(a_ref[...], b_ref[...])

        @pl.when(k == nk - 1)
        def _():
            o_ref[...] = acc_ref[...].astype(out_dtype)

    return pl.pallas_call(
        _after(body, 2, deps), name=name, grid=(m // tmc, nk),
        in_specs=[pl.BlockSpec((tk, tmc), lambda j, k: (k, j)), pl.BlockSpec((tk, n), lambda j, k: (k, 0))]
        + [_ANY] * len(deps),
        out_specs=pl.BlockSpec((tmc, n), lambda j, k: (j, 0)),
        out_shape=S((m, n), out_dtype),
        scratch_shapes=[pltpu.VMEM((tmc, n), F32)],
        compiler_params=_params(("arbitrary", "arbitrary")),
    )(a, b, *deps)


def _silu_parts(gt):
    sg = _sigmoid(gt)
    return gt * sg, sg * (1.0 + gt * (1.0 - sg))


def _attn_head(q_b, k_b, v_b):
    s = _dot_nt(q_b, k_b) * (HD ** -0.5)
    e = jnp.exp(s - jnp.max(s, axis=-1, keepdims=True))
    prob = e / jnp.sum(e, axis=-1, keepdims=True)
    return prob, _dot(prob.astype(BF16), v_b)


def _rms_residual(x, o, g):
    r = lax.rsqrt(_rowmean(o * o) + EPS)
    return x + o * r * g


def _rms_post_bwd(dres, o, g):
    r = lax.rsqrt(_rowmean(o * o) + EPS)
    oh = o * r
    doh = dres * g
    return r * (doh - oh * _rowmean(doh * oh)), _colsum(dres * oh)


def _sgu_chunk(vn_b, ws_ref, bmap_ref, masks):
    sg = bmap_ref[...]
    for h in range(NH):
        sg = sg + masks[h] * _dot(ws_ref[h], vn_b)
    return sg


def _shift_copies(buf, sh):
    n = buf.shape[0] - 8
    for b in range(1, 8):
        sh[b - 1, pl.ds(0, n), :] = buf[pl.ds(b, n), :]


def _rows_at(buf, sh, off):
    b = off % 8
    if b == 0 or sh is None:
        return buf[pl.ds(off, 32), :]
    return sh[b - 1, pl.ds(off - b, 32), :]


def _taps_causal(buf, sh, w_ref, taps, bias):
    rows = buf.shape[0] - HALO
    outs = []
    for rb in range(rows // 32):
        acc = None
        for k in range(taps):
            term = w_ref[k:k + 1, :] * _rows_at(buf, sh, rb * 32 + HALO - (taps - 1 - k))
            acc = term if acc is None else acc + term
        outs.append(acc if bias is None else acc + bias)
    return outs


def _taps_anticausal(buf, sh, w_ref, taps):
    rows = buf.shape[0] - HALO
    outs = []
    for rb in range(rows // 32):
        acc = None
        for k in range(taps):
            term = w_ref[k:k + 1, :] * _rows_at(buf, sh, rb * 32 + (taps - 1 - k))
            acc = term if acc is None else acc + term
        outs.append(acc)
    return outs


def _fold8(a):
    return a[0:8] + a[8:16] + a[16:24] + a[24:32]


def _tap_grads(d_ref, buf, sh, acc_ref, taps):
    rows = buf.shape[0] - HALO
    for rb in range(rows // 32):
        dv = d_ref[rb * 32:(rb + 1) * 32, :]
        for k in range(taps):
            prod = dv * _rows_at(buf, sh, rb * 32 + HALO - (taps - 1 - k))
            acc_ref[k * 8:(k + 1) * 8, :] += _fold8(prod)


def _halo_spec(n, nt, reverse, tm):
    per = tm // HALO
    if reverse:
        return pl.BlockSpec((HALO, n), lambda i: (jnp.maximum((nt - 1 - i) * per - 1, 0), 0))
    return pl.BlockSpec((HALO, n), lambda i: (jnp.maximum(i * per - 1, 0), 0))


def _even_fwd(x, p, lng, lnb, ws, bmap, wc, kv, wout, pg):
    t = x.shape[0]
    tm = min(TM_FWD, t)
    nt = t // tm

    def body(x_ref, p_ref, ph_ref, lng_ref, lnb_ref, ws_ref, bmap_ref, wc_ref, kv_ref, wout_ref, pg_ref,
             o_ref, x1_ref, ybuf, cbuf):
        i = pl.program_id(0)
        masks = _group_masks()
        vh, _ = _ln_stats(p_ref[:, E_V:E_V + BW])
        vn = vh * lng_ref[...] + lnb_ref[...]
        for c in range(tm // CHUNK):
            sl = slice(c * CHUNK, (c + 1) * CHUNK)
            sg = _sgu_chunk(vn[sl].astype(BF16), ws_ref, bmap_ref, masks)
            gate, _ = _silu_parts(p_ref[sl, E_GATE:E_GATE + BW])
            ybuf[sl, 0:BW] = (p_ref[sl, E_U:E_U + BW] * sg * gate).astype(BF16)

        cbuf[0:HALO] = jnp.where(i > 0, ph_ref[:, E_CG:E_CG + BW] * ph_ref[:, E_XIN:E_XIN + BW], 0.0)
        cbuf[HALO:HALO + tm] = p_ref[:, E_CG:E_CG + BW] * p_ref[:, E_XIN:E_XIN + BW]
        conv = _taps_causal(cbuf, None, wc_ref, 3, None)
        for rb, cv in enumerate(conv):
            sl = slice(rb * 32, (rb + 1) * 32)
            gate, _ = _silu_parts(p_ref[sl, E_GATE + BW:E_GATE + 2 * BW])
            ybuf[sl, BW:2 * BW] = (p_ref[sl, E_BG:E_BG + BW] * cv * gate).astype(BF16)

        for h in range(NH):
            qs = slice(E_Q + h * HD, E_Q + (h + 1) * HD)
            _, yx = _attn_head(p_ref[:, qs].astype(BF16), kv_ref[:, h * HD:(h + 1) * HD],
                               kv_ref[:, XA + h * HD:XA + (h + 1) * HD])
            gs = slice(E_GATE + 2 * BW + h * HD, E_GATE + 2 * BW + (h + 1) * HD)
            gate, _ = _silu_parts(p_ref[:, gs])
            ybuf[:, 2 * BW + h * HD:2 * BW + (h + 1) * HD] = (yx * gate).astype(BF16)

        o = _dot(ybuf[...], wout_ref[...])
        o_ref[...] = o
        x1_ref[...] = _rms_residual(x_ref[...], o, pg_ref[...])

    tile = lambda n: pl.BlockSpec((tm, n), lambda i: (i, 0))
    return pl.pallas_call(
        body, name="even_fwd", grid=(nt,),
        in_specs=[tile(D), tile(EVEN_IN), _halo_spec(EVEN_IN, nt, False, tm)] + [_whole()] * 8,
        out_specs=[tile(D), tile(D)],
        out_shape=[S((t, D), F32), S((t, D), F32)],
        scratch_shapes=[pltpu.VMEM((tm, MIX), BF16), pltpu.VMEM((tm + HALO, BW), F32)],
        compiler_params=_params(("arbitrary",)),
    )(x, p, p, lng, lnb, ws, bmap, wc, kv, wout, pg)


def _even_bwd(dres, o, p, lng, lnb, ws, wst, bmap, wc, kv, wout, pg):
    t = dres.shape[0]
    tm = min(TM_BWD, t)
    nt = t // tm

    def body(dres_ref, o_ref, p_ref, ph_ref, lng_ref, lnb_ref, ws_ref, wst_ref, bmap_ref, wc_ref, kv_ref, wout_ref,
             pg_ref, dp_ref, y_ref, do_ref, dpg_ref, dws_ref, dbs_ref, dlng_ref, dlnb_ref, dwc_ref, dkv_ref,
             dy, cbuf, gbuf, dconv, carry, dvn, dbmap, wacc):
        i = pl.program_id(0)
        ti = nt - 1 - i
        masks = _group_masks()

        @pl.when(i == 0)
        def _():
            for ref in (dpg_ref, dws_ref, dlng_ref, dlnb_ref, dkv_ref, dbmap, wacc):
                ref[...] = jnp.zeros_like(ref)

        do, dpg = _rms_post_bwd(dres_ref[...], o_ref[...], pg_ref[...])
        dpg_ref[...] += dpg
        do_b = do.astype(BF16)
        do_ref[...] = do_b
        dy[...] = _dot_nt(do_b, wout_ref[...])

        vh, rs = _ln_stats(p_ref[:, E_V:E_V + BW])
        vn = vh * lng_ref[...] + lnb_ref[...]
        for c in range(tm // CHUNK):
            sl = slice(c * CHUNK, (c + 1) * CHUNK)
            vn_b = vn[sl].astype(BF16)
            sg = _sgu_chunk(vn_b, ws_ref, bmap_ref, masks)
            u = p_ref[sl, E_U:E_U + BW]
            gate, dgate = _silu_parts(p_ref[sl, E_GATE:E_GATE + BW])
            dyc = dy[sl, 0:BW]
            ya = u * sg
            y_ref[sl, 0:BW] = (ya * gate).astype(BF16)
            dp_ref[sl, E_GATE:E_GATE + BW] = (dyc * ya * dgate).astype(BF16)
            dya = dyc * gate
            dp_ref[sl, E_U:E_U + BW] = (dya * sg).astype(BF16)
            dsg = dya * u
            dbmap[...] += dsg
            dsg_b = dsg.astype(BF16)
            acc = jnp.zeros((CHUNK, BW), F32)
            for h in range(NH):
                dws_ref[h] += _dot_nt((dsg * masks[h]).astype(BF16), vn_b)
                acc = acc + masks[h] * _dot(wst_ref[h], dsg_b)
            dvn[sl, :] = acc
        dn = dvn[...]
        dlng_ref[...] += _colsum(dn * vh)
        dlnb_ref[...] += _colsum(dn)
        dp_ref[:, E_V:E_V + BW] = _ln_bwd(dn, vh, rs, lng_ref[...]).astype(BF16)

        cbuf[0:HALO] = jnp.where(ti > 0, ph_ref[:, E_CG:E_CG + BW] * ph_ref[:, E_XIN:E_XIN + BW], 0.0)
        cbuf[HALO:HALO + tm] = p_ref[:, E_CG:E_CG + BW] * p_ref[:, E_XIN:E_XIN + BW]
        conv = _taps_causal(cbuf, None, wc_ref, 3, None)
        for rb, cv in enumerate(conv):
            sl = slice(rb * 32, (rb + 1) * 32)
            gate, dgate = _silu_parts(p_ref[sl, E_GATE + BW:E_GATE + 2 * BW])
            bg = p_ref[sl, E_BG:E_BG + BW]
            dyc = dy[sl, BW:2 * BW]
            yb = bg * cv
            y_ref[sl, BW:2 * BW] = (yb * gate).astype(BF16)
            dp_ref[sl, E_GATE + BW:E_GATE + 2 * BW] = (dyc * yb * dgate).astype(BF16)
            dyb = dyc * gate
            dp_ref[sl, E_BG:E_BG + BW] = (dyb * cv).astype(BF16)
            dconv[sl, :] = dyb * bg
        gbuf[0:tm] = dconv[...]
        gbuf[tm:tm + HALO] = jnp.where(i > 0, carry[...], 0.0)
        carry[...] = dconv[0:HALO]
        _tap_grads(dconv, cbuf, None, wacc, 3)
        for rb, dc in enumerate(_taps_anticausal(gbuf, None, wc_ref, 3)):
            sl = slice(rb * 32, (rb + 1) * 32)
            dp_ref[sl, E_CG:E_CG + BW] = (dc * p_ref[sl, E_XIN:E_XIN + BW]).astype(BF16)
            dp_ref[sl, E_XIN:E_XIN + BW] = (dc * p_ref[sl, E_CG:E_CG + BW]).astype(BF16)

        for h in range(NH):
            qs = slice(E_Q + h * HD, E_Q + (h + 1) * HD)
            ks = slice(h * HD, (h + 1) * HD)
            vs = slice(XA + h * HD, XA + (h + 1) * HD)
            gs = slice(E_GATE + 2 * BW + h * HD, E_GATE + 2 * BW + (h + 1) * HD)
            ys = slice(2 * BW + h * HD, 2 * BW + (h + 1) * HD)
            q_b = p_ref[:, qs].astype(BF16)
            prob, yx = _attn_head(q_b, kv_ref[:, ks], kv_ref[:, vs])
            gate, dgate = _silu_parts(p_ref[:, gs])
            dyc = dy[:, ys]
            y_ref[:, ys] = (yx * gate).astype(BF16)
            dp_ref[:, gs] = (dyc * yx * dgate).astype(BF16)
            dyx_b = (dyc * gate).astype(BF16)
            dprob = _dot_nt(dyx_b, kv_ref[:, vs])
            dkv_ref[:, vs] += _dot_tn(prob.astype(BF16), dyx_b)
            ds_b = (prob * (dprob - jnp.sum(dprob * prob, axis=-1, keepdims=True)) * (HD ** -0.5)).astype(BF16)
            dp_ref[:, qs] = _dot(ds_b, kv_ref[:, ks]).astype(BF16)
            dkv_ref[:, ks] += _dot_tn(ds_b, q_b)

        @pl.when(i == nt - 1)
        def _():
            for h in range(NH):
                dbs_ref[:, h * HD:(h + 1) * HD] = jnp.broadcast_to(
                    jnp.sum(dbmap[...] * masks[h], axis=-1, keepdims=True), (CHUNK, HD))
            for k in range(3):
                dwc_ref[k:k + 1, :] = _colsum(wacc[k * 8:(k + 1) * 8, :])
            dwc_ref[3:8, :] = jnp.zeros((5, BW), F32)
            causal = (lax.broadcasted_iota(jnp.int32, (CHUNK, CHUNK), 0)
                      >= lax.broadcasted_iota(jnp.int32, (CHUNK, CHUNK), 1))
            for h in range(NH):
                dws_ref[h] = jnp.where(causal, dws_ref[h], 0.0)

    rtile = lambda n: pl.BlockSpec((tm, n), lambda i: (nt - 1 - i, 0))
    outs = [S((t, EVEN_IN), BF16), S((t, MIX), BF16), S((t, D), BF16), S((1, D), F32), S((NH, CHUNK, CHUNK), F32),
            S((CHUNK, NH * HD), F32), S((1, BW), F32), S((1, BW), F32), S((8, BW), F32), S((N_MEM, 2 * XA), F32)]
    return pl.pallas_call(
        body, name="even_bwd", grid=(nt,),
        in_specs=[rtile(D), rtile(D), rtile(EVEN_IN), _halo_spec(EVEN_IN, nt, True, tm)] + [_whole()] * 9,
        out_specs=[rtile(EVEN_IN), rtile(MIX), rtile(D)] + [_full(s.shape) for s in outs[3:]],
        out_shape=outs,
        scratch_shapes=[pltpu.VMEM((tm, MIX), F32), pltpu.VMEM((tm + HALO, BW), F32), pltpu.VMEM((tm + HALO, BW), F32),
                        pltpu.VMEM((tm, BW), F32), pltpu.VMEM((HALO, BW), F32), pltpu.VMEM((tm, BW), F32),
                        pltpu.VMEM((CHUNK, BW), F32), pltpu.VMEM((3 * 8, BW), F32)],
        compiler_params=_params(("arbitrary",)),
    )(dres, o, p, p, lng, lnb, ws, wst, bmap, wc, kv, wout, pg)


def _pool_causal(za, zb, zc, zd, tm):
    n = tm + HALO
    zb[pl.ds(8, n - 8), :] = za[pl.ds(8, n - 8), :] + za[pl.ds(7, n - 8), :]
    zc[pl.ds(16, n - 16), :] = zb[pl.ds(16, n - 16), :] + zb[pl.ds(14, n - 16), :]
    zd[pl.ds(24, n - 24), :] = zc[pl.ds(24, n - 24), :] + zc[pl.ds(20, n - 24), :]
    s16 = zd[pl.ds(HALO, tm), :] + zd[pl.ds(HALO - 8, tm), :]
    return zb[pl.ds(HALO, tm), :], zc[pl.ds(HALO, tm), :], zd[pl.ds(HALO, tm), :], s16


def _pool_anticausal(ea, eb, ec, ed, tm):
    n = tm + HALO
    eb[pl.ds(0, n - 8), :] = ea[pl.ds(0, n - 8), :] + ea[pl.ds(1, n - 8), :]
    ec[pl.ds(0, n - 16), :] = eb[pl.ds(0, n - 16), :] + eb[pl.ds(2, n - 16), :]
    ed[pl.ds(0, n - 24), :] = ec[pl.ds(0, n - 24), :] + ec[pl.ds(4, n - 24), :]
    a16 = ed[pl.ds(0, tm), :] + ed[pl.ds(8, tm), :]
    return eb[pl.ds(0, tm), :], ec[pl.ds(0, tm), :], ed[pl.ds(0, tm), :], a16


def _pool_weights(ti, masks, tm):
    tf = (ti * tm + lax.broadcasted_iota(jnp.int32, (tm, 1), 0) + 1).astype(F32)
    inv = None
    for g, win in enumerate(POOL_WINDOWS):
        term = masks[g] * (1.0 / jnp.minimum(tf, float(win)))
        inv = term if inv is None else inv + term
    return inv


def _mix4(masks, parts):
    return masks[0] * parts[0] + masks[1] * parts[1] + masks[2] * parts[2] + masks[3] * parts[3]


def _odd_fwd(x1, tgt, p, wbd, cscale, dww, dwb, lng, lnb, wpw, pwb, kv, wout, pg):
    t = x1.shape[0]
    tm = min(TM_FWD, t)
    nt = t // tm

    def body(x_ref, tgt_ref, p_ref, ph_ref, wbd_ref, cs_ref, dww_ref, dwb_ref, lng_ref, lnb_ref, wpw_ref, pwb_ref,
             kv_ref, wout_ref, pg_ref, o_ref, dres_ref, loss_ref, conv_ref, ybuf, za, zb, zc, zd, gbuf, lacc, gsh):
        i = pl.program_id(0)
        masks = _group_masks()

        @pl.when(i == 0)
        def _():
            lacc[...] = jnp.zeros_like(lacc)

        z = p_ref[:, O_ZC:O_ZC + BW]
        za[0:HALO] = jnp.where(i > 0, ph_ref[:, O_ZC:O_ZC + BW], 0.0)
        za[HALO:HALO + tm] = z
        pooled = _mix4(masks, _pool_causal(za, zb, zc, zd, tm)) * _pool_weights(i, masks, tm) - z
        gate, _ = _silu_parts(p_ref[:, O_GATE:O_GATE + BW])
        ybuf[:, 0:BW] = (_dot(pooled.astype(BF16), wbd_ref[...]) * cs_ref[...] * gate).astype(BF16)

        gbuf[0:HALO] = jnp.where(i > 0, ph_ref[:, O_GA:O_GA + BW] * _sigmoid(ph_ref[:, O_GB:O_GB + BW]), 0.0)
        gbuf[HALO:HALO + tm] = p_ref[:, O_GA:O_GA + BW] * _sigmoid(p_ref[:, O_GB:O_GB + BW])
        _shift_copies(gbuf, gsh)
        for rb, blk in enumerate(_taps_causal(gbuf, gsh, dww_ref, CONF_K, dwb_ref[...])):
            conv_ref[rb * 32:(rb + 1) * 32, :] = blk
        zh, _ = _ln_stats(conv_ref[...])
        zn = zh * lng_ref[...] + lnb_ref[...]
        yd = _dot((zn * _sigmoid(zn)).astype(BF16), wpw_ref[...]) + pwb_ref[...]
        gate, _ = _silu_parts(p_ref[:, O_GATE + BW:O_GATE + 2 * BW])
        ybuf[:, BW:2 * BW] = (yd * gate).astype(BF16)

        for h in range(NH):
            qs = slice(O_Q + h * HD, O_Q + (h + 1) * HD)
            _, yx = _attn_head(p_ref[:, qs].astype(BF16), kv_ref[:, h * HD:(h + 1) * HD],
                               kv_ref[:, XA + h * HD:XA + (h + 1) * HD])
            gs = slice(O_GATE + 2 * BW + h * HD, O_GATE + 2 * BW + (h + 1) * HD)
            gate, _ = _silu_parts(p_ref[:, gs])
            ybuf[:, 2 * BW + h * HD:2 * BW + (h + 1) * HD] = (yx * gate).astype(BF16)

        o = _dot(ybuf[...], wout_ref[...])
        o_ref[...] = o
        err = _rms_residual(x_ref[...], o, pg_ref[...]) - tgt_ref[...]
        lacc[...] += _colsum(err * err)
        dres_ref[...] = err * (1.0 / D)

        @pl.when(i == nt - 1)
        def _():
            loss_ref[...] = jnp.full((1, HD), jnp.sum(lacc[...]) * (0.5 / D), F32)

    tile = lambda n: pl.BlockSpec((tm, n), lambda i: (i, 0))
    ext = pltpu.VMEM((tm + HALO, BW), F32)
    return pl.pallas_call(
        body, name="odd_fwd", grid=(nt,),
        in_specs=[tile(D), tile(D), tile(ODD_IN), _halo_spec(ODD_IN, nt, False, tm)] + [_whole()] * 11,
        out_specs=[tile(D), tile(D), _full((1, HD)), tile(BW)],
        out_shape=[S((t, D), F32), S((t, D), F32), S((1, HD), F32), S((t, BW), F32)],
        scratch_shapes=[pltpu.VMEM((tm, MIX), BF16), ext, ext, ext, ext, ext,
                        pltpu.VMEM((1, D), F32), pltpu.VMEM((7, tm + HALO, BW), F32)],
        compiler_params=_params(("arbitrary",)),
    )(x1, tgt, p, p, wbd, cscale, dww, dwb, lng, lnb, wpw, pwb, kv, wout, pg)


def _odd_bwd(dres, o, p, conv, wbd, cscale, dww, dwb, lng, lnb, wpw, pwb, kv, wout, pg):
    t = dres.shape[0]
    tm = min(TM_BWD, t)
    nt = t // tm

    def body(dres_ref, o_ref, p_ref, ph_ref, conv_ref, wbd_ref, cs_ref, dww_ref, dwb_ref, lng_ref, lnb_ref, wpw_ref,
             pwb_ref, kv_ref, wout_ref, pg_ref, dp_ref, y_ref, do_ref, dpg_ref, dwbd_ref, dcs_ref, ddww_ref, ddwb_ref,
             dlng_ref, dlnb_ref, dwpw_ref, dpwb_ref, dkv_ref,
             dy, za, zb, zc, zd, gbuf, hbuf, tmp, carry_e, carry_d, wacc, gsh, hsh):
        i = pl.program_id(0)
        ti = nt - 1 - i
        masks = _group_masks()

        @pl.when(i == 0)
        def _():
            for ref in (dpg_ref, dwbd_ref, dcs_ref, ddwb_ref, dlng_ref, dlnb_ref, dwpw_ref, dpwb_ref, dkv_ref, wacc):
                ref[...] = jnp.zeros_like(ref)

        do, dpg = _rms_post_bwd(dres_ref[...], o_ref[...], pg_ref[...])
        dpg_ref[...] += dpg
        do_b = do.astype(BF16)
        do_ref[...] = do_b
        dy[...] = _dot_nt(do_b, wout_ref[...])

        z = p_ref[:, O_ZC:O_ZC + BW]
        za[0:HALO] = jnp.where(ti > 0, ph_ref[:, O_ZC:O_ZC + BW], 0.0)
        za[HALO:HALO + tm] = z
        inv = _pool_weights(ti, masks, tm)
        pooled_b = (_mix4(masks, _pool_causal(za, zb, zc, zd, tm)) * inv - z).astype(BF16)
        pm = _dot(pooled_b, wbd_ref[...])
        gate, dgate = _silu_parts(p_ref[:, O_GATE:O_GATE + BW])
        dyc = dy[:, 0:BW]
        yc = pm * cs_ref[...]
        y_ref[:, 0:BW] = (yc * gate).astype(BF16)
        dp_ref[:, O_GATE:O_GATE + BW] = (dyc * yc * dgate).astype(BF16)
        dyc = dyc * gate
        dcs_ref[...] += _colsum(dyc * pm)
        dpm_b = (dyc * cs_ref[...]).astype(BF16)
        dwbd_ref[...] += _dot_tn(pooled_b, dpm_b)
        dpool = _dot_nt(dpm_b, wbd_ref[...])
        e = dpool * inv
        za[0:tm] = e
        za[tm:tm + HALO] = jnp.where(i > 0, carry_e[...], 0.0)
        carry_e[...] = e[0:HALO]
        dp_ref[:, O_ZC:O_ZC + BW] = (_mix4(masks, _pool_anticausal(za, zb, zc, zd, tm)) - dpool).astype(BF16)

        gbuf[0:HALO] = jnp.where(ti > 0, ph_ref[:, O_GA:O_GA + BW] * _sigmoid(ph_ref[:, O_GB:O_GB + BW]), 0.0)
        gbuf[HALO:HALO + tm] = p_ref[:, O_GA:O_GA + BW] * _sigmoid(p_ref[:, O_GB:O_GB + BW])
        _shift_copies(gbuf, gsh)
        zh, rs = _ln_stats(conv_ref[...])
        zn = zh * lng_ref[...] + lnb_ref[...]
        zs, dsilu = _silu_parts(zn)
        zs_b = zs.astype(BF16)
        yd = _dot(zs_b, wpw_ref[...]) + pwb_ref[...]
        gate, dgate = _silu_parts(p_ref[:, O_GATE + BW:O_GATE + 2 * BW])
        dyc = dy[:, BW:2 * BW]
        y_ref[:, BW:2 * BW] = (yd * gate).astype(BF16)
        dp_ref[:, O_GATE + BW:O_GATE + 2 * BW] = (dyc * yd * dgate).astype(BF16)
        dyd = dyc * gate
        dyd_b = dyd.astype(BF16)
        dpwb_ref[...] += _colsum(dyd)
        dwpw_ref[...] += _dot_tn(zs_b, dyd_b)
        dzn = _dot_nt(dyd_b, wpw_ref[...]) * dsilu
        dlng_ref[...] += _colsum(dzn * zh)
        dlnb_ref[...] += _colsum(dzn)
        dzd = _ln_bwd(dzn, zh, rs, lng_ref[...])
        ddwb_ref[...] += _colsum(dzd)
        tmp[...] = dzd
        hbuf[0:tm] = dzd
        hbuf[tm:tm + HALO] = jnp.where(i > 0, carry_d[...], 0.0)
        carry_d[...] = dzd[0:HALO]
        _shift_copies(hbuf, hsh)
        _tap_grads(tmp, gbuf, gsh, wacc, CONF_K)
        for rb, dzg in enumerate(_taps_anticausal(hbuf, hsh, dww_ref, CONF_K)):
            sl = slice(rb * 32, (rb + 1) * 32)
            sgb = _sigmoid(p_ref[sl, O_GB:O_GB + BW])
            dp_ref[sl, O_GA:O_GA + BW] = (dzg * sgb).astype(BF16)
            dp_ref[sl, O_GB:O_GB + BW] = (dzg * p_ref[sl, O_GA:O_GA + BW] * sgb * (1.0 - sgb)).astype(BF16)

        for h in range(NH):
            qs = slice(O_Q + h * HD, O_Q + (h + 1) * HD)
            ks = slice(h * HD, (h + 1) * HD)
            vs = slice(XA + h * HD, XA + (h + 1) * HD)
            gs = slice(O_GATE + 2 * BW + h * HD, O_GATE + 2 * BW + (h + 1) * HD)
            ys = slice(2 * BW + h * HD, 2 * BW + (h + 1) * HD)
            q_b = p_ref[:, qs].astype(BF16)
            prob, yx = _attn_head(q_b, kv_ref[:, ks], kv_ref[:, vs])
            gate, dgate = _silu_parts(p_ref[:, gs])
            dyc = dy[:, ys]
            y_ref[:, ys] = (yx * gate).astype(BF16)
            dp_ref[:, gs] = (dyc * yx * dgate).astype(BF16)
            dyx_b = (dyc * gate).astype(BF16)
            dprob = _dot_nt(dyx_b, kv_ref[:, vs])
            dkv_ref[:, vs] += _dot_tn(prob.astype(BF16), dyx_b)
            ds_b = (prob * (dprob - jnp.sum(dprob * prob, axis=-1, keepdims=True)) * (HD ** -0.5)).astype(BF16)
            dp_ref[:, qs] = _dot(ds_b, kv_ref[:, ks]).astype(BF16)
            dkv_ref[:, ks] += _dot_tn(ds_b, q_b)

        @pl.when(i == nt - 1)
        def _():
            for k in range(CONF_K):
                ddww_ref[k:k + 1, :] = _colsum(wacc[k * 8:(k + 1) * 8, :])
            ddww_ref[CONF_K:CONF_K + 1, :] = jnp.zeros((1, BW), F32)

    rtile = lambda n: pl.BlockSpec((tm, n), lambda i: (nt - 1 - i, 0))
    outs = [S((t, ODD_IN), BF16), S((t, MIX), BF16), S((t, D), BF16), S((1, D), F32), S((BW, BW), F32),
            S((1, BW), F32), S((CONF_K + 1, BW), F32), S((1, BW), F32), S((1, BW), F32), S((1, BW), F32),
            S((BW, BW), F32), S((1, BW), F32), S((N_MEM, 2 * XA), F32)]
    ext = pltpu.VMEM((tm + HALO, BW), F32)
    return pl.pallas_call(
        body, name="odd_bwd", grid=(nt,),
        in_specs=[rtile(D), rtile(D), rtile(ODD_IN), _halo_spec(ODD_IN, nt, True, tm), rtile(BW)] + [_whole()] * 11,
        out_specs=[rtile(ODD_IN), rtile(MIX), rtile(D)] + [_full(s.shape) for s in outs[3:]],
        out_shape=outs,
        scratch_shapes=[pltpu.VMEM((tm, MIX), F32), ext, ext, ext, ext, ext, ext, pltpu.VMEM((tm, BW), F32),
                        pltpu.VMEM((HALO, BW), F32), pltpu.VMEM((HALO, BW), F32), pltpu.VMEM((CONF_K * 8, BW), F32),
                        pltpu.VMEM((7, tm + HALO, BW), F32), pltpu.VMEM((7, tm + HALO, BW), F32)],
        compiler_params=_params(("arbitrary",)),
    )(dres, o, p, p, conv, wbd, cscale, dww, dwb, lng, lnb, wpw, pwb, kv, wout, pg)


def _pick_rows(n):
    for rows in (640, 2432, 1024, 768):
        if n % rows == 0:
            return rows
    return n


def _pad_rows(a, rows):
    return jnp.pad(a, ((0, rows - a.shape[0]), (0, 0)))


def _step(x, mem, tgt, ex):
    t = x.shape[0]
    tm = min(256, t)
    w, deps = ex.first()
    causal = jnp.tril(jnp.ones((CHUNK, CHUNK), bool))
    ws = jnp.where(causal[None], w["even_a_ws"], 0.0).astype(BF16)
    wst = jnp.transpose(ws, (0, 2, 1))
    bmap = jnp.repeat(w["even_a_bs"].T, GRP, axis=1)
    wc = _pad_rows(w["even_b_conv"], 8)
    wbd = jax.scipy.linalg.block_diag(*[w["odd_c_wgrp"][g] for g in range(NH)]).astype(BF16)
    dww = _pad_rows(w["odd_d_dw_w"], CONF_K + 1)
    tk = min(1024, t)
    zeros = jnp.zeros_like(mem)

    p_e, h_e = _rms_matmul(x, w["even_pre_g"], w["even_w_in"], tm=tm, name="in_even", transposed=True, deps=deps)
    w.update(ex.even_rest(h_e))
    kv_e, memn_e = _rms_matmul(mem, w["even_mem_g"], w["even_w_kv"], tm=N_MEM, name="kv_even")
    kv_e = kv_e.astype(BF16)
    even_args = (w["even_a_ln_g"], w["even_a_ln_b"], ws)
    o_e, x1 = _even_fwd(x, p_e, *even_args, bmap, wc, kv_e, w["even_w_out"], w["even_post_g"])
    w.update(ex.odd(o_e))
    kv_o, memn_o = _rms_matmul(mem, w["odd_mem_g"], w["odd_w_kv"], tm=N_MEM, name="kv_odd")
    kv_o = kv_o.astype(BF16)
    p_o, h_o = _rms_matmul(x1, w["odd_pre_g"], w["odd_w_in"], tm=tm, name="in_odd", transposed=True)
    odd_args = (wbd, w["odd_c_scale"], dww, w["odd_d_dw_b"], w["odd_d_ln_g"], w["odd_d_ln_b"], w["odd_d_pw_w"],
                w["odd_d_pw_b"], kv_o, w["odd_w_out"], w["odd_post_g"])
    o_o, dres, loss, conv_o = _odd_fwd(x1, tgt, p_o, *odd_args)

    g = {}
    (dp_o, y_o, do_o, post_g_o, dwbd, g["odd_c_scale"], ddww, g["odd_d_dw_b"], g["odd_d_ln_g"], g["odd_d_ln_b"],
     dwpw, g["odd_d_pw_b"], dkv_o) = _odd_bwd(dres, o_o, p_o, conv_o, *odd_args)
    g["odd_post_g"] = post_g_o
    g["odd_d_dw_w"] = ddww[:CONF_K]
    dkv_o = dkv_o.astype(BF16)
    deps = ex.send("odd_rest", {
        "odd_w_out": _tn_matmul(y_o, do_o, tmc=MIX, tk=tk, out_dtype=BF16, name="dw_out_odd"),
        "odd_w_kv": _tn_matmul(memn_o, dkv_o, tmc=D, tk=N_MEM, out_dtype=BF16, name="dw_kv_odd"),
        "odd_d_pw_w": dwpw.astype(BF16), "loss": loss,
        "odd_c_wgrp": jnp.concatenate([dwbd[i * GRP:(i + 1) * GRP, i * GRP:(i + 1) * GRP] for i in range(NH)])})
    deps = ex.send("odd_in", {"odd_w_in": _tn_matmul(dp_o, h_o, tmc=_pick_rows(ODD_IN), tk=tk, out_dtype=BF16,
                                                     name="dw_in_odd", deps=deps)})
    dx1, g["odd_pre_g"] = _nt_matmul_rms_bwd(dp_o, w["odd_w_in"], x1, w["odd_pre_g"], dres, tm=min(256, t),
                                             name="dx_odd", transposed=True, deps=deps)
    _, g["odd_mem_g"] = _nt_matmul_rms_bwd(dkv_o, w["odd_w_kv"], mem, w["odd_mem_g"], zeros, tm=N_MEM,
                                           name="dmem_odd")

    (dp_e, y_e, do_e, post_g_e, dws, dbs, ln_g_e, ln_b_e, dwc, dkv_e) = _even_bwd(
        dx1, o_e, p_e, *even_args, wst, bmap, wc, kv_e, w["even_w_out"], w["even_post_g"])
    g["even_b_conv"] = dwc[:3]
    dkv_e = dkv_e.astype(BF16)
    g["even_w_in"] = _tn_matmul(dp_e, h_e, tmc=_pick_rows(EVEN_IN), tk=tk, out_dtype=BF16, name="dw_in_even")
    deps = ex.send("even_in", g)
    grad_x, pre_g_e = _nt_matmul_rms_bwd(dp_e, w["even_w_in"], x, w["even_pre_g"], dx1, tm=min(256, t),
                                         name="dx_even", transposed=True, deps=deps)
    deps = ex.send("even_rest", {
        "even_w_out": _tn_matmul(y_e, do_e, tmc=MIX, tk=tk, out_dtype=BF16, name="dw_out_even", deps=(grad_x,)),
        "even_w_kv": _tn_matmul(memn_e, dkv_e, tmc=D, tk=N_MEM, out_dtype=BF16, name="dw_kv_even"),
        "even_a_ln_g": ln_g_e, "even_a_ln_b": ln_b_e,
        "even_a_ws": dws.reshape(NH * CHUNK, CHUNK), "even_a_bs": dbs[:, ::HD].T})
    _, mem_g_e = _nt_matmul_rms_bwd(dkv_e, w["even_w_kv"], mem, w["even_mem_g"], zeros, tm=N_MEM, name="dmem_even",
                                    deps=deps)
    ex.send("even_gains", {"even_pre_g": pre_g_e, "even_mem_g": mem_g_e, "even_post_g": post_g_e})
    return grad_x, mem_g_e


def _place():
    return lax.axis_index("x"), lax.axis_index("y"), lax.axis_index("c")


def _index(px, py, pc):
    return 4 * px + 2 * py + pc


_COPIES = N_DEV - 1


def _all_gather(arrs, name):
    n = len(arrs)

    def body(*refs):
        ins, outs = refs[:n], refs[n:2 * n]
        send_sems, recv_sems, local_sems = refs[2 * n:]
        x, y, c = _place()
        me, sibling = (x, y, c), (x, y, 1 - c)
        chips = [(1 - x, y), (x, 1 - y), (1 - x, 1 - y)]

        def copy(a, k, block, to, src=None):
            dst = outs[a].at[_index(*block)]
            return pltpu.make_async_remote_copy(
                src_ref=dst if src is None else src, dst_ref=dst, send_sem=send_sems.at[a * _COPIES + k],
                recv_sem=recv_sems.at[a * _COPIES + k], device_id=to, device_id_type=MESH)

        mine = [pltpu.make_async_copy(ins[a], outs[a].at[_index(*me)], local_sems.at[a]) for a in range(n)]
        first = []
        for a in range(n):
            mine[a].start()
            first.append(copy(a, 0, me, sibling, src=ins[a]))
            first += [copy(a, 1 + j, me, (*chip, c), src=ins[a]) for j, chip in enumerate(chips)]
        for cp in first:
            cp.start()
        passed = []
        for j, chip in enumerate(chips):
            for a in range(n):
                copy(a, 1 + j, (*chip, c), me).wait_recv()
                passed.append(copy(a, 4 + j, (*chip, c), sibling))
                passed[-1].start()
        for a in range(n):
            copy(a, 0, sibling, me).wait_recv()
            for j, chip in enumerate(chips):
                copy(a, 4 + j, (*chip, 1 - c), me).wait_recv()
        for cp in first + passed:
            cp.wait_send()
        for cp in mine:
            cp.wait()

    return pl.pallas_call(
        body, name=name, in_specs=[_ANY] * n, out_specs=[_ANY] * n,
        out_shape=[S((N_DEV,) + a.shape, a.dtype) for a in arrs],
        scratch_shapes=[pltpu.SemaphoreType.DMA((n * _COPIES,)), pltpu.SemaphoreType.DMA((n * _COPIES,)),
                        pltpu.SemaphoreType.DMA((n,))],
    )(*arrs)


_HBM = pl.BlockSpec(memory_space=pltpu.HBM)
_SEM = pl.BlockSpec(memory_space=pltpu.SEMAPHORE)
_EFFECT = pltpu.SideEffectType.DATAFLOW_SIDE_EFFECTING


_ALL_FLIPS = [(k >> 2 & 1, k >> 1 & 1, k & 1) for k in range(1, N_DEV)]
_CHIP_FLIPS = [(1, 0, 0), (0, 1, 0), (1, 1, 0)]
_FLIPS = {"gather": _ALL_FLIPS, "scatter": _ALL_FLIPS, "gather_chips": [(0, 0, 1)] + _CHIP_FLIPS,
          "scatter_chips": _CHIP_FLIPS}


def _landing_shape(kind, a):
    return (N_DEV,) + a.shape if kind.startswith("gather") else a.shape


def _exchange_copies(kinds, srcs, lands, send_sems, recv_sems, local_sems, arriving):
    x, y, c = _place()
    mine = _index(x, y, c)
    remote, local = [], []
    for a, kind in enumerate(kinds):
        by_chip = kind == "scatter_chips"
        here = 2 * x + y if by_chip else mine
        own = srcs[a] if kind.startswith("gather") else srcs[a].at[here]
        local.append(pltpu.make_async_copy(own, lands[a].at[here], local_sems.at[a]))
        for k, (fx, fy, fc) in enumerate(_FLIPS[kind]):
            peer = (1 - x if fx else x, 1 - y if fy else y, 1 - c if fc else c)
            there = 2 * peer[0] + peer[1] if by_chip else _index(*peer)
            remote.append(pltpu.make_async_remote_copy(
                src_ref=srcs[a] if kind.startswith("gather") else srcs[a].at[there],
                dst_ref=lands[a].at[there if arriving else here],
                send_sem=send_sems.at[a * _COPIES + k], recv_sem=recv_sems.at[a * _COPIES + k],
                device_id=peer, device_id_type=MESH))
    return remote, local


def _exchange_start(items, name, deps=()):
    kinds = [kind for kind, _ in items]
    srcs = [a for _, a in items]
    n = len(items)
    lands = [lax.empty(_landing_shape(kind, a), a.dtype) for kind, a in items]

    def body(*refs):
        send_sems, recv_sems, local_sems = refs[2 * n + len(deps):2 * n + len(deps) + 3]
        remote, local = _exchange_copies(kinds, refs[:n], refs[n:2 * n], send_sems, recv_sems, local_sems, False)
        for cp in local + remote:
            cp.start()
        refs[-1][...] = jnp.zeros_like(refs[-1])

    held = [pltpu.HBM(a.shape, a.dtype) for a in srcs + lands]
    res = pl.pallas_call(
        body, name=name,
        out_shape=(pltpu.SemaphoreType.DMA((n * _COPIES,)), pltpu.SemaphoreType.DMA((n * _COPIES,)),
                   pltpu.SemaphoreType.DMA((n,)), *held, S((8, 128), F32)),
        in_specs=[_HBM] * (2 * n) + [_ANY] * len(deps),
        out_specs=(_SEM, _SEM, _SEM, *[_HBM] * (2 * n), _whole()),
        input_output_aliases={i: 3 + i for i in range(2 * n)},
        compiler_params=pltpu.CompilerParams(has_side_effects=_EFFECT),
    )(*[pltpu.with_memory_space_constraint(a, pltpu.HBM) for a in srcs + lands], *deps)
    return (kinds, res[:3], res[3:3 + 2 * n]), res[-1]


def _exchange_wait(handle, after, name):
    kinds, sems, held = handle
    n = len(kinds)

    def body(*refs):
        send_sems, recv_sems, local_sems = refs[2 * n:2 * n + 3]
        remote, local = _exchange_copies(kinds, refs[:n], refs[n:2 * n], send_sems, recv_sems, local_sems, True)
        for cp in remote:
            cp.wait_send()
            cp.wait_recv()
        for cp in local:
            cp.wait()

    res = pl.pallas_call(
        body, name=name, out_shape=[pltpu.HBM(a.shape, a.dtype) for a in held],
        in_specs=[_HBM] * (2 * n) + [_SEM] * 3 + [_ANY] * len(after), out_specs=[_HBM] * (2 * n),
        input_output_aliases={i: i for i in range(2 * n)},
        compiler_params=pltpu.CompilerParams(has_side_effects=_EFFECT),
    )(*held, *sems, *after)
    return res[n:]


_CHIPS = [(0, 0), (0, 1), (1, 0), (1, 1)]
_N_CHIPS = len(_CHIPS)


def _sibling_forward(lands, name):
    n = len(lands)

    def body(*refs):
        ins, outs = refs[:n], refs[n:2 * n]
        send_sems, recv_sems = refs[2 * n:]
        x, y, c = _place()
        sent, arriving = [], []
        for a in range(n):
            for j, (fx, fy, _) in enumerate(_CHIP_FLIPS):
                chip = (1 - x if fx else x, 1 - y if fy else y)
                sems = dict(send_sem=send_sems.at[a * 3 + j], recv_sem=recv_sems.at[a * 3 + j],
                            device_id=(x, y, 1 - c), device_id_type=MESH)
                mine, theirs = _index(*chip, c), _index(*chip, 1 - c)
                sent.append(pltpu.make_async_remote_copy(src_ref=ins[a].at[mine], dst_ref=outs[a].at[mine], **sems))
                arriving.append(pltpu.make_async_remote_copy(src_ref=ins[a].at[theirs], dst_ref=outs[a].at[theirs],
                                                             **sems))
        for cp in sent:
            cp.start()
        for cp in sent:
            cp.wait_send()
        for cp in arriving:
            cp.wait_recv()

    return pl.pallas_call(
        body, name=name, in_specs=[_ANY] * n, out_specs=[_ANY] * n,
        out_shape=[S(a.shape, a.dtype) for a in lands], input_output_aliases={a: a for a in range(n)},
        scratch_shapes=[pltpu.SemaphoreType.DMA((3 * n,)), pltpu.SemaphoreType.DMA((3 * n,))],
    )(*lands)


def _sibling_swap(arrs, name):
    n = len(arrs)

    def body(*refs):
        ins, outs = refs[:n], refs[n:2 * n]
        send_sems, recv_sems = refs[2 * n:]
        x, y, c = _place()
        copies = []
        for a in range(n):
            for q, chip in enumerate(_CHIPS):
                copies.append(pltpu.make_async_remote_copy(
                    src_ref=ins[a].at[_index(*chip, 1 - c)], dst_ref=outs[a].at[q],
                    send_sem=send_sems.at[a * _N_CHIPS + q], recv_sem=recv_sems.at[a * _N_CHIPS + q],
                    device_id=(x, y, 1 - c), device_id_type=MESH))
        for cp in copies:
            cp.start()
        for cp in copies:
            cp.wait_send()
            cp.wait_recv()

    return pl.pallas_call(
        body, name=name, in_specs=[_ANY] * n, out_specs=[_ANY] * n,
        out_shape=[S((_N_CHIPS,) + a.shape[1:], a.dtype) for a in arrs],
        scratch_shapes=[pltpu.SemaphoreType.DMA((_N_CHIPS * n,)), pltpu.SemaphoreType.DMA((_N_CHIPS * n,))],
    )(*arrs)


def _add_partials(mine, theirs, *, tr, name):
    _, r, c = mine.shape

    def body(mine_ref, theirs_ref, out_ref):
        core = lax.axis_index("c")
        own = jnp.where(core == 0, mine_ref[0].astype(F32), mine_ref[1].astype(F32))
        out_ref[0] = (own + theirs_ref[0].astype(F32)).astype(out_ref.dtype)

    return pl.pallas_call(
        body, name=name, grid=(_N_CHIPS, r // tr),
        in_specs=[pl.BlockSpec((2, tr, c), lambda q, i: (q, i, 0)), pl.BlockSpec((1, tr, c), lambda q, i: (q, i, 0))],
        out_specs=pl.BlockSpec((1, tr, c), lambda q, i: (q, i, 0)),
        out_shape=S((_N_CHIPS, r, c), mine.dtype),
        compiler_params=_params(("arbitrary", "arbitrary")),
    )(mine, theirs)


def _adamw(w, g, m, v):
    m = ADAM_B1 * m + (1.0 - ADAM_B1) * g
    v = ADAM_B2 * v + (1.0 - ADAM_B2) * (g * g)
    m_hat = m / (1.0 - ADAM_B1 ** ADAM_STEP)
    v_hat = v / (1.0 - ADAM_B2 ** ADAM_STEP)
    return -ADAM_LR * (m_hat / (jnp.sqrt(v_hat) + ADAM_EPS) + ADAM_WD * w), m, v


def _sum_devices(ref, rows):
    total = ref[0, rows, :].astype(F32)
    for s in range(1, ref.shape[0]):
        total = total + ref[s, rows, :].astype(F32)
    return total


def _adam_big(recv, w, m, v, *, tr, name):
    r, c = w.shape

    def body(recv_ref, w_ref, m_ref, v_ref, g_ref, d_ref, m2_ref, v2_ref):
        g = _sum_devices(recv_ref, slice(None))
        g_ref[...] = g
        d_ref[...], m2_ref[...], v2_ref[...] = _adamw(w_ref[...], g, m_ref[...], v_ref[...])

    blk = pl.BlockSpec((tr, c), lambda i: (i, 0))
    return pl.pallas_call(
        body, name=name, grid=(r // tr,),
        in_specs=[pl.BlockSpec((recv.shape[0], tr, c), lambda i: (0, i, 0)), blk, blk, blk],
        out_specs=[blk] * 4, out_shape=[S((r, c), F32)] * 4,
        compiler_params=_params(("arbitrary",)),
    )(recv, w, m, v)


_REPLICATED = {"even_pre_g": (0, 0, 1), "even_mem_g": (0, 8, 1), "even_post_g": (0, 16, 1),
               "even_a_ln_g": (1, 0, 1), "even_a_ln_b": (1, 8, 1),
               "even_a_ws": (2, 0, NH * CHUNK), "even_a_bs": (2, NH * CHUNK, NH),
               "odd_c_wgrp": (3, 0, NH * GRP)}
_SHARDED = {"odd_pre_g": (4, 0, 1), "odd_mem_g": (4, 8, 1), "odd_post_g": (4, 16, 1),
            "even_b_conv": (5, 0, 3), "odd_c_scale": (5, 8, 1), "odd_d_dw_w": (5, 16, CONF_K),
            "odd_d_dw_b": (5, 48, 1), "odd_d_ln_g": (5, 56, 1), "odd_d_ln_b": (5, 64, 1), "odd_d_pw_b": (5, 72, 1)}
_SMALL = {**_REPLICATED, **_SHARDED}
_SMALL_ROWS = {0: 24, 1: 16, 2: NH * CHUNK + 8, 3: NH * GRP, 4: 24, 5: 80}


def _adam_small(sources, wmv):
    names = list(_SMALL)
    ns = len(sources)

    def body(*refs):
        src = refs[:ns]
        ins = refs[ns:ns + 3 * len(names)]
        outs = refs[ns + 3 * len(names):]
        outs[-1][...] = _sum_devices(src[-1], slice(0, 1))
        for i, nm in enumerate(names):
            a, row0, rows = _SMALL[nm]
            g = _sum_devices(src[a], slice(row0, row0 + rows))
            w_ref, m_ref, v_ref = ins[3 * i:3 * i + 3]
            g_ref, d_ref, m2_ref, v2_ref = outs[4 * i:4 * i + 4]
            g_ref[...] = g
            d_ref[...], m2_ref[...], v2_ref[...] = _adamw(w_ref[...], g, m_ref[...], v_ref[...])

    flat = [t for nm in names for t in wmv[nm]]
    out_shape = [S(wmv[nm][0].shape, F32) for nm in names for _ in range(4)] + [S((1, HD), F32)]
    res = pl.pallas_call(
        body, name="adam_small", in_specs=[_whole()] * (ns + len(flat)), out_specs=[_whole()] * len(out_shape),
        out_shape=out_shape, compiler_params=_params(),
    )(*sources, *flat)
    return {nm: tuple(res[4 * i:4 * i + 4]) for i, nm in enumerate(names)}, res[-1]


_WEIGHTS = ["even_pre_g", "even_w_in", "even_a_ln_g", "even_a_ln_b", "even_a_ws", "even_a_bs", "even_b_conv",
            "even_mem_g", "even_w_kv", "even_w_out", "even_post_g", "odd_pre_g", "odd_w_in", "odd_c_wgrp",
            "odd_c_scale", "odd_d_dw_w", "odd_d_dw_b", "odd_d_ln_g", "odd_d_ln_b", "odd_d_pw_w", "odd_d_pw_b",
            "odd_mem_g", "odd_w_kv", "odd_w_out", "odd_post_g"]
_TRANSPOSED = ["even_w_in", "odd_w_in"]
_BIG = _TRANSPOSED + ["even_w_kv", "even_w_out", "odd_w_kv", "odd_w_out", "odd_d_pw_w"]
_BIG_TILE_ROWS = {"even_w_in": 400, "odd_w_in": 304, "even_w_kv": 128, "even_w_out": 128, "odd_w_kv": 128,
                  "odd_w_out": 128, "odd_d_pw_w": 96}


def _view2d(a, transposed):
    a = a[0]
    if a.ndim == 1:
        return a[None]
    if transposed:
        return a.T
    return a.reshape(-1, a.shape[-1])


def _rows8(a):
    return _pad_rows(a, -(-a.shape[0] // 8) * 8)


def _pack_rows(parts):
    return jnp.concatenate([_rows8(p) for p in parts], axis=0)


def _unshard_cols(a):
    return jnp.transpose(a, (1, 0, 2)).reshape(a.shape[1], N_DEV * a.shape[2])


def _shard_cols(a):
    return jnp.transpose(a.reshape(a.shape[0], N_DEV, a.shape[1] // N_DEV), (1, 0, 2))


def _rows_of(a):
    return a.reshape(-1, a.shape[-1])


_GROUPS = {"odd_rest": (["odd_w_out", "odd_w_kv", "odd_d_pw_w"], [3], []),
           "odd_in": (["odd_w_in"], [], []),
           "even_in": (["even_w_in"], [], [4, 5]),
           "even_rest": (["even_w_out", "even_w_kv"], [1, 2], []),
           "even_gains": ([], [0], [])}


_TWO_LEVEL = ("even_in", "even_rest")


class _MeshExchange:
    def __init__(self, shard):
        self.shard = shard
        self.handles = {}

    def first(self):
        shard = self.shard
        packs = [_pack_rows([shard[nm] for nm in _SHARDED if _SHARDED[nm][0] == a]) for a in (4, 5)]
        w_in, p128, p96 = _all_gather([shard["even_w_in"].astype(BF16)] + packs, "gather_first")
        w = {nm: shard[nm] for nm in _REPLICATED}
        w["even_a_ws"] = w["even_a_ws"].reshape(NH, CHUNK, CHUNK)
        w["odd_c_wgrp"] = w["odd_c_wgrp"].reshape(NH, GRP, GRP)
        w["even_w_in"] = _rows_of(w_in)
        full_packs = {4: _unshard_cols(p128), 5: _unshard_cols(p96)}
        for nm, (a, row0, rows) in _SHARDED.items():
            w[nm] = full_packs[a][row0:row0 + rows]
        later = lambda names: [("gather_chips", shard[nm].astype(BF16)) for nm in names]
        self.handles["w_even"], token = _exchange_start(later(["even_w_kv", "even_w_out"]), "gather_even_start",
                                                        deps=(w_in,))
        self.handles["w_odd"], token = _exchange_start(later(["odd_w_in", "odd_w_kv", "odd_w_out", "odd_d_pw_w"]),
                                                       "gather_odd_start", deps=(token,))
        return w, (token,)

    def even_rest(self, after):
        landed = _exchange_wait(self.handles.pop("w_even"), (after,), "gather_even_wait")
        kv, out = _sibling_forward(landed, "forward_even")
        return {"even_w_kv": _rows_of(kv), "even_w_out": _rows_of(out)}

    def odd(self, after):
        landed = _exchange_wait(self.handles.pop("w_odd"), (after,), "gather_odd_wait")
        w_in, kv, out, pw = _sibling_forward(landed, "forward_odd")
        return {"odd_w_in": _rows_of(w_in), "odd_w_kv": _rows_of(kv), "odd_w_out": _rows_of(out),
                "odd_d_pw_w": _rows_of(pw)}

    def send(self, group, g):
        big, replicated, sharded = _GROUPS[group]
        by_owner = [g[nm].reshape(N_DEV, -1, g[nm].shape[-1]) for nm in big]
        if group in _TWO_LEVEL:
            theirs = _sibling_swap(by_owner, "swap_" + group)
            items = [("scatter_chips", _add_partials(a, b, tr=_BIG_TILE_ROWS[nm], name="chip_sum_" + nm))
                     for nm, a, b in zip(big, by_owner, theirs)]
        else:
            items = [("scatter", a) for a in by_owner]
        items += [("gather", _pack_rows([g[nm] for nm in _REPLICATED if _REPLICATED[nm][0] == a]))
                  for a in replicated]
        items += [("scatter", _shard_cols(_pack_rows([g[nm] for nm in _SHARDED if _SHARDED[nm][0] == a])))
                  for a in sharded]
        if group == "odd_rest":
            items.append(("gather", _rows8(g["loss"])))
        self.handles[group], token = _exchange_start(items, "send_" + group + "_start")
        return (token,)

    def receive(self, group, after):
        return _exchange_wait(self.handles.pop(group), (after,), "send_" + group + "_wait")


def kernel(x, mem, even_pre_g, even_w_in, even_a_ln_g, even_a_ln_b, even_a_ws, even_a_bs, even_b_conv, even_mem_g, even_w_kv, even_w_out, even_post_g, odd_pre_g, odd_w_in, odd_c_wgrp, odd_c_scale, odd_d_dw_w, odd_d_dw_b, odd_d_ln_g, odd_d_ln_b, odd_d_pw_w, odd_d_pw_b, odd_mem_g, odd_w_kv, odd_w_out, odd_post_g, loss_target, m_even_pre_g, m_even_w_in, m_even_a_ln_g, m_even_a_ln_b, m_even_a_ws, m_even_a_bs, m_even_b_conv, m_even_mem_g, m_even_w_kv, m_even_w_out, m_even_post_g, m_odd_pre_g, m_odd_w_in, m_odd_c_wgrp, m_odd_c_scale, m_odd_d_dw_w, m_odd_d_dw_b, m_odd_d_ln_g, m_odd_d_ln_b, m_odd_d_pw_w, m_odd_d_pw_b, m_odd_mem_g, m_odd_w_kv, m_odd_w_out, m_odd_post_g, v_even_pre_g, v_even_w_in, v_even_a_ln_g, v_even_a_ln_b, v_even_a_ws, v_even_a_bs, v_even_b_conv, v_even_mem_g, v_even_w_kv, v_even_w_out, v_even_post_g, v_odd_pre_g, v_odd_w_in, v_odd_c_wgrp, v_odd_c_scale, v_odd_d_dw_w, v_odd_d_dw_b, v_odd_d_ln_g, v_odd_d_ln_b, v_odd_d_pw_w, v_odd_d_pw_b, v_odd_mem_g, v_odd_w_kv, v_odd_w_out, v_odd_post_g):
    given = dict(locals())
    view = lambda nm, kind: _view2d(given[kind + nm], nm in _TRANSPOSED)
    shard = {nm: view(nm, "") for nm in _WEIGHTS}
    wmv = {nm: (shard[nm], view(nm, "m_"), view(nm, "v_")) for nm in _WEIGHTS}

    ex = _MeshExchange(shard)
    grad_x, last = _step(x[0], mem[0], loss_target[0], ex)

    res = {}

    def update(group, after):
        names = _GROUPS[group][0]
        landed = ex.receive(group, after)
        for nm, recv in zip(names, landed):
            res[nm] = _adam_big(recv, *wmv[nm], tr=_BIG_TILE_ROWS[nm], name="adam_" + nm)
        return landed[len(names):]

    c192, losses = update("odd_rest", last)
    update("odd_in", res["odd_d_pw_w"][0])
    a128, a96 = update("even_in", res["odd_w_in"][0])
    c768, c128 = update("even_rest", res["even_w_in"][0])
    (c1024,) = update("even_gains", res["even_w_kv"][0])
    small, loss = _adam_small([c1024, c768, c128, c192, a128, a96, losses], {nm: wmv[nm] for nm in _SMALL})
    res.update(small)
    total = loss[0, 0]
    back = lambda nm, a: (a.T if nm in _TRANSPOSED else a).reshape(given[nm].shape)
    outs = [[back(nm, res[nm][i]) for nm in _WEIGHTS] for i in range(4)]
    return (total, grad_x[None], *outs[0], *outs[1], *outs[2], *outs[3])
```

```python
import functools

import jax
import jax.numpy as jnp
from jax import lax
from jax.experimental import pallas as pl
from jax.experimental.pallas import tpu as pltpu

F32 = jnp.float32
BF16 = jnp.bfloat16
S = jax.ShapeDtypeStruct
MESH = pl.DeviceIdType.MESH
AXES = ("x", "y", "c")
N_DEV = 8

D = 1024
BW = 768
XA = 512
HD = 128
NH = 4
MIX = 2048
CHUNK = 128
GRP = 192
N_MEM = 256
CONF_K = 31
EPS = 1e-6
HALO = 32
POOL_WINDOWS = (2, 4, 8, 16)
TM_FWD = 256
TM_BWD = 128

E_U, E_V, E_BG, E_CG, E_XIN, E_Q, E_GATE = 0, 768, 1536, 2304, 3072, 3840, 4352
EVEN_IN = 6400
O_ZC, O_GA, O_GB, O_Q, O_GATE = 0, 768, 1536, 2304, 2816
ODD_IN = 4864

ADAM_LR, ADAM_B1, ADAM_B2, ADAM_EPS, ADAM_WD, ADAM_STEP = 0.001, 0.9, 0.999, 1e-08, 0.01, 10

VMEM_LIMIT_V7X = 56 * 1024 * 1024


def _params(sem=None):
    return pltpu.CompilerParams(dimension_semantics=sem, vmem_limit_bytes=VMEM_LIMIT_V7X)


def _dot(a, b):
    return jnp.dot(a, b, preferred_element_type=F32)


def _dot_nt(a, b):
    return lax.dot_general(a, b, (((1,), (1,)), ((), ())), preferred_element_type=F32)


def _dot_tn(a, b):
    return lax.dot_general(a, b, (((0,), (0,)), ((), ())), preferred_element_type=F32)


def _sigmoid(z):
    return 1.0 / (1.0 + jnp.exp(-z))


def _rowmean(a):
    return jnp.mean(a, axis=-1, keepdims=True)


def _colsum(a):
    return jnp.sum(a, axis=0, keepdims=True)


def _ln_stats(v):
    mu = _rowmean(v)
    vc = v - mu
    rs = lax.rsqrt(_rowmean(vc * vc) + EPS)
    return vc * rs, rs


def _ln_bwd(dn, vh, rs, g):
    dvh = dn * g
    return rs * (dvh - _rowmean(dvh) - vh * _rowmean(dvh * vh))


def _group_masks():
    col = lax.broadcasted_iota(jnp.int32, (1, BW), 1)
    return [((col >= GRP * h) & (col < GRP * (h + 1))).astype(F32) for h in range(NH)]


def _full(shape):
    nd = len(shape)
    return pl.BlockSpec(shape, lambda *_: (0,) * nd)


def _whole():
    return pl.BlockSpec(memory_space=pltpu.VMEM)


_ANY = pl.BlockSpec(memory_space=pl.ANY)


def _after(body, n_in, deps):
    def ordered(*refs):
        return body(*refs[:n_in], *refs[n_in + len(deps):])
    return ordered


def _rms_matmul(x, g, w, *, tm, name, transposed=False, deps=()):
    t, d = x.shape
    n = w.shape[0] if transposed else w.shape[1]

    def body(x_ref, g_ref, w_ref, p_ref, h_ref):
        xv = x_ref[...]
        r = lax.rsqrt(_rowmean(xv * xv) + EPS)
        h = (xv * r * g_ref[...]).astype(BF16)
        h_ref[...] = h
        p_ref[...] = _dot_nt(h, w_ref[...]) if transposed else _dot(h, w_ref[...])

    return pl.pallas_call(
        _after(body, 3, deps), name=name, grid=(t // tm,),
        in_specs=[pl.BlockSpec((tm, d), lambda i: (i, 0)), _whole(), _whole()] + [_ANY] * len(deps),
        out_specs=[pl.BlockSpec((tm, n), lambda i: (i, 0)), pl.BlockSpec((tm, d), lambda i: (i, 0))],
        out_shape=[S((t, n), F32), S((t, d), BF16)],
        compiler_params=_params(("arbitrary",)),
    )(x, g, w, *deps)


def _nt_matmul_rms_bwd(dp, w, x, g, dres, *, tm, name, transposed=False, deps=()):
    t, n = dp.shape
    d = x.shape[1]

    def body(dp_ref, w_ref, x_ref, g_ref, dres_ref, dx_ref, dg_ref):
        @pl.when(pl.program_id(0) == 0)
        def _():
            dg_ref[...] = jnp.zeros_like(dg_ref)

        dh = _dot(dp_ref[...], w_ref[...]) if transposed else _dot_nt(dp_ref[...], w_ref[...])
        xv = x_ref[...]
        r = lax.rsqrt(_rowmean(xv * xv) + EPS)
        xh = xv * r
        dg_ref[...] += _colsum(dh * xh)
        dxh = dh * g_ref[...]
        dx_ref[...] = dres_ref[...] + r * (dxh - xh * _rowmean(dxh * xh))

    return pl.pallas_call(
        _after(body, 5, deps), name=name, grid=(t // tm,),
        in_specs=[pl.BlockSpec((tm, n), lambda i: (i, 0)), _whole(), pl.BlockSpec((tm, d), lambda i: (i, 0)),
                  _whole(), pl.BlockSpec((tm, d), lambda i: (i, 0))] + [_ANY] * len(deps),
        out_specs=[pl.BlockSpec((tm, d), lambda i: (i, 0)), pl.BlockSpec((1, d), lambda i: (0, 0))],
        out_shape=[S((t, d), F32), S((1, d), F32)],
        compiler_params=_params(("arbitrary",)),
    )(dp, w, x, g, dres, *deps)


def _tn_matmul(a, b, *, tmc, tk, out_dtype, name, deps=()):
    t, m = a.shape
    n = b.shape[1]
    nk = t // tk

    def body(a_ref, b_ref, o_ref, acc_ref):
        k = pl.program_id(1)

        @pl.when(k == 0)
        def _():
            acc_ref[...] = jnp.zeros_like(acc_ref)

        acc_ref[...] += _dot_tn(a_ref[...], b_ref[...])

        @pl.when(k == nk - 1)
        def _():
            o_ref[...] = acc_ref[...].astype(out_dtype)

    return pl.pallas_call(
        _after(body, 2, deps), name=name, grid=(m // tmc, nk),
        in_specs=[pl.BlockSpec((tk, tmc), lambda j, k: (k, j)), pl.BlockSpec((tk, n), lambda j, k: (k, 0))]
        + [_ANY] * len(deps),
        out_specs=pl.BlockSpec((tmc, n), lambda j, k: (j, 0)),
        out_shape=S((m, n), out_dtype),
        scratch_shapes=[pltpu.VMEM((tmc, n), F32)],
        compiler_params=_params(("arbitrary", "arbitrary")),
    )(a, b, *deps)


def _silu_parts(gt):
    sg = _sigmoid(gt)
    return gt * sg, sg * (1.0 + gt * (1.0 - sg))


def _attn_head(q_b, k_b, v_b):
    s = _dot_nt(q_b, k_b) * (HD ** -0.5)
    e = jnp.exp(s - jnp.max(s, axis=-1, keepdims=True))
    prob = e / jnp.sum(e, axis=-1, keepdims=True)
    return prob, _dot(prob.astype(BF16), v_b)


def _rms_residual(x, o, g):
    r = lax.rsqrt(_rowmean(o * o) + EPS)
    return x + o * r * g


def _rms_post_bwd(dres, o, g):
    r = lax.rsqrt(_rowmean(o * o) + EPS)
    oh = o * r
    doh = dres * g
    return r * (doh - oh * _rowmean(doh * oh)), _colsum(dres * oh)


def _sgu_chunk(vn_b, ws_ref, bmap_ref, masks):
    sg = bmap_ref[...]
    for h in range(NH):
        sg = sg + masks[h] * _dot(ws_ref[h], vn_b)
    return sg


def _shift_copies(buf, sh):
    n = buf.shape[0] - 8
    for b in range(1, 8):
        sh[b - 1, pl.ds(0, n), :] = buf[pl.ds(b, n), :]


def _rows_at(buf, sh, off):
    b = off % 8
    if b == 0 or sh is None:
        return buf[pl.ds(off, 32), :]
    return sh[b - 1, pl.ds(off - b, 32), :]


def _taps_causal(buf, sh, w_ref, taps, bias):
    rows = buf.shape[0] - HALO
    outs = []
    for rb in range(rows // 32):
        acc = None
        for k in range(taps):
            term = w_ref[k:k + 1, :] * _rows_at(buf, sh, rb * 32 + HALO - (taps - 1 - k))
            acc = term if acc is None else acc + term
        outs.append(acc if bias is None else acc + bias)
    return outs


def _taps_anticausal(buf, sh, w_ref, taps):
    rows = buf.shape[0] - HALO
    outs = []
    for rb in range(rows // 32):
        acc = None
        for k in range(taps):
            term = w_ref[k:k + 1, :] * _rows_at(buf, sh, rb * 32 + (taps - 1 - k))
            acc = term if acc is None else acc + term
        outs.append(acc)
    return outs


def _fold8(a):
    return a[0:8] + a[8:16] + a[16:24] + a[24:32]


def _tap_grads(d_ref, buf, sh, acc_ref, taps):
    rows = buf.shape[0] - HALO
    for rb in range(rows // 32):
        dv = d_ref[rb * 32:(rb + 1) * 32, :]
        for k in range(taps):
            prod = dv * _rows_at(buf, sh, rb * 32 + HALO - (taps - 1 - k))
            acc_ref[k * 8:(k + 1) * 8, :] += _fold8(prod)


def _halo_spec(n, nt, reverse, tm):
    per = tm // HALO
    if reverse:
        return pl.BlockSpec((HALO, n), lambda i: (jnp.maximum((nt - 1 - i) * per - 1, 0), 0))
    return pl.BlockSpec((HALO, n), lambda i: (jnp.maximum(i * per - 1, 0), 0))


def _even_fwd(x, p, lng, lnb, ws, bmap, wc, kv, wout, pg):
    t = x.shape[0]
    tm = min(TM_FWD, t)
    nt = t // tm

    def body(x_ref, p_ref, ph_ref, lng_ref, lnb_ref, ws_ref, bmap_ref, wc_ref, kv_ref, wout_ref, pg_ref,
             o_ref, x1_ref, ybuf, cbuf):
        i = pl.program_id(0)
        masks = _group_masks()
        vh, _ = _ln_stats(p_ref[:, E_V:E_V + BW])
        vn = vh * lng_ref[...] + lnb_ref[...]
        for c in range(tm // CHUNK):
            sl = slice(c * CHUNK, (c + 1) * CHUNK)
            sg = _sgu_chunk(vn[sl].astype(BF16), ws_ref, bmap_ref, masks)
            gate, _ = _silu_parts(p_ref[sl, E_GATE:E_GATE + BW])
            ybuf[sl, 0:BW] = (p_ref[sl, E_U:E_U + BW] * sg * gate).astype(BF16)

        cbuf[0:HALO] = jnp.where(i > 0, ph_ref[:, E_CG:E_CG + BW] * ph_ref[:, E_XIN:E_XIN + BW], 0.0)
        cbuf[HALO:HALO + tm] = p_ref[:, E_CG:E_CG + BW] * p_ref[:, E_XIN:E_XIN + BW]
        conv = _taps_causal(cbuf, None, wc_ref, 3, None)
        for rb, cv in enumerate(conv):
            sl = slice(rb * 32, (rb + 1) * 32)
            gate, _ = _silu_parts(p_ref[sl, E_GATE + BW:E_GATE + 2 * BW])
            ybuf[sl, BW:2 * BW] = (p_ref[sl, E_BG:E_BG + BW] * cv * gate).astype(BF16)

        for h in range(NH):
            qs = slice(E_Q + h * HD, E_Q + (h + 1) * HD)
            _, yx = _attn_head(p_ref[:, qs].astype(BF16), kv_ref[:, h * HD:(h + 1) * HD],
                               kv_ref[:, XA + h * HD:XA + (h + 1) * HD])
            gs = slice(E_GATE + 2 * BW + h * HD, E_GATE + 2 * BW + (h + 1) * HD)
            gate, _ = _silu_parts(p_ref[:, gs])
            ybuf[:, 2 * BW + h * HD:2 * BW + (h + 1) * HD] = (yx * gate).astype(BF16)

        o = _dot(ybuf[...], wout_ref[...])
        o_ref[...] = o
        x1_ref[...] = _rms_residual(x_ref[...], o, pg_ref[...])

    tile = lambda n: pl.BlockSpec((tm, n), lambda i: (i, 0))
    return pl.pallas_call(
        body, name="even_fwd", grid=(nt,),
        in_specs=[tile(D), tile(EVEN_IN), _halo_spec(EVEN_IN, nt, False, tm)] + [_whole()] * 8,
        out_specs=[tile(D), tile(D)],
        out_shape=[S((t, D), F32), S((t, D), F32)],
        scratch_shapes=[pltpu.VMEM((tm, MIX), BF16), pltpu.VMEM((tm + HALO, BW), F32)],
        compiler_params=_params(("arbitrary",)),
    )(x, p, p, lng, lnb, ws, bmap, wc, kv, wout, pg)


def _even_bwd(dres, o, p, lng, lnb, ws, wst, bmap, wc, kv, wout, pg):
    t = dres.shape[0]
    tm = min(TM_BWD, t)
    nt = t // tm

    def body(dres_ref, o_ref, p_ref, ph_ref, lng_ref, lnb_ref, ws_ref, wst_ref, bmap_ref, wc_ref, kv_ref, wout_ref,
             pg_ref, dp_ref, y_ref, do_ref, dpg_ref, dws_ref, dbs_ref, dlng_ref, dlnb_ref, dwc_ref, dkv_ref,
             dy, cbuf, gbuf, dconv, carry, dvn, dbmap, wacc):
        i = pl.program_id(0)
        ti = nt - 1 - i
        masks = _group_masks()

        @pl.when(i == 0)
        def _():
            for ref in (dpg_ref, dws_ref, dlng_ref, dlnb_ref, dkv_ref, dbmap, wacc):
                ref[...] = jnp.zeros_like(ref)

        do, dpg = _rms_post_bwd(dres_ref[...], o_ref[...], pg_ref[...])
        dpg_ref[...] += dpg
        do_b = do.astype(BF16)
        do_ref[...] = do_b
        dy[...] = _dot_nt(do_b, wout_ref[...])

        vh, rs = _ln_stats(p_ref[:, E_V:E_V + BW])
        vn = vh * lng_ref[...] + lnb_ref[...]
        for c in range(tm // CHUNK):
            sl = slice(c * CHUNK, (c + 1) * CHUNK)
            vn_b = vn[sl].astype(BF16)
            sg = _sgu_chunk(vn_b, ws_ref, bmap_ref, masks)
            u = p_ref[sl, E_U:E_U + BW]
            gate, dgate = _silu_parts(p_ref[sl, E_GATE:E_GATE + BW])
            dyc = dy[sl, 0:BW]
            ya = u * sg
            y_ref[sl, 0:BW] = (ya * gate).astype(BF16)
            dp_ref[sl, E_GATE:E_GATE + BW] = (dyc * ya * dgate).astype(BF16)
            dya = dyc * gate
            dp_ref[sl, E_U:E_U + BW] = (dya * sg).astype(BF16)
            dsg = dya * u
            dbmap[...] += dsg
            dsg_b = dsg.astype(BF16)
            acc = jnp.zeros((CHUNK, BW), F32)
            for h in range(NH):
                dws_ref[h] += _dot_nt((dsg * masks[h]).astype(BF16), vn_b)
                acc = acc + masks[h] * _dot(wst_ref[h], dsg_b)
            dvn[sl, :] = acc
        dn = dvn[...]
        dlng_ref[...] += _colsum(dn * vh)
        dlnb_ref[...] += _colsum(dn)
        dp_ref[:, E_V:E_V + BW] = _ln_bwd(dn, vh, rs, lng_ref[...]).astype(BF16)

        cbuf[0:HALO] = jnp.where(ti > 0, ph_ref[:, E_CG:E_CG + BW] * ph_ref[:, E_XIN:E_XIN + BW], 0.0)
        cbuf[HALO:HALO + tm] = p_ref[:, E_CG:E_CG + BW] * p_ref[:, E_XIN:E_XIN + BW]
        conv = _taps_causal(cbuf, None, wc_ref, 3, None)
        for rb, cv in enumerate(conv):
            sl = slice(rb * 32, (rb + 1) * 32)
            gate, dgate = _silu_parts(p_ref[sl, E_GATE + BW:E_GATE + 2 * BW])
            bg = p_ref[sl, E_BG:E_BG + BW]
            dyc = dy[sl, BW:2 * BW]
            yb = bg * cv
            y_ref[sl, BW:2 * BW] = (yb * gate).astype(BF16)
            dp_ref[sl, E_GATE + BW:E_GATE + 2 * BW] = (dyc * yb * dgate).astype(BF16)
            dyb = dyc * gate
            dp_ref[sl, E_BG:E_BG + BW] = (dyb * cv).astype(BF16)
            dconv[sl, :] = dyb * bg
        gbuf[0:tm] = dconv[...]
        gbuf[tm:tm + HALO] = jnp.where(i > 0, carry[...], 0.0)
        carry[...] = dconv[0:HALO]
        _tap_grads(dconv, cbuf, None, wacc, 3)
        for rb, dc in enumerate(_taps_anticausal(gbuf, None, wc_ref, 3)):
            sl = slice(rb * 32, (rb + 1) * 32)
            dp_ref[sl, E_CG:E_CG + BW] = (dc * p_ref[sl, E_XIN:E_XIN + BW]).astype(BF16)
            dp_ref[sl, E_XIN:E_XIN + BW] = (dc * p_ref[sl, E_CG:E_CG + BW]).astype(BF16)

        for h in range(NH):
            qs = slice(E_Q + h * HD, E_Q + (h + 1) * HD)
            ks = slice(h * HD, (h + 1) * HD)
            vs = slice(XA + h * HD, XA + (h + 1) * HD)
            gs = slice(E_GATE + 2 * BW + h * HD, E_GATE + 2 * BW + (h + 1) * HD)
            ys = slice(2 * BW + h * HD, 2 * BW + (h + 1) * HD)
            q_b = p_ref[:, qs].astype(BF16)
            prob, yx = _attn_head(q_b, kv_ref[:, ks], kv_ref[:, vs])
            gate, dgate = _silu_parts(p_ref[:, gs])
            dyc = dy[:, ys]
            y_ref[:, ys] = (yx * gate).astype(BF16)
            dp_ref[:, gs] = (dyc * yx * dgate).astype(BF16)
            dyx_b = (dyc * gate).astype(BF16)
            dprob = _dot_nt(dyx_b, kv_ref[:, vs])
            dkv_ref[:, vs] += _dot_tn(prob.astype(BF16), dyx_b)
            ds_b = (prob * (dprob - jnp.sum(dprob * prob, axis=-1, keepdims=True)) * (HD ** -0.5)).astype(BF16)
            dp_ref[:, qs] = _dot(ds_b, kv_ref[:, ks]).astype(BF16)
            dkv_ref[:, ks] += _dot_tn(ds_b, q_b)

        @pl.when(i == nt - 1)
        def _():
            for h in range(NH):
                dbs_ref[:, h * HD:(h + 1) * HD] = jnp.broadcast_to(
                    jnp.sum(dbmap[...] * masks[h], axis=-1, keepdims=True), (CHUNK, HD))
            for k in range(3):
                dwc_ref[k:k + 1, :] = _colsum(wacc[k * 8:(k + 1) * 8, :])
            dwc_ref[3:8, :] = jnp.zeros((5, BW), F32)
            causal = (lax.broadcasted_iota(jnp.int32, (CHUNK, CHUNK), 0)
                      >= lax.broadcasted_iota(jnp.int32, (CHUNK, CHUNK), 1))
            for h in range(NH):
                dws_ref[h] = jnp.where(causal, dws_ref[h], 0.0)

    rtile = lambda n: pl.BlockSpec((tm, n), lambda i: (nt - 1 - i, 0))
    outs = [S((t, EVEN_IN), BF16), S((t, MIX), BF16), S((t, D), BF16), S((1, D), F32), S((NH, CHUNK, CHUNK), F32),
            S((CHUNK, NH * HD), F32), S((1, BW), F32), S((1, BW), F32), S((8, BW), F32), S((N_MEM, 2 * XA), F32)]
    return pl.pallas_call(
        body, name="even_bwd", grid=(nt,),
        in_specs=[rtile(D), rtile(D), rtile(EVEN_IN), _halo_spec(EVEN_IN, nt, True, tm)] + [_whole()] * 9,
        out_specs=[rtile(EVEN_IN), rtile(MIX), rtile(D)] + [_full(s.shape) for s in outs[3:]],
        out_shape=outs,
        scratch_shapes=[pltpu.VMEM((tm, MIX), F32), pltpu.VMEM((tm + HALO, BW), F32), pltpu.VMEM((tm + HALO, BW), F32),
                        pltpu.VMEM((tm, BW), F32), pltpu.VMEM((HALO, BW), F32), pltpu.VMEM((tm, BW), F32),
                        pltpu.VMEM((CHUNK, BW), F32), pltpu.VMEM((3 * 8, BW), F32)],
        compiler_params=_params(("arbitrary",)),
    )(dres, o, p, p, lng, lnb, ws, wst, bmap, wc, kv, wout, pg)


def _pool_causal(za, zb, zc, zd, tm):
    n = tm + HALO
    zb[pl.ds(8, n - 8), :] = za[pl.ds(8, n - 8), :] + za[pl.ds(7, n - 8), :]
    zc[pl.ds(16, n - 16), :] = zb[pl.ds(16, n - 16), :] + zb[pl.ds(14, n - 16), :]
    zd[pl.ds(24, n - 24), :] = zc[pl.ds(24, n - 24), :] + zc[pl.ds(20, n - 24), :]
    s16 = zd[pl.ds(HALO, tm), :] + zd[pl.ds(HALO - 8, tm), :]
    return zb[pl.ds(HALO, tm), :], zc[pl.ds(HALO, tm), :], zd[pl.ds(HALO, tm), :], s16


def _pool_anticausal(ea, eb, ec, ed, tm):
    n = tm + HALO
    eb[pl.ds(0, n - 8), :] = ea[pl.ds(0, n - 8), :] + ea[pl.ds(1, n - 8), :]
    ec[pl.ds(0, n - 16), :] = eb[pl.ds(0, n - 16), :] + eb[pl.ds(2, n - 16), :]
    ed[pl.ds(0, n - 24), :] = ec[pl.ds(0, n - 24), :] + ec[pl.ds(4, n - 24), :]
    a16 = ed[pl.ds(0, tm), :] + ed[pl.ds(8, tm), :]
    return eb[pl.ds(0, tm), :], ec[pl.ds(0, tm), :], ed[pl.ds(0, tm), :], a16


def _pool_weights(ti, masks, tm):
    tf = (ti * tm + lax.broadcasted_iota(jnp.int32, (tm, 1), 0) + 1).astype(F32)
    inv = None
    for g, win in enumerate(POOL_WINDOWS):
        term = masks[g] * (1.0 / jnp.minimum(tf, float(win)))
        inv = term if inv is None else inv + term
    return inv


def _mix4(masks, parts):
    return masks[0] * parts[0] + masks[1] * parts[1] + masks[2] * parts[2] + masks[3] * parts[3]


def _odd_fwd(x1, tgt, p, wbd, cscale, dww, dwb, lng, lnb, wpw, pwb, kv, wout, pg):
    t = x1.shape[0]
    tm = min(TM_FWD, t)
    nt = t // tm

    def body(x_ref, tgt_ref, p_ref, ph_ref, wbd_ref, cs_ref, dww_ref, dwb_ref, lng_ref, lnb_ref, wpw_ref, pwb_ref,
             kv_ref, wout_ref, pg_ref, o_ref, dres_ref, loss_ref, conv_ref, ybuf, za, zb, zc, zd, gbuf, lacc, gsh):
        i = pl.program_id(0)
        masks = _group_masks()

        @pl.when(i == 0)
        def _():
            lacc[...] = jnp.zeros_like(lacc)

        z = p_ref[:, O_ZC:O_ZC + BW]
        za[0:HALO] = jnp.where(i > 0, ph_ref[:, O_ZC:O_ZC + BW], 0.0)
        za[HALO:HALO + tm] = z
        pooled = _mix4(masks, _pool_causal(za, zb, zc, zd, tm)) * _pool_weights(i, masks, tm) - z
        gate, _ = _silu_parts(p_ref[:, O_GATE:O_GATE + BW])
        ybuf[:, 0:BW] = (_dot(pooled.astype(BF16), wbd_ref[...]) * cs_ref[...] * gate).astype(BF16)

        gbuf[0:HALO] = jnp.where(i > 0, ph_ref[:, O_GA:O_GA + BW] * _sigmoid(ph_ref[:, O_GB:O_GB + BW]), 0.0)
        gbuf[HALO:HALO + tm] = p_ref[:, O_GA:O_GA + BW] * _sigmoid(p_ref[:, O_GB:O_GB + BW])
        _shift_copies(gbuf, gsh)
        for rb, blk in enumerate(_taps_causal(gbuf, gsh, dww_ref, CONF_K, dwb_ref[...])):
            conv_ref[rb * 32:(rb + 1) * 32, :] = blk
        zh, _ = _ln_stats(conv_ref[...])
        zn = zh * lng_ref[...] + lnb_ref[...]
        yd = _dot((zn * _sigmoid(zn)).astype(BF16), wpw_ref[...]) + pwb_ref[...]
        gate, _ = _silu_parts(p_ref[:, O_GATE + BW:O_GATE + 2 * BW])
        ybuf[:, BW:2 * BW] = (yd * gate).astype(BF16)

        for h in range(NH):
            qs = slice(O_Q + h * HD, O_Q + (h + 1) * HD)
            _, yx = _attn_head(p_ref[:, qs].astype(BF16), kv_ref[:, h * HD:(h + 1) * HD],
                               kv_ref[:, XA + h * HD:XA + (h + 1) * HD])
            gs = slice(O_GATE + 2 * BW + h * HD, O_GATE + 2 * BW + (h + 1) * HD)
            gate, _ = _silu_parts(p_ref[:, gs])
            ybuf[:, 2 * BW + h * HD:2 * BW + (h + 1) * HD] = (yx * gate).astype(BF16)

        o = _dot(ybuf[...], wout_ref[...])
        o_ref[...] = o
        err = _rms_residual(x_ref[...], o, pg_ref[...]) - tgt_ref[...]
        lacc[...] += _colsum(err * err)
        dres_ref[...] = err * (1.0 / D)

        @pl.when(i == nt - 1)
        def _():
            loss_ref[...] = jnp.full((1, HD), jnp.sum(lacc[...]) * (0.5 / D), F32)

    tile = lambda n: pl.BlockSpec((tm, n), lambda i: (i, 0))
    ext = pltpu.VMEM((tm + HALO, BW), F32)
    return pl.pallas_call(
        body, name="odd_fwd", grid=(nt,),
        in_specs=[tile(D), tile(D), tile(ODD_IN), _halo_spec(ODD_IN, nt, False, tm)] + [_whole()] * 11,
        out_specs=[tile(D), tile(D), _full((1, HD)), tile(BW)],
        out_shape=[S((t, D), F32), S((t, D), F32), S((1, HD), F32), S((t, BW), F32)],
        scratch_shapes=[pltpu.VMEM((tm, MIX), BF16), ext, ext, ext, ext, ext,
                        pltpu.VMEM((1, D), F32), pltpu.VMEM((7, tm + HALO, BW), F32)],
        compiler_params=_params(("arbitrary",)),
    )(x1, tgt, p, p, wbd, cscale, dww, dwb, lng, lnb, wpw, pwb, kv, wout, pg)


def _odd_bwd(dres, o, p, conv, wbd, cscale, dww, dwb, lng, lnb, wpw, pwb, kv, wout, pg):
    t = dres.shape[0]
    tm = min(TM_BWD, t)
    nt = t // tm

    def body(dres_ref, o_ref, p_ref, ph_ref, conv_ref, wbd_ref, cs_ref, dww_ref, dwb_ref, lng_ref, lnb_ref, wpw_ref,
             pwb_ref, kv_ref, wout_ref, pg_ref, dp_ref, y_ref, do_ref, dpg_ref, dwbd_ref, dcs_ref, ddww_ref, ddwb_ref,
             dlng_ref, dlnb_ref, dwpw_ref, dpwb_ref, dkv_ref,
             dy, za, zb, zc, zd, gbuf, hbuf, tmp, carry_e, carry_d, wacc, gsh, hsh):
        i = pl.program_id(0)
        ti = nt - 1 - i
        masks = _group_masks()

        @pl.when(i == 0)
        def _():
            for ref in (dpg_ref, dwbd_ref, dcs_ref, ddwb_ref, dlng_ref, dlnb_ref, dwpw_ref, dpwb_ref, dkv_ref, wacc):
                ref[...] = jnp.zeros_like(ref)

        do, dpg = _rms_post_bwd(dres_ref[...], o_ref[...], pg_ref[...])
        dpg_ref[...] += dpg
        do_b = do.astype(BF16)
        do_ref[...] = do_b
        dy[...] = _dot_nt(do_b, wout_ref[...])

        z = p_ref[:, O_ZC:O_ZC + BW]
        za[0:HALO] = jnp.where(ti > 0, ph_ref[:, O_ZC:O_ZC + BW], 0.0)
        za[HALO:HALO + tm] = z
        inv = _pool_weights(ti, masks, tm)
        pooled_b = (_mix4(masks, _pool_causal(za, zb, zc, zd, tm)) * inv - z).astype(BF16)
        pm = _dot(pooled_b, wbd_ref[...])
        gate, dgate = _silu_parts(p_ref[:, O_GATE:O_GATE + BW])
        dyc = dy[:, 0:BW]
        yc = pm * cs_ref[...]
        y_ref[:, 0:BW] = (yc * gate).astype(BF16)
        dp_ref[:, O_GATE:O_GATE + BW] = (dyc * yc * dgate).astype(BF16)
        dyc = dyc * gate
        dcs_ref[...] += _colsum(dyc * pm)
        dpm_b = (dyc * cs_ref[...]).astype(BF16)
        dwbd_ref[...] += _dot_tn(pooled_b, dpm_b)
        dpool = _dot_nt(dpm_b, wbd_ref[...])
        e = dpool * inv
        za[0:tm] = e
        za[tm:tm + HALO] = jnp.where(i > 0, carry_e[...], 0.0)
        carry_e[...] = e[0:HALO]
        dp_ref[:, O_ZC:O_ZC + BW] = (_mix4(masks, _pool_anticausal(za, zb, zc, zd, tm)) - dpool).astype(BF16)

        gbuf[0:HALO] = jnp.where(ti > 0, ph_ref[:, O_GA:O_GA + BW] * _sigmoid(ph_ref[:, O_GB:O_GB + BW]), 0.0)
        gbuf[HALO:HALO + tm] = p_ref[:, O_GA:O_GA + BW] * _sigmoid(p_ref[:, O_GB:O_GB + BW])
        _shift_copies(gbuf, gsh)
        zh, rs = _ln_stats(conv_ref[...])
        zn = zh * lng_ref[...] + lnb_ref[...]
        zs, dsilu = _silu_parts(zn)
        zs_b = zs.astype(BF16)
        yd = _dot(zs_b, wpw_ref[...]) + pwb_ref[...]
        gate, dgate = _silu_parts(p_ref[:, O_GATE + BW:O_GATE + 2 * BW])
        dyc = dy[:, BW:2 * BW]
        y_ref[:, BW:2 * BW] = (yd * gate).astype(BF16)
        dp_ref[:, O_GATE + BW:O_GATE + 2 * BW] = (dyc * yd * dgate).astype(BF16)
        dyd = dyc * gate
        dyd_b = dyd.astype(BF16)
        dpwb_ref[...] += _colsum(dyd)
        dwpw_ref[...] += _dot_tn(zs_b, dyd_b)
        dzn = _dot_nt(dyd_b, wpw_ref[...]) * dsilu
        dlng_ref[...] += _colsum(dzn * zh)
        dlnb_ref[...] += _colsum(dzn)
        dzd = _ln_bwd(dzn, zh, rs, lng_ref[...])
        ddwb_ref[...] += _colsum(dzd)
        tmp[...] = dzd
        hbuf[0:tm] = dzd
        hbuf[tm:tm + HALO] = jnp.where(i > 0, carry_d[...], 0.0)
        carry_d[...] = dzd[0:HALO]
        _shift_copies(hbuf, hsh)
        _tap_grads(tmp, gbuf, gsh, wacc, CONF_K)
        for rb, dzg in enumerate(_taps_anticausal(hbuf, hsh, dww_ref, CONF_K)):
            sl = slice(rb * 32, (rb + 1) * 32)
            sgb = _sigmoid(p_ref[sl, O_GB:O_GB + BW])
            dp_ref[sl, O_GA:O_GA + BW] = (dzg * sgb).astype(BF16)
            dp_ref[sl, O_GB:O_GB + BW] = (dzg * p_ref[sl, O_GA:O_GA + BW] * sgb * (1.0 - sgb)).astype(BF16)

        for h in range(NH):
            qs = slice(O_Q + h * HD, O_Q + (h + 1) * HD)
            ks = slice(h * HD, (h + 1) * HD)
            vs = slice(XA + h * HD, XA + (h + 1) * HD)
            gs = slice(O_GATE + 2 * BW + h * HD, O_GATE + 2 * BW + (h + 1) * HD)
            ys = slice(2 * BW + h * HD, 2 * BW + (h + 1) * HD)
            q_b = p_ref[:, qs].astype(BF16)
            prob, yx = _attn_head(q_b, kv_ref[:, ks], kv_ref[:, vs])
            gate, dgate = _silu_parts(p_ref[:, gs])
            dyc = dy[:, ys]
            y_ref[:, ys] = (yx * gate).astype(BF16)
            dp_ref[:, gs] = (dyc * yx * dgate).astype(BF16)
            dyx_b = (dyc * gate).astype(BF16)
            dprob = _dot_nt(dyx_b, kv_ref[:, vs])
            dkv_ref[:, vs] += _dot_tn(prob.astype(BF16), dyx_b)
            ds_b = (prob * (dprob - jnp.sum(dprob * prob, axis=-1, keepdims=True)) * (HD ** -0.5)).astype(BF16)
            dp_ref[:, qs] = _dot(ds_b, kv_ref[:, ks]).astype(BF16)
            dkv_ref[:, ks] += _dot_tn(ds_b, q_b)

        @pl.when(i == nt - 1)
        def _():
            for k in range(CONF_K):
                ddww_ref[k:k + 1, :] = _colsum(wacc[k * 8:(k + 1) * 8, :])
            ddww_ref[CONF_K:CONF_K + 1, :] = jnp.zeros((1, BW), F32)

    rtile = lambda n: pl.BlockSpec((tm, n), lambda i: (nt - 1 - i, 0))
    outs = [S((t, ODD_IN), BF16), S((t, MIX), BF16), S((t, D), BF16), S((1, D), F32), S((BW, BW), F32),
            S((1, BW), F32), S((CONF_K + 1, BW), F32), S((1, BW), F32), S((1, BW), F32), S((1, BW), F32),
            S((BW, BW), F32), S((1, BW), F32), S((N_MEM, 2 * XA), F32)]
    ext = pltpu.VMEM((tm + HALO, BW), F32)
    return pl.pallas_call(
        body, name="odd_bwd", grid=(nt,),
        in_specs=[rtile(D), rtile(D), rtile(ODD_IN), _halo_spec(ODD_IN, nt, True, tm), rtile(BW)] + [_whole()] * 11,
        out_specs=[rtile(ODD_IN), rtile(MIX), rtile(D)] + [_full(s.shape) for s in outs[3:]],
        out_shape=outs,
        scratch_shapes=[pltpu.VMEM((tm, MIX), F32), ext, ext, ext, ext, ext, ext, pltpu.VMEM((tm, BW), F32),
                        pltpu.VMEM((HALO, BW), F32), pltpu.VMEM((HALO, BW), F32), pltpu.VMEM((CONF_K * 8, BW), F32),
                        pltpu.VMEM((7, tm + HALO, BW), F32), pltpu.VMEM((7, tm + HALO, BW), F32)],
        compiler_params=_params(("arbitrary",)),
    )(dres, o, p, p, conv, wbd, cscale, dww, dwb, lng, lnb, wpw, pwb, kv, wout, pg)


def _pick_rows(n):
    for rows in (640, 2432, 1024, 768):
        if n % rows == 0:
            return rows
    return n


def _pad_rows(a, rows):
    return jnp.pad(a, ((0, rows - a.shape[0]), (0, 0)))


def _step(x, mem, tgt, ex):
    t = x.shape[0]
    tm = min(256, t)
    w, deps = ex.first()
    causal = jnp.tril(jnp.ones((CHUNK, CHUNK), bool))
    ws = jnp.where(causal[None], w["even_a_ws"], 0.0).astype(BF16)
    wst = jnp.transpose(ws, (0, 2, 1))
    bmap = jnp.repeat(w["even_a_bs"].T, GRP, axis=1)
    wc = _pad_rows(w["even_b_conv"], 8)
    wbd = jax.scipy.linalg.block_diag(*[w["odd_c_wgrp"][g] for g in range(NH)]).astype(BF16)
    dww = _pad_rows(w["odd_d_dw_w"], CONF_K + 1)
    tk = min(1024, t)
    zeros = jnp.zeros_like(mem)

    p_e, h_e = _rms_matmul(x, w["even_pre_g"], w["even_w_in"], tm=tm, name="in_even", transposed=True, deps=deps)
    w.update(ex.even_rest(h_e))
    kv_e, memn_e = _rms_matmul(mem, w["even_mem_g"], w["even_w_kv"], tm=N_MEM, name="kv_even")
    kv_e = kv_e.astype(BF16)
    even_args = (w["even_a_ln_g"], w["even_a_ln_b"], ws)
    o_e, x1 = _even_fwd(x, p_e, *even_args, bmap, wc, kv_e, w["even_w_out"], w["even_post_g"])
    w.update(ex.odd(o_e))
    kv_o, memn_o = _rms_matmul(mem, w["odd_mem_g"], w["odd_w_kv"], tm=N_MEM, name="kv_odd")
    kv_o = kv_o.astype(BF16)
    p_o, h_o = _rms_matmul(x1, w["odd_pre_g"], w["odd_w_in"], tm=tm, name="in_odd", transposed=True)
    odd_args = (wbd, w["odd_c_scale"], dww, w["odd_d_dw_b"], w["odd_d_ln_g"], w["odd_d_ln_b"], w["odd_d_pw_w"],
                w["odd_d_pw_b"], kv_o, w["odd_w_out"], w["odd_post_g"])
    o_o, dres, loss, conv_o = _odd_fwd(x1, tgt, p_o, *odd_args)

    g = {}
    (dp_o, y_o, do_o, post_g_o, dwbd, g["odd_c_scale"], ddww, g["odd_d_dw_b"], g["odd_d_ln_g"], g["odd_d_ln_b"],
     dwpw, g["odd_d_pw_b"], dkv_o) = _odd_bwd(dres, o_o, p_o, conv_o, *odd_args)
    g["odd_post_g"] = post_g_o
    g["odd_d_dw_w"] = ddww[:CONF_K]
    dkv_o = dkv_o.astype(BF16)
    deps = ex.send("odd_rest", {
        "odd_w_out": _tn_matmul(y_o, do_o, tmc=MIX, tk=tk, out_dtype=BF16, name="dw_out_odd"),
        "odd_w_kv": _tn_matmul(memn_o, dkv_o, tmc=D, tk=N_MEM, out_dtype=BF16, name="dw_kv_odd"),
        "odd_d_pw_w": dwpw.astype(BF16), "loss": loss,
        "odd_c_wgrp": jnp.concatenate([dwbd[i * GRP:(i + 1) * GRP, i * GRP:(i + 1) * GRP] for i in range(NH)])})
    deps = ex.send("odd_in", {"odd_w_in": _tn_matmul(dp_o, h_o, tmc=_pick_rows(ODD_IN), tk=tk, out_dtype=BF16,
                                                     name="dw_in_odd", deps=deps)})
    dx1, g["odd_pre_g"] = _nt_matmul_rms_bwd(dp_o, w["odd_w_in"], x1, w["odd_pre_g"], dres, tm=min(256, t),
                                             name="dx_odd", transposed=True, deps=deps)
    _, g["odd_mem_g"] = _nt_matmul_rms_bwd(dkv_o, w["odd_w_kv"], mem, w["odd_mem_g"], zeros, tm=N_MEM,
                                           name="dmem_odd")

    (dp_e, y_e, do_e, post_g_e, dws, dbs, ln_g_e, ln_b_e, dwc, dkv_e) = _even_bwd(
        dx1, o_e, p_e, *even_args, wst, bmap, wc, kv_e, w["even_w_out"], w["even_post_g"])
    g["even_b_conv"] = dwc[:3]
    dkv_e = dkv_e.astype(BF16)
    deps = ex.send("even_rest", {
        "even_w_out": _tn_matmul(y_e, do_e, tmc=MIX, tk=tk, out_dtype=BF16, name="dw_out_even"),
        "even_w_kv": _tn_matmul(memn_e, dkv_e, tmc=D, tk=N_MEM, out_dtype=BF16, name="dw_kv_even"),
        "even_a_ln_g": ln_g_e, "even_a_ln_b": ln_b_e,
        "even_a_ws": dws.reshape(NH * CHUNK, CHUNK), "even_a_bs": dbs[:, ::HD].T})
    g["even_w_in"] = _tn_matmul(dp_e, h_e, tmc=_pick_rows(EVEN_IN), tk=tk, out_dtype=BF16, name="dw_in_even",
                                deps=deps)
    deps = ex.send("even_in", g)
    grad_x, pre_g_e = _nt_matmul_rms_bwd(dp_e, w["even_w_in"], x, w["even_pre_g"], dx1, tm=min(256, t),
                                         name="dx_even", transposed=True, deps=deps)
    _, mem_g_e = _nt_matmul_rms_bwd(dkv_e, w["even_w_kv"], mem, w["even_mem_g"], zeros, tm=N_MEM, name="dmem_even",
                                    deps=(grad_x,))
    ex.send("even_gains", {"even_pre_g": pre_g_e, "even_mem_g": mem_g_e, "even_post_g": post_g_e})
    return grad_x, mem_g_e


def _place():
    return lax.axis_index("x"), lax.axis_index("y"), lax.axis_index("c")


def _index(px, py, pc):
    return 4 * px + 2 * py + pc


_COPIES = N_DEV - 1


def _all_gather(arrs, name):
    n = len(arrs)

    def body(*refs):
        ins, outs = refs[:n], refs[n:2 * n]
        send_sems, recv_sems, local_sems = refs[2 * n:]
        x, y, c = _place()
        me, sibling = (x, y, c), (x, y, 1 - c)
        chips = [(1 - x, y), (x, 1 - y), (1 - x, 1 - y)]

        def copy(a, k, block, to, src=None):
            dst = outs[a].at[_index(*block)]
            return pltpu.make_async_remote_copy(
                src_ref=dst if src is None else src, dst_ref=dst, send_sem=send_sems.at[a * _COPIES + k],
                recv_sem=recv_sems.at[a * _COPIES + k], device_id=to, device_id_type=MESH)

        mine = [pltpu.make_async_copy(ins[a], outs[a].at[_index(*me)], local_sems.at[a]) for a in range(n)]
        first = []
        for a in range(n):
            mine[a].start()
            first.append(copy(a, 0, me, sibling, src=ins[a]))
            first += [copy(a, 1 + j, me, (*chip, c), src=ins[a]) for j, chip in enumerate(chips)]
        for cp in first:
            cp.start()
        passed = []
        for j, chip in enumerate(chips):
            for a in range(n):
                copy(a, 1 + j, (*chip, c), me).wait_recv()
                passed.append(copy(a, 4 + j, (*chip, c), sibling))
                passed[-1].start()
        for a in range(n):
            copy(a, 0, sibling, me).wait_recv()
            for j, chip in enumerate(chips):
                copy(a, 4 + j, (*chip, 1 - c), me).wait_recv()
        for cp in first + passed:
            cp.wait_send()
        for cp in mine:
            cp.wait()

    return pl.pallas_call(
        body, name=name, in_specs=[_ANY] * n, out_specs=[_ANY] * n,
        out_shape=[S((N_DEV,) + a.shape, a.dtype) for a in arrs],
        scratch_shapes=[pltpu.SemaphoreType.DMA((n * _COPIES,)), pltpu.SemaphoreType.DMA((n * _COPIES,)),
                        pltpu.SemaphoreType.DMA((n,))],
    )(*arrs)


_HBM = pl.BlockSpec(memory_space=pltpu.HBM)
_SEM = pl.BlockSpec(memory_space=pltpu.SEMAPHORE)
_EFFECT = pltpu.SideEffectType.DATAFLOW_SIDE_EFFECTING


_ALL_FLIPS = [(k >> 2 & 1, k >> 1 & 1, k & 1) for k in range(1, N_DEV)]
_CHIP_FLIPS = [(1, 0, 0), (0, 1, 0), (1, 1, 0)]
_FLIPS = {"gather": _ALL_FLIPS, "scatter": _ALL_FLIPS, "gather_chips": [(0, 0, 1)] + _CHIP_FLIPS,
          "scatter_chips": _CHIP_FLIPS}


def _landing_shape(kind, a):
    return (N_DEV,) + a.shape if kind.startswith("gather") else a.shape


def _exchange_copies(kinds, srcs, lands, send_sems, recv_sems, local_sems, arriving):
    x, y, c = _place()
    mine = _index(x, y, c)
    remote, local = [], []
    for a, kind in enumerate(kinds):
        by_chip = kind == "scatter_chips"
        here = 2 * x + y if by_chip else mine
        own = srcs[a] if kind.startswith("gather") else srcs[a].at[here]
        local.append(pltpu.make_async_copy(own, lands[a].at[here], local_sems.at[a]))
        for k, (fx, fy, fc) in enumerate(_FLIPS[kind]):
            peer = (1 - x if fx else x, 1 - y if fy else y, 1 - c if fc else c)
            there = 2 * peer[0] + peer[1] if by_chip else _index(*peer)
            remote.append(pltpu.make_async_remote_copy(
                src_ref=srcs[a] if kind.startswith("gather") else srcs[a].at[there],
                dst_ref=lands[a].at[there if arriving else here],
                send_sem=send_sems.at[a * _COPIES + k], recv_sem=recv_sems.at[a * _COPIES + k],
                device_id=peer, device_id_type=MESH))
    return remote, local


def _exchange_start(items, name, deps=()):
    kinds = [kind for kind, _ in items]
    srcs = [a for _, a in items]
    n = len(items)
    lands = [lax.empty(_landing_shape(kind, a), a.dtype) for kind, a in items]

    def body(*refs):
        send_sems, recv_sems, local_sems = refs[2 * n + len(deps):2 * n + len(deps) + 3]
        remote, local = _exchange_copies(kinds, refs[:n], refs[n:2 * n], send_sems, recv_sems, local_sems, False)
        for cp in local + remote:
            cp.start()
        refs[-1][...] = jnp.zeros_like(refs[-1])

    held = [pltpu.HBM(a.shape, a.dtype) for a in srcs + lands]
    res = pl.pallas_call(
        body, name=name,
        out_shape=(pltpu.SemaphoreType.DMA((n * _COPIES,)), pltpu.SemaphoreType.DMA((n * _COPIES,)),
                   pltpu.SemaphoreType.DMA((n,)), *held, S((8, 128), F32)),
        in_specs=[_HBM] * (2 * n) + [_ANY] * len(deps),
        out_specs=(_SEM, _SEM, _SEM, *[_HBM] * (2 * n), _whole()),
        input_output_aliases={i: 3 + i for i in range(2 * n)},
        compiler_params=pltpu.CompilerParams(has_side_effects=_EFFECT),
    )(*[pltpu.with_memory_space_constraint(a, pltpu.HBM) for a in srcs + lands], *deps)
    return (kinds, res[:3], res[3:3 + 2 * n]), res[-1]


def _exchange_wait(handle, after, name):
    kinds, sems, held = handle
    n = len(kinds)

    def body(*refs):
        send_sems, recv_sems, local_sems = refs[2 * n:2 * n + 3]
        remote, local = _exchange_copies(kinds, refs[:n], refs[n:2 * n], send_sems, recv_sems, local_sems, True)
        for cp in remote:
            cp.wait_send()
            cp.wait_recv()
        for cp in local:
            cp.wait()

    res = pl.pallas_call(
        body, name=name, out_shape=[pltpu.HBM(a.shape, a.dtype) for a in held],
        in_specs=[_HBM] * (2 * n) + [_SEM] * 3 + [_ANY] * len(after), out_specs=[_HBM] * (2 * n),
        input_output_aliases={i: i for i in range(2 * n)},
        compiler_params=pltpu.CompilerParams(has_side_effects=_EFFECT),
    )(*held, *sems, *after)
    return res[n:]


_CHIPS = [(0, 0), (0, 1), (1, 0), (1, 1)]
_N_CHIPS = len(_CHIPS)


def _sibling_forward(lands, name):
    n = len(lands)

    def body(*refs):
        ins, outs = refs[:n], refs[n:2 * n]
        send_sems, recv_sems = refs[2 * n:]
        x, y, c = _place()
        sent, arriving = [], []
        for a in range(n):
            for j, (fx, fy, _) in enumerate(_CHIP_FLIPS):
                chip = (1 - x if fx else x, 1 - y if fy else y)
                sems = dict(send_sem=send_sems.at[a * 3 + j], recv_sem=recv_sems.at[a * 3 + j],
                            device_id=(x, y, 1 - c), device_id_type=MESH)
                mine, theirs = _index(*chip, c), _index(*chip, 1 - c)
                sent.append(pltpu.make_async_remote_copy(src_ref=ins[a].at[mine], dst_ref=outs[a].at[mine], **sems))
                arriving.append(pltpu.make_async_remote_copy(src_ref=ins[a].at[theirs], dst_ref=outs[a].at[theirs],
                                                             **sems))
        for cp in sent:
            cp.start()
        for cp in sent:
            cp.wait_send()
        for cp in arriving:
            cp.wait_recv()

    return pl.pallas_call(
        body, name=name, in_specs=[_ANY] * n, out_specs=[_ANY] * n,
        out_shape=[S(a.shape, a.dtype) for a in lands], input_output_aliases={a: a for a in range(n)},
        scratch_shapes=[pltpu.SemaphoreType.DMA((3 * n,)), pltpu.SemaphoreType.DMA((3 * n,))],
    )(*lands)


def _sibling_swap(arrs, name):
    n = len(arrs)

    def body(*refs):
        ins, outs = refs[:n], refs[n:2 * n]
        send_sems, recv_sems = refs[2 * n:]
        x, y, c = _place()
        copies = []
        for a in range(n):
            for q, chip in enumerate(_CHIPS):
                copies.append(pltpu.make_async_remote_copy(
                    src_ref=ins[a].at[_index(*chip, 1 - c)], dst_ref=outs[a].at[q],
                    send_sem=send_sems.at[a * _N_CHIPS + q], recv_sem=recv_sems.at[a * _N_CHIPS + q],
                    device_id=(x, y, 1 - c), device_id_type=MESH))
        for cp in copies:
            cp.start()
        for cp in copies:
            cp.wait_send()
            cp.wait_recv()

    return pl.pallas_call(
        body, name=name, in_specs=[_ANY] * n, out_specs=[_ANY] * n,
        out_shape=[S((_N_CHIPS,) + a.shape[1:], a.dtype) for a in arrs],
        scratch_shapes=[pltpu.SemaphoreType.DMA((_N_CHIPS * n,)), pltpu.SemaphoreType.DMA((_N_CHIPS * n,))],
    )(*arrs)


def _add_partials(mine, theirs, *, tr, name):
    _, r, c = mine.shape

    def body(mine_ref, theirs_ref, out_ref):
        core = lax.axis_index("c")
        own = jnp.where(core == 0, mine_ref[0].astype(F32), mine_ref[1].astype(F32))
        out_ref[0] = (own + theirs_ref[0].astype(F32)).astype(out_ref.dtype)

    return pl.pallas_call(
        body, name=name, grid=(_N_CHIPS, r // tr),
        in_specs=[pl.BlockSpec((2, tr, c), lambda q, i: (q, i, 0)), pl.BlockSpec((1, tr, c), lambda q, i: (q, i, 0))],
        out_specs=pl.BlockSpec((1, tr, c), lambda q, i: (q, i, 0)),
        out_shape=S((_N_CHIPS, r, c), mine.dtype),
        compiler_params=_params(("arbitrary", "arbitrary")),
    )(mine, theirs)


def _adamw(w, g, m, v):
    m = ADAM_B1 * m + (1.0 - ADAM_B1) * g
    v = ADAM_B2 * v + (1.0 - ADAM_B2) * (g * g)
    m_hat = m / (1.0 - ADAM_B1 ** ADAM_STEP)
    v_hat = v / (1.0 - ADAM_B2 ** ADAM_STEP)
    return -ADAM_LR * (m_hat / (jnp.sqrt(v_hat) + ADAM_EPS) + ADAM_WD * w), m, v


def _sum_devices(ref, rows):
    total = ref[0, rows, :].astype(F32)
    for s in range(1, ref.shape[0]):
        total = total + ref[s, rows, :].astype(F32)
    return total


def _adam_big(recv, w, m, v, *, tr, name):
    r, c = w.shape

    def body(recv_ref, w_ref, m_ref, v_ref, g_ref, d_ref, m2_ref, v2_ref):
        g = _sum_devices(recv_ref, slice(None))
        g_ref[...] = g
        d_ref[...], m2_ref[...], v2_ref[...] = _adamw(w_ref[...], g, m_ref[...], v_ref[...])

    blk = pl.BlockSpec((tr, c), lambda i: (i, 0))
    return pl.pallas_call(
        body, name=name, grid=(r // tr,),
        in_specs=[pl.BlockSpec((recv.shape[0], tr, c), lambda i: (0, i, 0)), blk, blk, blk],
        out_specs=[blk] * 4, out_shape=[S((r, c), F32)] * 4,
        compiler_params=_params(("arbitrary",)),
    )(recv, w, m, v)


_REPLICATED = {"even_pre_g": (0, 0, 1), "even_mem_g": (0, 8, 1), "even_post_g": (0, 16, 1),
               "even_a_ln_g": (1, 0, 1), "even_a_ln_b": (1, 8, 1),
               "even_a_ws": (2, 0, NH * CHUNK), "even_a_bs": (2, NH * CHUNK, NH),
               "odd_c_wgrp": (3, 0, NH * GRP)}
_SHARDED = {"odd_pre_g": (4, 0, 1), "odd_mem_g": (4, 8, 1), "odd_post_g": (4, 16, 1),
            "even_b_conv": (5, 0, 3), "odd_c_scale": (5, 8, 1), "odd_d_dw_w": (5, 16, CONF_K),
            "odd_d_dw_b": (5, 48, 1), "odd_d_ln_g": (5, 56, 1), "odd_d_ln_b": (5, 64, 1), "odd_d_pw_b": (5, 72, 1)}
_SMALL = {**_REPLICATED, **_SHARDED}
_SMALL_ROWS = {0: 24, 1: 16, 2: NH * CHUNK + 8, 3: NH * GRP, 4: 24, 5: 80}


def _adam_small(sources, wmv):
    names = list(_SMALL)
    ns = len(sources)

    def body(*refs):
        src = refs[:ns]
        ins = refs[ns:ns + 3 * len(names)]
        outs = refs[ns + 3 * len(names):]
        outs[-1][...] = _sum_devices(src[-1], slice(0, 1))
        for i, nm in enumerate(names):
            a, row0, rows = _SMALL[nm]
            g = _sum_devices(src[a], slice(row0, row0 + rows))
            w_ref, m_ref, v_ref = ins[3 * i:3 * i + 3]
            g_ref, d_ref, m2_ref, v2_ref = outs[4 * i:4 * i + 4]
            g_ref[...] = g
            d_ref[...], m2_ref[...], v2_ref[...] = _adamw(w_ref[...], g, m_ref[...], v_ref[...])

    flat = [t for nm in names for t in wmv[nm]]
    out_shape = [S(wmv[nm][0].shape, F32) for nm in names for _ in range(4)] + [S((1, HD), F32)]
    res = pl.pallas_call(
        body, name="adam_small", in_specs=[_whole()] * (ns + len(flat)), out_specs=[_whole()] * len(out_shape),
        out_shape=out_shape, compiler_params=_params(),
    )(*sources, *flat)
    return {nm: tuple(res[4 * i:4 * i + 4]) for i, nm in enumerate(names)}, res[-1]


_WEIGHTS = ["even_pre_g", "even_w_in", "even_a_ln_g", "even_a_ln_b", "even_a_ws", "even_a_bs", "even_b_conv",
            "even_mem_g", "even_w_kv", "even_w_out", "even_post_g", "odd_pre_g", "odd_w_in", "odd_c_wgrp",
            "odd_c_scale", "odd_d_dw_w", "odd_d_dw_b", "odd_d_ln_g", "odd_d_ln_b", "odd_d_pw_w", "odd_d_pw_b",
            "odd_mem_g", "odd_w_kv", "odd_w_out", "odd_post_g"]
_TRANSPOSED = ["even_w_in", "odd_w_in"]
_BIG = _TRANSPOSED + ["even_w_kv", "even_w_out", "odd_w_kv", "odd_w_out", "odd_d_pw_w"]
_BIG_TILE_ROWS = {"even_w_in": 400, "odd_w_in": 304, "even_w_kv": 128, "even_w_out": 128, "odd_w_kv": 128,
                  "odd_w_out": 128, "odd_d_pw_w": 96}


def _view2d(a, transposed):
    a = a[0]
    if a.ndim == 1:
        return a[None]
    if transposed:
        return a.T
    return a.reshape(-1, a.shape[-1])


def _rows8(a):
    return _pad_rows(a, -(-a.shape[0] // 8) * 8)


def _pack_rows(parts):
    return jnp.concatenate([_rows8(p) for p in parts], axis=0)


def _unshard_cols(a):
    return jnp.transpose(a, (1, 0, 2)).reshape(a.shape[1], N_DEV * a.shape[2])


def _shard_cols(a):
    return jnp.transpose(a.reshape(a.shape[0], N_DEV, a.shape[1] // N_DEV), (1, 0, 2))


def _rows_of(a):
    return a.reshape(-1, a.shape[-1])


_GROUPS = {"odd_rest": (["odd_w_out", "odd_w_kv", "odd_d_pw_w"], [3], []),
           "odd_in": (["odd_w_in"], [], []),
           "even_rest": (["even_w_out", "even_w_kv"], [1, 2], []),
           "even_in": (["even_w_in"], [], [4, 5]),
           "even_gains": ([], [0], [])}


_TWO_LEVEL = ("even_in",)


class _MeshExchange:
    def __init__(self, shard):
        self.shard = shard
        self.handles = {}

    def first(self):
        shard = self.shard
        packs = [_pack_rows([shard[nm] for nm in _SHARDED if _SHARDED[nm][0] == a]) for a in (4, 5)]
        w_in, p128, p96 = _all_gather([shard["even_w_in"].astype(BF16)] + packs, "gather_first")
        w = {nm: shard[nm] for nm in _REPLICATED}
        w["even_a_ws"] = w["even_a_ws"].reshape(NH, CHUNK, CHUNK)
        w["odd_c_wgrp"] = w["odd_c_wgrp"].reshape(NH, GRP, GRP)
        w["even_w_in"] = _rows_of(w_in)
        full_packs = {4: _unshard_cols(p128), 5: _unshard_cols(p96)}
        for nm, (a, row0, rows) in _SHARDED.items():
            w[nm] = full_packs[a][row0:row0 + rows]
        later = lambda names: [("gather_chips", shard[nm].astype(BF16)) for nm in names]
        self.handles["w_even"], token = _exchange_start(later(["even_w_kv", "even_w_out"]), "gather_even_start",
                                                        deps=(w_in,))
        self.handles["w_odd"], token = _exchange_start(later(["odd_w_in", "odd_w_kv", "odd_w_out", "odd_d_pw_w"]),
                                                       "gather_odd_start", deps=(token,))
        return w, (token,)

    def even_rest(self, after):
        landed = _exchange_wait(self.handles.pop("w_even"), (after,), "gather_even_wait")
        kv, out = _sibling_forward(landed, "forward_even")
        return {"even_w_kv": _rows_of(kv), "even_w_out": _rows_of(out)}

    def odd(self, after):
        landed = _exchange_wait(self.handles.pop("w_odd"), (after,), "gather_odd_wait")
        w_in, kv, out, pw = _sibling_forward(landed, "forward_odd")
        return {"odd_w_in": _rows_of(w_in), "odd_w_kv": _rows_of(kv), "odd_w_out": _rows_of(out),
                "odd_d_pw_w": _rows_of(pw)}

    def send(self, group, g):
        big, replicated, sharded = _GROUPS[group]
        by_owner = [g[nm].reshape(N_DEV, -1, g[nm].shape[-1]) for nm in big]
        if group in _TWO_LEVEL:
            theirs = _sibling_swap(by_owner, "swap_" + group)
            items = [("scatter_chips", _add_partials(a, b, tr=_BIG_TILE_ROWS[nm], name="chip_sum_" + nm))
                     for nm, a, b in zip(big, by_owner, theirs)]
        else:
            items = [("scatter", a) for a in by_owner]
        items += [("gather", _pack_rows([g[nm] for nm in _REPLICATED if _REPLICATED[nm][0] == a]))
                  for a in replicated]
        items += [("scatter", _shard_cols(_pack_rows([g[nm] for nm in _SHARDED if _SHARDED[nm][0] == a])))
                  for a in sharded]
        if group == "odd_rest":
            items.append(("gather", _rows8(g["loss"])))
        self.handles[group], token = _exchange_start(items, "send_" + group + "_start")
        return (token,)

    def receive(self, group, after):
        return _exchange_wait(self.handles.pop(group), (after,), "send_" + group + "_wait")


def kernel(x, mem, even_pre_g, even_w_in, even_a_ln_g, even_a_ln_b, even_a_ws, even_a_bs, even_b_conv, even_mem_g, even_w_kv, even_w_out, even_post_g, odd_pre_g, odd_w_in, odd_c_wgrp, odd_c_scale, odd_d_dw_w, odd_d_dw_b, odd_d_ln_g, odd_d_ln_b, odd_d_pw_w, odd_d_pw_b, odd_mem_g, odd_w_kv, odd_w_out, odd_post_g, loss_target, m_even_pre_g, m_even_w_in, m_even_a_ln_g, m_even_a_ln_b, m_even_a_ws, m_even_a_bs, m_even_b_conv, m_even_mem_g, m_even_w_kv, m_even_w_out, m_even_post_g, m_odd_pre_g, m_odd_w_in, m_odd_c_wgrp, m_odd_c_scale, m_odd_d_dw_w, m_odd_d_dw_b, m_odd_d_ln_g, m_odd_d_ln_b, m_odd_d_pw_w, m_odd_d_pw_b, m_odd_mem_g, m_odd_w_kv, m_odd_w_out, m_odd_post_g, v_even_pre_g, v_even_w_in, v_even_a_ln_g, v_even_a_ln_b, v_even_a_ws, v_even_a_bs, v_even_b_conv, v_even_mem_g, v_even_w_kv, v_even_w_out, v_even_post_g, v_odd_pre_g, v_odd_w_in, v_odd_c_wgrp, v_odd_c_scale, v_odd_d_dw_w, v_odd_d_dw_b, v_odd_d_ln_g, v_odd_d_ln_b, v_odd_d_pw_w, v_odd_d_pw_b, v_odd_mem_g, v_odd_w_kv, v_odd_w_out, v_odd_post_g):
    given = dict(locals())
    view = lambda nm, kind: _view2d(given[kind + nm], nm in _TRANSPOSED)
    shard = {nm: view(nm, "") for nm in _WEIGHTS}
    wmv = {nm: (shard[nm], view(nm, "m_"), view(nm, "v_")) for nm in _WEIGHTS}

    ex = _MeshExchange(shard)
    grad_x, last = _step(x[0], mem[0], loss_target[0], ex)

    res = {}

    def update(group, after):
        names = _GROUPS[group][0]
        landed = ex.receive(group, after)
        for nm, recv in zip(names, landed):
            res[nm] = _adam_big(recv, *wmv[nm], tr=_BIG_TILE_ROWS[nm], name="adam_" + nm)
        return landed[len(names):]

    c192, losses = update("odd_rest", last)
    update("odd_in", res["odd_d_pw_w"][0])
    c768, c128 = update("even_rest", res["odd_w_in"][0])
    a128, a96 = update("even_in", res["even_w_kv"][0])
    (c1024,) = update("even_gains", res["even_w_in"][0])
    small, loss = _adam_small([c1024, c768, c128, c192, a128, a96, losses], {nm: wmv[nm] for nm in _SMALL})
    res.update(small)
    total = loss[0, 0]
    back = lambda nm, a: (a.T if nm in _TRANSPOSED else a).reshape(given[nm].shape)
    outs = [[back(nm, res[nm][i]) for nm in _WEIGHTS] for i in range(4)]
    return (total, grad_x[None], *outs[0], *outs[1], *outs[2], *outs[3])
```

```python
import functools

import jax
import jax.numpy as jnp
from jax import lax
from jax.experimental import pallas as pl
from jax.experimental.pallas import tpu as pltpu

F32 = jnp.float32
BF16 = jnp.bfloat16
S = jax.ShapeDtypeStruct
MESH = pl.DeviceIdType.MESH
AXES = ("x", "y", "c")
N_DEV = 8

D = 1024
BW = 768
XA = 512
HD = 128
NH = 4
MIX = 2048
CHUNK = 128
GRP = 192
N_MEM = 256
CONF_K = 31
EPS = 1e-6
HALO = 32
POOL_WINDOWS = (2, 4, 8, 16)
TM_FWD = 256
TM_BWD = 128
RB = 16

E_U, E_V, E_BG, E_CG, E_XIN, E_Q, E_GATE = 0, 768, 1536, 2304, 3072, 3840, 4352
EVEN_IN = 6400
O_ZC, O_GA, O_GB, O_Q, O_GATE = 0, 768, 1536, 2304, 2816
ODD_IN = 4864

ADAM_LR, ADAM_B1, ADAM_B2, ADAM_EPS, ADAM_WD, ADAM_STEP = 0.001, 0.9, 0.999, 1e-08, 0.01, 10

VMEM_LIMIT_V7X = 56 * 1024 * 1024


def _params(sem=None):
    return pltpu.CompilerParams(dimension_semantics=sem, vmem_limit_bytes=VMEM_LIMIT_V7X)


def _dot(a, b):
    return jnp.dot(a, b, preferred_element_type=F32)


def _dot_nt(a, b):
    return lax.dot_general(a, b, (((1,), (1,)), ((), ())), preferred_element_type=F32)


def _dot_tn(a, b):
    return lax.dot_general(a, b, (((0,), (0,)), ((), ())), preferred_element_type=F32)


def _sigmoid(z):
    return 1.0 / (1.0 + jnp.exp(-z))


def _rowmean(a):
    return jnp.mean(a, axis=-1, keepdims=True)


def _colsum(a):
    return jnp.sum(a, axis=0, keepdims=True)


def _ln_stats(v):
    mu = _rowmean(v)
    vc = v - mu
    rs = lax.rsqrt(_rowmean(vc * vc) + EPS)
    return vc * rs, rs


def _ln_bwd(dn, vh, rs, g):
    dvh = dn * g
    return rs * (dvh - _rowmean(dvh) - vh * _rowmean(dvh * vh))


def _group_masks():
    col = lax.broadcasted_iota(jnp.int32, (1, BW), 1)
    return [((col >= GRP * h) & (col < GRP * (h + 1))).astype(F32) for h in range(NH)]


def _full(shape):
    nd = len(shape)
    return pl.BlockSpec(shape, lambda *_: (0,) * nd)


def _whole():
    return pl.BlockSpec(memory_space=pltpu.VMEM)


_ANY = pl.BlockSpec(memory_space=pl.ANY)


def _after(body, n_in, deps):
    def ordered(*refs):
        return body(*refs[:n_in], *refs[n_in + len(deps):])
    return ordered


def _rms_matmul(x, g, w, *, tm, name, transposed=False, deps=()):
    t, d = x.shape
    n = w.shape[0] if transposed else w.shape[1]

    def body(x_ref, g_ref, w_ref, p_ref, h_ref):
        xv = x_ref[...]
        r = lax.rsqrt(_rowmean(xv * xv) + EPS)
        h = (xv * r * g_ref[...]).astype(BF16)
        h_ref[...] = h
        p_ref[...] = _dot_nt(h, w_ref[...]) if transposed else _dot(h, w_ref[...])

    return pl.pallas_call(
        _after(body, 3, deps), name=name, grid=(t // tm,),
        in_specs=[pl.BlockSpec((tm, d), lambda i: (i, 0)), _whole(), _whole()] + [_ANY] * len(deps),
        out_specs=[pl.BlockSpec((tm, n), lambda i: (i, 0)), pl.BlockSpec((tm, d), lambda i: (i, 0))],
        out_shape=[S((t, n), F32), S((t, d), BF16)],
        compiler_params=_params(("arbitrary",)),
    )(x, g, w, *deps)


def _nt_matmul_rms_bwd(dp, w, x, g, dres, *, tm, name, transposed=False, deps=()):
    t, n = dp.shape
    d = x.shape[1]

    def body(dp_ref, w_ref, x_ref, g_ref, dres_ref, dx_ref, dg_ref):
        @pl.when(pl.program_id(0) == 0)
        def _():
            dg_ref[...] = jnp.zeros_like(dg_ref)

        dh = _dot(dp_ref[...], w_ref[...]) if transposed else _dot_nt(dp_ref[...], w_ref[...])
        xv = x_ref[...]
        r = lax.rsqrt(_rowmean(xv * xv) + EPS)
        xh = xv * r
        dg_ref[...] += _colsum(dh * xh)
        dxh = dh * g_ref[...]
        dx_ref[...] = dres_ref[...] + r * (dxh - xh * _rowmean(dxh * xh))

    return pl.pallas_call(
        _after(body, 5, deps), name=name, grid=(t // tm,),
        in_specs=[pl.BlockSpec((tm, n), lambda i: (i, 0)), _whole(), pl.BlockSpec((tm, d), lambda i: (i, 0)),
                  _whole(), pl.BlockSpec((tm, d), lambda i: (i, 0))] + [_ANY] * len(deps),
        out_specs=[pl.BlockSpec((tm, d), lambda i: (i, 0)), pl.BlockSpec((1, d), lambda i: (0, 0))],
        out_shape=[S((t, d), F32), S((1, d), F32)],
        compiler_params=_params(("arbitrary",)),
    )(dp, w, x, g, dres, *deps)


def _tn_matmul(a, b, *, tmc, tk, out_dtype, name, deps=()):
    t, m = a.shape
    n = b.shape[1]
    nk = t // tk

    def body(a_ref, b_ref, o_ref, acc_ref):
        k = pl.program_id(1)

        @pl.when(k == 0)
        def _():
            acc_ref[...] = jnp.zeros_like(acc_ref)

        acc_ref[...] += _dot_tn(a_ref[...], b_ref[...])

        @pl.when(k == nk - 1)
        def _():
            o_ref[...] = acc_ref[...].astype(out_dtype)

    return pl.pallas_call(
        _after(body, 2, deps), name=name, grid=(m // tmc, nk),
        in_specs=[pl.BlockSpec((tk, tmc), lambda j, k: (k, j)), pl.BlockSpec((tk, n), lambda j, k: (k, 0))]
        + [_ANY] * len(deps),
        out_specs=pl.BlockSpec((tmc, n), lambda j, k: (j, 0)),
        out_shape=S((m, n), out_dtype),
        scratch_shapes=[pltpu.VMEM((tmc, n), F32)],
        compiler_params=_params(("arbitrary", "arbitrary")),
    )(a, b, *deps)


def _silu_parts(gt):
    sg = _sigmoid(gt)
    return gt * sg, sg * (1.0 + gt * (1.0 - sg))


def _attn_head(q_b, k_b, v_b):
    s = _dot_nt(q_b, k_b) * (HD ** -0.5)
    e = jnp.exp(s - jnp.max(s, axis=-1, keepdims=True))
    prob = e / jnp.sum(e, axis=-1, keepdims=True)
    return prob, _dot(prob.astype(BF16), v_b)


def _rms_residual(x, o, g):
    r = lax.rsqrt(_rowmean(o * o) + EPS)
    return x + o * r * g


def _rms_post_bwd(dres, o, g):
    r = lax.rsqrt(_rowmean(o * o) + EPS)
    oh = o * r
    doh = dres * g
    return r * (doh - oh * _rowmean(doh * oh)), _colsum(dres * oh)


def _sgu_chunk(vn_b, ws_ref, bmap_ref, masks):
    sg = bmap_ref[...]
    for h in range(NH):
        sg = sg + masks[h] * _dot(ws_ref[h], vn_b)
    return sg


def _shift_copies(buf, sh):
    n = buf.shape[0] - 8
    for b in range(1, 8):
        sh[b - 1, pl.ds(0, n), :] = buf[pl.ds(b, n), :]


def _loop_rows(rows, step, fn, carry=0, unrolled=True):
    if unrolled:
        for r0 in range(0, rows, step):
            carry = fn(r0, carry)
        return carry

    def body(j, c):
        return fn(pl.multiple_of(j * step, step), c)
    return lax.fori_loop(0, rows // step, body, carry)


def _rows_at(buf, sh, r0, off):
    b = off % 8
    if b == 0 or sh is None:
        return buf[pl.ds(r0 + off, 32), :]
    return sh[b - 1, pl.ds(r0 + (off - b), 32), :]


def _tap_sum(buf, sh, w_ref, r0, taps, causal):
    acc = None
    for k in range(taps):
        off = HALO - (taps - 1 - k) if causal else taps - 1 - k
        term = w_ref[k:k + 1, :] * _rows_at(buf, sh, r0, off)
        acc = term if acc is None else acc + term
    return acc


def _fold8(a):
    return a[0:8] + a[8:16] + a[16:24] + a[24:32]


def _tap_grads(dv, buf, sh, acc_ref, r0, taps):
    for k in range(taps):
        acc_ref[k * 8:(k + 1) * 8, :] += _fold8(dv * _rows_at(buf, sh, r0, HALO - (taps - 1 - k)))


def _halo_spec(n, nt, reverse, tm):
    per = tm // HALO
    if reverse:
        return pl.BlockSpec((HALO, n), lambda i: (jnp.maximum((nt - 1 - i) * per - 1, 0), 0))
    return pl.BlockSpec((HALO, n), lambda i: (jnp.maximum(i * per - 1, 0), 0))


def _even_fwd(x, p, lng, lnb, ws, bmap, wc, kv, wout, pg):
    t = x.shape[0]
    tm = min(TM_FWD, t)
    nt = t // tm

    def body(x_ref, p_ref, ph_ref, lng_ref, lnb_ref, ws_ref, bmap_ref, wc_ref, kv_ref, wout_ref, pg_ref,
             o_ref, x1_ref, ybuf, cbuf):
        i = pl.program_id(0)
        masks = _group_masks()
        vh, _ = _ln_stats(p_ref[:, E_V:E_V + BW])
        vn = vh * lng_ref[...] + lnb_ref[...]
        for c in range(tm // CHUNK):
            sl = slice(c * CHUNK, (c + 1) * CHUNK)
            sg = _sgu_chunk(vn[sl].astype(BF16), ws_ref, bmap_ref, masks)
            gate, _ = _silu_parts(p_ref[sl, E_GATE:E_GATE + BW])
            ybuf[sl, 0:BW] = (p_ref[sl, E_U:E_U + BW] * sg * gate).astype(BF16)

        cbuf[0:HALO] = jnp.where(i > 0, ph_ref[:, E_CG:E_CG + BW] * ph_ref[:, E_XIN:E_XIN + BW], 0.0)
        cbuf[HALO:HALO + tm] = p_ref[:, E_CG:E_CG + BW] * p_ref[:, E_XIN:E_XIN + BW]
        for r0 in range(0, tm, 32):
            sl = slice(r0, r0 + 32)
            cv = _tap_sum(cbuf, None, wc_ref, r0, 3, True)
            gate, _ = _silu_parts(p_ref[sl, E_GATE + BW:E_GATE + 2 * BW])
            ybuf[sl, BW:2 * BW] = (p_ref[sl, E_BG:E_BG + BW] * cv * gate).astype(BF16)

        for h in range(NH):
            qs = slice(E_Q + h * HD, E_Q + (h + 1) * HD)
            _, yx = _attn_head(p_ref[:, qs].astype(BF16), kv_ref[:, h * HD:(h + 1) * HD],
                               kv_ref[:, XA + h * HD:XA + (h + 1) * HD])
            gs = slice(E_GATE + 2 * BW + h * HD, E_GATE + 2 * BW + (h + 1) * HD)
            gate, _ = _silu_parts(p_ref[:, gs])
            ybuf[:, 2 * BW + h * HD:2 * BW + (h + 1) * HD] = (yx * gate).astype(BF16)

        o = _dot(ybuf[...], wout_ref[...])
        o_ref[...] = o
        x1_ref[...] = _rms_residual(x_ref[...], o, pg_ref[...])

    tile = lambda n: pl.BlockSpec((tm, n), lambda i: (i, 0))
    return pl.pallas_call(
        body, name="even_fwd", grid=(nt,),
        in_specs=[tile(D), tile(EVEN_IN), _halo_spec(EVEN_IN, nt, False, tm)] + [_whole()] * 8,
        out_specs=[tile(D), tile(D)],
        out_shape=[S((t, D), F32), S((t, D), F32)],
        scratch_shapes=[pltpu.VMEM((tm, MIX), BF16), pltpu.VMEM((tm + HALO, BW), F32)],
        compiler_params=_params(("arbitrary",)),
    )(x, p, p, lng, lnb, ws, bmap, wc, kv, wout, pg)


def _even_bwd(dres, o, p, lng, lnb, ws, wst, bmap, wc, kv, wout, pg):
    t = dres.shape[0]
    tm = min(TM_BWD, t)
    nt = t // tm

    def body(dres_ref, o_ref, p_ref, ph_ref, lng_ref, lnb_ref, ws_ref, wst_ref, bmap_ref, wc_ref, kv_ref, wout_ref,
             pg_ref, dp_ref, y_ref, do_ref, dpg_ref, dws_ref, dbs_ref, dlng_ref, dlnb_ref, dwc_ref, dkv_ref,
             dy, cbuf, gbuf, dconv, carry, dvn, dbmap, wacc):
        i = pl.program_id(0)
        ti = nt - 1 - i
        masks = _group_masks()

        @pl.when(i == 0)
        def _():
            for ref in (dpg_ref, dws_ref, dlng_ref, dlnb_ref, dkv_ref, dbmap, wacc):
                ref[...] = jnp.zeros_like(ref)

        do, dpg = _rms_post_bwd(dres_ref[...], o_ref[...], pg_ref[...])
        dpg_ref[...] += dpg
        do_b = do.astype(BF16)
        do_ref[...] = do_b
        dy[...] = _dot_nt(do_b, wout_ref[...])

        vh, rs = _ln_stats(p_ref[:, E_V:E_V + BW])
        vn = vh * lng_ref[...] + lnb_ref[...]
        for c in range(tm // CHUNK):
            sl = slice(c * CHUNK, (c + 1) * CHUNK)
            vn_b = vn[sl].astype(BF16)
            sg = _sgu_chunk(vn_b, ws_ref, bmap_ref, masks)
            u = p_ref[sl, E_U:E_U + BW]
            gate, dgate = _silu_parts(p_ref[sl, E_GATE:E_GATE + BW])
            dyc = dy[sl, 0:BW]
            ya = u * sg
            y_ref[sl, 0:BW] = (ya * gate).astype(BF16)
            dp_ref[sl, E_GATE:E_GATE + BW] = (dyc * ya * dgate).astype(BF16)
            dya = dyc * gate
            dp_ref[sl, E_U:E_U + BW] = (dya * sg).astype(BF16)
            dsg = dya * u
            dbmap[...] += dsg
            dsg_b = dsg.astype(BF16)
            acc = jnp.zeros((CHUNK, BW), F32)
            for h in range(NH):
                dws_ref[h] += _dot_nt((dsg * masks[h]).astype(BF16), vn_b)
                acc = acc + masks[h] * _dot(wst_ref[h], dsg_b)
            dvn[sl, :] = acc
        dn = dvn[...]
        dlng_ref[...] += _colsum(dn * vh)
        dlnb_ref[...] += _colsum(dn)
        dp_ref[:, E_V:E_V + BW] = _ln_bwd(dn, vh, rs, lng_ref[...]).astype(BF16)

        cbuf[0:HALO] = jnp.where(ti > 0, ph_ref[:, E_CG:E_CG + BW] * ph_ref[:, E_XIN:E_XIN + BW], 0.0)
        cbuf[HALO:HALO + tm] = p_ref[:, E_CG:E_CG + BW] * p_ref[:, E_XIN:E_XIN + BW]
        for r0 in range(0, tm, 32):
            sl = slice(r0, r0 + 32)
            cv = _tap_sum(cbuf, None, wc_ref, r0, 3, True)
            gate, dgate = _silu_parts(p_ref[sl, E_GATE + BW:E_GATE + 2 * BW])
            bg = p_ref[sl, E_BG:E_BG + BW]
            dyc = dy[sl, BW:2 * BW]
            yb = bg * cv
            y_ref[sl, BW:2 * BW] = (yb * gate).astype(BF16)
            dp_ref[sl, E_GATE + BW:E_GATE + 2 * BW] = (dyc * yb * dgate).astype(BF16)
            dyb = dyc * gate
            dp_ref[sl, E_BG:E_BG + BW] = (dyb * cv).astype(BF16)
            dconv[sl, :] = dyb * bg
        gbuf[0:tm] = dconv[...]
        gbuf[tm:tm + HALO] = jnp.where(i > 0, carry[...], 0.0)
        carry[...] = dconv[0:HALO]
        for r0 in range(0, tm, 32):
            sl = slice(r0, r0 + 32)
            _tap_grads(dconv[sl, :], cbuf, None, wacc, r0, 3)
            dc = _tap_sum(gbuf, None, wc_ref, r0, 3, False)
            dp_ref[sl, E_CG:E_CG + BW] = (dc * p_ref[sl, E_XIN:E_XIN + BW]).astype(BF16)
            dp_ref[sl, E_XIN:E_XIN + BW] = (dc * p_ref[sl, E_CG:E_CG + BW]).astype(BF16)

        for h in range(NH):
            qs = slice(E_Q + h * HD, E_Q + (h + 1) * HD)
            ks = slice(h * HD, (h + 1) * HD)
            vs = slice(XA + h * HD, XA + (h + 1) * HD)
            gs = slice(E_GATE + 2 * BW + h * HD, E_GATE + 2 * BW + (h + 1) * HD)
            ys = slice(2 * BW + h * HD, 2 * BW + (h + 1) * HD)
            q_b = p_ref[:, qs].astype(BF16)
            prob, yx = _attn_head(q_b, kv_ref[:, ks], kv_ref[:, vs])
            gate, dgate = _silu_parts(p_ref[:, gs])
            dyc = dy[:, ys]
            y_ref[:, ys] = (yx * gate).astype(BF16)
            dp_ref[:, gs] = (dyc * yx * dgate).astype(BF16)
            dyx_b = (dyc * gate).astype(BF16)
            dprob = _dot_nt(dyx_b, kv_ref[:, vs])
            dkv_ref[:, vs] += _dot_tn(prob.astype(BF16), dyx_b)
            ds_b = (prob * (dprob - jnp.sum(dprob * prob, axis=-1, keepdims=True)) * (HD ** -0.5)).astype(BF16)
            dp_ref[:, qs] = _dot(ds_b, kv_ref[:, ks]).astype(BF16)
            dkv_ref[:, ks] += _dot_tn(ds_b, q_b)

        @pl.when(i == nt - 1)
        def _():
            for h in range(NH):
                dbs_ref[:, h * HD:(h + 1) * HD] = jnp.broadcast_to(
                    jnp.sum(dbmap[...] * masks[h], axis=-1, keepdims=True), (CHUNK, HD))
            for k in range(3):
                dwc_ref[k:k + 1, :] = _colsum(wacc[k * 8:(k + 1) * 8, :])
            dwc_ref[3:8, :] = jnp.zeros((5, BW), F32)
            causal = (lax.broadcasted_iota(jnp.int32, (CHUNK, CHUNK), 0)
                      >= lax.broadcasted_iota(jnp.int32, (CHUNK, CHUNK), 1))
            for h in range(NH):
                dws_ref[h] = jnp.where(causal, dws_ref[h], 0.0)

    rtile = lambda n: pl.BlockSpec((tm, n), lambda i: (nt - 1 - i, 0))
    outs = [S((t, EVEN_IN), BF16), S((t, MIX), BF16), S((t, D), BF16), S((1, D), F32), S((NH, CHUNK, CHUNK), F32),
            S((CHUNK, NH * HD), F32), S((1, BW), F32), S((1, BW), F32), S((8, BW), F32), S((N_MEM, 2 * XA), F32)]
    return pl.pallas_call(
        body, name="even_bwd", grid=(nt,),
        in_specs=[rtile(D), rtile(D), rtile(EVEN_IN), _halo_spec(EVEN_IN, nt, True, tm)] + [_whole()] * 9,
        out_specs=[rtile(EVEN_IN), rtile(MIX), rtile(D)] + [_full(s.shape) for s in outs[3:]],
        out_shape=outs,
        scratch_shapes=[pltpu.VMEM((tm, MIX), F32), pltpu.VMEM((tm + HALO, BW), F32), pltpu.VMEM((tm + HALO, BW), F32),
                        pltpu.VMEM((tm, BW), F32), pltpu.VMEM((HALO, BW), F32), pltpu.VMEM((tm, BW), F32),
                        pltpu.VMEM((CHUNK, BW), F32), pltpu.VMEM((3 * 8, BW), F32)],
        compiler_params=_params(("arbitrary",)),
    )(dres, o, p, p, lng, lnb, ws, wst, bmap, wc, kv, wout, pg)


def _pool_causal_levels(za, zb, zc, zd, tm):
    n = tm + HALO
    zb[pl.ds(8, n - 8), :] = za[pl.ds(8, n - 8), :] + za[pl.ds(7, n - 8), :]
    zc[pl.ds(16, n - 16), :] = zb[pl.ds(16, n - 16), :] + zb[pl.ds(14, n - 16), :]
    zd[pl.ds(24, n - 24), :] = zc[pl.ds(24, n - 24), :] + zc[pl.ds(20, n - 24), :]


def _pool_causal(za, zb, zc, zd, tm):
    _pool_causal_levels(za, zb, zc, zd, tm)
    s16 = zd[pl.ds(HALO, tm), :] + zd[pl.ds(HALO - 8, tm), :]
    return zb[pl.ds(HALO, tm), :], zc[pl.ds(HALO, tm), :], zd[pl.ds(HALO, tm), :], s16


def _pool_anticausal_levels(ea, eb, ec, ed, tm):
    n = tm + HALO
    eb[pl.ds(0, n - 8), :] = ea[pl.ds(0, n - 8), :] + ea[pl.ds(1, n - 8), :]
    ec[pl.ds(0, n - 16), :] = eb[pl.ds(0, n - 16), :] + eb[pl.ds(2, n - 16), :]
    ed[pl.ds(0, n - 24), :] = ec[pl.ds(0, n - 24), :] + ec[pl.ds(4, n - 24), :]


def _pool_weights(t0, masks, rows):
    tf = (t0 + lax.broadcasted_iota(jnp.int32, (rows, 1), 0) + 1).astype(F32)
    inv = None
    for g, win in enumerate(POOL_WINDOWS):
        term = masks[g] * (1.0 / jnp.minimum(tf, float(win)))
        inv = term if inv is None else inv + term
    return inv


def _mix4(masks, parts):
    return masks[0] * parts[0] + masks[1] * parts[1] + masks[2] * parts[2] + masks[3] * parts[3]


def _odd_fwd(x1, tgt, p, wbd, cscale, dww, dwb, lng, lnb, wpw, pwb, kv, wout, pg):
    t = x1.shape[0]
    tm = min(TM_FWD, t)
    nt = t // tm

    def body(x_ref, tgt_ref, p_ref, ph_ref, wbd_ref, cs_ref, dww_ref, dwb_ref, lng_ref, lnb_ref, wpw_ref, pwb_ref,
             kv_ref, wout_ref, pg_ref, o_ref, dres_ref, loss_ref, conv_ref, ybuf, za, zb, zc, zd, gbuf, lacc, gsh):
        i = pl.program_id(0)
        masks = _group_masks()

        @pl.when(i == 0)
        def _():
            lacc[...] = jnp.zeros_like(lacc)

        z = p_ref[:, O_ZC:O_ZC + BW]
        za[0:HALO] = jnp.where(i > 0, ph_ref[:, O_ZC:O_ZC + BW], 0.0)
        za[HALO:HALO + tm] = z
        pooled = _mix4(masks, _pool_causal(za, zb, zc, zd, tm)) * _pool_weights(i * tm, masks, tm) - z
        gate, _ = _silu_parts(p_ref[:, O_GATE:O_GATE + BW])
        ybuf[:, 0:BW] = (_dot(pooled.astype(BF16), wbd_ref[...]) * cs_ref[...] * gate).astype(BF16)

        gbuf[0:HALO] = jnp.where(i > 0, ph_ref[:, O_GA:O_GA + BW] * _sigmoid(ph_ref[:, O_GB:O_GB + BW]), 0.0)
        gbuf[HALO:HALO + tm] = p_ref[:, O_GA:O_GA + BW] * _sigmoid(p_ref[:, O_GB:O_GB + BW])
        _shift_copies(gbuf, gsh)
        def conv_rows(r0, carry):
            conv_ref[pl.ds(r0, 32), :] = _tap_sum(gbuf, gsh, dww_ref, r0, CONF_K, True) + dwb_ref[...]
            return carry

        _loop_rows(tm, 32, conv_rows)
        zh, _ = _ln_stats(conv_ref[...])
        zn = zh * lng_ref[...] + lnb_ref[...]
        yd = _dot((zn * _sigmoid(zn)).astype(BF16), wpw_ref[...]) + pwb_ref[...]
        gate, _ = _silu_parts(p_ref[:, O_GATE + BW:O_GATE + 2 * BW])
        ybuf[:, BW:2 * BW] = (yd * gate).astype(BF16)

        for h in range(NH):
            qs = slice(O_Q + h * HD, O_Q + (h + 1) * HD)
            _, yx = _attn_head(p_ref[:, qs].astype(BF16), kv_ref[:, h * HD:(h + 1) * HD],
                               kv_ref[:, XA + h * HD:XA + (h + 1) * HD])
            gs = slice(O_GATE + 2 * BW + h * HD, O_GATE + 2 * BW + (h + 1) * HD)
            gate, _ = _silu_parts(p_ref[:, gs])
            ybuf[:, 2 * BW + h * HD:2 * BW + (h + 1) * HD] = (yx * gate).astype(BF16)

        o = _dot(ybuf[...], wout_ref[...])
        o_ref[...] = o
        err = _rms_residual(x_ref[...], o, pg_ref[...]) - tgt_ref[...]
        lacc[...] += _colsum(err * err)
        dres_ref[...] = err * (1.0 / D)

        @pl.when(i == nt - 1)
        def _():
            loss_ref[...] = jnp.full((1, HD), jnp.sum(lacc[...]) * (0.5 / D), F32)

    tile = lambda n: pl.BlockSpec((tm, n), lambda i: (i, 0))
    ext = pltpu.VMEM((tm + HALO, BW), F32)
    return pl.pallas_call(
        body, name="odd_fwd", grid=(nt,),
        in_specs=[tile(D), tile(D), tile(ODD_IN), _halo_spec(ODD_IN, nt, False, tm)] + [_whole()] * 11,
        out_specs=[tile(D), tile(D), _full((1, HD)), tile(BW)],
        out_shape=[S((t, D), F32), S((t, D), F32), S((1, HD), F32), S((t, BW), F32)],
        scratch_shapes=[pltpu.VMEM((tm, MIX), BF16), ext, ext, ext, ext, ext,
                        pltpu.VMEM((1, D), F32), pltpu.VMEM((7, tm + HALO, BW), F32)],
        compiler_params=_params(("arbitrary",)),
    )(x1, tgt, p, p, wbd, cscale, dww, dwb, lng, lnb, wpw, pwb, kv, wout, pg)


def _odd_bwd(dres, o, p, conv, wbd, cscale, dww, dwb, lng, lnb, wpw, pwb, kv, wout, pg):
    t = dres.shape[0]
    tm = min(TM_BWD, t)
    nt = t // tm

    def body(dres_ref, o_ref, p_ref, ph_ref, conv_ref, wbd_ref, cs_ref, dww_ref, dwb_ref, lng_ref, lnb_ref, wpw_ref,
             pwb_ref, kv_ref, wout_ref, pg_ref, dp_ref, y_ref, do_ref, dpg_ref, dwbd_ref, dcs_ref, ddww_ref, ddwb_ref,
             dlng_ref, dlnb_ref, dwpw_ref, dpwb_ref, dkv_ref,
             dy, za, zb, zc, zd, gbuf, hbuf, tmp, carry_e, carry_d, wacc, gsh, hsh, t1, invs, b1, b2):
        i = pl.program_id(0)
        ti = nt - 1 - i
        masks = _group_masks()

        @pl.when(i == 0)
        def _():
            for ref in (dpg_ref, dwbd_ref, dcs_ref, ddwb_ref, dlng_ref, dlnb_ref, dwpw_ref, dpwb_ref, dkv_ref, wacc):
                ref[...] = jnp.zeros_like(ref)

        def post_norm_rows(r0, acc):
            sl = pl.ds(r0, RB)
            ov, dv = o_ref[sl, :], dres_ref[sl, :]
            r = lax.rsqrt(_rowmean(ov * ov) + EPS)
            oh = ov * r
            doh = dv * pg_ref[...]
            do_ref[sl, :] = (r * (doh - oh * _rowmean(doh * oh))).astype(BF16)
            return acc + dv * oh

        dpg_ref[...] += _colsum(_loop_rows(tm, RB, post_norm_rows, jnp.zeros((RB, D), F32)))
        dy[...] = _dot_nt(do_ref[...], wout_ref[...])

        za[0:HALO] = jnp.where(ti > 0, ph_ref[:, O_ZC:O_ZC + BW], 0.0)
        za[HALO:HALO + tm] = p_ref[:, O_ZC:O_ZC + BW]
        _pool_causal_levels(za, zb, zc, zd, tm)

        def pooled_rows(r0, carry):
            sl = pl.ds(r0, RB)
            at = lambda ref, back=0: ref[pl.ds(HALO + r0 - back, RB), :]
            inv = _pool_weights(ti * tm + r0, masks, RB)
            invs[sl, :] = inv
            sums = (at(zb), at(zc), at(zd), at(zd) + at(zd, 8))
            b1[sl, :] = (_mix4(masks, sums) * inv - p_ref[sl, O_ZC:O_ZC + BW]).astype(BF16)
            return carry

        _loop_rows(tm, RB, pooled_rows)
        t1[...] = _dot(b1[...], wbd_ref[...])

        def pool_gate_rows(r0, acc):
            sl = pl.ds(r0, RB)
            pm = t1[sl, :]
            gate, dgate = _silu_parts(p_ref[sl, O_GATE:O_GATE + BW])
            dyc = dy[sl, 0:BW]
            yc = pm * cs_ref[...]
            y_ref[sl, 0:BW] = (yc * gate).astype(BF16)
            dp_ref[sl, O_GATE:O_GATE + BW] = (dyc * yc * dgate).astype(BF16)
            dyc = dyc * gate
            b2[sl, :] = (dyc * cs_ref[...]).astype(BF16)
            return acc + dyc * pm

        dcs_ref[...] += _colsum(_loop_rows(tm, RB, pool_gate_rows, jnp.zeros((RB, BW), F32)))
        dwbd_ref[...] += _dot_tn(b1[...], b2[...])
        t1[...] = _dot_nt(b2[...], wbd_ref[...])

        def weighted_rows(r0, carry):
            sl = pl.ds(r0, RB)
            za[sl, :] = t1[sl, :] * invs[sl, :]
            return carry

        _loop_rows(tm, RB, weighted_rows)
        za[tm:tm + HALO] = jnp.where(i > 0, carry_e[...], 0.0)
        carry_e[...] = za[0:HALO]
        _pool_anticausal_levels(za, zb, zc, zd, tm)

        def pool_back_rows(r0, carry):
            sl = pl.ds(r0, RB)
            ahead = lambda ref, fwd=0: ref[pl.ds(r0 + fwd, RB), :]
            sums = (ahead(zb), ahead(zc), ahead(zd), ahead(zd) + ahead(zd, 8))
            dp_ref[sl, O_ZC:O_ZC + BW] = (_mix4(masks, sums) - t1[sl, :]).astype(BF16)
            return carry

        _loop_rows(tm, RB, pool_back_rows)

        gbuf[0:HALO] = jnp.where(ti > 0, ph_ref[:, O_GA:O_GA + BW] * _sigmoid(ph_ref[:, O_GB:O_GB + BW]), 0.0)

        def glu_rows(r0, carry):
            sl = pl.ds(r0, RB)
            gbuf[pl.ds(HALO + r0, RB), :] = p_ref[sl, O_GA:O_GA + BW] * _sigmoid(p_ref[sl, O_GB:O_GB + BW])
            zh, _ = _ln_stats(conv_ref[sl, :])
            zn = zh * lng_ref[...] + lnb_ref[...]
            b1[sl, :] = (zn * _sigmoid(zn)).astype(BF16)
            return carry

        _loop_rows(tm, RB, glu_rows)
        _shift_copies(gbuf, gsh)
        t1[...] = _dot(b1[...], wpw_ref[...])

        def conf_gate_rows(r0, acc):
            sl = pl.ds(r0, RB)
            yd = t1[sl, :] + pwb_ref[...]
            gate, dgate = _silu_parts(p_ref[sl, O_GATE + BW:O_GATE + 2 * BW])
            dyc = dy[sl, BW:2 * BW]
            y_ref[sl, BW:2 * BW] = (yd * gate).astype(BF16)
            dp_ref[sl, O_GATE + BW:O_GATE + 2 * BW] = (dyc * yd * dgate).astype(BF16)
            dyd = dyc * gate
            b2[sl, :] = dyd.astype(BF16)
            return acc + dyd

        dpwb_ref[...] += _colsum(_loop_rows(tm, RB, conf_gate_rows, jnp.zeros((RB, BW), F32)))
        dwpw_ref[...] += _dot_tn(b1[...], b2[...])
        t1[...] = _dot_nt(b2[...], wpw_ref[...])

        def norm_back_rows(r0, accs):
            sl = pl.ds(r0, RB)
            zh, rs = _ln_stats(conv_ref[sl, :])
            _, dsilu = _silu_parts(zh * lng_ref[...] + lnb_ref[...])
            dzn = t1[sl, :] * dsilu
            dzd = _ln_bwd(dzn, zh, rs, lng_ref[...])
            tmp[sl, :] = dzd
            hbuf[sl, :] = dzd
            return accs[0] + dzn * zh, accs[1] + dzn, accs[2] + dzd

        zero = jnp.zeros((RB, BW), F32)
        acc_g, acc_b, acc_d = _loop_rows(tm, RB, norm_back_rows, (zero, zero, zero))
        dlng_ref[...] += _colsum(acc_g)
        dlnb_ref[...] += _colsum(acc_b)
        ddwb_ref[...] += _colsum(acc_d)
        hbuf[tm:tm + HALO] = jnp.where(i > 0, carry_d[...], 0.0)
        carry_d[...] = tmp[0:HALO]
        _shift_copies(hbuf, hsh)

        def conv_back_rows(r0, carry):
            sl = pl.ds(r0, 32)
            _tap_grads(tmp[sl, :], gbuf, gsh, wacc, r0, CONF_K)
            dzg = _tap_sum(hbuf, hsh, dww_ref, r0, CONF_K, False)
            sgb = _sigmoid(p_ref[sl, O_GB:O_GB + BW])
            dp_ref[sl, O_GA:O_GA + BW] = (dzg * sgb).astype(BF16)
            dp_ref[sl, O_GB:O_GB + BW] = (dzg * p_ref[sl, O_GA:O_GA + BW] * sgb * (1.0 - sgb)).astype(BF16)
            return carry

        _loop_rows(tm, 32, conv_back_rows, unrolled=False)

        for h in range(NH):
            qs = slice(O_Q + h * HD, O_Q + (h + 1) * HD)
            ks = slice(h * HD, (h + 1) * HD)
            vs = slice(XA + h * HD, XA + (h + 1) * HD)
            gs = slice(O_GATE + 2 * BW + h * HD, O_GATE + 2 * BW + (h + 1) * HD)
            ys = slice(2 * BW + h * HD, 2 * BW + (h + 1) * HD)
            q_b = p_ref[:, qs].astype(BF16)
            prob, yx = _attn_head(q_b, kv_ref[:, ks], kv_ref[:, vs])
            gate, dgate = _silu_parts(p_ref[:, gs])
            dyc = dy[:, ys]
            y_ref[:, ys] = (yx * gate).astype(BF16)
            dp_ref[:, gs] = (dyc * yx * dgate).astype(BF16)
            dyx_b = (dyc * gate).astype(BF16)
            dprob = _dot_nt(dyx_b, kv_ref[:, vs])
            dkv_ref[:, vs] += _dot_tn(prob.astype(BF16), dyx_b)
            ds_b = (prob * (dprob - jnp.sum(dprob * prob, axis=-1, keepdims=True)) * (HD ** -0.5)).astype(BF16)
            dp_ref[:, qs] = _dot(ds_b, kv_ref[:, ks]).astype(BF16)
            dkv_ref[:, ks] += _dot_tn(ds_b, q_b)

        @pl.when(i == nt - 1)
        def _():
            for k in range(CONF_K):
                ddww_ref[k:k + 1, :] = _colsum(wacc[k * 8:(k + 1) * 8, :])
            ddww_ref[CONF_K:CONF_K + 1, :] = jnp.zeros((1, BW), F32)

    rtile = lambda n: pl.BlockSpec((tm, n), lambda i: (nt - 1 - i, 0))
    outs = [S((t, ODD_IN), BF16), S((t, MIX), BF16), S((t, D), BF16), S((1, D), F32), S((BW, BW), F32),
            S((1, BW), F32), S((CONF_K + 1, BW), F32), S((1, BW), F32), S((1, BW), F32), S((1, BW), F32),
            S((BW, BW), F32), S((1, BW), F32), S((N_MEM, 2 * XA), F32)]
    ext = pltpu.VMEM((tm + HALO, BW), F32)
    return pl.pallas_call(
        body, name="odd_bwd", grid=(nt,),
        in_specs=[rtile(D), rtile(D), rtile(ODD_IN), _halo_spec(ODD_IN, nt, True, tm), rtile(BW)] + [_whole()] * 11,
        out_specs=[rtile(ODD_IN), rtile(MIX), rtile(D)] + [_full(s.shape) for s in outs[3:]],
        out_shape=outs,
        scratch_shapes=[pltpu.VMEM((tm, MIX), F32), ext, ext, ext, ext, ext, ext, pltpu.VMEM((tm, BW), F32),
                        pltpu.VMEM((HALO, BW), F32), pltpu.VMEM((HALO, BW), F32), pltpu.VMEM((CONF_K * 8, BW), F32),
                        pltpu.VMEM((7, tm + HALO, BW), F32), pltpu.VMEM((7, tm + HALO, BW), F32),
                        pltpu.VMEM((tm, BW), F32), pltpu.VMEM((tm, BW), F32), pltpu.VMEM((tm, BW), BF16),
                        pltpu.VMEM((tm, BW), BF16)],
        compiler_params=_params(("arbitrary",)),
    )(dres, o, p, p, conv, wbd, cscale, dww, dwb, lng, lnb, wpw, pwb, kv, wout, pg)


def _pick_rows(n):
    for rows in (640, 2432, 1024, 768):
        if n % rows == 0:
            return rows
    return n


def _pad_rows(a, rows):
    return jnp.pad(a, ((0, rows - a.shape[0]), (0, 0)))


def _step(x, mem, tgt, ex):
    t = x.shape[0]
    tm = min(256, t)
    w, deps = ex.first()
    causal = jnp.tril(jnp.ones((CHUNK, CHUNK), bool))
    ws = jnp.where(causal[None], w["even_a_ws"], 0.0).astype(BF16)
    wst = jnp.transpose(ws, (0, 2, 1))
    bmap = jnp.repeat(w["even_a_bs"].T, GRP, axis=1)
    wc = _pad_rows(w["even_b_conv"], 8)
    wbd = jax.scipy.linalg.block_diag(*[w["odd_c_wgrp"][g] for g in range(NH)]).astype(BF16)
    dww = _pad_rows(w["odd_d_dw_w"], CONF_K + 1)
    tk = min(1024, t)
    zeros = jnp.zeros_like(mem)

    p_e, h_e = _rms_matmul(x, w["even_pre_g"], w["even_w_in"], tm=tm, name="in_even", transposed=True, deps=deps)
    w.update(ex.even_rest(h_e))
    kv_e, memn_e = _rms_matmul(mem, w["even_mem_g"], w["even_w_kv"], tm=N_MEM, name="kv_even")
    kv_e = kv_e.astype(BF16)
    even_args = (w["even_a_ln_g"], w["even_a_ln_b"], ws)
    o_e, x1 = _even_fwd(x, p_e, *even_args, bmap, wc, kv_e, w["even_w_out"], w["even_post_g"])
    w.update(ex.odd(o_e))
    kv_o, memn_o = _rms_matmul(mem, w["odd_mem_g"], w["odd_w_kv"], tm=N_MEM, name="kv_odd")
    kv_o = kv_o.astype(BF16)
    p_o, h_o = _rms_matmul(x1, w["odd_pre_g"], w["odd_w_in"], tm=tm, name="in_odd", transposed=True)
    odd_args = (wbd, w["odd_c_scale"], dww, w["odd_d_dw_b"], w["odd_d_ln_g"], w["odd_d_ln_b"], w["odd_d_pw_w"],
                w["odd_d_pw_b"], kv_o, w["odd_w_out"], w["odd_post_g"])
    o_o, dres, loss, conv_o = _odd_fwd(x1, tgt, p_o, *odd_args)

    g = {}
    (dp_o, y_o, do_o, post_g_o, dwbd, g["odd_c_scale"], ddww, g["odd_d_dw_b"], g["odd_d_ln_g"], g["odd_d_ln_b"],
     dwpw, g["odd_d_pw_b"], dkv_o) = _odd_bwd(dres, o_o, p_o, conv_o, *odd_args)
    g["odd_post_g"] = post_g_o
    g["odd_d_dw_w"] = ddww[:CONF_K]
    dkv_o = dkv_o.astype(BF16)
    deps = ex.send("odd_rest", {
        "odd_w_out": _tn_matmul(y_o, do_o, tmc=MIX, tk=tk, out_dtype=BF16, name="dw_out_odd"),
        "odd_w_kv": _tn_matmul(memn_o, dkv_o, tmc=D, tk=N_MEM, out_dtype=BF16, name="dw_kv_odd"),
        "odd_d_pw_w": dwpw.astype(BF16), "loss": loss,
        "odd_c_wgrp": jnp.concatenate([dwbd[i * GRP:(i + 1) * GRP, i * GRP:(i + 1) * GRP] for i in range(NH)])})
    deps = ex.send("odd_in", {"odd_w_in": _tn_matmul(dp_o, h_o, tmc=_pick_rows(ODD_IN), tk=tk, out_dtype=BF16,
                                                     name="dw_in_odd", deps=deps)})
    dx1, g["odd_pre_g"] = _nt_matmul_rms_bwd(dp_o, w["odd_w_in"], x1, w["odd_pre_g"], dres, tm=min(256, t),
                                             name="dx_odd", transposed=True, deps=deps)
    _, g["odd_mem_g"] = _nt_matmul_rms_bwd(dkv_o, w["odd_w_kv"], mem, w["odd_mem_g"], zeros, tm=N_MEM,
                                           name="dmem_odd")

    (dp_e, y_e, do_e, post_g_e, dws, dbs, ln_g_e, ln_b_e, dwc, dkv_e) = _even_bwd(
        dx1, o_e, p_e, *even_args, wst, bmap, wc, kv_e, w["even_w_out"], w["even_post_g"])
    g["even_b_conv"] = dwc[:3]
    dkv_e = dkv_e.astype(BF16)
    deps = ex.send("even_rest", {
        "even_w_out": _tn_matmul(y_e, do_e, tmc=MIX, tk=tk, out_dtype=BF16, name="dw_out_even"),
        "even_w_kv": _tn_matmul(memn_e, dkv_e, tmc=D, tk=N_MEM, out_dtype=BF16, name="dw_kv_even"),
        "even_a_ln_g": ln_g_e, "even_a_ln_b": ln_b_e,
        "even_a_ws": dws.reshape(NH * CHUNK, CHUNK), "even_a_bs": dbs[:, ::HD].T})
    g["even_w_in"] = _tn_matmul(dp_e, h_e, tmc=_pick_rows(EVEN_IN), tk=tk, out_dtype=BF16, name="dw_in_even",
                                deps=deps)
    deps = ex.send("even_in", g)
    grad_x, pre_g_e = _nt_matmul_rms_bwd(dp_e, w["even_w_in"], x, w["even_pre_g"], dx1, tm=min(256, t),
                                         name="dx_even", transposed=True, deps=deps)
    _, mem_g_e = _nt_matmul_rms_bwd(dkv_e, w["even_w_kv"], mem, w["even_mem_g"], zeros, tm=N_MEM, name="dmem_even",
                                    deps=(grad_x,))
    ex.send("even_gains", {"even_pre_g": pre_g_e, "even_mem_g": mem_g_e, "even_post_g": post_g_e})
    return grad_x, mem_g_e


def _place():
    return lax.axis_index("x"), lax.axis_index("y"), lax.axis_index("c")


def _index(px, py, pc):
    return 4 * px + 2 * py + pc


_COPIES = N_DEV - 1


def _all_gather(arrs, name):
    n = len(arrs)

    def body(*refs):
        ins, outs = refs[:n], refs[n:2 * n]
        send_sems, recv_sems, local_sems = refs[2 * n:]
        x, y, c = _place()
        me, sibling = (x, y, c), (x, y, 1 - c)
        chips = [(1 - x, y), (x, 1 - y), (1 - x, 1 - y)]

        def copy(a, k, block, to, src=None):
            dst = outs[a].at[_index(*block)]
            return pltpu.make_async_remote_copy(
                src_ref=dst if src is None else src, dst_ref=dst, send_sem=send_sems.at[a * _COPIES + k],
                recv_sem=recv_sems.at[a * _COPIES + k], device_id=to, device_id_type=MESH)

        mine = [pltpu.make_async_copy(ins[a], outs[a].at[_index(*me)], local_sems.at[a]) for a in range(n)]
        first = []
        for a in range(n):
            mine[a].start()
            first.append(copy(a, 0, me, sibling, src=ins[a]))
            first += [copy(a, 1 + j, me, (*chip, c), src=ins[a]) for j, chip in enumerate(chips)]
        for cp in first:
            cp.start()
        passed = []
        for j, chip in enumerate(chips):
            for a in range(n):
                copy(a, 1 + j, (*chip, c), me).wait_recv()
                passed.append(copy(a, 4 + j, (*chip, c), sibling))
                passed[-1].start()
        for a in range(n):
            copy(a, 0, sibling, me).wait_recv()
            for j, chip in enumerate(chips):
                copy(a, 4 + j, (*chip, 1 - c), me).wait_recv()
        for cp in first + passed:
            cp.wait_send()
        for cp in mine:
            cp.wait()

    return pl.pallas_call(
        body, name=name, in_specs=[_ANY] * n, out_specs=[_ANY] * n,
        out_shape=[S((N_DEV,) + a.shape, a.dtype) for a in arrs],
        scratch_shapes=[pltpu.SemaphoreType.DMA((n * _COPIES,)), pltpu.SemaphoreType.DMA((n * _COPIES,)),
                        pltpu.SemaphoreType.DMA((n,))],
    )(*arrs)


_HBM = pl.BlockSpec(memory_space=pltpu.HBM)
_SEM = pl.BlockSpec(memory_space=pltpu.SEMAPHORE)
_EFFECT = pltpu.SideEffectType.DATAFLOW_SIDE_EFFECTING


_ALL_FLIPS = [(k >> 2 & 1, k >> 1 & 1, k & 1) for k in range(1, N_DEV)]
_CHIP_FLIPS = [(1, 0, 0), (0, 1, 0), (1, 1, 0)]
_FLIPS = {"gather": _ALL_FLIPS, "scatter": _ALL_FLIPS, "gather_chips": [(0, 0, 1)] + _CHIP_FLIPS,
          "scatter_chips": _CHIP_FLIPS}


def _landing_shape(kind, a):
    return (N_DEV,) + a.shape if kind.startswith("gather") else a.shape


def _exchange_copies(kinds, srcs, lands, send_sems, recv_sems, local_sems, arriving):
    x, y, c = _place()
    mine = _index(x, y, c)
    remote, local = [], []
    for a, kind in enumerate(kinds):
        by_chip = kind == "scatter_chips"
        here = 2 * x + y if by_chip else mine
        own = srcs[a] if kind.startswith("gather") else srcs[a].at[here]
        local.append(pltpu.make_async_copy(own, lands[a].at[here], local_sems.at[a]))
        for k, (fx, fy, fc) in enumerate(_FLIPS[kind]):
            peer = (1 - x if fx else x, 1 - y if fy else y, 1 - c if fc else c)
            there = 2 * peer[0] + peer[1] if by_chip else _index(*peer)
            remote.append(pltpu.make_async_remote_copy(
                src_ref=srcs[a] if kind.startswith("gather") else srcs[a].at[there],
                dst_ref=lands[a].at[there if arriving else here],
                send_sem=send_sems.at[a * _COPIES + k], recv_sem=recv_sems.at[a * _COPIES + k],
                device_id=peer, device_id_type=MESH))
    return remote, local


def _exchange_start(items, name, deps=()):
    kinds = [kind for kind, _ in items]
    srcs = [a for _, a in items]
    n = len(items)
    lands = [lax.empty(_landing_shape(kind, a), a.dtype) for kind, a in items]

    def body(*refs):
        send_sems, recv_sems, local_sems = refs[2 * n + len(deps):2 * n + len(deps) + 3]
        remote, local = _exchange_copies(kinds, refs[:n], refs[n:2 * n], send_sems, recv_sems, local_sems, False)
        for cp in local + remote:
            cp.start()
        refs[-1][...] = jnp.zeros_like(refs[-1])

    held = [pltpu.HBM(a.shape, a.dtype) for a in srcs + lands]
    res = pl.pallas_call(
        body, name=name,
        out_shape=(pltpu.SemaphoreType.DMA((n * _COPIES,)), pltpu.SemaphoreType.DMA((n * _COPIES,)),
                   pltpu.SemaphoreType.DMA((n,)), *held, S((8, 128), F32)),
        in_specs=[_HBM] * (2 * n) + [_ANY] * len(deps),
        out_specs=(_SEM, _SEM, _SEM, *[_HBM] * (2 * n), _whole()),
        input_output_aliases={i: 3 + i for i in range(2 * n)},
        compiler_params=pltpu.CompilerParams(has_side_effects=_EFFECT),
    )(*[pltpu.with_memory_space_constraint(a, pltpu.HBM) for a in srcs + lands], *deps)
    return (kinds, res[:3], res[3:3 + 2 * n]), res[-1]


def _exchange_wait(handle, after, name):
    kinds, sems, held = handle
    n = len(kinds)

    def body(*refs):
        send_sems, recv_sems, local_sems = refs[2 * n:2 * n + 3]
        remote, local = _exchange_copies(kinds, refs[:n], refs[n:2 * n], send_sems, recv_sems, local_sems, True)
        for cp in remote:
            cp.wait_send()
            cp.wait_recv()
        for cp in local:
            cp.wait()

    res = pl.pallas_call(
        body, name=name, out_shape=[pltpu.HBM(a.shape, a.dtype) for a in held],
        in_specs=[_HBM] * (2 * n) + [_SEM] * 3 + [_ANY] * len(after), out_specs=[_HBM] * (2 * n),
        input_output_aliases={i: i for i in range(2 * n)},
        compiler_params=pltpu.CompilerParams(has_side_effects=_EFFECT),
    )(*held, *sems, *after)
    return res[n:]


_CHIPS = [(0, 0), (0, 1), (1, 0), (1, 1)]
_N_CHIPS = len(_CHIPS)


def _sibling_forward(lands, name):
    n = len(lands)

    def body(*refs):
        ins, outs = refs[:n], refs[n:2 * n]
        send_sems, recv_sems = refs[2 * n:]
        x, y, c = _place()
        sent, arriving = [], []
        for a in range(n):
            for j, (fx, fy, _) in enumerate(_CHIP_FLIPS):
                chip = (1 - x if fx else x, 1 - y if fy else y)
                sems = dict(send_sem=send_sems.at[a * 3 + j], recv_sem=recv_sems.at[a * 3 + j],
                            device_id=(x, y, 1 - c), device_id_type=MESH)
                mine, theirs = _index(*chip, c), _index(*chip, 1 - c)
                sent.append(pltpu.make_async_remote_copy(src_ref=ins[a].at[mine], dst_ref=outs[a].at[mine], **sems))
                arriving.append(pltpu.make_async_remote_copy(src_ref=ins[a].at[theirs], dst_ref=outs[a].at[theirs],
                                                             **sems))
        for cp in sent:
            cp.start()
        for cp in sent:
            cp.wait_send()
        for cp in arriving:
            cp.wait_recv()

    return pl.pallas_call(
        body, name=name, in_specs=[_ANY] * n, out_specs=[_ANY] * n,
        out_shape=[S(a.shape, a.dtype) for a in lands], input_output_aliases={a: a for a in range(n)},
        scratch_shapes=[pltpu.SemaphoreType.DMA((3 * n,)), pltpu.SemaphoreType.DMA((3 * n,))],
    )(*lands)


def _sibling_swap(arrs, name):
    n = len(arrs)

    def body(*refs):
        ins, outs = refs[:n], refs[n:2 * n]
        send_sems, recv_sems = refs[2 * n:]
        x, y, c = _place()
        copies = []
        for a in range(n):
            for q, chip in enumerate(_CHIPS):
                copies.append(pltpu.make_async_remote_copy(
                    src_ref=ins[a].at[_index(*chip, 1 - c)], dst_ref=outs[a].at[q],
                    send_sem=send_sems.at[a * _N_CHIPS + q], recv_sem=recv_sems.at[a * _N_CHIPS + q],
                    device_id=(x, y, 1 - c), device_id_type=MESH))
        for cp in copies:
            cp.start()
        for cp in copies:
            cp.wait_send()
            cp.wait_recv()

    return pl.pallas_call(
        body, name=name, in_specs=[_ANY] * n, out_specs=[_ANY] * n,
        out_shape=[S((_N_CHIPS,) + a.shape[1:], a.dtype) for a in arrs],
        scratch_shapes=[pltpu.SemaphoreType.DMA((_N_CHIPS * n,)), pltpu.SemaphoreType.DMA((_N_CHIPS * n,))],
    )(*arrs)


def _add_partials(mine, theirs, *, tr, name):
    _, r, c = mine.shape

    def body(mine_ref, theirs_ref, out_ref):
        core = lax.axis_index("c")
        own = jnp.where(core == 0, mine_ref[0].astype(F32), mine_ref[1].astype(F32))
        out_ref[0] = (own + theirs_ref[0].astype(F32)).astype(out_ref.dtype)

    return pl.pallas_call(
        body, name=name, grid=(_N_CHIPS, r // tr),
        in_specs=[pl.BlockSpec((2, tr, c), lambda q, i: (q, i, 0)), pl.BlockSpec((1, tr, c), lambda q, i: (q, i, 0))],
        out_specs=pl.BlockSpec((1, tr, c), lambda q, i: (q, i, 0)),
        out_shape=S((_N_CHIPS, r, c), mine.dtype),
        compiler_params=_params(("arbitrary", "arbitrary")),
    )(mine, theirs)


def _adamw(w, g, m, v):
    m = ADAM_B1 * m + (1.0 - ADAM_B1) * g
    v = ADAM_B2 * v + (1.0 - ADAM_B2) * (g * g)
    m_hat = m / (1.0 - ADAM_B1 ** ADAM_STEP)
    v_hat = v / (1.0 - ADAM_B2 ** ADAM_STEP)
    return -ADAM_LR * (m_hat / (jnp.sqrt(v_hat) + ADAM_EPS) + ADAM_WD * w), m, v


def _sum_devices(ref, rows):
    total = ref[0, rows, :].astype(F32)
    for s in range(1, ref.shape[0]):
        total = total + ref[s, rows, :].astype(F32)
    return total


def _adam_big(recv, w, m, v, *, tr, name):
    r, c = w.shape

    def body(recv_ref, w_ref, m_ref, v_ref, g_ref, d_ref, m2_ref, v2_ref):
        g = _sum_devices(recv_ref, slice(None))
        g_ref[...] = g
        d_ref[...], m2_ref[...], v2_ref[...] = _adamw(w_ref[...], g, m_ref[...], v_ref[...])

    blk = pl.BlockSpec((tr, c), lambda i: (i, 0))
    return pl.pallas_call(
        body, name=name, grid=(r // tr,),
        in_specs=[pl.BlockSpec((recv.shape[0], tr, c), lambda i: (0, i, 0)), blk, blk, blk],
        out_specs=[blk] * 4, out_shape=[S((r, c), F32)] * 4,
        compiler_params=_params(("arbitrary",)),
    )(recv, w, m, v)


_REPLICATED = {"even_pre_g": (0, 0, 1), "even_mem_g": (0, 8, 1), "even_post_g": (0, 16, 1),
               "even_a_ln_g": (1, 0, 1), "even_a_ln_b": (1, 8, 1),
               "even_a_ws": (2, 0, NH * CHUNK), "even_a_bs": (2, NH * CHUNK, NH),
               "odd_c_wgrp": (3, 0, NH * GRP)}
_SHARDED = {"odd_pre_g": (4, 0, 1), "odd_mem_g": (4, 8, 1), "odd_post_g": (4, 16, 1),
            "even_b_conv": (5, 0, 3), "odd_c_scale": (5, 8, 1), "odd_d_dw_w": (5, 16, CONF_K),
            "odd_d_dw_b": (5, 48, 1), "odd_d_ln_g": (5, 56, 1), "odd_d_ln_b": (5, 64, 1), "odd_d_pw_b": (5, 72, 1)}
_SMALL = {**_REPLICATED, **_SHARDED}
_SMALL_ROWS = {0: 24, 1: 16, 2: NH * CHUNK + 8, 3: NH * GRP, 4: 24, 5: 80}


def _adam_small(sources, wmv):
    names = list(_SMALL)
    ns = len(sources)

    def body(*refs):
        src = refs[:ns]
        ins = refs[ns:ns + 3 * len(names)]
        outs = refs[ns + 3 * len(names):]
        outs[-1][...] = _sum_devices(src[-1], slice(0, 1))
        for i, nm in enumerate(names):
            a, row0, rows = _SMALL[nm]
            g = _sum_devices(src[a], slice(row0, row0 + rows))
            w_ref, m_ref, v_ref = ins[3 * i:3 * i + 3]
            g_ref, d_ref, m2_ref, v2_ref = outs[4 * i:4 * i + 4]
            g_ref[...] = g
            d_ref[...], m2_ref[...], v2_ref[...] = _adamw(w_ref[...], g, m_ref[...], v_ref[...])

    flat = [t for nm in names for t in wmv[nm]]
    out_shape = [S(wmv[nm][0].shape, F32) for nm in names for _ in range(4)] + [S((1, HD), F32)]
    res = pl.pallas_call(
        body, name="adam_small", in_specs=[_whole()] * (ns + len(flat)), out_specs=[_whole()] * len(out_shape),
        out_shape=out_shape, compiler_params=_params(),
    )(*sources, *flat)
    return {nm: tuple(res[4 * i:4 * i + 4]) for i, nm in enumerate(names)}, res[-1]


_WEIGHTS = ["even_pre_g", "even_w_in", "even_a_ln_g", "even_a_ln_b", "even_a_ws", "even_a_bs", "even_b_conv",
            "even_mem_g", "even_w_kv", "even_w_out", "even_post_g", "odd_pre_g", "odd_w_in", "odd_c_wgrp",
            "odd_c_scale", "odd_d_dw_w", "odd_d_dw_b", "odd_d_ln_g", "odd_d_ln_b", "odd_d_pw_w", "odd_d_pw_b",
            "odd_mem_g", "odd_w_kv", "odd_w_out", "odd_post_g"]
_TRANSPOSED = ["even_w_in", "odd_w_in"]
_BIG = _TRANSPOSED + ["even_w_kv", "even_w_out", "odd_w_kv", "odd_w_out", "odd_d_pw_w"]
_BIG_TILE_ROWS = {"even_w_in": 400, "odd_w_in": 304, "even_w_kv": 128, "even_w_out": 128, "odd_w_kv": 128,
                  "odd_w_out": 128, "odd_d_pw_w": 96}


def _view2d(a, transposed):
    a = a[0]
    if a.ndim == 1:
        return a[None]
    if transposed:
        return a.T
    return a.reshape(-1, a.shape[-1])


def _rows8(a):
    return _pad_rows(a, -(-a.shape[0] // 8) * 8)


def _pack_rows(parts):
    return jnp.concatenate([_rows8(p) for p in parts], axis=0)


def _unshard_cols(a):
    return jnp.transpose(a, (1, 0, 2)).reshape(a.shape[1], N_DEV * a.shape[2])


def _shard_cols(a):
    return jnp.transpose(a.reshape(a.shape[0], N_DEV, a.shape[1] // N_DEV), (1, 0, 2))


def _rows_of(a):
    return a.reshape(-1, a.shape[-1])


_GROUPS = {"odd_rest": (["odd_w_out", "odd_w_kv", "odd_d_pw_w"], [3], []),
           "odd_in": (["odd_w_in"], [], []),
           "even_rest": (["even_w_out", "even_w_kv"], [1, 2], []),
           "even_in": (["even_w_in"], [], [4, 5]),
           "even_gains": ([], [0], [])}


_TWO_LEVEL = ("even_in",)


class _MeshExchange:
    def __init__(self, shard):
        self.shard = shard
        self.handles = {}

    def first(self):
        shard = self.shard
        packs = [_pack_rows([shard[nm] for nm in _SHARDED if _SHARDED[nm][0] == a]) for a in (4, 5)]
        w_in, p128, p96 = _all_gather([shard["even_w_in"].astype(BF16)] + packs, "gather_first")
        w = {nm: shard[nm] for nm in _REPLICATED}
        w["even_a_ws"] = w["even_a_ws"].reshape(NH, CHUNK, CHUNK)
        w["odd_c_wgrp"] = w["odd_c_wgrp"].reshape(NH, GRP, GRP)
        w["even_w_in"] = _rows_of(w_in)
        full_packs = {4: _unshard_cols(p128), 5: _unshard_cols(p96)}
        for nm, (a, row0, rows) in _SHARDED.items():
            w[nm] = full_packs[a][row0:row0 + rows]
        later = lambda names: [("gather_chips", shard[nm].astype(BF16)) for nm in names]
        self.handles["w_even"], token = _exchange_start(later(["even_w_kv", "even_w_out"]), "gather_even_start",
                                                        deps=(w_in,))
        self.handles["w_odd"], token = _exchange_start(later(["odd_w_in", "odd_w_kv", "odd_w_out", "odd_d_pw_w"]),
                                                       "gather_odd_start", deps=(token,))
        return w, (token,)

    def even_rest(self, after):
        landed = _exchange_wait(self.handles.pop("w_even"), (after,), "gather_even_wait")
        kv, out = _sibling_forward(landed, "forward_even")
        return {"even_w_kv": _rows_of(kv), "even_w_out": _rows_of(out)}

    def odd(self, after):
        landed = _exchange_wait(self.handles.pop("w_odd"), (after,), "gather_odd_wait")
        w_in, kv, out, pw = _sibling_forward(landed, "forward_odd")
        return {"odd_w_in": _rows_of(w_in), "odd_w_kv": _rows_of(kv), "odd_w_out": _rows_of(out),
                "odd_d_pw_w": _rows_of(pw)}

    def send(self, group, g):
        big, replicated, sharded = _GROUPS[group]
        by_owner = [g[nm].reshape(N_DEV, -1, g[nm].shape[-1]) for nm in big]
        if group in _TWO_LEVEL:
            theirs = _sibling_swap(by_owner, "swap_" + group)
            items = [("scatter_chips", _add_partials(a, b, tr=_BIG_TILE_ROWS[nm], name="chip_sum_" + nm))
                     for nm, a, b in zip(big, by_owner, theirs)]
        else:
            items = [("scatter", a) for a in by_owner]
        items += [("gather", _pack_rows([g[nm] for nm in _REPLICATED if _REPLICATED[nm][0] == a]))
                  for a in replicated]
        items += [("scatter", _shard_cols(_pack_rows([g[nm] for nm in _SHARDED if _SHARDED[nm][0] == a])))
                  for a in sharded]
        if group == "odd_rest":
            items.append(("gather", _rows8(g["loss"])))
        self.handles[group], token = _exchange_start(items, "send_" + group + "_start")
        return (token,)

    def receive(self, group, after):
        return _exchange_wait(self.handles.pop(group), (after,), "send_" + group + "_wait")


def kernel(x, mem, even_pre_g, even_w_in, even_a_ln_g, even_a_ln_b, even_a_ws, even_a_bs, even_b_conv, even_mem_g, even_w_kv, even_w_out, even_post_g, odd_pre_g, odd_w_in, odd_c_wgrp, odd_c_scale, odd_d_dw_w, odd_d_dw_b, odd_d_ln_g, odd_d_ln_b, odd_d_pw_w, odd_d_pw_b, odd_mem_g, odd_w_kv, odd_w_out, odd_post_g, loss_target, m_even_pre_g, m_even_w_in, m_even_a_ln_g, m_even_a_ln_b, m_even_a_ws, m_even_a_bs, m_even_b_conv, m_even_mem_g, m_even_w_kv, m_even_w_out, m_even_post_g, m_odd_pre_g, m_odd_w_in, m_odd_c_wgrp, m_odd_c_scale, m_odd_d_dw_w, m_odd_d_dw_b, m_odd_d_ln_g, m_odd_d_ln_b, m_odd_d_pw_w, m_odd_d_pw_b, m_odd_mem_g, m_odd_w_kv, m_odd_w_out, m_odd_post_g, v_even_pre_g, v_even_w_in, v_even_a_ln_g, v_even_a_ln_b, v_even_a_ws, v_even_a_bs, v_even_b_conv, v_even_mem_g, v_even_w_kv, v_even_w_out, v_even_post_g, v_odd_pre_g, v_odd_w_in, v_odd_c_wgrp, v_odd_c_scale, v_odd_d_dw_w, v_odd_d_dw_b, v_odd_d_ln_g, v_odd_d_ln_b, v_odd_d_pw_w, v_odd_d_pw_b, v_odd_mem_g, v_odd_w_kv, v_odd_w_out, v_odd_post_g):
    given = dict(locals())
    view = lambda nm, kind: _view2d(given[kind + nm], nm in _TRANSPOSED)
    shard = {nm: view(nm, "") for nm in _WEIGHTS}
    wmv = {nm: (shard[nm], view(nm, "m_"), view(nm, "v_")) for nm in _WEIGHTS}

    ex = _MeshExchange(shard)
    grad_x, last = _step(x[0], mem[0], loss_target[0], ex)

    res = {}

    def update(group, after):
        names = _GROUPS[group][0]
        landed = ex.receive(group, after)
        for nm, recv in zip(names, landed):
            res[nm] = _adam_big(recv, *wmv[nm], tr=_BIG_TILE_ROWS[nm], name="adam_" + nm)
        return landed[len(names):]

    c192, losses = update("odd_rest", last)
    update("odd_in", res["odd_d_pw_w"][0])
    c768, c128 = update("even_rest", res["odd_w_in"][0])
    a128, a96 = update("even_in", res["even_w_kv"][0])
    (c1024,) = update("even_gains", res["even_w_in"][0])
    small, loss = _adam_small([c1024, c768, c128, c192, a128, a96, losses], {nm: wmv[nm] for nm in _SMALL})
    res.update(small)
    total = loss[0, 0]
    back = lambda nm, a: (a.T if nm in _TRANSPOSED else a).reshape(given[nm].shape)
    outs = [[back(nm, res[nm][i]) for nm in _WEIGHTS] for i in range(4)]
    return (total, grad_x[None], *outs[0], *outs[1], *outs[2], *outs[3])
```

```python
import functools

import jax
import jax.numpy as jnp
from jax import lax
from jax.experimental import pallas as pl
from jax.experimental.pallas import tpu as pltpu

F32 = jnp.float32
BF16 = jnp.bfloat16
S = jax.ShapeDtypeStruct
MESH = pl.DeviceIdType.MESH
AXES = ("x", "y", "c")
N_DEV = 8

D = 1024
BW = 768
XA = 512
HD = 128
NH = 4
MIX = 2048
CHUNK = 128
GRP = 192
N_MEM = 256
CONF_K = 31
EPS = 1e-6
HALO = 32
POOL_WINDOWS = (2, 4, 8, 16)
TM_FWD = 256
TM_BWD_EVEN = 256
TM_BWD_ODD = 128
RB = 16

E_U, E_V, E_BG, E_CG, E_XIN, E_Q, E_GATE = 0, 768, 1536, 2304, 3072, 3840, 4352
EVEN_IN = 6400
O_ZC, O_GA, O_GB, O_Q, O_GATE = 0, 768, 1536, 2304, 2816
ODD_IN = 4864

ADAM_LR, ADAM_B1, ADAM_B2, ADAM_EPS, ADAM_WD, ADAM_STEP = 0.001, 0.9, 0.999, 1e-08, 0.01, 10

VMEM_LIMIT_V7X = 56 * 1024 * 1024


def _params(sem=None):
    return pltpu.CompilerParams(dimension_semantics=sem, vmem_limit_bytes=VMEM_LIMIT_V7X)


def _dot(a, b):
    return jnp.dot(a, b, preferred_element_type=F32)


def _dot_nt(a, b):
    return lax.dot_general(a, b, (((1,), (1,)), ((), ())), preferred_element_type=F32)


def _dot_tn(a, b):
    return lax.dot_general(a, b, (((0,), (0,)), ((), ())), preferred_element_type=F32)


def _sigmoid(z):
    return 1.0 / (1.0 + jnp.exp(-z))


def _rowmean(a):
    return jnp.mean(a, axis=-1, keepdims=True)


def _colsum(a):
    return jnp.sum(a, axis=0, keepdims=True)


def _ln_stats(v):
    mu = _rowmean(v)
    vc = v - mu
    rs = lax.rsqrt(_rowmean(vc * vc) + EPS)
    return vc * rs, rs


def _ln_bwd(dn, vh, rs, g):
    dvh = dn * g
    return rs * (dvh - _rowmean(dvh) - vh * _rowmean(dvh * vh))


def _group_masks():
    col = lax.broadcasted_iota(jnp.int32, (1, BW), 1)
    return [((col >= GRP * h) & (col < GRP * (h + 1))).astype(F32) for h in range(NH)]


def _full(shape):
    nd = len(shape)
    return pl.BlockSpec(shape, lambda *_: (0,) * nd)


def _whole():
    return pl.BlockSpec(memory_space=pltpu.VMEM)


_ANY = pl.BlockSpec(memory_space=pl.ANY)


def _after(body, n_in, deps):
    def ordered(*refs):
        return body(*refs[:n_in], *refs[n_in + len(deps):])
    return ordered


def _rms_matmul(x, g, w, *, tm, name, transposed=False, deps=()):
    t, d = x.shape
    n = w.shape[0] if transposed else w.shape[1]

    def body(x_ref, g_ref, w_ref, p_ref, h_ref):
        xv = x_ref[...]
        r = lax.rsqrt(_rowmean(xv * xv) + EPS)
        h = (xv * r * g_ref[...]).astype(BF16)
        h_ref[...] = h
        p_ref[...] = _dot_nt(h, w_ref[...]) if transposed else _dot(h, w_ref[...])

    return pl.pallas_call(
        _after(body, 3, deps), name=name, grid=(t // tm,),
        in_specs=[pl.BlockSpec((tm, d), lambda i: (i, 0)), _whole(), _whole()] + [_ANY] * len(deps),
        out_specs=[pl.BlockSpec((tm, n), lambda i: (i, 0)), pl.BlockSpec((tm, d), lambda i: (i, 0))],
        out_shape=[S((t, n), F32), S((t, d), BF16)],
        compiler_params=_params(("arbitrary",)),
    )(x, g, w, *deps)


def _nt_matmul_rms_bwd(dp, w, x, g, dres, *, tm, name, transposed=False, deps=()):
    t, n = dp.shape
    d = x.shape[1]

    def body(dp_ref, w_ref, x_ref, g_ref, dres_ref, dx_ref, dg_ref):
        @pl.when(pl.program_id(0) == 0)
        def _():
            dg_ref[...] = jnp.zeros_like(dg_ref)

        dh = _dot(dp_ref[...], w_ref[...]) if transposed else _dot_nt(dp_ref[...], w_ref[...])
        xv = x_ref[...]
        r = lax.rsqrt(_rowmean(xv * xv) + EPS)
        xh = xv * r
        dg_ref[...] += _colsum(dh * xh)
        dxh = dh * g_ref[...]
        dx_ref[...] = dres_ref[...] + r * (dxh - xh * _rowmean(dxh * xh))

    return pl.pallas_call(
        _after(body, 5, deps), name=name, grid=(t // tm,),
        in_specs=[pl.BlockSpec((tm, n), lambda i: (i, 0)), _whole(), pl.BlockSpec((tm, d), lambda i: (i, 0)),
                  _whole(), pl.BlockSpec((tm, d), lambda i: (i, 0))] + [_ANY] * len(deps),
        out_specs=[pl.BlockSpec((tm, d), lambda i: (i, 0)), pl.BlockSpec((1, d), lambda i: (0, 0))],
        out_shape=[S((t, d), F32), S((1, d), F32)],
        compiler_params=_params(("arbitrary",)),
    )(dp, w, x, g, dres, *deps)


def _tn_matmul(a, b, *, tmc, tk, out_dtype, name, deps=()):
    t, m = a.shape
    n = b.shape[1]
    nk = t // tk

    def body(a_ref, b_ref, o_ref, acc_ref):
        k = pl.program_id(1)

        @pl.when(k == 0)
        def _():
            acc_ref[...] = jnp.zeros_like(acc_ref)

        acc_ref[...] += _dot_tn(a_ref[...], b_ref[...])

        @pl.when(k == nk - 1)
        def _():
            o_ref[...] = acc_ref[...].astype(out_dtype)

    return pl.pallas_call(
        _after(body, 2, deps), name=name, grid=(m // tmc, nk),
        in_specs=[pl.BlockSpec((tk, tmc), lambda j, k: (k, j)), pl.BlockSpec((tk, n), lambda j, k: (k, 0))]
        + [_ANY] * len(deps),
        out_specs=pl.BlockSpec((tmc, n), lambda j, k: (j, 0)),
        out_shape=S((m, n), out_dtype),
        scratch_shapes=[pltpu.VMEM((tmc, n), F32)],
        compiler_params=_params(("arbitrary", "arbitrary")),
    )(a, b, *deps)


def _silu_parts(gt):
    sg = _sigmoid(gt)
    return gt * sg, sg * (1.0 + gt * (1.0 - sg))


def _attn_head(q_b, k_b, v_b):
    s = _dot_nt(q_b, k_b) * (HD ** -0.5)
    e = jnp.exp(s - jnp.max(s, axis=-1, keepdims=True))
    prob = e / jnp.sum(e, axis=-1, keepdims=True)
    return prob, _dot(prob.astype(BF16), v_b)


def _rms_residual(x, o, g):
    r = lax.rsqrt(_rowmean(o * o) + EPS)
    return x + o * r * g


def _rms_post_bwd(dres, o, g):
    r = lax.rsqrt(_rowmean(o * o) + EPS)
    oh = o * r
    doh = dres * g
    return r * (doh - oh * _rowmean(doh * oh)), _colsum(dres * oh)


LANE = 128
_TILE_GROUPS = [sorted({LANE * j // GRP, (LANE * j + LANE - 1) // GRP}) for j in range(BW // LANE)]


def _tile(j):
    return slice(LANE * j, LANE * (j + 1))


def _low_lanes():
    return lax.broadcasted_iota(jnp.int32, (1, LANE), 1) < GRP - LANE


def _by_group(fn):
    tiles = []
    for j, groups in enumerate(_TILE_GROUPS):
        if len(groups) == 1:
            tiles.append(fn(groups[0], j))
        else:
            tiles.append(jnp.where(_low_lanes(), fn(groups[0], j), fn(groups[1], j)))
    return jnp.concatenate(tiles, axis=1)


def _sgu_chunk(vn_b, ws_ref, bmap_ref):
    return bmap_ref[...] + _by_group(lambda h, j: _dot(ws_ref[h], vn_b[:, _tile(j)]))


def _shift_copies(buf, sh):
    n = buf.shape[0] - 8
    for b in range(1, 8):
        sh[b - 1, pl.ds(0, n), :] = buf[pl.ds(b, n), :]


def _loop_rows(rows, step, fn, carry=0, unrolled=True):
    if unrolled:
        for r0 in range(0, rows, step):
            carry = fn(r0, carry)
        return carry

    def body(j, c):
        return fn(pl.multiple_of(j * step, step), c)
    return lax.fori_loop(0, rows // step, body, carry)


def _rows_at(buf, sh, r0, off):
    b = off % 8
    if b == 0 or sh is None:
        return buf[pl.ds(r0 + off, 32), :]
    return sh[b - 1, pl.ds(r0 + (off - b), 32), :]


def _tap_sum(buf, sh, w_ref, r0, taps, causal):
    acc = None
    for k in range(taps):
        off = HALO - (taps - 1 - k) if causal else taps - 1 - k
        term = w_ref[k:k + 1, :] * _rows_at(buf, sh, r0, off)
        acc = term if acc is None else acc + term
    return acc


def _fold8(a):
    return a[0:8] + a[8:16] + a[16:24] + a[24:32]


def _tap_grads(dv, buf, sh, acc_ref, r0, taps):
    for k in range(taps):
        acc_ref[k * 8:(k + 1) * 8, :] += _fold8(dv * _rows_at(buf, sh, r0, HALO - (taps - 1 - k)))


def _halo_spec(n, nt, reverse, tm):
    per = tm // HALO
    if reverse:
        return pl.BlockSpec((HALO, n), lambda i: (jnp.maximum((nt - 1 - i) * per - 1, 0), 0))
    return pl.BlockSpec((HALO, n), lambda i: (jnp.maximum(i * per - 1, 0), 0))


def _even_fwd(x, p, lng, lnb, ws, bmap, wc, kv, wout, pg):
    t = x.shape[0]
    tm = min(TM_FWD, t)
    nt = t // tm

    def body(x_ref, p_ref, ph_ref, lng_ref, lnb_ref, ws_ref, bmap_ref, wc_ref, kv_ref, wout_ref, pg_ref,
             o_ref, x1_ref, ybuf, cbuf):
        i = pl.program_id(0)
        vh, _ = _ln_stats(p_ref[:, E_V:E_V + BW])
        vn = vh * lng_ref[...] + lnb_ref[...]
        for c in range(tm // CHUNK):
            sl = slice(c * CHUNK, (c + 1) * CHUNK)
            sg = _sgu_chunk(vn[sl].astype(BF16), ws_ref, bmap_ref)
            gate, _ = _silu_parts(p_ref[sl, E_GATE:E_GATE + BW])
            ybuf[sl, 0:BW] = (p_ref[sl, E_U:E_U + BW] * sg * gate).astype(BF16)

        cbuf[0:HALO] = jnp.where(i > 0, ph_ref[:, E_CG:E_CG + BW] * ph_ref[:, E_XIN:E_XIN + BW], 0.0)
        cbuf[HALO:HALO + tm] = p_ref[:, E_CG:E_CG + BW] * p_ref[:, E_XIN:E_XIN + BW]
        for r0 in range(0, tm, 32):
            sl = slice(r0, r0 + 32)
            cv = _tap_sum(cbuf, None, wc_ref, r0, 3, True)
            gate, _ = _silu_parts(p_ref[sl, E_GATE + BW:E_GATE + 2 * BW])
            ybuf[sl, BW:2 * BW] = (p_ref[sl, E_BG:E_BG + BW] * cv * gate).astype(BF16)

        for h in range(NH):
            qs = slice(E_Q + h * HD, E_Q + (h + 1) * HD)
            _, yx = _attn_head(p_ref[:, qs].astype(BF16), kv_ref[:, h * HD:(h + 1) * HD],
                               kv_ref[:, XA + h * HD:XA + (h + 1) * HD])
            gs = slice(E_GATE + 2 * BW + h * HD, E_GATE + 2 * BW + (h + 1) * HD)
            gate, _ = _silu_parts(p_ref[:, gs])
            ybuf[:, 2 * BW + h * HD:2 * BW + (h + 1) * HD] = (yx * gate).astype(BF16)

        o = _dot(ybuf[...], wout_ref[...])
        o_ref[...] = o
        x1_ref[...] = _rms_residual(x_ref[...], o, pg_ref[...])

    tile = lambda n: pl.BlockSpec((tm, n), lambda i: (i, 0))
    return pl.pallas_call(
        body, name="even_fwd", grid=(nt,),
        in_specs=[tile(D), tile(EVEN_IN), _halo_spec(EVEN_IN, nt, False, tm)] + [_whole()] * 8,
        out_specs=[tile(D), tile(D)],
        out_shape=[S((t, D), F32), S((t, D), F32)],
        scratch_shapes=[pltpu.VMEM((tm, MIX), BF16), pltpu.VMEM((tm + HALO, BW), F32)],
        compiler_params=_params(("arbitrary",)),
    )(x, p, p, lng, lnb, ws, bmap, wc, kv, wout, pg)


def _even_bwd(dres, o, p, lng, lnb, ws, wst, bmap, wc, kv, wout, pg):
    t = dres.shape[0]
    tm = min(TM_BWD_EVEN, t)
    nt = t // tm

    def body(dres_ref, o_ref, p_ref, ph_ref, lng_ref, lnb_ref, ws_ref, wst_ref, bmap_ref, wc_ref, kv_ref, wout_ref,
             pg_ref, dp_ref, y_ref, do_ref, dpg_ref, dws_ref, dbs_ref, dlng_ref, dlnb_ref, dwc_ref, dkv_ref,
             dy, cbuf, gbuf, dconv, carry, dvn, dbmap, wacc):
        i = pl.program_id(0)
        ti = nt - 1 - i
        masks = _group_masks()

        @pl.when(i == 0)
        def _():
            for ref in (dpg_ref, dws_ref, dlng_ref, dlnb_ref, dkv_ref, dbmap, wacc):
                ref[...] = jnp.zeros_like(ref)

        do, dpg = _rms_post_bwd(dres_ref[...], o_ref[...], pg_ref[...])
        dpg_ref[...] += dpg
        do_b = do.astype(BF16)
        do_ref[...] = do_b
        dy[...] = _dot_nt(do_b, wout_ref[...])

        vh, rs = _ln_stats(p_ref[:, E_V:E_V + BW])
        vn = vh * lng_ref[...] + lnb_ref[...]
        for c in range(tm // CHUNK):
            sl = slice(c * CHUNK, (c + 1) * CHUNK)
            vn_b = vn[sl].astype(BF16)
            sg = _sgu_chunk(vn_b, ws_ref, bmap_ref)
            u = p_ref[sl, E_U:E_U + BW]
            gate, dgate = _silu_parts(p_ref[sl, E_GATE:E_GATE + BW])
            dyc = dy[sl, 0:BW]
            ya = u * sg
            y_ref[sl, 0:BW] = (ya * gate).astype(BF16)
            dp_ref[sl, E_GATE:E_GATE + BW] = (dyc * ya * dgate).astype(BF16)
            dya = dyc * gate
            dp_ref[sl, E_U:E_U + BW] = (dya * sg).astype(BF16)
            dsg = dya * u
            dbmap[...] += dsg
            dsg_b = dsg.astype(BF16)
            for h in range(NH):
                total = None
                for j, heads in enumerate(_TILE_GROUPS):
                    if h in heads:
                        d_t = dsg_b[:, _tile(j)]
                        if len(heads) == 2:
                            d_t = jnp.where(_low_lanes() == (h == heads[0]), d_t, jnp.zeros_like(d_t))
                        part = _dot_nt(d_t, vn_b[:, _tile(j)])
                        total = part if total is None else total + part
                dws_ref[h] += total
            dvn[sl, :] = _by_group(lambda h, j: _dot(wst_ref[h], dsg_b[:, _tile(j)]))
        dn = dvn[...]
        dlng_ref[...] += _colsum(dn * vh)
        dlnb_ref[...] += _colsum(dn)
        dp_ref[:, E_V:E_V + BW] = _ln_bwd(dn, vh, rs, lng_ref[...]).astype(BF16)

        cbuf[0:HALO] = jnp.where(ti > 0, ph_ref[:, E_CG:E_CG + BW] * ph_ref[:, E_XIN:E_XIN + BW], 0.0)
        cbuf[HALO:HALO + tm] = p_ref[:, E_CG:E_CG + BW] * p_ref[:, E_XIN:E_XIN + BW]
        for r0 in range(0, tm, 32):
            sl = slice(r0, r0 + 32)
            cv = _tap_sum(cbuf, None, wc_ref, r0, 3, True)
            gate, dgate = _silu_parts(p_ref[sl, E_GATE + BW:E_GATE + 2 * BW])
            bg = p_ref[sl, E_BG:E_BG + BW]
            dyc = dy[sl, BW:2 * BW]
            yb = bg * cv
            y_ref[sl, BW:2 * BW] = (yb * gate).astype(BF16)
            dp_ref[sl, E_GATE + BW:E_GATE + 2 * BW] = (dyc * yb * dgate).astype(BF16)
            dyb = dyc * gate
            dp_ref[sl, E_BG:E_BG + BW] = (dyb * cv).astype(BF16)
            dconv[sl, :] = dyb * bg
        gbuf[0:tm] = dconv[...]
        gbuf[tm:tm + HALO] = jnp.where(i > 0, carry[...], 0.0)
        carry[...] = dconv[0:HALO]
        for r0 in range(0, tm, 32):
            sl = slice(r0, r0 + 32)
            _tap_grads(dconv[sl, :], cbuf, None, wacc, r0, 3)
            dc = _tap_sum(gbuf, None, wc_ref, r0, 3, False)
            dp_ref[sl, E_CG:E_CG + BW] = (dc * p_ref[sl, E_XIN:E_XIN + BW]).astype(BF16)
            dp_ref[sl, E_XIN:E_XIN + BW] = (dc * p_ref[sl, E_CG:E_CG + BW]).astype(BF16)

        for h in range(NH):
            qs = slice(E_Q + h * HD, E_Q + (h + 1) * HD)
            ks = slice(h * HD, (h + 1) * HD)
            vs = slice(XA + h * HD, XA + (h + 1) * HD)
            gs = slice(E_GATE + 2 * BW + h * HD, E_GATE + 2 * BW + (h + 1) * HD)
            ys = slice(2 * BW + h * HD, 2 * BW + (h + 1) * HD)
            q_b = p_ref[:, qs].astype(BF16)
            prob, yx = _attn_head(q_b, kv_ref[:, ks], kv_ref[:, vs])
            gate, dgate = _silu_parts(p_ref[:, gs])
            dyc = dy[:, ys]
            y_ref[:, ys] = (yx * gate).astype(BF16)
            dp_ref[:, gs] = (dyc * yx * dgate).astype(BF16)
            dyx_b = (dyc * gate).astype(BF16)
            dprob = _dot_nt(dyx_b, kv_ref[:, vs])
            dkv_ref[:, vs] += _dot_tn(prob.astype(BF16), dyx_b)
            ds_b = (prob * (dprob - jnp.sum(dprob * prob, axis=-1, keepdims=True)) * (HD ** -0.5)).astype(BF16)
            dp_ref[:, qs] = _dot(ds_b, kv_ref[:, ks]).astype(BF16)
            dkv_ref[:, ks] += _dot_tn(ds_b, q_b)

        @pl.when(i == nt - 1)
        def _():
            for h in range(NH):
                dbs_ref[:, h * HD:(h + 1) * HD] = jnp.broadcast_to(
                    jnp.sum(dbmap[...] * masks[h], axis=-1, keepdims=True), (CHUNK, HD))
            for k in range(3):
                dwc_ref[k:k + 1, :] = _colsum(wacc[k * 8:(k + 1) * 8, :])
            dwc_ref[3:8, :] = jnp.zeros((5, BW), F32)
            causal = (lax.broadcasted_iota(jnp.int32, (CHUNK, CHUNK), 0)
                      >= lax.broadcasted_iota(jnp.int32, (CHUNK, CHUNK), 1))
            for h in range(NH):
                dws_ref[h] = jnp.where(causal, dws_ref[h], 0.0)

    rtile = lambda n: pl.BlockSpec((tm, n), lambda i: (nt - 1 - i, 0))
    outs = [S((t, EVEN_IN), BF16), S((t, MIX), BF16), S((t, D), BF16), S((1, D), F32), S((NH, CHUNK, CHUNK), F32),
            S((CHUNK, NH * HD), F32), S((1, BW), F32), S((1, BW), F32), S((8, BW), F32), S((N_MEM, 2 * XA), F32)]
    return pl.pallas_call(
        body, name="even_bwd", grid=(nt,),
        in_specs=[rtile(D), rtile(D), rtile(EVEN_IN), _halo_spec(EVEN_IN, nt, True, tm)] + [_whole()] * 9,
        out_specs=[rtile(EVEN_IN), rtile(MIX), rtile(D)] + [_full(s.shape) for s in outs[3:]],
        out_shape=outs,
        scratch_shapes=[pltpu.VMEM((tm, MIX), F32), pltpu.VMEM((tm + HALO, BW), F32), pltpu.VMEM((tm + HALO, BW), F32),
                        pltpu.VMEM((tm, BW), F32), pltpu.VMEM((HALO, BW), F32), pltpu.VMEM((tm, BW), F32),
                        pltpu.VMEM((CHUNK, BW), F32), pltpu.VMEM((3 * 8, BW), F32)],
        compiler_params=_params(("arbitrary",)),
    )(dres, o, p, p, lng, lnb, ws, wst, bmap, wc, kv, wout, pg)


def _pool_causal_levels(za, zb, zc, zd, tm):
    n = tm + HALO
    zb[pl.ds(8, n - 8), :] = za[pl.ds(8, n - 8), :] + za[pl.ds(7, n - 8), :]
    zc[pl.ds(16, n - 16), :] = zb[pl.ds(16, n - 16), :] + zb[pl.ds(14, n - 16), :]
    zd[pl.ds(24, n - 24), :] = zc[pl.ds(24, n - 24), :] + zc[pl.ds(20, n - 24), :]


def _pool_causal(za, zb, zc, zd, tm):
    _pool_causal_levels(za, zb, zc, zd, tm)
    s16 = zd[pl.ds(HALO, tm), :] + zd[pl.ds(HALO - 8, tm), :]
    return zb[pl.ds(HALO, tm), :], zc[pl.ds(HALO, tm), :], zd[pl.ds(HALO, tm), :], s16


def _pool_anticausal_levels(ea, eb, ec, ed, tm):
    n = tm + HALO
    eb[pl.ds(0, n - 8), :] = ea[pl.ds(0, n - 8), :] + ea[pl.ds(1, n - 8), :]
    ec[pl.ds(0, n - 16), :] = eb[pl.ds(0, n - 16), :] + eb[pl.ds(2, n - 16), :]
    ed[pl.ds(0, n - 24), :] = ec[pl.ds(0, n - 24), :] + ec[pl.ds(4, n - 24), :]


def _pool_weights(t0, masks, rows):
    del masks
    tf = (t0 + lax.broadcasted_iota(jnp.int32, (rows, 1), 0) + 1).astype(F32)
    inv = [jnp.broadcast_to(1.0 / jnp.minimum(tf, float(win)), (rows, LANE)) for win in POOL_WINDOWS]
    return _by_group(lambda g, j: inv[g])


_HALVES = (slice(0, BW // 2), slice(BW // 2, BW))


def _mix4(masks, parts):
    del masks
    return _by_group(lambda g, j: parts[g][:, _tile(j)])


def _odd_fwd(x1, tgt, p, wbd, cscale, dww, dwb, lng, lnb, wpw, pwb, kv, wout, pg):
    t = x1.shape[0]
    tm = min(TM_FWD, t)
    nt = t // tm

    def body(x_ref, tgt_ref, p_ref, ph_ref, wbd_ref, cs_ref, dww_ref, dwb_ref, lng_ref, lnb_ref, wpw_ref, pwb_ref,
             kv_ref, wout_ref, pg_ref, o_ref, dres_ref, loss_ref, conv_ref, ybuf, za, zb, zc, zd, gbuf, lacc, gsh):
        i = pl.program_id(0)
        masks = _group_masks()

        @pl.when(i == 0)
        def _():
            lacc[...] = jnp.zeros_like(lacc)

        z = p_ref[:, O_ZC:O_ZC + BW]
        za[0:HALO] = jnp.where(i > 0, ph_ref[:, O_ZC:O_ZC + BW], 0.0)
        za[HALO:HALO + tm] = z
        pooled = _mix4(masks, _pool_causal(za, zb, zc, zd, tm)) * _pool_weights(i * tm, masks, tm) - z
        pooled_b = pooled.astype(BF16)
        for hs in _HALVES:
            gate, _ = _silu_parts(p_ref[:, O_GATE + hs.start:O_GATE + hs.stop])
            ybuf[:, hs] = (_dot(pooled_b[:, hs], wbd_ref[hs, hs]) * cs_ref[:, hs] * gate).astype(BF16)

        gbuf[0:HALO] = jnp.where(i > 0, ph_ref[:, O_GA:O_GA + BW] * _sigmoid(ph_ref[:, O_GB:O_GB + BW]), 0.0)
        gbuf[HALO:HALO + tm] = p_ref[:, O_GA:O_GA + BW] * _sigmoid(p_ref[:, O_GB:O_GB + BW])
        _shift_copies(gbuf, gsh)
        def conv_rows(r0, carry):
            conv_ref[pl.ds(r0, 32), :] = _tap_sum(gbuf, gsh, dww_ref, r0, CONF_K, True) + dwb_ref[...]
            return carry

        _loop_rows(tm, 32, conv_rows)
        zh, _ = _ln_stats(conv_ref[...])
        zn = zh * lng_ref[...] + lnb_ref[...]
        yd = _dot((zn * _sigmoid(zn)).astype(BF16), wpw_ref[...]) + pwb_ref[...]
        gate, _ = _silu_parts(p_ref[:, O_GATE + BW:O_GATE + 2 * BW])
        ybuf[:, BW:2 * BW] = (yd * gate).astype(BF16)

        for h in range(NH):
            qs = slice(O_Q + h * HD, O_Q + (h + 1) * HD)
            _, yx = _attn_head(p_ref[:, qs].astype(BF16), kv_ref[:, h * HD:(h + 1) * HD],
                               kv_ref[:, XA + h * HD:XA + (h + 1) * HD])
            gs = slice(O_GATE + 2 * BW + h * HD, O_GATE + 2 * BW + (h + 1) * HD)
            gate, _ = _silu_parts(p_ref[:, gs])
            ybuf[:, 2 * BW + h * HD:2 * BW + (h + 1) * HD] = (yx * gate).astype(BF16)

        o = _dot(ybuf[...], wout_ref[...])
        o_ref[...] = o
        err = _rms_residual(x_ref[...], o, pg_ref[...]) - tgt_ref[...]
        lacc[...] += _colsum(err * err)
        dres_ref[...] = err * (1.0 / D)

        @pl.when(i == nt - 1)
        def _():
            loss_ref[...] = jnp.full((1, HD), jnp.sum(lacc[...]) * (0.5 / D), F32)

    tile = lambda n: pl.BlockSpec((tm, n), lambda i: (i, 0))
    ext = pltpu.VMEM((tm + HALO, BW), F32)
    return pl.pallas_call(
        body, name="odd_fwd", grid=(nt,),
        in_specs=[tile(D), tile(D), tile(ODD_IN), _halo_spec(ODD_IN, nt, False, tm)] + [_whole()] * 11,
        out_specs=[tile(D), tile(D), _full((1, HD)), tile(BW)],
        out_shape=[S((t, D), F32), S((t, D), F32), S((1, HD), F32), S((t, BW), F32)],
        scratch_shapes=[pltpu.VMEM((tm, MIX), BF16), ext, ext, ext, ext, ext,
                        pltpu.VMEM((1, D), F32), pltpu.VMEM((7, tm + HALO, BW), F32)],
        compiler_params=_params(("arbitrary",)),
    )(x1, tgt, p, p, wbd, cscale, dww, dwb, lng, lnb, wpw, pwb, kv, wout, pg)


def _odd_bwd(dres, o, p, conv, wbd, cscale, dww, dwb, lng, lnb, wpw, pwb, kv, wout, pg):
    t = dres.shape[0]
    tm = min(TM_BWD_ODD, t)
    nt = t // tm

    def body(dres_ref, o_ref, p_ref, ph_ref, conv_ref, wbd_ref, cs_ref, dww_ref, dwb_ref, lng_ref, lnb_ref, wpw_ref,
             pwb_ref, kv_ref, wout_ref, pg_ref, dp_ref, y_ref, do_ref, dpg_ref, dwbd_ref, dcs_ref, ddww_ref, ddwb_ref,
             dlng_ref, dlnb_ref, dwpw_ref, dpwb_ref, dkv_ref,
             dy, za, zb, zc, zd, gbuf, hbuf, tmp, carry_e, carry_d, wacc, gsh, hsh, t1, invs, b1, b2):
        i = pl.program_id(0)
        ti = nt - 1 - i
        masks = _group_masks()

        @pl.when(i == 0)
        def _():
            for ref in (dpg_ref, dwbd_ref, dcs_ref, ddwb_ref, dlng_ref, dlnb_ref, dwpw_ref, dpwb_ref, dkv_ref, wacc):
                ref[...] = jnp.zeros_like(ref)

        def post_norm_rows(r0, acc):
            sl = pl.ds(r0, RB)
            ov, dv = o_ref[sl, :], dres_ref[sl, :]
            r = lax.rsqrt(_rowmean(ov * ov) + EPS)
            oh = ov * r
            doh = dv * pg_ref[...]
            do_ref[sl, :] = (r * (doh - oh * _rowmean(doh * oh))).astype(BF16)
            return acc + dv * oh

        dpg_ref[...] += _colsum(_loop_rows(tm, RB, post_norm_rows, jnp.zeros((RB, D), F32)))
        dy[...] = _dot_nt(do_ref[...], wout_ref[...])

        za[0:HALO] = jnp.where(ti > 0, ph_ref[:, O_ZC:O_ZC + BW], 0.0)
        za[HALO:HALO + tm] = p_ref[:, O_ZC:O_ZC + BW]
        _pool_causal_levels(za, zb, zc, zd, tm)

        def pooled_rows(r0, carry):
            sl = pl.ds(r0, RB)
            at = lambda ref, back=0: ref[pl.ds(HALO + r0 - back, RB), :]
            inv = _pool_weights(ti * tm + r0, masks, RB)
            invs[sl, :] = inv
            sums = (at(zb), at(zc), at(zd), at(zd) + at(zd, 8))
            b1[sl, :] = (_mix4(masks, sums) * inv - p_ref[sl, O_ZC:O_ZC + BW]).astype(BF16)
            return carry

        _loop_rows(tm, RB, pooled_rows)
        for hs in _HALVES:
            t1[:, hs] = _dot(b1[:, hs], wbd_ref[hs, hs])

        def pool_gate_rows(r0, acc):
            sl = pl.ds(r0, RB)
            pm = t1[sl, :]
            gate, dgate = _silu_parts(p_ref[sl, O_GATE:O_GATE + BW])
            dyc = dy[sl, 0:BW]
            yc = pm * cs_ref[...]
            y_ref[sl, 0:BW] = (yc * gate).astype(BF16)
            dp_ref[sl, O_GATE:O_GATE + BW] = (dyc * yc * dgate).astype(BF16)
            dyc = dyc * gate
            b2[sl, :] = (dyc * cs_ref[...]).astype(BF16)
            return acc + dyc * pm

        dcs_ref[...] += _colsum(_loop_rows(tm, RB, pool_gate_rows, jnp.zeros((RB, BW), F32)))
        for hs in _HALVES:
            dwbd_ref[hs, hs] += _dot_tn(b1[:, hs], b2[:, hs])
            t1[:, hs] = _dot_nt(b2[:, hs], wbd_ref[hs, hs])

        def weighted_rows(r0, carry):
            sl = pl.ds(r0, RB)
            za[sl, :] = t1[sl, :] * invs[sl, :]
            return carry

        _loop_rows(tm, RB, weighted_rows)
        za[tm:tm + HALO] = jnp.where(i > 0, carry_e[...], 0.0)
        carry_e[...] = za[0:HALO]
        _pool_anticausal_levels(za, zb, zc, zd, tm)

        def pool_back_rows(r0, carry):
            sl = pl.ds(r0, RB)
            ahead = lambda ref, fwd=0: ref[pl.ds(r0 + fwd, RB), :]
            sums = (ahead(zb), ahead(zc), ahead(zd), ahead(zd) + ahead(zd, 8))
            dp_ref[sl, O_ZC:O_ZC + BW] = (_mix4(masks, sums) - t1[sl, :]).astype(BF16)
            return carry

        _loop_rows(tm, RB, pool_back_rows)

        gbuf[0:HALO] = jnp.where(ti > 0, ph_ref[:, O_GA:O_GA + BW] * _sigmoid(ph_ref[:, O_GB:O_GB + BW]), 0.0)

        def glu_rows(r0, carry):
            sl = pl.ds(r0, RB)
            gbuf[pl.ds(HALO + r0, RB), :] = p_ref[sl, O_GA:O_GA + BW] * _sigmoid(p_ref[sl, O_GB:O_GB + BW])
            zh, _ = _ln_stats(conv_ref[sl, :])
            zn = zh * lng_ref[...] + lnb_ref[...]
            b1[sl, :] = (zn * _sigmoid(zn)).astype(BF16)
            return carry

        _loop_rows(tm, RB, glu_rows)
        _shift_copies(gbuf, gsh)
        t1[...] = _dot(b1[...], wpw_ref[...])

        def conf_gate_rows(r0, acc):
            sl = pl.ds(r0, RB)
            yd = t1[sl, :] + pwb_ref[...]
            gate, dgate = _silu_parts(p_ref[sl, O_GATE + BW:O_GATE + 2 * BW])
            dyc = dy[sl, BW:2 * BW]
            y_ref[sl, BW:2 * BW] = (yd * gate).astype(BF16)
            dp_ref[sl, O_GATE + BW:O_GATE + 2 * BW] = (dyc * yd * dgate).astype(BF16)
            dyd = dyc * gate
            b2[sl, :] = dyd.astype(BF16)
            return acc + dyd

        dpwb_ref[...] += _colsum(_loop_rows(tm, RB, conf_gate_rows, jnp.zeros((RB, BW), F32)))
        dwpw_ref[...] += _dot_tn(b1[...], b2[...])
        t1[...] = _dot_nt(b2[...], wpw_ref[...])

        def norm_back_rows(r0, accs):
            sl = pl.ds(r0, RB)
            zh, rs = _ln_stats(conv_ref[sl, :])
            _, dsilu = _silu_parts(zh * lng_ref[...] + lnb_ref[...])
            dzn = t1[sl, :] * dsilu
            dzd = _ln_bwd(dzn, zh, rs, lng_ref[...])
            tmp[sl, :] = dzd
            hbuf[sl, :] = dzd
            return accs[0] + dzn * zh, accs[1] + dzn, accs[2] + dzd

        zero = jnp.zeros((RB, BW), F32)
        acc_g, acc_b, acc_d = _loop_rows(tm, RB, norm_back_rows, (zero, zero, zero))
        dlng_ref[...] += _colsum(acc_g)
        dlnb_ref[...] += _colsum(acc_b)
        ddwb_ref[...] += _colsum(acc_d)
        hbuf[tm:tm + HALO] = jnp.where(i > 0, carry_d[...], 0.0)
        carry_d[...] = tmp[0:HALO]
        _shift_copies(hbuf, hsh)

        def conv_back_rows(r0, carry):
            sl = pl.ds(r0, 32)
            _tap_grads(tmp[sl, :], gbuf, gsh, wacc, r0, CONF_K)
            dzg = _tap_sum(hbuf, hsh, dww_ref, r0, CONF_K, False)
            sgb = _sigmoid(p_ref[sl, O_GB:O_GB + BW])
            dp_ref[sl, O_GA:O_GA + BW] = (dzg * sgb).astype(BF16)
            dp_ref[sl, O_GB:O_GB + BW] = (dzg * p_ref[sl, O_GA:O_GA + BW] * sgb * (1.0 - sgb)).astype(BF16)
            return carry

        _loop_rows(tm, 32, conv_back_rows, unrolled=False)

        for h in range(NH):
            qs = slice(O_Q + h * HD, O_Q + (h + 1) * HD)
            ks = slice(h * HD, (h + 1) * HD)
            vs = slice(XA + h * HD, XA + (h + 1) * HD)
            gs = slice(O_GATE + 2 * BW + h * HD, O_GATE + 2 * BW + (h + 1) * HD)
            ys = slice(2 * BW + h * HD, 2 * BW + (h + 1) * HD)
            q_b = p_ref[:, qs].astype(BF16)
            prob, yx = _attn_head(q_b, kv_ref[:, ks], kv_ref[:, vs])
            gate, dgate = _silu_parts(p_ref[:, gs])
            dyc = dy[:, ys]
            y_ref[:, ys] = (yx * gate).astype(BF16)
            dp_ref[:, gs] = (dyc * yx * dgate).astype(BF16)
            dyx_b = (dyc * gate).astype(BF16)
            dprob = _dot_nt(dyx_b, kv_ref[:, vs])
            dkv_ref[:, vs] += _dot_tn(prob.astype(BF16), dyx_b)
            ds_b = (prob * (dprob - jnp.sum(dprob * prob, axis=-1, keepdims=True)) * (HD ** -0.5)).astype(BF16)
            dp_ref[:, qs] = _dot(ds_b, kv_ref[:, ks]).astype(BF16)
            dkv_ref[:, ks] += _dot_tn(ds_b, q_b)

        @pl.when(i == nt - 1)
        def _():
            for k in range(CONF_K):
                ddww_ref[k:k + 1, :] = _colsum(wacc[k * 8:(k + 1) * 8, :])
            ddww_ref[CONF_K:CONF_K + 1, :] = jnp.zeros((1, BW), F32)

    rtile = lambda n: pl.BlockSpec((tm, n), lambda i: (nt - 1 - i, 0))
    outs = [S((t, ODD_IN), BF16), S((t, MIX), BF16), S((t, D), BF16), S((1, D), F32), S((BW, BW), F32),
            S((1, BW), F32), S((CONF_K + 1, BW), F32), S((1, BW), F32), S((1, BW), F32), S((1, BW), F32),
            S((BW, BW), F32), S((1, BW), F32), S((N_MEM, 2 * XA), F32)]
    ext = pltpu.VMEM((tm + HALO, BW), F32)
    return pl.pallas_call(
        body, name="odd_bwd", grid=(nt,),
        in_specs=[rtile(D), rtile(D), rtile(ODD_IN), _halo_spec(ODD_IN, nt, True, tm), rtile(BW)] + [_whole()] * 11,
        out_specs=[rtile(ODD_IN), rtile(MIX), rtile(D)] + [_full(s.shape) for s in outs[3:]],
        out_shape=outs,
        scratch_shapes=[pltpu.VMEM((tm, MIX), F32), ext, ext, ext, ext, ext, ext, pltpu.VMEM((tm, BW), F32),
                        pltpu.VMEM((HALO, BW), F32), pltpu.VMEM((HALO, BW), F32), pltpu.VMEM((CONF_K * 8, BW), F32),
                        pltpu.VMEM((7, tm + HALO, BW), F32), pltpu.VMEM((7, tm + HALO, BW), F32),
                        pltpu.VMEM((tm, BW), F32), pltpu.VMEM((tm, BW), F32), pltpu.VMEM((tm, BW), BF16),
                        pltpu.VMEM((tm, BW), BF16)],
        compiler_params=_params(("arbitrary",)),
    )(dres, o, p, p, conv, wbd, cscale, dww, dwb, lng, lnb, wpw, pwb, kv, wout, pg)


def _pick_rows(n):
    for rows in (640, 2432, 1024, 768):
        if n % rows == 0:
            return rows
    return n


def _pad_rows(a, rows):
    return jnp.pad(a, ((0, rows - a.shape[0]), (0, 0)))


def _step(x, mem, tgt, ex):
    t = x.shape[0]
    tm = min(256, t)
    w, deps = ex.first()
    causal = jnp.tril(jnp.ones((CHUNK, CHUNK), bool))
    ws = jnp.where(causal[None], w["even_a_ws"], 0.0).astype(BF16)
    wst = jnp.transpose(ws, (0, 2, 1))
    bmap = jnp.repeat(w["even_a_bs"].T, GRP, axis=1)
    wc = _pad_rows(w["even_b_conv"], 8)
    wbd = jax.scipy.linalg.block_diag(*[w["odd_c_wgrp"][g] for g in range(NH)]).astype(BF16)
    dww = _pad_rows(w["odd_d_dw_w"], CONF_K + 1)
    tk = min(1024, t)
    zeros = jnp.zeros_like(mem)

    p_e, h_e = _rms_matmul(x, w["even_pre_g"], w["even_w_in"], tm=tm, name="in_even", transposed=True, deps=deps)
    w.update(ex.even_rest(h_e))
    kv_e, memn_e = _rms_matmul(mem, w["even_mem_g"], w["even_w_kv"], tm=N_MEM, name="kv_even")
    kv_e = kv_e.astype(BF16)
    even_args = (w["even_a_ln_g"], w["even_a_ln_b"], ws)
    o_e, x1 = _even_fwd(x, p_e, *even_args, bmap, wc, kv_e, w["even_w_out"], w["even_post_g"])
    w.update(ex.odd(o_e))
    kv_o, memn_o = _rms_matmul(mem, w["odd_mem_g"], w["odd_w_kv"], tm=N_MEM, name="kv_odd")
    kv_o = kv_o.astype(BF16)
    p_o, h_o = _rms_matmul(x1, w["odd_pre_g"], w["odd_w_in"], tm=tm, name="in_odd", transposed=True)
    odd_args = (wbd, w["odd_c_scale"], dww, w["odd_d_dw_b"], w["odd_d_ln_g"], w["odd_d_ln_b"], w["odd_d_pw_w"],
                w["odd_d_pw_b"], kv_o, w["odd_w_out"], w["odd_post_g"])
    o_o, dres, loss, conv_o = _odd_fwd(x1, tgt, p_o, *odd_args)

    g = {}
    (dp_o, y_o, do_o, post_g_o, dwbd, g["odd_c_scale"], ddww, g["odd_d_dw_b"], g["odd_d_ln_g"], g["odd_d_ln_b"],
     dwpw, g["odd_d_pw_b"], dkv_o) = _odd_bwd(dres, o_o, p_o, conv_o, *odd_args)
    g["odd_post_g"] = post_g_o
    g["odd_d_dw_w"] = ddww[:CONF_K]
    dkv_o = dkv_o.astype(BF16)
    deps = ex.send("odd_rest", {
        "odd_w_out": _tn_matmul(y_o, do_o, tmc=MIX, tk=tk, out_dtype=BF16, name="dw_out_odd"),
        "odd_w_kv": _tn_matmul(memn_o, dkv_o, tmc=D, tk=N_MEM, out_dtype=BF16, name="dw_kv_odd"),
        "odd_d_pw_w": dwpw.astype(BF16), "loss": loss,
        "odd_c_wgrp": jnp.concatenate([dwbd[i * GRP:(i + 1) * GRP, i * GRP:(i + 1) * GRP] for i in range(NH)])})
    deps = ex.send("odd_in", {"odd_w_in": _tn_matmul(dp_o, h_o, tmc=_pick_rows(ODD_IN), tk=tk, out_dtype=BF16,
                                                     name="dw_in_odd", deps=deps)})
    dx1, g["odd_pre_g"] = _nt_matmul_rms_bwd(dp_o, w["odd_w_in"], x1, w["odd_pre_g"], dres, tm=min(256, t),
                                             name="dx_odd", transposed=True, deps=deps)
    _, g["odd_mem_g"] = _nt_matmul_rms_bwd(dkv_o, w["odd_w_kv"], mem, w["odd_mem_g"], zeros, tm=N_MEM,
                                           name="dmem_odd")

    (dp_e, y_e, do_e, post_g_e, dws, dbs, ln_g_e, ln_b_e, dwc, dkv_e) = _even_bwd(
        dx1, o_e, p_e, *even_args, wst, bmap, wc, kv_e, w["even_w_out"], w["even_post_g"])
    g["even_b_conv"] = dwc[:3]
    dkv_e = dkv_e.astype(BF16)
    deps = ex.send("even_rest", {
        "even_w_out": _tn_matmul(y_e, do_e, tmc=MIX, tk=tk, out_dtype=BF16, name="dw_out_even"),
        "even_w_kv": _tn_matmul(memn_e, dkv_e, tmc=D, tk=N_MEM, out_dtype=BF16, name="dw_kv_even"),
        "even_a_ln_g": ln_g_e, "even_a_ln_b": ln_b_e,
        "even_a_ws": dws.reshape(NH * CHUNK, CHUNK), "even_a_bs": dbs[:, ::HD].T})
    g["even_w_in"] = _tn_matmul(dp_e, h_e, tmc=_pick_rows(EVEN_IN), tk=tk, out_dtype=BF16, name="dw_in_even",
                                deps=deps)
    deps = ex.send("even_in", g)
    grad_x, pre_g_e = _nt_matmul_rms_bwd(dp_e, w["even_w_in"], x, w["even_pre_g"], dx1, tm=min(256, t),
                                         name="dx_even", transposed=True, deps=deps)
    _, mem_g_e = _nt_matmul_rms_bwd(dkv_e, w["even_w_kv"], mem, w["even_mem_g"], zeros, tm=N_MEM, name="dmem_even",
                                    deps=(grad_x,))
    ex.send("even_gains", {"even_pre_g": pre_g_e, "even_mem_g": mem_g_e, "even_post_g": post_g_e})
    return grad_x, mem_g_e


def _place():
    return lax.axis_index("x"), lax.axis_index("y"), lax.axis_index("c")


def _index(px, py, pc):
    return 4 * px + 2 * py + pc


_COPIES = N_DEV - 1


def _all_gather(arrs, name):
    n = len(arrs)

    def body(*refs):
        ins, outs = refs[:n], refs[n:2 * n]
        send_sems, recv_sems, local_sems = refs[2 * n:]
        x, y, c = _place()
        me, sibling = (x, y, c), (x, y, 1 - c)
        chips = [(1 - x, y), (x, 1 - y), (1 - x, 1 - y)]

        def copy(a, k, block, to, src=None):
            dst = outs[a].at[_index(*block)]
            return pltpu.make_async_remote_copy(
                src_ref=dst if src is None else src, dst_ref=dst, send_sem=send_sems.at[a * _COPIES + k],
                recv_sem=recv_sems.at[a * _COPIES + k], device_id=to, device_id_type=MESH)

        mine = [pltpu.make_async_copy(ins[a], outs[a].at[_index(*me)], local_sems.at[a]) for a in range(n)]
        first = []
        for a in range(n):
            mine[a].start()
            first.append(copy(a, 0, me, sibling, src=ins[a]))
            first += [copy(a, 1 + j, me, (*chip, c), src=ins[a]) for j, chip in enumerate(chips)]
        for cp in first:
            cp.start()
        passed = []
        for j, chip in enumerate(chips):
            for a in range(n):
                copy(a, 1 + j, (*chip, c), me).wait_recv()
                passed.append(copy(a, 4 + j, (*chip, c), sibling))
                passed[-1].start()
        for a in range(n):
            copy(a, 0, sibling, me).wait_recv()
            for j, chip in enumerate(chips):
                copy(a, 4 + j, (*chip, 1 - c), me).wait_recv()
        for cp in first + passed:
            cp.wait_send()
        for cp in mine:
            cp.wait()

    return pl.pallas_call(
        body, name=name, in_specs=[_ANY] * n, out_specs=[_ANY] * n,
        out_shape=[S((N_DEV,) + a.shape, a.dtype) for a in arrs],
        scratch_shapes=[pltpu.SemaphoreType.DMA((n * _COPIES,)), pltpu.SemaphoreType.DMA((n * _COPIES,)),
                        pltpu.SemaphoreType.DMA((n,))],
    )(*arrs)


_HBM = pl.BlockSpec(memory_space=pltpu.HBM)
_SEM = pl.BlockSpec(memory_space=pltpu.SEMAPHORE)
_EFFECT = pltpu.SideEffectType.DATAFLOW_SIDE_EFFECTING


_ALL_FLIPS = [(k >> 2 & 1, k >> 1 & 1, k & 1) for k in range(1, N_DEV)]
_CHIP_FLIPS = [(1, 0, 0), (0, 1, 0), (1, 1, 0)]
_FLIPS = {"gather": _ALL_FLIPS, "scatter": _ALL_FLIPS, "gather_chips": [(0, 0, 1)] + _CHIP_FLIPS,
          "scatter_chips": _CHIP_FLIPS}


def _landing_shape(kind, a):
    return (N_DEV,) + a.shape if kind.startswith("gather") else a.shape


def _exchange_copies(kinds, srcs, lands, send_sems, recv_sems, local_sems, arriving):
    x, y, c = _place()
    mine = _index(x, y, c)
    remote, local = [], []
    for a, kind in enumerate(kinds):
        by_chip = kind == "scatter_chips"
        here = 2 * x + y if by_chip else mine
        own = srcs[a] if kind.startswith("gather") else srcs[a].at[here]
        local.append(pltpu.make_async_copy(own, lands[a].at[here], local_sems.at[a]))
        for k, (fx, fy, fc) in enumerate(_FLIPS[kind]):
            peer = (1 - x if fx else x, 1 - y if fy else y, 1 - c if fc else c)
            there = 2 * peer[0] + peer[1] if by_chip else _index(*peer)
            remote.append(pltpu.make_async_remote_copy(
                src_ref=srcs[a] if kind.startswith("gather") else srcs[a].at[there],
                dst_ref=lands[a].at[there if arriving else here],
                send_sem=send_sems.at[a * _COPIES + k], recv_sem=recv_sems.at[a * _COPIES + k],
                device_id=peer, device_id_type=MESH))
    return remote, local


def _exchange_start(items, name, deps=()):
    kinds = [kind for kind, _ in items]
    srcs = [a for _, a in items]
    n = len(items)
    lands = [lax.empty(_landing_shape(kind, a), a.dtype) for kind, a in items]

    def body(*refs):
        send_sems, recv_sems, local_sems = refs[2 * n + len(deps):2 * n + len(deps) + 3]
        remote, local = _exchange_copies(kinds, refs[:n], refs[n:2 * n], send_sems, recv_sems, local_sems, False)
        for cp in local + remote:
            cp.start()
        refs[-1][...] = jnp.zeros_like(refs[-1])

    held = [pltpu.HBM(a.shape, a.dtype) for a in srcs + lands]
    res = pl.pallas_call(
        body, name=name,
        out_shape=(pltpu.SemaphoreType.DMA((n * _COPIES,)), pltpu.SemaphoreType.DMA((n * _COPIES,)),
                   pltpu.SemaphoreType.DMA((n,)), *held, S((8, 128), F32)),
        in_specs=[_HBM] * (2 * n) + [_ANY] * len(deps),
        out_specs=(_SEM, _SEM, _SEM, *[_HBM] * (2 * n), _whole()),
        input_output_aliases={i: 3 + i for i in range(2 * n)},
        compiler_params=pltpu.CompilerParams(has_side_effects=_EFFECT),
    )(*[pltpu.with_memory_space_constraint(a, pltpu.HBM) for a in srcs + lands], *deps)
    return (kinds, res[:3], res[3:3 + 2 * n]), res[-1]


def _exchange_wait(handle, after, name):
    kinds, sems, held = handle
    n = len(kinds)

    def body(*refs):
        send_sems, recv_sems, local_sems = refs[2 * n:2 * n + 3]
        remote, local = _exchange_copies(kinds, refs[:n], refs[n:2 * n], send_sems, recv_sems, local_sems, True)
        for cp in remote:
            cp.wait_send()
            cp.wait_recv()
        for cp in local:
            cp.wait()

    res = pl.pallas_call(
        body, name=name, out_shape=[pltpu.HBM(a.shape, a.dtype) for a in held],
        in_specs=[_HBM] * (2 * n) + [_SEM] * 3 + [_ANY] * len(after), out_specs=[_HBM] * (2 * n),
        input_output_aliases={i: i for i in range(2 * n)},
        compiler_params=pltpu.CompilerParams(has_side_effects=_EFFECT),
    )(*held, *sems, *after)
    return res[n:]


_CHIPS = [(0, 0), (0, 1), (1, 0), (1, 1)]
_N_CHIPS = len(_CHIPS)


def _sibling_forward(lands, name):
    n = len(lands)

    def body(*refs):
        ins, outs = refs[:n], refs[n:2 * n]
        send_sems, recv_sems = refs[2 * n:]
        x, y, c = _place()
        sent, arriving = [], []
        for a in range(n):
            for j, (fx, fy, _) in enumerate(_CHIP_FLIPS):
                chip = (1 - x if fx else x, 1 - y if fy else y)
                sems = dict(send_sem=send_sems.at[a * 3 + j], recv_sem=recv_sems.at[a * 3 + j],
                            device_id=(x, y, 1 - c), device_id_type=MESH)
                mine, theirs = _index(*chip, c), _index(*chip, 1 - c)
                sent.append(pltpu.make_async_remote_copy(src_ref=ins[a].at[mine], dst_ref=outs[a].at[mine], **sems))
                arriving.append(pltpu.make_async_remote_copy(src_ref=ins[a].at[theirs], dst_ref=outs[a].at[theirs],
                                                             **sems))
        for cp in sent:
            cp.start()
        for cp in sent:
            cp.wait_send()
        for cp in arriving:
            cp.wait_recv()

    return pl.pallas_call(
        body, name=name, in_specs=[_ANY] * n, out_specs=[_ANY] * n,
        out_shape=[S(a.shape, a.dtype) for a in lands], input_output_aliases={a: a for a in range(n)},
        scratch_shapes=[pltpu.SemaphoreType.DMA((3 * n,)), pltpu.SemaphoreType.DMA((3 * n,))],
    )(*lands)


def _sibling_swap(arrs, name):
    n = len(arrs)

    def body(*refs):
        ins, outs = refs[:n], refs[n:2 * n]
        send_sems, recv_sems = refs[2 * n:]
        x, y, c = _place()
        copies = []
        for a in range(n):
            for q, chip in enumerate(_CHIPS):
                copies.append(pltpu.make_async_remote_copy(
                    src_ref=ins[a].at[_index(*chip, 1 - c)], dst_ref=outs[a].at[q],
                    send_sem=send_sems.at[a * _N_CHIPS + q], recv_sem=recv_sems.at[a * _N_CHIPS + q],
                    device_id=(x, y, 1 - c), device_id_type=MESH))
        for cp in copies:
            cp.start()
        for cp in copies:
            cp.wait_send()
            cp.wait_recv()

    return pl.pallas_call(
        body, name=name, in_specs=[_ANY] * n, out_specs=[_ANY] * n,
        out_shape=[S((_N_CHIPS,) + a.shape[1:], a.dtype) for a in arrs],
        scratch_shapes=[pltpu.SemaphoreType.DMA((_N_CHIPS * n,)), pltpu.SemaphoreType.DMA((_N_CHIPS * n,))],
    )(*arrs)


def _add_partials(mine, theirs, *, tr, name):
    _, r, c = mine.shape

    def body(mine_ref, theirs_ref, out_ref):
        core = lax.axis_index("c")
        own = jnp.where(core == 0, mine_ref[0].astype(F32), mine_ref[1].astype(F32))
        out_ref[0] = (own + theirs_ref[0].astype(F32)).astype(out_ref.dtype)

    return pl.pallas_call(
        body, name=name, grid=(_N_CHIPS, r // tr),
        in_specs=[pl.BlockSpec((2, tr, c), lambda q, i: (q, i, 0)), pl.BlockSpec((1, tr, c), lambda q, i: (q, i, 0))],
        out_specs=pl.BlockSpec((1, tr, c), lambda q, i: (q, i, 0)),
        out_shape=S((_N_CHIPS, r, c), mine.dtype),
        compiler_params=_params(("arbitrary", "arbitrary")),
    )(mine, theirs)


def _adamw(w, g, m, v):
    m = ADAM_B1 * m + (1.0 - ADAM_B1) * g
    v = ADAM_B2 * v + (1.0 - ADAM_B2) * (g * g)
    m_hat = m / (1.0 - ADAM_B1 ** ADAM_STEP)
    v_hat = v / (1.0 - ADAM_B2 ** ADAM_STEP)
    return -ADAM_LR * (m_hat / (jnp.sqrt(v_hat) + ADAM_EPS) + ADAM_WD * w), m, v


def _sum_devices(ref, rows):
    total = ref[0, rows, :].astype(F32)
    for s in range(1, ref.shape[0]):
        total = total + ref[s, rows, :].astype(F32)
    return total


def _adam_big(recv, w, m, v, *, tr, name):
    r, c = w.shape

    def body(recv_ref, w_ref, m_ref, v_ref, g_ref, d_ref, m2_ref, v2_ref):
        g = _sum_devices(recv_ref, slice(None))
        g_ref[...] = g
        d_ref[...], m2_ref[...], v2_ref[...] = _adamw(w_ref[...], g, m_ref[...], v_ref[...])

    blk = pl.BlockSpec((tr, c), lambda i: (i, 0))
    return pl.pallas_call(
        body, name=name, grid=(r // tr,),
        in_specs=[pl.BlockSpec((recv.shape[0], tr, c), lambda i: (0, i, 0)), blk, blk, blk],
        out_specs=[blk] * 4, out_shape=[S((r, c), F32)] * 4,
        compiler_params=_params(("arbitrary",)),
    )(recv, w, m, v)


_REPLICATED = {"even_pre_g": (0, 0, 1), "even_mem_g": (0, 8, 1), "even_post_g": (0, 16, 1),
               "even_a_ln_g": (1, 0, 1), "even_a_ln_b": (1, 8, 1),
               "even_a_ws": (2, 0, NH * CHUNK), "even_a_bs": (2, NH * CHUNK, NH),
               "odd_c_wgrp": (3, 0, NH * GRP)}
_SHARDED = {"odd_pre_g": (4, 0, 1), "odd_mem_g": (4, 8, 1), "odd_post_g": (4, 16, 1),
            "even_b_conv": (5, 0, 3), "odd_c_scale": (5, 8, 1), "odd_d_dw_w": (5, 16, CONF_K),
            "odd_d_dw_b": (5, 48, 1), "odd_d_ln_g": (5, 56, 1), "odd_d_ln_b": (5, 64, 1), "odd_d_pw_b": (5, 72, 1)}
_SMALL = {**_REPLICATED, **_SHARDED}
_SMALL_ROWS = {0: 24, 1: 16, 2: NH * CHUNK + 8, 3: NH * GRP, 4: 24, 5: 80}


def _adam_small(sources, wmv):
    names = list(_SMALL)
    ns = len(sources)

    def body(*refs):
        src = refs[:ns]
        ins = refs[ns:ns + 3 * len(names)]
        outs = refs[ns + 3 * len(names):]
        outs[-1][...] = _sum_devices(src[-1], slice(0, 1))
        for i, nm in enumerate(names):
            a, row0, rows = _SMALL[nm]
            g = _sum_devices(src[a], slice(row0, row0 + rows))
            w_ref, m_ref, v_ref = ins[3 * i:3 * i + 3]
            g_ref, d_ref, m2_ref, v2_ref = outs[4 * i:4 * i + 4]
            g_ref[...] = g
            d_ref[...], m2_ref[...], v2_ref[...] = _adamw(w_ref[...], g, m_ref[...], v_ref[...])

    flat = [t for nm in names for t in wmv[nm]]
    out_shape = [S(wmv[nm][0].shape, F32) for nm in names for _ in range(4)] + [S((1, HD), F32)]
    res = pl.pallas_call(
        body, name="adam_small", in_specs=[_whole()] * (ns + len(flat)), out_specs=[_whole()] * len(out_shape),
        out_shape=out_shape, compiler_params=_params(),
    )(*sources, *flat)
    return {nm: tuple(res[4 * i:4 * i + 4]) for i, nm in enumerate(names)}, res[-1]


_WEIGHTS = ["even_pre_g", "even_w_in", "even_a_ln_g", "even_a_ln_b", "even_a_ws", "even_a_bs", "even_b_conv",
            "even_mem_g", "even_w_kv", "even_w_out", "even_post_g", "odd_pre_g", "odd_w_in", "odd_c_wgrp",
            "odd_c_scale", "odd_d_dw_w", "odd_d_dw_b", "odd_d_ln_g", "odd_d_ln_b", "odd_d_pw_w", "odd_d_pw_b",
            "odd_mem_g", "odd_w_kv", "odd_w_out", "odd_post_g"]
_TRANSPOSED = ["even_w_in", "odd_w_in"]
_BIG = _TRANSPOSED + ["even_w_kv", "even_w_out", "odd_w_kv", "odd_w_out", "odd_d_pw_w"]
_BIG_TILE_ROWS = {"even_w_in": 400, "odd_w_in": 304, "even_w_kv": 128, "even_w_out": 128, "odd_w_kv": 128,
                  "odd_w_out": 128, "odd_d_pw_w": 96}


def _view2d(a, transposed):
    a = a[0]
    if a.ndim == 1:
        return a[None]
    if transposed:
        return a.T
    return a.reshape(-1, a.shape[-1])


def _rows8(a):
    return _pad_rows(a, -(-a.shape[0] // 8) * 8)


def _pack_rows(parts):
    return jnp.concatenate([_rows8(p) for p in parts], axis=0)


def _unshard_cols(a):
    return jnp.transpose(a, (1, 0, 2)).reshape(a.shape[1], N_DEV * a.shape[2])


def _shard_cols(a):
    return jnp.transpose(a.reshape(a.shape[0], N_DEV, a.shape[1] // N_DEV), (1, 0, 2))


def _rows_of(a):
    return a.reshape(-1, a.shape[-1])


_GROUPS = {"odd_rest": (["odd_w_out", "odd_w_kv", "odd_d_pw_w"], [3], []),
           "odd_in": (["odd_w_in"], [], []),
           "even_rest": (["even_w_out", "even_w_kv"], [1, 2], []),
           "even_in": (["even_w_in"], [], [4, 5]),
           "even_gains": ([], [0], [])}


_TWO_LEVEL = ("even_in",)


class _MeshExchange:
    def __init__(self, shard):
        self.shard = shard
        self.handles = {}

    def first(self):
        shard = self.shard
        packs = [_pack_rows([shard[nm] for nm in _SHARDED if _SHARDED[nm][0] == a]) for a in (4, 5)]
        w_in, p128, p96 = _all_gather([shard["even_w_in"].astype(BF16)] + packs, "gather_first")
        w = {nm: shard[nm] for nm in _REPLICATED}
        w["even_a_ws"] = w["even_a_ws"].reshape(NH, CHUNK, CHUNK)
        w["odd_c_wgrp"] = w["odd_c_wgrp"].reshape(NH, GRP, GRP)
        w["even_w_in"] = _rows_of(w_in)
        full_packs = {4: _unshard_cols(p128), 5: _unshard_cols(p96)}
        for nm, (a, row0, rows) in _SHARDED.items():
            w[nm] = full_packs[a][row0:row0 + rows]
        later = lambda names: [("gather_chips", shard[nm].astype(BF16)) for nm in names]
        self.handles["w_even"], token = _exchange_start(later(["even_w_kv", "even_w_out"]), "gather_even_start",
                                                        deps=(w_in,))
        self.handles["w_odd"], token = _exchange_start(later(["odd_w_in", "odd_w_kv", "odd_w_out", "odd_d_pw_w"]),
                                                       "gather_odd_start", deps=(token,))
        return w, (token,)

    def even_rest(self, after):
        landed = _exchange_wait(self.handles.pop("w_even"), (after,), "gather_even_wait")
        kv, out = _sibling_forward(landed, "forward_even")
        return {"even_w_kv": _rows_of(kv), "even_w_out": _rows_of(out)}

    def odd(self, after):
        landed = _exchange_wait(self.handles.pop("w_odd"), (after,), "gather_odd_wait")
        w_in, kv, out, pw = _sibling_forward(landed, "forward_odd")
        return {"odd_w_in": _rows_of(w_in), "odd_w_kv": _rows_of(kv), "odd_w_out": _rows_of(out),
                "odd_d_pw_w": _rows_of(pw)}

    def send(self, group, g):
        big, replicated, sharded = _GROUPS[group]
        by_owner = [g[nm].reshape(N_DEV, -1, g[nm].shape[-1]) for nm in big]
        if group in _TWO_LEVEL:
            theirs = _sibling_swap(by_owner, "swap_" + group)
            items = [("scatter_chips", _add_partials(a, b, tr=_BIG_TILE_ROWS[nm], name="chip_sum_" + nm))
                     for nm, a, b in zip(big, by_owner, theirs)]
        else:
            items = [("scatter", a) for a in by_owner]
        items += [("gather", _pack_rows([g[nm] for nm in _REPLICATED if _REPLICATED[nm][0] == a]))
                  for a in replicated]
        items += [("scatter", _shard_cols(_pack_rows([g[nm] for nm in _SHARDED if _SHARDED[nm][0] == a])))
                  for a in sharded]
        if group == "odd_rest":
            items.append(("gather", _rows8(g["loss"])))
        self.handles[group], token = _exchange_start(items, "send_" + group + "_start")
        return (token,)

    def receive(self, group, after):
        return _exchange_wait(self.handles.pop(group), (after,), "send_" + group + "_wait")


def kernel(x, mem, even_pre_g, even_w_in, even_a_ln_g, even_a_ln_b, even_a_ws, even_a_bs, even_b_conv, even_mem_g, even_w_kv, even_w_out, even_post_g, odd_pre_g, odd_w_in, odd_c_wgrp, odd_c_scale, odd_d_dw_w, odd_d_dw_b, odd_d_ln_g, odd_d_ln_b, odd_d_pw_w, odd_d_pw_b, odd_mem_g, odd_w_kv, odd_w_out, odd_post_g, loss_target, m_even_pre_g, m_even_w_in, m_even_a_ln_g, m_even_a_ln_b, m_even_a_ws, m_even_a_bs, m_even_b_conv, m_even_mem_g, m_even_w_kv, m_even_w_out, m_even_post_g, m_odd_pre_g, m_odd_w_in, m_odd_c_wgrp, m_odd_c_scale, m_odd_d_dw_w, m_odd_d_dw_b, m_odd_d_ln_g, m_odd_d_ln_b, m_odd_d_pw_w, m_odd_d_pw_b, m_odd_mem_g, m_odd_w_kv, m_odd_w_out, m_odd_post_g, v_even_pre_g, v_even_w_in, v_even_a_ln_g, v_even_a_ln_b, v_even_a_ws, v_even_a_bs, v_even_b_conv, v_even_mem_g, v_even_w_kv, v_even_w_out, v_even_post_g, v_odd_pre_g, v_odd_w_in, v_odd_c_wgrp, v_odd_c_scale, v_odd_d_dw_w, v_odd_d_dw_b, v_odd_d_ln_g, v_odd_d_ln_b, v_odd_d_pw_w, v_odd_d_pw_b, v_odd_mem_g, v_odd_w_kv, v_odd_w_out, v_odd_post_g):
    given = dict(locals())
    view = lambda nm, kind: _view2d(given[kind + nm], nm in _TRANSPOSED)
    shard = {nm: view(nm, "") for nm in _WEIGHTS}
    wmv = {nm: (shard[nm], view(nm, "m_"), view(nm, "v_")) for nm in _WEIGHTS}

    ex = _MeshExchange(shard)
    grad_x, last = _step(x[0], mem[0], loss_target[0], ex)

    res = {}

    def update(group, after):
        names = _GROUPS[group][0]
        landed = ex.receive(group, after)
        for nm, recv in zip(names, landed):
            res[nm] = _adam_big(recv, *wmv[nm], tr=_BIG_TILE_ROWS[nm], name="adam_" + nm)
        return landed[len(names):]

    c192, losses = update("odd_rest", last)
    update("odd_in", res["odd_d_pw_w"][0])
    c768, c128 = update("even_rest", res["odd_w_in"][0])
    a128, a96 = update("even_in", res["even_w_kv"][0])
    (c1024,) = update("even_gains", res["even_w_in"][0])
    small, loss = _adam_small([c1024, c768, c128, c192, a128, a96, losses], {nm: wmv[nm] for nm in _SMALL})
    res.update(small)
    total = loss[0, 0]
    back = lambda nm, a: (a.T if nm in _TRANSPOSED else a).reshape(given[nm].shape)
    outs = [[back(nm, res[nm][i]) for nm in _WEIGHTS] for i in range(4)]
    return (total, grad_x[None], *outs[0], *outs[1], *outs[2], *outs[3])
```

```python
import functools

import jax
import jax.numpy as jnp
from jax import lax
from jax.experimental import pallas as pl
from jax.experimental.pallas import tpu as pltpu

F32 = jnp.float32
BF16 = jnp.bfloat16
S = jax.ShapeDtypeStruct
MESH = pl.DeviceIdType.MESH
AXES = ("x", "y", "c")
N_DEV = 8

D = 1024
BW = 768
XA = 512
HD = 128
NH = 4
MIX = 2048
CHUNK = 128
GRP = 192
N_MEM = 256
CONF_K = 31
EPS = 1e-6
HALO = 32
POOL_WINDOWS = (2, 4, 8, 16)
TM_FWD = 256
TM_BWD_EVEN = 256
TM_BWD_ODD = 256
RB = 16

E_U, E_V, E_BG, E_CG, E_XIN, E_Q, E_GATE = 0, 768, 1536, 2304, 3072, 3840, 4352
EVEN_IN = 6400
O_ZC, O_GA, O_GB, O_Q, O_GATE = 0, 768, 1536, 2304, 2816
ODD_IN = 4864

ADAM_LR, ADAM_B1, ADAM_B2, ADAM_EPS, ADAM_WD, ADAM_STEP = 0.001, 0.9, 0.999, 1e-08, 0.01, 10

VMEM_LIMIT_V7X = 56 * 1024 * 1024
VMEM_LIMIT_ODD_BWD_V7X = 62 * 1024 * 1024


def _params(sem=None):
    return pltpu.CompilerParams(dimension_semantics=sem, vmem_limit_bytes=VMEM_LIMIT_V7X)


def _dot(a, b):
    return jnp.dot(a, b, preferred_element_type=F32)


def _dot_nt(a, b):
    return lax.dot_general(a, b, (((1,), (1,)), ((), ())), preferred_element_type=F32)


def _dot_tn(a, b):
    return lax.dot_general(a, b, (((0,), (0,)), ((), ())), preferred_element_type=F32)


def _sigmoid(z):
    return 1.0 / (1.0 + jnp.exp(-z))


def _rowmean(a):
    return jnp.mean(a, axis=-1, keepdims=True)


def _colsum(a):
    return jnp.sum(a, axis=0, keepdims=True)


def _ln_stats(v):
    mu = _rowmean(v)
    vc = v - mu
    rs = lax.rsqrt(_rowmean(vc * vc) + EPS)
    return vc * rs, rs


def _ln_bwd(dn, vh, rs, g):
    dvh = dn * g
    return rs * (dvh - _rowmean(dvh) - vh * _rowmean(dvh * vh))


def _group_masks():
    col = lax.broadcasted_iota(jnp.int32, (1, BW), 1)
    return [((col >= GRP * h) & (col < GRP * (h + 1))).astype(F32) for h in range(NH)]


def _full(shape):
    nd = len(shape)
    return pl.BlockSpec(shape, lambda *_: (0,) * nd)


def _whole():
    return pl.BlockSpec(memory_space=pltpu.VMEM)


_ANY = pl.BlockSpec(memory_space=pl.ANY)


def _after(body, n_in, deps):
    def ordered(*refs):
        return body(*refs[:n_in], *refs[n_in + len(deps):])
    return ordered


def _rms_matmul(x, g, w, *, tm, name, transposed=False, deps=()):
    t, d = x.shape
    n = w.shape[0] if transposed else w.shape[1]

    def body(x_ref, g_ref, w_ref, p_ref, h_ref):
        xv = x_ref[...]
        r = lax.rsqrt(_rowmean(xv * xv) + EPS)
        h = (xv * r * g_ref[...]).astype(BF16)
        h_ref[...] = h
        p_ref[...] = _dot_nt(h, w_ref[...]) if transposed else _dot(h, w_ref[...])

    return pl.pallas_call(
        _after(body, 3, deps), name=name, grid=(t // tm,),
        in_specs=[pl.BlockSpec((tm, d), lambda i: (i, 0)), _whole(), _whole()] + [_ANY] * len(deps),
        out_specs=[pl.BlockSpec((tm, n), lambda i: (i, 0)), pl.BlockSpec((tm, d), lambda i: (i, 0))],
        out_shape=[S((t, n), F32), S((t, d), BF16)],
        compiler_params=_params(("arbitrary",)),
    )(x, g, w, *deps)


def _nt_matmul_rms_bwd(dp, w, x, g, dres, *, tm, name, transposed=False, deps=()):
    t, n = dp.shape
    d = x.shape[1]

    def body(dp_ref, w_ref, x_ref, g_ref, dres_ref, dx_ref, dg_ref):
        @pl.when(pl.program_id(0) == 0)
        def _():
            dg_ref[...] = jnp.zeros_like(dg_ref)

        dh = _dot(dp_ref[...], w_ref[...]) if transposed else _dot_nt(dp_ref[...], w_ref[...])
        xv = x_ref[...]
        r = lax.rsqrt(_rowmean(xv * xv) + EPS)
        xh = xv * r
        dg_ref[...] += _colsum(dh * xh)
        dxh = dh * g_ref[...]
        dx_ref[...] = dres_ref[...] + r * (dxh - xh * _rowmean(dxh * xh))

    return pl.pallas_call(
        _after(body, 5, deps), name=name, grid=(t // tm,),
        in_specs=[pl.BlockSpec((tm, n), lambda i: (i, 0)), _whole(), pl.BlockSpec((tm, d), lambda i: (i, 0)),
                  _whole(), pl.BlockSpec((tm, d), lambda i: (i, 0))] + [_ANY] * len(deps),
        out_specs=[pl.BlockSpec((tm, d), lambda i: (i, 0)), pl.BlockSpec((1, d), lambda i: (0, 0))],
        out_shape=[S((t, d), F32), S((1, d), F32)],
        compiler_params=_params(("arbitrary",)),
    )(dp, w, x, g, dres, *deps)


def _tn_matmul(a, b, *, tmc, tk, out_dtype, name, deps=()):
    t, m = a.shape
    n = b.shape[1]
    nk = t // tk

    def body(a_ref, b_ref, o_ref, acc_ref):
        k = pl.program_id(1)

        @pl.when(k == 0)
        def _():
            acc_ref[...] = jnp.zeros_like(acc_ref)

        acc_ref[...] += _dot_tn(a_ref[...], b_ref[...])

        @pl.when(k == nk - 1)
        def _():
            o_ref[...] = acc_ref[...].astype(out_dtype)

    return pl.pallas_call(
        _after(body, 2, deps), name=name, grid=(m // tmc, nk),
        in_specs=[pl.BlockSpec((tk, tmc), lambda j, k: (k, j)), pl.BlockSpec((tk, n), lambda j, k: (k, 0))]
        + [_ANY] * len(deps),
        out_specs=pl.BlockSpec((tmc, n), lambda j, k: (j, 0)),
        out_shape=S((m, n), out_dtype),
        scratch_shapes=[pltpu.VMEM((tmc, n), F32)],
        compiler_params=_params(("arbitrary", "arbitrary")),
    )(a, b, *deps)


def _silu_parts(gt):
    sg = _sigmoid(gt)
    return gt * sg, sg * (1.0 + gt * (1.0 - sg))


def _attn_head(q_b, k_b, v_b):
    s = _dot_nt(q_b, k_b) * (HD ** -0.5)
    e = jnp.exp(s - jnp.max(s, axis=-1, keepdims=True))
    prob = e / jnp.sum(e, axis=-1, keepdims=True)
    return prob, _dot(prob.astype(BF16), v_b)


def _rms_residual(x, o, g):
    r = lax.rsqrt(_rowmean(o * o) + EPS)
    return x + o * r * g


def _rms_post_bwd(dres, o, g):
    r = lax.rsqrt(_rowmean(o * o) + EPS)
    oh = o * r
    doh = dres * g
    return r * (doh - oh * _rowmean(doh * oh)), _colsum(dres * oh)


LANE = 128
_TILE_GROUPS = [sorted({LANE * j // GRP, (LANE * j + LANE - 1) // GRP}) for j in range(BW // LANE)]


def _tile(j):
    return slice(LANE * j, LANE * (j + 1))


def _low_lanes():
    return lax.broadcasted_iota(jnp.int32, (1, LANE), 1) < GRP - LANE


def _by_group(fn):
    tiles = []
    for j, groups in enumerate(_TILE_GROUPS):
        if len(groups) == 1:
            tiles.append(fn(groups[0], j))
        else:
            tiles.append(jnp.where(_low_lanes(), fn(groups[0], j), fn(groups[1], j)))
    return jnp.concatenate(tiles, axis=1)


def _sgu_chunk(vn_b, ws_ref, bmap_ref):
    return bmap_ref[...] + _by_group(lambda h, j: _dot(ws_ref[h], vn_b[:, _tile(j)]))


def _shift_copies(buf, sh):
    n = buf.shape[0] - 8
    for b in range(1, 8):
        sh[b - 1, pl.ds(0, n), :] = buf[pl.ds(b, n), :]


def _loop_rows(rows, step, fn, carry=0, unrolled=True):
    if unrolled:
        for r0 in range(0, rows, step):
            carry = fn(r0, carry)
        return carry

    def body(j, c):
        return fn(pl.multiple_of(j * step, step), c)
    return lax.fori_loop(0, rows // step, body, carry)


def _rows_at(buf, sh, r0, off):
    b = off % 8
    if b == 0 or sh is None:
        return buf[pl.ds(r0 + off, 32), :]
    return sh[b - 1, pl.ds(r0 + (off - b), 32), :]


def _tap_sum(buf, sh, w_ref, r0, taps, causal):
    acc = None
    for k in range(taps):
        off = HALO - (taps - 1 - k) if causal else taps - 1 - k
        term = w_ref[k:k + 1, :] * _rows_at(buf, sh, r0, off)
        acc = term if acc is None else acc + term
    return acc


def _fold8(a):
    return a[0:8] + a[8:16] + a[16:24] + a[24:32]


def _tap_grads(dv, buf, sh, acc_ref, r0, taps):
    for k in range(taps):
        acc_ref[k * 8:(k + 1) * 8, :] += _fold8(dv * _rows_at(buf, sh, r0, HALO - (taps - 1 - k)))


def _halo_spec(n, nt, reverse, tm):
    per = tm // HALO
    if reverse:
        return pl.BlockSpec((HALO, n), lambda i: (jnp.maximum((nt - 1 - i) * per - 1, 0), 0))
    return pl.BlockSpec((HALO, n), lambda i: (jnp.maximum(i * per - 1, 0), 0))


def _even_fwd(x, p, lng, lnb, ws, bmap, wc, kv, wout, pg):
    t = x.shape[0]
    tm = min(TM_FWD, t)
    nt = t // tm

    def body(x_ref, p_ref, ph_ref, lng_ref, lnb_ref, ws_ref, bmap_ref, wc_ref, kv_ref, wout_ref, pg_ref,
             o_ref, x1_ref, ybuf, cbuf):
        i = pl.program_id(0)
        vh, _ = _ln_stats(p_ref[:, E_V:E_V + BW])
        vn = vh * lng_ref[...] + lnb_ref[...]
        for c in range(tm // CHUNK):
            sl = slice(c * CHUNK, (c + 1) * CHUNK)
            sg = _sgu_chunk(vn[sl].astype(BF16), ws_ref, bmap_ref)
            gate, _ = _silu_parts(p_ref[sl, E_GATE:E_GATE + BW])
            ybuf[sl, 0:BW] = (p_ref[sl, E_U:E_U + BW] * sg * gate).astype(BF16)

        cbuf[0:HALO] = jnp.where(i > 0, ph_ref[:, E_CG:E_CG + BW] * ph_ref[:, E_XIN:E_XIN + BW], 0.0)
        cbuf[HALO:HALO + tm] = p_ref[:, E_CG:E_CG + BW] * p_ref[:, E_XIN:E_XIN + BW]
        for r0 in range(0, tm, 32):
            sl = slice(r0, r0 + 32)
            cv = _tap_sum(cbuf, None, wc_ref, r0, 3, True)
            gate, _ = _silu_parts(p_ref[sl, E_GATE + BW:E_GATE + 2 * BW])
            ybuf[sl, BW:2 * BW] = (p_ref[sl, E_BG:E_BG + BW] * cv * gate).astype(BF16)

        for h in range(NH):
            qs = slice(E_Q + h * HD, E_Q + (h + 1) * HD)
            _, yx = _attn_head(p_ref[:, qs].astype(BF16), kv_ref[:, h * HD:(h + 1) * HD],
                               kv_ref[:, XA + h * HD:XA + (h + 1) * HD])
            gs = slice(E_GATE + 2 * BW + h * HD, E_GATE + 2 * BW + (h + 1) * HD)
            gate, _ = _silu_parts(p_ref[:, gs])
            ybuf[:, 2 * BW + h * HD:2 * BW + (h + 1) * HD] = (yx * gate).astype(BF16)

        o = _dot(ybuf[...], wout_ref[...])
        o_ref[...] = o
        x1_ref[...] = _rms_residual(x_ref[...], o, pg_ref[...])

    tile = lambda n: pl.BlockSpec((tm, n), lambda i: (i, 0))
    return pl.pallas_call(
        body, name="even_fwd", grid=(nt,),
        in_specs=[tile(D), tile(EVEN_IN), _halo_spec(EVEN_IN, nt, False, tm)] + [_whole()] * 8,
        out_specs=[tile(D), tile(D)],
        out_shape=[S((t, D), F32), S((t, D), F32)],
        scratch_shapes=[pltpu.VMEM((tm, MIX), BF16), pltpu.VMEM((tm + HALO, BW), F32)],
        compiler_params=_params(("arbitrary",)),
    )(x, p, p, lng, lnb, ws, bmap, wc, kv, wout, pg)


def _even_bwd(dres, o, p, lng, lnb, ws, wst, bmap, wc, kv, wout, pg):
    t = dres.shape[0]
    tm = min(TM_BWD_EVEN, t)
    nt = t // tm

    def body(dres_ref, o_ref, p_ref, ph_ref, lng_ref, lnb_ref, ws_ref, wst_ref, bmap_ref, wc_ref, kv_ref, wout_ref,
             pg_ref, dp_ref, y_ref, do_ref, dpg_ref, dws_ref, dbs_ref, dlng_ref, dlnb_ref, dwc_ref, dkv_ref,
             dy, cbuf, gbuf, dconv, carry, dvn, dbmap, wacc):
        i = pl.program_id(0)
        ti = nt - 1 - i
        masks = _group_masks()

        @pl.when(i == 0)
        def _():
            for ref in (dpg_ref, dws_ref, dlng_ref, dlnb_ref, dkv_ref, dbmap, wacc):
                ref[...] = jnp.zeros_like(ref)

        do, dpg = _rms_post_bwd(dres_ref[...], o_ref[...], pg_ref[...])
        dpg_ref[...] += dpg
        do_b = do.astype(BF16)
        do_ref[...] = do_b
        dy[...] = _dot_nt(do_b, wout_ref[...])

        vh, rs = _ln_stats(p_ref[:, E_V:E_V + BW])
        vn = vh * lng_ref[...] + lnb_ref[...]
        for c in range(tm // CHUNK):
            sl = slice(c * CHUNK, (c + 1) * CHUNK)
            vn_b = vn[sl].astype(BF16)
            sg = _sgu_chunk(vn_b, ws_ref, bmap_ref)
            u = p_ref[sl, E_U:E_U + BW]
            gate, dgate = _silu_parts(p_ref[sl, E_GATE:E_GATE + BW])
            dyc = dy[sl, 0:BW]
            ya = u * sg
            y_ref[sl, 0:BW] = (ya * gate).astype(BF16)
            dp_ref[sl, E_GATE:E_GATE + BW] = (dyc * ya * dgate).astype(BF16)
            dya = dyc * gate
            dp_ref[sl, E_U:E_U + BW] = (dya * sg).astype(BF16)
            dsg = dya * u
            dbmap[...] += dsg
            dsg_b = dsg.astype(BF16)
            for h in range(NH):
                total = None
                for j, heads in enumerate(_TILE_GROUPS):
                    if h in heads:
                        d_t = dsg_b[:, _tile(j)]
                        if len(heads) == 2:
                            d_t = jnp.where(_low_lanes() == (h == heads[0]), d_t, jnp.zeros_like(d_t))
                        part = _dot_nt(d_t, vn_b[:, _tile(j)])
                        total = part if total is None else total + part
                dws_ref[h] += total
            dvn[sl, :] = _by_group(lambda h, j: _dot(wst_ref[h], dsg_b[:, _tile(j)]))
        dn = dvn[...]
        dlng_ref[...] += _colsum(dn * vh)
        dlnb_ref[...] += _colsum(dn)
        dp_ref[:, E_V:E_V + BW] = _ln_bwd(dn, vh, rs, lng_ref[...]).astype(BF16)

        cbuf[0:HALO] = jnp.where(ti > 0, ph_ref[:, E_CG:E_CG + BW] * ph_ref[:, E_XIN:E_XIN + BW], 0.0)
        cbuf[HALO:HALO + tm] = p_ref[:, E_CG:E_CG + BW] * p_ref[:, E_XIN:E_XIN + BW]
        for r0 in range(0, tm, 32):
            sl = slice(r0, r0 + 32)
            cv = _tap_sum(cbuf, None, wc_ref, r0, 3, True)
            gate, dgate = _silu_parts(p_ref[sl, E_GATE + BW:E_GATE + 2 * BW])
            bg = p_ref[sl, E_BG:E_BG + BW]
            dyc = dy[sl, BW:2 * BW]
            yb = bg * cv
            y_ref[sl, BW:2 * BW] = (yb * gate).astype(BF16)
            dp_ref[sl, E_GATE + BW:E_GATE + 2 * BW] = (dyc * yb * dgate).astype(BF16)
            dyb = dyc * gate
            dp_ref[sl, E_BG:E_BG + BW] = (dyb * cv).astype(BF16)
            dconv[sl, :] = dyb * bg
        gbuf[0:tm] = dconv[...]
        gbuf[tm:tm + HALO] = jnp.where(i > 0, carry[...], 0.0)
        carry[...] = dconv[0:HALO]
        for r0 in range(0, tm, 32):
            sl = slice(r0, r0 + 32)
            _tap_grads(dconv[sl, :], cbuf, None, wacc, r0, 3)
            dc = _tap_sum(gbuf, None, wc_ref, r0, 3, False)
            dp_ref[sl, E_CG:E_CG + BW] = (dc * p_ref[sl, E_XIN:E_XIN + BW]).astype(BF16)
            dp_ref[sl, E_XIN:E_XIN + BW] = (dc * p_ref[sl, E_CG:E_CG + BW]).astype(BF16)

        for h in range(NH):
            qs = slice(E_Q + h * HD, E_Q + (h + 1) * HD)
            ks = slice(h * HD, (h + 1) * HD)
            vs = slice(XA + h * HD, XA + (h + 1) * HD)
            gs = slice(E_GATE + 2 * BW + h * HD, E_GATE + 2 * BW + (h + 1) * HD)
            ys = slice(2 * BW + h * HD, 2 * BW + (h + 1) * HD)
            q_b = p_ref[:, qs].astype(BF16)
            prob, yx = _attn_head(q_b, kv_ref[:, ks], kv_ref[:, vs])
            gate, dgate = _silu_parts(p_ref[:, gs])
            dyc = dy[:, ys]
            y_ref[:, ys] = (yx * gate).astype(BF16)
            dp_ref[:, gs] = (dyc * yx * dgate).astype(BF16)
            dyx_b = (dyc * gate).astype(BF16)
            dprob = _dot_nt(dyx_b, kv_ref[:, vs])
            dkv_ref[:, vs] += _dot_tn(prob.astype(BF16), dyx_b)
            ds_b = (prob * (dprob - jnp.sum(dprob * prob, axis=-1, keepdims=True)) * (HD ** -0.5)).astype(BF16)
            dp_ref[:, qs] = _dot(ds_b, kv_ref[:, ks]).astype(BF16)
            dkv_ref[:, ks] += _dot_tn(ds_b, q_b)

        @pl.when(i == nt - 1)
        def _():
            for h in range(NH):
                dbs_ref[:, h * HD:(h + 1) * HD] = jnp.broadcast_to(
                    jnp.sum(dbmap[...] * masks[h], axis=-1, keepdims=True), (CHUNK, HD))
            for k in range(3):
                dwc_ref[k:k + 1, :] = _colsum(wacc[k * 8:(k + 1) * 8, :])
            dwc_ref[3:8, :] = jnp.zeros((5, BW), F32)
            causal = (lax.broadcasted_iota(jnp.int32, (CHUNK, CHUNK), 0)
                      >= lax.broadcasted_iota(jnp.int32, (CHUNK, CHUNK), 1))
            for h in range(NH):
                dws_ref[h] = jnp.where(causal, dws_ref[h], 0.0)

    rtile = lambda n: pl.BlockSpec((tm, n), lambda i: (nt - 1 - i, 0))
    outs = [S((t, EVEN_IN), BF16), S((t, MIX), BF16), S((t, D), BF16), S((1, D), F32), S((NH, CHUNK, CHUNK), F32),
            S((CHUNK, NH * HD), F32), S((1, BW), F32), S((1, BW), F32), S((8, BW), F32), S((N_MEM, 2 * XA), F32)]
    return pl.pallas_call(
        body, name="even_bwd", grid=(nt,),
        in_specs=[rtile(D), rtile(D), rtile(EVEN_IN), _halo_spec(EVEN_IN, nt, True, tm)] + [_whole()] * 9,
        out_specs=[rtile(EVEN_IN), rtile(MIX), rtile(D)] + [_full(s.shape) for s in outs[3:]],
        out_shape=outs,
        scratch_shapes=[pltpu.VMEM((tm, MIX), F32), pltpu.VMEM((tm + HALO, BW), F32), pltpu.VMEM((tm + HALO, BW), F32),
                        pltpu.VMEM((tm, BW), F32), pltpu.VMEM((HALO, BW), F32), pltpu.VMEM((tm, BW), F32),
                        pltpu.VMEM((CHUNK, BW), F32), pltpu.VMEM((3 * 8, BW), F32)],
        compiler_params=_params(("arbitrary",)),
    )(dres, o, p, p, lng, lnb, ws, wst, bmap, wc, kv, wout, pg)


def _pool_causal_levels(za, zb, zc, zd, tm):
    n = tm + HALO
    zb[pl.ds(8, n - 8), :] = za[pl.ds(8, n - 8), :] + za[pl.ds(7, n - 8), :]
    zc[pl.ds(16, n - 16), :] = zb[pl.ds(16, n - 16), :] + zb[pl.ds(14, n - 16), :]
    zd[pl.ds(24, n - 24), :] = zc[pl.ds(24, n - 24), :] + zc[pl.ds(20, n - 24), :]


def _pool_causal(za, zb, zc, zd, tm):
    _pool_causal_levels(za, zb, zc, zd, tm)
    s16 = zd[pl.ds(HALO, tm), :] + zd[pl.ds(HALO - 8, tm), :]
    return zb[pl.ds(HALO, tm), :], zc[pl.ds(HALO, tm), :], zd[pl.ds(HALO, tm), :], s16


def _pool_anticausal_levels(ea, eb, ec, ed, tm):
    n = tm + HALO
    eb[pl.ds(0, n - 8), :] = ea[pl.ds(0, n - 8), :] + ea[pl.ds(1, n - 8), :]
    ec[pl.ds(0, n - 16), :] = eb[pl.ds(0, n - 16), :] + eb[pl.ds(2, n - 16), :]
    ed[pl.ds(0, n - 24), :] = ec[pl.ds(0, n - 24), :] + ec[pl.ds(4, n - 24), :]


def _pool_weights(t0, masks, rows):
    del masks
    tf = (t0 + lax.broadcasted_iota(jnp.int32, (rows, 1), 0) + 1).astype(F32)
    inv = [jnp.broadcast_to(1.0 / jnp.minimum(tf, float(win)), (rows, LANE)) for win in POOL_WINDOWS]
    return _by_group(lambda g, j: inv[g])


_HALVES = (slice(0, BW // 2), slice(BW // 2, BW))


def _mix4(masks, parts):
    del masks
    return _by_group(lambda g, j: parts[g][:, _tile(j)])


def _odd_fwd(x1, tgt, p, wbd, cscale, dww, dwb, lng, lnb, wpw, pwb, kv, wout, pg):
    t = x1.shape[0]
    tm = min(TM_FWD, t)
    nt = t // tm

    def body(x_ref, tgt_ref, p_ref, ph_ref, wbd_ref, cs_ref, dww_ref, dwb_ref, lng_ref, lnb_ref, wpw_ref, pwb_ref,
             kv_ref, wout_ref, pg_ref, o_ref, dres_ref, loss_ref, conv_ref, ybuf, za, zb, zc, zd, gbuf, lacc, gsh):
        i = pl.program_id(0)
        masks = _group_masks()

        @pl.when(i == 0)
        def _():
            lacc[...] = jnp.zeros_like(lacc)

        z = p_ref[:, O_ZC:O_ZC + BW]
        za[0:HALO] = jnp.where(i > 0, ph_ref[:, O_ZC:O_ZC + BW], 0.0)
        za[HALO:HALO + tm] = z
        pooled = _mix4(masks, _pool_causal(za, zb, zc, zd, tm)) * _pool_weights(i * tm, masks, tm) - z
        pooled_b = pooled.astype(BF16)
        for hs in _HALVES:
            gate, _ = _silu_parts(p_ref[:, O_GATE + hs.start:O_GATE + hs.stop])
            ybuf[:, hs] = (_dot(pooled_b[:, hs], wbd_ref[hs, hs]) * cs_ref[:, hs] * gate).astype(BF16)

        gbuf[0:HALO] = jnp.where(i > 0, ph_ref[:, O_GA:O_GA + BW] * _sigmoid(ph_ref[:, O_GB:O_GB + BW]), 0.0)
        gbuf[HALO:HALO + tm] = p_ref[:, O_GA:O_GA + BW] * _sigmoid(p_ref[:, O_GB:O_GB + BW])
        _shift_copies(gbuf, gsh)
        def conv_rows(r0, carry):
            conv_ref[pl.ds(r0, 32), :] = _tap_sum(gbuf, gsh, dww_ref, r0, CONF_K, True) + dwb_ref[...]
            return carry

        _loop_rows(tm, 32, conv_rows)
        zh, _ = _ln_stats(conv_ref[...])
        zn = zh * lng_ref[...] + lnb_ref[...]
        yd = _dot((zn * _sigmoid(zn)).astype(BF16), wpw_ref[...]) + pwb_ref[...]
        gate, _ = _silu_parts(p_ref[:, O_GATE + BW:O_GATE + 2 * BW])
        ybuf[:, BW:2 * BW] = (yd * gate).astype(BF16)

        for h in range(NH):
            qs = slice(O_Q + h * HD, O_Q + (h + 1) * HD)
            _, yx = _attn_head(p_ref[:, qs].astype(BF16), kv_ref[:, h * HD:(h + 1) * HD],
                               kv_ref[:, XA + h * HD:XA + (h + 1) * HD])
            gs = slice(O_GATE + 2 * BW + h * HD, O_GATE + 2 * BW + (h + 1) * HD)
            gate, _ = _silu_parts(p_ref[:, gs])
            ybuf[:, 2 * BW + h * HD:2 * BW + (h + 1) * HD] = (yx * gate).astype(BF16)

        o = _dot(ybuf[...], wout_ref[...])
        o_ref[...] = o
        err = _rms_residual(x_ref[...], o, pg_ref[...]) - tgt_ref[...]
        lacc[...] += _colsum(err * err)
        dres_ref[...] = err * (1.0 / D)

        @pl.when(i == nt - 1)
        def _():
            loss_ref[...] = jnp.full((1, HD), jnp.sum(lacc[...]) * (0.5 / D), F32)

    tile = lambda n: pl.BlockSpec((tm, n), lambda i: (i, 0))
    ext = pltpu.VMEM((tm + HALO, BW), F32)
    return pl.pallas_call(
        body, name="odd_fwd", grid=(nt,),
        in_specs=[tile(D), tile(D), tile(ODD_IN), _halo_spec(ODD_IN, nt, False, tm)] + [_whole()] * 11,
        out_specs=[tile(D), tile(D), _full((1, HD)), tile(BW)],
        out_shape=[S((t, D), F32), S((t, D), F32), S((1, HD), F32), S((t, BW), F32)],
        scratch_shapes=[pltpu.VMEM((tm, MIX), BF16), ext, ext, ext, ext, ext,
                        pltpu.VMEM((1, D), F32), pltpu.VMEM((7, tm + HALO, BW), F32)],
        compiler_params=_params(("arbitrary",)),
    )(x1, tgt, p, p, wbd, cscale, dww, dwb, lng, lnb, wpw, pwb, kv, wout, pg)


def _odd_bwd(dres, o, p, conv, wbd, cscale, dww, dwb, lng, lnb, wpw, pwb, kv, wout, pg):
    t = dres.shape[0]
    tm = min(TM_BWD_ODD, t)
    nt = t // tm

    def body(dres_ref, o_ref, p_ref, ph_ref, conv_ref, wbd_ref, cs_ref, dww_ref, dwb_ref, lng_ref, lnb_ref, wpw_ref,
             pwb_ref, kv_ref, wout_ref, pg_ref, dp_ref, y_ref, do_ref, dpg_ref, dwbd_ref, dcs_ref, ddww_ref, ddwb_ref,
             dlng_ref, dlnb_ref, dwpw_ref, dpwb_ref, dkv_ref,
             dy, za, zb, zc, zd, tmp, carry_e, carry_d, wacc, shifted, t1, b1, b2):
        gbuf, hbuf = za, zb
        i = pl.program_id(0)
        ti = nt - 1 - i
        masks = _group_masks()

        @pl.when(i == 0)
        def _():
            for ref in (dpg_ref, dwbd_ref, dcs_ref, ddwb_ref, dlng_ref, dlnb_ref, dwpw_ref, dpwb_ref, dkv_ref, wacc):
                ref[...] = jnp.zeros_like(ref)

        def post_norm_rows(r0, acc):
            sl = pl.ds(r0, RB)
            ov, dv = o_ref[sl, :], dres_ref[sl, :]
            r = lax.rsqrt(_rowmean(ov * ov) + EPS)
            oh = ov * r
            doh = dv * pg_ref[...]
            do_ref[sl, :] = (r * (doh - oh * _rowmean(doh * oh))).astype(BF16)
            return acc + dv * oh

        dpg_ref[...] += _colsum(_loop_rows(tm, RB, post_norm_rows, jnp.zeros((RB, D), F32)))
        dy[...] = _dot_nt(do_ref[...], wout_ref[...])

        za[0:HALO] = jnp.where(ti > 0, ph_ref[:, O_ZC:O_ZC + BW], 0.0)
        za[HALO:HALO + tm] = p_ref[:, O_ZC:O_ZC + BW]
        _pool_causal_levels(za, zb, zc, zd, tm)

        def pooled_rows(r0, carry):
            sl = pl.ds(r0, RB)
            at = lambda ref, back=0: ref[pl.ds(HALO + r0 - back, RB), :]
            inv = _pool_weights(ti * tm + r0, masks, RB)
            sums = (at(zb), at(zc), at(zd), at(zd) + at(zd, 8))
            b1[sl, :] = (_mix4(masks, sums) * inv - p_ref[sl, O_ZC:O_ZC + BW]).astype(BF16)
            return carry

        _loop_rows(tm, RB, pooled_rows)
        for hs in _HALVES:
            t1[:, hs] = _dot(b1[:, hs], wbd_ref[hs, hs])

        def pool_gate_rows(r0, acc):
            sl = pl.ds(r0, RB)
            pm = t1[sl, :]
            gate, dgate = _silu_parts(p_ref[sl, O_GATE:O_GATE + BW])
            dyc = dy[sl, 0:BW]
            yc = pm * cs_ref[...]
            y_ref[sl, 0:BW] = (yc * gate).astype(BF16)
            dp_ref[sl, O_GATE:O_GATE + BW] = (dyc * yc * dgate).astype(BF16)
            dyc = dyc * gate
            b2[sl, :] = (dyc * cs_ref[...]).astype(BF16)
            return acc + dyc * pm

        dcs_ref[...] += _colsum(_loop_rows(tm, RB, pool_gate_rows, jnp.zeros((RB, BW), F32)))
        for hs in _HALVES:
            dwbd_ref[hs, hs] += _dot_tn(b1[:, hs], b2[:, hs])
            t1[:, hs] = _dot_nt(b2[:, hs], wbd_ref[hs, hs])

        def weighted_rows(r0, carry):
            sl = pl.ds(r0, RB)
            za[sl, :] = t1[sl, :] * _pool_weights(ti * tm + r0, masks, RB)
            return carry

        _loop_rows(tm, RB, weighted_rows)
        za[tm:tm + HALO] = jnp.where(i > 0, carry_e[...], 0.0)
        carry_e[...] = za[0:HALO]
        _pool_anticausal_levels(za, zb, zc, zd, tm)

        def pool_back_rows(r0, carry):
            sl = pl.ds(r0, RB)
            ahead = lambda ref, fwd=0: ref[pl.ds(r0 + fwd, RB), :]
            sums = (ahead(zb), ahead(zc), ahead(zd), ahead(zd) + ahead(zd, 8))
            dp_ref[sl, O_ZC:O_ZC + BW] = (_mix4(masks, sums) - t1[sl, :]).astype(BF16)
            return carry

        _loop_rows(tm, RB, pool_back_rows)

        gbuf[0:HALO] = jnp.where(ti > 0, ph_ref[:, O_GA:O_GA + BW] * _sigmoid(ph_ref[:, O_GB:O_GB + BW]), 0.0)

        def glu_rows(r0, carry):
            sl = pl.ds(r0, RB)
            gbuf[pl.ds(HALO + r0, RB), :] = p_ref[sl, O_GA:O_GA + BW] * _sigmoid(p_ref[sl, O_GB:O_GB + BW])
            zh, _ = _ln_stats(conv_ref[sl, :])
            zn = zh * lng_ref[...] + lnb_ref[...]
            b1[sl, :] = (zn * _sigmoid(zn)).astype(BF16)
            return carry

        _loop_rows(tm, RB, glu_rows)
        _shift_copies(gbuf, shifted)
        t1[...] = _dot(b1[...], wpw_ref[...])

        def conf_gate_rows(r0, acc):
            sl = pl.ds(r0, RB)
            yd = t1[sl, :] + pwb_ref[...]
            gate, dgate = _silu_parts(p_ref[sl, O_GATE + BW:O_GATE + 2 * BW])
            dyc = dy[sl, BW:2 * BW]
            y_ref[sl, BW:2 * BW] = (yd * gate).astype(BF16)
            dp_ref[sl, O_GATE + BW:O_GATE + 2 * BW] = (dyc * yd * dgate).astype(BF16)
            dyd = dyc * gate
            b2[sl, :] = dyd.astype(BF16)
            return acc + dyd

        dpwb_ref[...] += _colsum(_loop_rows(tm, RB, conf_gate_rows, jnp.zeros((RB, BW), F32)))
        dwpw_ref[...] += _dot_tn(b1[...], b2[...])
        t1[...] = _dot_nt(b2[...], wpw_ref[...])

        def norm_back_rows(r0, accs):
            sl = pl.ds(r0, RB)
            zh, rs = _ln_stats(conv_ref[sl, :])
            _, dsilu = _silu_parts(zh * lng_ref[...] + lnb_ref[...])
            dzn = t1[sl, :] * dsilu
            dzd = _ln_bwd(dzn, zh, rs, lng_ref[...])
            tmp[sl, :] = dzd
            hbuf[sl, :] = dzd
            return accs[0] + dzn * zh, accs[1] + dzn, accs[2] + dzd

        zero = jnp.zeros((RB, BW), F32)
        acc_g, acc_b, acc_d = _loop_rows(tm, RB, norm_back_rows, (zero, zero, zero))
        dlng_ref[...] += _colsum(acc_g)
        dlnb_ref[...] += _colsum(acc_b)
        ddwb_ref[...] += _colsum(acc_d)
        hbuf[tm:tm + HALO] = jnp.where(i > 0, carry_d[...], 0.0)
        carry_d[...] = tmp[0:HALO]
        def tap_grad_rows(r0, carry):
            _tap_grads(tmp[pl.ds(r0, 32), :], gbuf, shifted, wacc, r0, CONF_K)
            return carry

        _loop_rows(tm, 32, tap_grad_rows, unrolled=False)
        _shift_copies(hbuf, shifted)

        def conv_back_rows(r0, carry):
            sl = pl.ds(r0, 32)
            dzg = _tap_sum(hbuf, shifted, dww_ref, r0, CONF_K, False)
            sgb = _sigmoid(p_ref[sl, O_GB:O_GB + BW])
            dp_ref[sl, O_GA:O_GA + BW] = (dzg * sgb).astype(BF16)
            dp_ref[sl, O_GB:O_GB + BW] = (dzg * p_ref[sl, O_GA:O_GA + BW] * sgb * (1.0 - sgb)).astype(BF16)
            return carry

        _loop_rows(tm, 32, conv_back_rows, unrolled=False)

        for h in range(NH):
            qs = slice(O_Q + h * HD, O_Q + (h + 1) * HD)
            ks = slice(h * HD, (h + 1) * HD)
            vs = slice(XA + h * HD, XA + (h + 1) * HD)
            gs = slice(O_GATE + 2 * BW + h * HD, O_GATE + 2 * BW + (h + 1) * HD)
            ys = slice(2 * BW + h * HD, 2 * BW + (h + 1) * HD)
            q_b = p_ref[:, qs].astype(BF16)
            prob, yx = _attn_head(q_b, kv_ref[:, ks], kv_ref[:, vs])
            gate, dgate = _silu_parts(p_ref[:, gs])
            dyc = dy[:, ys]
            y_ref[:, ys] = (yx * gate).astype(BF16)
            dp_ref[:, gs] = (dyc * yx * dgate).astype(BF16)
            dyx_b = (dyc * gate).astype(BF16)
            dprob = _dot_nt(dyx_b, kv_ref[:, vs])
            dkv_ref[:, vs] += _dot_tn(prob.astype(BF16), dyx_b)
            ds_b = (prob * (dprob - jnp.sum(dprob * prob, axis=-1, keepdims=True)) * (HD ** -0.5)).astype(BF16)
            dp_ref[:, qs] = _dot(ds_b, kv_ref[:, ks]).astype(BF16)
            dkv_ref[:, ks] += _dot_tn(ds_b, q_b)

        @pl.when(i == nt - 1)
        def _():
            for k in range(CONF_K):
                ddww_ref[k:k + 1, :] = _colsum(wacc[k * 8:(k + 1) * 8, :])
            ddww_ref[CONF_K:CONF_K + 1, :] = jnp.zeros((1, BW), F32)

    rtile = lambda n: pl.BlockSpec((tm, n), lambda i: (nt - 1 - i, 0))
    outs = [S((t, ODD_IN), BF16), S((t, MIX), BF16), S((t, D), BF16), S((1, D), F32), S((BW, BW), F32),
            S((1, BW), F32), S((CONF_K + 1, BW), F32), S((1, BW), F32), S((1, BW), F32), S((1, BW), F32),
            S((BW, BW), F32), S((1, BW), F32), S((N_MEM, 2 * XA), F32)]
    ext = pltpu.VMEM((tm + HALO, BW), F32)
    return pl.pallas_call(
        body, name="odd_bwd", grid=(nt,),
        in_specs=[rtile(D), rtile(D), rtile(ODD_IN), _halo_spec(ODD_IN, nt, True, tm), rtile(BW)] + [_whole()] * 11,
        out_specs=[rtile(ODD_IN), rtile(MIX), rtile(D)] + [_full(s.shape) for s in outs[3:]],
        out_shape=outs,
        scratch_shapes=[pltpu.VMEM((tm, MIX), F32), ext, ext, ext, ext, pltpu.VMEM((tm, BW), F32),
                        pltpu.VMEM((HALO, BW), F32), pltpu.VMEM((HALO, BW), F32), pltpu.VMEM((CONF_K * 8, BW), F32),
                        pltpu.VMEM((7, tm + HALO, BW), F32),
                        pltpu.VMEM((tm, BW), F32), pltpu.VMEM((tm, BW), BF16), pltpu.VMEM((tm, BW), BF16)],
        compiler_params=pltpu.CompilerParams(dimension_semantics=("arbitrary",),
                                             vmem_limit_bytes=VMEM_LIMIT_ODD_BWD_V7X),
    )(dres, o, p, p, conv, wbd, cscale, dww, dwb, lng, lnb, wpw, pwb, kv, wout, pg)


def _pick_rows(n):
    for rows in (640, 2432, 1024, 768):
        if n % rows == 0:
            return rows
    return n


def _pad_rows(a, rows):
    return jnp.pad(a, ((0, rows - a.shape[0]), (0, 0)))


def _step(x, mem, tgt, ex):
    t = x.shape[0]
    tm = min(256, t)
    w, deps = ex.first()
    causal = jnp.tril(jnp.ones((CHUNK, CHUNK), bool))
    ws = jnp.where(causal[None], w["even_a_ws"], 0.0).astype(BF16)
    wst = jnp.transpose(ws, (0, 2, 1))
    bmap = jnp.repeat(w["even_a_bs"].T, GRP, axis=1)
    wc = _pad_rows(w["even_b_conv"], 8)
    wbd = jax.scipy.linalg.block_diag(*[w["odd_c_wgrp"][g] for g in range(NH)]).astype(BF16)
    dww = _pad_rows(w["odd_d_dw_w"], CONF_K + 1)
    tk = min(1024, t)
    zeros = jnp.zeros_like(mem)

    p_e, h_e = _rms_matmul(x, w["even_pre_g"], w["even_w_in"], tm=tm, name="in_even", transposed=True, deps=deps)
    w.update(ex.even_rest(h_e))
    kv_e, memn_e = _rms_matmul(mem, w["even_mem_g"], w["even_w_kv"], tm=N_MEM, name="kv_even")
    kv_e = kv_e.astype(BF16)
    even_args = (w["even_a_ln_g"], w["even_a_ln_b"], ws)
    o_e, x1 = _even_fwd(x, p_e, *even_args, bmap, wc, kv_e, w["even_w_out"], w["even_post_g"])
    w.update(ex.odd(o_e))
    kv_o, memn_o = _rms_matmul(mem, w["odd_mem_g"], w["odd_w_kv"], tm=N_MEM, name="kv_odd")
    kv_o = kv_o.astype(BF16)
    p_o, h_o = _rms_matmul(x1, w["odd_pre_g"], w["odd_w_in"], tm=tm, name="in_odd", transposed=True)
    odd_args = (wbd, w["odd_c_scale"], dww, w["odd_d_dw_b"], w["odd_d_ln_g"], w["odd_d_ln_b"], w["odd_d_pw_w"],
                w["odd_d_pw_b"], kv_o, w["odd_w_out"], w["odd_post_g"])
    o_o, dres, loss, conv_o = _odd_fwd(x1, tgt, p_o, *odd_args)

    g = {}
    (dp_o, y_o, do_o, post_g_o, dwbd, g["odd_c_scale"], ddww, g["odd_d_dw_b"], g["odd_d_ln_g"], g["odd_d_ln_b"],
     dwpw, g["odd_d_pw_b"], dkv_o) = _odd_bwd(dres, o_o, p_o, conv_o, *odd_args)
    g["odd_post_g"] = post_g_o
    g["odd_d_dw_w"] = ddww[:CONF_K]
    dkv_o = dkv_o.astype(BF16)
    deps = ex.send("odd_rest", {
        "odd_w_out": _tn_matmul(y_o, do_o, tmc=MIX, tk=tk, out_dtype=BF16, name="dw_out_odd"),
        "odd_w_kv": _tn_matmul(memn_o, dkv_o, tmc=D, tk=N_MEM, out_dtype=BF16, name="dw_kv_odd"),
        "odd_d_pw_w": dwpw.astype(BF16), "loss": loss,
        "odd_c_wgrp": jnp.concatenate([dwbd[i * GRP:(i + 1) * GRP, i * GRP:(i + 1) * GRP] for i in range(NH)])})
    deps = ex.send("odd_in", {"odd_w_in": _tn_matmul(dp_o, h_o, tmc=_pick_rows(ODD_IN), tk=tk, out_dtype=BF16,
                                                     name="dw_in_odd", deps=deps)})
    dx1, g["odd_pre_g"] = _nt_matmul_rms_bwd(dp_o, w["odd_w_in"], x1, w["odd_pre_g"], dres, tm=min(256, t),
                                             name="dx_odd", transposed=True, deps=deps)
    _, g["odd_mem_g"] = _nt_matmul_rms_bwd(dkv_o, w["odd_w_kv"], mem, w["odd_mem_g"], zeros, tm=N_MEM,
                                           name="dmem_odd")

    (dp_e, y_e, do_e, post_g_e, dws, dbs, ln_g_e, ln_b_e, dwc, dkv_e) = _even_bwd(
        dx1, o_e, p_e, *even_args, wst, bmap, wc, kv_e, w["even_w_out"], w["even_post_g"])
    g["even_b_conv"] = dwc[:3]
    dkv_e = dkv_e.astype(BF16)
    deps = ex.send("even_rest", {
        "even_w_out": _tn_matmul(y_e, do_e, tmc=MIX, tk=tk, out_dtype=BF16, name="dw_out_even"),
        "even_w_kv": _tn_matmul(memn_e, dkv_e, tmc=D, tk=N_MEM, out_dtype=BF16, name="dw_kv_even"),
        "even_a_ln_g": ln_g_e, "even_a_ln_b": ln_b_e,
        "even_a_ws": dws.reshape(NH * CHUNK, CHUNK), "even_a_bs": dbs[:, ::HD].T})
    g["even_w_in"] = _tn_matmul(dp_e, h_e, tmc=_pick_rows(EVEN_IN), tk=tk, out_dtype=BF16, name="dw_in_even",
                                deps=deps)
    deps = ex.send("even_in", g)
    grad_x, pre_g_e = _nt_matmul_rms_bwd(dp_e, w["even_w_in"], x, w["even_pre_g"], dx1, tm=min(256, t),
                                         name="dx_even", transposed=True, deps=deps)
    _, mem_g_e = _nt_matmul_rms_bwd(dkv_e, w["even_w_kv"], mem, w["even_mem_g"], zeros, tm=N_MEM, name="dmem_even",
                                    deps=(grad_x,))
    ex.send("even_gains", {"even_pre_g": pre_g_e, "even_mem_g": mem_g_e, "even_post_g": post_g_e})
    return grad_x, mem_g_e


def _place():
    return lax.axis_index("x"), lax.axis_index("y"), lax.axis_index("c")


def _index(px, py, pc):
    return 4 * px + 2 * py + pc


_COPIES = N_DEV - 1


def _all_gather(arrs, name):
    n = len(arrs)

    def body(*refs):
        ins, outs = refs[:n], refs[n:2 * n]
        send_sems, recv_sems, local_sems = refs[2 * n:]
        x, y, c = _place()
        me, sibling = (x, y, c), (x, y, 1 - c)
        chips = [(1 - x, y), (x, 1 - y), (1 - x, 1 - y)]

        def copy(a, k, block, to, src=None):
            dst = outs[a].at[_index(*block)]
            return pltpu.make_async_remote_copy(
                src_ref=dst if src is None else src, dst_ref=dst, send_sem=send_sems.at[a * _COPIES + k],
                recv_sem=recv_sems.at[a * _COPIES + k], device_id=to, device_id_type=MESH)

        mine = [pltpu.make_async_copy(ins[a], outs[a].at[_index(*me)], local_sems.at[a]) for a in range(n)]
        first = []
        for a in range(n):
            mine[a].start()
            first.append(copy(a, 0, me, sibling, src=ins[a]))
            first += [copy(a, 1 + j, me, (*chip, c), src=ins[a]) for j, chip in enumerate(chips)]
        for cp in first:
            cp.start()
        passed = []
        for j, chip in enumerate(chips):
            for a in range(n):
                copy(a, 1 + j, (*chip, c), me).wait_recv()
                passed.append(copy(a, 4 + j, (*chip, c), sibling))
                passed[-1].start()
        for a in range(n):
            copy(a, 0, sibling, me).wait_recv()
            for j, chip in enumerate(chips):
                copy(a, 4 + j, (*chip, 1 - c), me).wait_recv()
        for cp in first + passed:
            cp.wait_send()
        for cp in mine:
            cp.wait()

    return pl.pallas_call(
        body, name=name, in_specs=[_ANY] * n, out_specs=[_ANY] * n,
        out_shape=[S((N_DEV,) + a.shape, a.dtype) for a in arrs],
        scratch_shapes=[pltpu.SemaphoreType.DMA((n * _COPIES,)), pltpu.SemaphoreType.DMA((n * _COPIES,)),
                        pltpu.SemaphoreType.DMA((n,))],
    )(*arrs)


_HBM = pl.BlockSpec(memory_space=pltpu.HBM)
_SEM = pl.BlockSpec(memory_space=pltpu.SEMAPHORE)
_EFFECT = pltpu.SideEffectType.DATAFLOW_SIDE_EFFECTING


_ALL_FLIPS = [(k >> 2 & 1, k >> 1 & 1, k & 1) for k in range(1, N_DEV)]
_CHIP_FLIPS = [(1, 0, 0), (0, 1, 0), (1, 1, 0)]
_FLIPS = {"gather": _ALL_FLIPS, "scatter": _ALL_FLIPS, "gather_chips": [(0, 0, 1)] + _CHIP_FLIPS,
          "scatter_chips": _CHIP_FLIPS}


def _landing_shape(kind, a):
    return (N_DEV,) + a.shape if kind.startswith("gather") else a.shape


def _exchange_copies(kinds, srcs, lands, send_sems, recv_sems, local_sems, arriving):
    x, y, c = _place()
    mine = _index(x, y, c)
    remote, local = [], []
    for a, kind in enumerate(kinds):
        by_chip = kind == "scatter_chips"
        here = 2 * x + y if by_chip else mine
        own = srcs[a] if kind.startswith("gather") else srcs[a].at[here]
        local.append(pltpu.make_async_copy(own, lands[a].at[here], local_sems.at[a]))
        for k, (fx, fy, fc) in enumerate(_FLIPS[kind]):
            peer = (1 - x if fx else x, 1 - y if fy else y, 1 - c if fc else c)
            there = 2 * peer[0] + peer[1] if by_chip else _index(*peer)
            remote.append(pltpu.make_async_remote_copy(
                src_ref=srcs[a] if kind.startswith("gather") else srcs[a].at[there],
                dst_ref=lands[a].at[there if arriving else here],
                send_sem=send_sems.at[a * _COPIES + k], recv_sem=recv_sems.at[a * _COPIES + k],
                device_id=peer, device_id_type=MESH))
    return remote, local


def _exchange_start(items, name, deps=()):
    kinds = [kind for kind, _ in items]
    srcs = [a for _, a in items]
    n = len(items)
    lands = [lax.empty(_landing_shape(kind, a), a.dtype) for kind, a in items]

    def body(*refs):
        send_sems, recv_sems, local_sems = refs[2 * n + len(deps):2 * n + len(deps) + 3]
        remote, local = _exchange_copies(kinds, refs[:n], refs[n:2 * n], send_sems, recv_sems, local_sems, False)
        for cp in local + remote:
            cp.start()
        refs[-1][...] = jnp.zeros_like(refs[-1])

    held = [pltpu.HBM(a.shape, a.dtype) for a in srcs + lands]
    res = pl.pallas_call(
        body, name=name,
        out_shape=(pltpu.SemaphoreType.DMA((n * _COPIES,)), pltpu.SemaphoreType.DMA((n * _COPIES,)),
                   pltpu.SemaphoreType.DMA((n,)), *held, S((8, 128), F32)),
        in_specs=[_HBM] * (2 * n) + [_ANY] * len(deps),
        out_specs=(_SEM, _SEM, _SEM, *[_HBM] * (2 * n), _whole()),
        input_output_aliases={i: 3 + i for i in range(2 * n)},
        compiler_params=pltpu.CompilerParams(has_side_effects=_EFFECT),
    )(*[pltpu.with_memory_space_constraint(a, pltpu.HBM) for a in srcs + lands], *deps)
    return (kinds, res[:3], res[3:3 + 2 * n]), res[-1]


def _exchange_wait(handle, after, name):
    kinds, sems, held = handle
    n = len(kinds)

    def body(*refs):
        send_sems, recv_sems, local_sems = refs[2 * n:2 * n + 3]
        remote, local = _exchange_copies(kinds, refs[:n], refs[n:2 * n], send_sems, recv_sems, local_sems, True)
        for cp in remote:
            cp.wait_send()
            cp.wait_recv()
        for cp in local:
            cp.wait()

    res = pl.pallas_call(
        body, name=name, out_shape=[pltpu.HBM(a.shape, a.dtype) for a in held],
        in_specs=[_HBM] * (2 * n) + [_SEM] * 3 + [_ANY] * len(after), out_specs=[_HBM] * (2 * n),
        input_output_aliases={i: i for i in range(2 * n)},
        compiler_params=pltpu.CompilerParams(has_side_effects=_EFFECT),
    )(*held, *sems, *after)
    return res[n:]


_CHIPS = [(0, 0), (0, 1), (1, 0), (1, 1)]
_N_CHIPS = len(_CHIPS)


def _sibling_forward(lands, name):
    n = len(lands)

    def body(*refs):
        ins, outs = refs[:n], refs[n:2 * n]
        send_sems, recv_sems = refs[2 * n:]
        x, y, c = _place()
        sent, arriving = [], []
        for a in range(n):
            for j, (fx, fy, _) in enumerate(_CHIP_FLIPS):
                chip = (1 - x if fx else x, 1 - y if fy else y)
                sems = dict(send_sem=send_sems.at[a * 3 + j], recv_sem=recv_sems.at[a * 3 + j],
                            device_id=(x, y, 1 - c), device_id_type=MESH)
                mine, theirs = _index(*chip, c), _index(*chip, 1 - c)
                sent.append(pltpu.make_async_remote_copy(src_ref=ins[a].at[mine], dst_ref=outs[a].at[mine], **sems))
                arriving.append(pltpu.make_async_remote_copy(src_ref=ins[a].at[theirs], dst_ref=outs[a].at[theirs],
                                                             **sems))
        for cp in sent:
            cp.start()
        for cp in sent:
            cp.wait_send()
        for cp in arriving:
            cp.wait_recv()

    return pl.pallas_call(
        body, name=name, in_specs=[_ANY] * n, out_specs=[_ANY] * n,
        out_shape=[S(a.shape, a.dtype) for a in lands], input_output_aliases={a: a for a in range(n)},
        scratch_shapes=[pltpu.SemaphoreType.DMA((3 * n,)), pltpu.SemaphoreType.DMA((3 * n,))],
    )(*lands)


def _sibling_swap(arrs, name):
    n = len(arrs)

    def body(*refs):
        ins, outs = refs[:n], refs[n:2 * n]
        send_sems, recv_sems = refs[2 * n:]
        x, y, c = _place()
        copies = []
        for a in range(n):
            for q, chip in enumerate(_CHIPS):
                copies.append(pltpu.make_async_remote_copy(
                    src_ref=ins[a].at[_index(*chip, 1 - c)], dst_ref=outs[a].at[q],
                    send_sem=send_sems.at[a * _N_CHIPS + q], recv_sem=recv_sems.at[a * _N_CHIPS + q],
                    device_id=(x, y, 1 - c), device_id_type=MESH))
        for cp in copies:
            cp.start()
        for cp in copies:
            cp.wait_send()
            cp.wait_recv()

    return pl.pallas_call(
        body, name=name, in_specs=[_ANY] * n, out_specs=[_ANY] * n,
        out_shape=[S((_N_CHIPS,) + a.shape[1:], a.dtype) for a in arrs],
        scratch_shapes=[pltpu.SemaphoreType.DMA((_N_CHIPS * n,)), pltpu.SemaphoreType.DMA((_N_CHIPS * n,))],
    )(*arrs)


def _add_partials(mine, theirs, *, tr, name):
    _, r, c = mine.shape

    def body(mine_ref, theirs_ref, out_ref):
        core = lax.axis_index("c")
        own = jnp.where(core == 0, mine_ref[0].astype(F32), mine_ref[1].astype(F32))
        out_ref[0] = (own + theirs_ref[0].astype(F32)).astype(out_ref.dtype)

    return pl.pallas_call(
        body, name=name, grid=(_N_CHIPS, r // tr),
        in_specs=[pl.BlockSpec((2, tr, c), lambda q, i: (q, i, 0)), pl.BlockSpec((1, tr, c), lambda q, i: (q, i, 0))],
        out_specs=pl.BlockSpec((1, tr, c), lambda q, i: (q, i, 0)),
        out_shape=S((_N_CHIPS, r, c), mine.dtype),
        compiler_params=_params(("arbitrary", "arbitrary")),
    )(mine, theirs)


def _adamw(w, g, m, v):
    m = ADAM_B1 * m + (1.0 - ADAM_B1) * g
    v = ADAM_B2 * v + (1.0 - ADAM_B2) * (g * g)
    m_hat = m / (1.0 - ADAM_B1 ** ADAM_STEP)
    v_hat = v / (1.0 - ADAM_B2 ** ADAM_STEP)
    return -ADAM_LR * (m_hat / (jnp.sqrt(v_hat) + ADAM_EPS) + ADAM_WD * w), m, v


def _sum_devices(ref, rows):
    total = ref[0, rows, :].astype(F32)
    for s in range(1, ref.shape[0]):
        total = total + ref[s, rows, :].astype(F32)
    return total


def _adam_big(recv, w, m, v, *, tr, name):
    r, c = w.shape

    def body(recv_ref, w_ref, m_ref, v_ref, g_ref, d_ref, m2_ref, v2_ref):
        g = _sum_devices(recv_ref, slice(None))
        g_ref[...] = g
        d_ref[...], m2_ref[...], v2_ref[...] = _adamw(w_ref[...], g, m_ref[...], v_ref[...])

    blk = pl.BlockSpec((tr, c), lambda i: (i, 0))
    return pl.pallas_call(
        body, name=name, grid=(r // tr,),
        in_specs=[pl.BlockSpec((recv.shape[0], tr, c), lambda i: (0, i, 0)), blk, blk, blk],
        out_specs=[blk] * 4, out_shape=[S((r, c), F32)] * 4,
        compiler_params=_params(("arbitrary",)),
    )(recv, w, m, v)


_REPLICATED = {"even_pre_g": (0, 0, 1), "even_mem_g": (0, 8, 1), "even_post_g": (0, 16, 1),
               "even_a_ln_g": (1, 0, 1), "even_a_ln_b": (1, 8, 1),
               "even_a_ws": (2, 0, NH * CHUNK), "even_a_bs": (2, NH * CHUNK, NH),
               "odd_c_wgrp": (3, 0, NH * GRP)}
_SHARDED = {"odd_pre_g": (4, 0, 1), "odd_mem_g": (4, 8, 1), "odd_post_g": (4, 16, 1),
            "even_b_conv": (5, 0, 3), "odd_c_scale": (5, 8, 1), "odd_d_dw_w": (5, 16, CONF_K),
            "odd_d_dw_b": (5, 48, 1), "odd_d_ln_g": (5, 56, 1), "odd_d_ln_b": (5, 64, 1), "odd_d_pw_b": (5, 72, 1)}
_SMALL = {**_REPLICATED, **_SHARDED}
_SMALL_ROWS = {0: 24, 1: 16, 2: NH * CHUNK + 8, 3: NH * GRP, 4: 24, 5: 80}


def _adam_small(sources, wmv):
    names = list(_SMALL)
    ns = len(sources)

    def body(*refs):
        src = refs[:ns]
        ins = refs[ns:ns + 3 * len(names)]
        outs = refs[ns + 3 * len(names):]
        outs[-1][...] = _sum_devices(src[-1], slice(0, 1))
        for i, nm in enumerate(names):
            a, row0, rows = _SMALL[nm]
            g = _sum_devices(src[a], slice(row0, row0 + rows))
            w_ref, m_ref, v_ref = ins[3 * i:3 * i + 3]
            g_ref, d_ref, m2_ref, v2_ref = outs[4 * i:4 * i + 4]
            g_ref[...] = g
            d_ref[...], m2_ref[...], v2_ref[...] = _adamw(w_ref[...], g, m_ref[...], v_ref[...])

    flat = [t for nm in names for t in wmv[nm]]
    out_shape = [S(wmv[nm][0].shape, F32) for nm in names for _ in range(4)] + [S((1, HD), F32)]
    res = pl.pallas_call(
        body, name="adam_small", in_specs=[_whole()] * (ns + len(flat)), out_specs=[_whole()] * len(out_shape),
        out_shape=out_shape, compiler_params=_params(),
    )(*sources, *flat)
    return {nm: tuple(res[4 * i:4 * i + 4]) for i, nm in enumerate(names)}, res[-1]


_WEIGHTS = ["even_pre_g", "even_w_in", "even_a_ln_g", "even_a_ln_b", "even_a_ws", "even_a_bs", "even_b_conv",
            "even_mem_g", "even_w_kv", "even_w_out", "even_post_g", "odd_pre_g", "odd_w_in", "odd_c_wgrp",
            "odd_c_scale", "odd_d_dw_w", "odd_d_dw_b", "odd_d_ln_g", "odd_d_ln_b", "odd_d_pw_w", "odd_d_pw_b",
            "odd_mem_g", "odd_w_kv", "odd_w_out", "odd_post_g"]
_TRANSPOSED = ["even_w_in", "odd_w_in"]
_BIG = _TRANSPOSED + ["even_w_kv", "even_w_out", "odd_w_kv", "odd_w_out", "odd_d_pw_w"]
_BIG_TILE_ROWS = {"even_w_in": 400, "odd_w_in": 304, "even_w_kv": 128, "even_w_out": 128, "odd_w_kv": 128,
                  "odd_w_out": 128, "odd_d_pw_w": 96}


def _view2d(a, transposed):
    a = a[0]
    if a.ndim == 1:
        return a[None]
    if transposed:
        return a.T
    return a.reshape(-1, a.shape[-1])


def _rows8(a):
    return _pad_rows(a, -(-a.shape[0] // 8) * 8)


def _pack_rows(parts):
    return jnp.concatenate([_rows8(p) for p in parts], axis=0)


def _unshard_cols(a):
    return jnp.transpose(a, (1, 0, 2)).reshape(a.shape[1], N_DEV * a.shape[2])


def _shard_cols(a):
    return jnp.transpose(a.reshape(a.shape[0], N_DEV, a.shape[1] // N_DEV), (1, 0, 2))


def _rows_of(a):
    return a.reshape(-1, a.shape[-1])


_GROUPS = {"odd_rest": (["odd_w_out", "odd_w_kv", "odd_d_pw_w"], [3], []),
           "odd_in": (["odd_w_in"], [], []),
           "even_rest": (["even_w_out", "even_w_kv"], [1, 2], []),
           "even_in": (["even_w_in"], [], [4, 5]),
           "even_gains": ([], [0], [])}


_TWO_LEVEL = ("even_in",)


class _MeshExchange:
    def __init__(self, shard):
        self.shard = shard
        self.handles = {}

    def first(self):
        shard = self.shard
        packs = [_pack_rows([shard[nm] for nm in _SHARDED if _SHARDED[nm][0] == a]) for a in (4, 5)]
        w_in, p128, p96 = _all_gather([shard["even_w_in"].astype(BF16)] + packs, "gather_first")
        w = {nm: shard[nm] for nm in _REPLICATED}
        w["even_a_ws"] = w["even_a_ws"].reshape(NH, CHUNK, CHUNK)
        w["odd_c_wgrp"] = w["odd_c_wgrp"].reshape(NH, GRP, GRP)
        w["even_w_in"] = _rows_of(w_in)
        full_packs = {4: _unshard_cols(p128), 5: _unshard_cols(p96)}
        for nm, (a, row0, rows) in _SHARDED.items():
            w[nm] = full_packs[a][row0:row0 + rows]
        later = lambda names: [("gather_chips", shard[nm].astype(BF16)) for nm in names]
        self.handles["w_even"], token = _exchange_start(later(["even_w_kv", "even_w_out"]), "gather_even_start",
                                                        deps=(w_in,))
        self.handles["w_odd"], token = _exchange_start(later(["odd_w_in", "odd_w_kv", "odd_w_out", "odd_d_pw_w"]),
                                                       "gather_odd_start", deps=(token,))
        return w, (token,)

    def even_rest(self, after):
        landed = _exchange_wait(self.handles.pop("w_even"), (after,), "gather_even_wait")
        kv, out = _sibling_forward(landed, "forward_even")
        return {"even_w_kv": _rows_of(kv), "even_w_out": _rows_of(out)}

    def odd(self, after):
        landed = _exchange_wait(self.handles.pop("w_odd"), (after,), "gather_odd_wait")
        w_in, kv, out, pw = _sibling_forward(landed, "forward_odd")
        return {"odd_w_in": _rows_of(w_in), "odd_w_kv": _rows_of(kv), "odd_w_out": _rows_of(out),
                "odd_d_pw_w": _rows_of(pw)}

    def send(self, group, g):
        big, replicated, sharded = _GROUPS[group]
        by_owner = [g[nm].reshape(N_DEV, -1, g[nm].shape[-1]) for nm in big]
        if group in _TWO_LEVEL:
            theirs = _sibling_swap(by_owner, "swap_" + group)
            items = [("scatter_chips", _add_partials(a, b, tr=_BIG_TILE_ROWS[nm], name="chip_sum_" + nm))
                     for nm, a, b in zip(big, by_owner, theirs)]
        else:
            items = [("scatter", a) for a in by_owner]
        items += [("gather", _pack_rows([g[nm] for nm in _REPLICATED if _REPLICATED[nm][0] == a]))
                  for a in replicated]
        items += [("scatter", _shard_cols(_pack_rows([g[nm] for nm in _SHARDED if _SHARDED[nm][0] == a])))
                  for a in sharded]
        if group == "odd_rest":
            items.append(("gather", _rows8(g["loss"])))
        self.handles[group], token = _exchange_start(items, "send_" + group + "_start")
        return (token,)

    def receive(self, group, after):
        return _exchange_wait(self.handles.pop(group), (after,), "send_" + group + "_wait")


def kernel(x, mem, even_pre_g, even_w_in, even_a_ln_g, even_a_ln_b, even_a_ws, even_a_bs, even_b_conv, even_mem_g, even_w_kv, even_w_out, even_post_g, odd_pre_g, odd_w_in, odd_c_wgrp, odd_c_scale, odd_d_dw_w, odd_d_dw_b, odd_d_ln_g, odd_d_ln_b, odd_d_pw_w, odd_d_pw_b, odd_mem_g, odd_w_kv, odd_w_out, odd_post_g, loss_target, m_even_pre_g, m_even_w_in, m_even_a_ln_g, m_even_a_ln_b, m_even_a_ws, m_even_a_bs, m_even_b_conv, m_even_mem_g, m_even_w_kv, m_even_w_out, m_even_post_g, m_odd_pre_g, m_odd_w_in, m_odd_c_wgrp, m_odd_c_scale, m_odd_d_dw_w, m_odd_d_dw_b, m_odd_d_ln_g, m_odd_d_ln_b, m_odd_d_pw_w, m_odd_d_pw_b, m_odd_mem_g, m_odd_w_kv, m_odd_w_out, m_odd_post_g, v_even_pre_g, v_even_w_in, v_even_a_ln_g, v_even_a_ln_b, v_even_a_ws, v_even_a_bs, v_even_b_conv, v_even_mem_g, v_even_w_kv, v_even_w_out, v_even_post_g, v_odd_pre_g, v_odd_w_in, v_odd_c_wgrp, v_odd_c_scale, v_odd_d_dw_w, v_odd_d_dw_b, v_odd_d_ln_g, v_odd_d_ln_b, v_odd_d_pw_w, v_odd_d_pw_b, v_odd_mem_g, v_odd_w_kv, v_odd_w_out, v_odd_post_g):
    given = dict(locals())
    view = lambda nm, kind: _view2d(given[kind + nm], nm in _TRANSPOSED)
    shard = {nm: view(nm, "") for nm in _WEIGHTS}
    wmv = {nm: (shard[nm], view(nm, "m_"), view(nm, "v_")) for nm in _WEIGHTS}

    ex = _MeshExchange(shard)
    grad_x, last = _step(x[0], mem[0], loss_target[0], ex)

    res = {}

    def update(group, after):
        names = _GROUPS[group][0]
        landed = ex.receive(group, after)
        for nm, recv in zip(names, landed):
            res[nm] = _adam_big(recv, *wmv[nm], tr=_BIG_TILE_ROWS[nm], name="adam_" + nm)
        return landed[len(names):]

    c192, losses = update("odd_rest", last)
    update("odd_in", res["odd_d_pw_w"][0])
    c768, c128 = update("even_rest", res["odd_w_in"][0])
    a128, a96 = update("even_in", res["even_w_kv"][0])
    (c1024,) = update("even_gains", res["even_w_in"][0])
    small, loss = _adam_small([c1024, c768, c128, c192, a128, a96, losses], {nm: wmv[nm] for nm in _SMALL})
    res.update(small)
    total = loss[0, 0]
    back = lambda nm, a: (a.T if nm in _TRANSPOSED else a).reshape(given[nm].shape)
    outs = [[back(nm, res[nm][i]) for nm in _WEIGHTS] for i in range(4)]
    return (total, grad_x[None], *outs[0], *outs[1], *outs[2], *outs[3])
```

```python
import functools

import jax
import jax.numpy as jnp
from jax import lax
from jax.experimental import pallas as pl
from jax.experimental.pallas import tpu as pltpu

F32 = jnp.float32
BF16 = jnp.bfloat16
S = jax.ShapeDtypeStruct
MESH = pl.DeviceIdType.MESH
AXES = ("x", "y", "c")
N_DEV = 8

D = 1024
BW = 768
XA = 512
HD = 128
NH = 4
MIX = 2048
CHUNK = 128
GRP = 192
N_MEM = 256
CONF_K = 31
EPS = 1e-6
HALO = 32
POOL_WINDOWS = (2, 4, 8, 16)
TM_FWD_EVEN = 512
TM_FWD_ODD = 256
TM_BWD_EVEN = 256
TM_BWD_ODD = 256
RB = 16

E_U, E_V, E_BG, E_CG, E_XIN, E_Q, E_GATE = 0, 768, 1536, 2304, 3072, 3840, 4352
EVEN_IN = 6400
O_ZC, O_GA, O_GB, O_Q, O_GATE = 0, 768, 1536, 2304, 2816
ODD_IN = 4864

ADAM_LR, ADAM_B1, ADAM_B2, ADAM_EPS, ADAM_WD, ADAM_STEP = 0.001, 0.9, 0.999, 1e-08, 0.01, 10

VMEM_LIMIT_V7X = 56 * 1024 * 1024
VMEM_LIMIT_ODD_BWD_V7X = 62 * 1024 * 1024


def _params(sem=None):
    return pltpu.CompilerParams(dimension_semantics=sem, vmem_limit_bytes=VMEM_LIMIT_V7X)


def _dot(a, b):
    return jnp.dot(a, b, preferred_element_type=F32)


def _dot_nt(a, b):
    return lax.dot_general(a, b, (((1,), (1,)), ((), ())), preferred_element_type=F32)


def _dot_tn(a, b):
    return lax.dot_general(a, b, (((0,), (0,)), ((), ())), preferred_element_type=F32)


def _sigmoid(z):
    return 1.0 / (1.0 + jnp.exp(-z))


def _rowmean(a):
    return jnp.mean(a, axis=-1, keepdims=True)


def _colsum(a):
    return jnp.sum(a, axis=0, keepdims=True)


def _ln_stats(v):
    mu = _rowmean(v)
    vc = v - mu
    rs = lax.rsqrt(_rowmean(vc * vc) + EPS)
    return vc * rs, rs


def _ln_bwd(dn, vh, rs, g):
    dvh = dn * g
    return rs * (dvh - _rowmean(dvh) - vh * _rowmean(dvh * vh))


def _group_masks():
    col = lax.broadcasted_iota(jnp.int32, (1, BW), 1)
    return [((col >= GRP * h) & (col < GRP * (h + 1))).astype(F32) for h in range(NH)]


def _full(shape):
    nd = len(shape)
    return pl.BlockSpec(shape, lambda *_: (0,) * nd)


def _whole():
    return pl.BlockSpec(memory_space=pltpu.VMEM)


_ANY = pl.BlockSpec(memory_space=pl.ANY)


def _after(body, n_in, deps):
    def ordered(*refs):
        return body(*refs[:n_in], *refs[n_in + len(deps):])
    return ordered


def _rms_matmul(x, g, w, *, tm, name, transposed=False, deps=()):
    t, d = x.shape
    n = w.shape[0] if transposed else w.shape[1]

    def body(x_ref, g_ref, w_ref, p_ref, h_ref):
        xv = x_ref[...]
        r = lax.rsqrt(_rowmean(xv * xv) + EPS)
        h = (xv * r * g_ref[...]).astype(BF16)
        h_ref[...] = h
        p_ref[...] = _dot_nt(h, w_ref[...]) if transposed else _dot(h, w_ref[...])

    return pl.pallas_call(
        _after(body, 3, deps), name=name, grid=(t // tm,),
        in_specs=[pl.BlockSpec((tm, d), lambda i: (i, 0)), _whole(), _whole()] + [_ANY] * len(deps),
        out_specs=[pl.BlockSpec((tm, n), lambda i: (i, 0)), pl.BlockSpec((tm, d), lambda i: (i, 0))],
        out_shape=[S((t, n), F32), S((t, d), BF16)],
        compiler_params=_params(("arbitrary",)),
    )(x, g, w, *deps)


def _nt_matmul_rms_bwd(dp, w, x, g, dres, *, tm, name, transposed=False, deps=()):
    t, n = dp.shape
    d = x.shape[1]

    def body(dp_ref, w_ref, x_ref, g_ref, dres_ref, dx_ref, dg_ref):
        @pl.when(pl.program_id(0) == 0)
        def _():
            dg_ref[...] = jnp.zeros_like(dg_ref)

        dh = _dot(dp_ref[...], w_ref[...]) if transposed else _dot_nt(dp_ref[...], w_ref[...])
        xv = x_ref[...]
        r = lax.rsqrt(_rowmean(xv * xv) + EPS)
        xh = xv * r
        dg_ref[...] += _colsum(dh * xh)
        dxh = dh * g_ref[...]
        dx_ref[...] = dres_ref[...] + r * (dxh - xh * _rowmean(dxh * xh))

    return pl.pallas_call(
        _after(body, 5, deps), name=name, grid=(t // tm,),
        in_specs=[pl.BlockSpec((tm, n), lambda i: (i, 0)), _whole(), pl.BlockSpec((tm, d), lambda i: (i, 0)),
                  _whole(), pl.BlockSpec((tm, d), lambda i: (i, 0))] + [_ANY] * len(deps),
        out_specs=[pl.BlockSpec((tm, d), lambda i: (i, 0)), pl.BlockSpec((1, d), lambda i: (0, 0))],
        out_shape=[S((t, d), F32), S((1, d), F32)],
        compiler_params=_params(("arbitrary",)),
    )(dp, w, x, g, dres, *deps)


def _tn_matmul(a, b, *, tmc, tk, out_dtype, name, deps=()):
    t, m = a.shape
    n = b.shape[1]
    nk = t // tk

    def body(a_ref, b_ref, o_ref, acc_ref):
        k = pl.program_id(1)

        @pl.when(k == 0)
        def _():
            acc_ref[...] = jnp.zeros_like(acc_ref)

        acc_ref[...] += _dot_tn(a_ref[...], b_ref[...])

        @pl.when(k == nk - 1)
        def _():
            o_ref[...] = acc_ref[...].astype(out_dtype)

    return pl.pallas_call(
        _after(body, 2, deps), name=name, grid=(m // tmc, nk),
        in_specs=[pl.BlockSpec((tk, tmc), lambda j, k: (k, j)), pl.BlockSpec((tk, n), lambda j, k: (k, 0))]
        + [_ANY] * len(deps),
        out_specs=pl.BlockSpec((tmc, n), lambda j, k: (j, 0)),
        out_shape=S((m, n), out_dtype),
        scratch_shapes=[pltpu.VMEM((tmc, n), F32)],
        compiler_params=_params(("arbitrary", "arbitrary")),
    )(a, b, *deps)


def _silu_parts(gt):
    sg = _sigmoid(gt)
    return gt * sg, sg * (1.0 + gt * (1.0 - sg))


def _attn_head(q_b, k_b, v_b):
    s = _dot_nt(q_b, k_b) * (HD ** -0.5)
    e = jnp.exp(s - jnp.max(s, axis=-1, keepdims=True))
    prob = e / jnp.sum(e, axis=-1, keepdims=True)
    return prob, _dot(prob.astype(BF16), v_b)


def _rms_residual(x, o, g):
    r = lax.rsqrt(_rowmean(o * o) + EPS)
    return x + o * r * g


def _rms_post_bwd(dres, o, g):
    r = lax.rsqrt(_rowmean(o * o) + EPS)
    oh = o * r
    doh = dres * g
    return r * (doh - oh * _rowmean(doh * oh)), _colsum(dres * oh)


LANE = 128
_TILE_GROUPS = [sorted({LANE * j // GRP, (LANE * j + LANE - 1) // GRP}) for j in range(BW // LANE)]


def _tile(j):
    return slice(LANE * j, LANE * (j + 1))


def _low_lanes():
    return lax.broadcasted_iota(jnp.int32, (1, LANE), 1) < GRP - LANE


def _by_group(fn):
    tiles = []
    for j, groups in enumerate(_TILE_GROUPS):
        if len(groups) == 1:
            tiles.append(fn(groups[0], j))
        else:
            tiles.append(jnp.where(_low_lanes(), fn(groups[0], j), fn(groups[1], j)))
    return jnp.concatenate(tiles, axis=1)


def _sgu_chunk(vn_b, ws_ref, bmap_ref):
    return bmap_ref[...] + _by_group(lambda h, j: _dot(ws_ref[h], vn_b[:, _tile(j)]))


def _shift_copies(buf, sh):
    n = buf.shape[0] - 8
    for b in range(1, 8):
        sh[b - 1, pl.ds(0, n), :] = buf[pl.ds(b, n), :]


def _loop_rows(rows, step, fn, carry=0, unrolled=True):
    if unrolled:
        for r0 in range(0, rows, step):
            carry = fn(r0, carry)
        return carry

    def body(j, c):
        return fn(pl.multiple_of(j * step, step), c)
    return lax.fori_loop(0, rows // step, body, carry)


def _rows_at(buf, sh, r0, off):
    b = off % 8
    if b == 0 or sh is None:
        return buf[pl.ds(r0 + off, 32), :]
    return sh[b - 1, pl.ds(r0 + (off - b), 32), :]


def _tap_sum(buf, sh, w_ref, r0, taps, causal):
    acc = None
    for k in range(taps):
        off = HALO - (taps - 1 - k) if causal else taps - 1 - k
        term = w_ref[k:k + 1, :] * _rows_at(buf, sh, r0, off)
        acc = term if acc is None else acc + term
    return acc


def _fold8(a):
    return a[0:8] + a[8:16] + a[16:24] + a[24:32]


def _tap_grads(dv, buf, sh, acc_ref, r0, taps):
    for k in range(taps):
        acc_ref[k * 8:(k + 1) * 8, :] += _fold8(dv * _rows_at(buf, sh, r0, HALO - (taps - 1 - k)))


def _halo_spec(n, nt, reverse, tm):
    per = tm // HALO
    if reverse:
        return pl.BlockSpec((HALO, n), lambda i: (jnp.maximum((nt - 1 - i) * per - 1, 0), 0))
    return pl.BlockSpec((HALO, n), lambda i: (jnp.maximum(i * per - 1, 0), 0))


def _even_fwd(x, p, lng, lnb, ws, bmap, wc, kv, wout, pg):
    t = x.shape[0]
    tm = min(TM_FWD_EVEN, t)
    nt = t // tm

    def body(x_ref, p_ref, ph_ref, lng_ref, lnb_ref, ws_ref, bmap_ref, wc_ref, kv_ref, wout_ref, pg_ref,
             o_ref, x1_ref, ybuf, cbuf):
        i = pl.program_id(0)
        vh, _ = _ln_stats(p_ref[:, E_V:E_V + BW])
        vn = vh * lng_ref[...] + lnb_ref[...]
        for c in range(tm // CHUNK):
            sl = slice(c * CHUNK, (c + 1) * CHUNK)
            sg = _sgu_chunk(vn[sl].astype(BF16), ws_ref, bmap_ref)
            gate, _ = _silu_parts(p_ref[sl, E_GATE:E_GATE + BW])
            ybuf[sl, 0:BW] = (p_ref[sl, E_U:E_U + BW] * sg * gate).astype(BF16)

        cbuf[0:HALO] = jnp.where(i > 0, ph_ref[:, E_CG:E_CG + BW] * ph_ref[:, E_XIN:E_XIN + BW], 0.0)
        cbuf[HALO:HALO + tm] = p_ref[:, E_CG:E_CG + BW] * p_ref[:, E_XIN:E_XIN + BW]
        for r0 in range(0, tm, 32):
            sl = slice(r0, r0 + 32)
            cv = _tap_sum(cbuf, None, wc_ref, r0, 3, True)
            gate, _ = _silu_parts(p_ref[sl, E_GATE + BW:E_GATE + 2 * BW])
            ybuf[sl, BW:2 * BW] = (p_ref[sl, E_BG:E_BG + BW] * cv * gate).astype(BF16)

        for h in range(NH):
            qs = slice(E_Q + h * HD, E_Q + (h + 1) * HD)
            _, yx = _attn_head(p_ref[:, qs].astype(BF16), kv_ref[:, h * HD:(h + 1) * HD],
                               kv_ref[:, XA + h * HD:XA + (h + 1) * HD])
            gs = slice(E_GATE + 2 * BW + h * HD, E_GATE + 2 * BW + (h + 1) * HD)
            gate, _ = _silu_parts(p_ref[:, gs])
            ybuf[:, 2 * BW + h * HD:2 * BW + (h + 1) * HD] = (yx * gate).astype(BF16)

        o = _dot(ybuf[...], wout_ref[...])
        o_ref[...] = o
        x1_ref[...] = _rms_residual(x_ref[...], o, pg_ref[...])

    tile = lambda n: pl.BlockSpec((tm, n), lambda i: (i, 0))
    return pl.pallas_call(
        body, name="even_fwd", grid=(nt,),
        in_specs=[tile(D), tile(EVEN_IN), _halo_spec(EVEN_IN, nt, False, tm)] + [_whole()] * 8,
        out_specs=[tile(D), tile(D)],
        out_shape=[S((t, D), F32), S((t, D), F32)],
        scratch_shapes=[pltpu.VMEM((tm, MIX), BF16), pltpu.VMEM((tm + HALO, BW), F32)],
        compiler_params=_params(("arbitrary",)),
    )(x, p, p, lng, lnb, ws, bmap, wc, kv, wout, pg)


def _even_bwd(dres, o, p, lng, lnb, ws, wst, bmap, wc, kv, wout, pg):
    t = dres.shape[0]
    tm = min(TM_BWD_EVEN, t)
    nt = t // tm

    def body(dres_ref, o_ref, p_ref, ph_ref, lng_ref, lnb_ref, ws_ref, wst_ref, bmap_ref, wc_ref, kv_ref, wout_ref,
             pg_ref, dp_ref, y_ref, do_ref, dpg_ref, dws_ref, dbs_ref, dlng_ref, dlnb_ref, dwc_ref, dkv_ref,
             dy, cbuf, gbuf, dconv, carry, dvn, dbmap, wacc):
        i = pl.program_id(0)
        ti = nt - 1 - i
        masks = _group_masks()

        @pl.when(i == 0)
        def _():
            for ref in (dpg_ref, dws_ref, dlng_ref, dlnb_ref, dkv_ref, dbmap, wacc):
                ref[...] = jnp.zeros_like(ref)

        do, dpg = _rms_post_bwd(dres_ref[...], o_ref[...], pg_ref[...])
        dpg_ref[...] += dpg
        do_b = do.astype(BF16)
        do_ref[...] = do_b
        dy[...] = _dot_nt(do_b, wout_ref[...])

        vh, rs = _ln_stats(p_ref[:, E_V:E_V + BW])
        vn = vh * lng_ref[...] + lnb_ref[...]
        for c in range(tm // CHUNK):
            sl = slice(c * CHUNK, (c + 1) * CHUNK)
            vn_b = vn[sl].astype(BF16)
            sg = _sgu_chunk(vn_b, ws_ref, bmap_ref)
            u = p_ref[sl, E_U:E_U + BW]
            gate, dgate = _silu_parts(p_ref[sl, E_GATE:E_GATE + BW])
            dyc = dy[sl, 0:BW]
            ya = u * sg
            y_ref[sl, 0:BW] = (ya * gate).astype(BF16)
            dp_ref[sl, E_GATE:E_GATE + BW] = (dyc * ya * dgate).astype(BF16)
            dya = dyc * gate
            dp_ref[sl, E_U:E_U + BW] = (dya * sg).astype(BF16)
            dsg = dya * u
            dbmap[...] += dsg
            dsg_b = dsg.astype(BF16)
            for h in range(NH):
                total = None
                for j, heads in enumerate(_TILE_GROUPS):
                    if h in heads:
                        d_t = dsg_b[:, _tile(j)]
                        if len(heads) == 2:
                            d_t = jnp.where(_low_lanes() == (h == heads[0]), d_t, jnp.zeros_like(d_t))
                        part = _dot_nt(d_t, vn_b[:, _tile(j)])
                        total = part if total is None else total + part
                dws_ref[h] += total
            dvn[sl, :] = _by_group(lambda h, j: _dot(wst_ref[h], dsg_b[:, _tile(j)]))
        dn = dvn[...]
        dlng_ref[...] += _colsum(dn * vh)
        dlnb_ref[...] += _colsum(dn)
        dp_ref[:, E_V:E_V + BW] = _ln_bwd(dn, vh, rs, lng_ref[...]).astype(BF16)

        cbuf[0:HALO] = jnp.where(ti > 0, ph_ref[:, E_CG:E_CG + BW] * ph_ref[:, E_XIN:E_XIN + BW], 0.0)
        cbuf[HALO:HALO + tm] = p_ref[:, E_CG:E_CG + BW] * p_ref[:, E_XIN:E_XIN + BW]
        for r0 in range(0, tm, 32):
            sl = slice(r0, r0 + 32)
            cv = _tap_sum(cbuf, None, wc_ref, r0, 3, True)
            gate, dgate = _silu_parts(p_ref[sl, E_GATE + BW:E_GATE + 2 * BW])
            bg = p_ref[sl, E_BG:E_BG + BW]
            dyc = dy[sl, BW:2 * BW]
            yb = bg * cv
            y_ref[sl, BW:2 * BW] = (yb * gate).astype(BF16)
            dp_ref[sl, E_GATE + BW:E_GATE + 2 * BW] = (dyc * yb * dgate).astype(BF16)
            dyb = dyc * gate
            dp_ref[sl, E_BG:E_BG + BW] = (dyb * cv).astype(BF16)
            dconv[sl, :] = dyb * bg
        gbuf[0:tm] = dconv[...]
        gbuf[tm:tm + HALO] = jnp.where(i > 0, carry[...], 0.0)
        carry[...] = dconv[0:HALO]
        for r0 in range(0, tm, 32):
            sl = slice(r0, r0 + 32)
            _tap_grads(dconv[sl, :], cbuf, None, wacc, r0, 3)
            dc = _tap_sum(gbuf, None, wc_ref, r0, 3, False)
            dp_ref[sl, E_CG:E_CG + BW] = (dc * p_ref[sl, E_XIN:E_XIN + BW]).astype(BF16)
            dp_ref[sl, E_XIN:E_XIN + BW] = (dc * p_ref[sl, E_CG:E_CG + BW]).astype(BF16)

        for h in range(NH):
            qs = slice(E_Q + h * HD, E_Q + (h + 1) * HD)
            ks = slice(h * HD, (h + 1) * HD)
            vs = slice(XA + h * HD, XA + (h + 1) * HD)
            gs = slice(E_GATE + 2 * BW + h * HD, E_GATE + 2 * BW + (h + 1) * HD)
            ys = slice(2 * BW + h * HD, 2 * BW + (h + 1) * HD)
            q_b = p_ref[:, qs].astype(BF16)
            prob, yx = _attn_head(q_b, kv_ref[:, ks], kv_ref[:, vs])
            gate, dgate = _silu_parts(p_ref[:, gs])
            dyc = dy[:, ys]
            y_ref[:, ys] = (yx * gate).astype(BF16)
            dp_ref[:, gs] = (dyc * yx * dgate).astype(BF16)
            dyx_b = (dyc * gate).astype(BF16)
            dprob = _dot_nt(dyx_b, kv_ref[:, vs])
            dkv_ref[:, vs] += _dot_tn(prob.astype(BF16), dyx_b)
            ds_b = (prob * (dprob - jnp.sum(dprob * prob, axis=-1, keepdims=True)) * (HD ** -0.5)).astype(BF16)
            dp_ref[:, qs] = _dot(ds_b, kv_ref[:, ks]).astype(BF16)
            dkv_ref[:, ks] += _dot_tn(ds_b, q_b)

        @pl.when(i == nt - 1)
        def _():
            for h in range(NH):
                dbs_ref[:, h * HD:(h + 1) * HD] = jnp.broadcast_to(
                    jnp.sum(dbmap[...] * masks[h], axis=-1, keepdims=True), (CHUNK, HD))
            for k in range(3):
                dwc_ref[k:k + 1, :] = _colsum(wacc[k * 8:(k + 1) * 8, :])
            dwc_ref[3:8, :] = jnp.zeros((5, BW), F32)
            causal = (lax.broadcasted_iota(jnp.int32, (CHUNK, CHUNK), 0)
                      >= lax.broadcasted_iota(jnp.int32, (CHUNK, CHUNK), 1))
            for h in range(NH):
                dws_ref[h] = jnp.where(causal, dws_ref[h], 0.0)

    rtile = lambda n: pl.BlockSpec((tm, n), lambda i: (nt - 1 - i, 0))
    outs = [S((t, EVEN_IN), BF16), S((t, MIX), BF16), S((t, D), BF16), S((1, D), F32), S((NH, CHUNK, CHUNK), F32),
            S((CHUNK, NH * HD), F32), S((1, BW), F32), S((1, BW), F32), S((8, BW), F32), S((N_MEM, 2 * XA), F32)]
    return pl.pallas_call(
        body, name="even_bwd", grid=(nt,),
        in_specs=[rtile(D), rtile(D), rtile(EVEN_IN), _halo_spec(EVEN_IN, nt, True, tm)] + [_whole()] * 9,
        out_specs=[rtile(EVEN_IN), rtile(MIX), rtile(D)] + [_full(s.shape) for s in outs[3:]],
        out_shape=outs,
        scratch_shapes=[pltpu.VMEM((tm, MIX), F32), pltpu.VMEM((tm + HALO, BW), F32), pltpu.VMEM((tm + HALO, BW), F32),
                        pltpu.VMEM((tm, BW), F32), pltpu.VMEM((HALO, BW), F32), pltpu.VMEM((tm, BW), F32),
                        pltpu.VMEM((CHUNK, BW), F32), pltpu.VMEM((3 * 8, BW), F32)],
        compiler_params=_params(("arbitrary",)),
    )(dres, o, p, p, lng, lnb, ws, wst, bmap, wc, kv, wout, pg)


def _pool_causal_levels(za, zb, zc, zd, tm):
    n = tm + HALO
    zb[pl.ds(8, n - 8), :] = za[pl.ds(8, n - 8), :] + za[pl.ds(7, n - 8), :]
    zc[pl.ds(16, n - 16), :] = zb[pl.ds(16, n - 16), :] + zb[pl.ds(14, n - 16), :]
    zd[pl.ds(24, n - 24), :] = zc[pl.ds(24, n - 24), :] + zc[pl.ds(20, n - 24), :]


def _pool_causal(za, zb, zc, zd, tm):
    _pool_causal_levels(za, zb, zc, zd, tm)
    s16 = zd[pl.ds(HALO, tm), :] + zd[pl.ds(HALO - 8, tm), :]
    return zb[pl.ds(HALO, tm), :], zc[pl.ds(HALO, tm), :], zd[pl.ds(HALO, tm), :], s16


def _pool_anticausal_levels(ea, eb, ec, ed, tm):
    n = tm + HALO
    eb[pl.ds(0, n - 8), :] = ea[pl.ds(0, n - 8), :] + ea[pl.ds(1, n - 8), :]
    ec[pl.ds(0, n - 16), :] = eb[pl.ds(0, n - 16), :] + eb[pl.ds(2, n - 16), :]
    ed[pl.ds(0, n - 24), :] = ec[pl.ds(0, n - 24), :] + ec[pl.ds(4, n - 24), :]


def _pool_weights(t0, masks, rows):
    del masks
    tf = (t0 + lax.broadcasted_iota(jnp.int32, (rows, 1), 0) + 1).astype(F32)
    inv = [jnp.broadcast_to(1.0 / jnp.minimum(tf, float(win)), (rows, LANE)) for win in POOL_WINDOWS]
    return _by_group(lambda g, j: inv[g])


_HALVES = (slice(0, BW // 2), slice(BW // 2, BW))


def _mix4(masks, parts):
    del masks
    return _by_group(lambda g, j: parts[g][:, _tile(j)])


def _odd_fwd(x1, tgt, p, wbd, cscale, dww, dwb, lng, lnb, wpw, pwb, kv, wout, pg):
    t = x1.shape[0]
    tm = min(TM_FWD_ODD, t)
    nt = t // tm

    def body(x_ref, tgt_ref, p_ref, ph_ref, wbd_ref, cs_ref, dww_ref, dwb_ref, lng_ref, lnb_ref, wpw_ref, pwb_ref,
             kv_ref, wout_ref, pg_ref, o_ref, dres_ref, loss_ref, conv_ref, ybuf, za, zb, zc, zd, gbuf, lacc, gsh):
        i = pl.program_id(0)
        masks = _group_masks()

        @pl.when(i == 0)
        def _():
            lacc[...] = jnp.zeros_like(lacc)

        z = p_ref[:, O_ZC:O_ZC + BW]
        za[0:HALO] = jnp.where(i > 0, ph_ref[:, O_ZC:O_ZC + BW], 0.0)
        za[HALO:HALO + tm] = z
        pooled = _mix4(masks, _pool_causal(za, zb, zc, zd, tm)) * _pool_weights(i * tm, masks, tm) - z
        pooled_b = pooled.astype(BF16)
        for hs in _HALVES:
            gate, _ = _silu_parts(p_ref[:, O_GATE + hs.start:O_GATE + hs.stop])
            ybuf[:, hs] = (_dot(pooled_b[:, hs], wbd_ref[hs, hs]) * cs_ref[:, hs] * gate).astype(BF16)

        gbuf[0:HALO] = jnp.where(i > 0, ph_ref[:, O_GA:O_GA + BW] * _sigmoid(ph_ref[:, O_GB:O_GB + BW]), 0.0)
        gbuf[HALO:HALO + tm] = p_ref[:, O_GA:O_GA + BW] * _sigmoid(p_ref[:, O_GB:O_GB + BW])
        _shift_copies(gbuf, gsh)
        def conv_rows(r0, carry):
            conv_ref[pl.ds(r0, 32), :] = _tap_sum(gbuf, gsh, dww_ref, r0, CONF_K, True) + dwb_ref[...]
            return carry

        _loop_rows(tm, 32, conv_rows)
        zh, _ = _ln_stats(conv_ref[...])
        zn = zh * lng_ref[...] + lnb_ref[...]
        yd = _dot((zn * _sigmoid(zn)).astype(BF16), wpw_ref[...]) + pwb_ref[...]
        gate, _ = _silu_parts(p_ref[:, O_GATE + BW:O_GATE + 2 * BW])
        ybuf[:, BW:2 * BW] = (yd * gate).astype(BF16)

        for h in range(NH):
            qs = slice(O_Q + h * HD, O_Q + (h + 1) * HD)
            _, yx = _attn_head(p_ref[:, qs].astype(BF16), kv_ref[:, h * HD:(h + 1) * HD],
                               kv_ref[:, XA + h * HD:XA + (h + 1) * HD])
            gs = slice(O_GATE + 2 * BW + h * HD, O_GATE + 2 * BW + (h + 1) * HD)
            gate, _ = _silu_parts(p_ref[:, gs])
            ybuf[:, 2 * BW + h * HD:2 * BW + (h + 1) * HD] = (yx * gate).astype(BF16)

        o = _dot(ybuf[...], wout_ref[...])
        o_ref[...] = o
        err = _rms_residual(x_ref[...], o, pg_ref[...]) - tgt_ref[...]
        lacc[...] += _colsum(err * err)
        dres_ref[...] = err * (1.0 / D)

        @pl.when(i == nt - 1)
        def _():
            loss_ref[...] = jnp.full((1, HD), jnp.sum(lacc[...]) * (0.5 / D), F32)

    tile = lambda n: pl.BlockSpec((tm, n), lambda i: (i, 0))
    ext = pltpu.VMEM((tm + HALO, BW), F32)
    return pl.pallas_call(
        body, name="odd_fwd", grid=(nt,),
        in_specs=[tile(D), tile(D), tile(ODD_IN), _halo_spec(ODD_IN, nt, False, tm)] + [_whole()] * 11,
        out_specs=[tile(D), tile(D), _full((1, HD)), tile(BW)],
        out_shape=[S((t, D), F32), S((t, D), F32), S((1, HD), F32), S((t, BW), F32)],
        scratch_shapes=[pltpu.VMEM((tm, MIX), BF16), ext, ext, ext, ext, ext,
                        pltpu.VMEM((1, D), F32), pltpu.VMEM((7, tm + HALO, BW), F32)],
        compiler_params=_params(("arbitrary",)),
    )(x1, tgt, p, p, wbd, cscale, dww, dwb, lng, lnb, wpw, pwb, kv, wout, pg)


def _odd_bwd(dres, o, p, conv, wbd, cscale, dww, dwb, lng, lnb, wpw, pwb, kv, wout, pg):
    t = dres.shape[0]
    tm = min(TM_BWD_ODD, t)
    nt = t // tm

    def body(dres_ref, o_ref, p_ref, ph_ref, conv_ref, wbd_ref, cs_ref, dww_ref, dwb_ref, lng_ref, lnb_ref, wpw_ref,
             pwb_ref, kv_ref, wout_ref, pg_ref, dp_ref, y_ref, do_ref, dpg_ref, dwbd_ref, dcs_ref, ddww_ref, ddwb_ref,
             dlng_ref, dlnb_ref, dwpw_ref, dpwb_ref, dkv_ref,
             dy, za, zb, zc, zd, tmp, carry_e, carry_d, wacc, shifted, t1, b1, b2):
        gbuf, hbuf = za, zb
        i = pl.program_id(0)
        ti = nt - 1 - i
        masks = _group_masks()

        @pl.when(i == 0)
        def _():
            for ref in (dpg_ref, dwbd_ref, dcs_ref, ddwb_ref, dlng_ref, dlnb_ref, dwpw_ref, dpwb_ref, dkv_ref, wacc):
                ref[...] = jnp.zeros_like(ref)

        def post_norm_rows(r0, acc):
            sl = pl.ds(r0, RB)
            ov, dv = o_ref[sl, :], dres_ref[sl, :]
            r = lax.rsqrt(_rowmean(ov * ov) + EPS)
            oh = ov * r
            doh = dv * pg_ref[...]
            do_ref[sl, :] = (r * (doh - oh * _rowmean(doh * oh))).astype(BF16)
            return acc + dv * oh

        dpg_ref[...] += _colsum(_loop_rows(tm, RB, post_norm_rows, jnp.zeros((RB, D), F32)))
        dy[...] = _dot_nt(do_ref[...], wout_ref[...])

        za[0:HALO] = jnp.where(ti > 0, ph_ref[:, O_ZC:O_ZC + BW], 0.0)
        za[HALO:HALO + tm] = p_ref[:, O_ZC:O_ZC + BW]
        _pool_causal_levels(za, zb, zc, zd, tm)

        def pooled_rows(r0, carry):
            sl = pl.ds(r0, RB)
            at = lambda ref, back=0: ref[pl.ds(HALO + r0 - back, RB), :]
            inv = _pool_weights(ti * tm + r0, masks, RB)
            sums = (at(zb), at(zc), at(zd), at(zd) + at(zd, 8))
            b1[sl, :] = (_mix4(masks, sums) * inv - p_ref[sl, O_ZC:O_ZC + BW]).astype(BF16)
            return carry

        _loop_rows(tm, RB, pooled_rows)
        for hs in _HALVES:
            t1[:, hs] = _dot(b1[:, hs], wbd_ref[hs, hs])

        def pool_gate_rows(r0, acc):
            sl = pl.ds(r0, RB)
            pm = t1[sl, :]
            gate, dgate = _silu_parts(p_ref[sl, O_GATE:O_GATE + BW])
            dyc = dy[sl, 0:BW]
            yc = pm * cs_ref[...]
            y_ref[sl, 0:BW] = (yc * gate).astype(BF16)
            dp_ref[sl, O_GATE:O_GATE + BW] = (dyc * yc * dgate).astype(BF16)
            dyc = dyc * gate
            b2[sl, :] = (dyc * cs_ref[...]).astype(BF16)
            return acc + dyc * pm

        dcs_ref[...] += _colsum(_loop_rows(tm, RB, pool_gate_rows, jnp.zeros((RB, BW), F32)))
        for hs in _HALVES:
            dwbd_ref[hs, hs] += _dot_tn(b1[:, hs], b2[:, hs])
            t1[:, hs] = _dot_nt(b2[:, hs], wbd_ref[hs, hs])

        def weighted_rows(r0, carry):
            sl = pl.ds(r0, RB)
            za[sl, :] = t1[sl, :] * _pool_weights(ti * tm + r0, masks, RB)
            return carry

        _loop_rows(tm, RB, weighted_rows)
        za[tm:tm + HALO] = jnp.where(i > 0, carry_e[...], 0.0)
        carry_e[...] = za[0:HALO]
        _pool_anticausal_levels(za, zb, zc, zd, tm)

        def pool_back_rows(r0, carry):
            sl = pl.ds(r0, RB)
            ahead = lambda ref, fwd=0: ref[pl.ds(r0 + fwd, RB), :]
            sums = (ahead(zb), ahead(zc), ahead(zd), ahead(zd) + ahead(zd, 8))
            dp_ref[sl, O_ZC:O_ZC + BW] = (_mix4(masks, sums) - t1[sl, :]).astype(BF16)
            return carry

        _loop_rows(tm, RB, pool_back_rows)

        gbuf[0:HALO] = jnp.where(ti > 0, ph_ref[:, O_GA:O_GA + BW] * _sigmoid(ph_ref[:, O_GB:O_GB + BW]), 0.0)

        def glu_rows(r0, carry):
            sl = pl.ds(r0, RB)
            gbuf[pl.ds(HALO + r0, RB), :] = p_ref[sl, O_GA:O_GA + BW] * _sigmoid(p_ref[sl, O_GB:O_GB + BW])
            zh, _ = _ln_stats(conv_ref[sl, :])
            zn = zh * lng_ref[...] + lnb_ref[...]
            b1[sl, :] = (zn * _sigmoid(zn)).astype(BF16)
            return carry

        _loop_rows(tm, RB, glu_rows)
        _shift_copies(gbuf, shifted)
        t1[...] = _dot(b1[...], wpw_ref[...])

        def conf_gate_rows(r0, acc):
            sl = pl.ds(r0, RB)
            yd = t1[sl, :] + pwb_ref[...]
            gate, dgate = _silu_parts(p_ref[sl, O_GATE + BW:O_GATE + 2 * BW])
            dyc = dy[sl, BW:2 * BW]
            y_ref[sl, BW:2 * BW] = (yd * gate).astype(BF16)
            dp_ref[sl, O_GATE + BW:O_GATE + 2 * BW] = (dyc * yd * dgate).astype(BF16)
            dyd = dyc * gate
            b2[sl, :] = dyd.astype(BF16)
            return acc + dyd

        dpwb_ref[...] += _colsum(_loop_rows(tm, RB, conf_gate_rows, jnp.zeros((RB, BW), F32)))
        dwpw_ref[...] += _dot_tn(b1[...], b2[...])
        t1[...] = _dot_nt(b2[...], wpw_ref[...])

        def norm_back_rows(r0, accs):
            sl = pl.ds(r0, RB)
            zh, rs = _ln_stats(conv_ref[sl, :])
            _, dsilu = _silu_parts(zh * lng_ref[...] + lnb_ref[...])
            dzn = t1[sl, :] * dsilu
            dzd = _ln_bwd(dzn, zh, rs, lng_ref[...])
            tmp[sl, :] = dzd
            hbuf[sl, :] = dzd
            return accs[0] + dzn * zh, accs[1] + dzn, accs[2] + dzd

        zero = jnp.zeros((RB, BW), F32)
        acc_g, acc_b, acc_d = _loop_rows(tm, RB, norm_back_rows, (zero, zero, zero))
        dlng_ref[...] += _colsum(acc_g)
        dlnb_ref[...] += _colsum(acc_b)
        ddwb_ref[...] += _colsum(acc_d)
        hbuf[tm:tm + HALO] = jnp.where(i > 0, carry_d[...], 0.0)
        carry_d[...] = tmp[0:HALO]
        def tap_grad_rows(r0, carry):
            _tap_grads(tmp[pl.ds(r0, 32), :], gbuf, shifted, wacc, r0, CONF_K)
            return carry

        _loop_rows(tm, 32, tap_grad_rows, unrolled=False)
        _shift_copies(hbuf, shifted)

        def conv_back_rows(r0, carry):
            sl = pl.ds(r0, 32)
            dzg = _tap_sum(hbuf, shifted, dww_ref, r0, CONF_K, False)
            sgb = _sigmoid(p_ref[sl, O_GB:O_GB + BW])
            dp_ref[sl, O_GA:O_GA + BW] = (dzg * sgb).astype(BF16)
            dp_ref[sl, O_GB:O_GB + BW] = (dzg * p_ref[sl, O_GA:O_GA + BW] * sgb * (1.0 - sgb)).astype(BF16)
            return carry

        _loop_rows(tm, 32, conv_back_rows, unrolled=False)

        for h in range(NH):
            qs = slice(O_Q + h * HD, O_Q + (h + 1) * HD)
            ks = slice(h * HD, (h + 1) * HD)
            vs = slice(XA + h * HD, XA + (h + 1) * HD)
            gs = slice(O_GATE + 2 * BW + h * HD, O_GATE + 2 * BW + (h + 1) * HD)
            ys = slice(2 * BW + h * HD, 2 * BW + (h + 1) * HD)
            q_b = p_ref[:, qs].astype(BF16)
            prob, yx = _attn_head(q_b, kv_ref[:, ks], kv_ref[:, vs])
            gate, dgate = _silu_parts(p_ref[:, gs])
            dyc = dy[:, ys]
            y_ref[:, ys] = (yx * gate).astype(BF16)
            dp_ref[:, gs] = (dyc * yx * dgate).astype(BF16)
            dyx_b = (dyc * gate).astype(BF16)
            dprob = _dot_nt(dyx_b, kv_ref[:, vs])
            dkv_ref[:, vs] += _dot_tn(prob.astype(BF16), dyx_b)
            ds_b = (prob * (dprob - jnp.sum(dprob * prob, axis=-1, keepdims=True)) * (HD ** -0.5)).astype(BF16)
            dp_ref[:, qs] = _dot(ds_b, kv_ref[:, ks]).astype(BF16)
            dkv_ref[:, ks] += _dot_tn(ds_b, q_b)

        @pl.when(i == nt - 1)
        def _():
            for k in range(CONF_K):
                ddww_ref[k:k + 1, :] = _colsum(wacc[k * 8:(k + 1) * 8, :])
            ddww_ref[CONF_K:CONF_K + 1, :] = jnp.zeros((1, BW), F32)

    rtile = lambda n: pl.BlockSpec((tm, n), lambda i: (nt - 1 - i, 0))
    outs = [S((t, ODD_IN), BF16), S((t, MIX), BF16), S((t, D), BF16), S((1, D), F32), S((BW, BW), F32),
            S((1, BW), F32), S((CONF_K + 1, BW), F32), S((1, BW), F32), S((1, BW), F32), S((1, BW), F32),
            S((BW, BW), F32), S((1, BW), F32), S((N_MEM, 2 * XA), F32)]
    ext = pltpu.VMEM((tm + HALO, BW), F32)
    return pl.pallas_call(
        body, name="odd_bwd", grid=(nt,),
        in_specs=[rtile(D), rtile(D), rtile(ODD_IN), _halo_spec(ODD_IN, nt, True, tm), rtile(BW)] + [_whole()] * 11,
        out_specs=[rtile(ODD_IN), rtile(MIX), rtile(D)] + [_full(s.shape) for s in outs[3:]],
        out_shape=outs,
        scratch_shapes=[pltpu.VMEM((tm, MIX), F32), ext, ext, ext, ext, pltpu.VMEM((tm, BW), F32),
                        pltpu.VMEM((HALO, BW), F32), pltpu.VMEM((HALO, BW), F32), pltpu.VMEM((CONF_K * 8, BW), F32),
                        pltpu.VMEM((7, tm + HALO, BW), F32),
                        pltpu.VMEM((tm, BW), F32), pltpu.VMEM((tm, BW), BF16), pltpu.VMEM((tm, BW), BF16)],
        compiler_params=pltpu.CompilerParams(dimension_semantics=("arbitrary",),
                                             vmem_limit_bytes=VMEM_LIMIT_ODD_BWD_V7X),
    )(dres, o, p, p, conv, wbd, cscale, dww, dwb, lng, lnb, wpw, pwb, kv, wout, pg)


def _pick_rows(n):
    for rows in (640, 2432, 1024, 768):
        if n % rows == 0:
            return rows
    return n


def _pad_rows(a, rows):
    return jnp.pad(a, ((0, rows - a.shape[0]), (0, 0)))


def _step(x, mem, tgt, ex):
    t = x.shape[0]
    tm = min(256, t)
    w, deps = ex.first()
    causal = jnp.tril(jnp.ones((CHUNK, CHUNK), bool))
    ws = jnp.where(causal[None], w["even_a_ws"], 0.0).astype(BF16)
    wst = jnp.transpose(ws, (0, 2, 1))
    bmap = jnp.repeat(w["even_a_bs"].T, GRP, axis=1)
    wc = _pad_rows(w["even_b_conv"], 8)
    wbd = jax.scipy.linalg.block_diag(*[w["odd_c_wgrp"][g] for g in range(NH)]).astype(BF16)
    dww = _pad_rows(w["odd_d_dw_w"], CONF_K + 1)
    tk = min(1024, t)
    zeros = jnp.zeros_like(mem)

    p_e, h_e = _rms_matmul(x, w["even_pre_g"], w["even_w_in"], tm=tm, name="in_even", transposed=True, deps=deps)
    w.update(ex.even_rest(h_e))
    kv_e, memn_e = _rms_matmul(mem, w["even_mem_g"], w["even_w_kv"], tm=N_MEM, name="kv_even")
    kv_e = kv_e.astype(BF16)
    even_args = (w["even_a_ln_g"], w["even_a_ln_b"], ws)
    o_e, x1 = _even_fwd(x, p_e, *even_args, bmap, wc, kv_e, w["even_w_out"], w["even_post_g"])
    w.update(ex.odd(o_e))
    kv_o, memn_o = _rms_matmul(mem, w["odd_mem_g"], w["odd_w_kv"], tm=N_MEM, name="kv_odd")
    kv_o = kv_o.astype(BF16)
    p_o, h_o = _rms_matmul(x1, w["odd_pre_g"], w["odd_w_in"], tm=tm, name="in_odd", transposed=True)
    odd_args = (wbd, w["odd_c_scale"], dww, w["odd_d_dw_b"], w["odd_d_ln_g"], w["odd_d_ln_b"], w["odd_d_pw_w"],
                w["odd_d_pw_b"], kv_o, w["odd_w_out"], w["odd_post_g"])
    o_o, dres, loss, conv_o = _odd_fwd(x1, tgt, p_o, *odd_args)

    g = {}
    (dp_o, y_o, do_o, post_g_o, dwbd, g["odd_c_scale"], ddww, g["odd_d_dw_b"], g["odd_d_ln_g"], g["odd_d_ln_b"],
     dwpw, g["odd_d_pw_b"], dkv_o) = _odd_bwd(dres, o_o, p_o, conv_o, *odd_args)
    g["odd_post_g"] = post_g_o
    g["odd_d_dw_w"] = ddww[:CONF_K]
    dkv_o = dkv_o.astype(BF16)
    deps = ex.send("odd_rest", {
        "odd_w_out": _tn_matmul(y_o, do_o, tmc=MIX, tk=tk, out_dtype=BF16, name="dw_out_odd"),
        "odd_w_kv": _tn_matmul(memn_o, dkv_o, tmc=D, tk=N_MEM, out_dtype=BF16, name="dw_kv_odd"),
        "odd_d_pw_w": dwpw.astype(BF16), "loss": loss,
        "odd_c_wgrp": jnp.concatenate([dwbd[i * GRP:(i + 1) * GRP, i * GRP:(i + 1) * GRP] for i in range(NH)])})
    deps = ex.send("odd_in", {"odd_w_in": _tn_matmul(dp_o, h_o, tmc=_pick_rows(ODD_IN), tk=tk, out_dtype=BF16,
                                                     name="dw_in_odd", deps=deps)})
    dx1, g["odd_pre_g"] = _nt_matmul_rms_bwd(dp_o, w["odd_w_in"], x1, w["odd_pre_g"], dres, tm=min(256, t),
                                             name="dx_odd", transposed=True, deps=deps)
    _, g["odd_mem_g"] = _nt_matmul_rms_bwd(dkv_o, w["odd_w_kv"], mem, w["odd_mem_g"], zeros, tm=N_MEM,
                                           name="dmem_odd")

    (dp_e, y_e, do_e, post_g_e, dws, dbs, ln_g_e, ln_b_e, dwc, dkv_e) = _even_bwd(
        dx1, o_e, p_e, *even_args, wst, bmap, wc, kv_e, w["even_w_out"], w["even_post_g"])
    g["even_b_conv"] = dwc[:3]
    dkv_e = dkv_e.astype(BF16)
    deps = ex.send("even_rest", {
        "even_w_out": _tn_matmul(y_e, do_e, tmc=MIX, tk=tk, out_dtype=BF16, name="dw_out_even"),
        "even_w_kv": _tn_matmul(memn_e, dkv_e, tmc=D, tk=N_MEM, out_dtype=BF16, name="dw_kv_even"),
        "even_a_ln_g": ln_g_e, "even_a_ln_b": ln_b_e,
        "even_a_ws": dws.reshape(NH * CHUNK, CHUNK), "even_a_bs": dbs[:, ::HD].T})
    g["even_w_in"] = _tn_matmul(dp_e, h_e, tmc=_pick_rows(EVEN_IN), tk=tk, out_dtype=BF16, name="dw_in_even",
                                deps=deps)
    deps = ex.send("even_in", g)
    grad_x, pre_g_e = _nt_matmul_rms_bwd(dp_e, w["even_w_in"], x, w["even_pre_g"], dx1, tm=min(256, t),
                                         name="dx_even", transposed=True, deps=deps)
    _, mem_g_e = _nt_matmul_rms_bwd(dkv_e, w["even_w_kv"], mem, w["even_mem_g"], zeros, tm=N_MEM, name="dmem_even",
                                    deps=(grad_x,))
    deps = ex.send("even_gains", {"even_pre_g": pre_g_e, "even_mem_g": mem_g_e, "even_post_g": post_g_e})
    return grad_x, deps


def _place():
    return lax.axis_index("x"), lax.axis_index("y"), lax.axis_index("c")


def _index(px, py, pc):
    return 4 * px + 2 * py + pc


_COPIES = N_DEV - 1


def _all_gather(arrs, name):
    n = len(arrs)

    def body(*refs):
        ins, outs = refs[:n], refs[n:2 * n]
        send_sems, recv_sems, local_sems = refs[2 * n:]
        x, y, c = _place()
        me, sibling = (x, y, c), (x, y, 1 - c)
        chips = [(1 - x, y), (x, 1 - y), (1 - x, 1 - y)]

        def copy(a, k, block, to, src=None):
            dst = outs[a].at[_index(*block)]
            return pltpu.make_async_remote_copy(
                src_ref=dst if src is None else src, dst_ref=dst, send_sem=send_sems.at[a * _COPIES + k],
                recv_sem=recv_sems.at[a * _COPIES + k], device_id=to, device_id_type=MESH)

        mine = [pltpu.make_async_copy(ins[a], outs[a].at[_index(*me)], local_sems.at[a]) for a in range(n)]
        first = []
        for a in range(n):
            mine[a].start()
            first.append(copy(a, 0, me, sibling, src=ins[a]))
            first += [copy(a, 1 + j, me, (*chip, c), src=ins[a]) for j, chip in enumerate(chips)]
        for cp in first:
            cp.start()
        passed = []
        for j, chip in enumerate(chips):
            for a in range(n):
                copy(a, 1 + j, (*chip, c), me).wait_recv()
                passed.append(copy(a, 4 + j, (*chip, c), sibling))
                passed[-1].start()
        for a in range(n):
            copy(a, 0, sibling, me).wait_recv()
            for j, chip in enumerate(chips):
                copy(a, 4 + j, (*chip, 1 - c), me).wait_recv()
        for cp in first + passed:
            cp.wait_send()
        for cp in mine:
            cp.wait()

    return pl.pallas_call(
        body, name=name, in_specs=[_ANY] * n, out_specs=[_ANY] * n,
        out_shape=[S((N_DEV,) + a.shape, a.dtype) for a in arrs],
        scratch_shapes=[pltpu.SemaphoreType.DMA((n * _COPIES,)), pltpu.SemaphoreType.DMA((n * _COPIES,)),
                        pltpu.SemaphoreType.DMA((n,))],
    )(*arrs)


_HBM = pl.BlockSpec(memory_space=pltpu.HBM)
_SEM = pl.BlockSpec(memory_space=pltpu.SEMAPHORE)
_EFFECT = pltpu.SideEffectType.DATAFLOW_SIDE_EFFECTING


_ALL_FLIPS = [(k >> 2 & 1, k >> 1 & 1, k & 1) for k in range(1, N_DEV)]
_CHIP_FLIPS = [(1, 0, 0), (0, 1, 0), (1, 1, 0)]
_FLIPS = {"gather": _ALL_FLIPS, "scatter": _ALL_FLIPS, "gather_chips": [(0, 0, 1)] + _CHIP_FLIPS,
          "scatter_chips": _CHIP_FLIPS}


def _landing_shape(kind, a):
    return (N_DEV,) + a.shape if kind.startswith("gather") else a.shape


def _exchange_copies(kinds, srcs, lands, send_sems, recv_sems, local_sems, arriving):
    x, y, c = _place()
    mine = _index(x, y, c)
    remote, local = [], []
    for a, kind in enumerate(kinds):
        by_chip = kind == "scatter_chips"
        here = 2 * x + y if by_chip else mine
        own = srcs[a] if kind.startswith("gather") else srcs[a].at[here]
        local.append(pltpu.make_async_copy(own, lands[a].at[here], local_sems.at[a]))
        for k, (fx, fy, fc) in enumerate(_FLIPS[kind]):
            peer = (1 - x if fx else x, 1 - y if fy else y, 1 - c if fc else c)
            there = 2 * peer[0] + peer[1] if by_chip else _index(*peer)
            remote.append(pltpu.make_async_remote_copy(
                src_ref=srcs[a] if kind.startswith("gather") else srcs[a].at[there],
                dst_ref=lands[a].at[there if arriving else here],
                send_sem=send_sems.at[a * _COPIES + k], recv_sem=recv_sems.at[a * _COPIES + k],
                device_id=peer, device_id_type=MESH))
    return remote, local


def _exchange_start(items, name, deps=()):
    kinds = [kind for kind, _ in items]
    srcs = [a for _, a in items]
    n = len(items)
    lands = [lax.empty(_landing_shape(kind, a), a.dtype) for kind, a in items]

    def body(*refs):
        send_sems, recv_sems, local_sems = refs[2 * n + len(deps):2 * n + len(deps) + 3]
        remote, local = _exchange_copies(kinds, refs[:n], refs[n:2 * n], send_sems, recv_sems, local_sems, False)
        for cp in local + remote:
            cp.start()
        refs[-1][...] = jnp.zeros_like(refs[-1])

    held = [pltpu.HBM(a.shape, a.dtype) for a in srcs + lands]
    res = pl.pallas_call(
        body, name=name,
        out_shape=(pltpu.SemaphoreType.DMA((n * _COPIES,)), pltpu.SemaphoreType.DMA((n * _COPIES,)),
                   pltpu.SemaphoreType.DMA((n,)), *held, S((8, 128), F32)),
        in_specs=[_HBM] * (2 * n) + [_ANY] * len(deps),
        out_specs=(_SEM, _SEM, _SEM, *[_HBM] * (2 * n), _whole()),
        input_output_aliases={i: 3 + i for i in range(2 * n)},
        compiler_params=pltpu.CompilerParams(has_side_effects=_EFFECT),
    )(*[pltpu.with_memory_space_constraint(a, pltpu.HBM) for a in srcs + lands], *deps)
    return (kinds, res[:3], res[3:3 + 2 * n]), res[-1]


def _exchange_wait(handle, after, name):
    kinds, sems, held = handle
    n = len(kinds)

    def body(*refs):
        send_sems, recv_sems, local_sems = refs[2 * n:2 * n + 3]
        remote, local = _exchange_copies(kinds, refs[:n], refs[n:2 * n], send_sems, recv_sems, local_sems, True)
        for cp in remote:
            cp.wait_send()
            cp.wait_recv()
        for cp in local:
            cp.wait()

    res = pl.pallas_call(
        body, name=name, out_shape=[pltpu.HBM(a.shape, a.dtype) for a in held],
        in_specs=[_HBM] * (2 * n) + [_SEM] * 3 + [_ANY] * len(after), out_specs=[_HBM] * (2 * n),
        input_output_aliases={i: i for i in range(2 * n)},
        compiler_params=pltpu.CompilerParams(has_side_effects=_EFFECT),
    )(*held, *sems, *after)
    return res[n:]


_CHIPS = [(0, 0), (0, 1), (1, 0), (1, 1)]
_N_CHIPS = len(_CHIPS)


def _sibling_forward(lands, name):
    n = len(lands)

    def body(*refs):
        ins, outs = refs[:n], refs[n:2 * n]
        send_sems, recv_sems = refs[2 * n:]
        x, y, c = _place()
        sent, arriving = [], []
        for a in range(n):
            for j, (fx, fy, _) in enumerate(_CHIP_FLIPS):
                chip = (1 - x if fx else x, 1 - y if fy else y)
                sems = dict(send_sem=send_sems.at[a * 3 + j], recv_sem=recv_sems.at[a * 3 + j],
                            device_id=(x, y, 1 - c), device_id_type=MESH)
                mine, theirs = _index(*chip, c), _index(*chip, 1 - c)
                sent.append(pltpu.make_async_remote_copy(src_ref=ins[a].at[mine], dst_ref=outs[a].at[mine], **sems))
                arriving.append(pltpu.make_async_remote_copy(src_ref=ins[a].at[theirs], dst_ref=outs[a].at[theirs],
                                                             **sems))
        for cp in sent:
            cp.start()
        for cp in sent:
            cp.wait_send()
        for cp in arriving:
            cp.wait_recv()

    return pl.pallas_call(
        body, name=name, in_specs=[_ANY] * n, out_specs=[_ANY] * n,
        out_shape=[S(a.shape, a.dtype) for a in lands], input_output_aliases={a: a for a in range(n)},
        scratch_shapes=[pltpu.SemaphoreType.DMA((3 * n,)), pltpu.SemaphoreType.DMA((3 * n,))],
    )(*lands)


def _sibling_swap(arrs, name):
    n = len(arrs)

    def body(*refs):
        ins, outs = refs[:n], refs[n:2 * n]
        send_sems, recv_sems = refs[2 * n:]
        x, y, c = _place()
        copies = []
        for a in range(n):
            for q, chip in enumerate(_CHIPS):
                copies.append(pltpu.make_async_remote_copy(
                    src_ref=ins[a].at[_index(*chip, 1 - c)], dst_ref=outs[a].at[q],
                    send_sem=send_sems.at[a * _N_CHIPS + q], recv_sem=recv_sems.at[a * _N_CHIPS + q],
                    device_id=(x, y, 1 - c), device_id_type=MESH))
        for cp in copies:
            cp.start()
        for cp in copies:
            cp.wait_send()
            cp.wait_recv()

    return pl.pallas_call(
        body, name=name, in_specs=[_ANY] * n, out_specs=[_ANY] * n,
        out_shape=[S((_N_CHIPS,) + a.shape[1:], a.dtype) for a in arrs],
        scratch_shapes=[pltpu.SemaphoreType.DMA((_N_CHIPS * n,)), pltpu.SemaphoreType.DMA((_N_CHIPS * n,))],
    )(*arrs)


def _add_partials(mine, theirs, *, tr, name):
    _, r, c = mine.shape

    def body(mine_ref, theirs_ref, out_ref):
        core = lax.axis_index("c")
        own = jnp.where(core == 0, mine_ref[0].astype(F32), mine_ref[1].astype(F32))
        out_ref[0] = (own + theirs_ref[0].astype(F32)).astype(out_ref.dtype)

    return pl.pallas_call(
        body, name=name, grid=(_N_CHIPS, r // tr),
        in_specs=[pl.BlockSpec((2, tr, c), lambda q, i: (q, i, 0)), pl.BlockSpec((1, tr, c), lambda q, i: (q, i, 0))],
        out_specs=pl.BlockSpec((1, tr, c), lambda q, i: (q, i, 0)),
        out_shape=S((_N_CHIPS, r, c), mine.dtype),
        compiler_params=_params(("arbitrary", "arbitrary")),
    )(mine, theirs)


def _adamw(w, g, m, v):
    m = ADAM_B1 * m + (1.0 - ADAM_B1) * g
    v = ADAM_B2 * v + (1.0 - ADAM_B2) * (g * g)
    m_hat = m / (1.0 - ADAM_B1 ** ADAM_STEP)
    v_hat = v / (1.0 - ADAM_B2 ** ADAM_STEP)
    return -ADAM_LR * (m_hat / (jnp.sqrt(v_hat) + ADAM_EPS) + ADAM_WD * w), m, v


def _sum_devices(ref, rows):
    total = ref[0, rows, :].astype(F32)
    for s in range(1, ref.shape[0]):
        total = total + ref[s, rows, :].astype(F32)
    return total


def _adam_big(recv, w, m, v, *, tr, name):
    r, c = w.shape

    def body(recv_ref, w_ref, m_ref, v_ref, g_ref, d_ref, m2_ref, v2_ref):
        g = _sum_devices(recv_ref, slice(None))
        g_ref[...] = g
        d_ref[...], m2_ref[...], v2_ref[...] = _adamw(w_ref[...], g, m_ref[...], v_ref[...])

    blk = pl.BlockSpec((tr, c), lambda i: (i, 0))
    return pl.pallas_call(
        body, name=name, grid=(r // tr,),
        in_specs=[pl.BlockSpec((recv.shape[0], tr, c), lambda i: (0, i, 0)), blk, blk, blk],
        out_specs=[blk] * 4, out_shape=[S((r, c), F32)] * 4,
        compiler_params=_params(("arbitrary",)),
    )(recv, w, m, v)


_REPLICATED = {"even_pre_g": (0, 0, 1), "even_mem_g": (0, 8, 1), "even_post_g": (0, 16, 1),
               "even_a_ln_g": (1, 0, 1), "even_a_ln_b": (1, 8, 1),
               "even_a_ws": (2, 0, NH * CHUNK), "even_a_bs": (2, NH * CHUNK, NH),
               "odd_c_wgrp": (3, 0, NH * GRP)}
_SHARDED = {"odd_pre_g": (4, 0, 1), "odd_mem_g": (4, 8, 1), "odd_post_g": (4, 16, 1),
            "even_b_conv": (5, 0, 3), "odd_c_scale": (5, 8, 1), "odd_d_dw_w": (5, 16, CONF_K),
            "odd_d_dw_b": (5, 48, 1), "odd_d_ln_g": (5, 56, 1), "odd_d_ln_b": (5, 64, 1), "odd_d_pw_b": (5, 72, 1)}
_SMALL = {**_REPLICATED, **_SHARDED}
_SMALL_ROWS = {0: 24, 1: 16, 2: NH * CHUNK + 8, 3: NH * GRP, 4: 24, 5: 80}


def _adam_small(sources, wmv):
    names = list(_SMALL)
    ns = len(sources)

    def body(*refs):
        src = refs[:ns]
        ins = refs[ns:ns + 3 * len(names)]
        outs = refs[ns + 3 * len(names):]
        outs[-1][...] = _sum_devices(src[-1], slice(0, 1))
        for i, nm in enumerate(names):
            a, row0, rows = _SMALL[nm]
            g = _sum_devices(src[a], slice(row0, row0 + rows))
            w_ref, m_ref, v_ref = ins[3 * i:3 * i + 3]
            g_ref, d_ref, m2_ref, v2_ref = outs[4 * i:4 * i + 4]
            g_ref[...] = g
            d_ref[...], m2_ref[...], v2_ref[...] = _adamw(w_ref[...], g, m_ref[...], v_ref[...])

    flat = [t for nm in names for t in wmv[nm]]
    out_shape = [S(wmv[nm][0].shape, F32) for nm in names for _ in range(4)] + [S((1, HD), F32)]
    res = pl.pallas_call(
        body, name="adam_small", in_specs=[_whole()] * (ns + len(flat)), out_specs=[_whole()] * len(out_shape),
        out_shape=out_shape, compiler_params=_params(),
    )(*sources, *flat)
    return {nm: tuple(res[4 * i:4 * i + 4]) for i, nm in enumerate(names)}, res[-1]


_WEIGHTS = ["even_pre_g", "even_w_in", "even_a_ln_g", "even_a_ln_b", "even_a_ws", "even_a_bs", "even_b_conv",
            "even_mem_g", "even_w_kv", "even_w_out", "even_post_g", "odd_pre_g", "odd_w_in", "odd_c_wgrp",
            "odd_c_scale", "odd_d_dw_w", "odd_d_dw_b", "odd_d_ln_g", "odd_d_ln_b", "odd_d_pw_w", "odd_d_pw_b",
            "odd_mem_g", "odd_w_kv", "odd_w_out", "odd_post_g"]
_TRANSPOSED = ["even_w_in", "odd_w_in"]
_BIG = _TRANSPOSED + ["even_w_kv", "even_w_out", "odd_w_kv", "odd_w_out", "odd_d_pw_w"]
_BIG_TILE_ROWS = {"even_w_in": 400, "odd_w_in": 304, "even_w_kv": 128, "even_w_out": 128, "odd_w_kv": 128,
                  "odd_w_out": 128, "odd_d_pw_w": 96}


def _view2d(a, transposed):
    a = a[0]
    if a.ndim == 1:
        return a[None]
    if transposed:
        return a.T
    return a.reshape(-1, a.shape[-1])


def _rows8(a):
    return _pad_rows(a, -(-a.shape[0] // 8) * 8)


def _pack_rows(parts):
    return jnp.concatenate([_rows8(p) for p in parts], axis=0)


def _unshard_cols(a):
    return jnp.transpose(a, (1, 0, 2)).reshape(a.shape[1], N_DEV * a.shape[2])


def _shard_cols(a):
    return jnp.transpose(a.reshape(a.shape[0], N_DEV, a.shape[1] // N_DEV), (1, 0, 2))


def _rows_of(a):
    return a.reshape(-1, a.shape[-1])


_GROUPS = {"odd_rest": (["odd_w_out", "odd_w_kv", "odd_d_pw_w"], [3], []),
           "odd_in": (["odd_w_in"], [], []),
           "even_rest": (["even_w_out", "even_w_kv"], [1, 2], []),
           "even_in": (["even_w_in"], [], [4, 5]),
           "even_gains": ([], [0], [])}


_TWO_LEVEL = ("even_in",)


class _MeshExchange:
    def __init__(self, shard):
        self.shard = shard
        self.handles = {}

    def first(self):
        shard = self.shard
        packs = [_pack_rows([shard[nm] for nm in _SHARDED if _SHARDED[nm][0] == a]) for a in (4, 5)]
        w_in, p128, p96 = _all_gather([shard["even_w_in"].astype(BF16)] + packs, "gather_first")
        w = {nm: shard[nm] for nm in _REPLICATED}
        w["even_a_ws"] = w["even_a_ws"].reshape(NH, CHUNK, CHUNK)
        w["odd_c_wgrp"] = w["odd_c_wgrp"].reshape(NH, GRP, GRP)
        w["even_w_in"] = _rows_of(w_in)
        full_packs = {4: _unshard_cols(p128), 5: _unshard_cols(p96)}
        for nm, (a, row0, rows) in _SHARDED.items():
            w[nm] = full_packs[a][row0:row0 + rows]
        later = lambda names: [("gather_chips", shard[nm].astype(BF16)) for nm in names]
        self.handles["w_even"], token = _exchange_start(later(["even_w_kv", "even_w_out"]), "gather_even_start",
                                                        deps=(w_in,))
        self.handles["w_odd"], token = _exchange_start(later(["odd_w_in", "odd_w_kv", "odd_w_out", "odd_d_pw_w"]),
                                                       "gather_odd_start", deps=(token,))
        return w, (token,)

    def even_rest(self, after):
        landed = _exchange_wait(self.handles.pop("w_even"), (after,), "gather_even_wait")
        kv, out = _sibling_forward(landed, "forward_even")
        return {"even_w_kv": _rows_of(kv), "even_w_out": _rows_of(out)}

    def odd(self, after):
        landed = _exchange_wait(self.handles.pop("w_odd"), (after,), "gather_odd_wait")
        w_in, kv, out, pw = _sibling_forward(landed, "forward_odd")
        return {"odd_w_in": _rows_of(w_in), "odd_w_kv": _rows_of(kv), "odd_w_out": _rows_of(out),
                "odd_d_pw_w": _rows_of(pw)}

    def send(self, group, g):
        big, replicated, sharded = _GROUPS[group]
        by_owner = [g[nm].reshape(N_DEV, -1, g[nm].shape[-1]) for nm in big]
        if group in _TWO_LEVEL:
            theirs = _sibling_swap(by_owner, "swap_" + group)
            items = [("scatter_chips", _add_partials(a, b, tr=_BIG_TILE_ROWS[nm], name="chip_sum_" + nm))
                     for nm, a, b in zip(big, by_owner, theirs)]
        else:
            items = [("scatter", a) for a in by_owner]
        items += [("gather", _pack_rows([g[nm] for nm in _REPLICATED if _REPLICATED[nm][0] == a]))
                  for a in replicated]
        items += [("scatter", _shard_cols(_pack_rows([g[nm] for nm in _SHARDED if _SHARDED[nm][0] == a])))
                  for a in sharded]
        if group == "odd_rest":
            items.append(("gather", _rows8(g["loss"])))
        self.handles[group], token = _exchange_start(items, "send_" + group + "_start")
        return (token,)

    def receive(self, group, after):
        after = after if isinstance(after, tuple) else (after,)
        return _exchange_wait(self.handles.pop(group), after, "send_" + group + "_wait")


def kernel(x, mem, even_pre_g, even_w_in, even_a_ln_g, even_a_ln_b, even_a_ws, even_a_bs, even_b_conv, even_mem_g, even_w_kv, even_w_out, even_post_g, odd_pre_g, odd_w_in, odd_c_wgrp, odd_c_scale, odd_d_dw_w, odd_d_dw_b, odd_d_ln_g, odd_d_ln_b, odd_d_pw_w, odd_d_pw_b, odd_mem_g, odd_w_kv, odd_w_out, odd_post_g, loss_target, m_even_pre_g, m_even_w_in, m_even_a_ln_g, m_even_a_ln_b, m_even_a_ws, m_even_a_bs, m_even_b_conv, m_even_mem_g, m_even_w_kv, m_even_w_out, m_even_post_g, m_odd_pre_g, m_odd_w_in, m_odd_c_wgrp, m_odd_c_scale, m_odd_d_dw_w, m_odd_d_dw_b, m_odd_d_ln_g, m_odd_d_ln_b, m_odd_d_pw_w, m_odd_d_pw_b, m_odd_mem_g, m_odd_w_kv, m_odd_w_out, m_odd_post_g, v_even_pre_g, v_even_w_in, v_even_a_ln_g, v_even_a_ln_b, v_even_a_ws, v_even_a_bs, v_even_b_conv, v_even_mem_g, v_even_w_kv, v_even_w_out, v_even_post_g, v_odd_pre_g, v_odd_w_in, v_odd_c_wgrp, v_odd_c_scale, v_odd_d_dw_w, v_odd_d_dw_b, v_odd_d_ln_g, v_odd_d_ln_b, v_odd_d_pw_w, v_odd_d_pw_b, v_odd_mem_g, v_odd_w_kv, v_odd_w_out, v_odd_post_g):
    given = dict(locals())
    view = lambda nm, kind: _view2d(given[kind + nm], nm in _TRANSPOSED)
    shard = {nm: view(nm, "") for nm in _WEIGHTS}
    wmv = {nm: (shard[nm], view(nm, "m_"), view(nm, "v_")) for nm in _WEIGHTS}

    ex = _MeshExchange(shard)
    grad_x, last = _step(x[0], mem[0], loss_target[0], ex)

    res = {}

    def update(group, after):
        names = _GROUPS[group][0]
        landed = ex.receive(group, after)
        for nm, recv in zip(names, landed):
            res[nm] = _adam_big(recv, *wmv[nm], tr=_BIG_TILE_ROWS[nm], name="adam_" + nm)
        return landed[len(names):]

    c192, losses = update("odd_rest", last)
    update("odd_in", res["odd_d_pw_w"][0])
    c768, c128 = update("even_rest", res["odd_w_in"][0])
    a128, a96 = update("even_in", res["even_w_kv"][0])
    (c1024,) = update("even_gains", res["even_w_in"][0])
    small, loss = _adam_small([c1024, c768, c128, c192, a128, a96, losses], {nm: wmv[nm] for nm in _SMALL})
    res.update(small)
    total = loss[0, 0]
    back = lambda nm, a: (a.T if nm in _TRANSPOSED else a).reshape(given[nm].shape)
    outs = [[back(nm, res[nm][i]) for nm in _WEIGHTS] for i in range(4)]
    return (total, grad_x[None], *outs[0], *outs[1], *outs[2], *outs[3])
```

```python
import functools

import jax
import jax.numpy as jnp
from jax import lax
from jax.experimental import pallas as pl
from jax.experimental.pallas import tpu as pltpu

F32 = jnp.float32
BF16 = jnp.bfloat16
S = jax.ShapeDtypeStruct
MESH = pl.DeviceIdType.MESH
AXES = ("x", "y", "c")
N_DEV = 8

D = 1024
BW = 768
XA = 512
HD = 128
NH = 4
MIX = 2048
CHUNK = 128
GRP = 192
N_MEM = 256
CONF_K = 31
EPS = 1e-6
HALO = 32
POOL_WINDOWS = (2, 4, 8, 16)
TM_FWD_EVEN = 512
TM_FWD_ODD = 256
TM_BWD_EVEN = 256
TM_BWD_ODD = 256
RB = 16

E_U, E_V, E_BG, E_CG, E_XIN, E_Q, E_GATE = 0, 768, 1536, 2304, 3072, 3840, 4352
EVEN_IN = 6400
O_ZC, O_GA, O_GB, O_Q, O_GATE = 0, 768, 1536, 2304, 2816
ODD_IN = 4864

ADAM_LR, ADAM_B1, ADAM_B2, ADAM_EPS, ADAM_WD, ADAM_STEP = 0.001, 0.9, 0.999, 1e-08, 0.01, 10

VMEM_LIMIT_V7X = 56 * 1024 * 1024
VMEM_LIMIT_ODD_BWD_V7X = 62 * 1024 * 1024


def _params(sem=None):
    return pltpu.CompilerParams(dimension_semantics=sem, vmem_limit_bytes=VMEM_LIMIT_V7X)


def _dot(a, b):
    return jnp.dot(a, b, preferred_element_type=F32)


def _dot_nt(a, b):
    return lax.dot_general(a, b, (((1,), (1,)), ((), ())), preferred_element_type=F32)


def _dot_tn(a, b):
    return lax.dot_general(a, b, (((0,), (0,)), ((), ())), preferred_element_type=F32)


def _sigmoid(z):
    return 1.0 / (1.0 + jnp.exp(-z))


def _rowmean(a):
    return jnp.mean(a, axis=-1, keepdims=True)


def _colsum(a):
    return jnp.sum(a, axis=0, keepdims=True)


def _ln_stats(v):
    mu = _rowmean(v)
    vc = v - mu
    rs = lax.rsqrt(_rowmean(vc * vc) + EPS)
    return vc * rs, rs


def _ln_bwd(dn, vh, rs, g):
    dvh = dn * g
    return rs * (dvh - _rowmean(dvh) - vh * _rowmean(dvh * vh))


def _group_masks():
    col = lax.broadcasted_iota(jnp.int32, (1, BW), 1)
    return [((col >= GRP * h) & (col < GRP * (h + 1))).astype(F32) for h in range(NH)]


def _full(shape):
    nd = len(shape)
    return pl.BlockSpec(shape, lambda *_: (0,) * nd)


def _whole():
    return pl.BlockSpec(memory_space=pltpu.VMEM)


_ANY = pl.BlockSpec(memory_space=pl.ANY)


def _after(body, n_in, deps):
    def ordered(*refs):
        return body(*refs[:n_in], *refs[n_in + len(deps):])
    return ordered


def _rms_matmul(x, g, w, *, tm, name, transposed=False, out_dtype=F32, deps=()):
    t, d = x.shape
    n = w.shape[0] if transposed else w.shape[1]

    def body(x_ref, g_ref, w_ref, p_ref, h_ref):
        xv = x_ref[...]
        r = lax.rsqrt(_rowmean(xv * xv) + EPS)
        h = (xv * r * g_ref[...]).astype(BF16)
        h_ref[...] = h
        p_ref[...] = (_dot_nt(h, w_ref[...]) if transposed else _dot(h, w_ref[...])).astype(out_dtype)

    return pl.pallas_call(
        _after(body, 3, deps), name=name, grid=(t // tm,),
        in_specs=[pl.BlockSpec((tm, d), lambda i: (i, 0)), _whole(), _whole()] + [_ANY] * len(deps),
        out_specs=[pl.BlockSpec((tm, n), lambda i: (i, 0)), pl.BlockSpec((tm, d), lambda i: (i, 0))],
        out_shape=[S((t, n), out_dtype), S((t, d), BF16)],
        compiler_params=_params(("arbitrary",)),
    )(x, g, w, *deps)


def _nt_matmul_rms_bwd(dp, w, x, g, dres, *, tm, name, transposed=False, deps=()):
    t, n = dp.shape
    d = x.shape[1]

    def body(dp_ref, w_ref, x_ref, g_ref, dres_ref, dx_ref, dg_ref):
        @pl.when(pl.program_id(0) == 0)
        def _():
            dg_ref[...] = jnp.zeros_like(dg_ref)

        dh = _dot(dp_ref[...], w_ref[...]) if transposed else _dot_nt(dp_ref[...], w_ref[...])
        xv = x_ref[...]
        r = lax.rsqrt(_rowmean(xv * xv) + EPS)
        xh = xv * r
        dg_ref[...] += _colsum(dh * xh)
        dxh = dh * g_ref[...]
        dx_ref[...] = dres_ref[...] + r * (dxh - xh * _rowmean(dxh * xh))

    return pl.pallas_call(
        _after(body, 5, deps), name=name, grid=(t // tm,),
        in_specs=[pl.BlockSpec((tm, n), lambda i: (i, 0)), _whole(), pl.BlockSpec((tm, d), lambda i: (i, 0)),
                  _whole(), pl.BlockSpec((tm, d), lambda i: (i, 0))] + [_ANY] * len(deps),
        out_specs=[pl.BlockSpec((tm, d), lambda i: (i, 0)), pl.BlockSpec((1, d), lambda i: (0, 0))],
        out_shape=[S((t, d), F32), S((1, d), F32)],
        compiler_params=_params(("arbitrary",)),
    )(dp, w, x, g, dres, *deps)


def _tn_matmul(a, b, *, tmc, tk, out_dtype, name, deps=()):
    t, m = a.shape
    n = b.shape[1]
    nk = t // tk

    def body(a_ref, b_ref, o_ref, acc_ref):
        k = pl.program_id(1)

        @pl.when(k == 0)
        def _():
            acc_ref[...] = jnp.zeros_like(acc_ref)

        acc_ref[...] += _dot_tn(a_ref[...], b_ref[...])

        @pl.when(k == nk - 1)
        def _():
            o_ref[...] = acc_ref[...].astype(out_dtype)

    return pl.pallas_call(
        _after(body, 2, deps), name=name, grid=(m // tmc, nk),
        in_specs=[pl.BlockSpec((tk, tmc), lambda j, k: (k, j)), pl.BlockSpec((tk, n), lambda j, k: (k, 0))]
        + [_ANY] * len(deps),
        out_specs=pl.BlockSpec((tmc, n), lambda j, k: (j, 0)),
        out_shape=S((m, n), out_dtype),
        scratch_shapes=[pltpu.VMEM((tmc, n), F32)],
        compiler_params=_params(("arbitrary", "arbitrary")),
    )(a, b, *deps)


def _silu_parts(gt):
    sg = _sigmoid(gt)
    return gt * sg, sg * (1.0 + gt * (1.0 - sg))


def _attn_head(q_b, k_b, v_b):
    s = _dot_nt(q_b, k_b) * (HD ** -0.5)
    e = jnp.exp(s - jnp.max(s, axis=-1, keepdims=True))
    prob = e / jnp.sum(e, axis=-1, keepdims=True)
    return prob, _dot(prob.astype(BF16), v_b)


def _rms_residual(x, o, g):
    r = lax.rsqrt(_rowmean(o * o) + EPS)
    return x + o * r * g


def _rms_post_bwd(dres, o, g):
    r = lax.rsqrt(_rowmean(o * o) + EPS)
    oh = o * r
    doh = dres * g
    return r * (doh - oh * _rowmean(doh * oh)), _colsum(dres * oh)


LANE = 128
_TILE_GROUPS = [sorted({LANE * j // GRP, (LANE * j + LANE - 1) // GRP}) for j in range(BW // LANE)]


def _tile(j):
    return slice(LANE * j, LANE * (j + 1))


def _low_lanes():
    return lax.broadcasted_iota(jnp.int32, (1, LANE), 1) < GRP - LANE


def _by_group(fn):
    tiles = []
    for j, groups in enumerate(_TILE_GROUPS):
        if len(groups) == 1:
            tiles.append(fn(groups[0], j))
        else:
            tiles.append(jnp.where(_low_lanes(), fn(groups[0], j), fn(groups[1], j)))
    return jnp.concatenate(tiles, axis=1)


def _sgu_chunk(vn_b, ws_ref, bmap_ref):
    return bmap_ref[...] + _by_group(lambda h, j: _dot(ws_ref[h], vn_b[:, _tile(j)]))


def _shift_copies(buf, sh):
    n = buf.shape[0] - 8
    for b in range(1, 8):
        sh[b - 1, pl.ds(0, n), :] = buf[pl.ds(b, n), :]


def _loop_rows(rows, step, fn, carry=0, unrolled=True):
    if unrolled:
        for r0 in range(0, rows, step):
            carry = fn(r0, carry)
        return carry

    def body(j, c):
        return fn(pl.multiple_of(j * step, step), c)
    return lax.fori_loop(0, rows // step, body, carry)


def _rows_at(buf, sh, r0, off):
    b = off % 8
    if b == 0 or sh is None:
        return buf[pl.ds(r0 + off, 32), :]
    return sh[b - 1, pl.ds(r0 + (off - b), 32), :]


def _tap_sum(buf, sh, w_ref, r0, taps, causal):
    acc = None
    for k in range(taps):
        off = HALO - (taps - 1 - k) if causal else taps - 1 - k
        term = w_ref[k:k + 1, :] * _rows_at(buf, sh, r0, off)
        acc = term if acc is None else acc + term
    return acc


def _fold8(a):
    return a[0:8] + a[8:16] + a[16:24] + a[24:32]


def _tap_grads(dv, buf, sh, acc_ref, r0, taps):
    for k in range(taps):
        acc_ref[k * 8:(k + 1) * 8, :] += _fold8(dv * _rows_at(buf, sh, r0, HALO - (taps - 1 - k)))


def _halo_spec(n, nt, reverse, tm):
    per = tm // HALO
    if reverse:
        return pl.BlockSpec((HALO, n), lambda i: (jnp.maximum((nt - 1 - i) * per - 1, 0), 0))
    return pl.BlockSpec((HALO, n), lambda i: (jnp.maximum(i * per - 1, 0), 0))


def _even_fwd(x, p, lng, lnb, ws, bmap, wc, kv, wout, pg):
    t = x.shape[0]
    tm = min(TM_FWD_EVEN, t)
    nt = t // tm

    def body(x_ref, p_ref, ph_ref, lng_ref, lnb_ref, ws_ref, bmap_ref, wc_ref, kv_ref, wout_ref, pg_ref,
             o_ref, x1_ref, ybuf, cbuf):
        i = pl.program_id(0)
        vh, _ = _ln_stats(p_ref[:, E_V:E_V + BW])
        vn = vh * lng_ref[...] + lnb_ref[...]
        for c in range(tm // CHUNK):
            sl = slice(c * CHUNK, (c + 1) * CHUNK)
            sg = _sgu_chunk(vn[sl].astype(BF16), ws_ref, bmap_ref)
            gate, _ = _silu_parts(p_ref[sl, E_GATE:E_GATE + BW])
            ybuf[sl, 0:BW] = (p_ref[sl, E_U:E_U + BW] * sg * gate).astype(BF16)

        cbuf[0:HALO] = jnp.where(i > 0, ph_ref[:, E_CG:E_CG + BW] * ph_ref[:, E_XIN:E_XIN + BW], 0.0)
        cbuf[HALO:HALO + tm] = p_ref[:, E_CG:E_CG + BW] * p_ref[:, E_XIN:E_XIN + BW]
        for r0 in range(0, tm, 32):
            sl = slice(r0, r0 + 32)
            cv = _tap_sum(cbuf, None, wc_ref, r0, 3, True)
            gate, _ = _silu_parts(p_ref[sl, E_GATE + BW:E_GATE + 2 * BW])
            ybuf[sl, BW:2 * BW] = (p_ref[sl, E_BG:E_BG + BW] * cv * gate).astype(BF16)

        for h in range(NH):
            qs = slice(E_Q + h * HD, E_Q + (h + 1) * HD)
            _, yx = _attn_head(p_ref[:, qs].astype(BF16), kv_ref[:, h * HD:(h + 1) * HD],
                               kv_ref[:, XA + h * HD:XA + (h + 1) * HD])
            gs = slice(E_GATE + 2 * BW + h * HD, E_GATE + 2 * BW + (h + 1) * HD)
            gate, _ = _silu_parts(p_ref[:, gs])
            ybuf[:, 2 * BW + h * HD:2 * BW + (h + 1) * HD] = (yx * gate).astype(BF16)

        o = _dot(ybuf[...], wout_ref[...])
        o_ref[...] = o
        x1_ref[...] = _rms_residual(x_ref[...], o, pg_ref[...])

    tile = lambda n: pl.BlockSpec((tm, n), lambda i: (i, 0))
    return pl.pallas_call(
        body, name="even_fwd", grid=(nt,),
        in_specs=[tile(D), tile(EVEN_IN), _halo_spec(EVEN_IN, nt, False, tm)] + [_whole()] * 8,
        out_specs=[tile(D), tile(D)],
        out_shape=[S((t, D), F32), S((t, D), F32)],
        scratch_shapes=[pltpu.VMEM((tm, MIX), BF16), pltpu.VMEM((tm + HALO, BW), F32)],
        compiler_params=_params(("arbitrary",)),
    )(x, p, p, lng, lnb, ws, bmap, wc, kv, wout, pg)


def _even_bwd(dres, o, p, lng, lnb, ws, wst, bmap, wc, kv, wout, pg):
    t = dres.shape[0]
    tm = min(TM_BWD_EVEN, t)
    nt = t // tm

    def body(dres_ref, o_ref, p_ref, ph_ref, lng_ref, lnb_ref, ws_ref, wst_ref, bmap_ref, wc_ref, kv_ref, wout_ref,
             pg_ref, dp_ref, y_ref, do_ref, dpg_ref, dws_ref, dbs_ref, dlng_ref, dlnb_ref, dwc_ref, dkv_ref,
             dy, cbuf, gbuf, dconv, carry, dvn, dbmap, wacc):
        i = pl.program_id(0)
        ti = nt - 1 - i
        masks = _group_masks()

        @pl.when(i == 0)
        def _():
            for ref in (dpg_ref, dws_ref, dlng_ref, dlnb_ref, dkv_ref, dbmap, wacc):
                ref[...] = jnp.zeros_like(ref)

        do, dpg = _rms_post_bwd(dres_ref[...], o_ref[...], pg_ref[...])
        dpg_ref[...] += dpg
        do_b = do.astype(BF16)
        do_ref[...] = do_b
        dy[...] = _dot_nt(do_b, wout_ref[...])

        vh, rs = _ln_stats(p_ref[:, E_V:E_V + BW])
        vn = vh * lng_ref[...] + lnb_ref[...]
        for c in range(tm // CHUNK):
            sl = slice(c * CHUNK, (c + 1) * CHUNK)
            vn_b = vn[sl].astype(BF16)
            sg = _sgu_chunk(vn_b, ws_ref, bmap_ref)
            u = p_ref[sl, E_U:E_U + BW]
            gate, dgate = _silu_parts(p_ref[sl, E_GATE:E_GATE + BW])
            dyc = dy[sl, 0:BW]
            ya = u * sg
            y_ref[sl, 0:BW] = (ya * gate).astype(BF16)
            dp_ref[sl, E_GATE:E_GATE + BW] = (dyc * ya * dgate).astype(BF16)
            dya = dyc * gate
            dp_ref[sl, E_U:E_U + BW] = (dya * sg).astype(BF16)
            dsg = dya * u
            dbmap[...] += dsg
            dsg_b = dsg.astype(BF16)
            for h in range(NH):
                total = None
                for j, heads in enumerate(_TILE_GROUPS):
                    if h in heads:
                        d_t = dsg_b[:, _tile(j)]
                        if len(heads) == 2:
                            d_t = jnp.where(_low_lanes() == (h == heads[0]), d_t, jnp.zeros_like(d_t))
                        part = _dot_nt(d_t, vn_b[:, _tile(j)])
                        total = part if total is None else total + part
                dws_ref[h] += total
            dvn[sl, :] = _by_group(lambda h, j: _dot(wst_ref[h], dsg_b[:, _tile(j)]))
        dn = dvn[...]
        dlng_ref[...] += _colsum(dn * vh)
        dlnb_ref[...] += _colsum(dn)
        dp_ref[:, E_V:E_V + BW] = _ln_bwd(dn, vh, rs, lng_ref[...]).astype(BF16)

        cbuf[0:HALO] = jnp.where(ti > 0, ph_ref[:, E_CG:E_CG + BW] * ph_ref[:, E_XIN:E_XIN + BW], 0.0)
        cbuf[HALO:HALO + tm] = p_ref[:, E_CG:E_CG + BW] * p_ref[:, E_XIN:E_XIN + BW]
        for r0 in range(0, tm, 32):
            sl = slice(r0, r0 + 32)
            cv = _tap_sum(cbuf, None, wc_ref, r0, 3, True)
            gate, dgate = _silu_parts(p_ref[sl, E_GATE + BW:E_GATE + 2 * BW])
            bg = p_ref[sl, E_BG:E_BG + BW]
            dyc = dy[sl, BW:2 * BW]
            yb = bg * cv
            y_ref[sl, BW:2 * BW] = (yb * gate).astype(BF16)
            dp_ref[sl, E_GATE + BW:E_GATE + 2 * BW] = (dyc * yb * dgate).astype(BF16)
            dyb = dyc * gate
            dp_ref[sl, E_BG:E_BG + BW] = (dyb * cv).astype(BF16)
            dconv[sl, :] = dyb * bg
        gbuf[0:tm] = dconv[...]
        gbuf[tm:tm + HALO] = jnp.where(i > 0, carry[...], 0.0)
        carry[...] = dconv[0:HALO]
        for r0 in range(0, tm, 32):
            sl = slice(r0, r0 + 32)
            _tap_grads(dconv[sl, :], cbuf, None, wacc, r0, 3)
            dc = _tap_sum(gbuf, None, wc_ref, r0, 3, False)
            dp_ref[sl, E_CG:E_CG + BW] = (dc * p_ref[sl, E_XIN:E_XIN + BW]).astype(BF16)
            dp_ref[sl, E_XIN:E_XIN + BW] = (dc * p_ref[sl, E_CG:E_CG + BW]).astype(BF16)

        for h in range(NH):
            qs = slice(E_Q + h * HD, E_Q + (h + 1) * HD)
            ks = slice(h * HD, (h + 1) * HD)
            vs = slice(XA + h * HD, XA + (h + 1) * HD)
            gs = slice(E_GATE + 2 * BW + h * HD, E_GATE + 2 * BW + (h + 1) * HD)
            ys = slice(2 * BW + h * HD, 2 * BW + (h + 1) * HD)
            q_b = p_ref[:, qs].astype(BF16)
            prob, yx = _attn_head(q_b, kv_ref[:, ks], kv_ref[:, vs])
            gate, dgate = _silu_parts(p_ref[:, gs])
            dyc = dy[:, ys]
            y_ref[:, ys] = (yx * gate).astype(BF16)
            dp_ref[:, gs] = (dyc * yx * dgate).astype(BF16)
            dyx_b = (dyc * gate).astype(BF16)
            dprob = _dot_nt(dyx_b, kv_ref[:, vs])
            dkv_ref[:, vs] += _dot_tn(prob.astype(BF16), dyx_b)
            ds_b = (prob * (dprob - jnp.sum(dprob * prob, axis=-1, keepdims=True)) * (HD ** -0.5)).astype(BF16)
            dp_ref[:, qs] = _dot(ds_b, kv_ref[:, ks]).astype(BF16)
            dkv_ref[:, ks] += _dot_tn(ds_b, q_b)

        @pl.when(i == nt - 1)
        def _():
            for h in range(NH):
                dbs_ref[:, h * HD:(h + 1) * HD] = jnp.broadcast_to(
                    jnp.sum(dbmap[...] * masks[h], axis=-1, keepdims=True), (CHUNK, HD))
            for k in range(3):
                dwc_ref[k:k + 1, :] = _colsum(wacc[k * 8:(k + 1) * 8, :])
            dwc_ref[3:8, :] = jnp.zeros((5, BW), F32)
            causal = (lax.broadcasted_iota(jnp.int32, (CHUNK, CHUNK), 0)
                      >= lax.broadcasted_iota(jnp.int32, (CHUNK, CHUNK), 1))
            for h in range(NH):
                dws_ref[h] = jnp.where(causal, dws_ref[h], 0.0)

    rtile = lambda n: pl.BlockSpec((tm, n), lambda i: (nt - 1 - i, 0))
    outs = [S((t, EVEN_IN), BF16), S((t, MIX), BF16), S((t, D), BF16), S((1, D), F32), S((NH, CHUNK, CHUNK), F32),
            S((CHUNK, NH * HD), F32), S((1, BW), F32), S((1, BW), F32), S((8, BW), F32), S((N_MEM, 2 * XA), F32)]
    return pl.pallas_call(
        body, name="even_bwd", grid=(nt,),
        in_specs=[rtile(D), rtile(D), rtile(EVEN_IN), _halo_spec(EVEN_IN, nt, True, tm)] + [_whole()] * 9,
        out_specs=[rtile(EVEN_IN), rtile(MIX), rtile(D)] + [_full(s.shape) for s in outs[3:]],
        out_shape=outs,
        scratch_shapes=[pltpu.VMEM((tm, MIX), F32), pltpu.VMEM((tm + HALO, BW), F32), pltpu.VMEM((tm + HALO, BW), F32),
                        pltpu.VMEM((tm, BW), F32), pltpu.VMEM((HALO, BW), F32), pltpu.VMEM((tm, BW), F32),
                        pltpu.VMEM((CHUNK, BW), F32), pltpu.VMEM((3 * 8, BW), F32)],
        compiler_params=_params(("arbitrary",)),
    )(dres, o, p, p, lng, lnb, ws, wst, bmap, wc, kv, wout, pg)


def _pool_causal_levels(za, zb, zc, zd, tm):
    n = tm + HALO
    zb[pl.ds(8, n - 8), :] = za[pl.ds(8, n - 8), :] + za[pl.ds(7, n - 8), :]
    zc[pl.ds(16, n - 16), :] = zb[pl.ds(16, n - 16), :] + zb[pl.ds(14, n - 16), :]
    zd[pl.ds(24, n - 24), :] = zc[pl.ds(24, n - 24), :] + zc[pl.ds(20, n - 24), :]


def _pool_causal(za, zb, zc, zd, tm):
    _pool_causal_levels(za, zb, zc, zd, tm)
    s16 = zd[pl.ds(HALO, tm), :] + zd[pl.ds(HALO - 8, tm), :]
    return zb[pl.ds(HALO, tm), :], zc[pl.ds(HALO, tm), :], zd[pl.ds(HALO, tm), :], s16


def _pool_anticausal_levels(ea, eb, ec, ed, tm):
    n = tm + HALO
    eb[pl.ds(0, n - 8), :] = ea[pl.ds(0, n - 8), :] + ea[pl.ds(1, n - 8), :]
    ec[pl.ds(0, n - 16), :] = eb[pl.ds(0, n - 16), :] + eb[pl.ds(2, n - 16), :]
    ed[pl.ds(0, n - 24), :] = ec[pl.ds(0, n - 24), :] + ec[pl.ds(4, n - 24), :]


def _pool_weights(t0, masks, rows):
    del masks
    tf = (t0 + lax.broadcasted_iota(jnp.int32, (rows, 1), 0) + 1).astype(F32)
    inv = [jnp.broadcast_to(1.0 / jnp.minimum(tf, float(win)), (rows, LANE)) for win in POOL_WINDOWS]
    return _by_group(lambda g, j: inv[g])


_HALVES = (slice(0, BW // 2), slice(BW // 2, BW))


def _mix4(masks, parts):
    del masks
    return _by_group(lambda g, j: parts[g][:, _tile(j)])


def _odd_fwd(x1, tgt, p, wbd, cscale, dww, dwb, lng, lnb, wpw, pwb, kv, wout, pg):
    t = x1.shape[0]
    tm = min(TM_FWD_ODD, t)
    nt = t // tm

    def body(x_ref, tgt_ref, p_ref, ph_ref, wbd_ref, cs_ref, dww_ref, dwb_ref, lng_ref, lnb_ref, wpw_ref, pwb_ref,
             kv_ref, wout_ref, pg_ref, o_ref, dres_ref, loss_ref, conv_ref, ybuf, za, zb, zc, zd, gbuf, lacc, gsh):
        i = pl.program_id(0)
        masks = _group_masks()

        @pl.when(i == 0)
        def _():
            lacc[...] = jnp.zeros_like(lacc)

        z = p_ref[:, O_ZC:O_ZC + BW]
        za[0:HALO] = jnp.where(i > 0, ph_ref[:, O_ZC:O_ZC + BW], 0.0)
        za[HALO:HALO + tm] = z
        pooled = _mix4(masks, _pool_causal(za, zb, zc, zd, tm)) * _pool_weights(i * tm, masks, tm) - z
        pooled_b = pooled.astype(BF16)
        for hs in _HALVES:
            gate, _ = _silu_parts(p_ref[:, O_GATE + hs.start:O_GATE + hs.stop])
            ybuf[:, hs] = (_dot(pooled_b[:, hs], wbd_ref[hs, hs]) * cs_ref[:, hs] * gate).astype(BF16)

        gbuf[0:HALO] = jnp.where(i > 0, ph_ref[:, O_GA:O_GA + BW] * _sigmoid(ph_ref[:, O_GB:O_GB + BW]), 0.0)
        gbuf[HALO:HALO + tm] = p_ref[:, O_GA:O_GA + BW] * _sigmoid(p_ref[:, O_GB:O_GB + BW])
        _shift_copies(gbuf, gsh)
        def conv_rows(r0, carry):
            conv_ref[pl.ds(r0, 32), :] = _tap_sum(gbuf, gsh, dww_ref, r0, CONF_K, True) + dwb_ref[...]
            return carry

        _loop_rows(tm, 32, conv_rows)
        zh, _ = _ln_stats(conv_ref[...])
        zn = zh * lng_ref[...] + lnb_ref[...]
        yd = _dot((zn * _sigmoid(zn)).astype(BF16), wpw_ref[...]) + pwb_ref[...]
        gate, _ = _silu_parts(p_ref[:, O_GATE + BW:O_GATE + 2 * BW])
        ybuf[:, BW:2 * BW] = (yd * gate).astype(BF16)

        for h in range(NH):
            qs = slice(O_Q + h * HD, O_Q + (h + 1) * HD)
            _, yx = _attn_head(p_ref[:, qs].astype(BF16), kv_ref[:, h * HD:(h + 1) * HD],
                               kv_ref[:, XA + h * HD:XA + (h + 1) * HD])
            gs = slice(O_GATE + 2 * BW + h * HD, O_GATE + 2 * BW + (h + 1) * HD)
            gate, _ = _silu_parts(p_ref[:, gs])
            ybuf[:, 2 * BW + h * HD:2 * BW + (h + 1) * HD] = (yx * gate).astype(BF16)

        o = _dot(ybuf[...], wout_ref[...])
        o_ref[...] = o
        err = _rms_residual(x_ref[...], o, pg_ref[...]) - tgt_ref[...]
        lacc[...] += _colsum(err * err)
        dres_ref[...] = err * (1.0 / D)

        @pl.when(i == nt - 1)
        def _():
            loss_ref[...] = jnp.full((1, HD), jnp.sum(lacc[...]) * (0.5 / D), F32)

    tile = lambda n: pl.BlockSpec((tm, n), lambda i: (i, 0))
    ext = pltpu.VMEM((tm + HALO, BW), F32)
    return pl.pallas_call(
        body, name="odd_fwd", grid=(nt,),
        in_specs=[tile(D), tile(D), tile(ODD_IN), _halo_spec(ODD_IN, nt, False, tm)] + [_whole()] * 11,
        out_specs=[tile(D), tile(D), _full((1, HD)), tile(BW)],
        out_shape=[S((t, D), F32), S((t, D), F32), S((1, HD), F32), S((t, BW), F32)],
        scratch_shapes=[pltpu.VMEM((tm, MIX), BF16), ext, ext, ext, ext, ext,
                        pltpu.VMEM((1, D), F32), pltpu.VMEM((7, tm + HALO, BW), F32)],
        compiler_params=_params(("arbitrary",)),
    )(x1, tgt, p, p, wbd, cscale, dww, dwb, lng, lnb, wpw, pwb, kv, wout, pg)


def _odd_bwd(dres, o, p, conv, wbd, cscale, dww, dwb, lng, lnb, wpw, pwb, kv, wout, pg):
    t = dres.shape[0]
    tm = min(TM_BWD_ODD, t)
    nt = t // tm

    def body(dres_ref, o_ref, p_ref, ph_ref, conv_ref, wbd_ref, cs_ref, dww_ref, dwb_ref, lng_ref, lnb_ref, wpw_ref,
             pwb_ref, kv_ref, wout_ref, pg_ref, dp_ref, y_ref, do_ref, dpg_ref, dwbd_ref, dcs_ref, ddww_ref, ddwb_ref,
             dlng_ref, dlnb_ref, dwpw_ref, dpwb_ref, dkv_ref,
             dy, za, zb, zc, zd, tmp, carry_e, carry_d, wacc, shifted, t1, b1, b2):
        gbuf, hbuf = za, zb
        i = pl.program_id(0)
        ti = nt - 1 - i
        masks = _group_masks()

        @pl.when(i == 0)
        def _():
            for ref in (dpg_ref, dwbd_ref, dcs_ref, ddwb_ref, dlng_ref, dlnb_ref, dwpw_ref, dpwb_ref, dkv_ref, wacc):
                ref[...] = jnp.zeros_like(ref)

        def post_norm_rows(r0, acc):
            sl = pl.ds(r0, RB)
            ov, dv = o_ref[sl, :], dres_ref[sl, :]
            r = lax.rsqrt(_rowmean(ov * ov) + EPS)
            oh = ov * r
            doh = dv * pg_ref[...]
            do_ref[sl, :] = (r * (doh - oh * _rowmean(doh * oh))).astype(BF16)
            return acc + dv * oh

        dpg_ref[...] += _colsum(_loop_rows(tm, RB, post_norm_rows, jnp.zeros((RB, D), F32)))
        dy[...] = _dot_nt(do_ref[...], wout_ref[...])

        za[0:HALO] = jnp.where(ti > 0, ph_ref[:, O_ZC:O_ZC + BW], 0.0)
        za[HALO:HALO + tm] = p_ref[:, O_ZC:O_ZC + BW]
        _pool_causal_levels(za, zb, zc, zd, tm)

        def pooled_rows(r0, carry):
            sl = pl.ds(r0, RB)
            at = lambda ref, back=0: ref[pl.ds(HALO + r0 - back, RB), :]
            inv = _pool_weights(ti * tm + r0, masks, RB)
            sums = (at(zb), at(zc), at(zd), at(zd) + at(zd, 8))
            b1[sl, :] = (_mix4(masks, sums) * inv - p_ref[sl, O_ZC:O_ZC + BW]).astype(BF16)
            return carry

        _loop_rows(tm, RB, pooled_rows)
        for hs in _HALVES:
            t1[:, hs] = _dot(b1[:, hs], wbd_ref[hs, hs])

        def pool_gate_rows(r0, acc):
            sl = pl.ds(r0, RB)
            pm = t1[sl, :]
            gate, dgate = _silu_parts(p_ref[sl, O_GATE:O_GATE + BW])
            dyc = dy[sl, 0:BW]
            yc = pm * cs_ref[...]
            y_ref[sl, 0:BW] = (yc * gate).astype(BF16)
            dp_ref[sl, O_GATE:O_GATE + BW] = (dyc * yc * dgate).astype(BF16)
            dyc = dyc * gate
            b2[sl, :] = (dyc * cs_ref[...]).astype(BF16)
            return acc + dyc * pm

        dcs_ref[...] += _colsum(_loop_rows(tm, RB, pool_gate_rows, jnp.zeros((RB, BW), F32)))
        for hs in _HALVES:
            dwbd_ref[hs, hs] += _dot_tn(b1[:, hs], b2[:, hs])
            t1[:, hs] = _dot_nt(b2[:, hs], wbd_ref[hs, hs])

        def weighted_rows(r0, carry):
            sl = pl.ds(r0, RB)
            za[sl, :] = t1[sl, :] * _pool_weights(ti * tm + r0, masks, RB)
            return carry

        _loop_rows(tm, RB, weighted_rows)
        za[tm:tm + HALO] = jnp.where(i > 0, carry_e[...], 0.0)
        carry_e[...] = za[0:HALO]
        _pool_anticausal_levels(za, zb, zc, zd, tm)

        def pool_back_rows(r0, carry):
            sl = pl.ds(r0, RB)
            ahead = lambda ref, fwd=0: ref[pl.ds(r0 + fwd, RB), :]
            sums = (ahead(zb), ahead(zc), ahead(zd), ahead(zd) + ahead(zd, 8))
            dp_ref[sl, O_ZC:O_ZC + BW] = (_mix4(masks, sums) - t1[sl, :]).astype(BF16)
            return carry

        _loop_rows(tm, RB, pool_back_rows)

        gbuf[0:HALO] = jnp.where(ti > 0, ph_ref[:, O_GA:O_GA + BW] * _sigmoid(ph_ref[:, O_GB:O_GB + BW]), 0.0)

        def glu_rows(r0, carry):
            sl = pl.ds(r0, RB)
            gbuf[pl.ds(HALO + r0, RB), :] = p_ref[sl, O_GA:O_GA + BW] * _sigmoid(p_ref[sl, O_GB:O_GB + BW])
            zh, _ = _ln_stats(conv_ref[sl, :])
            zn = zh * lng_ref[...] + lnb_ref[...]
            b1[sl, :] = (zn * _sigmoid(zn)).astype(BF16)
            return carry

        _loop_rows(tm, RB, glu_rows)
        _shift_copies(gbuf, shifted)
        t1[...] = _dot(b1[...], wpw_ref[...])

        def conf_gate_rows(r0, acc):
            sl = pl.ds(r0, RB)
            yd = t1[sl, :] + pwb_ref[...]
            gate, dgate = _silu_parts(p_ref[sl, O_GATE + BW:O_GATE + 2 * BW])
            dyc = dy[sl, BW:2 * BW]
            y_ref[sl, BW:2 * BW] = (yd * gate).astype(BF16)
            dp_ref[sl, O_GATE + BW:O_GATE + 2 * BW] = (dyc * yd * dgate).astype(BF16)
            dyd = dyc * gate
            b2[sl, :] = dyd.astype(BF16)
            return acc + dyd

        dpwb_ref[...] += _colsum(_loop_rows(tm, RB, conf_gate_rows, jnp.zeros((RB, BW), F32)))
        dwpw_ref[...] += _dot_tn(b1[...], b2[...])
        t1[...] = _dot_nt(b2[...], wpw_ref[...])

        def norm_back_rows(r0, accs):
            sl = pl.ds(r0, RB)
            zh, rs = _ln_stats(conv_ref[sl, :])
            _, dsilu = _silu_parts(zh * lng_ref[...] + lnb_ref[...])
            dzn = t1[sl, :] * dsilu
            dzd = _ln_bwd(dzn, zh, rs, lng_ref[...])
            tmp[sl, :] = dzd
            hbuf[sl, :] = dzd
            return accs[0] + dzn * zh, accs[1] + dzn, accs[2] + dzd

        zero = jnp.zeros((RB, BW), F32)
        acc_g, acc_b, acc_d = _loop_rows(tm, RB, norm_back_rows, (zero, zero, zero))
        dlng_ref[...] += _colsum(acc_g)
        dlnb_ref[...] += _colsum(acc_b)
        ddwb_ref[...] += _colsum(acc_d)
        hbuf[tm:tm + HALO] = jnp.where(i > 0, carry_d[...], 0.0)
        carry_d[...] = tmp[0:HALO]
        def tap_grad_rows(r0, carry):
            _tap_grads(tmp[pl.ds(r0, 32), :], gbuf, shifted, wacc, r0, CONF_K)
            return carry

        _loop_rows(tm, 32, tap_grad_rows, unrolled=False)
        _shift_copies(hbuf, shifted)

        def conv_back_rows(r0, carry):
            sl = pl.ds(r0, 32)
            dzg = _tap_sum(hbuf, shifted, dww_ref, r0, CONF_K, False)
            sgb = _sigmoid(p_ref[sl, O_GB:O_GB + BW])
            dp_ref[sl, O_GA:O_GA + BW] = (dzg * sgb).astype(BF16)
            dp_ref[sl, O_GB:O_GB + BW] = (dzg * p_ref[sl, O_GA:O_GA + BW] * sgb * (1.0 - sgb)).astype(BF16)
            return carry

        _loop_rows(tm, 32, conv_back_rows, unrolled=False)

        for h in range(NH):
            qs = slice(O_Q + h * HD, O_Q + (h + 1) * HD)
            ks = slice(h * HD, (h + 1) * HD)
            vs = slice(XA + h * HD, XA + (h + 1) * HD)
            gs = slice(O_GATE + 2 * BW + h * HD, O_GATE + 2 * BW + (h + 1) * HD)
            ys = slice(2 * BW + h * HD, 2 * BW + (h + 1) * HD)
            q_b = p_ref[:, qs].astype(BF16)
            prob, yx = _attn_head(q_b, kv_ref[:, ks], kv_ref[:, vs])
            gate, dgate = _silu_parts(p_ref[:, gs])
            dyc = dy[:, ys]
            y_ref[:, ys] = (yx * gate).astype(BF16)
            dp_ref[:, gs] = (dyc * yx * dgate).astype(BF16)
            dyx_b = (dyc * gate).astype(BF16)
            dprob = _dot_nt(dyx_b, kv_ref[:, vs])
            dkv_ref[:, vs] += _dot_tn(prob.astype(BF16), dyx_b)
            ds_b = (prob * (dprob - jnp.sum(dprob * prob, axis=-1, keepdims=True)) * (HD ** -0.5)).astype(BF16)
            dp_ref[:, qs] = _dot(ds_b, kv_ref[:, ks]).astype(BF16)
            dkv_ref[:, ks] += _dot_tn(ds_b, q_b)

        @pl.when(i == nt - 1)
        def _():
            for k in range(CONF_K):
                ddww_ref[k:k + 1, :] = _colsum(wacc[k * 8:(k + 1) * 8, :])
            ddww_ref[CONF_K:CONF_K + 1, :] = jnp.zeros((1, BW), F32)

    rtile = lambda n: pl.BlockSpec((tm, n), lambda i: (nt - 1 - i, 0))
    outs = [S((t, ODD_IN), BF16), S((t, MIX), BF16), S((t, D), BF16), S((1, D), F32), S((BW, BW), F32),
            S((1, BW), F32), S((CONF_K + 1, BW), F32), S((1, BW), F32), S((1, BW), F32), S((1, BW), F32),
            S((BW, BW), F32), S((1, BW), F32), S((N_MEM, 2 * XA), F32)]
    ext = pltpu.VMEM((tm + HALO, BW), F32)
    return pl.pallas_call(
        body, name="odd_bwd", grid=(nt,),
        in_specs=[rtile(D), rtile(D), rtile(ODD_IN), _halo_spec(ODD_IN, nt, True, tm), rtile(BW)] + [_whole()] * 11,
        out_specs=[rtile(ODD_IN), rtile(MIX), rtile(D)] + [_full(s.shape) for s in outs[3:]],
        out_shape=outs,
        scratch_shapes=[pltpu.VMEM((tm, MIX), F32), ext, ext, ext, ext, pltpu.VMEM((tm, BW), F32),
                        pltpu.VMEM((HALO, BW), F32), pltpu.VMEM((HALO, BW), F32), pltpu.VMEM((CONF_K * 8, BW), F32),
                        pltpu.VMEM((7, tm + HALO, BW), F32),
                        pltpu.VMEM((tm, BW), F32), pltpu.VMEM((tm, BW), BF16), pltpu.VMEM((tm, BW), BF16)],
        compiler_params=pltpu.CompilerParams(dimension_semantics=("arbitrary",),
                                             vmem_limit_bytes=VMEM_LIMIT_ODD_BWD_V7X),
    )(dres, o, p, p, conv, wbd, cscale, dww, dwb, lng, lnb, wpw, pwb, kv, wout, pg)


def _pick_rows(n):
    for rows in (640, 2432, 1024, 768):
        if n % rows == 0:
            return rows
    return n


def _pad_rows(a, rows):
    return jnp.pad(a, ((0, rows - a.shape[0]), (0, 0)))


def _step(x, mem, tgt, ex):
    t = x.shape[0]
    tm = min(512, t)
    w, deps = ex.first()
    causal = jnp.tril(jnp.ones((CHUNK, CHUNK), bool))
    ws = jnp.where(causal[None], w["even_a_ws"], 0.0).astype(BF16)
    wst = jnp.transpose(ws, (0, 2, 1))
    bmap = jnp.repeat(w["even_a_bs"].T, GRP, axis=1)
    wc = _pad_rows(w["even_b_conv"], 8)
    wbd = jax.scipy.linalg.block_diag(*[w["odd_c_wgrp"][g] for g in range(NH)]).astype(BF16)
    dww = _pad_rows(w["odd_d_dw_w"], CONF_K + 1)
    tk = min(1024, t)
    zeros = jnp.zeros_like(mem)

    p_e, h_e = _rms_matmul(x, w["even_pre_g"], w["even_w_in"], tm=tm, name="in_even", transposed=True, deps=deps)
    w.update(ex.even_rest(h_e))
    kv_e, memn_e = _rms_matmul(mem, w["even_mem_g"], w["even_w_kv"], tm=N_MEM, name="kv_even", out_dtype=BF16)
    even_args = (w["even_a_ln_g"], w["even_a_ln_b"], ws)
    o_e, x1 = _even_fwd(x, p_e, *even_args, bmap, wc, kv_e, w["even_w_out"], w["even_post_g"])
    w.update(ex.odd(o_e))
    kv_o, memn_o = _rms_matmul(mem, w["odd_mem_g"], w["odd_w_kv"], tm=N_MEM, name="kv_odd", out_dtype=BF16)
    p_o, h_o = _rms_matmul(x1, w["odd_pre_g"], w["odd_w_in"], tm=tm, name="in_odd", transposed=True)
    odd_args = (wbd, w["odd_c_scale"], dww, w["odd_d_dw_b"], w["odd_d_ln_g"], w["odd_d_ln_b"], w["odd_d_pw_w"],
                w["odd_d_pw_b"], kv_o, w["odd_w_out"], w["odd_post_g"])
    o_o, dres, loss, conv_o = _odd_fwd(x1, tgt, p_o, *odd_args)

    g = {}
    (dp_o, y_o, do_o, post_g_o, dwbd, g["odd_c_scale"], ddww, g["odd_d_dw_b"], g["odd_d_ln_g"], g["odd_d_ln_b"],
     dwpw, g["odd_d_pw_b"], dkv_o) = _odd_bwd(dres, o_o, p_o, conv_o, *odd_args)
    g["odd_post_g"] = post_g_o
    g["odd_d_dw_w"] = ddww[:CONF_K]
    dkv_o = dkv_o.astype(BF16)
    deps = ex.send("odd_rest", {
        "odd_w_out": _tn_matmul(y_o, do_o, tmc=MIX, tk=tk, out_dtype=BF16, name="dw_out_odd"),
        "odd_w_kv": _tn_matmul(memn_o, dkv_o, tmc=D, tk=N_MEM, out_dtype=BF16, name="dw_kv_odd"),
        "odd_d_pw_w": dwpw.astype(BF16), "loss": loss,
        "odd_c_wgrp": jnp.concatenate([dwbd[i * GRP:(i + 1) * GRP, i * GRP:(i + 1) * GRP] for i in range(NH)])})
    deps = ex.send("odd_in", {"odd_w_in": _tn_matmul(dp_o, h_o, tmc=_pick_rows(ODD_IN), tk=tk, out_dtype=BF16,
                                                     name="dw_in_odd", deps=deps)})
    dx1, g["odd_pre_g"] = _nt_matmul_rms_bwd(dp_o, w["odd_w_in"], x1, w["odd_pre_g"], dres, tm=tm,
                                             name="dx_odd", transposed=True, deps=deps)
    _, g["odd_mem_g"] = _nt_matmul_rms_bwd(dkv_o, w["odd_w_kv"], mem, w["odd_mem_g"], zeros, tm=N_MEM,
                                           name="dmem_odd")

    (dp_e, y_e, do_e, post_g_e, dws, dbs, ln_g_e, ln_b_e, dwc, dkv_e) = _even_bwd(
        dx1, o_e, p_e, *even_args, wst, bmap, wc, kv_e, w["even_w_out"], w["even_post_g"])
    g["even_b_conv"] = dwc[:3]
    dkv_e = dkv_e.astype(BF16)
    deps = ex.send("even_rest", {
        "even_w_out": _tn_matmul(y_e, do_e, tmc=MIX, tk=tk, out_dtype=BF16, name="dw_out_even"),
        "even_w_kv": _tn_matmul(memn_e, dkv_e, tmc=D, tk=N_MEM, out_dtype=BF16, name="dw_kv_even"),
        "even_a_ln_g": ln_g_e, "even_a_ln_b": ln_b_e,
        "even_a_ws": dws.reshape(NH * CHUNK, CHUNK), "even_a_bs": dbs[:, ::HD].T})
    g["even_w_in"] = _tn_matmul(dp_e, h_e, tmc=_pick_rows(EVEN_IN), tk=tk, out_dtype=BF16, name="dw_in_even",
                                deps=deps)
    deps = ex.send("even_in", g)
    grad_x, pre_g_e = _nt_matmul_rms_bwd(dp_e, w["even_w_in"], x, w["even_pre_g"], dx1, tm=tm,
                                         name="dx_even", transposed=True, deps=deps)
    _, mem_g_e = _nt_matmul_rms_bwd(dkv_e, w["even_w_kv"], mem, w["even_mem_g"], zeros, tm=N_MEM, name="dmem_even",
                                    deps=(grad_x,))
    deps = ex.send("even_gains", {"even_pre_g": pre_g_e, "even_mem_g": mem_g_e, "even_post_g": post_g_e})
    return grad_x, deps


def _place():
    return lax.axis_index("x"), lax.axis_index("y"), lax.axis_index("c")


def _index(px, py, pc):
    return 4 * px + 2 * py + pc


_COPIES = N_DEV - 1


def _all_gather(arrs, name):
    n = len(arrs)

    def body(*refs):
        ins, outs = refs[:n], refs[n:2 * n]
        send_sems, recv_sems, local_sems = refs[2 * n:]
        x, y, c = _place()
        me, sibling = (x, y, c), (x, y, 1 - c)
        chips = [(1 - x, y), (x, 1 - y), (1 - x, 1 - y)]

        def copy(a, k, block, to, src=None):
            dst = outs[a].at[_index(*block)]
            return pltpu.make_async_remote_copy(
                src_ref=dst if src is None else src, dst_ref=dst, send_sem=send_sems.at[a * _COPIES + k],
                recv_sem=recv_sems.at[a * _COPIES + k], device_id=to, device_id_type=MESH)

        mine = [pltpu.make_async_copy(ins[a], outs[a].at[_index(*me)], local_sems.at[a]) for a in range(n)]
        first = []
        for a in range(n):
            mine[a].start()
            first.append(copy(a, 0, me, sibling, src=ins[a]))
            first += [copy(a, 1 + j, me, (*chip, c), src=ins[a]) for j, chip in enumerate(chips)]
        for cp in first:
            cp.start()
        passed = []
        for j, chip in enumerate(chips):
            for a in range(n):
                copy(a, 1 + j, (*chip, c), me).wait_recv()
                passed.append(copy(a, 4 + j, (*chip, c), sibling))
                passed[-1].start()
        for a in range(n):
            copy(a, 0, sibling, me).wait_recv()
            for j, chip in enumerate(chips):
                copy(a, 4 + j, (*chip, 1 - c), me).wait_recv()
        for cp in first + passed:
            cp.wait_send()
        for cp in mine:
            cp.wait()

    return pl.pallas_call(
        body, name=name, in_specs=[_ANY] * n, out_specs=[_ANY] * n,
        out_shape=[S((N_DEV,) + a.shape, a.dtype) for a in arrs],
        scratch_shapes=[pltpu.SemaphoreType.DMA((n * _COPIES,)), pltpu.SemaphoreType.DMA((n * _COPIES,)),
                        pltpu.SemaphoreType.DMA((n,))],
    )(*arrs)


_HBM = pl.BlockSpec(memory_space=pltpu.HBM)
_SEM = pl.BlockSpec(memory_space=pltpu.SEMAPHORE)
_EFFECT = pltpu.SideEffectType.DATAFLOW_SIDE_EFFECTING


_ALL_FLIPS = [(k >> 2 & 1, k >> 1 & 1, k & 1) for k in range(1, N_DEV)]
_CHIP_FLIPS = [(1, 0, 0), (0, 1, 0), (1, 1, 0)]
_FLIPS = {"gather": _ALL_FLIPS, "scatter": _ALL_FLIPS, "gather_chips": [(0, 0, 1)] + _CHIP_FLIPS,
          "scatter_chips": _CHIP_FLIPS}


def _landing_shape(kind, a):
    return (N_DEV,) + a.shape if kind.startswith("gather") else a.shape


def _exchange_copies(kinds, srcs, lands, send_sems, recv_sems, local_sems, arriving):
    x, y, c = _place()
    mine = _index(x, y, c)
    remote, local = [], []
    for a, kind in enumerate(kinds):
        by_chip = kind == "scatter_chips"
        here = 2 * x + y if by_chip else mine
        own = srcs[a] if kind.startswith("gather") else srcs[a].at[here]
        local.append(pltpu.make_async_copy(own, lands[a].at[here], local_sems.at[a]))
        for k, (fx, fy, fc) in enumerate(_FLIPS[kind]):
            peer = (1 - x if fx else x, 1 - y if fy else y, 1 - c if fc else c)
            there = 2 * peer[0] + peer[1] if by_chip else _index(*peer)
            remote.append(pltpu.make_async_remote_copy(
                src_ref=srcs[a] if kind.startswith("gather") else srcs[a].at[there],
                dst_ref=lands[a].at[there if arriving else here],
                send_sem=send_sems.at[a * _COPIES + k], recv_sem=recv_sems.at[a * _COPIES + k],
                device_id=peer, device_id_type=MESH))
    return remote, local


def _exchange_start(items, name, deps=()):
    kinds = [kind for kind, _ in items]
    srcs = [a for _, a in items]
    n = len(items)
    lands = [lax.empty(_landing_shape(kind, a), a.dtype) for kind, a in items]

    def body(*refs):
        send_sems, recv_sems, local_sems = refs[2 * n + len(deps):2 * n + len(deps) + 3]
        remote, local = _exchange_copies(kinds, refs[:n], refs[n:2 * n], send_sems, recv_sems, local_sems, False)
        for cp in local + remote:
            cp.start()
        refs[-1][...] = jnp.zeros_like(refs[-1])

    held = [pltpu.HBM(a.shape, a.dtype) for a in srcs + lands]
    res = pl.pallas_call(
        body, name=name,
        out_shape=(pltpu.SemaphoreType.DMA((n * _COPIES,)), pltpu.SemaphoreType.DMA((n * _COPIES,)),
                   pltpu.SemaphoreType.DMA((n,)), *held, S((8, 128), F32)),
        in_specs=[_HBM] * (2 * n) + [_ANY] * len(deps),
        out_specs=(_SEM, _SEM, _SEM, *[_HBM] * (2 * n), _whole()),
        input_output_aliases={i: 3 + i for i in range(2 * n)},
        compiler_params=pltpu.CompilerParams(has_side_effects=_EFFECT),
    )(*[pltpu.with_memory_space_constraint(a, pltpu.HBM) for a in srcs + lands], *deps)
    return (kinds, res[:3], res[3:3 + 2 * n]), res[-1]


def _exchange_wait(handle, after, name):
    kinds, sems, held = handle
    n = len(kinds)

    def body(*refs):
        send_sems, recv_sems, local_sems = refs[2 * n:2 * n + 3]
        remote, local = _exchange_copies(kinds, refs[:n], refs[n:2 * n], send_sems, recv_sems, local_sems, True)
        for cp in remote:
            cp.wait_send()
            cp.wait_recv()
        for cp in local:
            cp.wait()

    res = pl.pallas_call(
        body, name=name, out_shape=[pltpu.HBM(a.shape, a.dtype) for a in held],
        in_specs=[_HBM] * (2 * n) + [_SEM] * 3 + [_ANY] * len(after), out_specs=[_HBM] * (2 * n),
        input_output_aliases={i: i for i in range(2 * n)},
        compiler_params=pltpu.CompilerParams(has_side_effects=_EFFECT),
    )(*held, *sems, *after)
    return res[n:]


_CHIPS = [(0, 0), (0, 1), (1, 0), (1, 1)]
_N_CHIPS = len(_CHIPS)


def _sibling_forward(lands, name):
    n = len(lands)

    def body(*refs):
        ins, outs = refs[:n], refs[n:2 * n]
        send_sems, recv_sems = refs[2 * n:]
        x, y, c = _place()
        sent, arriving = [], []
        for a in range(n):
            for j, (fx, fy, _) in enumerate(_CHIP_FLIPS):
                chip = (1 - x if fx else x, 1 - y if fy else y)
                sems = dict(send_sem=send_sems.at[a * 3 + j], recv_sem=recv_sems.at[a * 3 + j],
                            device_id=(x, y, 1 - c), device_id_type=MESH)
                mine, theirs = _index(*chip, c), _index(*chip, 1 - c)
                sent.append(pltpu.make_async_remote_copy(src_ref=ins[a].at[mine], dst_ref=outs[a].at[mine], **sems))
                arriving.append(pltpu.make_async_remote_copy(src_ref=ins[a].at[theirs], dst_ref=outs[a].at[theirs],
                                                             **sems))
        for cp in sent:
            cp.start()
        for cp in sent:
            cp.wait_send()
        for cp in arriving:
            cp.wait_recv()

    return pl.pallas_call(
        body, name=name, in_specs=[_ANY] * n, out_specs=[_ANY] * n,
        out_shape=[S(a.shape, a.dtype) for a in lands], input_output_aliases={a: a for a in range(n)},
        scratch_shapes=[pltpu.SemaphoreType.DMA((3 * n,)), pltpu.SemaphoreType.DMA((3 * n,))],
    )(*lands)


def _sibling_swap(arrs, name):
    n = len(arrs)

    def body(*refs):
        ins, outs = refs[:n], refs[n:2 * n]
        send_sems, recv_sems = refs[2 * n:]
        x, y, c = _place()
        copies = []
        for a in range(n):
            for q, chip in enumerate(_CHIPS):
                copies.append(pltpu.make_async_remote_copy(
                    src_ref=ins[a].at[_index(*chip, 1 - c)], dst_ref=outs[a].at[q],
                    send_sem=send_sems.at[a * _N_CHIPS + q], recv_sem=recv_sems.at[a * _N_CHIPS + q],
                    device_id=(x, y, 1 - c), device_id_type=MESH))
        for cp in copies:
            cp.start()
        for cp in copies:
            cp.wait_send()
            cp.wait_recv()

    return pl.pallas_call(
        body, name=name, in_specs=[_ANY] * n, out_specs=[_ANY] * n,
        out_shape=[S((_N_CHIPS,) + a.shape[1:], a.dtype) for a in arrs],
        scratch_shapes=[pltpu.SemaphoreType.DMA((_N_CHIPS * n,)), pltpu.SemaphoreType.DMA((_N_CHIPS * n,))],
    )(*arrs)


def _add_partials(mine, theirs, *, tr, name):
    _, r, c = mine.shape

    def body(mine_ref, theirs_ref, out_ref):
        core = lax.axis_index("c")
        own = jnp.where(core == 0, mine_ref[0].astype(F32), mine_ref[1].astype(F32))
        out_ref[0] = (own + theirs_ref[0].astype(F32)).astype(out_ref.dtype)

    return pl.pallas_call(
        body, name=name, grid=(_N_CHIPS, r // tr),
        in_specs=[pl.BlockSpec((2, tr, c), lambda q, i: (q, i, 0)), pl.BlockSpec((1, tr, c), lambda q, i: (q, i, 0))],
        out_specs=pl.BlockSpec((1, tr, c), lambda q, i: (q, i, 0)),
        out_shape=S((_N_CHIPS, r, c), mine.dtype),
        compiler_params=_params(("arbitrary", "arbitrary")),
    )(mine, theirs)


def _adamw(w, g, m, v):
    m = ADAM_B1 * m + (1.0 - ADAM_B1) * g
    v = ADAM_B2 * v + (1.0 - ADAM_B2) * (g * g)
    m_hat = m / (1.0 - ADAM_B1 ** ADAM_STEP)
    v_hat = v / (1.0 - ADAM_B2 ** ADAM_STEP)
    return -ADAM_LR * (m_hat / (jnp.sqrt(v_hat) + ADAM_EPS) + ADAM_WD * w), m, v


def _sum_devices(ref, rows):
    total = ref[0, rows, :].astype(F32)
    for s in range(1, ref.shape[0]):
        total = total + ref[s, rows, :].astype(F32)
    return total


def _adam_big(recv, w, m, v, *, tr, name):
    r, c = w.shape

    def body(recv_ref, w_ref, m_ref, v_ref, g_ref, d_ref, m2_ref, v2_ref):
        g = _sum_devices(recv_ref, slice(None))
        g_ref[...] = g
        d_ref[...], m2_ref[...], v2_ref[...] = _adamw(w_ref[...], g, m_ref[...], v_ref[...])

    blk = pl.BlockSpec((tr, c), lambda i: (i, 0))
    return pl.pallas_call(
        body, name=name, grid=(r // tr,),
        in_specs=[pl.BlockSpec((recv.shape[0], tr, c), lambda i: (0, i, 0)), blk, blk, blk],
        out_specs=[blk] * 4, out_shape=[S((r, c), F32)] * 4,
        compiler_params=_params(("arbitrary",)),
    )(recv, w, m, v)


_REPLICATED = {"even_pre_g": (0, 0, 1), "even_mem_g": (0, 8, 1), "even_post_g": (0, 16, 1),
               "even_a_ln_g": (1, 0, 1), "even_a_ln_b": (1, 8, 1),
               "even_a_ws": (2, 0, NH * CHUNK), "even_a_bs": (2, NH * CHUNK, NH),
               "odd_c_wgrp": (3, 0, NH * GRP)}
_SHARDED = {"odd_pre_g": (4, 0, 1), "odd_mem_g": (4, 8, 1), "odd_post_g": (4, 16, 1),
            "even_b_conv": (5, 0, 3), "odd_c_scale": (5, 8, 1), "odd_d_dw_w": (5, 16, CONF_K),
            "odd_d_dw_b": (5, 48, 1), "odd_d_ln_g": (5, 56, 1), "odd_d_ln_b": (5, 64, 1), "odd_d_pw_b": (5, 72, 1)}
_SMALL = {**_REPLICATED, **_SHARDED}
_SMALL_ROWS = {0: 24, 1: 16, 2: NH * CHUNK + 8, 3: NH * GRP, 4: 24, 5: 80}


def _adam_small(sources, wmv):
    names = list(_SMALL)
    ns = len(sources)

    def body(*refs):
        src = refs[:ns]
        ins = refs[ns:ns + 3 * len(names)]
        outs = refs[ns + 3 * len(names):]
        outs[-1][...] = _sum_devices(src[-1], slice(0, 1))
        for i, nm in enumerate(names):
            a, row0, rows = _SMALL[nm]
            g = _sum_devices(src[a], slice(row0, row0 + rows))
            w_ref, m_ref, v_ref = ins[3 * i:3 * i + 3]
            g_ref, d_ref, m2_ref, v2_ref = outs[4 * i:4 * i + 4]
            g_ref[...] = g
            d_ref[...], m2_ref[...], v2_ref[...] = _adamw(w_ref[...], g, m_ref[...], v_ref[...])

    flat = [t for nm in names for t in wmv[nm]]
    out_shape = [S(wmv[nm][0].shape, F32) for nm in names for _ in range(4)] + [S((1, HD), F32)]
    res = pl.pallas_call(
        body, name="adam_small", in_specs=[_whole()] * (ns + len(flat)), out_specs=[_whole()] * len(out_shape),
        out_shape=out_shape, compiler_params=_params(),
    )(*sources, *flat)
    return {nm: tuple(res[4 * i:4 * i + 4]) for i, nm in enumerate(names)}, res[-1]


_WEIGHTS = ["even_pre_g", "even_w_in", "even_a_ln_g", "even_a_ln_b", "even_a_ws", "even_a_bs", "even_b_conv",
            "even_mem_g", "even_w_kv", "even_w_out", "even_post_g", "odd_pre_g", "odd_w_in", "odd_c_wgrp",
            "odd_c_scale", "odd_d_dw_w", "odd_d_dw_b", "odd_d_ln_g", "odd_d_ln_b", "odd_d_pw_w", "odd_d_pw_b",
            "odd_mem_g", "odd_w_kv", "odd_w_out", "odd_post_g"]
_TRANSPOSED = ["even_w_in", "odd_w_in"]
_BIG = _TRANSPOSED + ["even_w_kv", "even_w_out", "odd_w_kv", "odd_w_out", "odd_d_pw_w"]
_BIG_TILE_ROWS = {"even_w_in": 400, "odd_w_in": 304, "even_w_kv": 128, "even_w_out": 128, "odd_w_kv": 128,
                  "odd_w_out": 128, "odd_d_pw_w": 96}


def _view2d(a, transposed):
    a = a[0]
    if a.ndim == 1:
        return a[None]
    if transposed:
        return a.T
    return a.reshape(-1, a.shape[-1])


def _rows8(a):
    return _pad_rows(a, -(-a.shape[0] // 8) * 8)


def _pack_rows(parts):
    return jnp.concatenate([_rows8(p) for p in parts], axis=0)


def _unshard_cols(a):
    return jnp.transpose(a, (1, 0, 2)).reshape(a.shape[1], N_DEV * a.shape[2])


def _shard_cols(a):
    return jnp.transpose(a.reshape(a.shape[0], N_DEV, a.shape[1] // N_DEV), (1, 0, 2))


def _rows_of(a):
    return a.reshape(-1, a.shape[-1])


_GROUPS = {"odd_rest": (["odd_w_out", "odd_w_kv", "odd_d_pw_w"], [3], []),
           "odd_in": (["odd_w_in"], [], []),
           "even_rest": (["even_w_out", "even_w_kv"], [1, 2], []),
           "even_in": (["even_w_in"], [], [4, 5]),
           "even_gains": ([], [0], [])}


_TWO_LEVEL = ("even_in",)


class _MeshExchange:
    def __init__(self, shard):
        self.shard = shard
        self.handles = {}

    def first(self):
        shard = self.shard
        packs = [_pack_rows([shard[nm] for nm in _SHARDED if _SHARDED[nm][0] == a]) for a in (4, 5)]
        w_in, p128, p96 = _all_gather([shard["even_w_in"].astype(BF16)] + packs, "gather_first")
        w = {nm: shard[nm] for nm in _REPLICATED}
        w["even_a_ws"] = w["even_a_ws"].reshape(NH, CHUNK, CHUNK)
        w["odd_c_wgrp"] = w["odd_c_wgrp"].reshape(NH, GRP, GRP)
        w["even_w_in"] = _rows_of(w_in)
        full_packs = {4: _unshard_cols(p128), 5: _unshard_cols(p96)}
        for nm, (a, row0, rows) in _SHARDED.items():
            w[nm] = full_packs[a][row0:row0 + rows]
        later = lambda names: [("gather_chips", shard[nm].astype(BF16)) for nm in names]
        self.handles["w_even"], token = _exchange_start(later(["even_w_kv", "even_w_out"]), "gather_even_start",
                                                        deps=(w_in,))
        self.handles["w_odd"], token = _exchange_start(later(["odd_w_in", "odd_w_kv", "odd_w_out", "odd_d_pw_w"]),
                                                       "gather_odd_start", deps=(token,))
        return w, (token,)

    def even_rest(self, after):
        landed = _exchange_wait(self.handles.pop("w_even"), (after,), "gather_even_wait")
        kv, out = _sibling_forward(landed, "forward_even")
        return {"even_w_kv": _rows_of(kv), "even_w_out": _rows_of(out)}

    def odd(self, after):
        landed = _exchange_wait(self.handles.pop("w_odd"), (after,), "gather_odd_wait")
        w_in, kv, out, pw = _sibling_forward(landed, "forward_odd")
        return {"odd_w_in": _rows_of(w_in), "odd_w_kv": _rows_of(kv), "odd_w_out": _rows_of(out),
                "odd_d_pw_w": _rows_of(pw)}

    def send(self, group, g):
        big, replicated, sharded = _GROUPS[group]
        by_owner = [g[nm].reshape(N_DEV, -1, g[nm].shape[-1]) for nm in big]
        if group in _TWO_LEVEL:
            theirs = _sibling_swap(by_owner, "swap_" + group)
            items = [("scatter_chips", _add_partials(a, b, tr=_BIG_TILE_ROWS[nm], name="chip_sum_" + nm))
                     for nm, a, b in zip(big, by_owner, theirs)]
        else:
            items = [("scatter", a) for a in by_owner]
        items += [("gather", _pack_rows([g[nm] for nm in _REPLICATED if _REPLICATED[nm][0] == a]))
                  for a in replicated]
        items += [("scatter", _shard_cols(_pack_rows([g[nm] for nm in _SHARDED if _SHARDED[nm][0] == a])))
                  for a in sharded]
        if group == "odd_rest":
            items.append(("gather", _rows8(g["loss"])))
        self.handles[group], token = _exchange_start(items, "send_" + group + "_start")
        return (token,)

    def receive(self, group, after):
        after = after if isinstance(after, tuple) else (after,)
        return _exchange_wait(self.handles.pop(group), after, "send_" + group + "_wait")


def kernel(x, mem, even_pre_g, even_w_in, even_a_ln_g, even_a_ln_b, even_a_ws, even_a_bs, even_b_conv, even_mem_g, even_w_kv, even_w_out, even_post_g, odd_pre_g, odd_w_in, odd_c_wgrp, odd_c_scale, odd_d_dw_w, odd_d_dw_b, odd_d_ln_g, odd_d_ln_b, odd_d_pw_w, odd_d_pw_b, odd_mem_g, odd_w_kv, odd_w_out, odd_post_g, loss_target, m_even_pre_g, m_even_w_in, m_even_a_ln_g, m_even_a_ln_b, m_even_a_ws, m_even_a_bs, m_even_b_conv, m_even_mem_g, m_even_w_kv, m_even_w_out, m_even_post_g, m_odd_pre_g, m_odd_w_in, m_odd_c_wgrp, m_odd_c_scale, m_odd_d_dw_w, m_odd_d_dw_b, m_odd_d_ln_g, m_odd_d_ln_b, m_odd_d_pw_w, m_odd_d_pw_b, m_odd_mem_g, m_odd_w_kv, m_odd_w_out, m_odd_post_g, v_even_pre_g, v_even_w_in, v_even_a_ln_g, v_even_a_ln_b, v_even_a_ws, v_even_a_bs, v_even_b_conv, v_even_mem_g, v_even_w_kv, v_even_w_out, v_even_post_g, v_odd_pre_g, v_odd_w_in, v_odd_c_wgrp, v_odd_c_scale, v_odd_d_dw_w, v_odd_d_dw_b, v_odd_d_ln_g, v_odd_d_ln_b, v_odd_d_pw_w, v_odd_d_pw_b, v_odd_mem_g, v_odd_w_kv, v_odd_w_out, v_odd_post_g):
    given = dict(locals())
    view = lambda nm, kind: _view2d(given[kind + nm], nm in _TRANSPOSED)
    shard = {nm: view(nm, "") for nm in _WEIGHTS}
    wmv = {nm: (shard[nm], view(nm, "m_"), view(nm, "v_")) for nm in _WEIGHTS}

    ex = _MeshExchange(shard)
    grad_x, last = _step(x[0], mem[0], loss_target[0], ex)

    res = {}

    def update(group, after):
        names = _GROUPS[group][0]
        landed = ex.receive(group, after)
        for nm, recv in zip(names, landed):
            res[nm] = _adam_big(recv, *wmv[nm], tr=_BIG_TILE_ROWS[nm], name="adam_" + nm)
        return landed[len(names):]

    c192, losses = update("odd_rest", last)
    update("odd_in", res["odd_d_pw_w"][0])
    c768, c128 = update("even_rest", res["odd_w_in"][0])
    a128, a96 = update("even_in", res["even_w_kv"][0])
    (c1024,) = update("even_gains", res["even_w_in"][0])
    small, loss = _adam_small([c1024, c768, c128, c192, a128, a96, losses], {nm: wmv[nm] for nm in _SMALL})
    res.update(small)
    total = loss[0, 0]
    back = lambda nm, a: (a.T if nm in _TRANSPOSED else a).reshape(given[nm].shape)
    outs = [[back(nm, res[nm][i]) for nm in _WEIGHTS] for i in range(4)]
    return (total, grad_x[None], *outs[0], *outs[1], *outs[2], *outs[3])
```

```python
import functools

import jax
import jax.numpy as jnp
from jax import lax
from jax.experimental import pallas as pl
from jax.experimental.pallas import tpu as pltpu

F32 = jnp.float32
BF16 = jnp.bfloat16
S = jax.ShapeDtypeStruct
MESH = pl.DeviceIdType.MESH
AXES = ("x", "y", "c")
N_DEV = 8

D = 1024
BW = 768
XA = 512
HD = 128
NH = 4
MIX = 2048
CHUNK = 128
GRP = 192
N_MEM = 256
CONF_K = 31
EPS = 1e-6
HALO = 32
POOL_WINDOWS = (2, 4, 8, 16)
TM_FWD_EVEN = 512
TM_FWD_ODD = 256
TM_BWD_EVEN = 256
TM_BWD_ODD = 256
RB = 16

E_U, E_V, E_BG, E_CG, E_XIN, E_Q, E_GATE = 0, 768, 1536, 2304, 3072, 3840, 4352
EVEN_IN = 6400
O_ZC, O_GA, O_GB, O_Q, O_GATE = 0, 768, 1536, 2304, 2816
ODD_IN = 4864

ADAM_LR, ADAM_B1, ADAM_B2, ADAM_EPS, ADAM_WD, ADAM_STEP = 0.001, 0.9, 0.999, 1e-08, 0.01, 10

VMEM_LIMIT_V7X = 56 * 1024 * 1024
VMEM_LIMIT_ODD_BWD_V7X = 62 * 1024 * 1024


def _params(sem=None):
    return pltpu.CompilerParams(dimension_semantics=sem, vmem_limit_bytes=VMEM_LIMIT_V7X)


def _dot(a, b):
    return jnp.dot(a, b, preferred_element_type=F32)


def _dot_nt(a, b):
    return lax.dot_general(a, b, (((1,), (1,)), ((), ())), preferred_element_type=F32)


def _dot_tn(a, b):
    return lax.dot_general(a, b, (((0,), (0,)), ((), ())), preferred_element_type=F32)


def _sigmoid(z):
    return pl.reciprocal(1.0 + jnp.exp(-z), approx=True)


def _rowmean(a):
    return jnp.mean(a, axis=-1, keepdims=True)


def _colsum(a):
    return jnp.sum(a, axis=0, keepdims=True)


def _ln_stats(v):
    mu = _rowmean(v)
    vc = v - mu
    rs = lax.rsqrt(_rowmean(vc * vc) + EPS)
    return vc * rs, rs


def _ln_bwd(dn, vh, rs, g):
    dvh = dn * g
    return rs * (dvh - _rowmean(dvh) - vh * _rowmean(dvh * vh))


def _group_masks():
    col = lax.broadcasted_iota(jnp.int32, (1, BW), 1)
    return [((col >= GRP * h) & (col < GRP * (h + 1))).astype(F32) for h in range(NH)]


def _full(shape):
    nd = len(shape)
    return pl.BlockSpec(shape, lambda *_: (0,) * nd)


def _whole():
    return pl.BlockSpec(memory_space=pltpu.VMEM)


_ANY = pl.BlockSpec(memory_space=pl.ANY)


def _after(body, n_in, deps):
    def ordered(*refs):
        return body(*refs[:n_in], *refs[n_in + len(deps):])
    return ordered


def _rms_matmul(x, g, w, *, tm, name, transposed=False, out_dtype=F32, deps=()):
    t, d = x.shape
    n = w.shape[0] if transposed else w.shape[1]

    def body(x_ref, g_ref, w_ref, p_ref, h_ref):
        xv = x_ref[...]
        r = lax.rsqrt(_rowmean(xv * xv) + EPS)
        h = (xv * r * g_ref[...]).astype(BF16)
        h_ref[...] = h
        p_ref[...] = (_dot_nt(h, w_ref[...]) if transposed else _dot(h, w_ref[...])).astype(out_dtype)

    return pl.pallas_call(
        _after(body, 3, deps), name=name, grid=(t // tm,),
        in_specs=[pl.BlockSpec((tm, d), lambda i: (i, 0)), _whole(), _whole()] + [_ANY] * len(deps),
        out_specs=[pl.BlockSpec((tm, n), lambda i: (i, 0)), pl.BlockSpec((tm, d), lambda i: (i, 0))],
        out_shape=[S((t, n), out_dtype), S((t, d), BF16)],
        compiler_params=_params(("arbitrary",)),
    )(x, g, w, *deps)


def _nt_matmul_rms_bwd(dp, w, x, g, dres, *, tm, name, transposed=False, deps=()):
    t, n = dp.shape
    d = x.shape[1]

    def body(dp_ref, w_ref, x_ref, g_ref, dres_ref, dx_ref, dg_ref):
        @pl.when(pl.program_id(0) == 0)
        def _():
            dg_ref[...] = jnp.zeros_like(dg_ref)

        dh = _dot(dp_ref[...], w_ref[...]) if transposed else _dot_nt(dp_ref[...], w_ref[...])
        xv = x_ref[...]
        r = lax.rsqrt(_rowmean(xv * xv) + EPS)
        xh = xv * r
        dg_ref[...] += _colsum(dh * xh)
        dxh = dh * g_ref[...]
        dx_ref[...] = dres_ref[...] + r * (dxh - xh * _rowmean(dxh * xh))

    return pl.pallas_call(
        _after(body, 5, deps), name=name, grid=(t // tm,),
        in_specs=[pl.BlockSpec((tm, n), lambda i: (i, 0)), _whole(), pl.BlockSpec((tm, d), lambda i: (i, 0)),
                  _whole(), pl.BlockSpec((tm, d), lambda i: (i, 0))] + [_ANY] * len(deps),
        out_specs=[pl.BlockSpec((tm, d), lambda i: (i, 0)), pl.BlockSpec((1, d), lambda i: (0, 0))],
        out_shape=[S((t, d), F32), S((1, d), F32)],
        compiler_params=_params(("arbitrary",)),
    )(dp, w, x, g, dres, *deps)


def _tn_matmul(a, b, *, tmc, tk, out_dtype, name, deps=()):
    t, m = a.shape
    n = b.shape[1]
    nk = t // tk

    def body(a_ref, b_ref, o_ref, acc_ref):
        k = pl.program_id(1)

        @pl.when(k == 0)
        def _():
            acc_ref[...] = jnp.zeros_like(acc_ref)

        acc_ref[...] += _dot_tn(a_ref[...], b_ref[...])

        @pl.when(k == nk - 1)
        def _():
            o_ref[...] = acc_ref[...].astype(out_dtype)

    return pl.pallas_call(
        _after(body, 2, deps), name=name, grid=(m // tmc, nk),
        in_specs=[pl.BlockSpec((tk, tmc), lambda j, k: (k, j)), pl.BlockSpec((tk, n), lambda j, k: (k, 0))]
        + [_ANY] * len(deps),
        out_specs=pl.BlockSpec((tmc, n), lambda j, k: (j, 0)),
        out_shape=S((m, n), out_dtype),
        scratch_shapes=[pltpu.VMEM((tmc, n), F32)],
        compiler_params=_params(("arbitrary", "arbitrary")),
    )(a, b, *deps)


def _silu_parts(gt):
    sg = _sigmoid(gt)
    return gt * sg, sg * (1.0 + gt * (1.0 - sg))


def _attn_head(q_b, k_b, v_b):
    s = _dot_nt(q_b, k_b) * (HD ** -0.5)
    e = jnp.exp(s - jnp.max(s, axis=-1, keepdims=True))
    prob = e / jnp.sum(e, axis=-1, keepdims=True)
    return prob, _dot(prob.astype(BF16), v_b)


def _rms_residual(x, o, g):
    r = lax.rsqrt(_rowmean(o * o) + EPS)
    return x + o * r * g


def _rms_post_bwd(dres, o, g):
    r = lax.rsqrt(_rowmean(o * o) + EPS)
    oh = o * r
    doh = dres * g
    return r * (doh - oh * _rowmean(doh * oh)), _colsum(dres * oh)


LANE = 128
_TILE_GROUPS = [sorted({LANE * j // GRP, (LANE * j + LANE - 1) // GRP}) for j in range(BW // LANE)]


def _tile(j):
    return slice(LANE * j, LANE * (j + 1))


def _low_lanes():
    return lax.broadcasted_iota(jnp.int32, (1, LANE), 1) < GRP - LANE


def _by_group(fn):
    tiles = []
    for j, groups in enumerate(_TILE_GROUPS):
        if len(groups) == 1:
            tiles.append(fn(groups[0], j))
        else:
            tiles.append(jnp.where(_low_lanes(), fn(groups[0], j), fn(groups[1], j)))
    return jnp.concatenate(tiles, axis=1)


def _sgu_chunk(vn_b, ws_ref, bmap_ref):
    return bmap_ref[...] + _by_group(lambda h, j: _dot(ws_ref[h], vn_b[:, _tile(j)]))


def _shift_copies(buf, sh):
    n = buf.shape[0] - 8
    for b in range(1, 8):
        sh[b - 1, pl.ds(0, n), :] = buf[pl.ds(b, n), :]


def _loop_rows(rows, step, fn, carry=0, unrolled=True):
    if unrolled:
        for r0 in range(0, rows, step):
            carry = fn(r0, carry)
        return carry

    def body(j, c):
        return fn(pl.multiple_of(j * step, step), c)
    return lax.fori_loop(0, rows // step, body, carry)


def _rows_at(buf, sh, r0, off):
    b = off % 8
    if b == 0 or sh is None:
        return buf[pl.ds(r0 + off, 32), :]
    return sh[b - 1, pl.ds(r0 + (off - b), 32), :]


def _tap_sum(buf, sh, w_ref, r0, taps, causal):
    acc = None
    for k in range(taps):
        off = HALO - (taps - 1 - k) if causal else taps - 1 - k
        term = w_ref[k:k + 1, :] * _rows_at(buf, sh, r0, off)
        acc = term if acc is None else acc + term
    return acc


def _fold8(a):
    return a[0:8] + a[8:16] + a[16:24] + a[24:32]


def _tap_grads(dv, buf, sh, acc_ref, r0, taps):
    for k in range(taps):
        acc_ref[k * 8:(k + 1) * 8, :] += _fold8(dv * _rows_at(buf, sh, r0, HALO - (taps - 1 - k)))


def _halo_spec(n, nt, reverse, tm):
    per = tm // HALO
    if reverse:
        return pl.BlockSpec((HALO, n), lambda i: (jnp.maximum((nt - 1 - i) * per - 1, 0), 0))
    return pl.BlockSpec((HALO, n), lambda i: (jnp.maximum(i * per - 1, 0), 0))


def _even_fwd(x, p, lng, lnb, ws, bmap, wc, kv, wout, pg):
    t = x.shape[0]
    tm = min(TM_FWD_EVEN, t)
    nt = t // tm

    def body(x_ref, p_ref, ph_ref, lng_ref, lnb_ref, ws_ref, bmap_ref, wc_ref, kv_ref, wout_ref, pg_ref,
             o_ref, x1_ref, ybuf, cbuf):
        i = pl.program_id(0)
        vh, _ = _ln_stats(p_ref[:, E_V:E_V + BW])
        vn = vh * lng_ref[...] + lnb_ref[...]
        for c in range(tm // CHUNK):
            sl = slice(c * CHUNK, (c + 1) * CHUNK)
            sg = _sgu_chunk(vn[sl].astype(BF16), ws_ref, bmap_ref)
            gate, _ = _silu_parts(p_ref[sl, E_GATE:E_GATE + BW])
            ybuf[sl, 0:BW] = (p_ref[sl, E_U:E_U + BW] * sg * gate).astype(BF16)

        cbuf[0:HALO] = jnp.where(i > 0, ph_ref[:, E_CG:E_CG + BW] * ph_ref[:, E_XIN:E_XIN + BW], 0.0)
        cbuf[HALO:HALO + tm] = p_ref[:, E_CG:E_CG + BW] * p_ref[:, E_XIN:E_XIN + BW]
        for r0 in range(0, tm, 32):
            sl = slice(r0, r0 + 32)
            cv = _tap_sum(cbuf, None, wc_ref, r0, 3, True)
            gate, _ = _silu_parts(p_ref[sl, E_GATE + BW:E_GATE + 2 * BW])
            ybuf[sl, BW:2 * BW] = (p_ref[sl, E_BG:E_BG + BW] * cv * gate).astype(BF16)

        for h in range(NH):
            qs = slice(E_Q + h * HD, E_Q + (h + 1) * HD)
            _, yx = _attn_head(p_ref[:, qs].astype(BF16), kv_ref[:, h * HD:(h + 1) * HD],
                               kv_ref[:, XA + h * HD:XA + (h + 1) * HD])
            gs = slice(E_GATE + 2 * BW + h * HD, E_GATE + 2 * BW + (h + 1) * HD)
            gate, _ = _silu_parts(p_ref[:, gs])
            ybuf[:, 2 * BW + h * HD:2 * BW + (h + 1) * HD] = (yx * gate).astype(BF16)

        o = _dot(ybuf[...], wout_ref[...])
        o_ref[...] = o
        x1_ref[...] = _rms_residual(x_ref[...], o, pg_ref[...])

    tile = lambda n: pl.BlockSpec((tm, n), lambda i: (i, 0))
    return pl.pallas_call(
        body, name="even_fwd", grid=(nt,),
        in_specs=[tile(D), tile(EVEN_IN), _halo_spec(EVEN_IN, nt, False, tm)] + [_whole()] * 8,
        out_specs=[tile(D), tile(D)],
        out_shape=[S((t, D), F32), S((t, D), F32)],
        scratch_shapes=[pltpu.VMEM((tm, MIX), BF16), pltpu.VMEM((tm + HALO, BW), F32)],
        compiler_params=_params(("arbitrary",)),
    )(x, p, p, lng, lnb, ws, bmap, wc, kv, wout, pg)


def _even_bwd(dres, o, p, lng, lnb, ws, wst, bmap, wc, kv, wout, pg):
    t = dres.shape[0]
    tm = min(TM_BWD_EVEN, t)
    nt = t // tm

    def body(dres_ref, o_ref, p_ref, ph_ref, lng_ref, lnb_ref, ws_ref, wst_ref, bmap_ref, wc_ref, kv_ref, wout_ref,
             pg_ref, dp_ref, y_ref, do_ref, dpg_ref, dws_ref, dbs_ref, dlng_ref, dlnb_ref, dwc_ref, dkv_ref,
             dy, cbuf, gbuf, dconv, carry, dvn, dbmap, wacc):
        i = pl.program_id(0)
        ti = nt - 1 - i
        masks = _group_masks()

        @pl.when(i == 0)
        def _():
            for ref in (dpg_ref, dws_ref, dlng_ref, dlnb_ref, dkv_ref, dbmap, wacc):
                ref[...] = jnp.zeros_like(ref)

        do, dpg = _rms_post_bwd(dres_ref[...], o_ref[...], pg_ref[...])
        dpg_ref[...] += dpg
        do_b = do.astype(BF16)
        do_ref[...] = do_b
        dy[...] = _dot_nt(do_b, wout_ref[...])

        vh, rs = _ln_stats(p_ref[:, E_V:E_V + BW])
        vn = vh * lng_ref[...] + lnb_ref[...]
        for c in range(tm // CHUNK):
            sl = slice(c * CHUNK, (c + 1) * CHUNK)
            vn_b = vn[sl].astype(BF16)
            sg = _sgu_chunk(vn_b, ws_ref, bmap_ref)
            u = p_ref[sl, E_U:E_U + BW]
            gate, dgate = _silu_parts(p_ref[sl, E_GATE:E_GATE + BW])
            dyc = dy[sl, 0:BW]
            ya = u * sg
            y_ref[sl, 0:BW] = (ya * gate).astype(BF16)
            dp_ref[sl, E_GATE:E_GATE + BW] = (dyc * ya * dgate).astype(BF16)
            dya = dyc * gate
            dp_ref[sl, E_U:E_U + BW] = (dya * sg).astype(BF16)
            dsg = dya * u
            dbmap[...] += dsg
            dsg_b = dsg.astype(BF16)
            for h in range(NH):
                total = None
                for j, heads in enumerate(_TILE_GROUPS):
                    if h in heads:
                        d_t = dsg_b[:, _tile(j)]
                        if len(heads) == 2:
                            d_t = jnp.where(_low_lanes() == (h == heads[0]), d_t, jnp.zeros_like(d_t))
                        part = _dot_nt(d_t, vn_b[:, _tile(j)])
                        total = part if total is None else total + part
                dws_ref[h] += total
            dvn[sl, :] = _by_group(lambda h, j: _dot(wst_ref[h], dsg_b[:, _tile(j)]))
        dn = dvn[...]
        dlng_ref[...] += _colsum(dn * vh)
        dlnb_ref[...] += _colsum(dn)
        dp_ref[:, E_V:E_V + BW] = _ln_bwd(dn, vh, rs, lng_ref[...]).astype(BF16)

        cbuf[0:HALO] = jnp.where(ti > 0, ph_ref[:, E_CG:E_CG + BW] * ph_ref[:, E_XIN:E_XIN + BW], 0.0)
        cbuf[HALO:HALO + tm] = p_ref[:, E_CG:E_CG + BW] * p_ref[:, E_XIN:E_XIN + BW]
        for r0 in range(0, tm, 32):
            sl = slice(r0, r0 + 32)
            cv = _tap_sum(cbuf, None, wc_ref, r0, 3, True)
            gate, dgate = _silu_parts(p_ref[sl, E_GATE + BW:E_GATE + 2 * BW])
            bg = p_ref[sl, E_BG:E_BG + BW]
            dyc = dy[sl, BW:2 * BW]
            yb = bg * cv
            y_ref[sl, BW:2 * BW] = (yb * gate).astype(BF16)
            dp_ref[sl, E_GATE + BW:E_GATE + 2 * BW] = (dyc * yb * dgate).astype(BF16)
            dyb = dyc * gate
            dp_ref[sl, E_BG:E_BG + BW] = (dyb * cv).astype(BF16)
            dconv[sl, :] = dyb * bg
        gbuf[0:tm] = dconv[...]
        gbuf[tm:tm + HALO] = jnp.where(i > 0, carry[...], 0.0)
        carry[...] = dconv[0:HALO]
        for r0 in range(0, tm, 32):
            sl = slice(r0, r0 + 32)
            _tap_grads(dconv[sl, :], cbuf, None, wacc, r0, 3)
            dc = _tap_sum(gbuf, None, wc_ref, r0, 3, False)
            dp_ref[sl, E_CG:E_CG + BW] = (dc * p_ref[sl, E_XIN:E_XIN + BW]).astype(BF16)
            dp_ref[sl, E_XIN:E_XIN + BW] = (dc * p_ref[sl, E_CG:E_CG + BW]).astype(BF16)

        for h in range(NH):
            qs = slice(E_Q + h * HD, E_Q + (h + 1) * HD)
            ks = slice(h * HD, (h + 1) * HD)
            vs = slice(XA + h * HD, XA + (h + 1) * HD)
            gs = slice(E_GATE + 2 * BW + h * HD, E_GATE + 2 * BW + (h + 1) * HD)
            ys = slice(2 * BW + h * HD, 2 * BW + (h + 1) * HD)
            q_b = p_ref[:, qs].astype(BF16)
            prob, yx = _attn_head(q_b, kv_ref[:, ks], kv_ref[:, vs])
            gate, dgate = _silu_parts(p_ref[:, gs])
            dyc = dy[:, ys]
            y_ref[:, ys] = (yx * gate).astype(BF16)
            dp_ref[:, gs] = (dyc * yx * dgate).astype(BF16)
            dyx_b = (dyc * gate).astype(BF16)
            dprob = _dot_nt(dyx_b, kv_ref[:, vs])
            dkv_ref[:, vs] += _dot_tn(prob.astype(BF16), dyx_b)
            ds_b = (prob * (dprob - jnp.sum(dprob * prob, axis=-1, keepdims=True)) * (HD ** -0.5)).astype(BF16)
            dp_ref[:, qs] = _dot(ds_b, kv_ref[:, ks]).astype(BF16)
            dkv_ref[:, ks] += _dot_tn(ds_b, q_b)

        @pl.when(i == nt - 1)
        def _():
            for h in range(NH):
                dbs_ref[:, h * HD:(h + 1) * HD] = jnp.broadcast_to(
                    jnp.sum(dbmap[...] * masks[h], axis=-1, keepdims=True), (CHUNK, HD))
            for k in range(3):
                dwc_ref[k:k + 1, :] = _colsum(wacc[k * 8:(k + 1) * 8, :])
            dwc_ref[3:8, :] = jnp.zeros((5, BW), F32)
            causal = (lax.broadcasted_iota(jnp.int32, (CHUNK, CHUNK), 0)
                      >= lax.broadcasted_iota(jnp.int32, (CHUNK, CHUNK), 1))
            for h in range(NH):
                dws_ref[h] = jnp.where(causal, dws_ref[h], 0.0)

    rtile = lambda n: pl.BlockSpec((tm, n), lambda i: (nt - 1 - i, 0))
    outs = [S((t, EVEN_IN), BF16), S((t, MIX), BF16), S((t, D), BF16), S((1, D), F32), S((NH, CHUNK, CHUNK), F32),
            S((CHUNK, NH * HD), F32), S((1, BW), F32), S((1, BW), F32), S((8, BW), F32), S((N_MEM, 2 * XA), F32)]
    return pl.pallas_call(
        body, name="even_bwd", grid=(nt,),
        in_specs=[rtile(D), rtile(D), rtile(EVEN_IN), _halo_spec(EVEN_IN, nt, True, tm)] + [_whole()] * 9,
        out_specs=[rtile(EVEN_IN), rtile(MIX), rtile(D)] + [_full(s.shape) for s in outs[3:]],
        out_shape=outs,
        scratch_shapes=[pltpu.VMEM((tm, MIX), F32), pltpu.VMEM((tm + HALO, BW), F32), pltpu.VMEM((tm + HALO, BW), F32),
                        pltpu.VMEM((tm, BW), F32), pltpu.VMEM((HALO, BW), F32), pltpu.VMEM((tm, BW), F32),
                        pltpu.VMEM((CHUNK, BW), F32), pltpu.VMEM((3 * 8, BW), F32)],
        compiler_params=_params(("arbitrary",)),
    )(dres, o, p, p, lng, lnb, ws, wst, bmap, wc, kv, wout, pg)


def _pool_causal_levels(za, zb, zc, zd, tm):
    n = tm + HALO
    zb[pl.ds(8, n - 8), :] = za[pl.ds(8, n - 8), :] + za[pl.ds(7, n - 8), :]
    zc[pl.ds(16, n - 16), :] = zb[pl.ds(16, n - 16), :] + zb[pl.ds(14, n - 16), :]
    zd[pl.ds(24, n - 24), :] = zc[pl.ds(24, n - 24), :] + zc[pl.ds(20, n - 24), :]


def _pool_causal(za, zb, zc, zd, tm):
    _pool_causal_levels(za, zb, zc, zd, tm)
    s16 = zd[pl.ds(HALO, tm), :] + zd[pl.ds(HALO - 8, tm), :]
    return zb[pl.ds(HALO, tm), :], zc[pl.ds(HALO, tm), :], zd[pl.ds(HALO, tm), :], s16


def _pool_anticausal_levels(ea, eb, ec, ed, tm):
    n = tm + HALO
    eb[pl.ds(0, n - 8), :] = ea[pl.ds(0, n - 8), :] + ea[pl.ds(1, n - 8), :]
    ec[pl.ds(0, n - 16), :] = eb[pl.ds(0, n - 16), :] + eb[pl.ds(2, n - 16), :]
    ed[pl.ds(0, n - 24), :] = ec[pl.ds(0, n - 24), :] + ec[pl.ds(4, n - 24), :]


def _pool_weights(t0, masks, rows):
    del masks
    tf = (t0 + lax.broadcasted_iota(jnp.int32, (rows, 1), 0) + 1).astype(F32)
    inv = [jnp.broadcast_to(1.0 / jnp.minimum(tf, float(win)), (rows, LANE)) for win in POOL_WINDOWS]
    return _by_group(lambda g, j: inv[g])


_HALVES = (slice(0, BW // 2), slice(BW // 2, BW))


def _mix4(masks, parts):
    del masks
    return _by_group(lambda g, j: parts[g][:, _tile(j)])


def _odd_fwd(x1, tgt, p, wbd, cscale, dww, dwb, lng, lnb, wpw, pwb, kv, wout, pg):
    t = x1.shape[0]
    tm = min(TM_FWD_ODD, t)
    nt = t // tm

    def body(x_ref, tgt_ref, p_ref, ph_ref, wbd_ref, cs_ref, dww_ref, dwb_ref, lng_ref, lnb_ref, wpw_ref, pwb_ref,
             kv_ref, wout_ref, pg_ref, o_ref, dres_ref, loss_ref, conv_ref, ybuf, za, zb, zc, zd, gbuf, lacc, gsh):
        i = pl.program_id(0)
        masks = _group_masks()

        @pl.when(i == 0)
        def _():
            lacc[...] = jnp.zeros_like(lacc)

        z = p_ref[:, O_ZC:O_ZC + BW]
        za[0:HALO] = jnp.where(i > 0, ph_ref[:, O_ZC:O_ZC + BW], 0.0)
        za[HALO:HALO + tm] = z
        pooled = _mix4(masks, _pool_causal(za, zb, zc, zd, tm)) * _pool_weights(i * tm, masks, tm) - z
        pooled_b = pooled.astype(BF16)
        for hs in _HALVES:
            gate, _ = _silu_parts(p_ref[:, O_GATE + hs.start:O_GATE + hs.stop])
            ybuf[:, hs] = (_dot(pooled_b[:, hs], wbd_ref[hs, hs]) * cs_ref[:, hs] * gate).astype(BF16)

        gbuf[0:HALO] = jnp.where(i > 0, ph_ref[:, O_GA:O_GA + BW] * _sigmoid(ph_ref[:, O_GB:O_GB + BW]), 0.0)
        gbuf[HALO:HALO + tm] = p_ref[:, O_GA:O_GA + BW] * _sigmoid(p_ref[:, O_GB:O_GB + BW])
        _shift_copies(gbuf, gsh)
        def conv_rows(r0, carry):
            conv_ref[pl.ds(r0, 32), :] = _tap_sum(gbuf, gsh, dww_ref, r0, CONF_K, True) + dwb_ref[...]
            return carry

        _loop_rows(tm, 32, conv_rows)
        zh, _ = _ln_stats(conv_ref[...])
        zn = zh * lng_ref[...] + lnb_ref[...]
        yd = _dot((zn * _sigmoid(zn)).astype(BF16), wpw_ref[...]) + pwb_ref[...]
        gate, _ = _silu_parts(p_ref[:, O_GATE + BW:O_GATE + 2 * BW])
        ybuf[:, BW:2 * BW] = (yd * gate).astype(BF16)

        for h in range(NH):
            qs = slice(O_Q + h * HD, O_Q + (h + 1) * HD)
            _, yx = _attn_head(p_ref[:, qs].astype(BF16), kv_ref[:, h * HD:(h + 1) * HD],
                               kv_ref[:, XA + h * HD:XA + (h + 1) * HD])
            gs = slice(O_GATE + 2 * BW + h * HD, O_GATE + 2 * BW + (h + 1) * HD)
            gate, _ = _silu_parts(p_ref[:, gs])
            ybuf[:, 2 * BW + h * HD:2 * BW + (h + 1) * HD] = (yx * gate).astype(BF16)

        o = _dot(ybuf[...], wout_ref[...])
        o_ref[...] = o
        err = _rms_residual(x_ref[...], o, pg_ref[...]) - tgt_ref[...]
        lacc[...] += _colsum(err * err)
        dres_ref[...] = err * (1.0 / D)

        @pl.when(i == nt - 1)
        def _():
            loss_ref[...] = jnp.full((1, HD), jnp.sum(lacc[...]) * (0.5 / D), F32)

    tile = lambda n: pl.BlockSpec((tm, n), lambda i: (i, 0))
    ext = pltpu.VMEM((tm + HALO, BW), F32)
    return pl.pallas_call(
        body, name="odd_fwd", grid=(nt,),
        in_specs=[tile(D), tile(D), tile(ODD_IN), _halo_spec(ODD_IN, nt, False, tm)] + [_whole()] * 11,
        out_specs=[tile(D), tile(D), _full((1, HD)), tile(BW)],
        out_shape=[S((t, D), F32), S((t, D), F32), S((1, HD), F32), S((t, BW), F32)],
        scratch_shapes=[pltpu.VMEM((tm, MIX), BF16), ext, ext, ext, ext, ext,
                        pltpu.VMEM((1, D), F32), pltpu.VMEM((7, tm + HALO, BW), F32)],
        compiler_params=_params(("arbitrary",)),
    )(x1, tgt, p, p, wbd, cscale, dww, dwb, lng, lnb, wpw, pwb, kv, wout, pg)


def _odd_bwd(dres, o, p, conv, wbd, cscale, dww, dwb, lng, lnb, wpw, pwb, kv, wout, pg):
    t = dres.shape[0]
    tm = min(TM_BWD_ODD, t)
    nt = t // tm

    def body(dres_ref, o_ref, p_ref, ph_ref, conv_ref, wbd_ref, cs_ref, dww_ref, dwb_ref, lng_ref, lnb_ref, wpw_ref,
             pwb_ref, kv_ref, wout_ref, pg_ref, dp_ref, y_ref, do_ref, dpg_ref, dwbd_ref, dcs_ref, ddww_ref, ddwb_ref,
             dlng_ref, dlnb_ref, dwpw_ref, dpwb_ref, dkv_ref,
             dy, za, zb, zc, zd, tmp, carry_e, carry_d, wacc, shifted, t1, b1, b2):
        gbuf, hbuf = za, zb
        i = pl.program_id(0)
        ti = nt - 1 - i
        masks = _group_masks()

        @pl.when(i == 0)
        def _():
            for ref in (dpg_ref, dwbd_ref, dcs_ref, ddwb_ref, dlng_ref, dlnb_ref, dwpw_ref, dpwb_ref, dkv_ref, wacc):
                ref[...] = jnp.zeros_like(ref)

        def post_norm_rows(r0, acc):
            sl = pl.ds(r0, RB)
            ov, dv = o_ref[sl, :], dres_ref[sl, :]
            r = lax.rsqrt(_rowmean(ov * ov) + EPS)
            oh = ov * r
            doh = dv * pg_ref[...]
            do_ref[sl, :] = (r * (doh - oh * _rowmean(doh * oh))).astype(BF16)
            return acc + dv * oh

        dpg_ref[...] += _colsum(_loop_rows(tm, RB, post_norm_rows, jnp.zeros((RB, D), F32)))
        dy[...] = _dot_nt(do_ref[...], wout_ref[...])

        za[0:HALO] = jnp.where(ti > 0, ph_ref[:, O_ZC:O_ZC + BW], 0.0)
        za[HALO:HALO + tm] = p_ref[:, O_ZC:O_ZC + BW]
        _pool_causal_levels(za, zb, zc, zd, tm)

        def pooled_rows(r0, carry):
            sl = pl.ds(r0, RB)
            at = lambda ref, back=0: ref[pl.ds(HALO + r0 - back, RB), :]
            inv = _pool_weights(ti * tm + r0, masks, RB)
            sums = (at(zb), at(zc), at(zd), at(zd) + at(zd, 8))
            b1[sl, :] = (_mix4(masks, sums) * inv - p_ref[sl, O_ZC:O_ZC + BW]).astype(BF16)
            return carry

        _loop_rows(tm, RB, pooled_rows)
        for hs in _HALVES:
            t1[:, hs] = _dot(b1[:, hs], wbd_ref[hs, hs])

        def pool_gate_rows(r0, acc):
            sl = pl.ds(r0, RB)
            pm = t1[sl, :]
            gate, dgate = _silu_parts(p_ref[sl, O_GATE:O_GATE + BW])
            dyc = dy[sl, 0:BW]
            yc = pm * cs_ref[...]
            y_ref[sl, 0:BW] = (yc * gate).astype(BF16)
            dp_ref[sl, O_GATE:O_GATE + BW] = (dyc * yc * dgate).astype(BF16)
            dyc = dyc * gate
            b2[sl, :] = (dyc * cs_ref[...]).astype(BF16)
            return acc + dyc * pm

        dcs_ref[...] += _colsum(_loop_rows(tm, RB, pool_gate_rows, jnp.zeros((RB, BW), F32)))
        for hs in _HALVES:
            dwbd_ref[hs, hs] += _dot_tn(b1[:, hs], b2[:, hs])
            t1[:, hs] = _dot_nt(b2[:, hs], wbd_ref[hs, hs])

        def weighted_rows(r0, carry):
            sl = pl.ds(r0, RB)
            za[sl, :] = t1[sl, :] * _pool_weights(ti * tm + r0, masks, RB)
            return carry

        _loop_rows(tm, RB, weighted_rows)
        za[tm:tm + HALO] = jnp.where(i > 0, carry_e[...], 0.0)
        carry_e[...] = za[0:HALO]
        _pool_anticausal_levels(za, zb, zc, zd, tm)

        def pool_back_rows(r0, carry):
            sl = pl.ds(r0, RB)
            ahead = lambda ref, fwd=0: ref[pl.ds(r0 + fwd, RB), :]
            sums = (ahead(zb), ahead(zc), ahead(zd), ahead(zd) + ahead(zd, 8))
            dp_ref[sl, O_ZC:O_ZC + BW] = (_mix4(masks, sums) - t1[sl, :]).astype(BF16)
            return carry

        _loop_rows(tm, RB, pool_back_rows)

        gbuf[0:HALO] = jnp.where(ti > 0, ph_ref[:, O_GA:O_GA + BW] * _sigmoid(ph_ref[:, O_GB:O_GB + BW]), 0.0)

        def glu_rows(r0, carry):
            sl = pl.ds(r0, RB)
            gbuf[pl.ds(HALO + r0, RB), :] = p_ref[sl, O_GA:O_GA + BW] * _sigmoid(p_ref[sl, O_GB:O_GB + BW])
            zh, _ = _ln_stats(conv_ref[sl, :])
            zn = zh * lng_ref[...] + lnb_ref[...]
            b1[sl, :] = (zn * _sigmoid(zn)).astype(BF16)
            return carry

        _loop_rows(tm, RB, glu_rows)
        _shift_copies(gbuf, shifted)
        t1[...] = _dot(b1[...], wpw_ref[...])

        def conf_gate_rows(r0, acc):
            sl = pl.ds(r0, RB)
            yd = t1[sl, :] + pwb_ref[...]
            gate, dgate = _silu_parts(p_ref[sl, O_GATE + BW:O_GATE + 2 * BW])
            dyc = dy[sl, BW:2 * BW]
            y_ref[sl, BW:2 * BW] = (yd * gate).astype(BF16)
            dp_ref[sl, O_GATE + BW:O_GATE + 2 * BW] = (dyc * yd * dgate).astype(BF16)
            dyd = dyc * gate
            b2[sl, :] = dyd.astype(BF16)
            return acc + dyd

        dpwb_ref[...] += _colsum(_loop_rows(tm, RB, conf_gate_rows, jnp.zeros((RB, BW), F32)))
        dwpw_ref[...] += _dot_tn(b1[...], b2[...])
        t1[...] = _dot_nt(b2[...], wpw_ref[...])

        def norm_back_rows(r0, accs):
            sl = pl.ds(r0, RB)
            zh, rs = _ln_stats(conv_ref[sl, :])
            _, dsilu = _silu_parts(zh * lng_ref[...] + lnb_ref[...])
            dzn = t1[sl, :] * dsilu
            dzd = _ln_bwd(dzn, zh, rs, lng_ref[...])
            tmp[sl, :] = dzd
            hbuf[sl, :] = dzd
            return accs[0] + dzn * zh, accs[1] + dzn, accs[2] + dzd

        zero = jnp.zeros((RB, BW), F32)
        acc_g, acc_b, acc_d = _loop_rows(tm, RB, norm_back_rows, (zero, zero, zero))
        dlng_ref[...] += _colsum(acc_g)
        dlnb_ref[...] += _colsum(acc_b)
        ddwb_ref[...] += _colsum(acc_d)
        hbuf[tm:tm + HALO] = jnp.where(i > 0, carry_d[...], 0.0)
        carry_d[...] = tmp[0:HALO]
        def tap_grad_rows(r0, carry):
            _tap_grads(tmp[pl.ds(r0, 32), :], gbuf, shifted, wacc, r0, CONF_K)
            return carry

        _loop_rows(tm, 32, tap_grad_rows, unrolled=False)
        _shift_copies(hbuf, shifted)

        def conv_back_rows(r0, carry):
            sl = pl.ds(r0, 32)
            dzg = _tap_sum(hbuf, shifted, dww_ref, r0, CONF_K, False)
            sgb = _sigmoid(p_ref[sl, O_GB:O_GB + BW])
            dp_ref[sl, O_GA:O_GA + BW] = (dzg * sgb).astype(BF16)
            dp_ref[sl, O_GB:O_GB + BW] = (dzg * p_ref[sl, O_GA:O_GA + BW] * sgb * (1.0 - sgb)).astype(BF16)
            return carry

        _loop_rows(tm, 32, conv_back_rows, unrolled=False)

        for h in range(NH):
            qs = slice(O_Q + h * HD, O_Q + (h + 1) * HD)
            ks = slice(h * HD, (h + 1) * HD)
            vs = slice(XA + h * HD, XA + (h + 1) * HD)
            gs = slice(O_GATE + 2 * BW + h * HD, O_GATE + 2 * BW + (h + 1) * HD)
            ys = slice(2 * BW + h * HD, 2 * BW + (h + 1) * HD)
            q_b = p_ref[:, qs].astype(BF16)
            prob, yx = _attn_head(q_b, kv_ref[:, ks], kv_ref[:, vs])
            gate, dgate = _silu_parts(p_ref[:, gs])
            dyc = dy[:, ys]
            y_ref[:, ys] = (yx * gate).astype(BF16)
            dp_ref[:, gs] = (dyc * yx * dgate).astype(BF16)
            dyx_b = (dyc * gate).astype(BF16)
            dprob = _dot_nt(dyx_b, kv_ref[:, vs])
            dkv_ref[:, vs] += _dot_tn(prob.astype(BF16), dyx_b)
            ds_b = (prob * (dprob - jnp.sum(dprob * prob, axis=-1, keepdims=True)) * (HD ** -0.5)).astype(BF16)
            dp_ref[:, qs] = _dot(ds_b, kv_ref[:, ks]).astype(BF16)
            dkv_ref[:, ks] += _dot_tn(ds_b, q_b)

        @pl.when(i == nt - 1)
        def _():
            for k in range(CONF_K):
                ddww_ref[k:k + 1, :] = _colsum(wacc[k * 8:(k + 1) * 8, :])
            ddww_ref[CONF_K:CONF_K + 1, :] = jnp.zeros((1, BW), F32)

    rtile = lambda n: pl.BlockSpec((tm, n), lambda i: (nt - 1 - i, 0))
    outs = [S((t, ODD_IN), BF16), S((t, MIX), BF16), S((t, D), BF16), S((1, D), F32), S((BW, BW), F32),
            S((1, BW), F32), S((CONF_K + 1, BW), F32), S((1, BW), F32), S((1, BW), F32), S((1, BW), F32),
            S((BW, BW), F32), S((1, BW), F32), S((N_MEM, 2 * XA), F32)]
    ext = pltpu.VMEM((tm + HALO, BW), F32)
    return pl.pallas_call(
        body, name="odd_bwd", grid=(nt,),
        in_specs=[rtile(D), rtile(D), rtile(ODD_IN), _halo_spec(ODD_IN, nt, True, tm), rtile(BW)] + [_whole()] * 11,
        out_specs=[rtile(ODD_IN), rtile(MIX), rtile(D)] + [_full(s.shape) for s in outs[3:]],
        out_shape=outs,
        scratch_shapes=[pltpu.VMEM((tm, MIX), F32), ext, ext, ext, ext, pltpu.VMEM((tm, BW), F32),
                        pltpu.VMEM((HALO, BW), F32), pltpu.VMEM((HALO, BW), F32), pltpu.VMEM((CONF_K * 8, BW), F32),
                        pltpu.VMEM((7, tm + HALO, BW), F32),
                        pltpu.VMEM((tm, BW), F32), pltpu.VMEM((tm, BW), BF16), pltpu.VMEM((tm, BW), BF16)],
        compiler_params=pltpu.CompilerParams(dimension_semantics=("arbitrary",),
                                             vmem_limit_bytes=VMEM_LIMIT_ODD_BWD_V7X),
    )(dres, o, p, p, conv, wbd, cscale, dww, dwb, lng, lnb, wpw, pwb, kv, wout, pg)


def _pick_rows(n):
    for rows in (640, 2432, 1024, 768):
        if n % rows == 0:
            return rows
    return n


def _pad_rows(a, rows):
    return jnp.pad(a, ((0, rows - a.shape[0]), (0, 0)))


def _step(x, mem, tgt, ex):
    t = x.shape[0]
    tm = min(512, t)
    w, deps = ex.first()
    causal = jnp.tril(jnp.ones((CHUNK, CHUNK), bool))
    ws = jnp.where(causal[None], w["even_a_ws"], 0.0).astype(BF16)
    wst = jnp.transpose(ws, (0, 2, 1))
    bmap = jnp.repeat(w["even_a_bs"].T, GRP, axis=1)
    wc = _pad_rows(w["even_b_conv"], 8)
    wbd = jax.scipy.linalg.block_diag(*[w["odd_c_wgrp"][g] for g in range(NH)]).astype(BF16)
    dww = _pad_rows(w["odd_d_dw_w"], CONF_K + 1)
    tk = min(1024, t)
    zeros = jnp.zeros_like(mem)

    p_e, h_e = _rms_matmul(x, w["even_pre_g"], w["even_w_in"], tm=tm, name="in_even", transposed=True, deps=deps)
    w.update(ex.even_rest(h_e))
    kv_e, memn_e = _rms_matmul(mem, w["even_mem_g"], w["even_w_kv"], tm=N_MEM, name="kv_even", out_dtype=BF16)
    even_args = (w["even_a_ln_g"], w["even_a_ln_b"], ws)
    o_e, x1 = _even_fwd(x, p_e, *even_args, bmap, wc, kv_e, w["even_w_out"], w["even_post_g"])
    w.update(ex.odd(o_e))
    kv_o, memn_o = _rms_matmul(mem, w["odd_mem_g"], w["odd_w_kv"], tm=N_MEM, name="kv_odd", out_dtype=BF16)
    p_o, h_o = _rms_matmul(x1, w["odd_pre_g"], w["odd_w_in"], tm=tm, name="in_odd", transposed=True)
    odd_args = (wbd, w["odd_c_scale"], dww, w["odd_d_dw_b"], w["odd_d_ln_g"], w["odd_d_ln_b"], w["odd_d_pw_w"],
                w["odd_d_pw_b"], kv_o, w["odd_w_out"], w["odd_post_g"])
    o_o, dres, loss, conv_o = _odd_fwd(x1, tgt, p_o, *odd_args)

    g = {}
    (dp_o, y_o, do_o, post_g_o, dwbd, g["odd_c_scale"], ddww, g["odd_d_dw_b"], g["odd_d_ln_g"], g["odd_d_ln_b"],
     dwpw, g["odd_d_pw_b"], dkv_o) = _odd_bwd(dres, o_o, p_o, conv_o, *odd_args)
    g["odd_post_g"] = post_g_o
    g["odd_d_dw_w"] = ddww[:CONF_K]
    dkv_o = dkv_o.astype(BF16)
    deps = ex.send("odd_rest", {
        "odd_w_out": _tn_matmul(y_o, do_o, tmc=MIX, tk=tk, out_dtype=BF16, name="dw_out_odd"),
        "odd_w_kv": _tn_matmul(memn_o, dkv_o, tmc=D, tk=N_MEM, out_dtype=BF16, name="dw_kv_odd"),
        "odd_d_pw_w": dwpw.astype(BF16), "loss": loss,
        "odd_c_wgrp": jnp.concatenate([dwbd[i * GRP:(i + 1) * GRP, i * GRP:(i + 1) * GRP] for i in range(NH)])})
    deps = ex.send("odd_in", {"odd_w_in": _tn_matmul(dp_o, h_o, tmc=_pick_rows(ODD_IN), tk=tk, out_dtype=BF16,
                                                     name="dw_in_odd", deps=deps)})
    dx1, g["odd_pre_g"] = _nt_matmul_rms_bwd(dp_o, w["odd_w_in"], x1, w["odd_pre_g"], dres, tm=tm,
                                             name="dx_odd", transposed=True, deps=deps)
    _, g["odd_mem_g"] = _nt_matmul_rms_bwd(dkv_o, w["odd_w_kv"], mem, w["odd_mem_g"], zeros, tm=N_MEM,
                                           name="dmem_odd")

    (dp_e, y_e, do_e, post_g_e, dws, dbs, ln_g_e, ln_b_e, dwc, dkv_e) = _even_bwd(
        dx1, o_e, p_e, *even_args, wst, bmap, wc, kv_e, w["even_w_out"], w["even_post_g"])
    g["even_b_conv"] = dwc[:3]
    dkv_e = dkv_e.astype(BF16)
    deps = ex.send("even_rest", {
        "even_w_out": _tn_matmul(y_e, do_e, tmc=MIX, tk=tk, out_dtype=BF16, name="dw_out_even"),
        "even_w_kv": _tn_matmul(memn_e, dkv_e, tmc=D, tk=N_MEM, out_dtype=BF16, name="dw_kv_even"),
        "even_a_ln_g": ln_g_e, "even_a_ln_b": ln_b_e,
        "even_a_ws": dws.reshape(NH * CHUNK, CHUNK), "even_a_bs": dbs[:, ::HD].T})
    g["even_w_in"] = _tn_matmul(dp_e, h_e, tmc=_pick_rows(EVEN_IN), tk=tk, out_dtype=BF16, name="dw_in_even",
                                deps=deps)
    deps = ex.send("even_in", g)
    grad_x, pre_g_e = _nt_matmul_rms_bwd(dp_e, w["even_w_in"], x, w["even_pre_g"], dx1, tm=tm,
                                         name="dx_even", transposed=True, deps=deps)
    _, mem_g_e = _nt_matmul_rms_bwd(dkv_e, w["even_w_kv"], mem, w["even_mem_g"], zeros, tm=N_MEM, name="dmem_even",
                                    deps=(grad_x,))
    deps = ex.send("even_gains", {"even_pre_g": pre_g_e, "even_mem_g": mem_g_e, "even_post_g": post_g_e})
    return grad_x, deps


def _place():
    return lax.axis_index("x"), lax.axis_index("y"), lax.axis_index("c")


def _index(px, py, pc):
    return 4 * px + 2 * py + pc


_COPIES = N_DEV - 1


def _all_gather(arrs, name):
    n = len(arrs)

    def body(*refs):
        ins, outs = refs[:n], refs[n:2 * n]
        send_sems, recv_sems, local_sems = refs[2 * n:]
        x, y, c = _place()
        me, sibling = (x, y, c), (x, y, 1 - c)
        chips = [(1 - x, y), (x, 1 - y), (1 - x, 1 - y)]

        def copy(a, k, block, to, src=None):
            dst = outs[a].at[_index(*block)]
            return pltpu.make_async_remote_copy(
                src_ref=dst if src is None else src, dst_ref=dst, send_sem=send_sems.at[a * _COPIES + k],
                recv_sem=recv_sems.at[a * _COPIES + k], device_id=to, device_id_type=MESH)

        mine = [pltpu.make_async_copy(ins[a], outs[a].at[_index(*me)], local_sems.at[a]) for a in range(n)]
        first = []
        for a in range(n):
            mine[a].start()
            first.append(copy(a, 0, me, sibling, src=ins[a]))
            first += [copy(a, 1 + j, me, (*chip, c), src=ins[a]) for j, chip in enumerate(chips)]
        for cp in first:
            cp.start()
        passed = []
        for j, chip in enumerate(chips):
            for a in range(n):
                copy(a, 1 + j, (*chip, c), me).wait_recv()
                passed.append(copy(a, 4 + j, (*chip, c), sibling))
                passed[-1].start()
        for a in range(n):
            copy(a, 0, sibling, me).wait_recv()
            for j, chip in enumerate(chips):
                copy(a, 4 + j, (*chip, 1 - c), me).wait_recv()
        for cp in first + passed:
            cp.wait_send()
        for cp in mine:
            cp.wait()

    return pl.pallas_call(
        body, name=name, in_specs=[_ANY] * n, out_specs=[_ANY] * n,
        out_shape=[S((N_DEV,) + a.shape, a.dtype) for a in arrs],
        scratch_shapes=[pltpu.SemaphoreType.DMA((n * _COPIES,)), pltpu.SemaphoreType.DMA((n * _COPIES,)),
                        pltpu.SemaphoreType.DMA((n,))],
    )(*arrs)


_HBM = pl.BlockSpec(memory_space=pltpu.HBM)
_SEM = pl.BlockSpec(memory_space=pltpu.SEMAPHORE)
_EFFECT = pltpu.SideEffectType.DATAFLOW_SIDE_EFFECTING


_ALL_FLIPS = [(k >> 2 & 1, k >> 1 & 1, k & 1) for k in range(1, N_DEV)]
_CHIP_FLIPS = [(1, 0, 0), (0, 1, 0), (1, 1, 0)]
_FLIPS = {"gather": _ALL_FLIPS, "scatter": _ALL_FLIPS, "gather_chips": [(0, 0, 1)] + _CHIP_FLIPS,
          "scatter_chips": _CHIP_FLIPS}


def _landing_shape(kind, a):
    return (N_DEV,) + a.shape if kind.startswith("gather") else a.shape


def _exchange_copies(kinds, srcs, lands, send_sems, recv_sems, local_sems, arriving):
    x, y, c = _place()
    mine = _index(x, y, c)
    remote, local = [], []
    for a, kind in enumerate(kinds):
        by_chip = kind == "scatter_chips"
        here = 2 * x + y if by_chip else mine
        own = srcs[a] if kind.startswith("gather") else srcs[a].at[here]
        local.append(pltpu.make_async_copy(own, lands[a].at[here], local_sems.at[a]))
        for k, (fx, fy, fc) in enumerate(_FLIPS[kind]):
            peer = (1 - x if fx else x, 1 - y if fy else y, 1 - c if fc else c)
            there = 2 * peer[0] + peer[1] if by_chip else _index(*peer)
            remote.append(pltpu.make_async_remote_copy(
                src_ref=srcs[a] if kind.startswith("gather") else srcs[a].at[there],
                dst_ref=lands[a].at[there if arriving else here],
                send_sem=send_sems.at[a * _COPIES + k], recv_sem=recv_sems.at[a * _COPIES + k],
                device_id=peer, device_id_type=MESH))
    return remote, local


def _exchange_start(items, name, deps=()):
    kinds = [kind for kind, _ in items]
    srcs = [a for _, a in items]
    n = len(items)
    lands = [lax.empty(_landing_shape(kind, a), a.dtype) for kind, a in items]

    def body(*refs):
        send_sems, recv_sems, local_sems = refs[2 * n + len(deps):2 * n + len(deps) + 3]
        remote, local = _exchange_copies(kinds, refs[:n], refs[n:2 * n], send_sems, recv_sems, local_sems, False)
        for cp in local + remote:
            cp.start()
        refs[-1][...] = jnp.zeros_like(refs[-1])

    held = [pltpu.HBM(a.shape, a.dtype) for a in srcs + lands]
    res = pl.pallas_call(
        body, name=name,
        out_shape=(pltpu.SemaphoreType.DMA((n * _COPIES,)), pltpu.SemaphoreType.DMA((n * _COPIES,)),
                   pltpu.SemaphoreType.DMA((n,)), *held, S((8, 128), F32)),
        in_specs=[_HBM] * (2 * n) + [_ANY] * len(deps),
        out_specs=(_SEM, _SEM, _SEM, *[_HBM] * (2 * n), _whole()),
        input_output_aliases={i: 3 + i for i in range(2 * n)},
        compiler_params=pltpu.CompilerParams(has_side_effects=_EFFECT),
    )(*[pltpu.with_memory_space_constraint(a, pltpu.HBM) for a in srcs + lands], *deps)
    return (kinds, res[:3], res[3:3 + 2 * n]), res[-1]


def _exchange_wait(handle, after, name):
    kinds, sems, held = handle
    n = len(kinds)

    def body(*refs):
        send_sems, recv_sems, local_sems = refs[2 * n:2 * n + 3]
        remote, local = _exchange_copies(kinds, refs[:n], refs[n:2 * n], send_sems, recv_sems, local_sems, True)
        for cp in remote:
            cp.wait_send()
            cp.wait_recv()
        for cp in local:
            cp.wait()

    res = pl.pallas_call(
        body, name=name, out_shape=[pltpu.HBM(a.shape, a.dtype) for a in held],
        in_specs=[_HBM] * (2 * n) + [_SEM] * 3 + [_ANY] * len(after), out_specs=[_HBM] * (2 * n),
        input_output_aliases={i: i for i in range(2 * n)},
        compiler_params=pltpu.CompilerParams(has_side_effects=_EFFECT),
    )(*held, *sems, *after)
    return res[n:]


_CHIPS = [(0, 0), (0, 1), (1, 0), (1, 1)]
_N_CHIPS = len(_CHIPS)


def _sibling_forward(lands, name):
    n = len(lands)

    def body(*refs):
        ins, outs = refs[:n], refs[n:2 * n]
        send_sems, recv_sems = refs[2 * n:]
        x, y, c = _place()
        sent, arriving = [], []
        for a in range(n):
            for j, (fx, fy, _) in enumerate(_CHIP_FLIPS):
                chip = (1 - x if fx else x, 1 - y if fy else y)
                sems = dict(send_sem=send_sems.at[a * 3 + j], recv_sem=recv_sems.at[a * 3 + j],
                            device_id=(x, y, 1 - c), device_id_type=MESH)
                mine, theirs = _index(*chip, c), _index(*chip, 1 - c)
                sent.append(pltpu.make_async_remote_copy(src_ref=ins[a].at[mine], dst_ref=outs[a].at[mine], **sems))
                arriving.append(pltpu.make_async_remote_copy(src_ref=ins[a].at[theirs], dst_ref=outs[a].at[theirs],
                                                             **sems))
        for cp in sent:
            cp.start()
        for cp in sent:
            cp.wait_send()
        for cp in arriving:
            cp.wait_recv()

    return pl.pallas_call(
        body, name=name, in_specs=[_ANY] * n, out_specs=[_ANY] * n,
        out_shape=[S(a.shape, a.dtype) for a in lands], input_output_aliases={a: a for a in range(n)},
        scratch_shapes=[pltpu.SemaphoreType.DMA((3 * n,)), pltpu.SemaphoreType.DMA((3 * n,))],
    )(*lands)


def _sibling_swap(arrs, name):
    n = len(arrs)

    def body(*refs):
        ins, outs = refs[:n], refs[n:2 * n]
        send_sems, recv_sems = refs[2 * n:]
        x, y, c = _place()
        copies = []
        for a in range(n):
            for q, chip in enumerate(_CHIPS):
                copies.append(pltpu.make_async_remote_copy(
                    src_ref=ins[a].at[_index(*chip, 1 - c)], dst_ref=outs[a].at[q],
                    send_sem=send_sems.at[a * _N_CHIPS + q], recv_sem=recv_sems.at[a * _N_CHIPS + q],
                    device_id=(x, y, 1 - c), device_id_type=MESH))
        for cp in copies:
            cp.start()
        for cp in copies:
            cp.wait_send()
            cp.wait_recv()

    return pl.pallas_call(
        body, name=name, in_specs=[_ANY] * n, out_specs=[_ANY] * n,
        out_shape=[S((_N_CHIPS,) + a.shape[1:], a.dtype) for a in arrs],
        scratch_shapes=[pltpu.SemaphoreType.DMA((_N_CHIPS * n,)), pltpu.SemaphoreType.DMA((_N_CHIPS * n,))],
    )(*arrs)


def _add_partials(mine, theirs, *, tr, name):
    _, r, c = mine.shape

    def body(mine_ref, theirs_ref, out_ref):
        core = lax.axis_index("c")
        own = jnp.where(core == 0, mine_ref[0].astype(F32), mine_ref[1].astype(F32))
        out_ref[0] = (own + theirs_ref[0].astype(F32)).astype(out_ref.dtype)

    return pl.pallas_call(
        body, name=name, grid=(_N_CHIPS, r // tr),
        in_specs=[pl.BlockSpec((2, tr, c), lambda q, i: (q, i, 0)), pl.BlockSpec((1, tr, c), lambda q, i: (q, i, 0))],
        out_specs=pl.BlockSpec((1, tr, c), lambda q, i: (q, i, 0)),
        out_shape=S((_N_CHIPS, r, c), mine.dtype),
        compiler_params=_params(("arbitrary", "arbitrary")),
    )(mine, theirs)


def _adamw(w, g, m, v):
    m = ADAM_B1 * m + (1.0 - ADAM_B1) * g
    v = ADAM_B2 * v + (1.0 - ADAM_B2) * (g * g)
    m_hat = m / (1.0 - ADAM_B1 ** ADAM_STEP)
    v_hat = v / (1.0 - ADAM_B2 ** ADAM_STEP)
    return -ADAM_LR * (m_hat / (jnp.sqrt(v_hat) + ADAM_EPS) + ADAM_WD * w), m, v


def _sum_devices(ref, rows):
    total = ref[0, rows, :].astype(F32)
    for s in range(1, ref.shape[0]):
        total = total + ref[s, rows, :].astype(F32)
    return total


def _adam_big(recv, w, m, v, *, tr, name):
    r, c = w.shape

    def body(recv_ref, w_ref, m_ref, v_ref, g_ref, d_ref, m2_ref, v2_ref):
        g = _sum_devices(recv_ref, slice(None))
        g_ref[...] = g
        d_ref[...], m2_ref[...], v2_ref[...] = _adamw(w_ref[...], g, m_ref[...], v_ref[...])

    blk = pl.BlockSpec((tr, c), lambda i: (i, 0))
    return pl.pallas_call(
        body, name=name, grid=(r // tr,),
        in_specs=[pl.BlockSpec((recv.shape[0], tr, c), lambda i: (0, i, 0)), blk, blk, blk],
        out_specs=[blk] * 4, out_shape=[S((r, c), F32)] * 4,
        compiler_params=_params(("arbitrary",)),
    )(recv, w, m, v)


_REPLICATED = {"even_pre_g": (0, 0, 1), "even_mem_g": (0, 8, 1), "even_post_g": (0, 16, 1),
               "even_a_ln_g": (1, 0, 1), "even_a_ln_b": (1, 8, 1),
               "even_a_ws": (2, 0, NH * CHUNK), "even_a_bs": (2, NH * CHUNK, NH),
               "odd_c_wgrp": (3, 0, NH * GRP)}
_SHARDED = {"odd_pre_g": (4, 0, 1), "odd_mem_g": (4, 8, 1), "odd_post_g": (4, 16, 1),
            "even_b_conv": (5, 0, 3), "odd_c_scale": (5, 8, 1), "odd_d_dw_w": (5, 16, CONF_K),
            "odd_d_dw_b": (5, 48, 1), "odd_d_ln_g": (5, 56, 1), "odd_d_ln_b": (5, 64, 1), "odd_d_pw_b": (5, 72, 1)}
_SMALL = {**_REPLICATED, **_SHARDED}
_SMALL_ROWS = {0: 24, 1: 16, 2: NH * CHUNK + 8, 3: NH * GRP, 4: 24, 5: 80}


def _adam_small(sources, wmv):
    names = list(_SMALL)
    ns = len(sources)

    def body(*refs):
        src = refs[:ns]
        ins = refs[ns:ns + 3 * len(names)]
        outs = refs[ns + 3 * len(names):]
        outs[-1][...] = _sum_devices(src[-1], slice(0, 1))
        for i, nm in enumerate(names):
            a, row0, rows = _SMALL[nm]
            g = _sum_devices(src[a], slice(row0, row0 + rows))
            w_ref, m_ref, v_ref = ins[3 * i:3 * i + 3]
            g_ref, d_ref, m2_ref, v2_ref = outs[4 * i:4 * i + 4]
            g_ref[...] = g
            d_ref[...], m2_ref[...], v2_ref[...] = _adamw(w_ref[...], g, m_ref[...], v_ref[...])

    flat = [t for nm in names for t in wmv[nm]]
    out_shape = [S(wmv[nm][0].shape, F32) for nm in names for _ in range(4)] + [S((1, HD), F32)]
    res = pl.pallas_call(
        body, name="adam_small", in_specs=[_whole()] * (ns + len(flat)), out_specs=[_whole()] * len(out_shape),
        out_shape=out_shape, compiler_params=_params(),
    )(*sources, *flat)
    return {nm: tuple(res[4 * i:4 * i + 4]) for i, nm in enumerate(names)}, res[-1]


_WEIGHTS = ["even_pre_g", "even_w_in", "even_a_ln_g", "even_a_ln_b", "even_a_ws", "even_a_bs", "even_b_conv",
            "even_mem_g", "even_w_kv", "even_w_out", "even_post_g", "odd_pre_g", "odd_w_in", "odd_c_wgrp",
            "odd_c_scale", "odd_d_dw_w", "odd_d_dw_b", "odd_d_ln_g", "odd_d_ln_b", "odd_d_pw_w", "odd_d_pw_b",
            "odd_mem_g", "odd_w_kv", "odd_w_out", "odd_post_g"]
_TRANSPOSED = ["even_w_in", "odd_w_in"]
_BIG = _TRANSPOSED + ["even_w_kv", "even_w_out", "odd_w_kv", "odd_w_out", "odd_d_pw_w"]
_BIG_TILE_ROWS = {"even_w_in": 400, "odd_w_in": 304, "even_w_kv": 128, "even_w_out": 128, "odd_w_kv": 128,
                  "odd_w_out": 128, "odd_d_pw_w": 96}


def _view2d(a, transposed):
    a = a[0]
    if a.ndim == 1:
        return a[None]
    if transposed:
        return a.T
    return a.reshape(-1, a.shape[-1])


def _rows8(a):
    return _pad_rows(a, -(-a.shape[0] // 8) * 8)


def _pack_rows(parts):
    return jnp.concatenate([_rows8(p) for p in parts], axis=0)


def _unshard_cols(a):
    return jnp.transpose(a, (1, 0, 2)).reshape(a.shape[1], N_DEV * a.shape[2])


def _shard_cols(a):
    return jnp.transpose(a.reshape(a.shape[0], N_DEV, a.shape[1] // N_DEV), (1, 0, 2))


def _rows_of(a):
    return a.reshape(-1, a.shape[-1])


_GROUPS = {"odd_rest": (["odd_w_out", "odd_w_kv", "odd_d_pw_w"], [3], []),
           "odd_in": (["odd_w_in"], [], []),
           "even_rest": (["even_w_out", "even_w_kv"], [1, 2], []),
           "even_in": (["even_w_in"], [], [4, 5]),
           "even_gains": ([], [0], [])}


_TWO_LEVEL = ("even_in",)


class _MeshExchange:
    def __init__(self, shard):
        self.shard = shard
        self.handles = {}

    def first(self):
        shard = self.shard
        packs = [_pack_rows([shard[nm] for nm in _SHARDED if _SHARDED[nm][0] == a]) for a in (4, 5)]
        w_in, p128, p96 = _all_gather([shard["even_w_in"].astype(BF16)] + packs, "gather_first")
        w = {nm: shard[nm] for nm in _REPLICATED}
        w["even_a_ws"] = w["even_a_ws"].reshape(NH, CHUNK, CHUNK)
        w["odd_c_wgrp"] = w["odd_c_wgrp"].reshape(NH, GRP, GRP)
        w["even_w_in"] = _rows_of(w_in)
        full_packs = {4: _unshard_cols(p128), 5: _unshard_cols(p96)}
        for nm, (a, row0, rows) in _SHARDED.items():
            w[nm] = full_packs[a][row0:row0 + rows]
        later = lambda names: [("gather_chips", shard[nm].astype(BF16)) for nm in names]
        self.handles["w_even"], token = _exchange_start(later(["even_w_kv", "even_w_out"]), "gather_even_start",
                                                        deps=(w_in,))
        self.handles["w_odd"], token = _exchange_start(later(["odd_w_in", "odd_w_kv", "odd_w_out", "odd_d_pw_w"]),
                                                       "gather_odd_start", deps=(token,))
        return w, (token,)

    def even_rest(self, after):
        landed = _exchange_wait(self.handles.pop("w_even"), (after,), "gather_even_wait")
        kv, out = _sibling_forward(landed, "forward_even")
        return {"even_w_kv": _rows_of(kv), "even_w_out": _rows_of(out)}

    def odd(self, after):
        landed = _exchange_wait(self.handles.pop("w_odd"), (after,), "gather_odd_wait")
        w_in, kv, out, pw = _sibling_forward(landed, "forward_odd")
        return {"odd_w_in": _rows_of(w_in), "odd_w_kv": _rows_of(kv), "odd_w_out": _rows_of(out),
                "odd_d_pw_w": _rows_of(pw)}

    def send(self, group, g):
        big, replicated, sharded = _GROUPS[group]
        by_owner = [g[nm].reshape(N_DEV, -1, g[nm].shape[-1]) for nm in big]
        if group in _TWO_LEVEL:
            theirs = _sibling_swap(by_owner, "swap_" + group)
            items = [("scatter_chips", _add_partials(a, b, tr=_BIG_TILE_ROWS[nm], name="chip_sum_" + nm))
                     for nm, a, b in zip(big, by_owner, theirs)]
        else:
            items = [("scatter", a) for a in by_owner]
        items += [("gather", _pack_rows([g[nm] for nm in _REPLICATED if _REPLICATED[nm][0] == a]))
                  for a in replicated]
        items += [("scatter", _shard_cols(_pack_rows([g[nm] for nm in _SHARDED if _SHARDED[nm][0] == a])))
                  for a in sharded]
        if group == "odd_rest":
            items.append(("gather", _rows8(g["loss"])))
        self.handles[group], token = _exchange_start(items, "send_" + group + "_start")
        return (token,)

    def receive(self, group, after):
        after = after if isinstance(after, tuple) else (after,)
        return _exchange_wait(self.handles.pop(group), after, "send_" + group + "_wait")


def kernel(x, mem, even_pre_g, even_w_in, even_a_ln_g, even_a_ln_b, even_a_ws, even_a_bs, even_b_conv, even_mem_g, even_w_kv, even_w_out, even_post_g, odd_pre_g, odd_w_in, odd_c_wgrp, odd_c_scale, odd_d_dw_w, odd_d_dw_b, odd_d_ln_g, odd_d_ln_b, odd_d_pw_w, odd_d_pw_b, odd_mem_g, odd_w_kv, odd_w_out, odd_post_g, loss_target, m_even_pre_g, m_even_w_in, m_even_a_ln_g, m_even_a_ln_b, m_even_a_ws, m_even_a_bs, m_even_b_conv, m_even_mem_g, m_even_w_kv, m_even_w_out, m_even_post_g, m_odd_pre_g, m_odd_w_in, m_odd_c_wgrp, m_odd_c_scale, m_odd_d_dw_w, m_odd_d_dw_b, m_odd_d_ln_g, m_odd_d_ln_b, m_odd_d_pw_w, m_odd_d_pw_b, m_odd_mem_g, m_odd_w_kv, m_odd_w_out, m_odd_post_g, v_even_pre_g, v_even_w_in, v_even_a_ln_g, v_even_a_ln_b, v_even_a_ws, v_even_a_bs, v_even_b_conv, v_even_mem_g, v_even_w_kv, v_even_w_out, v_even_post_g, v_odd_pre_g, v_odd_w_in, v_odd_c_wgrp, v_odd_c_scale, v_odd_d_dw_w, v_odd_d_dw_b, v_odd_d_ln_g, v_odd_d_ln_b, v_odd_d_pw_w, v_odd_d_pw_b, v_odd_mem_g, v_odd_w_kv, v_odd_w_out, v_odd_post_g):
    given = dict(locals())
    view = lambda nm, kind: _view2d(given[kind + nm], nm in _TRANSPOSED)
    shard = {nm: view(nm, "") for nm in _WEIGHTS}
    wmv = {nm: (shard[nm], view(nm, "m_"), view(nm, "v_")) for nm in _WEIGHTS}

    ex = _MeshExchange(shard)
    grad_x, last = _step(x[0], mem[0], loss_target[0], ex)

    res = {}

    def update(group, after):
        names = _GROUPS[group][0]
        landed = ex.receive(group, after)
        for nm, recv in zip(names, landed):
            res[nm] = _adam_big(recv, *wmv[nm], tr=_BIG_TILE_ROWS[nm], name="adam_" + nm)
        return landed[len(names):]

    c192, losses = update("odd_rest", last)
    update("odd_in", res["odd_d_pw_w"][0])
    c768, c128 = update("even_rest", res["odd_w_in"][0])
    a128, a96 = update("even_in", res["even_w_kv"][0])
    (c1024,) = update("even_gains", res["even_w_in"][0])
    small, loss = _adam_small([c1024, c768, c128, c192, a128, a96, losses], {nm: wmv[nm] for nm in _SMALL})
    res.update(small)
    total = loss[0, 0]
    back = lambda nm, a: (a.T if nm in _TRANSPOSED else a).reshape(given[nm].shape)
    outs = [[back(nm, res[nm][i]) for nm in _WEIGHTS] for i in range(4)]
    return (total, grad_x[None], *outs[0], *outs[1], *outs[2], *outs[3])
```

```python
import functools

import jax
import jax.numpy as jnp
from jax import lax
from jax.experimental import pallas as pl
from jax.experimental.pallas import tpu as pltpu

F32 = jnp.float32
BF16 = jnp.bfloat16
S = jax.ShapeDtypeStruct
MESH = pl.DeviceIdType.MESH
AXES = ("x", "y", "c")
N_DEV = 8

D = 1024
BW = 768
XA = 512
HD = 128
NH = 4
MIX = 2048
CHUNK = 128
GRP = 192
N_MEM = 256
CONF_K = 31
EPS = 1e-6
HALO = 32
POOL_WINDOWS = (2, 4, 8, 16)
TM_FWD_EVEN = 512
TM_FWD_ODD = 256
TM_BWD_EVEN = 256
TM_BWD_ODD = 256
RB = 16

E_U, E_V, E_BG, E_CG, E_XIN, E_Q, E_GATE = 0, 768, 1536, 2304, 3072, 3840, 4352
EVEN_IN = 6400
O_ZC, O_GA, O_GB, O_Q, O_GATE = 0, 768, 1536, 2304, 2816
ODD_IN = 4864

ADAM_LR, ADAM_B1, ADAM_B2, ADAM_EPS, ADAM_WD, ADAM_STEP = 0.001, 0.9, 0.999, 1e-08, 0.01, 10

VMEM_LIMIT_V7X = 56 * 1024 * 1024
VMEM_LIMIT_ODD_BWD_V7X = 62 * 1024 * 1024


def _params(sem=None):
    return pltpu.CompilerParams(dimension_semantics=sem, vmem_limit_bytes=VMEM_LIMIT_V7X)


def _dot(a, b):
    return jnp.dot(a, b, preferred_element_type=F32)


def _dot_nt(a, b):
    return lax.dot_general(a, b, (((1,), (1,)), ((), ())), preferred_element_type=F32)


def _dot_tn(a, b):
    return lax.dot_general(a, b, (((0,), (0,)), ((), ())), preferred_element_type=F32)


def _sigmoid(z):
    return 1.0 / (1.0 + jnp.exp(-z))


def _rowmean(a):
    return jnp.mean(a, axis=-1, keepdims=True)


def _colsum(a):
    return jnp.sum(a, axis=0, keepdims=True)


def _ln_stats(v):
    mu = _rowmean(v)
    vc = v - mu
    rs = lax.rsqrt(_rowmean(vc * vc) + EPS)
    return vc * rs, rs


def _ln_bwd(dn, vh, rs, g):
    dvh = dn * g
    return rs * (dvh - _rowmean(dvh) - vh * _rowmean(dvh * vh))


def _group_masks():
    col = lax.broadcasted_iota(jnp.int32, (1, BW), 1)
    return [((col >= GRP * h) & (col < GRP * (h + 1))).astype(F32) for h in range(NH)]


def _full(shape):
    nd = len(shape)
    return pl.BlockSpec(shape, lambda *_: (0,) * nd)


def _whole():
    return pl.BlockSpec(memory_space=pltpu.VMEM)


_ANY = pl.BlockSpec(memory_space=pl.ANY)


def _after(body, n_in, deps):
    def ordered(*refs):
        return body(*refs[:n_in], *refs[n_in + len(deps):])
    return ordered


def _rms_matmul(x, g, w, *, tm, name, transposed=False, out_dtype=F32, deps=()):
    t, d = x.shape
    n = w.shape[0] if transposed else w.shape[1]

    def body(x_ref, g_ref, w_ref, p_ref, h_ref):
        xv = x_ref[...]
        r = lax.rsqrt(_rowmean(xv * xv) + EPS)
        h = (xv * r * g_ref[...]).astype(BF16)
        h_ref[...] = h
        p_ref[...] = (_dot_nt(h, w_ref[...]) if transposed else _dot(h, w_ref[...])).astype(out_dtype)

    return pl.pallas_call(
        _after(body, 3, deps), name=name, grid=(t // tm,),
        in_specs=[pl.BlockSpec((tm, d), lambda i: (i, 0)), _whole(), _whole()] + [_ANY] * len(deps),
        out_specs=[pl.BlockSpec((tm, n), lambda i: (i, 0)), pl.BlockSpec((tm, d), lambda i: (i, 0))],
        out_shape=[S((t, n), out_dtype), S((t, d), BF16)],
        compiler_params=_params(("arbitrary",)),
    )(x, g, w, *deps)


def _nt_matmul_rms_bwd(dp, w, x, g, dres, *, tm, name, transposed=False, deps=()):
    t, n = dp.shape
    d = x.shape[1]

    def body(dp_ref, w_ref, x_ref, g_ref, dres_ref, dx_ref, dg_ref):
        @pl.when(pl.program_id(0) == 0)
        def _():
            dg_ref[...] = jnp.zeros_like(dg_ref)

        dh = _dot(dp_ref[...], w_ref[...]) if transposed else _dot_nt(dp_ref[...], w_ref[...])
        xv = x_ref[...]
        r = lax.rsqrt(_rowmean(xv * xv) + EPS)
        xh = xv * r
        dg_ref[...] += _colsum(dh * xh)
        dxh = dh * g_ref[...]
        dx_ref[...] = dres_ref[...] + r * (dxh - xh * _rowmean(dxh * xh))

    return pl.pallas_call(
        _after(body, 5, deps), name=name, grid=(t // tm,),
        in_specs=[pl.BlockSpec((tm, n), lambda i: (i, 0)), _whole(), pl.BlockSpec((tm, d), lambda i: (i, 0)),
                  _whole(), pl.BlockSpec((tm, d), lambda i: (i, 0))] + [_ANY] * len(deps),
        out_specs=[pl.BlockSpec((tm, d), lambda i: (i, 0)), pl.BlockSpec((1, d), lambda i: (0, 0))],
        out_shape=[S((t, d), F32), S((1, d), F32)],
        compiler_params=_params(("arbitrary",)),
    )(dp, w, x, g, dres, *deps)


def _tn_matmul(a, b, *, tmc, tk, out_dtype, name, deps=()):
    t, m = a.shape
    n = b.shape[1]
    nk = t // tk

    def body(a_ref, b_ref, o_ref, acc_ref):
        k = pl.program_id(1)

        @pl.when(k == 0)
        def _():
            acc_ref[...] = jnp.zeros_like(acc_ref)

        acc_ref[...] += _dot_tn(a_ref[...], b_ref[...])

        @pl.when(k == nk - 1)
        def _():
            o_ref[...] = acc_ref[...].astype(out_dtype)

    return pl.pallas_call(
        _after(body, 2, deps), name=name, grid=(m // tmc, nk),
        in_specs=[pl.BlockSpec((tk, tmc), lambda j, k: (k, j)), pl.BlockSpec((tk, n), lambda j, k: (k, 0))]
        + [_ANY] * len(deps),
        out_specs=pl.BlockSpec((tmc, n), lambda j, k: (j, 0)),
        out_shape=S((m, n), out_dtype),
        scratch_shapes=[pltpu.VMEM((tmc, n), F32)],
        compiler_params=_params(("arbitrary", "arbitrary")),
    )(a, b, *deps)


def _silu_parts(gt):
    sg = _sigmoid(gt)
    return gt * sg, sg * (1.0 + gt * (1.0 - sg))


def _attn_head(q_b, k_b, v_b):
    s = _dot_nt(q_b, k_b) * (HD ** -0.5)
    e = jnp.exp(s - jnp.max(s, axis=-1, keepdims=True))
    prob = e / jnp.sum(e, axis=-1, keepdims=True)
    return prob, _dot(prob.astype(BF16), v_b)


def _rms_residual(x, o, g):
    r = lax.rsqrt(_rowmean(o * o) + EPS)
    return x + o * r * g


def _rms_post_bwd(dres, o, g):
    r = lax.rsqrt(_rowmean(o * o) + EPS)
    oh = o * r
    doh = dres * g
    return r * (doh - oh * _rowmean(doh * oh)), _colsum(dres * oh)


LANE = 128
_TILE_GROUPS = [sorted({LANE * j // GRP, (LANE * j + LANE - 1) // GRP}) for j in range(BW // LANE)]


def _tile(j):
    return slice(LANE * j, LANE * (j + 1))


def _low_lanes():
    return lax.broadcasted_iota(jnp.int32, (1, LANE), 1) < GRP - LANE


def _by_group(fn):
    tiles = []
    for j, groups in enumerate(_TILE_GROUPS):
        if len(groups) == 1:
            tiles.append(fn(groups[0], j))
        else:
            tiles.append(jnp.where(_low_lanes(), fn(groups[0], j), fn(groups[1], j)))
    return jnp.concatenate(tiles, axis=1)


def _sgu_chunk(vn_b, ws_ref, bmap_ref):
    return bmap_ref[...] + _by_group(lambda h, j: _dot(ws_ref[h], vn_b[:, _tile(j)]))


def _shift_copies(buf, sh):
    n = buf.shape[0] - 8
    for b in range(1, 8):
        sh[b - 1, pl.ds(0, n), :] = buf[pl.ds(b, n), :]


def _loop_rows(rows, step, fn, carry=0, unrolled=True):
    if unrolled:
        for r0 in range(0, rows, step):
            carry = fn(r0, carry)
        return carry

    def body(j, c):
        return fn(pl.multiple_of(j * step, step), c)
    return lax.fori_loop(0, rows // step, body, carry)


def _rows_at(buf, sh, r0, off):
    b = off % 8
    if b == 0 or sh is None:
        return buf[pl.ds(r0 + off, 32), :]
    return sh[b - 1, pl.ds(r0 + (off - b), 32), :]


def _tap_sum(buf, sh, w_ref, r0, taps, causal):
    acc = None
    for k in range(taps):
        off = HALO - (taps - 1 - k) if causal else taps - 1 - k
        term = w_ref[k:k + 1, :] * _rows_at(buf, sh, r0, off)
        acc = term if acc is None else acc + term
    return acc


def _fold8(a):
    return a[0:8] + a[8:16] + a[16:24] + a[24:32]


def _tap_grads(dv, buf, sh, acc_ref, r0, taps):
    for k in range(taps):
        acc_ref[k * 8:(k + 1) * 8, :] += _fold8(dv * _rows_at(buf, sh, r0, HALO - (taps - 1 - k)))


def _halo_spec(n, nt, reverse, tm):
    per = tm // HALO
    if reverse:
        return pl.BlockSpec((HALO, n), lambda i: (jnp.maximum((nt - 1 - i) * per - 1, 0), 0))
    return pl.BlockSpec((HALO, n), lambda i: (jnp.maximum(i * per - 1, 0), 0))


def _even_fwd(x, p, lng, lnb, ws, bmap, wc, kv, wout, pg):
    t = x.shape[0]
    tm = min(TM_FWD_EVEN, t)
    nt = t // tm

    def body(x_ref, p_ref, ph_ref, lng_ref, lnb_ref, ws_ref, bmap_ref, wc_ref, kv_ref, wout_ref, pg_ref,
             o_ref, x1_ref, ybuf, cbuf):
        i = pl.program_id(0)
        vh, _ = _ln_stats(p_ref[:, E_V:E_V + BW])
        vn = vh * lng_ref[...] + lnb_ref[...]
        for c in range(tm // CHUNK):
            sl = slice(c * CHUNK, (c + 1) * CHUNK)
            sg = _sgu_chunk(vn[sl].astype(BF16), ws_ref, bmap_ref)
            gate, _ = _silu_parts(p_ref[sl, E_GATE:E_GATE + BW])
            ybuf[sl, 0:BW] = (p_ref[sl, E_U:E_U + BW] * sg * gate).astype(BF16)

        cbuf[0:HALO] = jnp.where(i > 0, ph_ref[:, E_CG:E_CG + BW] * ph_ref[:, E_XIN:E_XIN + BW], 0.0)
        cbuf[HALO:HALO + tm] = p_ref[:, E_CG:E_CG + BW] * p_ref[:, E_XIN:E_XIN + BW]
        for r0 in range(0, tm, 32):
            sl = slice(r0, r0 + 32)
            cv = _tap_sum(cbuf, None, wc_ref, r0, 3, True)
            gate, _ = _silu_parts(p_ref[sl, E_GATE + BW:E_GATE + 2 * BW])
            ybuf[sl, BW:2 * BW] = (p_ref[sl, E_BG:E_BG + BW] * cv * gate).astype(BF16)

        for h in range(NH):
            qs = slice(E_Q + h * HD, E_Q + (h + 1) * HD)
            _, yx = _attn_head(p_ref[:, qs].astype(BF16), kv_ref[:, h * HD:(h + 1) * HD],
                               kv_ref[:, XA + h * HD:XA + (h + 1) * HD])
            gs = slice(E_GATE + 2 * BW + h * HD, E_GATE + 2 * BW + (h + 1) * HD)
            gate, _ = _silu_parts(p_ref[:, gs])
            ybuf[:, 2 * BW + h * HD:2 * BW + (h + 1) * HD] = (yx * gate).astype(BF16)

        o = _dot(ybuf[...], wout_ref[...])
        o_ref[...] = o
        x1_ref[...] = _rms_residual(x_ref[...], o, pg_ref[...])

    tile = lambda n: pl.BlockSpec((tm, n), lambda i: (i, 0))
    return pl.pallas_call(
        body, name="even_fwd", grid=(nt,),
        in_specs=[tile(D), tile(EVEN_IN), _halo_spec(EVEN_IN, nt, False, tm)] + [_whole()] * 8,
        out_specs=[tile(D), tile(D)],
        out_shape=[S((t, D), F32), S((t, D), F32)],
        scratch_shapes=[pltpu.VMEM((tm, MIX), BF16), pltpu.VMEM((tm + HALO, BW), F32)],
        compiler_params=_params(("arbitrary",)),
    )(x, p, p, lng, lnb, ws, bmap, wc, kv, wout, pg)


def _even_bwd(dres, o, p, lng, lnb, ws, wst, bmap, wc, kv, wout, pg):
    t = dres.shape[0]
    tm = min(TM_BWD_EVEN, t)
    nt = t // tm

    def body(dres_ref, o_ref, p_ref, ph_ref, lng_ref, lnb_ref, ws_ref, wst_ref, bmap_ref, wc_ref, kv_ref, wout_ref,
             pg_ref, dp_ref, y_ref, do_ref, dpg_ref, dws_ref, dbs_ref, dlng_ref, dlnb_ref, dwc_ref, dkv_ref,
             dy, cbuf, gbuf, dconv, carry, dvn, dbmap, wacc):
        i = pl.program_id(0)
        ti = nt - 1 - i
        masks = _group_masks()

        @pl.when(i == 0)
        def _():
            for ref in (dpg_ref, dws_ref, dlng_ref, dlnb_ref, dkv_ref, dbmap, wacc):
                ref[...] = jnp.zeros_like(ref)

        do, dpg = _rms_post_bwd(dres_ref[...], o_ref[...], pg_ref[...])
        dpg_ref[...] += dpg
        do_b = do.astype(BF16)
        do_ref[...] = do_b
        dy[...] = _dot_nt(do_b, wout_ref[...])

        vh, rs = _ln_stats(p_ref[:, E_V:E_V + BW])
        vn = vh * lng_ref[...] + lnb_ref[...]
        for c in range(tm // CHUNK):
            sl = slice(c * CHUNK, (c + 1) * CHUNK)
            vn_b = vn[sl].astype(BF16)
            sg = _sgu_chunk(vn_b, ws_ref, bmap_ref)
            u = p_ref[sl, E_U:E_U + BW]
            gate, dgate = _silu_parts(p_ref[sl, E_GATE:E_GATE + BW])
            dyc = dy[sl, 0:BW]
            ya = u * sg
            y_ref[sl, 0:BW] = (ya * gate).astype(BF16)
            dp_ref[sl, E_GATE:E_GATE + BW] = (dyc * ya * dgate).astype(BF16)
            dya = dyc * gate
            dp_ref[sl, E_U:E_U + BW] = (dya * sg).astype(BF16)
            dsg = dya * u
            dbmap[...] += dsg
            dsg_b = dsg.astype(BF16)
            for h in range(NH):
                total = None
                for j, heads in enumerate(_TILE_GROUPS):
                    if h in heads:
                        d_t = dsg_b[:, _tile(j)]
                        if len(heads) == 2:
                            d_t = jnp.where(_low_lanes() == (h == heads[0]), d_t, jnp.zeros_like(d_t))
                        part = _dot_nt(d_t, vn_b[:, _tile(j)])
                        total = part if total is None else total + part
                dws_ref[h] += total
            dvn[sl, :] = _by_group(lambda h, j: _dot(wst_ref[h], dsg_b[:, _tile(j)]))
        dn = dvn[...]
        dlng_ref[...] += _colsum(dn * vh)
        dlnb_ref[...] += _colsum(dn)
        dp_ref[:, E_V:E_V + BW] = _ln_bwd(dn, vh, rs, lng_ref[...]).astype(BF16)

        cbuf[0:HALO] = jnp.where(ti > 0, ph_ref[:, E_CG:E_CG + BW] * ph_ref[:, E_XIN:E_XIN + BW], 0.0)
        cbuf[HALO:HALO + tm] = p_ref[:, E_CG:E_CG + BW] * p_ref[:, E_XIN:E_XIN + BW]
        for r0 in range(0, tm, 32):
            sl = slice(r0, r0 + 32)
            cv = _tap_sum(cbuf, None, wc_ref, r0, 3, True)
            gate, dgate = _silu_parts(p_ref[sl, E_GATE + BW:E_GATE + 2 * BW])
            bg = p_ref[sl, E_BG:E_BG + BW]
            dyc = dy[sl, BW:2 * BW]
            yb = bg * cv
            y_ref[sl, BW:2 * BW] = (yb * gate).astype(BF16)
            dp_ref[sl, E_GATE + BW:E_GATE + 2 * BW] = (dyc * yb * dgate).astype(BF16)
            dyb = dyc * gate
            dp_ref[sl, E_BG:E_BG + BW] = (dyb * cv).astype(BF16)
            dconv[sl, :] = dyb * bg
        gbuf[0:tm] = dconv[...]
        gbuf[tm:tm + HALO] = jnp.where(i > 0, carry[...], 0.0)
        carry[...] = dconv[0:HALO]
        for r0 in range(0, tm, 32):
            sl = slice(r0, r0 + 32)
            _tap_grads(dconv[sl, :], cbuf, None, wacc, r0, 3)
            dc = _tap_sum(gbuf, None, wc_ref, r0, 3, False)
            dp_ref[sl, E_CG:E_CG + BW] = (dc * p_ref[sl, E_XIN:E_XIN + BW]).astype(BF16)
            dp_ref[sl, E_XIN:E_XIN + BW] = (dc * p_ref[sl, E_CG:E_CG + BW]).astype(BF16)

        for h in range(NH):
            qs = slice(E_Q + h * HD, E_Q + (h + 1) * HD)
            ks = slice(h * HD, (h + 1) * HD)
            vs = slice(XA + h * HD, XA + (h + 1) * HD)
            gs = slice(E_GATE + 2 * BW + h * HD, E_GATE + 2 * BW + (h + 1) * HD)
            ys = slice(2 * BW + h * HD, 2 * BW + (h + 1) * HD)
            q_b = p_ref[:, qs].astype(BF16)
            prob, yx = _attn_head(q_b, kv_ref[:, ks], kv_ref[:, vs])
            gate, dgate = _silu_parts(p_ref[:, gs])
            dyc = dy[:, ys]
            y_ref[:, ys] = (yx * gate).astype(BF16)
            dp_ref[:, gs] = (dyc * yx * dgate).astype(BF16)
            dyx_b = (dyc * gate).astype(BF16)
            dprob = _dot_nt(dyx_b, kv_ref[:, vs])
            dkv_ref[:, vs] += _dot_tn(prob.astype(BF16), dyx_b)
            ds_b = (prob * (dprob - jnp.sum(dprob * prob, axis=-1, keepdims=True)) * (HD ** -0.5)).astype(BF16)
            dp_ref[:, qs] = _dot(ds_b, kv_ref[:, ks]).astype(BF16)
            dkv_ref[:, ks] += _dot_tn(ds_b, q_b)

        @pl.when(i == nt - 1)
        def _():
            for h in range(NH):
                dbs_ref[:, h * HD:(h + 1) * HD] = jnp.broadcast_to(
                    jnp.sum(dbmap[...] * masks[h], axis=-1, keepdims=True), (CHUNK, HD))
            for k in range(3):
                dwc_ref[k:k + 1, :] = _colsum(wacc[k * 8:(k + 1) * 8, :])
            dwc_ref[3:8, :] = jnp.zeros((5, BW), F32)
            causal = (lax.broadcasted_iota(jnp.int32, (CHUNK, CHUNK), 0)
                      >= lax.broadcasted_iota(jnp.int32, (CHUNK, CHUNK), 1))
            for h in range(NH):
                dws_ref[h] = jnp.where(causal, dws_ref[h], 0.0)

    rtile = lambda n: pl.BlockSpec((tm, n), lambda i: (nt - 1 - i, 0))
    outs = [S((t, EVEN_IN), BF16), S((t, MIX), BF16), S((t, D), BF16), S((1, D), F32), S((NH, CHUNK, CHUNK), F32),
            S((CHUNK, NH * HD), F32), S((1, BW), F32), S((1, BW), F32), S((8, BW), F32), S((N_MEM, 2 * XA), F32)]
    return pl.pallas_call(
        body, name="even_bwd", grid=(nt,),
        in_specs=[rtile(D), rtile(D), rtile(EVEN_IN), _halo_spec(EVEN_IN, nt, True, tm)] + [_whole()] * 9,
        out_specs=[rtile(EVEN_IN), rtile(MIX), rtile(D)] + [_full(s.shape) for s in outs[3:]],
        out_shape=outs,
        scratch_shapes=[pltpu.VMEM((tm, MIX), F32), pltpu.VMEM((tm + HALO, BW), F32), pltpu.VMEM((tm + HALO, BW), F32),
                        pltpu.VMEM((tm, BW), F32), pltpu.VMEM((HALO, BW), F32), pltpu.VMEM((tm, BW), F32),
                        pltpu.VMEM((CHUNK, BW), F32), pltpu.VMEM((3 * 8, BW), F32)],
        compiler_params=_params(("arbitrary",)),
    )(dres, o, p, p, lng, lnb, ws, wst, bmap, wc, kv, wout, pg)


def _pool_causal_levels(za, zb, zc, zd, tm):
    n = tm + HALO
    zb[pl.ds(8, n - 8), :] = za[pl.ds(8, n - 8), :] + za[pl.ds(7, n - 8), :]
    zc[pl.ds(16, n - 16), :] = zb[pl.ds(16, n - 16), :] + zb[pl.ds(14, n - 16), :]
    zd[pl.ds(24, n - 24), :] = zc[pl.ds(24, n - 24), :] + zc[pl.ds(20, n - 24), :]


def _pool_causal(za, zb, zc, zd, tm):
    _pool_causal_levels(za, zb, zc, zd, tm)
    s16 = zd[pl.ds(HALO, tm), :] + zd[pl.ds(HALO - 8, tm), :]
    return zb[pl.ds(HALO, tm), :], zc[pl.ds(HALO, tm), :], zd[pl.ds(HALO, tm), :], s16


def _pool_anticausal_levels(ea, eb, ec, ed, tm):
    n = tm + HALO
    eb[pl.ds(0, n - 8), :] = ea[pl.ds(0, n - 8), :] + ea[pl.ds(1, n - 8), :]
    ec[pl.ds(0, n - 16), :] = eb[pl.ds(0, n - 16), :] + eb[pl.ds(2, n - 16), :]
    ed[pl.ds(0, n - 24), :] = ec[pl.ds(0, n - 24), :] + ec[pl.ds(4, n - 24), :]


def _pool_weights(t0, masks, rows):
    del masks
    tf = (t0 + lax.broadcasted_iota(jnp.int32, (rows, 1), 0) + 1).astype(F32)
    inv = [jnp.broadcast_to(1.0 / jnp.minimum(tf, float(win)), (rows, LANE)) for win in POOL_WINDOWS]
    return _by_group(lambda g, j: inv[g])


_HALVES = (slice(0, BW // 2), slice(BW // 2, BW))


def _mix4(masks, parts):
    del masks
    return _by_group(lambda g, j: parts[g][:, _tile(j)])


def _odd_fwd(x1, tgt, p, wbd, cscale, dww, dwb, lng, lnb, wpw, pwb, kv, wout, pg):
    t = x1.shape[0]
    tm = min(TM_FWD_ODD, t)
    nt = t // tm

    def body(x_ref, tgt_ref, p_ref, ph_ref, wbd_ref, cs_ref, dww_ref, dwb_ref, lng_ref, lnb_ref, wpw_ref, pwb_ref,
             kv_ref, wout_ref, pg_ref, o_ref, dres_ref, loss_ref, conv_ref, ybuf, za, zb, zc, zd, gbuf, lacc, gsh):
        i = pl.program_id(0)
        masks = _group_masks()

        @pl.when(i == 0)
        def _():
            lacc[...] = jnp.zeros_like(lacc)

        z = p_ref[:, O_ZC:O_ZC + BW]
        za[0:HALO] = jnp.where(i > 0, ph_ref[:, O_ZC:O_ZC + BW], 0.0)
        za[HALO:HALO + tm] = z
        pooled = _mix4(masks, _pool_causal(za, zb, zc, zd, tm)) * _pool_weights(i * tm, masks, tm) - z
        pooled_b = pooled.astype(BF16)
        for hs in _HALVES:
            gate, _ = _silu_parts(p_ref[:, O_GATE + hs.start:O_GATE + hs.stop])
            ybuf[:, hs] = (_dot(pooled_b[:, hs], wbd_ref[hs, hs]) * cs_ref[:, hs] * gate).astype(BF16)

        gbuf[0:HALO] = jnp.where(i > 0, ph_ref[:, O_GA:O_GA + BW] * _sigmoid(ph_ref[:, O_GB:O_GB + BW]), 0.0)
        gbuf[HALO:HALO + tm] = p_ref[:, O_GA:O_GA + BW] * _sigmoid(p_ref[:, O_GB:O_GB + BW])
        _shift_copies(gbuf, gsh)
        def conv_rows(r0, carry):
            conv_ref[pl.ds(r0, 32), :] = _tap_sum(gbuf, gsh, dww_ref, r0, CONF_K, True) + dwb_ref[...]
            return carry

        _loop_rows(tm, 32, conv_rows)
        zh, _ = _ln_stats(conv_ref[...])
        zn = zh * lng_ref[...] + lnb_ref[...]
        yd = _dot((zn * _sigmoid(zn)).astype(BF16), wpw_ref[...]) + pwb_ref[...]
        gate, _ = _silu_parts(p_ref[:, O_GATE + BW:O_GATE + 2 * BW])
        ybuf[:, BW:2 * BW] = (yd * gate).astype(BF16)

        for h in range(NH):
            qs = slice(O_Q + h * HD, O_Q + (h + 1) * HD)
            _, yx = _attn_head(p_ref[:, qs].astype(BF16), kv_ref[:, h * HD:(h + 1) * HD],
                               kv_ref[:, XA + h * HD:XA + (h + 1) * HD])
            gs = slice(O_GATE + 2 * BW + h * HD, O_GATE + 2 * BW + (h + 1) * HD)
            gate, _ = _silu_parts(p_ref[:, gs])
            ybuf[:, 2 * BW + h * HD:2 * BW + (h + 1) * HD] = (yx * gate).astype(BF16)

        o = _dot(ybuf[...], wout_ref[...])
        o_ref[...] = o
        err = _rms_residual(x_ref[...], o, pg_ref[...]) - tgt_ref[...]
        lacc[...] += _colsum(err * err)
        dres_ref[...] = err * (1.0 / D)

        @pl.when(i == nt - 1)
        def _():
            loss_ref[...] = jnp.full((1, HD), jnp.sum(lacc[...]) * (0.5 / D), F32)

    tile = lambda n: pl.BlockSpec((tm, n), lambda i: (i, 0))
    ext = pltpu.VMEM((tm + HALO, BW), F32)
    return pl.pallas_call(
        body, name="odd_fwd", grid=(nt,),
        in_specs=[tile(D), tile(D), tile(ODD_IN), _halo_spec(ODD_IN, nt, False, tm)] + [_whole()] * 11,
        out_specs=[tile(D), tile(D), _full((1, HD)), tile(BW)],
        out_shape=[S((t, D), F32), S((t, D), F32), S((1, HD), F32), S((t, BW), F32)],
        scratch_shapes=[pltpu.VMEM((tm, MIX), BF16), ext, ext, ext, ext, ext,
                        pltpu.VMEM((1, D), F32), pltpu.VMEM((7, tm + HALO, BW), F32)],
        compiler_params=_params(("arbitrary",)),
    )(x1, tgt, p, p, wbd, cscale, dww, dwb, lng, lnb, wpw, pwb, kv, wout, pg)


def _odd_bwd(dres, o, p, conv, wbd, cscale, dww, dwb, lng, lnb, wpw, pwb, kv, wout, pg):
    t = dres.shape[0]
    tm = min(TM_BWD_ODD, t)
    nt = t // tm

    def body(dres_ref, o_ref, p_ref, ph_ref, conv_ref, wbd_ref, cs_ref, dww_ref, dwb_ref, lng_ref, lnb_ref, wpw_ref,
             pwb_ref, kv_ref, wout_ref, pg_ref, dp_ref, y_ref, do_ref, dpg_ref, dwbd_ref, dcs_ref, ddww_ref, ddwb_ref,
             dlng_ref, dlnb_ref, dwpw_ref, dpwb_ref, dkv_ref,
             dy, za, zb, zc, zd, carry_e, carry_d, wacc, shifted, t1, b1, b2):
        hbuf = zb
        i = pl.program_id(0)
        ti = nt - 1 - i
        masks = _group_masks()

        @pl.when(i == 0)
        def _():
            for ref in (dpg_ref, dwbd_ref, dcs_ref, ddwb_ref, dlng_ref, dlnb_ref, dwpw_ref, dpwb_ref, dkv_ref, wacc):
                ref[...] = jnp.zeros_like(ref)

        def post_norm_rows(r0, acc):
            sl = pl.ds(r0, RB)
            ov, dv = o_ref[sl, :], dres_ref[sl, :]
            r = lax.rsqrt(_rowmean(ov * ov) + EPS)
            oh = ov * r
            doh = dv * pg_ref[...]
            do_ref[sl, :] = (r * (doh - oh * _rowmean(doh * oh))).astype(BF16)
            return acc + dv * oh

        dpg_ref[...] += _colsum(_loop_rows(tm, RB, post_norm_rows, jnp.zeros((RB, D), F32)))
        dy[...] = _dot_nt(do_ref[...], wout_ref[...])

        za[0:HALO] = jnp.where(ti > 0, ph_ref[:, O_ZC:O_ZC + BW], 0.0)
        za[HALO:HALO + tm] = p_ref[:, O_ZC:O_ZC + BW]
        _pool_causal_levels(za, zb, zc, zd, tm)

        def pooled_rows(r0, carry):
            sl = pl.ds(r0, RB)
            at = lambda ref, back=0: ref[pl.ds(HALO + r0 - back, RB), :]
            inv = _pool_weights(ti * tm + r0, masks, RB)
            sums = (at(zb), at(zc), at(zd), at(zd) + at(zd, 8))
            b1[sl, :] = (_mix4(masks, sums) * inv - p_ref[sl, O_ZC:O_ZC + BW]).astype(BF16)
            return carry

        _loop_rows(tm, RB, pooled_rows)
        for hs in _HALVES:
            t1[:, hs] = _dot(b1[:, hs], wbd_ref[hs, hs])

        def pool_gate_rows(r0, acc):
            sl = pl.ds(r0, RB)
            pm = t1[sl, :]
            gate, dgate = _silu_parts(p_ref[sl, O_GATE:O_GATE + BW])
            dyc = dy[sl, 0:BW]
            yc = pm * cs_ref[...]
            y_ref[sl, 0:BW] = (yc * gate).astype(BF16)
            dp_ref[sl, O_GATE:O_GATE + BW] = (dyc * yc * dgate).astype(BF16)
            dyc = dyc * gate
            b2[sl, :] = (dyc * cs_ref[...]).astype(BF16)
            return acc + dyc * pm

        dcs_ref[...] += _colsum(_loop_rows(tm, RB, pool_gate_rows, jnp.zeros((RB, BW), F32)))
        for hs in _HALVES:
            dwbd_ref[hs, hs] += _dot_tn(b1[:, hs], b2[:, hs])
            t1[:, hs] = _dot_nt(b2[:, hs], wbd_ref[hs, hs])

        def weighted_rows(r0, carry):
            sl = pl.ds(r0, RB)
            za[sl, :] = t1[sl, :] * _pool_weights(ti * tm + r0, masks, RB)
            return carry

        _loop_rows(tm, RB, weighted_rows)
        za[tm:tm + HALO] = jnp.where(i > 0, carry_e[...], 0.0)
        carry_e[...] = za[0:HALO]
        _pool_anticausal_levels(za, zb, zc, zd, tm)

        def pool_back_rows(r0, carry):
            sl = pl.ds(r0, RB)
            ahead = lambda ref, fwd=0: ref[pl.ds(r0 + fwd, RB), :]
            sums = (ahead(zb), ahead(zc), ahead(zd), ahead(zd) + ahead(zd, 8))
            dp_ref[sl, O_ZC:O_ZC + BW] = (_mix4(masks, sums) - t1[sl, :]).astype(BF16)
            return carry

        _loop_rows(tm, RB, pool_back_rows)

        def swish_rows(r0, carry):
            sl = pl.ds(r0, RB)
            zh, _ = _ln_stats(conv_ref[sl, :])
            zn = zh * lng_ref[...] + lnb_ref[...]
            b1[sl, :] = (zn * _sigmoid(zn)).astype(BF16)
            return carry

        _loop_rows(tm, RB, swish_rows)
        t1[...] = _dot(b1[...], wpw_ref[...])

        def conf_gate_rows(r0, acc):
            sl = pl.ds(r0, RB)
            yd = t1[sl, :] + pwb_ref[...]
            gate, dgate = _silu_parts(p_ref[sl, O_GATE + BW:O_GATE + 2 * BW])
            dyc = dy[sl, BW:2 * BW]
            y_ref[sl, BW:2 * BW] = (yd * gate).astype(BF16)
            dp_ref[sl, O_GATE + BW:O_GATE + 2 * BW] = (dyc * yd * dgate).astype(BF16)
            dyd = dyc * gate
            b2[sl, :] = dyd.astype(BF16)
            return acc + dyd

        dpwb_ref[...] += _colsum(_loop_rows(tm, RB, conf_gate_rows, jnp.zeros((RB, BW), F32)))
        dwpw_ref[...] += _dot_tn(b1[...], b2[...])
        t1[...] = _dot_nt(b2[...], wpw_ref[...])

        def norm_back_rows(r0, accs):
            sl = pl.ds(r0, RB)
            zh, rs = _ln_stats(conv_ref[sl, :])
            _, dsilu = _silu_parts(zh * lng_ref[...] + lnb_ref[...])
            dzn = t1[sl, :] * dsilu
            dzd = _ln_bwd(dzn, zh, rs, lng_ref[...])
            hbuf[sl, :] = dzd
            return accs[0] + dzn * zh, accs[1] + dzn, accs[2] + dzd

        zero = jnp.zeros((RB, BW), F32)
        acc_g, acc_b, acc_d = _loop_rows(tm, RB, norm_back_rows, (zero, zero, zero))
        dlng_ref[...] += _colsum(acc_g)
        dlnb_ref[...] += _colsum(acc_b)
        ddwb_ref[...] += _colsum(acc_d)
        hbuf[tm:tm + HALO] = jnp.where(i > 0, carry_d[...], 0.0)
        carry_d[...] = hbuf[0:HALO]
        _shift_copies(hbuf, shifted)

        def conv_back_rows(r0, carry):
            sl = pl.ds(r0, 32)
            sgb = _sigmoid(p_ref[sl, O_GB:O_GB + BW])
            ga = p_ref[sl, O_GA:O_GA + BW]
            zg = ga * sgb
            dzg = None
            for k in range(CONF_K):
                ahead = _rows_at(hbuf, shifted, r0, CONF_K - 1 - k)
                term = dww_ref[k:k + 1, :] * ahead
                dzg = term if dzg is None else dzg + term
                wacc[k * 8:(k + 1) * 8, :] += _fold8(zg * ahead)
            dp_ref[sl, O_GA:O_GA + BW] = (dzg * sgb).astype(BF16)
            dp_ref[sl, O_GB:O_GB + BW] = (dzg * ga * sgb * (1.0 - sgb)).astype(BF16)
            return carry

        _loop_rows(tm, 32, conv_back_rows, unrolled=False)

        for h in range(NH):
            qs = slice(O_Q + h * HD, O_Q + (h + 1) * HD)
            ks = slice(h * HD, (h + 1) * HD)
            vs = slice(XA + h * HD, XA + (h + 1) * HD)
            gs = slice(O_GATE + 2 * BW + h * HD, O_GATE + 2 * BW + (h + 1) * HD)
            ys = slice(2 * BW + h * HD, 2 * BW + (h + 1) * HD)
            q_b = p_ref[:, qs].astype(BF16)
            prob, yx = _attn_head(q_b, kv_ref[:, ks], kv_ref[:, vs])
            gate, dgate = _silu_parts(p_ref[:, gs])
            dyc = dy[:, ys]
            y_ref[:, ys] = (yx * gate).astype(BF16)
            dp_ref[:, gs] = (dyc * yx * dgate).astype(BF16)
            dyx_b = (dyc * gate).astype(BF16)
            dprob = _dot_nt(dyx_b, kv_ref[:, vs])
            dkv_ref[:, vs] += _dot_tn(prob.astype(BF16), dyx_b)
            ds_b = (prob * (dprob - jnp.sum(dprob * prob, axis=-1, keepdims=True)) * (HD ** -0.5)).astype(BF16)
            dp_ref[:, qs] = _dot(ds_b, kv_ref[:, ks]).astype(BF16)
            dkv_ref[:, ks] += _dot_tn(ds_b, q_b)

        @pl.when(i == nt - 1)
        def _():
            for k in range(CONF_K):
                ddww_ref[k:k + 1, :] = _colsum(wacc[k * 8:(k + 1) * 8, :])
            ddww_ref[CONF_K:CONF_K + 1, :] = jnp.zeros((1, BW), F32)

    rtile = lambda n: pl.BlockSpec((tm, n), lambda i: (nt - 1 - i, 0))
    outs = [S((t, ODD_IN), BF16), S((t, MIX), BF16), S((t, D), BF16), S((1, D), F32), S((BW, BW), F32),
            S((1, BW), F32), S((CONF_K + 1, BW), F32), S((1, BW), F32), S((1, BW), F32), S((1, BW), F32),
            S((BW, BW), F32), S((1, BW), F32), S((N_MEM, 2 * XA), F32)]
    ext = pltpu.VMEM((tm + HALO, BW), F32)
    return pl.pallas_call(
        body, name="odd_bwd", grid=(nt,),
        in_specs=[rtile(D), rtile(D), rtile(ODD_IN), _halo_spec(ODD_IN, nt, True, tm), rtile(BW)] + [_whole()] * 11,
        out_specs=[rtile(ODD_IN), rtile(MIX), rtile(D)] + [_full(s.shape) for s in outs[3:]],
        out_shape=outs,
        scratch_shapes=[pltpu.VMEM((tm, MIX), F32), ext, ext, ext, ext,
                        pltpu.VMEM((HALO, BW), F32), pltpu.VMEM((HALO, BW), F32), pltpu.VMEM((CONF_K * 8, BW), F32),
                        pltpu.VMEM((7, tm + HALO, BW), F32),
                        pltpu.VMEM((tm, BW), F32), pltpu.VMEM((tm, BW), BF16), pltpu.VMEM((tm, BW), BF16)],
        compiler_params=pltpu.CompilerParams(dimension_semantics=("arbitrary",),
                                             vmem_limit_bytes=VMEM_LIMIT_ODD_BWD_V7X),
    )(dres, o, p, p, conv, wbd, cscale, dww, dwb, lng, lnb, wpw, pwb, kv, wout, pg)


def _pick_rows(n):
    for rows in (640, 2432, 1024, 768):
        if n % rows == 0:
            return rows
    return n


def _pad_rows(a, rows):
    return jnp.pad(a, ((0, rows - a.shape[0]), (0, 0)))


def _step(x, mem, tgt, ex):
    t = x.shape[0]
    tm = min(512, t)
    w, deps = ex.first()
    causal = jnp.tril(jnp.ones((CHUNK, CHUNK), bool))
    ws = jnp.where(causal[None], w["even_a_ws"], 0.0).astype(BF16)
    wst = jnp.transpose(ws, (0, 2, 1))
    bmap = jnp.repeat(w["even_a_bs"].T, GRP, axis=1)
    wc = _pad_rows(w["even_b_conv"], 8)
    wbd = jax.scipy.linalg.block_diag(*[w["odd_c_wgrp"][g] for g in range(NH)]).astype(BF16)
    dww = _pad_rows(w["odd_d_dw_w"], CONF_K + 1)
    tk = min(1024, t)
    zeros = jnp.zeros_like(mem)

    p_e, h_e = _rms_matmul(x, w["even_pre_g"], w["even_w_in"], tm=tm, name="in_even", transposed=True, deps=deps)
    w.update(ex.even_rest(h_e))
    kv_e, memn_e = _rms_matmul(mem, w["even_mem_g"], w["even_w_kv"], tm=N_MEM, name="kv_even", out_dtype=BF16)
    even_args = (w["even_a_ln_g"], w["even_a_ln_b"], ws)
    o_e, x1 = _even_fwd(x, p_e, *even_args, bmap, wc, kv_e, w["even_w_out"], w["even_post_g"])
    w.update(ex.odd(o_e))
    kv_o, memn_o = _rms_matmul(mem, w["odd_mem_g"], w["odd_w_kv"], tm=N_MEM, name="kv_odd", out_dtype=BF16)
    p_o, h_o = _rms_matmul(x1, w["odd_pre_g"], w["odd_w_in"], tm=tm, name="in_odd", transposed=True)
    odd_args = (wbd, w["odd_c_scale"], dww, w["odd_d_dw_b"], w["odd_d_ln_g"], w["odd_d_ln_b"], w["odd_d_pw_w"],
                w["odd_d_pw_b"], kv_o, w["odd_w_out"], w["odd_post_g"])
    o_o, dres, loss, conv_o = _odd_fwd(x1, tgt, p_o, *odd_args)

    g = {}
    (dp_o, y_o, do_o, post_g_o, dwbd, g["odd_c_scale"], ddww, g["odd_d_dw_b"], g["odd_d_ln_g"], g["odd_d_ln_b"],
     dwpw, g["odd_d_pw_b"], dkv_o) = _odd_bwd(dres, o_o, p_o, conv_o, *odd_args)
    g["odd_post_g"] = post_g_o
    g["odd_d_dw_w"] = ddww[:CONF_K]
    dkv_o = dkv_o.astype(BF16)
    deps = ex.send("odd_rest", {
        "odd_w_out": _tn_matmul(y_o, do_o, tmc=MIX, tk=tk, out_dtype=BF16, name="dw_out_odd"),
        "odd_w_kv": _tn_matmul(memn_o, dkv_o, tmc=D, tk=N_MEM, out_dtype=BF16, name="dw_kv_odd"),
        "odd_d_pw_w": dwpw.astype(BF16), "loss": loss,
        "odd_c_wgrp": jnp.concatenate([dwbd[i * GRP:(i + 1) * GRP, i * GRP:(i + 1) * GRP] for i in range(NH)])})
    deps = ex.send("odd_in", {"odd_w_in": _tn_matmul(dp_o, h_o, tmc=_pick_rows(ODD_IN), tk=tk, out_dtype=BF16,
                                                     name="dw_in_odd", deps=deps)})
    dx1, g["odd_pre_g"] = _nt_matmul_rms_bwd(dp_o, w["odd_w_in"], x1, w["odd_pre_g"], dres, tm=tm,
                                             name="dx_odd", transposed=True, deps=deps)
    _, g["odd_mem_g"] = _nt_matmul_rms_bwd(dkv_o, w["odd_w_kv"], mem, w["odd_mem_g"], zeros, tm=N_MEM,
                                           name="dmem_odd")

    (dp_e, y_e, do_e, post_g_e, dws, dbs, ln_g_e, ln_b_e, dwc, dkv_e) = _even_bwd(
        dx1, o_e, p_e, *even_args, wst, bmap, wc, kv_e, w["even_w_out"], w["even_post_g"])
    g["even_b_conv"] = dwc[:3]
    dkv_e = dkv_e.astype(BF16)
    deps = ex.send("even_rest", {
        "even_w_out": _tn_matmul(y_e, do_e, tmc=MIX, tk=tk, out_dtype=BF16, name="dw_out_even"),
        "even_w_kv": _tn_matmul(memn_e, dkv_e, tmc=D, tk=N_MEM, out_dtype=BF16, name="dw_kv_even"),
        "even_a_ln_g": ln_g_e, "even_a_ln_b": ln_b_e,
        "even_a_ws": dws.reshape(NH * CHUNK, CHUNK), "even_a_bs": dbs[:, ::HD].T})
    g["even_w_in"] = _tn_matmul(dp_e, h_e, tmc=_pick_rows(EVEN_IN), tk=tk, out_dtype=BF16, name="dw_in_even",
                                deps=deps)
    deps = ex.send("even_in", g)
    grad_x, pre_g_e = _nt_matmul_rms_bwd(dp_e, w["even_w_in"], x, w["even_pre_g"], dx1, tm=tm,
                                         name="dx_even", transposed=True, deps=deps)
    _, mem_g_e = _nt_matmul_rms_bwd(dkv_e, w["even_w_kv"], mem, w["even_mem_g"], zeros, tm=N_MEM, name="dmem_even",
                                    deps=(grad_x,))
    deps = ex.send("even_gains", {"even_pre_g": pre_g_e, "even_mem_g": mem_g_e, "even_post_g": post_g_e})
    return grad_x, deps


def _place():
    return lax.axis_index("x"), lax.axis_index("y"), lax.axis_index("c")


def _index(px, py, pc):
    return 4 * px + 2 * py + pc


_COPIES = N_DEV - 1


def _all_gather(arrs, name):
    n = len(arrs)

    def body(*refs):
        ins, outs = refs[:n], refs[n:2 * n]
        send_sems, recv_sems, local_sems = refs[2 * n:]
        x, y, c = _place()
        me, sibling = (x, y, c), (x, y, 1 - c)
        chips = [(1 - x, y), (x, 1 - y), (1 - x, 1 - y)]

        def copy(a, k, block, to, src=None):
            dst = outs[a].at[_index(*block)]
            return pltpu.make_async_remote_copy(
                src_ref=dst if src is None else src, dst_ref=dst, send_sem=send_sems.at[a * _COPIES + k],
                recv_sem=recv_sems.at[a * _COPIES + k], device_id=to, device_id_type=MESH)

        mine = [pltpu.make_async_copy(ins[a], outs[a].at[_index(*me)], local_sems.at[a]) for a in range(n)]
        first = []
        for a in range(n):
            mine[a].start()
            first.append(copy(a, 0, me, sibling, src=ins[a]))
            first += [copy(a, 1 + j, me, (*chip, c), src=ins[a]) for j, chip in enumerate(chips)]
        for cp in first:
            cp.start()
        passed = []
        for j, chip in enumerate(chips):
            for a in range(n):
                copy(a, 1 + j, (*chip, c), me).wait_recv()
                passed.append(copy(a, 4 + j, (*chip, c), sibling))
                passed[-1].start()
        for a in range(n):
            copy(a, 0, sibling, me).wait_recv()
            for j, chip in enumerate(chips):
                copy(a, 4 + j, (*chip, 1 - c), me).wait_recv()
        for cp in first + passed:
            cp.wait_send()
        for cp in mine:
            cp.wait()

    return pl.pallas_call(
        body, name=name, in_specs=[_ANY] * n, out_specs=[_ANY] * n,
        out_shape=[S((N_DEV,) + a.shape, a.dtype) for a in arrs],
        scratch_shapes=[pltpu.SemaphoreType.DMA((n * _COPIES,)), pltpu.SemaphoreType.DMA((n * _COPIES,)),
                        pltpu.SemaphoreType.DMA((n,))],
    )(*arrs)


_HBM = pl.BlockSpec(memory_space=pltpu.HBM)
_SEM = pl.BlockSpec(memory_space=pltpu.SEMAPHORE)
_EFFECT = pltpu.SideEffectType.DATAFLOW_SIDE_EFFECTING


_ALL_FLIPS = [(k >> 2 & 1, k >> 1 & 1, k & 1) for k in range(1, N_DEV)]
_CHIP_FLIPS = [(1, 0, 0), (0, 1, 0), (1, 1, 0)]
_FLIPS = {"gather": _ALL_FLIPS, "scatter": _ALL_FLIPS, "gather_chips": [(0, 0, 1)] + _CHIP_FLIPS,
          "scatter_chips": _CHIP_FLIPS}


def _landing_shape(kind, a):
    return (N_DEV,) + a.shape if kind.startswith("gather") else a.shape


def _exchange_copies(kinds, srcs, lands, send_sems, recv_sems, local_sems, arriving):
    x, y, c = _place()
    mine = _index(x, y, c)
    remote, local = [], []
    for a, kind in enumerate(kinds):
        by_chip = kind == "scatter_chips"
        here = 2 * x + y if by_chip else mine
        own = srcs[a] if kind.startswith("gather") else srcs[a].at[here]
        local.append(pltpu.make_async_copy(own, lands[a].at[here], local_sems.at[a]))
        for k, (fx, fy, fc) in enumerate(_FLIPS[kind]):
            peer = (1 - x if fx else x, 1 - y if fy else y, 1 - c if fc else c)
            there = 2 * peer[0] + peer[1] if by_chip else _index(*peer)
            remote.append(pltpu.make_async_remote_copy(
                src_ref=srcs[a] if kind.startswith("gather") else srcs[a].at[there],
                dst_ref=lands[a].at[there if arriving else here],
                send_sem=send_sems.at[a * _COPIES + k], recv_sem=recv_sems.at[a * _COPIES + k],
                device_id=peer, device_id_type=MESH))
    return remote, local


def _exchange_start(items, name, deps=()):
    kinds = [kind for kind, _ in items]
    srcs = [a for _, a in items]
    n = len(items)
    lands = [lax.empty(_landing_shape(kind, a), a.dtype) for kind, a in items]

    def body(*refs):
        send_sems, recv_sems, local_sems = refs[2 * n + len(deps):2 * n + len(deps) + 3]
        remote, local = _exchange_copies(kinds, refs[:n], refs[n:2 * n], send_sems, recv_sems, local_sems, False)
        for cp in local + remote:
            cp.start()
        refs[-1][...] = jnp.zeros_like(refs[-1])

    held = [pltpu.HBM(a.shape, a.dtype) for a in srcs + lands]
    res = pl.pallas_call(
        body, name=name,
        out_shape=(pltpu.SemaphoreType.DMA((n * _COPIES,)), pltpu.SemaphoreType.DMA((n * _COPIES,)),
                   pltpu.SemaphoreType.DMA((n,)), *held, S((8, 128), F32)),
        in_specs=[_HBM] * (2 * n) + [_ANY] * len(deps),
        out_specs=(_SEM, _SEM, _SEM, *[_HBM] * (2 * n), _whole()),
        input_output_aliases={i: 3 + i for i in range(2 * n)},
        compiler_params=pltpu.CompilerParams(has_side_effects=_EFFECT),
    )(*[pltpu.with_memory_space_constraint(a, pltpu.HBM) for a in srcs + lands], *deps)
    return (kinds, res[:3], res[3:3 + 2 * n]), res[-1]


def _exchange_wait(handle, after, name):
    kinds, sems, held = handle
    n = len(kinds)

    def body(*refs):
        send_sems, recv_sems, local_sems = refs[2 * n:2 * n + 3]
        remote, local = _exchange_copies(kinds, refs[:n], refs[n:2 * n], send_sems, recv_sems, local_sems, True)
        for cp in remote:
            cp.wait_send()
            cp.wait_recv()
        for cp in local:
            cp.wait()

    res = pl.pallas_call(
        body, name=name, out_shape=[pltpu.HBM(a.shape, a.dtype) for a in held],
        in_specs=[_HBM] * (2 * n) + [_SEM] * 3 + [_ANY] * len(after), out_specs=[_HBM] * (2 * n),
        input_output_aliases={i: i for i in range(2 * n)},
        compiler_params=pltpu.CompilerParams(has_side_effects=_EFFECT),
    )(*held, *sems, *after)
    return res[n:]


_CHIPS = [(0, 0), (0, 1), (1, 0), (1, 1)]
_N_CHIPS = len(_CHIPS)


def _sibling_forward(lands, name):
    n = len(lands)

    def body(*refs):
        ins, outs = refs[:n], refs[n:2 * n]
        send_sems, recv_sems = refs[2 * n:]
        x, y, c = _place()
        sent, arriving = [], []
        for a in range(n):
            for j, (fx, fy, _) in enumerate(_CHIP_FLIPS):
                chip = (1 - x if fx else x, 1 - y if fy else y)
                sems = dict(send_sem=send_sems.at[a * 3 + j], recv_sem=recv_sems.at[a * 3 + j],
                            device_id=(x, y, 1 - c), device_id_type=MESH)
                mine, theirs = _index(*chip, c), _index(*chip, 1 - c)
                sent.append(pltpu.make_async_remote_copy(src_ref=ins[a].at[mine], dst_ref=outs[a].at[mine], **sems))
                arriving.append(pltpu.make_async_remote_copy(src_ref=ins[a].at[theirs], dst_ref=outs[a].at[theirs],
                                                             **sems))
        for cp in sent:
            cp.start()
        for cp in sent:
            cp.wait_send()
        for cp in arriving:
            cp.wait_recv()

    return pl.pallas_call(
        body, name=name, in_specs=[_ANY] * n, out_specs=[_ANY] * n,
        out_shape=[S(a.shape, a.dtype) for a in lands], input_output_aliases={a: a for a in range(n)},
        scratch_shapes=[pltpu.SemaphoreType.DMA((3 * n,)), pltpu.SemaphoreType.DMA((3 * n,))],
    )(*lands)


def _sibling_swap(arrs, name):
    n = len(arrs)

    def body(*refs):
        ins, outs = refs[:n], refs[n:2 * n]
        send_sems, recv_sems = refs[2 * n:]
        x, y, c = _place()
        copies = []
        for a in range(n):
            for q, chip in enumerate(_CHIPS):
                copies.append(pltpu.make_async_remote_copy(
                    src_ref=ins[a].at[_index(*chip, 1 - c)], dst_ref=outs[a].at[q],
                    send_sem=send_sems.at[a * _N_CHIPS + q], recv_sem=recv_sems.at[a * _N_CHIPS + q],
                    device_id=(x, y, 1 - c), device_id_type=MESH))
        for cp in copies:
            cp.start()
        for cp in copies:
            cp.wait_send()
            cp.wait_recv()

    return pl.pallas_call(
        body, name=name, in_specs=[_ANY] * n, out_specs=[_ANY] * n,
        out_shape=[S((_N_CHIPS,) + a.shape[1:], a.dtype) for a in arrs],
        scratch_shapes=[pltpu.SemaphoreType.DMA((_N_CHIPS * n,)), pltpu.SemaphoreType.DMA((_N_CHIPS * n,))],
    )(*arrs)


def _add_partials(mine, theirs, *, tr, name):
    _, r, c = mine.shape

    def body(mine_ref, theirs_ref, out_ref):
        core = lax.axis_index("c")
        own = jnp.where(core == 0, mine_ref[0].astype(F32), mine_ref[1].astype(F32))
        out_ref[0] = (own + theirs_ref[0].astype(F32)).astype(out_ref.dtype)

    return pl.pallas_call(
        body, name=name, grid=(_N_CHIPS, r // tr),
        in_specs=[pl.BlockSpec((2, tr, c), lambda q, i: (q, i, 0)), pl.BlockSpec((1, tr, c), lambda q, i: (q, i, 0))],
        out_specs=pl.BlockSpec((1, tr, c), lambda q, i: (q, i, 0)),
        out_shape=S((_N_CHIPS, r, c), mine.dtype),
        compiler_params=_params(("arbitrary", "arbitrary")),
    )(mine, theirs)


def _adamw(w, g, m, v):
    m = ADAM_B1 * m + (1.0 - ADAM_B1) * g
    v = ADAM_B2 * v + (1.0 - ADAM_B2) * (g * g)
    m_hat = m / (1.0 - ADAM_B1 ** ADAM_STEP)
    v_hat = v / (1.0 - ADAM_B2 ** ADAM_STEP)
    return -ADAM_LR * (m_hat / (jnp.sqrt(v_hat) + ADAM_EPS) + ADAM_WD * w), m, v


def _sum_devices(ref, rows):
    total = ref[0, rows, :].astype(F32)
    for s in range(1, ref.shape[0]):
        total = total + ref[s, rows, :].astype(F32)
    return total


def _adam_big(recv, w, m, v, *, tr, name):
    r, c = w.shape

    def body(recv_ref, w_ref, m_ref, v_ref, g_ref, d_ref, m2_ref, v2_ref):
        g = _sum_devices(recv_ref, slice(None))
        g_ref[...] = g
        d_ref[...], m2_ref[...], v2_ref[...] = _adamw(w_ref[...], g, m_ref[...], v_ref[...])

    blk = pl.BlockSpec((tr, c), lambda i: (i, 0))
    return pl.pallas_call(
        body, name=name, grid=(r // tr,),
        in_specs=[pl.BlockSpec((recv.shape[0], tr, c), lambda i: (0, i, 0)), blk, blk, blk],
        out_specs=[blk] * 4, out_shape=[S((r, c), F32)] * 4,
        compiler_params=_params(("arbitrary",)),
    )(recv, w, m, v)


_REPLICATED = {"even_pre_g": (0, 0, 1), "even_mem_g": (0, 8, 1), "even_post_g": (0, 16, 1),
               "even_a_ln_g": (1, 0, 1), "even_a_ln_b": (1, 8, 1),
               "even_a_ws": (2, 0, NH * CHUNK), "even_a_bs": (2, NH * CHUNK, NH),
               "odd_c_wgrp": (3, 0, NH * GRP)}
_SHARDED = {"odd_pre_g": (4, 0, 1), "odd_mem_g": (4, 8, 1), "odd_post_g": (4, 16, 1),
            "even_b_conv": (5, 0, 3), "odd_c_scale": (5, 8, 1), "odd_d_dw_w": (5, 16, CONF_K),
            "odd_d_dw_b": (5, 48, 1), "odd_d_ln_g": (5, 56, 1), "odd_d_ln_b": (5, 64, 1), "odd_d_pw_b": (5, 72, 1)}
_SMALL = {**_REPLICATED, **_SHARDED}
_SMALL_ROWS = {0: 24, 1: 16, 2: NH * CHUNK + 8, 3: NH * GRP, 4: 24, 5: 80}


def _adam_small(sources, wmv):
    names = list(_SMALL)
    ns = len(sources)

    def body(*refs):
        src = refs[:ns]
        ins = refs[ns:ns + 3 * len(names)]
        outs = refs[ns + 3 * len(names):]
        outs[-1][...] = _sum_devices(src[-1], slice(0, 1))
        for i, nm in enumerate(names):
            a, row0, rows = _SMALL[nm]
            g = _sum_devices(src[a], slice(row0, row0 + rows))
            w_ref, m_ref, v_ref = ins[3 * i:3 * i + 3]
            g_ref, d_ref, m2_ref, v2_ref = outs[4 * i:4 * i + 4]
            g_ref[...] = g
            d_ref[...], m2_ref[...], v2_ref[...] = _adamw(w_ref[...], g, m_ref[...], v_ref[...])

    flat = [t for nm in names for t in wmv[nm]]
    out_shape = [S(wmv[nm][0].shape, F32) for nm in names for _ in range(4)] + [S((1, HD), F32)]
    res = pl.pallas_call(
        body, name="adam_small", in_specs=[_whole()] * (ns + len(flat)), out_specs=[_whole()] * len(out_shape),
        out_shape=out_shape, compiler_params=_params(),
    )(*sources, *flat)
    return {nm: tuple(res[4 * i:4 * i + 4]) for i, nm in enumerate(names)}, res[-1]


_WEIGHTS = ["even_pre_g", "even_w_in", "even_a_ln_g", "even_a_ln_b", "even_a_ws", "even_a_bs", "even_b_conv",
            "even_mem_g", "even_w_kv", "even_w_out", "even_post_g", "odd_pre_g", "odd_w_in", "odd_c_wgrp",
            "odd_c_scale", "odd_d_dw_w", "odd_d_dw_b", "odd_d_ln_g", "odd_d_ln_b", "odd_d_pw_w", "odd_d_pw_b",
            "odd_mem_g", "odd_w_kv", "odd_w_out", "odd_post_g"]
_TRANSPOSED = ["even_w_in", "odd_w_in"]
_BIG = _TRANSPOSED + ["even_w_kv", "even_w_out", "odd_w_kv", "odd_w_out", "odd_d_pw_w"]
_BIG_TILE_ROWS = {"even_w_in": 400, "odd_w_in": 304, "even_w_kv": 128, "even_w_out": 128, "odd_w_kv": 128,
                  "odd_w_out": 128, "odd_d_pw_w": 96}


def _view2d(a, transposed):
    a = a[0]
    if a.ndim == 1:
        return a[None]
    if transposed:
        return a.T
    return a.reshape(-1, a.shape[-1])


def _rows8(a):
    return _pad_rows(a, -(-a.shape[0] // 8) * 8)


def _pack_rows(parts):
    return jnp.concatenate([_rows8(p) for p in parts], axis=0)


def _unshard_cols(a):
    return jnp.transpose(a, (1, 0, 2)).reshape(a.shape[1], N_DEV * a.shape[2])


def _shard_cols(a):
    return jnp.transpose(a.reshape(a.shape[0], N_DEV, a.shape[1] // N_DEV), (1, 0, 2))


def _rows_of(a):
    return a.reshape(-1, a.shape[-1])


_GROUPS = {"odd_rest": (["odd_w_out", "odd_w_kv", "odd_d_pw_w"], [3], []),
           "odd_in": (["odd_w_in"], [], []),
           "even_rest": (["even_w_out", "even_w_kv"], [1, 2], []),
           "even_in": (["even_w_in"], [], [4, 5]),
           "even_gains": ([], [0], [])}


_TWO_LEVEL = ("even_in",)


class _MeshExchange:
    def __init__(self, shard):
        self.shard = shard
        self.handles = {}

    def first(self):
        shard = self.shard
        packs = [_pack_rows([shard[nm] for nm in _SHARDED if _SHARDED[nm][0] == a]) for a in (4, 5)]
        w_in, p128, p96 = _all_gather([shard["even_w_in"].astype(BF16)] + packs, "gather_first")
        w = {nm: shard[nm] for nm in _REPLICATED}
        w["even_a_ws"] = w["even_a_ws"].reshape(NH, CHUNK, CHUNK)
        w["odd_c_wgrp"] = w["odd_c_wgrp"].reshape(NH, GRP, GRP)
        w["even_w_in"] = _rows_of(w_in)
        full_packs = {4: _unshard_cols(p128), 5: _unshard_cols(p96)}
        for nm, (a, row0, rows) in _SHARDED.items():
            w[nm] = full_packs[a][row0:row0 + rows]
        later = lambda names: [("gather_chips", shard[nm].astype(BF16)) for nm in names]
        self.handles["w_even"], token = _exchange_start(later(["even_w_kv", "even_w_out"]), "gather_even_start",
                                                        deps=(w_in,))
        self.handles["w_odd"], token = _exchange_start(later(["odd_w_in", "odd_w_kv", "odd_w_out", "odd_d_pw_w"]),
                                                       "gather_odd_start", deps=(token,))
        return w, (token,)

    def even_rest(self, after):
        landed = _exchange_wait(self.handles.pop("w_even"), (after,), "gather_even_wait")
        kv, out = _sibling_forward(landed, "forward_even")
        return {"even_w_kv": _rows_of(kv), "even_w_out": _rows_of(out)}

    def odd(self, after):
        landed = _exchange_wait(self.handles.pop("w_odd"), (after,), "gather_odd_wait")
        w_in, kv, out, pw = _sibling_forward(landed, "forward_odd")
        return {"odd_w_in": _rows_of(w_in), "odd_w_kv": _rows_of(kv), "odd_w_out": _rows_of(out),
                "odd_d_pw_w": _rows_of(pw)}

    def send(self, group, g):
        big, replicated, sharded = _GROUPS[group]
        by_owner = [g[nm].reshape(N_DEV, -1, g[nm].shape[-1]) for nm in big]
        if group in _TWO_LEVEL:
            theirs = _sibling_swap(by_owner, "swap_" + group)
            items = [("scatter_chips", _add_partials(a, b, tr=_BIG_TILE_ROWS[nm], name="chip_sum_" + nm))
                     for nm, a, b in zip(big, by_owner, theirs)]
        else:
            items = [("scatter", a) for a in by_owner]
        items += [("gather", _pack_rows([g[nm] for nm in _REPLICATED if _REPLICATED[nm][0] == a]))
                  for a in replicated]
        items += [("scatter", _shard_cols(_pack_rows([g[nm] for nm in _SHARDED if _SHARDED[nm][0] == a])))
                  for a in sharded]
        if group == "odd_rest":
            items.append(("gather", _rows8(g["loss"])))
        self.handles[group], token = _exchange_start(items, "send_" + group + "_start")
        return (token,)

    def receive(self, group, after):
        after = after if isinstance(after, tuple) else (after,)
        return _exchange_wait(self.handles.pop(group), after, "send_" + group + "_wait")


def kernel(x, mem, even_pre_g, even_w_in, even_a_ln_g, even_a_ln_b, even_a_ws, even_a_bs, even_b_conv, even_mem_g, even_w_kv, even_w_out, even_post_g, odd_pre_g, odd_w_in, odd_c_wgrp, odd_c_scale, odd_d_dw_w, odd_d_dw_b, odd_d_ln_g, odd_d_ln_b, odd_d_pw_w, odd_d_pw_b, odd_mem_g, odd_w_kv, odd_w_out, odd_post_g, loss_target, m_even_pre_g, m_even_w_in, m_even_a_ln_g, m_even_a_ln_b, m_even_a_ws, m_even_a_bs, m_even_b_conv, m_even_mem_g, m_even_w_kv, m_even_w_out, m_even_post_g, m_odd_pre_g, m_odd_w_in, m_odd_c_wgrp, m_odd_c_scale, m_odd_d_dw_w, m_odd_d_dw_b, m_odd_d_ln_g, m_odd_d_ln_b, m_odd_d_pw_w, m_odd_d_pw_b, m_odd_mem_g, m_odd_w_kv, m_odd_w_out, m_odd_post_g, v_even_pre_g, v_even_w_in, v_even_a_ln_g, v_even_a_ln_b, v_even_a_ws, v_even_a_bs, v_even_b_conv, v_even_mem_g, v_even_w_kv, v_even_w_out, v_even_post_g, v_odd_pre_g, v_odd_w_in, v_odd_c_wgrp, v_odd_c_scale, v_odd_d_dw_w, v_odd_d_dw_b, v_odd_d_ln_g, v_odd_d_ln_b, v_odd_d_pw_w, v_odd_d_pw_b, v_odd_mem_g, v_odd_w_kv, v_odd_w_out, v_odd_post_g):
    given = dict(locals())
    view = lambda nm, kind: _view2d(given[kind + nm], nm in _TRANSPOSED)
    shard = {nm: view(nm, "") for nm in _WEIGHTS}
    wmv = {nm: (shard[nm], view(nm, "m_"), view(nm, "v_")) for nm in _WEIGHTS}

    ex = _MeshExchange(shard)
    grad_x, last = _step(x[0], mem[0], loss_target[0], ex)

    res = {}

    def update(group, after):
        names = _GROUPS[group][0]
        landed = ex.receive(group, after)
        for nm, recv in zip(names, landed):
            res[nm] = _adam_big(recv, *wmv[nm], tr=_BIG_TILE_ROWS[nm], name="adam_" + nm)
        return landed[len(names):]

    c192, losses = update("odd_rest", last)
    update("odd_in", res["odd_d_pw_w"][0])
    c768, c128 = update("even_rest", res["odd_w_in"][0])
    a128, a96 = update("even_in", res["even_w_kv"][0])
    (c1024,) = update("even_gains", res["even_w_in"][0])
    small, loss = _adam_small([c1024, c768, c128, c192, a128, a96, losses], {nm: wmv[nm] for nm in _SMALL})
    res.update(small)
    total = loss[0, 0]
    back = lambda nm, a: (a.T if nm in _TRANSPOSED else a).reshape(given[nm].shape)
    outs = [[back(nm, res[nm][i]) for nm in _WEIGHTS] for i in range(4)]
    return (total, grad_x[None], *outs[0], *outs[1], *outs[2], *outs[3])
```

```python
import functools

import jax
import jax.numpy as jnp
from jax import lax
from jax.experimental import pallas as pl
from jax.experimental.pallas import tpu as pltpu

F32 = jnp.float32
BF16 = jnp.bfloat16
S = jax.ShapeDtypeStruct
MESH = pl.DeviceIdType.MESH
AXES = ("x", "y", "c")
N_DEV = 8

D = 1024
BW = 768
XA = 512
HD = 128
NH = 4
MIX = 2048
CHUNK = 128
GRP = 192
N_MEM = 256
CONF_K = 31
EPS = 1e-6
HALO = 32
POOL_WINDOWS = (2, 4, 8, 16)
TM_FWD_EVEN = 512
TM_FWD_ODD = 256
TM_BWD_EVEN = 256
TM_BWD_ODD = 256
RB = 16

E_U, E_V, E_BG, E_CG, E_XIN, E_Q, E_GATE = 0, 768, 1536, 2304, 3072, 3840, 4352
EVEN_IN = 6400
O_ZC, O_GA, O_GB, O_Q, O_GATE = 0, 768, 1536, 2304, 2816
ODD_IN = 4864

ADAM_LR, ADAM_B1, ADAM_B2, ADAM_EPS, ADAM_WD, ADAM_STEP = 0.001, 0.9, 0.999, 1e-08, 0.01, 10

VMEM_LIMIT_V7X = 56 * 1024 * 1024
VMEM_LIMIT_ODD_BWD_V7X = 62 * 1024 * 1024


def _params(sem=None):
    return pltpu.CompilerParams(dimension_semantics=sem, vmem_limit_bytes=VMEM_LIMIT_V7X)


def _dot(a, b):
    return jnp.dot(a, b, preferred_element_type=F32)


def _dot_nt(a, b):
    return lax.dot_general(a, b, (((1,), (1,)), ((), ())), preferred_element_type=F32)


def _dot_tn(a, b):
    return lax.dot_general(a, b, (((0,), (0,)), ((), ())), preferred_element_type=F32)


def _sigmoid(z):
    return 1.0 / (1.0 + jnp.exp(-z))


def _rowmean(a):
    return jnp.mean(a, axis=-1, keepdims=True)


def _colsum(a):
    return jnp.sum(a, axis=0, keepdims=True)


def _ln_stats(v):
    mu = _rowmean(v)
    vc = v - mu
    rs = lax.rsqrt(_rowmean(vc * vc) + EPS)
    return vc * rs, rs


def _ln_bwd(dn, vh, rs, g):
    dvh = dn * g
    return rs * (dvh - _rowmean(dvh) - vh * _rowmean(dvh * vh))


def _group_masks():
    col = lax.broadcasted_iota(jnp.int32, (1, BW), 1)
    return [((col >= GRP * h) & (col < GRP * (h + 1))).astype(F32) for h in range(NH)]


def _full(shape):
    nd = len(shape)
    return pl.BlockSpec(shape, lambda *_: (0,) * nd)


def _whole():
    return pl.BlockSpec(memory_space=pltpu.VMEM)


_ANY = pl.BlockSpec(memory_space=pl.ANY)


def _after(body, n_in, deps):
    def ordered(*refs):
        return body(*refs[:n_in], *refs[n_in + len(deps):])
    return ordered


def _rms_matmul(x, g, w, *, tm, name, transposed=False, out_dtype=F32, deps=()):
    t, d = x.shape
    n = w.shape[0] if transposed else w.shape[1]

    def body(x_ref, g_ref, w_ref, p_ref, h_ref):
        xv = x_ref[...]
        r = lax.rsqrt(_rowmean(xv * xv) + EPS)
        h = (xv * r * g_ref[...]).astype(BF16)
        h_ref[...] = h
        p_ref[...] = (_dot_nt(h, w_ref[...]) if transposed else _dot(h, w_ref[...])).astype(out_dtype)

    return pl.pallas_call(
        _after(body, 3, deps), name=name, grid=(t // tm,),
        in_specs=[pl.BlockSpec((tm, d), lambda i: (i, 0)), _whole(), _whole()] + [_ANY] * len(deps),
        out_specs=[pl.BlockSpec((tm, n), lambda i: (i, 0)), pl.BlockSpec((tm, d), lambda i: (i, 0))],
        out_shape=[S((t, n), out_dtype), S((t, d), BF16)],
        compiler_params=_params(("arbitrary",)),
    )(x, g, w, *deps)


def _nt_matmul_rms_bwd(dp, w, x, g, dres, *, tm, name, transposed=False, deps=()):
    t, n = dp.shape
    d = x.shape[1]

    def body(dp_ref, w_ref, x_ref, g_ref, dres_ref, dx_ref, dg_ref):
        @pl.when(pl.program_id(0) == 0)
        def _():
            dg_ref[...] = jnp.zeros_like(dg_ref)

        dh = _dot(dp_ref[...], w_ref[...]) if transposed else _dot_nt(dp_ref[...], w_ref[...])
        xv = x_ref[...]
        r = lax.rsqrt(_rowmean(xv * xv) + EPS)
        xh = xv * r
        dg_ref[...] += _colsum(dh * xh)
        dxh = dh * g_ref[...]
        dx_ref[...] = dres_ref[...] + r * (dxh - xh * _rowmean(dxh * xh))

    return pl.pallas_call(
        _after(body, 5, deps), name=name, grid=(t // tm,),
        in_specs=[pl.BlockSpec((tm, n), lambda i: (i, 0)), _whole(), pl.BlockSpec((tm, d), lambda i: (i, 0)),
                  _whole(), pl.BlockSpec((tm, d), lambda i: (i, 0))] + [_ANY] * len(deps),
        out_specs=[pl.BlockSpec((tm, d), lambda i: (i, 0)), pl.BlockSpec((1, d), lambda i: (0, 0))],
        out_shape=[S((t, d), F32), S((1, d), F32)],
        compiler_params=_params(("arbitrary",)),
    )(dp, w, x, g, dres, *deps)


def _tn_matmul(a, b, *, tmc, tk, out_dtype, name, deps=()):
    t, m = a.shape
    n = b.shape[1]
    nk = t // tk

    def body(a_ref, b_ref, o_ref, acc_ref):
        k = pl.program_id(1)

        @pl.when(k == 0)
        def _():
            acc_ref[...] = jnp.zeros_like(acc_ref)

        acc_ref[...] += _dot_tn(a_ref[...], b_ref[...])

        @pl.when(k == nk - 1)
        def _():
            o_ref[...] = acc_ref[...].astype(out_dtype)

    return pl.pallas_call(
        _after(body, 2, deps), name=name, grid=(m // tmc, nk),
        in_specs=[pl.BlockSpec((tk, tmc), lambda j, k: (k, j)), pl.BlockSpec((tk, n), lambda j, k: (k, 0))]
        + [_ANY] * len(deps),
        out_specs=pl.BlockSpec((tmc, n), lambda j, k: (j, 0)),
        out_shape=S((m, n), out_dtype),
        scratch_shapes=[pltpu.VMEM((tmc, n), F32)],
        compiler_params=_params(("arbitrary", "arbitrary")),
    )(a, b, *deps)


def _silu_parts(gt):
    sg = _sigmoid(gt)
    return gt * sg, sg * (1.0 + gt * (1.0 - sg))


def _attn_head(q_b, k_b, v_b):
    s = _dot_nt(q_b, k_b) * (HD ** -0.5)
    e = jnp.exp(s - jnp.max(s, axis=-1, keepdims=True))
    prob = e / jnp.sum(e, axis=-1, keepdims=True)
    return prob, _dot(prob.astype(BF16), v_b)


def _rms_residual(x, o, g):
    r = lax.rsqrt(_rowmean(o * o) + EPS)
    return x + o * r * g


def _rms_post_bwd(dres, o, g):
    r = lax.rsqrt(_rowmean(o * o) + EPS)
    oh = o * r
    doh = dres * g
    return r * (doh - oh * _rowmean(doh * oh)), _colsum(dres * oh)


LANE = 128
_TILE_GROUPS = [sorted({LANE * j // GRP, (LANE * j + LANE - 1) // GRP}) for j in range(BW // LANE)]


def _tile(j):
    return slice(LANE * j, LANE * (j + 1))


def _low_lanes():
    return lax.broadcasted_iota(jnp.int32, (1, LANE), 1) < GRP - LANE


def _by_group(fn):
    tiles = []
    for j, groups in enumerate(_TILE_GROUPS):
        if len(groups) == 1:
            tiles.append(fn(groups[0], j))
        else:
            tiles.append(jnp.where(_low_lanes(), fn(groups[0], j), fn(groups[1], j)))
    return jnp.concatenate(tiles, axis=1)


def _sgu_chunk(vn_b, ws_ref, bmap_ref):
    return bmap_ref[...] + _by_group(lambda h, j: _dot(ws_ref[h], vn_b[:, _tile(j)]))


def _shift_copies(buf, sh):
    n = buf.shape[0] - 8
    for b in range(1, 8):
        sh[b - 1, pl.ds(0, n), :] = buf[pl.ds(b, n), :]


def _loop_rows(rows, step, fn, carry=0, unrolled=True):
    if unrolled:
        for r0 in range(0, rows, step):
            carry = fn(r0, carry)
        return carry

    def body(j, c):
        return fn(pl.multiple_of(j * step, step), c)
    return lax.fori_loop(0, rows // step, body, carry)


def _rows_at(buf, sh, r0, off):
    b = off % 8
    if b == 0 or sh is None:
        return buf[pl.ds(r0 + off, 32), :]
    return sh[b - 1, pl.ds(r0 + (off - b), 32), :]


def _tap_sum(buf, sh, w_ref, r0, taps, causal):
    acc = None
    for k in range(taps):
        off = HALO - (taps - 1 - k) if causal else taps - 1 - k
        term = w_ref[k:k + 1, :] * _rows_at(buf, sh, r0, off)
        acc = term if acc is None else acc + term
    return acc


def _fold8(a):
    return a[0:8] + a[8:16] + a[16:24] + a[24:32]


def _conv_back(buf, sh, w_ref, acc_ref, z, r0, taps):
    dz = None
    for k in range(taps):
        ahead = _rows_at(buf, sh, r0, taps - 1 - k)
        term = w_ref[k:k + 1, :] * ahead
        dz = term if dz is None else dz + term
        acc_ref[k * 8:(k + 1) * 8, :] += _fold8(z * ahead)
    return dz


def _halo_spec(n, nt, reverse, tm):
    per = tm // HALO
    if reverse:
        return pl.BlockSpec((HALO, n), lambda i: (jnp.maximum((nt - 1 - i) * per - 1, 0), 0))
    return pl.BlockSpec((HALO, n), lambda i: (jnp.maximum(i * per - 1, 0), 0))


def _even_fwd(x, p, lng, lnb, ws, bmap, wc, kv, wout, pg):
    t = x.shape[0]
    tm = min(TM_FWD_EVEN, t)
    nt = t // tm

    def body(x_ref, p_ref, ph_ref, lng_ref, lnb_ref, ws_ref, bmap_ref, wc_ref, kv_ref, wout_ref, pg_ref,
             o_ref, x1_ref, ybuf, cbuf):
        i = pl.program_id(0)
        vh, _ = _ln_stats(p_ref[:, E_V:E_V + BW])
        vn = vh * lng_ref[...] + lnb_ref[...]
        for c in range(tm // CHUNK):
            sl = slice(c * CHUNK, (c + 1) * CHUNK)
            sg = _sgu_chunk(vn[sl].astype(BF16), ws_ref, bmap_ref)
            gate, _ = _silu_parts(p_ref[sl, E_GATE:E_GATE + BW])
            ybuf[sl, 0:BW] = (p_ref[sl, E_U:E_U + BW] * sg * gate).astype(BF16)

        cbuf[0:HALO] = jnp.where(i > 0, ph_ref[:, E_CG:E_CG + BW] * ph_ref[:, E_XIN:E_XIN + BW], 0.0)
        cbuf[HALO:HALO + tm] = p_ref[:, E_CG:E_CG + BW] * p_ref[:, E_XIN:E_XIN + BW]
        for r0 in range(0, tm, 32):
            sl = slice(r0, r0 + 32)
            cv = _tap_sum(cbuf, None, wc_ref, r0, 3, True)
            gate, _ = _silu_parts(p_ref[sl, E_GATE + BW:E_GATE + 2 * BW])
            ybuf[sl, BW:2 * BW] = (p_ref[sl, E_BG:E_BG + BW] * cv * gate).astype(BF16)

        for h in range(NH):
            qs = slice(E_Q + h * HD, E_Q + (h + 1) * HD)
            _, yx = _attn_head(p_ref[:, qs].astype(BF16), kv_ref[:, h * HD:(h + 1) * HD],
                               kv_ref[:, XA + h * HD:XA + (h + 1) * HD])
            gs = slice(E_GATE + 2 * BW + h * HD, E_GATE + 2 * BW + (h + 1) * HD)
            gate, _ = _silu_parts(p_ref[:, gs])
            ybuf[:, 2 * BW + h * HD:2 * BW + (h + 1) * HD] = (yx * gate).astype(BF16)

        o = _dot(ybuf[...], wout_ref[...])
        o_ref[...] = o
        x1_ref[...] = _rms_residual(x_ref[...], o, pg_ref[...])

    tile = lambda n: pl.BlockSpec((tm, n), lambda i: (i, 0))
    return pl.pallas_call(
        body, name="even_fwd", grid=(nt,),
        in_specs=[tile(D), tile(EVEN_IN), _halo_spec(EVEN_IN, nt, False, tm)] + [_whole()] * 8,
        out_specs=[tile(D), tile(D)],
        out_shape=[S((t, D), F32), S((t, D), F32)],
        scratch_shapes=[pltpu.VMEM((tm, MIX), BF16), pltpu.VMEM((tm + HALO, BW), F32)],
        compiler_params=_params(("arbitrary",)),
    )(x, p, p, lng, lnb, ws, bmap, wc, kv, wout, pg)


def _even_bwd(dres, o, p, lng, lnb, ws, wst, bmap, wc, kv, wout, pg):
    t = dres.shape[0]
    tm = min(TM_BWD_EVEN, t)
    nt = t // tm

    def body(dres_ref, o_ref, p_ref, ph_ref, lng_ref, lnb_ref, ws_ref, wst_ref, bmap_ref, wc_ref, kv_ref, wout_ref,
             pg_ref, dp_ref, y_ref, do_ref, dpg_ref, dws_ref, dbs_ref, dlng_ref, dlnb_ref, dwc_ref, dkv_ref,
             dy, cbuf, gbuf, dconv, carry, dvn, dbmap, wacc):
        i = pl.program_id(0)
        ti = nt - 1 - i
        masks = _group_masks()

        @pl.when(i == 0)
        def _():
            for ref in (dpg_ref, dws_ref, dlng_ref, dlnb_ref, dkv_ref, dbmap, wacc):
                ref[...] = jnp.zeros_like(ref)

        do, dpg = _rms_post_bwd(dres_ref[...], o_ref[...], pg_ref[...])
        dpg_ref[...] += dpg
        do_b = do.astype(BF16)
        do_ref[...] = do_b
        dy[...] = _dot_nt(do_b, wout_ref[...])

        vh, rs = _ln_stats(p_ref[:, E_V:E_V + BW])
        vn = vh * lng_ref[...] + lnb_ref[...]
        for c in range(tm // CHUNK):
            sl = slice(c * CHUNK, (c + 1) * CHUNK)
            vn_b = vn[sl].astype(BF16)
            sg = _sgu_chunk(vn_b, ws_ref, bmap_ref)
            u = p_ref[sl, E_U:E_U + BW]
            gate, dgate = _silu_parts(p_ref[sl, E_GATE:E_GATE + BW])
            dyc = dy[sl, 0:BW]
            ya = u * sg
            y_ref[sl, 0:BW] = (ya * gate).astype(BF16)
            dp_ref[sl, E_GATE:E_GATE + BW] = (dyc * ya * dgate).astype(BF16)
            dya = dyc * gate
            dp_ref[sl, E_U:E_U + BW] = (dya * sg).astype(BF16)
            dsg = dya * u
            dbmap[...] += dsg
            dsg_b = dsg.astype(BF16)
            for h in range(NH):
                total = None
                for j, heads in enumerate(_TILE_GROUPS):
                    if h in heads:
                        d_t = dsg_b[:, _tile(j)]
                        if len(heads) == 2:
                            d_t = jnp.where(_low_lanes() == (h == heads[0]), d_t, jnp.zeros_like(d_t))
                        part = _dot_nt(d_t, vn_b[:, _tile(j)])
                        total = part if total is None else total + part
                dws_ref[h] += total
            dvn[sl, :] = _by_group(lambda h, j: _dot(wst_ref[h], dsg_b[:, _tile(j)]))
        dn = dvn[...]
        dlng_ref[...] += _colsum(dn * vh)
        dlnb_ref[...] += _colsum(dn)
        dp_ref[:, E_V:E_V + BW] = _ln_bwd(dn, vh, rs, lng_ref[...]).astype(BF16)

        cbuf[0:HALO] = jnp.where(ti > 0, ph_ref[:, E_CG:E_CG + BW] * ph_ref[:, E_XIN:E_XIN + BW], 0.0)
        cbuf[HALO:HALO + tm] = p_ref[:, E_CG:E_CG + BW] * p_ref[:, E_XIN:E_XIN + BW]
        for r0 in range(0, tm, 32):
            sl = slice(r0, r0 + 32)
            cv = _tap_sum(cbuf, None, wc_ref, r0, 3, True)
            gate, dgate = _silu_parts(p_ref[sl, E_GATE + BW:E_GATE + 2 * BW])
            bg = p_ref[sl, E_BG:E_BG + BW]
            dyc = dy[sl, BW:2 * BW]
            yb = bg * cv
            y_ref[sl, BW:2 * BW] = (yb * gate).astype(BF16)
            dp_ref[sl, E_GATE + BW:E_GATE + 2 * BW] = (dyc * yb * dgate).astype(BF16)
            dyb = dyc * gate
            dp_ref[sl, E_BG:E_BG + BW] = (dyb * cv).astype(BF16)
            dconv[sl, :] = dyb * bg
        gbuf[0:tm] = dconv[...]
        gbuf[tm:tm + HALO] = jnp.where(i > 0, carry[...], 0.0)
        carry[...] = dconv[0:HALO]
        for r0 in range(0, tm, 32):
            sl = slice(r0, r0 + 32)
            dc = _conv_back(gbuf, None, wc_ref, wacc, cbuf[HALO + r0:HALO + r0 + 32, :], r0, 3)
            dp_ref[sl, E_CG:E_CG + BW] = (dc * p_ref[sl, E_XIN:E_XIN + BW]).astype(BF16)
            dp_ref[sl, E_XIN:E_XIN + BW] = (dc * p_ref[sl, E_CG:E_CG + BW]).astype(BF16)

        for h in range(NH):
            qs = slice(E_Q + h * HD, E_Q + (h + 1) * HD)
            ks = slice(h * HD, (h + 1) * HD)
            vs = slice(XA + h * HD, XA + (h + 1) * HD)
            gs = slice(E_GATE + 2 * BW + h * HD, E_GATE + 2 * BW + (h + 1) * HD)
            ys = slice(2 * BW + h * HD, 2 * BW + (h + 1) * HD)
            q_b = p_ref[:, qs].astype(BF16)
            prob, yx = _attn_head(q_b, kv_ref[:, ks], kv_ref[:, vs])
            gate, dgate = _silu_parts(p_ref[:, gs])
            dyc = dy[:, ys]
            y_ref[:, ys] = (yx * gate).astype(BF16)
            dp_ref[:, gs] = (dyc * yx * dgate).astype(BF16)
            dyx_b = (dyc * gate).astype(BF16)
            dprob = _dot_nt(dyx_b, kv_ref[:, vs])
            dkv_ref[:, vs] += _dot_tn(prob.astype(BF16), dyx_b)
            ds_b = (prob * (dprob - jnp.sum(dprob * prob, axis=-1, keepdims=True)) * (HD ** -0.5)).astype(BF16)
            dp_ref[:, qs] = _dot(ds_b, kv_ref[:, ks]).astype(BF16)
            dkv_ref[:, ks] += _dot_tn(ds_b, q_b)

        @pl.when(i == nt - 1)
        def _():
            for h in range(NH):
                dbs_ref[:, h * HD:(h + 1) * HD] = jnp.broadcast_to(
                    jnp.sum(dbmap[...] * masks[h], axis=-1, keepdims=True), (CHUNK, HD))
            for k in range(3):
                dwc_ref[k:k + 1, :] = _colsum(wacc[k * 8:(k + 1) * 8, :])
            dwc_ref[3:8, :] = jnp.zeros((5, BW), F32)
            causal = (lax.broadcasted_iota(jnp.int32, (CHUNK, CHUNK), 0)
                      >= lax.broadcasted_iota(jnp.int32, (CHUNK, CHUNK), 1))
            for h in range(NH):
                dws_ref[h] = jnp.where(causal, dws_ref[h], 0.0)

    rtile = lambda n: pl.BlockSpec((tm, n), lambda i: (nt - 1 - i, 0))
    outs = [S((t, EVEN_IN), BF16), S((t, MIX), BF16), S((t, D), BF16), S((1, D), F32), S((NH, CHUNK, CHUNK), F32),
            S((CHUNK, NH * HD), F32), S((1, BW), F32), S((1, BW), F32), S((8, BW), F32), S((N_MEM, 2 * XA), F32)]
    return pl.pallas_call(
        body, name="even_bwd", grid=(nt,),
        in_specs=[rtile(D), rtile(D), rtile(EVEN_IN), _halo_spec(EVEN_IN, nt, True, tm)] + [_whole()] * 9,
        out_specs=[rtile(EVEN_IN), rtile(MIX), rtile(D)] + [_full(s.shape) for s in outs[3:]],
        out_shape=outs,
        scratch_shapes=[pltpu.VMEM((tm, MIX), F32), pltpu.VMEM((tm + HALO, BW), F32), pltpu.VMEM((tm + HALO, BW), F32),
                        pltpu.VMEM((tm, BW), F32), pltpu.VMEM((HALO, BW), F32), pltpu.VMEM((tm, BW), F32),
                        pltpu.VMEM((CHUNK, BW), F32), pltpu.VMEM((3 * 8, BW), F32)],
        compiler_params=_params(("arbitrary",)),
    )(dres, o, p, p, lng, lnb, ws, wst, bmap, wc, kv, wout, pg)


def _pool_causal_levels(za, zb, zc, zd, tm):
    n = tm + HALO
    zb[pl.ds(8, n - 8), :] = za[pl.ds(8, n - 8), :] + za[pl.ds(7, n - 8), :]
    zc[pl.ds(16, n - 16), :] = zb[pl.ds(16, n - 16), :] + zb[pl.ds(14, n - 16), :]
    zd[pl.ds(24, n - 24), :] = zc[pl.ds(24, n - 24), :] + zc[pl.ds(20, n - 24), :]


def _pool_causal(za, zb, zc, zd, tm):
    _pool_causal_levels(za, zb, zc, zd, tm)
    s16 = zd[pl.ds(HALO, tm), :] + zd[pl.ds(HALO - 8, tm), :]
    return zb[pl.ds(HALO, tm), :], zc[pl.ds(HALO, tm), :], zd[pl.ds(HALO, tm), :], s16


def _pool_anticausal_levels(ea, eb, ec, ed, tm):
    n = tm + HALO
    eb[pl.ds(0, n - 8), :] = ea[pl.ds(0, n - 8), :] + ea[pl.ds(1, n - 8), :]
    ec[pl.ds(0, n - 16), :] = eb[pl.ds(0, n - 16), :] + eb[pl.ds(2, n - 16), :]
    ed[pl.ds(0, n - 24), :] = ec[pl.ds(0, n - 24), :] + ec[pl.ds(4, n - 24), :]


def _pool_weights(t0, rows):
    tf = (t0 + lax.broadcasted_iota(jnp.int32, (rows, 1), 0) + 1).astype(F32)
    inv = [jnp.broadcast_to(1.0 / jnp.minimum(tf, float(win)), (rows, LANE)) for win in POOL_WINDOWS]
    return _by_group(lambda g, j: inv[g])


_HALVES = (slice(0, BW // 2), slice(BW // 2, BW))


def _mix4(parts):
    return _by_group(lambda g, j: parts[g][:, _tile(j)])


def _odd_fwd(x1, tgt, p, wbd, cscale, dww, dwb, lng, lnb, wpw, pwb, kv, wout, pg):
    t = x1.shape[0]
    tm = min(TM_FWD_ODD, t)
    nt = t // tm

    def body(x_ref, tgt_ref, p_ref, ph_ref, wbd_ref, cs_ref, dww_ref, dwb_ref, lng_ref, lnb_ref, wpw_ref, pwb_ref,
             kv_ref, wout_ref, pg_ref, o_ref, dres_ref, loss_ref, conv_ref, ybuf, za, zb, zc, zd, gbuf, lacc, gsh):
        i = pl.program_id(0)

        @pl.when(i == 0)
        def _():
            lacc[...] = jnp.zeros_like(lacc)

        z = p_ref[:, O_ZC:O_ZC + BW]
        za[0:HALO] = jnp.where(i > 0, ph_ref[:, O_ZC:O_ZC + BW], 0.0)
        za[HALO:HALO + tm] = z
        pooled = _mix4(_pool_causal(za, zb, zc, zd, tm)) * _pool_weights(i * tm, tm) - z
        pooled_b = pooled.astype(BF16)
        for hs in _HALVES:
            gate, _ = _silu_parts(p_ref[:, O_GATE + hs.start:O_GATE + hs.stop])
            ybuf[:, hs] = (_dot(pooled_b[:, hs], wbd_ref[hs, hs]) * cs_ref[:, hs] * gate).astype(BF16)

        gbuf[0:HALO] = jnp.where(i > 0, ph_ref[:, O_GA:O_GA + BW] * _sigmoid(ph_ref[:, O_GB:O_GB + BW]), 0.0)
        gbuf[HALO:HALO + tm] = p_ref[:, O_GA:O_GA + BW] * _sigmoid(p_ref[:, O_GB:O_GB + BW])
        _shift_copies(gbuf, gsh)
        def conv_rows(r0, carry):
            conv_ref[pl.ds(r0, 32), :] = _tap_sum(gbuf, gsh, dww_ref, r0, CONF_K, True) + dwb_ref[...]
            return carry

        _loop_rows(tm, 32, conv_rows)
        zh, _ = _ln_stats(conv_ref[...])
        zn = zh * lng_ref[...] + lnb_ref[...]
        yd = _dot((zn * _sigmoid(zn)).astype(BF16), wpw_ref[...]) + pwb_ref[...]
        gate, _ = _silu_parts(p_ref[:, O_GATE + BW:O_GATE + 2 * BW])
        ybuf[:, BW:2 * BW] = (yd * gate).astype(BF16)

        for h in range(NH):
            qs = slice(O_Q + h * HD, O_Q + (h + 1) * HD)
            _, yx = _attn_head(p_ref[:, qs].astype(BF16), kv_ref[:, h * HD:(h + 1) * HD],
                               kv_ref[:, XA + h * HD:XA + (h + 1) * HD])
            gs = slice(O_GATE + 2 * BW + h * HD, O_GATE + 2 * BW + (h + 1) * HD)
            gate, _ = _silu_parts(p_ref[:, gs])
            ybuf[:, 2 * BW + h * HD:2 * BW + (h + 1) * HD] = (yx * gate).astype(BF16)

        o = _dot(ybuf[...], wout_ref[...])
        o_ref[...] = o
        err = _rms_residual(x_ref[...], o, pg_ref[...]) - tgt_ref[...]
        lacc[...] += _colsum(err * err)
        dres_ref[...] = err * (1.0 / D)

        @pl.when(i == nt - 1)
        def _():
            loss_ref[...] = jnp.full((1, HD), jnp.sum(lacc[...]) * (0.5 / D), F32)

    tile = lambda n: pl.BlockSpec((tm, n), lambda i: (i, 0))
    ext = pltpu.VMEM((tm + HALO, BW), F32)
    return pl.pallas_call(
        body, name="odd_fwd", grid=(nt,),
        in_specs=[tile(D), tile(D), tile(ODD_IN), _halo_spec(ODD_IN, nt, False, tm)] + [_whole()] * 11,
        out_specs=[tile(D), tile(D), _full((1, HD)), tile(BW)],
        out_shape=[S((t, D), F32), S((t, D), F32), S((1, HD), F32), S((t, BW), F32)],
        scratch_shapes=[pltpu.VMEM((tm, MIX), BF16), ext, ext, ext, ext, ext,
                        pltpu.VMEM((1, D), F32), pltpu.VMEM((7, tm + HALO, BW), F32)],
        compiler_params=_params(("arbitrary",)),
    )(x1, tgt, p, p, wbd, cscale, dww, dwb, lng, lnb, wpw, pwb, kv, wout, pg)


def _odd_bwd(dres, o, p, conv, wbd, cscale, dww, dwb, lng, lnb, wpw, pwb, kv, wout, pg):
    t = dres.shape[0]
    tm = min(TM_BWD_ODD, t)
    nt = t // tm

    def body(dres_ref, o_ref, p_ref, ph_ref, conv_ref, wbd_ref, cs_ref, dww_ref, dwb_ref, lng_ref, lnb_ref, wpw_ref,
             pwb_ref, kv_ref, wout_ref, pg_ref, dp_ref, y_ref, do_ref, dpg_ref, dwbd_ref, dcs_ref, ddww_ref, ddwb_ref,
             dlng_ref, dlnb_ref, dwpw_ref, dpwb_ref, dkv_ref,
             dy, za, zb, zc, zd, carry_e, carry_d, wacc, shifted, t1, b1, b2):
        hbuf = zb
        i = pl.program_id(0)
        ti = nt - 1 - i

        @pl.when(i == 0)
        def _():
            for ref in (dpg_ref, dwbd_ref, dcs_ref, ddwb_ref, dlng_ref, dlnb_ref, dwpw_ref, dpwb_ref, dkv_ref, wacc):
                ref[...] = jnp.zeros_like(ref)

        def post_norm_rows(r0, acc):
            sl = pl.ds(r0, RB)
            ov, dv = o_ref[sl, :], dres_ref[sl, :]
            r = lax.rsqrt(_rowmean(ov * ov) + EPS)
            oh = ov * r
            doh = dv * pg_ref[...]
            do_ref[sl, :] = (r * (doh - oh * _rowmean(doh * oh))).astype(BF16)
            return acc + dv * oh

        dpg_ref[...] += _colsum(_loop_rows(tm, RB, post_norm_rows, jnp.zeros((RB, D), F32)))
        dy[...] = _dot_nt(do_ref[...], wout_ref[...])

        za[0:HALO] = jnp.where(ti > 0, ph_ref[:, O_ZC:O_ZC + BW], 0.0)
        za[HALO:HALO + tm] = p_ref[:, O_ZC:O_ZC + BW]
        _pool_causal_levels(za, zb, zc, zd, tm)

        def pooled_rows(r0, carry):
            sl = pl.ds(r0, RB)
            at = lambda ref, back=0: ref[pl.ds(HALO + r0 - back, RB), :]
            inv = _pool_weights(ti * tm + r0, RB)
            sums = (at(zb), at(zc), at(zd), at(zd) + at(zd, 8))
            b1[sl, :] = (_mix4(sums) * inv - p_ref[sl, O_ZC:O_ZC + BW]).astype(BF16)
            return carry

        _loop_rows(tm, RB, pooled_rows)
        for hs in _HALVES:
            t1[:, hs] = _dot(b1[:, hs], wbd_ref[hs, hs])

        def pool_gate_rows(r0, acc):
            sl = pl.ds(r0, RB)
            pm = t1[sl, :]
            gate, dgate = _silu_parts(p_ref[sl, O_GATE:O_GATE + BW])
            dyc = dy[sl, 0:BW]
            yc = pm * cs_ref[...]
            y_ref[sl, 0:BW] = (yc * gate).astype(BF16)
            dp_ref[sl, O_GATE:O_GATE + BW] = (dyc * yc * dgate).astype(BF16)
            dyc = dyc * gate
            b2[sl, :] = (dyc * cs_ref[...]).astype(BF16)
            return acc + dyc * pm

        dcs_ref[...] += _colsum(_loop_rows(tm, RB, pool_gate_rows, jnp.zeros((RB, BW), F32)))
        for hs in _HALVES:
            dwbd_ref[hs, hs] += _dot_tn(b1[:, hs], b2[:, hs])
            t1[:, hs] = _dot_nt(b2[:, hs], wbd_ref[hs, hs])

        def weighted_rows(r0, carry):
            sl = pl.ds(r0, RB)
            za[sl, :] = t1[sl, :] * _pool_weights(ti * tm + r0, RB)
            return carry

        _loop_rows(tm, RB, weighted_rows)
        za[tm:tm + HALO] = jnp.where(i > 0, carry_e[...], 0.0)
        carry_e[...] = za[0:HALO]
        _pool_anticausal_levels(za, zb, zc, zd, tm)

        def pool_back_rows(r0, carry):
            sl = pl.ds(r0, RB)
            ahead = lambda ref, fwd=0: ref[pl.ds(r0 + fwd, RB), :]
            sums = (ahead(zb), ahead(zc), ahead(zd), ahead(zd) + ahead(zd, 8))
            dp_ref[sl, O_ZC:O_ZC + BW] = (_mix4(sums) - t1[sl, :]).astype(BF16)
            return carry

        _loop_rows(tm, RB, pool_back_rows)

        def swish_rows(r0, carry):
            sl = pl.ds(r0, RB)
            zh, _ = _ln_stats(conv_ref[sl, :])
            zn = zh * lng_ref[...] + lnb_ref[...]
            b1[sl, :] = (zn * _sigmoid(zn)).astype(BF16)
            return carry

        _loop_rows(tm, RB, swish_rows)
        t1[...] = _dot(b1[...], wpw_ref[...])

        def conf_gate_rows(r0, acc):
            sl = pl.ds(r0, RB)
            yd = t1[sl, :] + pwb_ref[...]
            gate, dgate = _silu_parts(p_ref[sl, O_GATE + BW:O_GATE + 2 * BW])
            dyc = dy[sl, BW:2 * BW]
            y_ref[sl, BW:2 * BW] = (yd * gate).astype(BF16)
            dp_ref[sl, O_GATE + BW:O_GATE + 2 * BW] = (dyc * yd * dgate).astype(BF16)
            dyd = dyc * gate
            b2[sl, :] = dyd.astype(BF16)
            return acc + dyd

        dpwb_ref[...] += _colsum(_loop_rows(tm, RB, conf_gate_rows, jnp.zeros((RB, BW), F32)))
        dwpw_ref[...] += _dot_tn(b1[...], b2[...])
        t1[...] = _dot_nt(b2[...], wpw_ref[...])

        def norm_back_rows(r0, accs):
            sl = pl.ds(r0, RB)
            zh, rs = _ln_stats(conv_ref[sl, :])
            _, dsilu = _silu_parts(zh * lng_ref[...] + lnb_ref[...])
            dzn = t1[sl, :] * dsilu
            dzd = _ln_bwd(dzn, zh, rs, lng_ref[...])
            hbuf[sl, :] = dzd
            return accs[0] + dzn * zh, accs[1] + dzn, accs[2] + dzd

        zero = jnp.zeros((RB, BW), F32)
        acc_g, acc_b, acc_d = _loop_rows(tm, RB, norm_back_rows, (zero, zero, zero))
        dlng_ref[...] += _colsum(acc_g)
        dlnb_ref[...] += _colsum(acc_b)
        ddwb_ref[...] += _colsum(acc_d)
        hbuf[tm:tm + HALO] = jnp.where(i > 0, carry_d[...], 0.0)
        carry_d[...] = hbuf[0:HALO]
        _shift_copies(hbuf, shifted)

        def conv_back_rows(r0, carry):
            sl = pl.ds(r0, 32)
            sgb = _sigmoid(p_ref[sl, O_GB:O_GB + BW])
            ga = p_ref[sl, O_GA:O_GA + BW]
            dzg = _conv_back(hbuf, shifted, dww_ref, wacc, ga * sgb, r0, CONF_K)
            dp_ref[sl, O_GA:O_GA + BW] = (dzg * sgb).astype(BF16)
            dp_ref[sl, O_GB:O_GB + BW] = (dzg * ga * sgb * (1.0 - sgb)).astype(BF16)
            return carry

        _loop_rows(tm, 32, conv_back_rows, unrolled=False)

        for h in range(NH):
            qs = slice(O_Q + h * HD, O_Q + (h + 1) * HD)
            ks = slice(h * HD, (h + 1) * HD)
            vs = slice(XA + h * HD, XA + (h + 1) * HD)
            gs = slice(O_GATE + 2 * BW + h * HD, O_GATE + 2 * BW + (h + 1) * HD)
            ys = slice(2 * BW + h * HD, 2 * BW + (h + 1) * HD)
            q_b = p_ref[:, qs].astype(BF16)
            prob, yx = _attn_head(q_b, kv_ref[:, ks], kv_ref[:, vs])
            gate, dgate = _silu_parts(p_ref[:, gs])
            dyc = dy[:, ys]
            y_ref[:, ys] = (yx * gate).astype(BF16)
            dp_ref[:, gs] = (dyc * yx * dgate).astype(BF16)
            dyx_b = (dyc * gate).astype(BF16)
            dprob = _dot_nt(dyx_b, kv_ref[:, vs])
            dkv_ref[:, vs] += _dot_tn(prob.astype(BF16), dyx_b)
            ds_b = (prob * (dprob - jnp.sum(dprob * prob, axis=-1, keepdims=True)) * (HD ** -0.5)).astype(BF16)
            dp_ref[:, qs] = _dot(ds_b, kv_ref[:, ks]).astype(BF16)
            dkv_ref[:, ks] += _dot_tn(ds_b, q_b)

        @pl.when(i == nt - 1)
        def _():
            for k in range(CONF_K):
                ddww_ref[k:k + 1, :] = _colsum(wacc[k * 8:(k + 1) * 8, :])
            ddww_ref[CONF_K:CONF_K + 1, :] = jnp.zeros((1, BW), F32)

    rtile = lambda n: pl.BlockSpec((tm, n), lambda i: (nt - 1 - i, 0))
    outs = [S((t, ODD_IN), BF16), S((t, MIX), BF16), S((t, D), BF16), S((1, D), F32), S((BW, BW), F32),
            S((1, BW), F32), S((CONF_K + 1, BW), F32), S((1, BW), F32), S((1, BW), F32), S((1, BW), F32),
            S((BW, BW), F32), S((1, BW), F32), S((N_MEM, 2 * XA), F32)]
    ext = pltpu.VMEM((tm + HALO, BW), F32)
    return pl.pallas_call(
        body, name="odd_bwd", grid=(nt,),
        in_specs=[rtile(D), rtile(D), rtile(ODD_IN), _halo_spec(ODD_IN, nt, True, tm), rtile(BW)] + [_whole()] * 11,
        out_specs=[rtile(ODD_IN), rtile(MIX), rtile(D)] + [_full(s.shape) for s in outs[3:]],
        out_shape=outs,
        scratch_shapes=[pltpu.VMEM((tm, MIX), F32), ext, ext, ext, ext,
                        pltpu.VMEM((HALO, BW), F32), pltpu.VMEM((HALO, BW), F32), pltpu.VMEM((CONF_K * 8, BW), F32),
                        pltpu.VMEM((7, tm + HALO, BW), F32),
                        pltpu.VMEM((tm, BW), F32), pltpu.VMEM((tm, BW), BF16), pltpu.VMEM((tm, BW), BF16)],
        compiler_params=pltpu.CompilerParams(dimension_semantics=("arbitrary",),
                                             vmem_limit_bytes=VMEM_LIMIT_ODD_BWD_V7X),
    )(dres, o, p, p, conv, wbd, cscale, dww, dwb, lng, lnb, wpw, pwb, kv, wout, pg)


def _pick_rows(n):
    for rows in (640, 2432, 1024, 768):
        if n % rows == 0:
            return rows
    return n


def _pad_rows(a, rows):
    return jnp.pad(a, ((0, rows - a.shape[0]), (0, 0)))


def _step(x, mem, tgt, ex):
    t = x.shape[0]
    tm = min(512, t)
    w, deps = ex.first()
    causal = jnp.tril(jnp.ones((CHUNK, CHUNK), bool))
    ws = jnp.where(causal[None], w["even_a_ws"], 0.0).astype(BF16)
    wst = jnp.transpose(ws, (0, 2, 1))
    bmap = jnp.repeat(w["even_a_bs"].T, GRP, axis=1)
    wc = _pad_rows(w["even_b_conv"], 8)
    wbd = jax.scipy.linalg.block_diag(*[w["odd_c_wgrp"][g] for g in range(NH)]).astype(BF16)
    dww = _pad_rows(w["odd_d_dw_w"], CONF_K + 1)
    tk = min(1024, t)
    zeros = jnp.zeros_like(mem)

    p_e, h_e = _rms_matmul(x, w["even_pre_g"], w["even_w_in"], tm=tm, name="in_even", transposed=True, deps=deps)
    w.update(ex.even_rest(h_e))
    kv_e, memn_e = _rms_matmul(mem, w["even_mem_g"], w["even_w_kv"], tm=N_MEM, name="kv_even", out_dtype=BF16)
    even_args = (w["even_a_ln_g"], w["even_a_ln_b"], ws)
    o_e, x1 = _even_fwd(x, p_e, *even_args, bmap, wc, kv_e, w["even_w_out"], w["even_post_g"])
    w.update(ex.odd(o_e))
    kv_o, memn_o = _rms_matmul(mem, w["odd_mem_g"], w["odd_w_kv"], tm=N_MEM, name="kv_odd", out_dtype=BF16)
    p_o, h_o = _rms_matmul(x1, w["odd_pre_g"], w["odd_w_in"], tm=tm, name="in_odd", transposed=True)
    odd_args = (wbd, w["odd_c_scale"], dww, w["odd_d_dw_b"], w["odd_d_ln_g"], w["odd_d_ln_b"], w["odd_d_pw_w"],
                w["odd_d_pw_b"], kv_o, w["odd_w_out"], w["odd_post_g"])
    o_o, dres, loss, conv_o = _odd_fwd(x1, tgt, p_o, *odd_args)

    g = {}
    (dp_o, y_o, do_o, post_g_o, dwbd, g["odd_c_scale"], ddww, g["odd_d_dw_b"], g["odd_d_ln_g"], g["odd_d_ln_b"],
     dwpw, g["odd_d_pw_b"], dkv_o) = _odd_bwd(dres, o_o, p_o, conv_o, *odd_args)
    g["odd_post_g"] = post_g_o
    g["odd_d_dw_w"] = ddww[:CONF_K]
    dkv_o = dkv_o.astype(BF16)
    deps = ex.send("odd", {
        "odd_w_in": _tn_matmul(dp_o, h_o, tmc=_pick_rows(ODD_IN), tk=tk, out_dtype=BF16, name="dw_in_odd"),
        "odd_w_out": _tn_matmul(y_o, do_o, tmc=MIX, tk=tk, out_dtype=BF16, name="dw_out_odd"),
        "odd_w_kv": _tn_matmul(memn_o, dkv_o, tmc=D, tk=N_MEM, out_dtype=BF16, name="dw_kv_odd"),
        "odd_d_pw_w": dwpw.astype(BF16), "loss": loss,
        "odd_c_wgrp": jnp.concatenate([dwbd[i * GRP:(i + 1) * GRP, i * GRP:(i + 1) * GRP] for i in range(NH)])})
    dx1, g["odd_pre_g"] = _nt_matmul_rms_bwd(dp_o, w["odd_w_in"], x1, w["odd_pre_g"], dres, tm=tm,
                                             name="dx_odd", transposed=True, deps=deps)
    _, g["odd_mem_g"] = _nt_matmul_rms_bwd(dkv_o, w["odd_w_kv"], mem, w["odd_mem_g"], zeros, tm=N_MEM,
                                           name="dmem_odd")

    (dp_e, y_e, do_e, post_g_e, dws, dbs, ln_g_e, ln_b_e, dwc, dkv_e) = _even_bwd(
        dx1, o_e, p_e, *even_args, wst, bmap, wc, kv_e, w["even_w_out"], w["even_post_g"])
    g["even_b_conv"] = dwc[:3]
    dkv_e = dkv_e.astype(BF16)
    deps = ex.send("even_rest", {
        "even_w_out": _tn_matmul(y_e, do_e, tmc=MIX, tk=tk, out_dtype=BF16, name="dw_out_even"),
        "even_w_kv": _tn_matmul(memn_e, dkv_e, tmc=D, tk=N_MEM, out_dtype=BF16, name="dw_kv_even"),
        "even_a_ln_g": ln_g_e, "even_a_ln_b": ln_b_e,
        "even_a_ws": dws.reshape(NH * CHUNK, CHUNK), "even_a_bs": dbs[:, ::HD].T})
    g["even_w_in"] = _tn_matmul(dp_e, h_e, tmc=_pick_rows(EVEN_IN), tk=tk, out_dtype=BF16, name="dw_in_even",
                                deps=deps)
    deps = ex.send("even_in", g)
    grad_x, pre_g_e = _nt_matmul_rms_bwd(dp_e, w["even_w_in"], x, w["even_pre_g"], dx1, tm=tm,
                                         name="dx_even", transposed=True, deps=deps)
    _, mem_g_e = _nt_matmul_rms_bwd(dkv_e, w["even_w_kv"], mem, w["even_mem_g"], zeros, tm=N_MEM, name="dmem_even",
                                    deps=(grad_x,))
    deps = ex.send("even_gains", {"even_pre_g": pre_g_e, "even_mem_g": mem_g_e, "even_post_g": post_g_e})
    return grad_x, deps


def _place():
    return lax.axis_index("x"), lax.axis_index("y"), lax.axis_index("c")


def _index(px, py, pc):
    return 4 * px + 2 * py + pc


_COPIES = N_DEV - 1


def _all_gather(arrs, name):
    n = len(arrs)

    def body(*refs):
        ins, outs = refs[:n], refs[n:2 * n]
        send_sems, recv_sems, local_sems = refs[2 * n:]
        x, y, c = _place()
        me, sibling = (x, y, c), (x, y, 1 - c)
        chips = [(1 - x, y), (x, 1 - y), (1 - x, 1 - y)]

        def copy(a, k, block, to, src=None):
            dst = outs[a].at[_index(*block)]
            return pltpu.make_async_remote_copy(
                src_ref=dst if src is None else src, dst_ref=dst, send_sem=send_sems.at[a * _COPIES + k],
                recv_sem=recv_sems.at[a * _COPIES + k], device_id=to, device_id_type=MESH)

        mine = [pltpu.make_async_copy(ins[a], outs[a].at[_index(*me)], local_sems.at[a]) for a in range(n)]
        first = []
        for a in range(n):
            mine[a].start()
            first.append(copy(a, 0, me, sibling, src=ins[a]))
            first += [copy(a, 1 + j, me, (*chip, c), src=ins[a]) for j, chip in enumerate(chips)]
        for cp in first:
            cp.start()
        passed = []
        for j, chip in enumerate(chips):
            for a in range(n):
                copy(a, 1 + j, (*chip, c), me).wait_recv()
                passed.append(copy(a, 4 + j, (*chip, c), sibling))
                passed[-1].start()
        for a in range(n):
            copy(a, 0, sibling, me).wait_recv()
            for j, chip in enumerate(chips):
                copy(a, 4 + j, (*chip, 1 - c), me).wait_recv()
        for cp in first + passed:
            cp.wait_send()
        for cp in mine:
            cp.wait()

    return pl.pallas_call(
        body, name=name, in_specs=[_ANY] * n, out_specs=[_ANY] * n,
        out_shape=[S((N_DEV,) + a.shape, a.dtype) for a in arrs],
        scratch_shapes=[pltpu.SemaphoreType.DMA((n * _COPIES,)), pltpu.SemaphoreType.DMA((n * _COPIES,)),
                        pltpu.SemaphoreType.DMA((n,))],
    )(*arrs)


_HBM = pl.BlockSpec(memory_space=pltpu.HBM)
_SEM = pl.BlockSpec(memory_space=pltpu.SEMAPHORE)
_EFFECT = pltpu.SideEffectType.DATAFLOW_SIDE_EFFECTING


_ALL_FLIPS = [(k >> 2 & 1, k >> 1 & 1, k & 1) for k in range(1, N_DEV)]
_CHIP_FLIPS = [(1, 0, 0), (0, 1, 0), (1, 1, 0)]
_FLIPS = {"gather": _ALL_FLIPS, "scatter": _ALL_FLIPS, "gather_chips": [(0, 0, 1)] + _CHIP_FLIPS,
          "scatter_chips": _CHIP_FLIPS}


def _landing_shape(kind, a):
    return (N_DEV,) + a.shape if kind.startswith("gather") else a.shape


def _exchange_copies(kinds, srcs, lands, send_sems, recv_sems, local_sems, arriving):
    x, y, c = _place()
    mine = _index(x, y, c)
    remote, local = [], []
    for a, kind in enumerate(kinds):
        by_chip = kind == "scatter_chips"
        here = 2 * x + y if by_chip else mine
        own = srcs[a] if kind.startswith("gather") else srcs[a].at[here]
        local.append(pltpu.make_async_copy(own, lands[a].at[here], local_sems.at[a]))
        for k, (fx, fy, fc) in enumerate(_FLIPS[kind]):
            peer = (1 - x if fx else x, 1 - y if fy else y, 1 - c if fc else c)
            there = 2 * peer[0] + peer[1] if by_chip else _index(*peer)
            remote.append(pltpu.make_async_remote_copy(
                src_ref=srcs[a] if kind.startswith("gather") else srcs[a].at[there],
                dst_ref=lands[a].at[there if arriving else here],
                send_sem=send_sems.at[a * _COPIES + k], recv_sem=recv_sems.at[a * _COPIES + k],
                device_id=peer, device_id_type=MESH))
    return remote, local


def _exchange_start(items, name, deps=()):
    kinds = [kind for kind, _ in items]
    srcs = [a for _, a in items]
    n = len(items)
    lands = [lax.empty(_landing_shape(kind, a), a.dtype) for kind, a in items]

    def body(*refs):
        send_sems, recv_sems, local_sems = refs[2 * n + len(deps):2 * n + len(deps) + 3]
        remote, local = _exchange_copies(kinds, refs[:n], refs[n:2 * n], send_sems, recv_sems, local_sems, False)
        for cp in local + remote:
            cp.start()
        refs[-1][...] = jnp.zeros_like(refs[-1])

    held = [pltpu.HBM(a.shape, a.dtype) for a in srcs + lands]
    res = pl.pallas_call(
        body, name=name,
        out_shape=(pltpu.SemaphoreType.DMA((n * _COPIES,)), pltpu.SemaphoreType.DMA((n * _COPIES,)),
                   pltpu.SemaphoreType.DMA((n,)), *held, S((8, 128), F32)),
        in_specs=[_HBM] * (2 * n) + [_ANY] * len(deps),
        out_specs=(_SEM, _SEM, _SEM, *[_HBM] * (2 * n), _whole()),
        input_output_aliases={i: 3 + i for i in range(2 * n)},
        compiler_params=pltpu.CompilerParams(has_side_effects=_EFFECT),
    )(*[pltpu.with_memory_space_constraint(a, pltpu.HBM) for a in srcs + lands], *deps)
    return (kinds, res[:3], res[3:3 + 2 * n]), res[-1]


def _exchange_wait(handle, after, name):
    kinds, sems, held = handle
    n = len(kinds)

    def body(*refs):
        send_sems, recv_sems, local_sems = refs[2 * n:2 * n + 3]
        remote, local = _exchange_copies(kinds, refs[:n], refs[n:2 * n], send_sems, recv_sems, local_sems, True)
        for cp in remote:
            cp.wait_send()
            cp.wait_recv()
        for cp in local:
            cp.wait()

    res = pl.pallas_call(
        body, name=name, out_shape=[pltpu.HBM(a.shape, a.dtype) for a in held],
        in_specs=[_HBM] * (2 * n) + [_SEM] * 3 + [_ANY] * len(after), out_specs=[_HBM] * (2 * n),
        input_output_aliases={i: i for i in range(2 * n)},
        compiler_params=pltpu.CompilerParams(has_side_effects=_EFFECT),
    )(*held, *sems, *after)
    return res[n:]


_CHIPS = [(0, 0), (0, 1), (1, 0), (1, 1)]
_N_CHIPS = len(_CHIPS)


def _sibling_forward(lands, name):
    n = len(lands)

    def body(*refs):
        ins, outs = refs[:n], refs[n:2 * n]
        send_sems, recv_sems = refs[2 * n:]
        x, y, c = _place()
        sent, arriving = [], []
        for a in range(n):
            for j, (fx, fy, _) in enumerate(_CHIP_FLIPS):
                chip = (1 - x if fx else x, 1 - y if fy else y)
                sems = dict(send_sem=send_sems.at[a * 3 + j], recv_sem=recv_sems.at[a * 3 + j],
                            device_id=(x, y, 1 - c), device_id_type=MESH)
                mine, theirs = _index(*chip, c), _index(*chip, 1 - c)
                sent.append(pltpu.make_async_remote_copy(src_ref=ins[a].at[mine], dst_ref=outs[a].at[mine], **sems))
                arriving.append(pltpu.make_async_remote_copy(src_ref=ins[a].at[theirs], dst_ref=outs[a].at[theirs],
                                                             **sems))
        for cp in sent:
            cp.start()
        for cp in sent:
            cp.wait_send()
        for cp in arriving:
            cp.wait_recv()

    return pl.pallas_call(
        body, name=name, in_specs=[_ANY] * n, out_specs=[_ANY] * n,
        out_shape=[S(a.shape, a.dtype) for a in lands], input_output_aliases={a: a for a in range(n)},
        scratch_shapes=[pltpu.SemaphoreType.DMA((3 * n,)), pltpu.SemaphoreType.DMA((3 * n,))],
    )(*lands)


def _sibling_swap(arrs, name):
    n = len(arrs)

    def body(*refs):
        ins, outs = refs[:n], refs[n:2 * n]
        send_sems, recv_sems = refs[2 * n:]
        x, y, c = _place()
        copies = []
        for a in range(n):
            for q, chip in enumerate(_CHIPS):
                copies.append(pltpu.make_async_remote_copy(
                    src_ref=ins[a].at[_index(*chip, 1 - c)], dst_ref=outs[a].at[q],
                    send_sem=send_sems.at[a * _N_CHIPS + q], recv_sem=recv_sems.at[a * _N_CHIPS + q],
                    device_id=(x, y, 1 - c), device_id_type=MESH))
        for cp in copies:
            cp.start()
        for cp in copies:
            cp.wait_send()
            cp.wait_recv()

    return pl.pallas_call(
        body, name=name, in_specs=[_ANY] * n, out_specs=[_ANY] * n,
        out_shape=[S((_N_CHIPS,) + a.shape[1:], a.dtype) for a in arrs],
        scratch_shapes=[pltpu.SemaphoreType.DMA((_N_CHIPS * n,)), pltpu.SemaphoreType.DMA((_N_CHIPS * n,))],
    )(*arrs)


def _add_partials(mine, theirs, *, tr, name):
    _, r, c = mine.shape

    def body(mine_ref, theirs_ref, out_ref):
        core = lax.axis_index("c")
        own = jnp.where(core == 0, mine_ref[0].astype(F32), mine_ref[1].astype(F32))
        out_ref[0] = (own + theirs_ref[0].astype(F32)).astype(out_ref.dtype)

    return pl.pallas_call(
        body, name=name, grid=(_N_CHIPS, r // tr),
        in_specs=[pl.BlockSpec((2, tr, c), lambda q, i: (q, i, 0)), pl.BlockSpec((1, tr, c), lambda q, i: (q, i, 0))],
        out_specs=pl.BlockSpec((1, tr, c), lambda q, i: (q, i, 0)),
        out_shape=S((_N_CHIPS, r, c), mine.dtype),
        compiler_params=_params(("arbitrary", "arbitrary")),
    )(mine, theirs)


def _adamw(w, g, m, v):
    m = ADAM_B1 * m + (1.0 - ADAM_B1) * g
    v = ADAM_B2 * v + (1.0 - ADAM_B2) * (g * g)
    m_hat = m / (1.0 - ADAM_B1 ** ADAM_STEP)
    v_hat = v / (1.0 - ADAM_B2 ** ADAM_STEP)
    return -ADAM_LR * (m_hat / (jnp.sqrt(v_hat) + ADAM_EPS) + ADAM_WD * w), m, v


def _sum_devices(ref, rows):
    total = ref[0, rows, :].astype(F32)
    for s in range(1, ref.shape[0]):
        total = total + ref[s, rows, :].astype(F32)
    return total


def _adam_big(recv, w, m, v, *, tr, name):
    r, c = w.shape

    def body(recv_ref, w_ref, m_ref, v_ref, g_ref, d_ref, m2_ref, v2_ref):
        g = _sum_devices(recv_ref, slice(None))
        g_ref[...] = g
        d_ref[...], m2_ref[...], v2_ref[...] = _adamw(w_ref[...], g, m_ref[...], v_ref[...])

    blk = pl.BlockSpec((tr, c), lambda i: (i, 0))
    return pl.pallas_call(
        body, name=name, grid=(r // tr,),
        in_specs=[pl.BlockSpec((recv.shape[0], tr, c), lambda i: (0, i, 0)), blk, blk, blk],
        out_specs=[blk] * 4, out_shape=[S((r, c), F32)] * 4,
        compiler_params=_params(("arbitrary",)),
    )(recv, w, m, v)


_REPLICATED = {"even_pre_g": (0, 0, 1), "even_mem_g": (0, 8, 1), "even_post_g": (0, 16, 1),
               "even_a_ln_g": (1, 0, 1), "even_a_ln_b": (1, 8, 1),
               "even_a_ws": (2, 0, NH * CHUNK), "even_a_bs": (2, NH * CHUNK, NH),
               "odd_c_wgrp": (3, 0, NH * GRP)}
_SHARDED = {"odd_pre_g": (4, 0, 1), "odd_mem_g": (4, 8, 1), "odd_post_g": (4, 16, 1),
            "even_b_conv": (5, 0, 3), "odd_c_scale": (5, 8, 1), "odd_d_dw_w": (5, 16, CONF_K),
            "odd_d_dw_b": (5, 48, 1), "odd_d_ln_g": (5, 56, 1), "odd_d_ln_b": (5, 64, 1), "odd_d_pw_b": (5, 72, 1)}
_SMALL = {**_REPLICATED, **_SHARDED}
_SMALL_ROWS = {0: 24, 1: 16, 2: NH * CHUNK + 8, 3: NH * GRP, 4: 24, 5: 80}


def _adam_small(sources, wmv):
    names = list(_SMALL)
    ns = len(sources)

    def body(*refs):
        src = refs[:ns]
        ins = refs[ns:ns + 3 * len(names)]
        outs = refs[ns + 3 * len(names):]
        outs[-1][...] = _sum_devices(src[-1], slice(0, 1))
        for i, nm in enumerate(names):
            a, row0, rows = _SMALL[nm]
            g = _sum_devices(src[a], slice(row0, row0 + rows))
            w_ref, m_ref, v_ref = ins[3 * i:3 * i + 3]
            g_ref, d_ref, m2_ref, v2_ref = outs[4 * i:4 * i + 4]
            g_ref[...] = g
            d_ref[...], m2_ref[...], v2_ref[...] = _adamw(w_ref[...], g, m_ref[...], v_ref[...])

    flat = [t for nm in names for t in wmv[nm]]
    out_shape = [S(wmv[nm][0].shape, F32) for nm in names for _ in range(4)] + [S((1, HD), F32)]
    res = pl.pallas_call(
        body, name="adam_small", in_specs=[_whole()] * (ns + len(flat)), out_specs=[_whole()] * len(out_shape),
        out_shape=out_shape, compiler_params=_params(),
    )(*sources, *flat)
    return {nm: tuple(res[4 * i:4 * i + 4]) for i, nm in enumerate(names)}, res[-1]


_WEIGHTS = ["even_pre_g", "even_w_in", "even_a_ln_g", "even_a_ln_b", "even_a_ws", "even_a_bs", "even_b_conv",
            "even_mem_g", "even_w_kv", "even_w_out", "even_post_g", "odd_pre_g", "odd_w_in", "odd_c_wgrp",
            "odd_c_scale", "odd_d_dw_w", "odd_d_dw_b", "odd_d_ln_g", "odd_d_ln_b", "odd_d_pw_w", "odd_d_pw_b",
            "odd_mem_g", "odd_w_kv", "odd_w_out", "odd_post_g"]
_TRANSPOSED = ["even_w_in", "odd_w_in"]
_BIG = _TRANSPOSED + ["even_w_kv", "even_w_out", "odd_w_kv", "odd_w_out", "odd_d_pw_w"]
_BIG_TILE_ROWS = {"even_w_in": 400, "odd_w_in": 304, "even_w_kv": 128, "even_w_out": 128, "odd_w_kv": 128,
                  "odd_w_out": 128, "odd_d_pw_w": 96}


def _view2d(a, transposed):
    a = a[0]
    if a.ndim == 1:
        return a[None]
    if transposed:
        return a.T
    return a.reshape(-1, a.shape[-1])


def _rows8(a):
    return _pad_rows(a, -(-a.shape[0] // 8) * 8)


def _pack_rows(parts):
    return jnp.concatenate([_rows8(p) for p in parts], axis=0)


def _unshard_cols(a):
    return jnp.transpose(a, (1, 0, 2)).reshape(a.shape[1], N_DEV * a.shape[2])


def _shard_cols(a):
    return jnp.transpose(a.reshape(a.shape[0], N_DEV, a.shape[1] // N_DEV), (1, 0, 2))


def _rows_of(a):
    return a.reshape(-1, a.shape[-1])


_GROUPS = {"odd": (["odd_w_in", "odd_w_out", "odd_w_kv", "odd_d_pw_w"], [3], []),
           "even_rest": (["even_w_out", "even_w_kv"], [1, 2], []),
           "even_in": (["even_w_in"], [], [4, 5]),
           "even_gains": ([], [0], [])}


_TWO_LEVEL = ("even_in",)


class _MeshExchange:
    def __init__(self, shard):
        self.shard = shard
        self.handles = {}

    def first(self):
        shard = self.shard
        packs = [_pack_rows([shard[nm] for nm in _SHARDED if _SHARDED[nm][0] == a]) for a in (4, 5)]
        w_in, p128, p96 = _all_gather([shard["even_w_in"].astype(BF16)] + packs, "gather_first")
        w = {nm: shard[nm] for nm in _REPLICATED}
        w["even_a_ws"] = w["even_a_ws"].reshape(NH, CHUNK, CHUNK)
        w["odd_c_wgrp"] = w["odd_c_wgrp"].reshape(NH, GRP, GRP)
        w["even_w_in"] = _rows_of(w_in)
        full_packs = {4: _unshard_cols(p128), 5: _unshard_cols(p96)}
        for nm, (a, row0, rows) in _SHARDED.items():
            w[nm] = full_packs[a][row0:row0 + rows]
        later = lambda names: [("gather_chips", shard[nm].astype(BF16)) for nm in names]
        self.handles["w_even"], token = _exchange_start(later(["even_w_kv", "even_w_out"]), "gather_even_start",
                                                        deps=(w_in,))
        self.handles["w_odd"], token = _exchange_start(later(["odd_w_in", "odd_w_kv", "odd_w_out", "odd_d_pw_w"]),
                                                       "gather_odd_start", deps=(token,))
        return w, (token,)

    def even_rest(self, after):
        landed = _exchange_wait(self.handles.pop("w_even"), (after,), "gather_even_wait")
        kv, out = _sibling_forward(landed, "forward_even")
        return {"even_w_kv": _rows_of(kv), "even_w_out": _rows_of(out)}

    def odd(self, after):
        landed = _exchange_wait(self.handles.pop("w_odd"), (after,), "gather_odd_wait")
        w_in, kv, out, pw = _sibling_forward(landed, "forward_odd")
        return {"odd_w_in": _rows_of(w_in), "odd_w_kv": _rows_of(kv), "odd_w_out": _rows_of(out),
                "odd_d_pw_w": _rows_of(pw)}

    def send(self, group, g):
        big, replicated, sharded = _GROUPS[group]
        by_owner = [g[nm].reshape(N_DEV, -1, g[nm].shape[-1]) for nm in big]
        if group in _TWO_LEVEL:
            theirs = _sibling_swap(by_owner, "swap_" + group)
            items = [("scatter_chips", _add_partials(a, b, tr=_BIG_TILE_ROWS[nm], name="chip_sum_" + nm))
                     for nm, a, b in zip(big, by_owner, theirs)]
        else:
            items = [("scatter", a) for a in by_owner]
        items += [("gather", _pack_rows([g[nm] for nm in _REPLICATED if _REPLICATED[nm][0] == a]))
                  for a in replicated]
        items += [("scatter", _shard_cols(_pack_rows([g[nm] for nm in _SHARDED if _SHARDED[nm][0] == a])))
                  for a in sharded]
        if group == "odd":
            items.append(("gather", _rows8(g["loss"])))
        self.handles[group], token = _exchange_start(items, "send_" + group + "_start")
        return (token,)

    def receive(self, group, after):
        after = after if isinstance(after, tuple) else (after,)
        return _exchange_wait(self.handles.pop(group), after, "send_" + group + "_wait")


def kernel(x, mem, even_pre_g, even_w_in, even_a_ln_g, even_a_ln_b, even_a_ws, even_a_bs, even_b_conv, even_mem_g, even_w_kv, even_w_out, even_post_g, odd_pre_g, odd_w_in, odd_c_wgrp, odd_c_scale, odd_d_dw_w, odd_d_dw_b, odd_d_ln_g, odd_d_ln_b, odd_d_pw_w, odd_d_pw_b, odd_mem_g, odd_w_kv, odd_w_out, odd_post_g, loss_target, m_even_pre_g, m_even_w_in, m_even_a_ln_g, m_even_a_ln_b, m_even_a_ws, m_even_a_bs, m_even_b_conv, m_even_mem_g, m_even_w_kv, m_even_w_out, m_even_post_g, m_odd_pre_g, m_odd_w_in, m_odd_c_wgrp, m_odd_c_scale, m_odd_d_dw_w, m_odd_d_dw_b, m_odd_d_ln_g, m_odd_d_ln_b, m_odd_d_pw_w, m_odd_d_pw_b, m_odd_mem_g, m_odd_w_kv, m_odd_w_out, m_odd_post_g, v_even_pre_g, v_even_w_in, v_even_a_ln_g, v_even_a_ln_b, v_even_a_ws, v_even_a_bs, v_even_b_conv, v_even_mem_g, v_even_w_kv, v_even_w_out, v_even_post_g, v_odd_pre_g, v_odd_w_in, v_odd_c_wgrp, v_odd_c_scale, v_odd_d_dw_w, v_odd_d_dw_b, v_odd_d_ln_g, v_odd_d_ln_b, v_odd_d_pw_w, v_odd_d_pw_b, v_odd_mem_g, v_odd_w_kv, v_odd_w_out, v_odd_post_g):
    given = dict(locals())
    view = lambda nm, kind: _view2d(given[kind + nm], nm in _TRANSPOSED)
    shard = {nm: view(nm, "") for nm in _WEIGHTS}
    wmv = {nm: (shard[nm], view(nm, "m_"), view(nm, "v_")) for nm in _WEIGHTS}

    ex = _MeshExchange(shard)
    grad_x, last = _step(x[0], mem[0], loss_target[0], ex)

    res = {}

    def update(group, after):
        names = _GROUPS[group][0]
        landed = ex.receive(group, after)
        for nm, recv in zip(names, landed):
            res[nm] = _adam_big(recv, *wmv[nm], tr=_BIG_TILE_ROWS[nm], name="adam_" + nm)
        return landed[len(names):]

    c192, losses = update("odd", last)
    c768, c128 = update("even_rest", res["odd_d_pw_w"][0])
    a128, a96 = update("even_in", res["even_w_kv"][0])
    (c1024,) = update("even_gains", res["even_w_in"][0])
    small, loss = _adam_small([c1024, c768, c128, c192, a128, a96, losses], {nm: wmv[nm] for nm in _SMALL})
    res.update(small)
    total = loss[0, 0]
    back = lambda nm, a: (a.T if nm in _TRANSPOSED else a).reshape(given[nm].shape)
    outs = [[back(nm, res[nm][i]) for nm in _WEIGHTS] for i in range(4)]
    return (total, grad_x[None], *outs[0], *outs[1], *outs[2], *outs[3])
```

```python
import functools

import jax
import jax.numpy as jnp
from jax import lax
from jax.experimental import pallas as pl
from jax.experimental.pallas import tpu as pltpu

F32 = jnp.float32
BF16 = jnp.bfloat16
S = jax.ShapeDtypeStruct
MESH = pl.DeviceIdType.MESH
AXES = ("x", "y", "c")
N_DEV = 8

D = 1024
BW = 768
XA = 512
HD = 128
NH = 4
MIX = 2048
CHUNK = 128
GRP = 192
N_MEM = 256
CONF_K = 31
EPS = 1e-6
HALO = 32
POOL_WINDOWS = (2, 4, 8, 16)
TM_FWD_EVEN = 512
TM_FWD_ODD = 256
TM_BWD_EVEN = 256
TM_BWD_ODD = 256
RB = 16

E_U, E_V, E_BG, E_CG, E_XIN, E_Q, E_GATE = 0, 768, 1536, 2304, 3072, 3840, 4352
EVEN_IN = 6400
O_ZC, O_GA, O_GB, O_Q, O_GATE = 0, 768, 1536, 2304, 2816
ODD_IN = 4864

ADAM_LR, ADAM_B1, ADAM_B2, ADAM_EPS, ADAM_WD, ADAM_STEP = 0.001, 0.9, 0.999, 1e-08, 0.01, 10

VMEM_LIMIT_V7X = 56 * 1024 * 1024
VMEM_LIMIT_ODD_BWD_V7X = 62 * 1024 * 1024


def _params(sem=None):
    return pltpu.CompilerParams(dimension_semantics=sem, vmem_limit_bytes=VMEM_LIMIT_V7X)


def _dot(a, b):
    return jnp.dot(a, b, preferred_element_type=F32)


def _dot_nt(a, b):
    return lax.dot_general(a, b, (((1,), (1,)), ((), ())), preferred_element_type=F32)


def _dot_tn(a, b):
    return lax.dot_general(a, b, (((0,), (0,)), ((), ())), preferred_element_type=F32)


def _sigmoid(z):
    return 1.0 / (1.0 + jnp.exp(-z))


def _rowmean(a):
    return jnp.mean(a, axis=-1, keepdims=True)


def _colsum(a):
    return jnp.sum(a, axis=0, keepdims=True)


def _ln_stats(v):
    mu = _rowmean(v)
    vc = v - mu
    rs = lax.rsqrt(_rowmean(vc * vc) + EPS)
    return vc * rs, rs


def _ln_bwd(dn, vh, rs, g):
    dvh = dn * g
    return rs * (dvh - _rowmean(dvh) - vh * _rowmean(dvh * vh))


def _group_masks():
    col = lax.broadcasted_iota(jnp.int32, (1, BW), 1)
    return [((col >= GRP * h) & (col < GRP * (h + 1))).astype(F32) for h in range(NH)]


def _full(shape):
    nd = len(shape)
    return pl.BlockSpec(shape, lambda *_: (0,) * nd)


def _whole():
    return pl.BlockSpec(memory_space=pltpu.VMEM)


_ANY = pl.BlockSpec(memory_space=pl.ANY)


def _after(body, n_in, deps):
    def ordered(*refs):
        return body(*refs[:n_in], *refs[n_in + len(deps):])
    return ordered


def _rms_matmul(x, g, w, *, tm, name, transposed=False, out_dtype=F32, deps=()):
    t, d = x.shape
    n = w.shape[0] if transposed else w.shape[1]

    def body(x_ref, g_ref, w_ref, p_ref, h_ref):
        xv = x_ref[...]
        r = lax.rsqrt(_rowmean(xv * xv) + EPS)
        h = (xv * r * g_ref[...]).astype(BF16)
        h_ref[...] = h
        p_ref[...] = (_dot_nt(h, w_ref[...]) if transposed else _dot(h, w_ref[...])).astype(out_dtype)

    return pl.pallas_call(
        _after(body, 3, deps), name=name, grid=(t // tm,),
        in_specs=[pl.BlockSpec((tm, d), lambda i: (i, 0)), _whole(), _whole()] + [_ANY] * len(deps),
        out_specs=[pl.BlockSpec((tm, n), lambda i: (i, 0)), pl.BlockSpec((tm, d), lambda i: (i, 0))],
        out_shape=[S((t, n), out_dtype), S((t, d), BF16)],
        compiler_params=_params(("arbitrary",)),
    )(x, g, w, *deps)


def _nt_matmul_rms_bwd(dp, w, x, g, dres, *, tm, name, transposed=False, deps=()):
    t, n = dp.shape
    d = x.shape[1]

    def body(dp_ref, w_ref, x_ref, g_ref, dres_ref, dx_ref, dg_ref):
        @pl.when(pl.program_id(0) == 0)
        def _():
            dg_ref[...] = jnp.zeros_like(dg_ref)

        dh = _dot(dp_ref[...], w_ref[...]) if transposed else _dot_nt(dp_ref[...], w_ref[...])
        xv = x_ref[...]
        r = lax.rsqrt(_rowmean(xv * xv) + EPS)
        xh = xv * r
        dg_ref[...] += _colsum(dh * xh)
        dxh = dh * g_ref[...]
        dx_ref[...] = dres_ref[...] + r * (dxh - xh * _rowmean(dxh * xh))

    return pl.pallas_call(
        _after(body, 5, deps), name=name, grid=(t // tm,),
        in_specs=[pl.BlockSpec((tm, n), lambda i: (i, 0)), _whole(), pl.BlockSpec((tm, d), lambda i: (i, 0)),
                  _whole(), pl.BlockSpec((tm, d), lambda i: (i, 0))] + [_ANY] * len(deps),
        out_specs=[pl.BlockSpec((tm, d), lambda i: (i, 0)), pl.BlockSpec((1, d), lambda i: (0, 0))],
        out_shape=[S((t, d), F32), S((1, d), F32)],
        compiler_params=_params(("arbitrary",)),
    )(dp, w, x, g, dres, *deps)


def _tn_matmul(a, b, *, tmc, tk, out_dtype, name, deps=()):
    t, m = a.shape
    n = b.shape[1]
    nk = t // tk

    def body(a_ref, b_ref, o_ref, acc_ref):
        k = pl.program_id(1)

        @pl.when(k == 0)
        def _():
            acc_ref[...] = jnp.zeros_like(acc_ref)

        acc_ref[...] += _dot_tn(a_ref[...], b_ref[...])

        @pl.when(k == nk - 1)
        def _():
            o_ref[...] = acc_ref[...].astype(out_dtype)

    return pl.pallas_call(
        _after(body, 2, deps), name=name, grid=(m // tmc, nk),
        in_specs=[pl.BlockSpec((tk, tmc), lambda j, k: (k, j)), pl.BlockSpec((tk, n), lambda j, k: (k, 0))]
        + [_ANY] * len(deps),
        out_specs=pl.BlockSpec((tmc, n), lambda j, k: (j, 0)),
        out_shape=S((m, n), out_dtype),
        scratch_shapes=[pltpu.VMEM((tmc, n), F32)],
        compiler_params=_params(("arbitrary", "arbitrary")),
    )(a, b, *deps)


def _silu_parts(gt):
    sg = _sigmoid(gt)
    return gt * sg, sg * (1.0 + gt * (1.0 - sg))


def _attn_head(q_b, k_b, v_b):
    s = _dot_nt(q_b, k_b) * (HD ** -0.5)
    e = jnp.exp(s - jnp.max(s, axis=-1, keepdims=True))
    prob = e / jnp.sum(e, axis=-1, keepdims=True)
    return prob, _dot(prob.astype(BF16), v_b)


def _rms_residual(x, o, g):
    r = lax.rsqrt(_rowmean(o * o) + EPS)
    return x + o * r * g


def _rms_post_bwd(dres, o, g):
    r = lax.rsqrt(_rowmean(o * o) + EPS)
    oh = o * r
    doh = dres * g
    return r * (doh - oh * _rowmean(doh * oh)), _colsum(dres * oh)


LANE = 128
_TILE_GROUPS = [sorted({LANE * j // GRP, (LANE * j + LANE - 1) // GRP}) for j in range(BW // LANE)]


def _tile(j):
    return slice(LANE * j, LANE * (j + 1))


def _low_lanes():
    return lax.broadcasted_iota(jnp.int32, (1, LANE), 1) < GRP - LANE


def _by_group(fn):
    tiles = []
    for j, groups in enumerate(_TILE_GROUPS):
        if len(groups) == 1:
            tiles.append(fn(groups[0], j))
        else:
            tiles.append(jnp.where(_low_lanes(), fn(groups[0], j), fn(groups[1], j)))
    return jnp.concatenate(tiles, axis=1)


def _sgu_chunk(vn_b, ws_ref, bmap_ref):
    return bmap_ref[...] + _by_group(lambda h, j: _dot(ws_ref[h], vn_b[:, _tile(j)]))


def _shift_copies(buf, sh):
    n = buf.shape[0] - 8
    for b in range(1, 8):
        sh[b - 1, pl.ds(0, n), :] = buf[pl.ds(b, n), :]


def _loop_rows(rows, step, fn, carry=0, unrolled=True):
    if unrolled:
        for r0 in range(0, rows, step):
            carry = fn(r0, carry)
        return carry

    def body(j, c):
        return fn(pl.multiple_of(j * step, step), c)
    return lax.fori_loop(0, rows // step, body, carry)


def _rows_at(buf, sh, r0, off):
    b = off % 8
    if b == 0 or sh is None:
        return buf[pl.ds(r0 + off, 32), :]
    return sh[b - 1, pl.ds(r0 + (off - b), 32), :]


def _tap_sum(buf, sh, w_ref, r0, taps, causal):
    acc = None
    for k in range(taps):
        off = HALO - (taps - 1 - k) if causal else taps - 1 - k
        term = w_ref[k:k + 1, :] * _rows_at(buf, sh, r0, off)
        acc = term if acc is None else acc + term
    return acc


def _fold8(a):
    return a[0:8] + a[8:16] + a[16:24] + a[24:32]


def _conv_back(buf, sh, w_ref, acc_ref, z, r0, taps):
    dz = None
    for k in range(taps):
        ahead = _rows_at(buf, sh, r0, taps - 1 - k)
        term = w_ref[k:k + 1, :] * ahead
        dz = term if dz is None else dz + term
        acc_ref[k * 8:(k + 1) * 8, :] += _fold8(z * ahead)
    return dz


def _halo_spec(n, nt, reverse, tm):
    per = tm // HALO
    if reverse:
        return pl.BlockSpec((HALO, n), lambda i: (jnp.maximum((nt - 1 - i) * per - 1, 0), 0))
    return pl.BlockSpec((HALO, n), lambda i: (jnp.maximum(i * per - 1, 0), 0))


def _even_fwd(x, p, lng, lnb, ws, bmap, wc, kv, wout, pg):
    t = x.shape[0]
    tm = min(TM_FWD_EVEN, t)
    nt = t // tm

    def body(x_ref, p_ref, ph_ref, lng_ref, lnb_ref, ws_ref, bmap_ref, wc_ref, kv_ref, wout_ref, pg_ref,
             o_ref, x1_ref, ybuf, cbuf):
        i = pl.program_id(0)
        vh, _ = _ln_stats(p_ref[:, E_V:E_V + BW])
        vn = vh * lng_ref[...] + lnb_ref[...]
        for c in range(tm // CHUNK):
            sl = slice(c * CHUNK, (c + 1) * CHUNK)
            sg = _sgu_chunk(vn[sl].astype(BF16), ws_ref, bmap_ref)
            gate, _ = _silu_parts(p_ref[sl, E_GATE:E_GATE + BW])
            ybuf[sl, 0:BW] = (p_ref[sl, E_U:E_U + BW] * sg * gate).astype(BF16)

        cbuf[0:HALO] = jnp.where(i > 0, ph_ref[:, E_CG:E_CG + BW] * ph_ref[:, E_XIN:E_XIN + BW], 0.0)
        cbuf[HALO:HALO + tm] = p_ref[:, E_CG:E_CG + BW] * p_ref[:, E_XIN:E_XIN + BW]
        for r0 in range(0, tm, 32):
            sl = slice(r0, r0 + 32)
            cv = _tap_sum(cbuf, None, wc_ref, r0, 3, True)
            gate, _ = _silu_parts(p_ref[sl, E_GATE + BW:E_GATE + 2 * BW])
            ybuf[sl, BW:2 * BW] = (p_ref[sl, E_BG:E_BG + BW] * cv * gate).astype(BF16)

        for h in range(NH):
            qs = slice(E_Q + h * HD, E_Q + (h + 1) * HD)
            _, yx = _attn_head(p_ref[:, qs].astype(BF16), kv_ref[:, h * HD:(h + 1) * HD],
                               kv_ref[:, XA + h * HD:XA + (h + 1) * HD])
            gs = slice(E_GATE + 2 * BW + h * HD, E_GATE + 2 * BW + (h + 1) * HD)
            gate, _ = _silu_parts(p_ref[:, gs])
            ybuf[:, 2 * BW + h * HD:2 * BW + (h + 1) * HD] = (yx * gate).astype(BF16)

        o = _dot(ybuf[...], wout_ref[...])
        o_ref[...] = o
        x1_ref[...] = _rms_residual(x_ref[...], o, pg_ref[...])

    tile = lambda n: pl.BlockSpec((tm, n), lambda i: (i, 0))
    return pl.pallas_call(
        body, name="even_fwd", grid=(nt,),
        in_specs=[tile(D), tile(EVEN_IN), _halo_spec(EVEN_IN, nt, False, tm)] + [_whole()] * 8,
        out_specs=[tile(D), tile(D)],
        out_shape=[S((t, D), F32), S((t, D), F32)],
        scratch_shapes=[pltpu.VMEM((tm, MIX), BF16), pltpu.VMEM((tm + HALO, BW), F32)],
        compiler_params=_params(("arbitrary",)),
    )(x, p, p, lng, lnb, ws, bmap, wc, kv, wout, pg)


def _even_bwd(dres, o, p, lng, lnb, ws, wst, bmap, wc, kv, wout, pg):
    t = dres.shape[0]
    tm = min(TM_BWD_EVEN, t)
    nt = t // tm

    def body(dres_ref, o_ref, p_ref, ph_ref, lng_ref, lnb_ref, ws_ref, wst_ref, bmap_ref, wc_ref, kv_ref, wout_ref,
             pg_ref, dp_ref, y_ref, do_ref, dpg_ref, dws_ref, dbs_ref, dlng_ref, dlnb_ref, dwc_ref, dkv_ref,
             dy, cbuf, gbuf, dconv, carry, dvn, dbmap, wacc):
        i = pl.program_id(0)
        ti = nt - 1 - i
        masks = _group_masks()

        @pl.when(i == 0)
        def _():
            for ref in (dpg_ref, dws_ref, dlng_ref, dlnb_ref, dkv_ref, dbmap, wacc):
                ref[...] = jnp.zeros_like(ref)

        do, dpg = _rms_post_bwd(dres_ref[...], o_ref[...], pg_ref[...])
        dpg_ref[...] += dpg
        do_b = do.astype(BF16)
        do_ref[...] = do_b
        dy[...] = _dot_nt(do_b, wout_ref[...])

        vh, rs = _ln_stats(p_ref[:, E_V:E_V + BW])
        vn = vh * lng_ref[...] + lnb_ref[...]
        for c in range(tm // CHUNK):
            sl = slice(c * CHUNK, (c + 1) * CHUNK)
            vn_b = vn[sl].astype(BF16)
            sg = _sgu_chunk(vn_b, ws_ref, bmap_ref)
            u = p_ref[sl, E_U:E_U + BW]
            gate, dgate = _silu_parts(p_ref[sl, E_GATE:E_GATE + BW])
            dyc = dy[sl, 0:BW]
            ya = u * sg
            y_ref[sl, 0:BW] = (ya * gate).astype(BF16)
            dp_ref[sl, E_GATE:E_GATE + BW] = (dyc * ya * dgate).astype(BF16)
            dya = dyc * gate
            dp_ref[sl, E_U:E_U + BW] = (dya * sg).astype(BF16)
            dsg = dya * u
            dbmap[...] += dsg
            dsg_b = dsg.astype(BF16)
            for h in range(NH):
                total = None
                for j, heads in enumerate(_TILE_GROUPS):
                    if h in heads:
                        d_t = dsg_b[:, _tile(j)]
                        if len(heads) == 2:
                            d_t = jnp.where(_low_lanes() == (h == heads[0]), d_t, jnp.zeros_like(d_t))
                        part = _dot_nt(d_t, vn_b[:, _tile(j)])
                        total = part if total is None else total + part
                dws_ref[h] += total
            dvn[sl, :] = _by_group(lambda h, j: _dot(wst_ref[h], dsg_b[:, _tile(j)]))
        dn = dvn[...]
        dlng_ref[...] += _colsum(dn * vh)
        dlnb_ref[...] += _colsum(dn)
        dp_ref[:, E_V:E_V + BW] = _ln_bwd(dn, vh, rs, lng_ref[...]).astype(BF16)

        cbuf[0:HALO] = jnp.where(ti > 0, ph_ref[:, E_CG:E_CG + BW] * ph_ref[:, E_XIN:E_XIN + BW], 0.0)
        cbuf[HALO:HALO + tm] = p_ref[:, E_CG:E_CG + BW] * p_ref[:, E_XIN:E_XIN + BW]
        for r0 in range(0, tm, 32):
            sl = slice(r0, r0 + 32)
            cv = _tap_sum(cbuf, None, wc_ref, r0, 3, True)
            gate, dgate = _silu_parts(p_ref[sl, E_GATE + BW:E_GATE + 2 * BW])
            bg = p_ref[sl, E_BG:E_BG + BW]
            dyc = dy[sl, BW:2 * BW]
            yb = bg * cv
            y_ref[sl, BW:2 * BW] = (yb * gate).astype(BF16)
            dp_ref[sl, E_GATE + BW:E_GATE + 2 * BW] = (dyc * yb * dgate).astype(BF16)
            dyb = dyc * gate
            dp_ref[sl, E_BG:E_BG + BW] = (dyb * cv).astype(BF16)
            dconv[sl, :] = dyb * bg
        gbuf[0:tm] = dconv[...]
        gbuf[tm:tm + HALO] = jnp.where(i > 0, carry[...], 0.0)
        carry[...] = dconv[0:HALO]
        for r0 in range(0, tm, 32):
            sl = slice(r0, r0 + 32)
            dc = _conv_back(gbuf, None, wc_ref, wacc, cbuf[HALO + r0:HALO + r0 + 32, :], r0, 3)
            dp_ref[sl, E_CG:E_CG + BW] = (dc * p_ref[sl, E_XIN:E_XIN + BW]).astype(BF16)
            dp_ref[sl, E_XIN:E_XIN + BW] = (dc * p_ref[sl, E_CG:E_CG + BW]).astype(BF16)

        for h in range(NH):
            qs = slice(E_Q + h * HD, E_Q + (h + 1) * HD)
            ks = slice(h * HD, (h + 1) * HD)
            vs = slice(XA + h * HD, XA + (h + 1) * HD)
            gs = slice(E_GATE + 2 * BW + h * HD, E_GATE + 2 * BW + (h + 1) * HD)
            ys = slice(2 * BW + h * HD, 2 * BW + (h + 1) * HD)
            q_b = p_ref[:, qs].astype(BF16)
            prob, yx = _attn_head(q_b, kv_ref[:, ks], kv_ref[:, vs])
            gate, dgate = _silu_parts(p_ref[:, gs])
            dyc = dy[:, ys]
            y_ref[:, ys] = (yx * gate).astype(BF16)
            dp_ref[:, gs] = (dyc * yx * dgate).astype(BF16)
            dyx_b = (dyc * gate).astype(BF16)
            dprob = _dot_nt(dyx_b, kv_ref[:, vs])
            dkv_ref[:, vs] += _dot_tn(prob.astype(BF16), dyx_b)
            ds_b = (prob * (dprob - jnp.sum(dprob * prob, axis=-1, keepdims=True)) * (HD ** -0.5)).astype(BF16)
            dp_ref[:, qs] = _dot(ds_b, kv_ref[:, ks]).astype(BF16)
            dkv_ref[:, ks] += _dot_tn(ds_b, q_b)

        @pl.when(i == nt - 1)
        def _():
            for h in range(NH):
                dbs_ref[:, h * HD:(h + 1) * HD] = jnp.broadcast_to(
                    jnp.sum(dbmap[...] * masks[h], axis=-1, keepdims=True), (CHUNK, HD))
            for k in range(3):
                dwc_ref[k:k + 1, :] = _colsum(wacc[k * 8:(k + 1) * 8, :])
            dwc_ref[3:8, :] = jnp.zeros((5, BW), F32)
            causal = (lax.broadcasted_iota(jnp.int32, (CHUNK, CHUNK), 0)
                      >= lax.broadcasted_iota(jnp.int32, (CHUNK, CHUNK), 1))
            for h in range(NH):
                dws_ref[h] = jnp.where(causal, dws_ref[h], 0.0)

    rtile = lambda n: pl.BlockSpec((tm, n), lambda i: (nt - 1 - i, 0))
    outs = [S((t, EVEN_IN), BF16), S((t, MIX), BF16), S((t, D), BF16), S((1, D), F32), S((NH, CHUNK, CHUNK), F32),
            S((CHUNK, NH * HD), F32), S((1, BW), F32), S((1, BW), F32), S((8, BW), F32), S((N_MEM, 2 * XA), F32)]
    return pl.pallas_call(
        body, name="even_bwd", grid=(nt,),
        in_specs=[rtile(D), rtile(D), rtile(EVEN_IN), _halo_spec(EVEN_IN, nt, True, tm)] + [_whole()] * 9,
        out_specs=[rtile(EVEN_IN), rtile(MIX), rtile(D)] + [_full(s.shape) for s in outs[3:]],
        out_shape=outs,
        scratch_shapes=[pltpu.VMEM((tm, MIX), F32), pltpu.VMEM((tm + HALO, BW), F32), pltpu.VMEM((tm + HALO, BW), F32),
                        pltpu.VMEM((tm, BW), F32), pltpu.VMEM((HALO, BW), F32), pltpu.VMEM((tm, BW), F32),
                        pltpu.VMEM((CHUNK, BW), F32), pltpu.VMEM((3 * 8, BW), F32)],
        compiler_params=_params(("arbitrary",)),
    )(dres, o, p, p, lng, lnb, ws, wst, bmap, wc, kv, wout, pg)


def _pool_causal_levels(za, zb, zc, zd, tm):
    n = tm + HALO
    zb[pl.ds(8, n - 8), :] = za[pl.ds(8, n - 8), :] + za[pl.ds(7, n - 8), :]
    zc[pl.ds(16, n - 16), :] = zb[pl.ds(16, n - 16), :] + zb[pl.ds(14, n - 16), :]
    zd[pl.ds(24, n - 24), :] = zc[pl.ds(24, n - 24), :] + zc[pl.ds(20, n - 24), :]


def _pool_causal(za, zb, zc, zd, tm):
    _pool_causal_levels(za, zb, zc, zd, tm)
    s16 = zd[pl.ds(HALO, tm), :] + zd[pl.ds(HALO - 8, tm), :]
    return zb[pl.ds(HALO, tm), :], zc[pl.ds(HALO, tm), :], zd[pl.ds(HALO, tm), :], s16


def _pool_anticausal_levels(ea, eb, ec, ed, tm):
    n = tm + HALO
    eb[pl.ds(0, n - 8), :] = ea[pl.ds(0, n - 8), :] + ea[pl.ds(1, n - 8), :]
    ec[pl.ds(0, n - 16), :] = eb[pl.ds(0, n - 16), :] + eb[pl.ds(2, n - 16), :]
    ed[pl.ds(0, n - 24), :] = ec[pl.ds(0, n - 24), :] + ec[pl.ds(4, n - 24), :]


def _pool_weights(t0, rows):
    tf = (t0 + lax.broadcasted_iota(jnp.int32, (rows, 1), 0) + 1).astype(F32)
    inv = [jnp.broadcast_to(1.0 / jnp.minimum(tf, float(win)), (rows, LANE)) for win in POOL_WINDOWS]
    return _by_group(lambda g, j: inv[g])


_HALVES = (slice(0, BW // 2), slice(BW // 2, BW))


def _mix4(parts):
    return _by_group(lambda g, j: parts[g][:, _tile(j)])


def _odd_fwd(x1, tgt, p, wbd, cscale, dww, dwb, lng, lnb, wpw, pwb, kv, wout, pg):
    t = x1.shape[0]
    tm = min(TM_FWD_ODD, t)
    nt = t // tm

    def body(x_ref, tgt_ref, p_ref, ph_ref, wbd_ref, cs_ref, dww_ref, dwb_ref, lng_ref, lnb_ref, wpw_ref, pwb_ref,
             kv_ref, wout_ref, pg_ref, o_ref, dres_ref, loss_ref, conv_ref, ybuf, za, zb, zc, zd, gbuf, lacc, gsh):
        i = pl.program_id(0)

        @pl.when(i == 0)
        def _():
            lacc[...] = jnp.zeros_like(lacc)

        z = p_ref[:, O_ZC:O_ZC + BW]
        za[0:HALO] = jnp.where(i > 0, ph_ref[:, O_ZC:O_ZC + BW], 0.0)
        za[HALO:HALO + tm] = z
        pooled = _mix4(_pool_causal(za, zb, zc, zd, tm)) * _pool_weights(i * tm, tm) - z
        pooled_b = pooled.astype(BF16)
        for hs in _HALVES:
            gate, _ = _silu_parts(p_ref[:, O_GATE + hs.start:O_GATE + hs.stop])
            ybuf[:, hs] = (_dot(pooled_b[:, hs], wbd_ref[hs, hs]) * cs_ref[:, hs] * gate).astype(BF16)

        gbuf[0:HALO] = jnp.where(i > 0, ph_ref[:, O_GA:O_GA + BW] * _sigmoid(ph_ref[:, O_GB:O_GB + BW]), 0.0)
        gbuf[HALO:HALO + tm] = p_ref[:, O_GA:O_GA + BW] * _sigmoid(p_ref[:, O_GB:O_GB + BW])
        _shift_copies(gbuf, gsh)
        def conv_rows(r0, carry):
            conv_ref[pl.ds(r0, 32), :] = _tap_sum(gbuf, gsh, dww_ref, r0, CONF_K, True) + dwb_ref[...]
            return carry

        _loop_rows(tm, 32, conv_rows)
        zh, _ = _ln_stats(conv_ref[...])
        zn = zh * lng_ref[...] + lnb_ref[...]
        yd = _dot((zn * _sigmoid(zn)).astype(BF16), wpw_ref[...]) + pwb_ref[...]
        gate, _ = _silu_parts(p_ref[:, O_GATE + BW:O_GATE + 2 * BW])
        ybuf[:, BW:2 * BW] = (yd * gate).astype(BF16)

        for h in range(NH):
            qs = slice(O_Q + h * HD, O_Q + (h + 1) * HD)
            _, yx = _attn_head(p_ref[:, qs].astype(BF16), kv_ref[:, h * HD:(h + 1) * HD],
                               kv_ref[:, XA + h * HD:XA + (h + 1) * HD])
            gs = slice(O_GATE + 2 * BW + h * HD, O_GATE + 2 * BW + (h + 1) * HD)
            gate, _ = _silu_parts(p_ref[:, gs])
            ybuf[:, 2 * BW + h * HD:2 * BW + (h + 1) * HD] = (yx * gate).astype(BF16)

        o = _dot(ybuf[...], wout_ref[...])
        o_ref[...] = o
        err = _rms_residual(x_ref[...], o, pg_ref[...]) - tgt_ref[...]
        lacc[...] += _colsum(err * err)
        dres_ref[...] = err * (1.0 / D)

        @pl.when(i == nt - 1)
        def _():
            loss_ref[...] = jnp.full((1, HD), jnp.sum(lacc[...]) * (0.5 / D), F32)

    tile = lambda n: pl.BlockSpec((tm, n), lambda i: (i, 0))
    ext = pltpu.VMEM((tm + HALO, BW), F32)
    return pl.pallas_call(
        body, name="odd_fwd", grid=(nt,),
        in_specs=[tile(D), tile(D), tile(ODD_IN), _halo_spec(ODD_IN, nt, False, tm)] + [_whole()] * 11,
        out_specs=[tile(D), tile(D), _full((1, HD)), tile(BW)],
        out_shape=[S((t, D), F32), S((t, D), F32), S((1, HD), F32), S((t, BW), F32)],
        scratch_shapes=[pltpu.VMEM((tm, MIX), BF16), ext, ext, ext, ext, ext,
                        pltpu.VMEM((1, D), F32), pltpu.VMEM((7, tm + HALO, BW), F32)],
        compiler_params=_params(("arbitrary",)),
    )(x1, tgt, p, p, wbd, cscale, dww, dwb, lng, lnb, wpw, pwb, kv, wout, pg)


def _odd_bwd(dres, o, p, conv, wbd, cscale, dww, dwb, lng, lnb, wpw, pwb, kv, wout, pg):
    t = dres.shape[0]
    tm = min(TM_BWD_ODD, t)
    nt = t // tm

    def body(dres_ref, o_ref, p_ref, ph_ref, conv_ref, wbd_ref, cs_ref, dww_ref, dwb_ref, lng_ref, lnb_ref, wpw_ref,
             pwb_ref, kv_ref, wout_ref, pg_ref, dp_ref, y_ref, do_ref, dpg_ref, dwbd_ref, dcs_ref, ddww_ref, ddwb_ref,
             dlng_ref, dlnb_ref, dwpw_ref, dpwb_ref, dkv_ref,
             dy, za, zb, zc, zd, carry_e, carry_d, wacc, shifted, t1, b1, b2):
        hbuf = zb
        i = pl.program_id(0)
        ti = nt - 1 - i

        @pl.when(i == 0)
        def _():
            for ref in (dpg_ref, dwbd_ref, dcs_ref, ddwb_ref, dlng_ref, dlnb_ref, dwpw_ref, dpwb_ref, dkv_ref, wacc):
                ref[...] = jnp.zeros_like(ref)

        def post_norm_rows(r0, acc):
            sl = pl.ds(r0, RB)
            ov, dv = o_ref[sl, :], dres_ref[sl, :]
            r = lax.rsqrt(_rowmean(ov * ov) + EPS)
            oh = ov * r
            doh = dv * pg_ref[...]
            do_ref[sl, :] = (r * (doh - oh * _rowmean(doh * oh))).astype(BF16)
            return acc + dv * oh

        dpg_ref[...] += _colsum(_loop_rows(tm, RB, post_norm_rows, jnp.zeros((RB, D), F32)))
        dy[...] = _dot_nt(do_ref[...], wout_ref[...])

        za[0:HALO] = jnp.where(ti > 0, ph_ref[:, O_ZC:O_ZC + BW], 0.0)
        za[HALO:HALO + tm] = p_ref[:, O_ZC:O_ZC + BW]
        _pool_causal_levels(za, zb, zc, zd, tm)

        def pooled_rows(r0, carry):
            sl = pl.ds(r0, RB)
            at = lambda ref, back=0: ref[pl.ds(HALO + r0 - back, RB), :]
            inv = _pool_weights(ti * tm + r0, RB)
            sums = (at(zb), at(zc), at(zd), at(zd) + at(zd, 8))
            b1[sl, :] = (_mix4(sums) * inv - p_ref[sl, O_ZC:O_ZC + BW]).astype(BF16)
            return carry

        _loop_rows(tm, RB, pooled_rows)
        for hs in _HALVES:
            t1[:, hs] = _dot(b1[:, hs], wbd_ref[hs, hs])

        def pool_gate_rows(r0, acc):
            sl = pl.ds(r0, RB)
            pm = t1[sl, :]
            gate, dgate = _silu_parts(p_ref[sl, O_GATE:O_GATE + BW])
            dyc = dy[sl, 0:BW]
            yc = pm * cs_ref[...]
            y_ref[sl, 0:BW] = (yc * gate).astype(BF16)
            dp_ref[sl, O_GATE:O_GATE + BW] = (dyc * yc * dgate).astype(BF16)
            dyc = dyc * gate
            b2[sl, :] = (dyc * cs_ref[...]).astype(BF16)
            return acc + dyc * pm

        dcs_ref[...] += _colsum(_loop_rows(tm, RB, pool_gate_rows, jnp.zeros((RB, BW), F32)))
        for hs in _HALVES:
            dwbd_ref[hs, hs] += _dot_tn(b1[:, hs], b2[:, hs])
            t1[:, hs] = _dot_nt(b2[:, hs], wbd_ref[hs, hs])

        def weighted_rows(r0, carry):
            sl = pl.ds(r0, RB)
            za[sl, :] = t1[sl, :] * _pool_weights(ti * tm + r0, RB)
            return carry

        _loop_rows(tm, RB, weighted_rows)
        za[tm:tm + HALO] = jnp.where(i > 0, carry_e[...], 0.0)
        carry_e[...] = za[0:HALO]
        _pool_anticausal_levels(za, zb, zc, zd, tm)

        def pool_back_rows(r0, carry):
            sl = pl.ds(r0, RB)
            ahead = lambda ref, fwd=0: ref[pl.ds(r0 + fwd, RB), :]
            sums = (ahead(zb), ahead(zc), ahead(zd), ahead(zd) + ahead(zd, 8))
            dp_ref[sl, O_ZC:O_ZC + BW] = (_mix4(sums) - t1[sl, :]).astype(BF16)
            return carry

        _loop_rows(tm, RB, pool_back_rows)

        def swish_rows(r0, carry):
            sl = pl.ds(r0, RB)
            zh, _ = _ln_stats(conv_ref[sl, :])
            zn = zh * lng_ref[...] + lnb_ref[...]
            b1[sl, :] = (zn * _sigmoid(zn)).astype(BF16)
            return carry

        _loop_rows(tm, RB, swish_rows)
        t1[...] = _dot(b1[...], wpw_ref[...])

        def conf_gate_rows(r0, acc):
            sl = pl.ds(r0, RB)
            yd = t1[sl, :] + pwb_ref[...]
            gate, dgate = _silu_parts(p_ref[sl, O_GATE + BW:O_GATE + 2 * BW])
            dyc = dy[sl, BW:2 * BW]
            y_ref[sl, BW:2 * BW] = (yd * gate).astype(BF16)
            dp_ref[sl, O_GATE + BW:O_GATE + 2 * BW] = (dyc * yd * dgate).astype(BF16)
            dyd = dyc * gate
            b2[sl, :] = dyd.astype(BF16)
            return acc + dyd

        dpwb_ref[...] += _colsum(_loop_rows(tm, RB, conf_gate_rows, jnp.zeros((RB, BW), F32)))
        dwpw_ref[...] += _dot_tn(b1[...], b2[...])
        t1[...] = _dot_nt(b2[...], wpw_ref[...])

        def norm_back_rows(r0, accs):
            sl = pl.ds(r0, RB)
            zh, rs = _ln_stats(conv_ref[sl, :])
            _, dsilu = _silu_parts(zh * lng_ref[...] + lnb_ref[...])
            dzn = t1[sl, :] * dsilu
            dzd = _ln_bwd(dzn, zh, rs, lng_ref[...])
            hbuf[sl, :] = dzd
            return accs[0] + dzn * zh, accs[1] + dzn, accs[2] + dzd

        zero = jnp.zeros((RB, BW), F32)
        acc_g, acc_b, acc_d = _loop_rows(tm, RB, norm_back_rows, (zero, zero, zero))
        dlng_ref[...] += _colsum(acc_g)
        dlnb_ref[...] += _colsum(acc_b)
        ddwb_ref[...] += _colsum(acc_d)
        hbuf[tm:tm + HALO] = jnp.where(i > 0, carry_d[...], 0.0)
        carry_d[...] = hbuf[0:HALO]
        _shift_copies(hbuf, shifted)

        def conv_back_rows(r0, carry):
            sl = pl.ds(r0, 32)
            sgb = _sigmoid(p_ref[sl, O_GB:O_GB + BW])
            ga = p_ref[sl, O_GA:O_GA + BW]
            dzg = _conv_back(hbuf, shifted, dww_ref, wacc, ga * sgb, r0, CONF_K)
            dp_ref[sl, O_GA:O_GA + BW] = (dzg * sgb).astype(BF16)
            dp_ref[sl, O_GB:O_GB + BW] = (dzg * ga * sgb * (1.0 - sgb)).astype(BF16)
            return carry

        _loop_rows(tm, 32, conv_back_rows, unrolled=False)

        for h in range(NH):
            qs = slice(O_Q + h * HD, O_Q + (h + 1) * HD)
            ks = slice(h * HD, (h + 1) * HD)
            vs = slice(XA + h * HD, XA + (h + 1) * HD)
            gs = slice(O_GATE + 2 * BW + h * HD, O_GATE + 2 * BW + (h + 1) * HD)
            ys = slice(2 * BW + h * HD, 2 * BW + (h + 1) * HD)
            q_b = p_ref[:, qs].astype(BF16)
            prob, yx = _attn_head(q_b, kv_ref[:, ks], kv_ref[:, vs])
            gate, dgate = _silu_parts(p_ref[:, gs])
            dyc = dy[:, ys]
            y_ref[:, ys] = (yx * gate).astype(BF16)
            dp_ref[:, gs] = (dyc * yx * dgate).astype(BF16)
            dyx_b = (dyc * gate).astype(BF16)
            dprob = _dot_nt(dyx_b, kv_ref[:, vs])
            dkv_ref[:, vs] += _dot_tn(prob.astype(BF16), dyx_b)
            ds_b = (prob * (dprob - jnp.sum(dprob * prob, axis=-1, keepdims=True)) * (HD ** -0.5)).astype(BF16)
            dp_ref[:, qs] = _dot(ds_b, kv_ref[:, ks]).astype(BF16)
            dkv_ref[:, ks] += _dot_tn(ds_b, q_b)

        @pl.when(i == nt - 1)
        def _():
            for k in range(CONF_K):
                ddww_ref[k:k + 1, :] = _colsum(wacc[k * 8:(k + 1) * 8, :])
            ddww_ref[CONF_K:CONF_K + 1, :] = jnp.zeros((1, BW), F32)

    rtile = lambda n: pl.BlockSpec((tm, n), lambda i: (nt - 1 - i, 0))
    outs = [S((t, ODD_IN), BF16), S((t, MIX), BF16), S((t, D), BF16), S((1, D), F32), S((BW, BW), F32),
            S((1, BW), F32), S((CONF_K + 1, BW), F32), S((1, BW), F32), S((1, BW), F32), S((1, BW), F32),
            S((BW, BW), F32), S((1, BW), F32), S((N_MEM, 2 * XA), F32)]
    ext = pltpu.VMEM((tm + HALO, BW), F32)
    return pl.pallas_call(
        body, name="odd_bwd", grid=(nt,),
        in_specs=[rtile(D), rtile(D), rtile(ODD_IN), _halo_spec(ODD_IN, nt, True, tm), rtile(BW)] + [_whole()] * 11,
        out_specs=[rtile(ODD_IN), rtile(MIX), rtile(D)] + [_full(s.shape) for s in outs[3:]],
        out_shape=outs,
        scratch_shapes=[pltpu.VMEM((tm, MIX), F32), ext, ext, ext, ext,
                        pltpu.VMEM((HALO, BW), F32), pltpu.VMEM((HALO, BW), F32), pltpu.VMEM((CONF_K * 8, BW), F32),
                        pltpu.VMEM((7, tm + HALO, BW), F32),
                        pltpu.VMEM((tm, BW), F32), pltpu.VMEM((tm, BW), BF16), pltpu.VMEM((tm, BW), BF16)],
        compiler_params=pltpu.CompilerParams(dimension_semantics=("arbitrary",),
                                             vmem_limit_bytes=VMEM_LIMIT_ODD_BWD_V7X),
    )(dres, o, p, p, conv, wbd, cscale, dww, dwb, lng, lnb, wpw, pwb, kv, wout, pg)


def _pick_rows(n):
    for rows in (1280, 2432, 1024, 768):
        if n % rows == 0:
            return rows
    return n


def _pad_rows(a, rows):
    return jnp.pad(a, ((0, rows - a.shape[0]), (0, 0)))


def _step(x, mem, tgt, ex):
    t = x.shape[0]
    tm = min(512, t)
    w, deps = ex.first()
    causal = jnp.tril(jnp.ones((CHUNK, CHUNK), bool))
    ws = jnp.where(causal[None], w["even_a_ws"], 0.0).astype(BF16)
    wst = jnp.transpose(ws, (0, 2, 1))
    bmap = jnp.repeat(w["even_a_bs"].T, GRP, axis=1)
    wc = _pad_rows(w["even_b_conv"], 8)
    wbd = jax.scipy.linalg.block_diag(*[w["odd_c_wgrp"][g] for g in range(NH)]).astype(BF16)
    dww = _pad_rows(w["odd_d_dw_w"], CONF_K + 1)
    tk = min(1024, t)
    zeros = jnp.zeros_like(mem)

    p_e, h_e = _rms_matmul(x, w["even_pre_g"], w["even_w_in"], tm=tm, name="in_even", transposed=True, deps=deps)
    w.update(ex.even_rest(h_e))
    kv_e, memn_e = _rms_matmul(mem, w["even_mem_g"], w["even_w_kv"], tm=N_MEM, name="kv_even", out_dtype=BF16)
    even_args = (w["even_a_ln_g"], w["even_a_ln_b"], ws)
    o_e, x1 = _even_fwd(x, p_e, *even_args, bmap, wc, kv_e, w["even_w_out"], w["even_post_g"])
    w.update(ex.odd(o_e))
    kv_o, memn_o = _rms_matmul(mem, w["odd_mem_g"], w["odd_w_kv"], tm=N_MEM, name="kv_odd", out_dtype=BF16)
    p_o, h_o = _rms_matmul(x1, w["odd_pre_g"], w["odd_w_in"], tm=tm, name="in_odd", transposed=True)
    odd_args = (wbd, w["odd_c_scale"], dww, w["odd_d_dw_b"], w["odd_d_ln_g"], w["odd_d_ln_b"], w["odd_d_pw_w"],
                w["odd_d_pw_b"], kv_o, w["odd_w_out"], w["odd_post_g"])
    o_o, dres, loss, conv_o = _odd_fwd(x1, tgt, p_o, *odd_args)

    g = {}
    (dp_o, y_o, do_o, post_g_o, dwbd, g["odd_c_scale"], ddww, g["odd_d_dw_b"], g["odd_d_ln_g"], g["odd_d_ln_b"],
     dwpw, g["odd_d_pw_b"], dkv_o) = _odd_bwd(dres, o_o, p_o, conv_o, *odd_args)
    g["odd_post_g"] = post_g_o
    g["odd_d_dw_w"] = ddww[:CONF_K]
    dkv_o = dkv_o.astype(BF16)
    deps = ex.send("odd", {
        "odd_w_in": _tn_matmul(dp_o, h_o, tmc=_pick_rows(ODD_IN), tk=tk, out_dtype=BF16, name="dw_in_odd"),
        "odd_w_out": _tn_matmul(y_o, do_o, tmc=MIX, tk=tk, out_dtype=BF16, name="dw_out_odd"),
        "odd_w_kv": _tn_matmul(memn_o, dkv_o, tmc=D, tk=N_MEM, out_dtype=BF16, name="dw_kv_odd"),
        "odd_d_pw_w": dwpw.astype(BF16), "loss": loss,
        "odd_c_wgrp": jnp.concatenate([dwbd[i * GRP:(i + 1) * GRP, i * GRP:(i + 1) * GRP] for i in range(NH)])})
    dx1, g["odd_pre_g"] = _nt_matmul_rms_bwd(dp_o, w["odd_w_in"], x1, w["odd_pre_g"], dres, tm=tm,
                                             name="dx_odd", transposed=True, deps=deps)
    _, g["odd_mem_g"] = _nt_matmul_rms_bwd(dkv_o, w["odd_w_kv"], mem, w["odd_mem_g"], zeros, tm=N_MEM,
                                           name="dmem_odd")

    (dp_e, y_e, do_e, post_g_e, dws, dbs, ln_g_e, ln_b_e, dwc, dkv_e) = _even_bwd(
        dx1, o_e, p_e, *even_args, wst, bmap, wc, kv_e, w["even_w_out"], w["even_post_g"])
    g["even_b_conv"] = dwc[:3]
    dkv_e = dkv_e.astype(BF16)
    deps = ex.send("even_rest", {
        "even_w_out": _tn_matmul(y_e, do_e, tmc=MIX, tk=tk, out_dtype=BF16, name="dw_out_even"),
        "even_w_kv": _tn_matmul(memn_e, dkv_e, tmc=D, tk=N_MEM, out_dtype=BF16, name="dw_kv_even"),
        "even_a_ln_g": ln_g_e, "even_a_ln_b": ln_b_e,
        "even_a_ws": dws.reshape(NH * CHUNK, CHUNK), "even_a_bs": dbs[:, ::HD].T})
    g["even_w_in"] = _tn_matmul(dp_e, h_e, tmc=_pick_rows(EVEN_IN), tk=tk, out_dtype=BF16, name="dw_in_even",
                                deps=deps)
    deps = ex.send("even_in", g)
    grad_x, pre_g_e = _nt_matmul_rms_bwd(dp_e, w["even_w_in"], x, w["even_pre_g"], dx1, tm=tm,
                                         name="dx_even", transposed=True, deps=deps)
    _, mem_g_e = _nt_matmul_rms_bwd(dkv_e, w["even_w_kv"], mem, w["even_mem_g"], zeros, tm=N_MEM, name="dmem_even",
                                    deps=(grad_x,))
    deps = ex.send("even_gains", {"even_pre_g": pre_g_e, "even_mem_g": mem_g_e, "even_post_g": post_g_e})
    return grad_x, deps


def _place():
    return lax.axis_index("x"), lax.axis_index("y"), lax.axis_index("c")


def _index(px, py, pc):
    return 4 * px + 2 * py + pc


_COPIES = N_DEV - 1


def _all_gather(arrs, name):
    n = len(arrs)

    def body(*refs):
        ins, outs = refs[:n], refs[n:2 * n]
        send_sems, recv_sems, local_sems = refs[2 * n:]
        x, y, c = _place()
        me, sibling = (x, y, c), (x, y, 1 - c)
        chips = [(1 - x, y), (x, 1 - y), (1 - x, 1 - y)]

        def copy(a, k, block, to, src=None):
            dst = outs[a].at[_index(*block)]
            return pltpu.make_async_remote_copy(
                src_ref=dst if src is None else src, dst_ref=dst, send_sem=send_sems.at[a * _COPIES + k],
                recv_sem=recv_sems.at[a * _COPIES + k], device_id=to, device_id_type=MESH)

        mine = [pltpu.make_async_copy(ins[a], outs[a].at[_index(*me)], local_sems.at[a]) for a in range(n)]
        first = []
        for a in range(n):
            mine[a].start()
            first.append(copy(a, 0, me, sibling, src=ins[a]))
            first += [copy(a, 1 + j, me, (*chip, c), src=ins[a]) for j, chip in enumerate(chips)]
        for cp in first:
            cp.start()
        passed = []
        for j, chip in enumerate(chips):
            for a in range(n):
                copy(a, 1 + j, (*chip, c), me).wait_recv()
                passed.append(copy(a, 4 + j, (*chip, c), sibling))
                passed[-1].start()
        for a in range(n):
            copy(a, 0, sibling, me).wait_recv()
            for j, chip in enumerate(chips):
                copy(a, 4 + j, (*chip, 1 - c), me).wait_recv()
        for cp in first + passed:
            cp.wait_send()
        for cp in mine:
            cp.wait()

    return pl.pallas_call(
        body, name=name, in_specs=[_ANY] * n, out_specs=[_ANY] * n,
        out_shape=[S((N_DEV,) + a.shape, a.dtype) for a in arrs],
        scratch_shapes=[pltpu.SemaphoreType.DMA((n * _COPIES,)), pltpu.SemaphoreType.DMA((n * _COPIES,)),
                        pltpu.SemaphoreType.DMA((n,))],
    )(*arrs)


_HBM = pl.BlockSpec(memory_space=pltpu.HBM)
_SEM = pl.BlockSpec(memory_space=pltpu.SEMAPHORE)
_EFFECT = pltpu.SideEffectType.DATAFLOW_SIDE_EFFECTING


_ALL_FLIPS = [(k >> 2 & 1, k >> 1 & 1, k & 1) for k in range(1, N_DEV)]
_CHIP_FLIPS = [(1, 0, 0), (0, 1, 0), (1, 1, 0)]
_FLIPS = {"gather": _ALL_FLIPS, "scatter": _ALL_FLIPS, "gather_chips": [(0, 0, 1)] + _CHIP_FLIPS,
          "scatter_chips": _CHIP_FLIPS}


def _landing_shape(kind, a):
    return (N_DEV,) + a.shape if kind.startswith("gather") else a.shape


def _exchange_copies(kinds, srcs, lands, send_sems, recv_sems, local_sems, arriving):
    x, y, c = _place()
    mine = _index(x, y, c)
    remote, local = [], []
    for a, kind in enumerate(kinds):
        by_chip = kind == "scatter_chips"
        here = 2 * x + y if by_chip else mine
        own = srcs[a] if kind.startswith("gather") else srcs[a].at[here]
        local.append(pltpu.make_async_copy(own, lands[a].at[here], local_sems.at[a]))
        for k, (fx, fy, fc) in enumerate(_FLIPS[kind]):
            peer = (1 - x if fx else x, 1 - y if fy else y, 1 - c if fc else c)
            there = 2 * peer[0] + peer[1] if by_chip else _index(*peer)
            remote.append(pltpu.make_async_remote_copy(
                src_ref=srcs[a] if kind.startswith("gather") else srcs[a].at[there],
                dst_ref=lands[a].at[there if arriving else here],
                send_sem=send_sems.at[a * _COPIES + k], recv_sem=recv_sems.at[a * _COPIES + k],
                device_id=peer, device_id_type=MESH))
    return remote, local


def _exchange_start(items, name, deps=()):
    kinds = [kind for kind, _ in items]
    srcs = [a for _, a in items]
    n = len(items)
    lands = [lax.empty(_landing_shape(kind, a), a.dtype) for kind, a in items]

    def body(*refs):
        send_sems, recv_sems, local_sems = refs[2 * n + len(deps):2 * n + len(deps) + 3]
        remote, local = _exchange_copies(kinds, refs[:n], refs[n:2 * n], send_sems, recv_sems, local_sems, False)
        for cp in local + remote:
            cp.start()
        refs[-1][...] = jnp.zeros_like(refs[-1])

    held = [pltpu.HBM(a.shape, a.dtype) for a in srcs + lands]
    res = pl.pallas_call(
        body, name=name,
        out_shape=(pltpu.SemaphoreType.DMA((n * _COPIES,)), pltpu.SemaphoreType.DMA((n * _COPIES,)),
                   pltpu.SemaphoreType.DMA((n,)), *held, S((8, 128), F32)),
        in_specs=[_HBM] * (2 * n) + [_ANY] * len(deps),
        out_specs=(_SEM, _SEM, _SEM, *[_HBM] * (2 * n), _whole()),
        input_output_aliases={i: 3 + i for i in range(2 * n)},
        compiler_params=pltpu.CompilerParams(has_side_effects=_EFFECT),
    )(*[pltpu.with_memory_space_constraint(a, pltpu.HBM) for a in srcs + lands], *deps)
    return (kinds, res[:3], res[3:3 + 2 * n]), res[-1]


def _exchange_wait(handle, after, name):
    kinds, sems, held = handle
    n = len(kinds)

    def body(*refs):
        send_sems, recv_sems, local_sems = refs[2 * n:2 * n + 3]
        remote, local = _exchange_copies(kinds, refs[:n], refs[n:2 * n], send_sems, recv_sems, local_sems, True)
        for cp in remote:
            cp.wait_send()
            cp.wait_recv()
        for cp in local:
            cp.wait()

    res = pl.pallas_call(
        body, name=name, out_shape=[pltpu.HBM(a.shape, a.dtype) for a in held],
        in_specs=[_HBM] * (2 * n) + [_SEM] * 3 + [_ANY] * len(after), out_specs=[_HBM] * (2 * n),
        input_output_aliases={i: i for i in range(2 * n)},
        compiler_params=pltpu.CompilerParams(has_side_effects=_EFFECT),
    )(*held, *sems, *after)
    return res[n:]


_CHIPS = [(0, 0), (0, 1), (1, 0), (1, 1)]
_N_CHIPS = len(_CHIPS)


def _sibling_forward(lands, name):
    n = len(lands)

    def body(*refs):
        ins, outs = refs[:n], refs[n:2 * n]
        send_sems, recv_sems = refs[2 * n:]
        x, y, c = _place()
        sent, arriving = [], []
        for a in range(n):
            for j, (fx, fy, _) in enumerate(_CHIP_FLIPS):
                chip = (1 - x if fx else x, 1 - y if fy else y)
                sems = dict(send_sem=send_sems.at[a * 3 + j], recv_sem=recv_sems.at[a * 3 + j],
                            device_id=(x, y, 1 - c), device_id_type=MESH)
                mine, theirs = _index(*chip, c), _index(*chip, 1 - c)
                sent.append(pltpu.make_async_remote_copy(src_ref=ins[a].at[mine], dst_ref=outs[a].at[mine], **sems))
                arriving.append(pltpu.make_async_remote_copy(src_ref=ins[a].at[theirs], dst_ref=outs[a].at[theirs],
                                                             **sems))
        for cp in sent:
            cp.start()
        for cp in sent:
            cp.wait_send()
        for cp in arriving:
            cp.wait_recv()

    return pl.pallas_call(
        body, name=name, in_specs=[_ANY] * n, out_specs=[_ANY] * n,
        out_shape=[S(a.shape, a.dtype) for a in lands], input_output_aliases={a: a for a in range(n)},
        scratch_shapes=[pltpu.SemaphoreType.DMA((3 * n,)), pltpu.SemaphoreType.DMA((3 * n,))],
    )(*lands)


def _sibling_swap(arrs, name):
    n = len(arrs)

    def body(*refs):
        ins, outs = refs[:n], refs[n:2 * n]
        send_sems, recv_sems = refs[2 * n:]
        x, y, c = _place()
        copies = []
        for a in range(n):
            for q, chip in enumerate(_CHIPS):
                copies.append(pltpu.make_async_remote_copy(
                    src_ref=ins[a].at[_index(*chip, 1 - c)], dst_ref=outs[a].at[q],
                    send_sem=send_sems.at[a * _N_CHIPS + q], recv_sem=recv_sems.at[a * _N_CHIPS + q],
                    device_id=(x, y, 1 - c), device_id_type=MESH))
        for cp in copies:
            cp.start()
        for cp in copies:
            cp.wait_send()
            cp.wait_recv()

    return pl.pallas_call(
        body, name=name, in_specs=[_ANY] * n, out_specs=[_ANY] * n,
        out_shape=[S((_N_CHIPS,) + a.shape[1:], a.dtype) for a in arrs],
        scratch_shapes=[pltpu.SemaphoreType.DMA((_N_CHIPS * n,)), pltpu.SemaphoreType.DMA((_N_CHIPS * n,))],
    )(*arrs)


def _add_partials(mine, theirs, *, tr, name):
    _, r, c = mine.shape

    def body(mine_ref, theirs_ref, out_ref):
        core = lax.axis_index("c")
        own = jnp.where(core == 0, mine_ref[0].astype(F32), mine_ref[1].astype(F32))
        out_ref[0] = (own + theirs_ref[0].astype(F32)).astype(out_ref.dtype)

    return pl.pallas_call(
        body, name=name, grid=(_N_CHIPS, r // tr),
        in_specs=[pl.BlockSpec((2, tr, c), lambda q, i: (q, i, 0)), pl.BlockSpec((1, tr, c), lambda q, i: (q, i, 0))],
        out_specs=pl.BlockSpec((1, tr, c), lambda q, i: (q, i, 0)),
        out_shape=S((_N_CHIPS, r, c), mine.dtype),
        compiler_params=_params(("arbitrary", "arbitrary")),
    )(mine, theirs)


def _adamw(w, g, m, v):
    m = ADAM_B1 * m + (1.0 - ADAM_B1) * g
    v = ADAM_B2 * v + (1.0 - ADAM_B2) * (g * g)
    m_hat = m / (1.0 - ADAM_B1 ** ADAM_STEP)
    v_hat = v / (1.0 - ADAM_B2 ** ADAM_STEP)
    return -ADAM_LR * (m_hat / (jnp.sqrt(v_hat) + ADAM_EPS) + ADAM_WD * w), m, v


def _sum_devices(ref, rows):
    total = ref[0, rows, :].astype(F32)
    for s in range(1, ref.shape[0]):
        total = total + ref[s, rows, :].astype(F32)
    return total


def _adam_big(recv, w, m, v, *, tr, name):
    r, c = w.shape

    def body(recv_ref, w_ref, m_ref, v_ref, g_ref, d_ref, m2_ref, v2_ref):
        g = _sum_devices(recv_ref, slice(None))
        g_ref[...] = g
        d_ref[...], m2_ref[...], v2_ref[...] = _adamw(w_ref[...], g, m_ref[...], v_ref[...])

    blk = pl.BlockSpec((tr, c), lambda i: (i, 0))
    return pl.pallas_call(
        body, name=name, grid=(r // tr,),
        in_specs=[pl.BlockSpec((recv.shape[0], tr, c), lambda i: (0, i, 0)), blk, blk, blk],
        out_specs=[blk] * 4, out_shape=[S((r, c), F32)] * 4,
        compiler_params=_params(("arbitrary",)),
    )(recv, w, m, v)


_REPLICATED = {"even_pre_g": (0, 0, 1), "even_mem_g": (0, 8, 1), "even_post_g": (0, 16, 1),
               "even_a_ln_g": (1, 0, 1), "even_a_ln_b": (1, 8, 1),
               "even_a_ws": (2, 0, NH * CHUNK), "even_a_bs": (2, NH * CHUNK, NH),
               "odd_c_wgrp": (3, 0, NH * GRP)}
_SHARDED = {"odd_pre_g": (4, 0, 1), "odd_mem_g": (4, 8, 1), "odd_post_g": (4, 16, 1),
            "even_b_conv": (5, 0, 3), "odd_c_scale": (5, 8, 1), "odd_d_dw_w": (5, 16, CONF_K),
            "odd_d_dw_b": (5, 48, 1), "odd_d_ln_g": (5, 56, 1), "odd_d_ln_b": (5, 64, 1), "odd_d_pw_b": (5, 72, 1)}
_SMALL = {**_REPLICATED, **_SHARDED}
_SMALL_ROWS = {0: 24, 1: 16, 2: NH * CHUNK + 8, 3: NH * GRP, 4: 24, 5: 80}


def _adam_small(sources, wmv):
    names = list(_SMALL)
    ns = len(sources)

    def body(*refs):
        src = refs[:ns]
        ins = refs[ns:ns + 3 * len(names)]
        outs = refs[ns + 3 * len(names):]
        outs[-1][...] = _sum_devices(src[-1], slice(0, 1))
        for i, nm in enumerate(names):
            a, row0, rows = _SMALL[nm]
            g = _sum_devices(src[a], slice(row0, row0 + rows))
            w_ref, m_ref, v_ref = ins[3 * i:3 * i + 3]
            g_ref, d_ref, m2_ref, v2_ref = outs[4 * i:4 * i + 4]
            g_ref[...] = g
            d_ref[...], m2_ref[...], v2_ref[...] = _adamw(w_ref[...], g, m_ref[...], v_ref[...])

    flat = [t for nm in names for t in wmv[nm]]
    out_shape = [S(wmv[nm][0].shape, F32) for nm in names for _ in range(4)] + [S((1, HD), F32)]
    res = pl.pallas_call(
        body, name="adam_small", in_specs=[_whole()] * (ns + len(flat)), out_specs=[_whole()] * len(out_shape),
        out_shape=out_shape, compiler_params=_params(),
    )(*sources, *flat)
    return {nm: tuple(res[4 * i:4 * i + 4]) for i, nm in enumerate(names)}, res[-1]


_WEIGHTS = ["even_pre_g", "even_w_in", "even_a_ln_g", "even_a_ln_b", "even_a_ws", "even_a_bs", "even_b_conv",
            "even_mem_g", "even_w_kv", "even_w_out", "even_post_g", "odd_pre_g", "odd_w_in", "odd_c_wgrp",
            "odd_c_scale", "odd_d_dw_w", "odd_d_dw_b", "odd_d_ln_g", "odd_d_ln_b", "odd_d_pw_w", "odd_d_pw_b",
            "odd_mem_g", "odd_w_kv", "odd_w_out", "odd_post_g"]
_TRANSPOSED = ["even_w_in", "odd_w_in"]
_BIG = _TRANSPOSED + ["even_w_kv", "even_w_out", "odd_w_kv", "odd_w_out", "odd_d_pw_w"]
_BIG_TILE_ROWS = {"even_w_in": 400, "odd_w_in": 304, "even_w_kv": 128, "even_w_out": 128, "odd_w_kv": 128,
                  "odd_w_out": 128, "odd_d_pw_w": 96}


def _view2d(a, transposed):
    a = a[0]
    if a.ndim == 1:
        return a[None]
    if transposed:
        return a.T
    return a.reshape(-1, a.shape[-1])


def _rows8(a):
    return _pad_rows(a, -(-a.shape[0] // 8) * 8)


def _pack_rows(parts):
    return jnp.concatenate([_rows8(p) for p in parts], axis=0)


def _unshard_cols(a):
    return jnp.transpose(a, (1, 0, 2)).reshape(a.shape[1], N_DEV * a.shape[2])


def _shard_cols(a):
    return jnp.transpose(a.reshape(a.shape[0], N_DEV, a.shape[1] // N_DEV), (1, 0, 2))


def _rows_of(a):
    return a.reshape(-1, a.shape[-1])


_GROUPS = {"odd": (["odd_w_in", "odd_w_out", "odd_w_kv", "odd_d_pw_w"], [3], []),
           "even_rest": (["even_w_out", "even_w_kv"], [1, 2], []),
           "even_in": (["even_w_in"], [], [4, 5]),
           "even_gains": ([], [0], [])}


_TWO_LEVEL = ("even_in",)


class _MeshExchange:
    def __init__(self, shard):
        self.shard = shard
        self.handles = {}

    def first(self):
        shard = self.shard
        packs = [_pack_rows([shard[nm] for nm in _SHARDED if _SHARDED[nm][0] == a]) for a in (4, 5)]
        w_in, p128, p96 = _all_gather([shard["even_w_in"].astype(BF16)] + packs, "gather_first")
        w = {nm: shard[nm] for nm in _REPLICATED}
        w["even_a_ws"] = w["even_a_ws"].reshape(NH, CHUNK, CHUNK)
        w["odd_c_wgrp"] = w["odd_c_wgrp"].reshape(NH, GRP, GRP)
        w["even_w_in"] = _rows_of(w_in)
        full_packs = {4: _unshard_cols(p128), 5: _unshard_cols(p96)}
        for nm, (a, row0, rows) in _SHARDED.items():
            w[nm] = full_packs[a][row0:row0 + rows]
        later = lambda names: [("gather_chips", shard[nm].astype(BF16)) for nm in names]
        self.handles["w_even"], token = _exchange_start(later(["even_w_kv", "even_w_out"]), "gather_even_start",
                                                        deps=(w_in,))
        self.handles["w_odd"], token = _exchange_start(later(["odd_w_in", "odd_w_kv", "odd_w_out", "odd_d_pw_w"]),
                                                       "gather_odd_start", deps=(token,))
        return w, (token,)

    def even_rest(self, after):
        landed = _exchange_wait(self.handles.pop("w_even"), (after,), "gather_even_wait")
        kv, out = _sibling_forward(landed, "forward_even")
        return {"even_w_kv": _rows_of(kv), "even_w_out": _rows_of(out)}

    def odd(self, after):
        landed = _exchange_wait(self.handles.pop("w_odd"), (after,), "gather_odd_wait")
        w_in, kv, out, pw = _sibling_forward(landed, "forward_odd")
        return {"odd_w_in": _rows_of(w_in), "odd_w_kv": _rows_of(kv), "odd_w_out": _rows_of(out),
                "odd_d_pw_w": _rows_of(pw)}

    def send(self, group, g):
        big, replicated, sharded = _GROUPS[group]
        by_owner = [g[nm].reshape(N_DEV, -1, g[nm].shape[-1]) for nm in big]
        if group in _TWO_LEVEL:
            theirs = _sibling_swap(by_owner, "swap_" + group)
            items = [("scatter_chips", _add_partials(a, b, tr=_BIG_TILE_ROWS[nm], name="chip_sum_" + nm))
                     for nm, a, b in zip(big, by_owner, theirs)]
        else:
            items = [("scatter", a) for a in by_owner]
        items += [("gather", _pack_rows([g[nm] for nm in _REPLICATED if _REPLICATED[nm][0] == a]))
                  for a in replicated]
        items += [("scatter", _shard_cols(_pack_rows([g[nm] for nm in _SHARDED if _SHARDED[nm][0] == a])))
                  for a in sharded]
        if group == "odd":
            items.append(("gather", _rows8(g["loss"])))
        self.handles[group], token = _exchange_start(items, "send_" + group + "_start")
        return (token,)

    def receive(self, group, after):
        after = after if isinstance(after, tuple) else (after,)
        return _exchange_wait(self.handles.pop(group), after, "send_" + group + "_wait")


def kernel(x, mem, even_pre_g, even_w_in, even_a_ln_g, even_a_ln_b, even_a_ws, even_a_bs, even_b_conv, even_mem_g, even_w_kv, even_w_out, even_post_g, odd_pre_g, odd_w_in, odd_c_wgrp, odd_c_scale, odd_d_dw_w, odd_d_dw_b, odd_d_ln_g, odd_d_ln_b, odd_d_pw_w, odd_d_pw_b, odd_mem_g, odd_w_kv, odd_w_out, odd_post_g, loss_target, m_even_pre_g, m_even_w_in, m_even_a_ln_g, m_even_a_ln_b, m_even_a_ws, m_even_a_bs, m_even_b_conv, m_even_mem_g, m_even_w_kv, m_even_w_out, m_even_post_g, m_odd_pre_g, m_odd_w_in, m_odd_c_wgrp, m_odd_c_scale, m_odd_d_dw_w, m_odd_d_dw_b, m_odd_d_ln_g, m_odd_d_ln_b, m_odd_d_pw_w, m_odd_d_pw_b, m_odd_mem_g, m_odd_w_kv, m_odd_w_out, m_odd_post_g, v_even_pre_g, v_even_w_in, v_even_a_ln_g, v_even_a_ln_b, v_even_a_ws, v_even_a_bs, v_even_b_conv, v_even_mem_g, v_even_w_kv, v_even_w_out, v_even_post_g, v_odd_pre_g, v_odd_w_in, v_odd_c_wgrp, v_odd_c_scale, v_odd_d_dw_w, v_odd_d_dw_b, v_odd_d_ln_g, v_odd_d_ln_b, v_odd_d_pw_w, v_odd_d_pw_b, v_odd_mem_g, v_odd_w_kv, v_odd_w_out, v_odd_post_g):
    given = dict(locals())
    view = lambda nm, kind: _view2d(given[kind + nm], nm in _TRANSPOSED)
    shard = {nm: view(nm, "") for nm in _WEIGHTS}
    wmv = {nm: (shard[nm], view(nm, "m_"), view(nm, "v_")) for nm in _WEIGHTS}

    ex = _MeshExchange(shard)
    grad_x, last = _step(x[0], mem[0], loss_target[0], ex)

    res = {}

    def update(group, after):
        names = _GROUPS[group][0]
        landed = ex.receive(group, after)
        for nm, recv in zip(names, landed):
            res[nm] = _adam_big(recv, *wmv[nm], tr=_BIG_TILE_ROWS[nm], name="adam_" + nm)
        return landed[len(names):]

    c192, losses = update("odd", last)
    c768, c128 = update("even_rest", res["odd_d_pw_w"][0])
    a128, a96 = update("even_in", res["even_w_kv"][0])
    (c1024,) = update("even_gains", res["even_w_in"][0])
    small, loss = _adam_small([c1024, c768, c128, c192, a128, a96, losses], {nm: wmv[nm] for nm in _SMALL})
    res.update(small)
    total = loss[0, 0]
    back = lambda nm, a: (a.T if nm in _TRANSPOSED else a).reshape(given[nm].shape)
    outs = [[back(nm, res[nm][i]) for nm in _WEIGHTS] for i in range(4)]
    return (total, grad_x[None], *outs[0], *outs[1], *outs[2], *outs[3])
```

```python
import functools

import jax
import jax.numpy as jnp
from jax import lax
from jax.experimental import pallas as pl
from jax.experimental.pallas import tpu as pltpu

F32 = jnp.float32
BF16 = jnp.bfloat16
S = jax.ShapeDtypeStruct
MESH = pl.DeviceIdType.MESH
AXES = ("x", "y", "c")
N_DEV = 8

D = 1024
BW = 768
XA = 512
HD = 128
NH = 4
MIX = 2048
CHUNK = 128
GRP = 192
N_MEM = 256
CONF_K = 31
EPS = 1e-6
HALO = 32
POOL_WINDOWS = (2, 4, 8, 16)
TM_FWD_EVEN = 512
TM_FWD_ODD = 256
TM_BWD_EVEN = 256
TM_BWD_ODD = 256
RB = 16

E_U, E_V, E_BG, E_CG, E_XIN, E_Q, E_GATE = 0, 768, 1536, 2304, 3072, 3840, 4352
EVEN_IN = 6400
O_ZC, O_GA, O_GB, O_Q, O_GATE = 0, 768, 1536, 2304, 2816
ODD_IN = 4864

ADAM_LR, ADAM_B1, ADAM_B2, ADAM_EPS, ADAM_WD, ADAM_STEP = 0.001, 0.9, 0.999, 1e-08, 0.01, 10

VMEM_LIMIT_V7X = 56 * 1024 * 1024
VMEM_LIMIT_ODD_BWD_V7X = 62 * 1024 * 1024


def _params(sem=None):
    return pltpu.CompilerParams(dimension_semantics=sem, vmem_limit_bytes=VMEM_LIMIT_V7X)


def _dot(a, b):
    return jnp.dot(a, b, preferred_element_type=F32)


def _dot_nt(a, b):
    return lax.dot_general(a, b, (((1,), (1,)), ((), ())), preferred_element_type=F32)


def _dot_tn(a, b):
    return lax.dot_general(a, b, (((0,), (0,)), ((), ())), preferred_element_type=F32)


def _sigmoid(z):
    return 1.0 / (1.0 + jnp.exp(-z))


def _rowmean(a):
    return jnp.mean(a, axis=-1, keepdims=True)


def _colsum(a):
    return jnp.sum(a, axis=0, keepdims=True)


def _ln_stats(v):
    mu = _rowmean(v)
    vc = v - mu
    rs = lax.rsqrt(_rowmean(vc * vc) + EPS)
    return vc * rs, rs


def _ln_bwd(dn, vh, rs, g):
    dvh = dn * g
    return rs * (dvh - _rowmean(dvh) - vh * _rowmean(dvh * vh))


def _group_masks():
    col = lax.broadcasted_iota(jnp.int32, (1, BW), 1)
    return [((col >= GRP * h) & (col < GRP * (h + 1))).astype(F32) for h in range(NH)]


def _full(shape):
    nd = len(shape)
    return pl.BlockSpec(shape, lambda *_: (0,) * nd)


def _whole():
    return pl.BlockSpec(memory_space=pltpu.VMEM)


_ANY = pl.BlockSpec(memory_space=pl.ANY)


def _after(body, n_in, deps):
    def ordered(*refs):
        return body(*refs[:n_in], *refs[n_in + len(deps):])
    return ordered


def _rms_matmul(x, g, w, *, tm, name, transposed=False, out_dtype=F32, deps=()):
    t, d = x.shape
    n = w.shape[0] if transposed else w.shape[1]

    def body(x_ref, g_ref, w_ref, p_ref, h_ref):
        xv = x_ref[...]
        r = lax.rsqrt(_rowmean(xv * xv) + EPS)
        h = (xv * r * g_ref[...]).astype(BF16)
        h_ref[...] = h
        p_ref[...] = (_dot_nt(h, w_ref[...]) if transposed else _dot(h, w_ref[...])).astype(out_dtype)

    return pl.pallas_call(
        _after(body, 3, deps), name=name, grid=(t // tm,),
        in_specs=[pl.BlockSpec((tm, d), lambda i: (i, 0)), _whole(), _whole()] + [_ANY] * len(deps),
        out_specs=[pl.BlockSpec((tm, n), lambda i: (i, 0)), pl.BlockSpec((tm, d), lambda i: (i, 0))],
        out_shape=[S((t, n), out_dtype), S((t, d), BF16)],
        compiler_params=_params(("arbitrary",)),
    )(x, g, w, *deps)


def _nt_matmul_rms_bwd(dp, w, x, g, dres, *, tm, name, transposed=False, deps=()):
    t, n = dp.shape
    d = x.shape[1]

    def body(dp_ref, w_ref, x_ref, g_ref, dres_ref, dx_ref, dg_ref):
        @pl.when(pl.program_id(0) == 0)
        def _():
            dg_ref[...] = jnp.zeros_like(dg_ref)

        dh = _dot(dp_ref[...], w_ref[...]) if transposed else _dot_nt(dp_ref[...], w_ref[...])
        xv = x_ref[...]
        r = lax.rsqrt(_rowmean(xv * xv) + EPS)
        xh = xv * r
        dg_ref[...] += _colsum(dh * xh)
        dxh = dh * g_ref[...]
        dx_ref[...] = dres_ref[...] + r * (dxh - xh * _rowmean(dxh * xh))

    return pl.pallas_call(
        _after(body, 5, deps), name=name, grid=(t // tm,),
        in_specs=[pl.BlockSpec((tm, n), lambda i: (i, 0)), _whole(), pl.BlockSpec((tm, d), lambda i: (i, 0)),
                  _whole(), pl.BlockSpec((tm, d), lambda i: (i, 0))] + [_ANY] * len(deps),
        out_specs=[pl.BlockSpec((tm, d), lambda i: (i, 0)), pl.BlockSpec((1, d), lambda i: (0, 0))],
        out_shape=[S((t, d), F32), S((1, d), F32)],
        compiler_params=_params(("arbitrary",)),
    )(dp, w, x, g, dres, *deps)


def _tn_matmul(a, b, *, tmc, tk, out_dtype, name, deps=()):
    t, m = a.shape
    n = b.shape[1]
    nk = t // tk

    def body(a_ref, b_ref, o_ref, acc_ref):
        k = pl.program_id(1)

        @pl.when(k == 0)
        def _():
            acc_ref[...] = jnp.zeros_like(acc_ref)

        acc_ref[...] += _dot_tn(a_ref[...], b_ref[...])

        @pl.when(k == nk - 1)
        def _():
            o_ref[...] = acc_ref[...].astype(out_dtype)

    return pl.pallas_call(
        _after(body, 2, deps), name=name, grid=(m // tmc, nk),
        in_specs=[pl.BlockSpec((tk, tmc), lambda j, k: (k, j)), pl.BlockSpec((tk, n), lambda j, k: (k, 0))]
        + [_ANY] * len(deps),
        out_specs=pl.BlockSpec((tmc, n), lambda j, k: (j, 0)),
        out_shape=S((m, n), out_dtype),
        scratch_shapes=[pltpu.VMEM((tmc, n), F32)],
        compiler_params=_params(("arbitrary", "arbitrary")),
    )(a, b, *deps)


def _silu_parts(gt):
    sg = _sigmoid(gt)
    return gt * sg, sg * (1.0 + gt * (1.0 - sg))


def _attn_head(q_b, k_b, v_b):
    s = _dot_nt(q_b, k_b) * (HD ** -0.5)
    e = jnp.exp(s - jnp.max(s, axis=-1, keepdims=True))
    prob = e / jnp.sum(e, axis=-1, keepdims=True)
    return prob, _dot(prob.astype(BF16), v_b)


def _rms_residual(x, o, g):
    r = lax.rsqrt(_rowmean(o * o) + EPS)
    return x + o * r * g


def _rms_post_bwd(dres, o, g):
    r = lax.rsqrt(_rowmean(o * o) + EPS)
    oh = o * r
    doh = dres * g
    return r * (doh - oh * _rowmean(doh * oh)), _colsum(dres * oh)


LANE = 128
_TILE_GROUPS = [sorted({LANE * j // GRP, (LANE * j + LANE - 1) // GRP}) for j in range(BW // LANE)]


def _tile(j):
    return slice(LANE * j, LANE * (j + 1))


def _low_lanes():
    return lax.broadcasted_iota(jnp.int32, (1, LANE), 1) < GRP - LANE


def _by_group(fn):
    tiles = []
    for j, groups in enumerate(_TILE_GROUPS):
        if len(groups) == 1:
            tiles.append(fn(groups[0], j))
        else:
            tiles.append(jnp.where(_low_lanes(), fn(groups[0], j), fn(groups[1], j)))
    return jnp.concatenate(tiles, axis=1)


def _sgu_chunk(vn_b, ws_ref, bmap_ref):
    return bmap_ref[...] + _by_group(lambda h, j: _dot(ws_ref[h], vn_b[:, _tile(j)]))


def _shift_copies(buf, sh):
    n = buf.shape[0] - 8
    for b in range(1, 8):
        sh[b - 1, pl.ds(0, n), :] = buf[pl.ds(b, n), :]


def _loop_rows(rows, step, fn, carry=0, unrolled=True):
    if unrolled:
        for r0 in range(0, rows, step):
            carry = fn(r0, carry)
        return carry

    def body(j, c):
        return fn(pl.multiple_of(j * step, step), c)
    return lax.fori_loop(0, rows // step, body, carry)


def _rows_at(buf, sh, r0, off):
    b = off % 8
    if b == 0 or sh is None:
        return buf[pl.ds(r0 + off, 32), :]
    return sh[b - 1, pl.ds(r0 + (off - b), 32), :]


def _tap_sum(buf, sh, w_ref, r0, taps, causal):
    acc = None
    for k in range(taps):
        off = HALO - (taps - 1 - k) if causal else taps - 1 - k
        term = w_ref[k:k + 1, :] * _rows_at(buf, sh, r0, off)
        acc = term if acc is None else acc + term
    return acc


def _fold8(a):
    return a[0:8] + a[8:16] + a[16:24] + a[24:32]


def _conv_back(buf, sh, w_ref, acc_ref, z, r0, taps):
    dz = None
    for k in range(taps):
        ahead = _rows_at(buf, sh, r0, taps - 1 - k)
        term = w_ref[k:k + 1, :] * ahead
        dz = term if dz is None else dz + term
        acc_ref[k * 8:(k + 1) * 8, :] += _fold8(z * ahead)
    return dz


def _halo_spec(n, nt, reverse, tm):
    per = tm // HALO
    if reverse:
        return pl.BlockSpec((HALO, n), lambda i: (jnp.maximum((nt - 1 - i) * per - 1, 0), 0))
    return pl.BlockSpec((HALO, n), lambda i: (jnp.maximum(i * per - 1, 0), 0))


def _even_fwd(x, p, lng, lnb, ws, bmap, wc, kv, wout, pg):
    t = x.shape[0]
    tm = min(TM_FWD_EVEN, t)
    nt = t // tm

    def body(x_ref, p_ref, ph_ref, lng_ref, lnb_ref, ws_ref, bmap_ref, wc_ref, kv_ref, wout_ref, pg_ref,
             o_ref, x1_ref, ybuf, cbuf):
        i = pl.program_id(0)
        vh, _ = _ln_stats(p_ref[:, E_V:E_V + BW])
        vn = vh * lng_ref[...] + lnb_ref[...]
        for c in range(tm // CHUNK):
            sl = slice(c * CHUNK, (c + 1) * CHUNK)
            sg = _sgu_chunk(vn[sl].astype(BF16), ws_ref, bmap_ref)
            gate, _ = _silu_parts(p_ref[sl, E_GATE:E_GATE + BW])
            ybuf[sl, 0:BW] = (p_ref[sl, E_U:E_U + BW] * sg * gate).astype(BF16)

        cbuf[0:HALO] = jnp.where(i > 0, ph_ref[:, E_CG:E_CG + BW] * ph_ref[:, E_XIN:E_XIN + BW], 0.0)
        cbuf[HALO:HALO + tm] = p_ref[:, E_CG:E_CG + BW] * p_ref[:, E_XIN:E_XIN + BW]
        for r0 in range(0, tm, 32):
            sl = slice(r0, r0 + 32)
            cv = _tap_sum(cbuf, None, wc_ref, r0, 3, True)
            gate, _ = _silu_parts(p_ref[sl, E_GATE + BW:E_GATE + 2 * BW])
            ybuf[sl, BW:2 * BW] = (p_ref[sl, E_BG:E_BG + BW] * cv * gate).astype(BF16)

        for h in range(NH):
            qs = slice(E_Q + h * HD, E_Q + (h + 1) * HD)
            _, yx = _attn_head(p_ref[:, qs].astype(BF16), kv_ref[:, h * HD:(h + 1) * HD],
                               kv_ref[:, XA + h * HD:XA + (h + 1) * HD])
            gs = slice(E_GATE + 2 * BW + h * HD, E_GATE + 2 * BW + (h + 1) * HD)
            gate, _ = _silu_parts(p_ref[:, gs])
            ybuf[:, 2 * BW + h * HD:2 * BW + (h + 1) * HD] = (yx * gate).astype(BF16)

        o = _dot(ybuf[...], wout_ref[...])
        o_ref[...] = o
        x1_ref[...] = _rms_residual(x_ref[...], o, pg_ref[...])

    tile = lambda n: pl.BlockSpec((tm, n), lambda i: (i, 0))
    return pl.pallas_call(
        body, name="even_fwd", grid=(nt,),
        in_specs=[tile(D), tile(EVEN_IN), _halo_spec(EVEN_IN, nt, False, tm)] + [_whole()] * 8,
        out_specs=[tile(D), tile(D)],
        out_shape=[S((t, D), F32), S((t, D), F32)],
        scratch_shapes=[pltpu.VMEM((tm, MIX), BF16), pltpu.VMEM((tm + HALO, BW), F32)],
        compiler_params=_params(("arbitrary",)),
    )(x, p, p, lng, lnb, ws, bmap, wc, kv, wout, pg)


def _even_bwd(dres, o, p, lng, lnb, ws, wst, bmap, wc, kv, wout, pg):
    t = dres.shape[0]
    tm = min(TM_BWD_EVEN, t)
    nt = t // tm

    def body(dres_ref, o_ref, p_ref, ph_ref, lng_ref, lnb_ref, ws_ref, wst_ref, bmap_ref, wc_ref, kv_ref, wout_ref,
             pg_ref, dp_ref, y_ref, do_ref, dpg_ref, dws_ref, dbs_ref, dlng_ref, dlnb_ref, dwc_ref, dkv_ref,
             dy, cbuf, gbuf, dconv, carry, dvn, dbmap, wacc):
        i = pl.program_id(0)
        ti = nt - 1 - i
        masks = _group_masks()

        @pl.when(i == 0)
        def _():
            for ref in (dpg_ref, dws_ref, dlng_ref, dlnb_ref, dkv_ref, dbmap, wacc):
                ref[...] = jnp.zeros_like(ref)

        do, dpg = _rms_post_bwd(dres_ref[...], o_ref[...], pg_ref[...])
        dpg_ref[...] += dpg
        do_b = do.astype(BF16)
        do_ref[...] = do_b
        dy[...] = _dot_nt(do_b, wout_ref[...])

        vh, rs = _ln_stats(p_ref[:, E_V:E_V + BW])
        vn = vh * lng_ref[...] + lnb_ref[...]
        for c in range(tm // CHUNK):
            sl = slice(c * CHUNK, (c + 1) * CHUNK)
            vn_b = vn[sl].astype(BF16)
            sg = _sgu_chunk(vn_b, ws_ref, bmap_ref)
            u = p_ref[sl, E_U:E_U + BW]
            gate, dgate = _silu_parts(p_ref[sl, E_GATE:E_GATE + BW])
            dyc = dy[sl, 0:BW]
            ya = u * sg
            y_ref[sl, 0:BW] = (ya * gate).astype(BF16)
            dp_ref[sl, E_GATE:E_GATE + BW] = (dyc * ya * dgate).astype(BF16)
            dya = dyc * gate
            dp_ref[sl, E_U:E_U + BW] = (dya * sg).astype(BF16)
            dsg = dya * u
            dbmap[...] += dsg
            dsg_b = dsg.astype(BF16)
            for h in range(NH):
                total = None
                for j, heads in enumerate(_TILE_GROUPS):
                    if h in heads:
                        d_t = dsg_b[:, _tile(j)]
                        if len(heads) == 2:
                            d_t = jnp.where(_low_lanes() == (h == heads[0]), d_t, jnp.zeros_like(d_t))
                        part = _dot_nt(d_t, vn_b[:, _tile(j)])
                        total = part if total is None else total + part
                dws_ref[h] += total
            dvn[sl, :] = _by_group(lambda h, j: _dot(wst_ref[h], dsg_b[:, _tile(j)]))
        dn = dvn[...]
        dlng_ref[...] += _colsum(dn * vh)
        dlnb_ref[...] += _colsum(dn)
        dp_ref[:, E_V:E_V + BW] = _ln_bwd(dn, vh, rs, lng_ref[...]).astype(BF16)

        cbuf[0:HALO] = jnp.where(ti > 0, ph_ref[:, E_CG:E_CG + BW] * ph_ref[:, E_XIN:E_XIN + BW], 0.0)
        cbuf[HALO:HALO + tm] = p_ref[:, E_CG:E_CG + BW] * p_ref[:, E_XIN:E_XIN + BW]
        for r0 in range(0, tm, 32):
            sl = slice(r0, r0 + 32)
            cv = _tap_sum(cbuf, None, wc_ref, r0, 3, True)
            gate, dgate = _silu_parts(p_ref[sl, E_GATE + BW:E_GATE + 2 * BW])
            bg = p_ref[sl, E_BG:E_BG + BW]
            dyc = dy[sl, BW:2 * BW]
            yb = bg * cv
            y_ref[sl, BW:2 * BW] = (yb * gate).astype(BF16)
            dp_ref[sl, E_GATE + BW:E_GATE + 2 * BW] = (dyc * yb * dgate).astype(BF16)
            dyb = dyc * gate
            dp_ref[sl, E_BG:E_BG + BW] = (dyb * cv).astype(BF16)
            dconv[sl, :] = dyb * bg
        gbuf[0:tm] = dconv[...]
        gbuf[tm:tm + HALO] = jnp.where(i > 0, carry[...], 0.0)
        carry[...] = dconv[0:HALO]
        for r0 in range(0, tm, 32):
            sl = slice(r0, r0 + 32)
            dc = _conv_back(gbuf, None, wc_ref, wacc, cbuf[HALO + r0:HALO + r0 + 32, :], r0, 3)
            dp_ref[sl, E_CG:E_CG + BW] = (dc * p_ref[sl, E_XIN:E_XIN + BW]).astype(BF16)
            dp_ref[sl, E_XIN:E_XIN + BW] = (dc * p_ref[sl, E_CG:E_CG + BW]).astype(BF16)

        for h in range(NH):
            qs = slice(E_Q + h * HD, E_Q + (h + 1) * HD)
            ks = slice(h * HD, (h + 1) * HD)
            vs = slice(XA + h * HD, XA + (h + 1) * HD)
            gs = slice(E_GATE + 2 * BW + h * HD, E_GATE + 2 * BW + (h + 1) * HD)
            ys = slice(2 * BW + h * HD, 2 * BW + (h + 1) * HD)
            q_b = p_ref[:, qs].astype(BF16)
            prob, yx = _attn_head(q_b, kv_ref[:, ks], kv_ref[:, vs])
            gate, dgate = _silu_parts(p_ref[:, gs])
            dyc = dy[:, ys]
            y_ref[:, ys] = (yx * gate).astype(BF16)
            dp_ref[:, gs] = (dyc * yx * dgate).astype(BF16)
            dyx_b = (dyc * gate).astype(BF16)
            dprob = _dot_nt(dyx_b, kv_ref[:, vs])
            dkv_ref[:, vs] += _dot_tn(prob.astype(BF16), dyx_b)
            ds_b = (prob * (dprob - jnp.sum(dprob * prob, axis=-1, keepdims=True)) * (HD ** -0.5)).astype(BF16)
            dp_ref[:, qs] = _dot(ds_b, kv_ref[:, ks]).astype(BF16)
            dkv_ref[:, ks] += _dot_tn(ds_b, q_b)

        @pl.when(i == nt - 1)
        def _():
            for h in range(NH):
                dbs_ref[:, h * HD:(h + 1) * HD] = jnp.broadcast_to(
                    jnp.sum(dbmap[...] * masks[h], axis=-1, keepdims=True), (CHUNK, HD))
            for k in range(3):
                dwc_ref[k:k + 1, :] = _colsum(wacc[k * 8:(k + 1) * 8, :])
            dwc_ref[3:8, :] = jnp.zeros((5, BW), F32)
            causal = (lax.broadcasted_iota(jnp.int32, (CHUNK, CHUNK), 0)
                      >= lax.broadcasted_iota(jnp.int32, (CHUNK, CHUNK), 1))
            for h in range(NH):
                dws_ref[h] = jnp.where(causal, dws_ref[h], 0.0)

    rtile = lambda n: pl.BlockSpec((tm, n), lambda i: (nt - 1 - i, 0))
    outs = [S((t, EVEN_IN), BF16), S((t, MIX), BF16), S((t, D), BF16), S((1, D), F32), S((NH, CHUNK, CHUNK), F32),
            S((CHUNK, NH * HD), F32), S((1, BW), F32), S((1, BW), F32), S((8, BW), F32), S((N_MEM, 2 * XA), F32)]
    return pl.pallas_call(
        body, name="even_bwd", grid=(nt,),
        in_specs=[rtile(D), rtile(D), rtile(EVEN_IN), _halo_spec(EVEN_IN, nt, True, tm)] + [_whole()] * 9,
        out_specs=[rtile(EVEN_IN), rtile(MIX), rtile(D)] + [_full(s.shape) for s in outs[3:]],
        out_shape=outs,
        scratch_shapes=[pltpu.VMEM((tm, MIX), F32), pltpu.VMEM((tm + HALO, BW), F32), pltpu.VMEM((tm + HALO, BW), F32),
                        pltpu.VMEM((tm, BW), F32), pltpu.VMEM((HALO, BW), F32), pltpu.VMEM((tm, BW), F32),
                        pltpu.VMEM((CHUNK, BW), F32), pltpu.VMEM((3 * 8, BW), F32)],
        compiler_params=_params(("arbitrary",)),
    )(dres, o, p, p, lng, lnb, ws, wst, bmap, wc, kv, wout, pg)


def _pool_causal_levels(za, zb, zc, zd, tm):
    n = tm + HALO
    zb[pl.ds(8, n - 8), :] = za[pl.ds(8, n - 8), :] + za[pl.ds(7, n - 8), :]
    zc[pl.ds(16, n - 16), :] = zb[pl.ds(16, n - 16), :] + zb[pl.ds(14, n - 16), :]
    zd[pl.ds(24, n - 24), :] = zc[pl.ds(24, n - 24), :] + zc[pl.ds(20, n - 24), :]


def _pool_causal(za, zb, zc, zd, tm):
    _pool_causal_levels(za, zb, zc, zd, tm)
    s16 = zd[pl.ds(HALO, tm), :] + zd[pl.ds(HALO - 8, tm), :]
    return zb[pl.ds(HALO, tm), :], zc[pl.ds(HALO, tm), :], zd[pl.ds(HALO, tm), :], s16


def _pool_anticausal_levels(ea, eb, ec, ed, tm):
    n = tm + HALO
    eb[pl.ds(0, n - 8), :] = ea[pl.ds(0, n - 8), :] + ea[pl.ds(1, n - 8), :]
    ec[pl.ds(0, n - 16), :] = eb[pl.ds(0, n - 16), :] + eb[pl.ds(2, n - 16), :]
    ed[pl.ds(0, n - 24), :] = ec[pl.ds(0, n - 24), :] + ec[pl.ds(4, n - 24), :]


def _pool_weights(t0, rows):
    tf = (t0 + lax.broadcasted_iota(jnp.int32, (rows, 1), 0) + 1).astype(F32)
    inv = [jnp.broadcast_to(1.0 / jnp.minimum(tf, float(win)), (rows, LANE)) for win in POOL_WINDOWS]
    return _by_group(lambda g, j: inv[g])


_HALVES = (slice(0, BW // 2), slice(BW // 2, BW))


def _mix4(parts):
    return _by_group(lambda g, j: parts[g][:, _tile(j)])


def _odd_fwd(x1, tgt, p, wbd, cscale, dww, dwb, lng, lnb, wpw, pwb, kv, wout, pg):
    t = x1.shape[0]
    tm = min(TM_FWD_ODD, t)
    nt = t // tm

    def body(x_ref, tgt_ref, p_ref, ph_ref, wbd_ref, cs_ref, dww_ref, dwb_ref, lng_ref, lnb_ref, wpw_ref, pwb_ref,
             kv_ref, wout_ref, pg_ref, o_ref, dres_ref, loss_ref, conv_ref, ybuf, za, zb, zc, zd, gbuf, lacc, gsh):
        i = pl.program_id(0)

        @pl.when(i == 0)
        def _():
            lacc[...] = jnp.zeros_like(lacc)

        z = p_ref[:, O_ZC:O_ZC + BW]
        za[0:HALO] = jnp.where(i > 0, ph_ref[:, O_ZC:O_ZC + BW], 0.0)
        za[HALO:HALO + tm] = z
        pooled = _mix4(_pool_causal(za, zb, zc, zd, tm)) * _pool_weights(i * tm, tm) - z
        pooled_b = pooled.astype(BF16)
        for hs in _HALVES:
            gate, _ = _silu_parts(p_ref[:, O_GATE + hs.start:O_GATE + hs.stop])
            ybuf[:, hs] = (_dot(pooled_b[:, hs], wbd_ref[hs, hs]) * cs_ref[:, hs] * gate).astype(BF16)

        gbuf[0:HALO] = jnp.where(i > 0, ph_ref[:, O_GA:O_GA + BW] * _sigmoid(ph_ref[:, O_GB:O_GB + BW]), 0.0)
        gbuf[HALO:HALO + tm] = p_ref[:, O_GA:O_GA + BW] * _sigmoid(p_ref[:, O_GB:O_GB + BW])
        _shift_copies(gbuf, gsh)
        def conv_rows(r0, carry):
            conv_ref[pl.ds(r0, 32), :] = _tap_sum(gbuf, gsh, dww_ref, r0, CONF_K, True) + dwb_ref[...]
            return carry

        _loop_rows(tm, 32, conv_rows)
        zh, _ = _ln_stats(conv_ref[...])
        zn = zh * lng_ref[...] + lnb_ref[...]
        yd = _dot((zn * _sigmoid(zn)).astype(BF16), wpw_ref[...]) + pwb_ref[...]
        gate, _ = _silu_parts(p_ref[:, O_GATE + BW:O_GATE + 2 * BW])
        ybuf[:, BW:2 * BW] = (yd * gate).astype(BF16)

        for h in range(NH):
            qs = slice(O_Q + h * HD, O_Q + (h + 1) * HD)
            _, yx = _attn_head(p_ref[:, qs].astype(BF16), kv_ref[:, h * HD:(h + 1) * HD],
                               kv_ref[:, XA + h * HD:XA + (h + 1) * HD])
            gs = slice(O_GATE + 2 * BW + h * HD, O_GATE + 2 * BW + (h + 1) * HD)
            gate, _ = _silu_parts(p_ref[:, gs])
            ybuf[:, 2 * BW + h * HD:2 * BW + (h + 1) * HD] = (yx * gate).astype(BF16)

        o = _dot(ybuf[...], wout_ref[...])
        o_ref[...] = o
        err = _rms_residual(x_ref[...], o, pg_ref[...]) - tgt_ref[...]
        lacc[...] += _colsum(err * err)
        dres_ref[...] = err * (1.0 / D)

        @pl.when(i == nt - 1)
        def _():
            loss_ref[...] = jnp.full((1, HD), jnp.sum(lacc[...]) * (0.5 / D), F32)

    tile = lambda n: pl.BlockSpec((tm, n), lambda i: (i, 0))
    ext = pltpu.VMEM((tm + HALO, BW), F32)
    return pl.pallas_call(
        body, name="odd_fwd", grid=(nt,),
        in_specs=[tile(D), tile(D), tile(ODD_IN), _halo_spec(ODD_IN, nt, False, tm)] + [_whole()] * 11,
        out_specs=[tile(D), tile(D), _full((1, HD)), tile(BW)],
        out_shape=[S((t, D), F32), S((t, D), F32), S((1, HD), F32), S((t, BW), F32)],
        scratch_shapes=[pltpu.VMEM((tm, MIX), BF16), ext, ext, ext, ext, ext,
                        pltpu.VMEM((1, D), F32), pltpu.VMEM((7, tm + HALO, BW), F32)],
        compiler_params=_params(("arbitrary",)),
    )(x1, tgt, p, p, wbd, cscale, dww, dwb, lng, lnb, wpw, pwb, kv, wout, pg)


def _odd_bwd(dres, o, p, conv, wbd, cscale, dww, dwb, lng, lnb, wpw, pwb, kv, wout, pg):
    t = dres.shape[0]
    tm = min(TM_BWD_ODD, t)
    nt = t // tm

    def body(dres_ref, o_ref, p_ref, ph_ref, conv_ref, wbd_ref, cs_ref, dww_ref, dwb_ref, lng_ref, lnb_ref, wpw_ref,
             pwb_ref, kv_ref, wout_ref, pg_ref, dp_ref, y_ref, do_ref, dpg_ref, dwbd_ref, dcs_ref, ddww_ref, ddwb_ref,
             dlng_ref, dlnb_ref, dwpw_ref, dpwb_ref, dkv_ref,
             dy, za, zb, zc, zd, carry_e, carry_d, wacc, shifted, t1, b1, b2):
        hbuf = zb
        i = pl.program_id(0)
        ti = nt - 1 - i

        @pl.when(i == 0)
        def _():
            for ref in (dpg_ref, dwbd_ref, dcs_ref, ddwb_ref, dlng_ref, dlnb_ref, dwpw_ref, dpwb_ref, dkv_ref, wacc):
                ref[...] = jnp.zeros_like(ref)

        def post_norm_rows(r0, acc):
            sl = pl.ds(r0, RB)
            ov, dv = o_ref[sl, :], dres_ref[sl, :]
            r = lax.rsqrt(_rowmean(ov * ov) + EPS)
            oh = ov * r
            doh = dv * pg_ref[...]
            do_ref[sl, :] = (r * (doh - oh * _rowmean(doh * oh))).astype(BF16)
            return acc + dv * oh

        dpg_ref[...] += _colsum(_loop_rows(tm, RB, post_norm_rows, jnp.zeros((RB, D), F32)))
        dy[...] = _dot_nt(do_ref[...], wout_ref[...])

        za[0:HALO] = jnp.where(ti > 0, ph_ref[:, O_ZC:O_ZC + BW], 0.0)
        za[HALO:HALO + tm] = p_ref[:, O_ZC:O_ZC + BW]
        _pool_causal_levels(za, zb, zc, zd, tm)

        def pooled_rows(r0, carry):
            sl = pl.ds(r0, RB)
            at = lambda ref, back=0: ref[pl.ds(HALO + r0 - back, RB), :]
            inv = _pool_weights(ti * tm + r0, RB)
            sums = (at(zb), at(zc), at(zd), at(zd) + at(zd, 8))
            b1[sl, :] = (_mix4(sums) * inv - p_ref[sl, O_ZC:O_ZC + BW]).astype(BF16)
            return carry

        _loop_rows(tm, RB, pooled_rows)
        for hs in _HALVES:
            t1[:, hs] = _dot(b1[:, hs], wbd_ref[hs, hs])

        def pool_gate_rows(r0, acc):
            sl = pl.ds(r0, RB)
            pm = t1[sl, :]
            gate, dgate = _silu_parts(p_ref[sl, O_GATE:O_GATE + BW])
            dyc = dy[sl, 0:BW]
            yc = pm * cs_ref[...]
            y_ref[sl, 0:BW] = (yc * gate).astype(BF16)
            dp_ref[sl, O_GATE:O_GATE + BW] = (dyc * yc * dgate).astype(BF16)
            dyc = dyc * gate
            b2[sl, :] = (dyc * cs_ref[...]).astype(BF16)
            return acc + dyc * pm

        dcs_ref[...] += _colsum(_loop_rows(tm, RB, pool_gate_rows, jnp.zeros((RB, BW), F32)))
        for hs in _HALVES:
            dwbd_ref[hs, hs] += _dot_tn(b1[:, hs], b2[:, hs])
            t1[:, hs] = _dot_nt(b2[:, hs], wbd_ref[hs, hs])

        def weighted_rows(r0, carry):
            sl = pl.ds(r0, RB)
            za[sl, :] = t1[sl, :] * _pool_weights(ti * tm + r0, RB)
            return carry

        _loop_rows(tm, RB, weighted_rows)
        za[tm:tm + HALO] = jnp.where(i > 0, carry_e[...], 0.0)
        carry_e[...] = za[0:HALO]
        _pool_anticausal_levels(za, zb, zc, zd, tm)

        def pool_back_rows(r0, carry):
            sl = pl.ds(r0, RB)
            ahead = lambda ref, fwd=0: ref[pl.ds(r0 + fwd, RB), :]
            sums = (ahead(zb), ahead(zc), ahead(zd), ahead(zd) + ahead(zd, 8))
            dp_ref[sl, O_ZC:O_ZC + BW] = (_mix4(sums) - t1[sl, :]).astype(BF16)
            return carry

        _loop_rows(tm, RB, pool_back_rows)

        def swish_rows(r0, carry):
            sl = pl.ds(r0, RB)
            zh, _ = _ln_stats(conv_ref[sl, :])
            zn = zh * lng_ref[...] + lnb_ref[...]
            b1[sl, :] = (zn * _sigmoid(zn)).astype(BF16)
            return carry

        _loop_rows(tm, RB, swish_rows)
        t1[...] = _dot(b1[...], wpw_ref[...])

        def conf_gate_rows(r0, acc):
            sl = pl.ds(r0, RB)
            yd = t1[sl, :] + pwb_ref[...]
            gate, dgate = _silu_parts(p_ref[sl, O_GATE + BW:O_GATE + 2 * BW])
            dyc = dy[sl, BW:2 * BW]
            y_ref[sl, BW:2 * BW] = (yd * gate).astype(BF16)
            dp_ref[sl, O_GATE + BW:O_GATE + 2 * BW] = (dyc * yd * dgate).astype(BF16)
            dyd = dyc * gate
            b2[sl, :] = dyd.astype(BF16)
            return acc + dyd

        dpwb_ref[...] += _colsum(_loop_rows(tm, RB, conf_gate_rows, jnp.zeros((RB, BW), F32)))
        dwpw_ref[...] += _dot_tn(b1[...], b2[...])
        t1[...] = _dot_nt(b2[...], wpw_ref[...])

        def norm_back_rows(r0, accs):
            sl = pl.ds(r0, RB)
            zh, rs = _ln_stats(conv_ref[sl, :])
            _, dsilu = _silu_parts(zh * lng_ref[...] + lnb_ref[...])
            dzn = t1[sl, :] * dsilu
            dzd = _ln_bwd(dzn, zh, rs, lng_ref[...])
            hbuf[sl, :] = dzd
            return accs[0] + dzn * zh, accs[1] + dzn, accs[2] + dzd

        zero = jnp.zeros((RB, BW), F32)
        acc_g, acc_b, acc_d = _loop_rows(tm, RB, norm_back_rows, (zero, zero, zero))
        dlng_ref[...] += _colsum(acc_g)
        dlnb_ref[...] += _colsum(acc_b)
        ddwb_ref[...] += _colsum(acc_d)
        hbuf[tm:tm + HALO] = jnp.where(i > 0, carry_d[...], 0.0)
        carry_d[...] = hbuf[0:HALO]
        _shift_copies(hbuf, shifted)

        def conv_back_rows(r0, carry):
            sl = pl.ds(r0, 32)
            sgb = _sigmoid(p_ref[sl, O_GB:O_GB + BW])
            ga = p_ref[sl, O_GA:O_GA + BW]
            dzg = _conv_back(hbuf, shifted, dww_ref, wacc, ga * sgb, r0, CONF_K)
            dp_ref[sl, O_GA:O_GA + BW] = (dzg * sgb).astype(BF16)
            dp_ref[sl, O_GB:O_GB + BW] = (dzg * ga * sgb * (1.0 - sgb)).astype(BF16)
            return carry

        _loop_rows(tm, 32, conv_back_rows, unrolled=False)

        for h in range(NH):
            qs = slice(O_Q + h * HD, O_Q + (h + 1) * HD)
            ks = slice(h * HD, (h + 1) * HD)
            vs = slice(XA + h * HD, XA + (h + 1) * HD)
            gs = slice(O_GATE + 2 * BW + h * HD, O_GATE + 2 * BW + (h + 1) * HD)
            ys = slice(2 * BW + h * HD, 2 * BW + (h + 1) * HD)
            q_b = p_ref[:, qs].astype(BF16)
            prob, yx = _attn_head(q_b, kv_ref[:, ks], kv_ref[:, vs])
            gate, dgate = _silu_parts(p_ref[:, gs])
            dyc = dy[:, ys]
            y_ref[:, ys] = (yx * gate).astype(BF16)
            dp_ref[:, gs] = (dyc * yx * dgate).astype(BF16)
            dyx_b = (dyc * gate).astype(BF16)
            dprob = _dot_nt(dyx_b, kv_ref[:, vs])
            dkv_ref[:, vs] += _dot_tn(prob.astype(BF16), dyx_b)
            ds_b = (prob * (dprob - jnp.sum(dprob * prob, axis=-1, keepdims=True)) * (HD ** -0.5)).astype(BF16)
            dp_ref[:, qs] = _dot(ds_b, kv_ref[:, ks]).astype(BF16)
            dkv_ref[:, ks] += _dot_tn(ds_b, q_b)

        @pl.when(i == nt - 1)
        def _():
            for k in range(CONF_K):
                ddww_ref[k:k + 1, :] = _colsum(wacc[k * 8:(k + 1) * 8, :])
            ddww_ref[CONF_K:CONF_K + 1, :] = jnp.zeros((1, BW), F32)

    rtile = lambda n: pl.BlockSpec((tm, n), lambda i: (nt - 1 - i, 0))
    outs = [S((t, ODD_IN), BF16), S((t, MIX), BF16), S((t, D), BF16), S((1, D), F32), S((BW, BW), F32),
            S((1, BW), F32), S((CONF_K + 1, BW), F32), S((1, BW), F32), S((1, BW), F32), S((1, BW), F32),
            S((BW, BW), F32), S((1, BW), F32), S((N_MEM, 2 * XA), F32)]
    ext = pltpu.VMEM((tm + HALO, BW), F32)
    return pl.pallas_call(
        body, name="odd_bwd", grid=(nt,),
        in_specs=[rtile(D), rtile(D), rtile(ODD_IN), _halo_spec(ODD_IN, nt, True, tm), rtile(BW)] + [_whole()] * 11,
        out_specs=[rtile(ODD_IN), rtile(MIX), rtile(D)] + [_full(s.shape) for s in outs[3:]],
        out_shape=outs,
        scratch_shapes=[pltpu.VMEM((tm, MIX), F32), ext, ext, ext, ext,
                        pltpu.VMEM((HALO, BW), F32), pltpu.VMEM((HALO, BW), F32), pltpu.VMEM((CONF_K * 8, BW), F32),
                        pltpu.VMEM((7, tm + HALO, BW), F32),
                        pltpu.VMEM((tm, BW), F32), pltpu.VMEM((tm, BW), BF16), pltpu.VMEM((tm, BW), BF16)],
        compiler_params=pltpu.CompilerParams(dimension_semantics=("arbitrary",),
                                             vmem_limit_bytes=VMEM_LIMIT_ODD_BWD_V7X),
    )(dres, o, p, p, conv, wbd, cscale, dww, dwb, lng, lnb, wpw, pwb, kv, wout, pg)


def _pick_rows(n):
    for rows in (3200, 2432, 1024, 768):
        if n % rows == 0:
            return rows
    return n


def _pad_rows(a, rows):
    return jnp.pad(a, ((0, rows - a.shape[0]), (0, 0)))


def _step(x, mem, tgt, ex):
    t = x.shape[0]
    tm = min(512, t)
    w, deps = ex.first()
    causal = jnp.tril(jnp.ones((CHUNK, CHUNK), bool))
    ws = jnp.where(causal[None], w["even_a_ws"], 0.0).astype(BF16)
    wst = jnp.transpose(ws, (0, 2, 1))
    bmap = jnp.repeat(w["even_a_bs"].T, GRP, axis=1)
    wc = _pad_rows(w["even_b_conv"], 8)
    wbd = jax.scipy.linalg.block_diag(*[w["odd_c_wgrp"][g] for g in range(NH)]).astype(BF16)
    dww = _pad_rows(w["odd_d_dw_w"], CONF_K + 1)
    tk = min(1024, t)
    zeros = jnp.zeros_like(mem)

    p_e, h_e = _rms_matmul(x, w["even_pre_g"], w["even_w_in"], tm=tm, name="in_even", transposed=True, deps=deps)
    w.update(ex.even_rest(h_e))
    kv_e, memn_e = _rms_matmul(mem, w["even_mem_g"], w["even_w_kv"], tm=N_MEM, name="kv_even", out_dtype=BF16)
    even_args = (w["even_a_ln_g"], w["even_a_ln_b"], ws)
    o_e, x1 = _even_fwd(x, p_e, *even_args, bmap, wc, kv_e, w["even_w_out"], w["even_post_g"])
    w.update(ex.odd(o_e))
    kv_o, memn_o = _rms_matmul(mem, w["odd_mem_g"], w["odd_w_kv"], tm=N_MEM, name="kv_odd", out_dtype=BF16)
    p_o, h_o = _rms_matmul(x1, w["odd_pre_g"], w["odd_w_in"], tm=tm, name="in_odd", transposed=True)
    odd_args = (wbd, w["odd_c_scale"], dww, w["odd_d_dw_b"], w["odd_d_ln_g"], w["odd_d_ln_b"], w["odd_d_pw_w"],
                w["odd_d_pw_b"], kv_o, w["odd_w_out"], w["odd_post_g"])
    o_o, dres, loss, conv_o = _odd_fwd(x1, tgt, p_o, *odd_args)

    g = {}
    (dp_o, y_o, do_o, post_g_o, dwbd, g["odd_c_scale"], ddww, g["odd_d_dw_b"], g["odd_d_ln_g"], g["odd_d_ln_b"],
     dwpw, g["odd_d_pw_b"], dkv_o) = _odd_bwd(dres, o_o, p_o, conv_o, *odd_args)
    g["odd_post_g"] = post_g_o
    g["odd_d_dw_w"] = ddww[:CONF_K]
    dkv_o = dkv_o.astype(BF16)
    deps = ex.send("odd", {
        "odd_w_in": _tn_matmul(dp_o, h_o, tmc=_pick_rows(ODD_IN), tk=tk, out_dtype=BF16, name="dw_in_odd"),
        "odd_w_out": _tn_matmul(y_o, do_o, tmc=MIX, tk=tk, out_dtype=BF16, name="dw_out_odd"),
        "odd_w_kv": _tn_matmul(memn_o, dkv_o, tmc=D, tk=N_MEM, out_dtype=BF16, name="dw_kv_odd"),
        "odd_d_pw_w": dwpw.astype(BF16), "loss": loss,
        "odd_c_wgrp": jnp.concatenate([dwbd[i * GRP:(i + 1) * GRP, i * GRP:(i + 1) * GRP] for i in range(NH)])})
    dx1, g["odd_pre_g"] = _nt_matmul_rms_bwd(dp_o, w["odd_w_in"], x1, w["odd_pre_g"], dres, tm=tm,
                                             name="dx_odd", transposed=True, deps=deps)
    _, g["odd_mem_g"] = _nt_matmul_rms_bwd(dkv_o, w["odd_w_kv"], mem, w["odd_mem_g"], zeros, tm=N_MEM,
                                           name="dmem_odd")

    (dp_e, y_e, do_e, post_g_e, dws, dbs, ln_g_e, ln_b_e, dwc, dkv_e) = _even_bwd(
        dx1, o_e, p_e, *even_args, wst, bmap, wc, kv_e, w["even_w_out"], w["even_post_g"])
    g["even_b_conv"] = dwc[:3]
    dkv_e = dkv_e.astype(BF16)
    deps = ex.send("even_rest", {
        "even_w_out": _tn_matmul(y_e, do_e, tmc=MIX, tk=tk, out_dtype=BF16, name="dw_out_even"),
        "even_w_kv": _tn_matmul(memn_e, dkv_e, tmc=D, tk=N_MEM, out_dtype=BF16, name="dw_kv_even"),
        "even_a_ln_g": ln_g_e, "even_a_ln_b": ln_b_e,
        "even_a_ws": dws.reshape(NH * CHUNK, CHUNK), "even_a_bs": dbs[:, ::HD].T})
    g["even_w_in"] = _tn_matmul(dp_e, h_e, tmc=_pick_rows(EVEN_IN), tk=tk, out_dtype=BF16, name="dw_in_even",
                                deps=deps)
    deps = ex.send("even_in", g)
    grad_x, pre_g_e = _nt_matmul_rms_bwd(dp_e, w["even_w_in"], x, w["even_pre_g"], dx1, tm=tm,
                                         name="dx_even", transposed=True, deps=deps)
    _, mem_g_e = _nt_matmul_rms_bwd(dkv_e, w["even_w_kv"], mem, w["even_mem_g"], zeros, tm=N_MEM, name="dmem_even",
                                    deps=(grad_x,))
    deps = ex.send("even_gains", {"even_pre_g": pre_g_e, "even_mem_g": mem_g_e, "even_post_g": post_g_e})
    return grad_x, deps


def _place():
    return lax.axis_index("x"), lax.axis_index("y"), lax.axis_index("c")


def _index(px, py, pc):
    return 4 * px + 2 * py + pc


_COPIES = N_DEV - 1


def _all_gather(arrs, name):
    n = len(arrs)

    def body(*refs):
        ins, outs = refs[:n], refs[n:2 * n]
        send_sems, recv_sems, local_sems = refs[2 * n:]
        x, y, c = _place()
        me, sibling = (x, y, c), (x, y, 1 - c)
        chips = [(1 - x, y), (x, 1 - y), (1 - x, 1 - y)]

        def copy(a, k, block, to, src=None):
            dst = outs[a].at[_index(*block)]
            return pltpu.make_async_remote_copy(
                src_ref=dst if src is None else src, dst_ref=dst, send_sem=send_sems.at[a * _COPIES + k],
                recv_sem=recv_sems.at[a * _COPIES + k], device_id=to, device_id_type=MESH)

        mine = [pltpu.make_async_copy(ins[a], outs[a].at[_index(*me)], local_sems.at[a]) for a in range(n)]
        first = []
        for a in range(n):
            mine[a].start()
            first.append(copy(a, 0, me, sibling, src=ins[a]))
            first += [copy(a, 1 + j, me, (*chip, c), src=ins[a]) for j, chip in enumerate(chips)]
        for cp in first:
            cp.start()
        passed = []
        for j, chip in enumerate(chips):
            for a in range(n):
                copy(a, 1 + j, (*chip, c), me).wait_recv()
                passed.append(copy(a, 4 + j, (*chip, c), sibling))
                passed[-1].start()
        for a in range(n):
            copy(a, 0, sibling, me).wait_recv()
            for j, chip in enumerate(chips):
                copy(a, 4 + j, (*chip, 1 - c), me).wait_recv()
        for cp in first + passed:
            cp.wait_send()
        for cp in mine:
            cp.wait()

    return pl.pallas_call(
        body, name=name, in_specs=[_ANY] * n, out_specs=[_ANY] * n,
        out_shape=[S((N_DEV,) + a.shape, a.dtype) for a in arrs],
        scratch_shapes=[pltpu.SemaphoreType.DMA((n * _COPIES,)), pltpu.SemaphoreType.DMA((n * _COPIES,)),
                        pltpu.SemaphoreType.DMA((n,))],
    )(*arrs)


_HBM = pl.BlockSpec(memory_space=pltpu.HBM)
_SEM = pl.BlockSpec(memory_space=pltpu.SEMAPHORE)
_EFFECT = pltpu.SideEffectType.DATAFLOW_SIDE_EFFECTING


_ALL_FLIPS = [(k >> 2 & 1, k >> 1 & 1, k & 1) for k in range(1, N_DEV)]
_CHIP_FLIPS = [(1, 0, 0), (0, 1, 0), (1, 1, 0)]
_FLIPS = {"gather": _ALL_FLIPS, "scatter": _ALL_FLIPS, "gather_chips": [(0, 0, 1)] + _CHIP_FLIPS,
          "scatter_chips": _CHIP_FLIPS}


def _landing_shape(kind, a):
    return (N_DEV,) + a.shape if kind.startswith("gather") else a.shape


def _exchange_copies(kinds, srcs, lands, send_sems, recv_sems, local_sems, arriving):
    x, y, c = _place()
    mine = _index(x, y, c)
    remote, local = [], []
    for a, kind in enumerate(kinds):
        by_chip = kind == "scatter_chips"
        here = 2 * x + y if by_chip else mine
        own = srcs[a] if kind.startswith("gather") else srcs[a].at[here]
        local.append(pltpu.make_async_copy(own, lands[a].at[here], local_sems.at[a]))
        for k, (fx, fy, fc) in enumerate(_FLIPS[kind]):
            peer = (1 - x if fx else x, 1 - y if fy else y, 1 - c if fc else c)
            there = 2 * peer[0] + peer[1] if by_chip else _index(*peer)
            remote.append(pltpu.make_async_remote_copy(
                src_ref=srcs[a] if kind.startswith("gather") else srcs[a].at[there],
                dst_ref=lands[a].at[there if arriving else here],
                send_sem=send_sems.at[a * _COPIES + k], recv_sem=recv_sems.at[a * _COPIES + k],
                device_id=peer, device_id_type=MESH))
    return remote, local


def _exchange_start(items, name, deps=()):
    kinds = [kind for kind, _ in items]
    srcs = [a for _, a in items]
    n = len(items)
    lands = [lax.empty(_landing_shape(kind, a), a.dtype) for kind, a in items]

    def body(*refs):
        send_sems, recv_sems, local_sems = refs[2 * n + len(deps):2 * n + len(deps) + 3]
        remote, local = _exchange_copies(kinds, refs[:n], refs[n:2 * n], send_sems, recv_sems, local_sems, False)
        for cp in local + remote:
            cp.start()
        refs[-1][...] = jnp.zeros_like(refs[-1])

    held = [pltpu.HBM(a.shape, a.dtype) for a in srcs + lands]
    res = pl.pallas_call(
        body, name=name,
        out_shape=(pltpu.SemaphoreType.DMA((n * _COPIES,)), pltpu.SemaphoreType.DMA((n * _COPIES,)),
                   pltpu.SemaphoreType.DMA((n,)), *held, S((8, 128), F32)),
        in_specs=[_HBM] * (2 * n) + [_ANY] * len(deps),
        out_specs=(_SEM, _SEM, _SEM, *[_HBM] * (2 * n), _whole()),
        input_output_aliases={i: 3 + i for i in range(2 * n)},
        compiler_params=pltpu.CompilerParams(has_side_effects=_EFFECT),
    )(*[pltpu.with_memory_space_constraint(a, pltpu.HBM) for a in srcs + lands], *deps)
    return (kinds, res[:3], res[3:3 + 2 * n]), res[-1]


def _exchange_wait(handle, after, name):
    kinds, sems, held = handle
    n = len(kinds)

    def body(*refs):
        send_sems, recv_sems, local_sems = refs[2 * n:2 * n + 3]
        remote, local = _exchange_copies(kinds, refs[:n], refs[n:2 * n], send_sems, recv_sems, local_sems, True)
        for cp in remote:
            cp.wait_send()
            cp.wait_recv()
        for cp in local:
            cp.wait()

    res = pl.pallas_call(
        body, name=name, out_shape=[pltpu.HBM(a.shape, a.dtype) for a in held],
        in_specs=[_HBM] * (2 * n) + [_SEM] * 3 + [_ANY] * len(after), out_specs=[_HBM] * (2 * n),
        input_output_aliases={i: i for i in range(2 * n)},
        compiler_params=pltpu.CompilerParams(has_side_effects=_EFFECT),
    )(*held, *sems, *after)
    return res[n:]


_CHIPS = [(0, 0), (0, 1), (1, 0), (1, 1)]
_N_CHIPS = len(_CHIPS)


def _sibling_forward(lands, name):
    n = len(lands)

    def body(*refs):
        ins, outs = refs[:n], refs[n:2 * n]
        send_sems, recv_sems = refs[2 * n:]
        x, y, c = _place()
        sent, arriving = [], []
        for a in range(n):
            for j, (fx, fy, _) in enumerate(_CHIP_FLIPS):
                chip = (1 - x if fx else x, 1 - y if fy else y)
                sems = dict(send_sem=send_sems.at[a * 3 + j], recv_sem=recv_sems.at[a * 3 + j],
                            device_id=(x, y, 1 - c), device_id_type=MESH)
                mine, theirs = _index(*chip, c), _index(*chip, 1 - c)
                sent.append(pltpu.make_async_remote_copy(src_ref=ins[a].at[mine], dst_ref=outs[a].at[mine], **sems))
                arriving.append(pltpu.make_async_remote_copy(src_ref=ins[a].at[theirs], dst_ref=outs[a].at[theirs],
                                                             **sems))
        for cp in sent:
            cp.start()
        for cp in sent:
            cp.wait_send()
        for cp in arriving:
            cp.wait_recv()

    return pl.pallas_call(
        body, name=name, in_specs=[_ANY] * n, out_specs=[_ANY] * n,
        out_shape=[S(a.shape, a.dtype) for a in lands], input_output_aliases={a: a for a in range(n)},
        scratch_shapes=[pltpu.SemaphoreType.DMA((3 * n,)), pltpu.SemaphoreType.DMA((3 * n,))],
    )(*lands)


def _sibling_swap(arrs, name):
    n = len(arrs)

    def body(*refs):
        ins, outs = refs[:n], refs[n:2 * n]
        send_sems, recv_sems = refs[2 * n:]
        x, y, c = _place()
        copies = []
        for a in range(n):
            for q, chip in enumerate(_CHIPS):
                copies.append(pltpu.make_async_remote_copy(
                    src_ref=ins[a].at[_index(*chip, 1 - c)], dst_ref=outs[a].at[q],
                    send_sem=send_sems.at[a * _N_CHIPS + q], recv_sem=recv_sems.at[a * _N_CHIPS + q],
                    device_id=(x, y, 1 - c), device_id_type=MESH))
        for cp in copies:
            cp.start()
        for cp in copies:
            cp.wait_send()
            cp.wait_recv()

    return pl.pallas_call(
        body, name=name, in_specs=[_ANY] * n, out_specs=[_ANY] * n,
        out_shape=[S((_N_CHIPS,) + a.shape[1:], a.dtype) for a in arrs],
        scratch_shapes=[pltpu.SemaphoreType.DMA((_N_CHIPS * n,)), pltpu.SemaphoreType.DMA((_N_CHIPS * n,))],
    )(*arrs)


def _add_partials(mine, theirs, *, tr, name):
    _, r, c = mine.shape

    def body(mine_ref, theirs_ref, out_ref):
        core = lax.axis_index("c")
        own = jnp.where(core == 0, mine_ref[0].astype(F32), mine_ref[1].astype(F32))
        out_ref[0] = (own + theirs_ref[0].astype(F32)).astype(out_ref.dtype)

    return pl.pallas_call(
        body, name=name, grid=(_N_CHIPS, r // tr),
        in_specs=[pl.BlockSpec((2, tr, c), lambda q, i: (q, i, 0)), pl.BlockSpec((1, tr, c), lambda q, i: (q, i, 0))],
        out_specs=pl.BlockSpec((1, tr, c), lambda q, i: (q, i, 0)),
        out_shape=S((_N_CHIPS, r, c), mine.dtype),
        compiler_params=_params(("arbitrary", "arbitrary")),
    )(mine, theirs)


def _adamw(w, g, m, v):
    m = ADAM_B1 * m + (1.0 - ADAM_B1) * g
    v = ADAM_B2 * v + (1.0 - ADAM_B2) * (g * g)
    m_hat = m / (1.0 - ADAM_B1 ** ADAM_STEP)
    v_hat = v / (1.0 - ADAM_B2 ** ADAM_STEP)
    return -ADAM_LR * (m_hat / (jnp.sqrt(v_hat) + ADAM_EPS) + ADAM_WD * w), m, v


def _sum_devices(ref, rows):
    total = ref[0, rows, :].astype(F32)
    for s in range(1, ref.shape[0]):
        total = total + ref[s, rows, :].astype(F32)
    return total


def _adam_big(recv, w, m, v, *, tr, name):
    r, c = w.shape

    def body(recv_ref, w_ref, m_ref, v_ref, g_ref, d_ref, m2_ref, v2_ref):
        g = _sum_devices(recv_ref, slice(None))
        g_ref[...] = g
        d_ref[...], m2_ref[...], v2_ref[...] = _adamw(w_ref[...], g, m_ref[...], v_ref[...])

    blk = pl.BlockSpec((tr, c), lambda i: (i, 0))
    return pl.pallas_call(
        body, name=name, grid=(r // tr,),
        in_specs=[pl.BlockSpec((recv.shape[0], tr, c), lambda i: (0, i, 0)), blk, blk, blk],
        out_specs=[blk] * 4, out_shape=[S((r, c), F32)] * 4,
        compiler_params=_params(("arbitrary",)),
    )(recv, w, m, v)


_REPLICATED = {"even_pre_g": (0, 0, 1), "even_mem_g": (0, 8, 1), "even_post_g": (0, 16, 1),
               "even_a_ln_g": (1, 0, 1), "even_a_ln_b": (1, 8, 1),
               "even_a_ws": (2, 0, NH * CHUNK), "even_a_bs": (2, NH * CHUNK, NH),
               "odd_c_wgrp": (3, 0, NH * GRP)}
_SHARDED = {"odd_pre_g": (4, 0, 1), "odd_mem_g": (4, 8, 1), "odd_post_g": (4, 16, 1),
            "even_b_conv": (5, 0, 3), "odd_c_scale": (5, 8, 1), "odd_d_dw_w": (5, 16, CONF_K),
            "odd_d_dw_b": (5, 48, 1), "odd_d_ln_g": (5, 56, 1), "odd_d_ln_b": (5, 64, 1), "odd_d_pw_b": (5, 72, 1)}
_SMALL = {**_REPLICATED, **_SHARDED}
_SMALL_ROWS = {0: 24, 1: 16, 2: NH * CHUNK + 8, 3: NH * GRP, 4: 24, 5: 80}


def _adam_small(sources, wmv):
    names = list(_SMALL)
    ns = len(sources)

    def body(*refs):
        src = refs[:ns]
        ins = refs[ns:ns + 3 * len(names)]
        outs = refs[ns + 3 * len(names):]
        outs[-1][...] = _sum_devices(src[-1], slice(0, 1))
        for i, nm in enumerate(names):
            a, row0, rows = _SMALL[nm]
            g = _sum_devices(src[a], slice(row0, row0 + rows))
            w_ref, m_ref, v_ref = ins[3 * i:3 * i + 3]
            g_ref, d_ref, m2_ref, v2_ref = outs[4 * i:4 * i + 4]
            g_ref[...] = g
            d_ref[...], m2_ref[...], v2_ref[...] = _adamw(w_ref[...], g, m_ref[...], v_ref[...])

    flat = [t for nm in names for t in wmv[nm]]
    out_shape = [S(wmv[nm][0].shape, F32) for nm in names for _ in range(4)] + [S((1, HD), F32)]
    res = pl.pallas_call(
        body, name="adam_small", in_specs=[_whole()] * (ns + len(flat)), out_specs=[_whole()] * len(out_shape),
        out_shape=out_shape, compiler_params=_params(),
    )(*sources, *flat)
    return {nm: tuple(res[4 * i:4 * i + 4]) for i, nm in enumerate(names)}, res[-1]


_WEIGHTS = ["even_pre_g", "even_w_in", "even_a_ln_g", "even_a_ln_b", "even_a_ws", "even_a_bs", "even_b_conv",
            "even_mem_g", "even_w_kv", "even_w_out", "even_post_g", "odd_pre_g", "odd_w_in", "odd_c_wgrp",
            "odd_c_scale", "odd_d_dw_w", "odd_d_dw_b", "odd_d_ln_g", "odd_d_ln_b", "odd_d_pw_w", "odd_d_pw_b",
            "odd_mem_g", "odd_w_kv", "odd_w_out", "odd_post_g"]
_TRANSPOSED = ["even_w_in", "odd_w_in"]
_BIG = _TRANSPOSED + ["even_w_kv", "even_w_out", "odd_w_kv", "odd_w_out", "odd_d_pw_w"]
_BIG_TILE_ROWS = {"even_w_in": 400, "odd_w_in": 304, "even_w_kv": 128, "even_w_out": 128, "odd_w_kv": 128,
                  "odd_w_out": 128, "odd_d_pw_w": 96}


def _view2d(a, transposed):
    a = a[0]
    if a.ndim == 1:
        return a[None]
    if transposed:
        return a.T
    return a.reshape(-1, a.shape[-1])


def _rows8(a):
    return _pad_rows(a, -(-a.shape[0] // 8) * 8)


def _pack_rows(parts):
    return jnp.concatenate([_rows8(p) for p in parts], axis=0)


def _unshard_cols(a):
    return jnp.transpose(a, (1, 0, 2)).reshape(a.shape[1], N_DEV * a.shape[2])


def _shard_cols(a):
    return jnp.transpose(a.reshape(a.shape[0], N_DEV, a.shape[1] // N_DEV), (1, 0, 2))


def _rows_of(a):
    return a.reshape(-1, a.shape[-1])


_GROUPS = {"odd": (["odd_w_in", "odd_w_out", "odd_w_kv", "odd_d_pw_w"], [3], []),
           "even_rest": (["even_w_out", "even_w_kv"], [1, 2], []),
           "even_in": (["even_w_in"], [], [4, 5]),
           "even_gains": ([], [0], [])}


_TWO_LEVEL = ("even_in",)


class _MeshExchange:
    def __init__(self, shard):
        self.shard = shard
        self.handles = {}

    def first(self):
        shard = self.shard
        packs = [_pack_rows([shard[nm] for nm in _SHARDED if _SHARDED[nm][0] == a]) for a in (4, 5)]
        w_in, p128, p96 = _all_gather([shard["even_w_in"].astype(BF16)] + packs, "gather_first")
        w = {nm: shard[nm] for nm in _REPLICATED}
        w["even_a_ws"] = w["even_a_ws"].reshape(NH, CHUNK, CHUNK)
        w["odd_c_wgrp"] = w["odd_c_wgrp"].reshape(NH, GRP, GRP)
        w["even_w_in"] = _rows_of(w_in)
        full_packs = {4: _unshard_cols(p128), 5: _unshard_cols(p96)}
        for nm, (a, row0, rows) in _SHARDED.items():
            w[nm] = full_packs[a][row0:row0 + rows]
        later = lambda names: [("gather_chips", shard[nm].astype(BF16)) for nm in names]
        self.handles["w_even"], token = _exchange_start(later(["even_w_kv", "even_w_out"]), "gather_even_start",
                                                        deps=(w_in,))
        self.handles["w_odd"], token = _exchange_start(later(["odd_w_in", "odd_w_kv", "odd_w_out", "odd_d_pw_w"]),
                                                       "gather_odd_start", deps=(token,))
        return w, (token,)

    def even_rest(self, after):
        landed = _exchange_wait(self.handles.pop("w_even"), (after,), "gather_even_wait")
        kv, out = _sibling_forward(landed, "forward_even")
        return {"even_w_kv": _rows_of(kv), "even_w_out": _rows_of(out)}

    def odd(self, after):
        landed = _exchange_wait(self.handles.pop("w_odd"), (after,), "gather_odd_wait")
        w_in, kv, out, pw = _sibling_forward(landed, "forward_odd")
        return {"odd_w_in": _rows_of(w_in), "odd_w_kv": _rows_of(kv), "odd_w_out": _rows_of(out),
                "odd_d_pw_w": _rows_of(pw)}

    def send(self, group, g):
        big, replicated, sharded = _GROUPS[group]
        by_owner = [g[nm].reshape(N_DEV, -1, g[nm].shape[-1]) for nm in big]
        if group in _TWO_LEVEL:
            theirs = _sibling_swap(by_owner, "swap_" + group)
            items = [("scatter_chips", _add_partials(a, b, tr=_BIG_TILE_ROWS[nm], name="chip_sum_" + nm))
                     for nm, a, b in zip(big, by_owner, theirs)]
        else:
            items = [("scatter", a) for a in by_owner]
        items += [("gather", _pack_rows([g[nm] for nm in _REPLICATED if _REPLICATED[nm][0] == a]))
                  for a in replicated]
        items += [("scatter", _shard_cols(_pack_rows([g[nm] for nm in _SHARDED if _SHARDED[nm][0] == a])))
                  for a in sharded]
        if group == "odd":
            items.append(("gather", _rows8(g["loss"])))
        self.handles[group], token = _exchange_start(items, "send_" + group + "_start")
        return (token,)

    def receive(self, group, after):
        after = after if isinstance(after, tuple) else (after,)
        return _exchange_wait(self.handles.pop(group), after, "send_" + group + "_wait")


def kernel(x, mem, even_pre_g, even_w_in, even_a_ln_g, even_a_ln_b, even_a_ws, even_a_bs, even_b_conv, even_mem_g, even_w_kv, even_w_out, even_post_g, odd_pre_g, odd_w_in, odd_c_wgrp, odd_c_scale, odd_d_dw_w, odd_d_dw_b, odd_d_ln_g, odd_d_ln_b, odd_d_pw_w, odd_d_pw_b, odd_mem_g, odd_w_kv, odd_w_out, odd_post_g, loss_target, m_even_pre_g, m_even_w_in, m_even_a_ln_g, m_even_a_ln_b, m_even_a_ws, m_even_a_bs, m_even_b_conv, m_even_mem_g, m_even_w_kv, m_even_w_out, m_even_post_g, m_odd_pre_g, m_odd_w_in, m_odd_c_wgrp, m_odd_c_scale, m_odd_d_dw_w, m_odd_d_dw_b, m_odd_d_ln_g, m_odd_d_ln_b, m_odd_d_pw_w, m_odd_d_pw_b, m_odd_mem_g, m_odd_w_kv, m_odd_w_out, m_odd_post_g, v_even_pre_g, v_even_w_in, v_even_a_ln_g, v_even_a_ln_b, v_even_a_ws, v_even_a_bs, v_even_b_conv, v_even_mem_g, v_even_w_kv, v_even_w_out, v_even_post_g, v_odd_pre_g, v_odd_w_in, v_odd_c_wgrp, v_odd_c_scale, v_odd_d_dw_w, v_odd_d_dw_b, v_odd_d_ln_g, v_odd_d_ln_b, v_odd_d_pw_w, v_odd_d_pw_b, v_odd_mem_g, v_odd_w_kv, v_odd_w_out, v_odd_post_g):
    given = dict(locals())
    view = lambda nm, kind: _view2d(given[kind + nm], nm in _TRANSPOSED)
    shard = {nm: view(nm, "") for nm in _WEIGHTS}
    wmv = {nm: (shard[nm], view(nm, "m_"), view(nm, "v_")) for nm in _WEIGHTS}

    ex = _MeshExchange(shard)
    grad_x, last = _step(x[0], mem[0], loss_target[0], ex)

    res = {}

    def update(group, after):
        names = _GROUPS[group][0]
        landed = ex.receive(group, after)
        for nm, recv in zip(names, landed):
            res[nm] = _adam_big(recv, *wmv[nm], tr=_BIG_TILE_ROWS[nm], name="adam_" + nm)
        return landed[len(names):]

    c192, losses = update("odd", last)
    c768, c128 = update("even_rest", res["odd_d_pw_w"][0])
    a128, a96 = update("even_in", res["even_w_kv"][0])
    (c1024,) = update("even_gains", res["even_w_in"][0])
    small, loss = _adam_small([c1024, c768, c128, c192, a128, a96, losses], {nm: wmv[nm] for nm in _SMALL})
    res.update(small)
    total = loss[0, 0]
    back = lambda nm, a: (a.T if nm in _TRANSPOSED else a).reshape(given[nm].shape)
    outs = [[back(nm, res[nm][i]) for nm in _WEIGHTS] for i in range(4)]
    return (total, grad_x[None], *outs[0], *outs[1], *outs[2], *outs[3])
```

```python
import functools

import jax
import jax.numpy as jnp
from jax import lax
from jax.experimental import pallas as pl
from jax.experimental.pallas import tpu as pltpu

F32 = jnp.float32
BF16 = jnp.bfloat16
S = jax.ShapeDtypeStruct
MESH = pl.DeviceIdType.MESH
AXES = ("x", "y", "c")
N_DEV = 8

D = 1024
BW = 768
XA = 512
HD = 128
NH = 4
MIX = 2048
CHUNK = 128
GRP = 192
N_MEM = 256
CONF_K = 31
EPS = 1e-6
HALO = 32
POOL_WINDOWS = (2, 4, 8, 16)
TM_FWD_EVEN = 512
TM_FWD_ODD = 256
TM_BWD_EVEN = 256
TM_BWD_ODD = 256
RB = 16

E_U, E_V, E_BG, E_CG, E_XIN, E_Q, E_GATE = 0, 768, 1536, 2304, 3072, 3840, 4352
EVEN_IN = 6400
O_ZC, O_GA, O_GB, O_Q, O_GATE = 0, 768, 1536, 2304, 2816
ODD_IN = 4864

ADAM_LR, ADAM_B1, ADAM_B2, ADAM_EPS, ADAM_WD, ADAM_STEP = 0.001, 0.9, 0.999, 1e-08, 0.01, 10

VMEM_LIMIT_V7X = 56 * 1024 * 1024
VMEM_LIMIT_ODD_BWD_V7X = 62 * 1024 * 1024


def _params(sem=None):
    return pltpu.CompilerParams(dimension_semantics=sem, vmem_limit_bytes=VMEM_LIMIT_V7X)


def _dot(a, b):
    return jnp.dot(a, b, preferred_element_type=F32)


def _dot_nt(a, b):
    return lax.dot_general(a, b, (((1,), (1,)), ((), ())), preferred_element_type=F32)


def _dot_tn(a, b):
    return lax.dot_general(a, b, (((0,), (0,)), ((), ())), preferred_element_type=F32)


def _sigmoid(z):
    return 1.0 / (1.0 + jnp.exp(-z))


def _rowmean(a):
    return jnp.mean(a, axis=-1, keepdims=True)


def _colsum(a):
    return jnp.sum(a, axis=0, keepdims=True)


def _ln_stats(v):
    mu = _rowmean(v)
    vc = v - mu
    rs = lax.rsqrt(_rowmean(vc * vc) + EPS)
    return vc * rs, rs


def _ln_bwd(dn, vh, rs, g):
    dvh = dn * g
    return rs * (dvh - _rowmean(dvh) - vh * _rowmean(dvh * vh))


def _group_masks():
    col = lax.broadcasted_iota(jnp.int32, (1, BW), 1)
    return [((col >= GRP * h) & (col < GRP * (h + 1))).astype(F32) for h in range(NH)]


def _full(shape):
    nd = len(shape)
    return pl.BlockSpec(shape, lambda *_: (0,) * nd)


def _whole():
    return pl.BlockSpec(memory_space=pltpu.VMEM)


_ANY = pl.BlockSpec(memory_space=pl.ANY)


def _after(body, n_in, deps):
    def ordered(*refs):
        return body(*refs[:n_in], *refs[n_in + len(deps):])
    return ordered


def _rms_matmul(x, g, w, *, tm, name, transposed=False, out_dtype=F32, deps=()):
    t, d = x.shape
    n = w.shape[0] if transposed else w.shape[1]

    def body(x_ref, g_ref, w_ref, p_ref, h_ref):
        xv = x_ref[...]
        r = lax.rsqrt(_rowmean(xv * xv) + EPS)
        h = (xv * r * g_ref[...]).astype(BF16)
        h_ref[...] = h
        p_ref[...] = (_dot_nt(h, w_ref[...]) if transposed else _dot(h, w_ref[...])).astype(out_dtype)

    return pl.pallas_call(
        _after(body, 3, deps), name=name, grid=(t // tm,),
        in_specs=[pl.BlockSpec((tm, d), lambda i: (i, 0)), _whole(), _whole()] + [_ANY] * len(deps),
        out_specs=[pl.BlockSpec((tm, n), lambda i: (i, 0)), pl.BlockSpec((tm, d), lambda i: (i, 0))],
        out_shape=[S((t, n), out_dtype), S((t, d), BF16)],
        compiler_params=_params(("arbitrary",)),
    )(x, g, w, *deps)


def _nt_matmul_rms_bwd(dp, w, x, g, dres, *, tm, name, transposed=False, deps=()):
    t, n = dp.shape
    d = x.shape[1]

    def body(dp_ref, w_ref, x_ref, g_ref, dres_ref, dx_ref, dg_ref):
        @pl.when(pl.program_id(0) == 0)
        def _():
            dg_ref[...] = jnp.zeros_like(dg_ref)

        dh = _dot(dp_ref[...], w_ref[...]) if transposed else _dot_nt(dp_ref[...], w_ref[...])
        xv = x_ref[...]
        r = lax.rsqrt(_rowmean(xv * xv) + EPS)
        xh = xv * r
        dg_ref[...] += _colsum(dh * xh)
        dxh = dh * g_ref[...]
        dx_ref[...] = dres_ref[...] + r * (dxh - xh * _rowmean(dxh * xh))

    return pl.pallas_call(
        _after(body, 5, deps), name=name, grid=(t // tm,),
        in_specs=[pl.BlockSpec((tm, n), lambda i: (i, 0)), _whole(), pl.BlockSpec((tm, d), lambda i: (i, 0)),
                  _whole(), pl.BlockSpec((tm, d), lambda i: (i, 0))] + [_ANY] * len(deps),
        out_specs=[pl.BlockSpec((tm, d), lambda i: (i, 0)), pl.BlockSpec((1, d), lambda i: (0, 0))],
        out_shape=[S((t, d), F32), S((1, d), F32)],
        compiler_params=_params(("arbitrary",)),
    )(dp, w, x, g, dres, *deps)


def _tn_matmul(a, b, *, tmc, tk, out_dtype, name, deps=()):
    t, m = a.shape
    n = b.shape[1]
    nk = t // tk

    def body(a_ref, b_ref, o_ref, acc_ref):
        k = pl.program_id(1)

        @pl.when(k == 0)
        def _():
            acc_ref[...] = jnp.zeros_like(acc_ref)

        acc_ref[...] += _dot_tn(a_ref[...], b_ref[...])

        @pl.when(k == nk - 1)
        def _():
            o_ref[...] = acc_ref[...].astype(out_dtype)

    return pl.pallas_call(
        _after(body, 2, deps), name=name, grid=(m // tmc, nk),
        in_specs=[pl.BlockSpec((tk, tmc), lambda j, k: (k, j)), pl.BlockSpec((tk, n), lambda j, k: (k, 0))]
        + [_ANY] * len(deps),
        out_specs=pl.BlockSpec((tmc, n), lambda j, k: (j, 0)),
        out_shape=S((m, n), out_dtype),
        scratch_shapes=[pltpu.VMEM((tmc, n), F32)],
        compiler_params=_params(("arbitrary", "arbitrary")),
    )(a, b, *deps)


def _silu_parts(gt):
    sg = _sigmoid(gt)
    return gt * sg, sg * (1.0 + gt * (1.0 - sg))


def _attn_head(q_b, k_b, v_b):
    s = _dot_nt(q_b, k_b) * (HD ** -0.5)
    e = jnp.exp(s - jnp.max(s, axis=-1, keepdims=True))
    prob = e / jnp.sum(e, axis=-1, keepdims=True)
    return prob, _dot(prob.astype(BF16), v_b)


def _rms_residual(x, o, g):
    r = lax.rsqrt(_rowmean(o * o) + EPS)
    return x + o * r * g


def _rms_post_bwd(dres, o, g):
    r = lax.rsqrt(_rowmean(o * o) + EPS)
    oh = o * r
    doh = dres * g
    return r * (doh - oh * _rowmean(doh * oh)), _colsum(dres * oh)


LANE = 128
_TILE_GROUPS = [sorted({LANE * j // GRP, (LANE * j + LANE - 1) // GRP}) for j in range(BW // LANE)]


def _tile(j):
    return slice(LANE * j, LANE * (j + 1))


def _low_lanes():
    return lax.broadcasted_iota(jnp.int32, (1, LANE), 1) < GRP - LANE


def _by_group(fn):
    tiles = []
    for j, groups in enumerate(_TILE_GROUPS):
        if len(groups) == 1:
            tiles.append(fn(groups[0], j))
        else:
            tiles.append(jnp.where(_low_lanes(), fn(groups[0], j), fn(groups[1], j)))
    return jnp.concatenate(tiles, axis=1)


def _sgu_chunk(vn_b, ws_ref, bmap_ref):
    return bmap_ref[...] + _by_group(lambda h, j: _dot(ws_ref[h], vn_b[:, _tile(j)]))


def _shift_copies(buf, sh):
    n = buf.shape[0] - 8
    for b in range(1, 8):
        sh[b - 1, pl.ds(0, n), :] = buf[pl.ds(b, n), :]


def _loop_rows(rows, step, fn, carry=0, unrolled=True):
    if unrolled:
        for r0 in range(0, rows, step):
            carry = fn(r0, carry)
        return carry

    def body(j, c):
        return fn(pl.multiple_of(j * step, step), c)
    return lax.fori_loop(0, rows // step, body, carry)


def _rows_at(buf, sh, r0, off):
    b = off % 8
    if b == 0 or sh is None:
        return buf[pl.ds(r0 + off, 32), :]
    return sh[b - 1, pl.ds(r0 + (off - b), 32), :]


def _tap_sum(buf, sh, w_ref, r0, taps, causal):
    acc = None
    for k in range(taps):
        off = HALO - (taps - 1 - k) if causal else taps - 1 - k
        term = w_ref[k:k + 1, :] * _rows_at(buf, sh, r0, off)
        acc = term if acc is None else acc + term
    return acc


def _fold8(a):
    return a[0:8] + a[8:16] + a[16:24] + a[24:32]


def _conv_back(buf, sh, w_ref, acc_ref, z, r0, taps):
    dz = None
    for k in range(taps):
        ahead = _rows_at(buf, sh, r0, taps - 1 - k)
        term = w_ref[k:k + 1, :] * ahead
        dz = term if dz is None else dz + term
        acc_ref[k * 8:(k + 1) * 8, :] += _fold8(z * ahead)
    return dz


def _halo_spec(n, nt, reverse, tm):
    per = tm // HALO
    if reverse:
        return pl.BlockSpec((HALO, n), lambda i: (jnp.maximum((nt - 1 - i) * per - 1, 0), 0))
    return pl.BlockSpec((HALO, n), lambda i: (jnp.maximum(i * per - 1, 0), 0))


def _even_fwd(x, p, lng, lnb, ws, bmap, wc, kv, wout, pg):
    t = x.shape[0]
    tm = min(TM_FWD_EVEN, t)
    nt = t // tm

    def body(x_ref, p_ref, ph_ref, lng_ref, lnb_ref, ws_ref, bmap_ref, wc_ref, kv_ref, wout_ref, pg_ref,
             o_ref, x1_ref, ybuf, cbuf):
        i = pl.program_id(0)
        vh, _ = _ln_stats(p_ref[:, E_V:E_V + BW])
        vn = vh * lng_ref[...] + lnb_ref[...]
        for c in range(tm // CHUNK):
            sl = slice(c * CHUNK, (c + 1) * CHUNK)
            sg = _sgu_chunk(vn[sl].astype(BF16), ws_ref, bmap_ref)
            gate, _ = _silu_parts(p_ref[sl, E_GATE:E_GATE + BW])
            ybuf[sl, 0:BW] = (p_ref[sl, E_U:E_U + BW] * sg * gate).astype(BF16)

        cbuf[0:HALO] = jnp.where(i > 0, ph_ref[:, E_CG:E_CG + BW] * ph_ref[:, E_XIN:E_XIN + BW], 0.0)
        cbuf[HALO:HALO + tm] = p_ref[:, E_CG:E_CG + BW] * p_ref[:, E_XIN:E_XIN + BW]
        for r0 in range(0, tm, 32):
            sl = slice(r0, r0 + 32)
            cv = _tap_sum(cbuf, None, wc_ref, r0, 3, True)
            gate, _ = _silu_parts(p_ref[sl, E_GATE + BW:E_GATE + 2 * BW])
            ybuf[sl, BW:2 * BW] = (p_ref[sl, E_BG:E_BG + BW] * cv * gate).astype(BF16)

        for h in range(NH):
            qs = slice(E_Q + h * HD, E_Q + (h + 1) * HD)
            _, yx = _attn_head(p_ref[:, qs].astype(BF16), kv_ref[:, h * HD:(h + 1) * HD],
                               kv_ref[:, XA + h * HD:XA + (h + 1) * HD])
            gs = slice(E_GATE + 2 * BW + h * HD, E_GATE + 2 * BW + (h + 1) * HD)
            gate, _ = _silu_parts(p_ref[:, gs])
            ybuf[:, 2 * BW + h * HD:2 * BW + (h + 1) * HD] = (yx * gate).astype(BF16)

        o = _dot(ybuf[...], wout_ref[...])
        o_ref[...] = o
        x1_ref[...] = _rms_residual(x_ref[...], o, pg_ref[...])

    tile = lambda n: pl.BlockSpec((tm, n), lambda i: (i, 0))
    return pl.pallas_call(
        body, name="even_fwd", grid=(nt,),
        in_specs=[tile(D), tile(EVEN_IN), _halo_spec(EVEN_IN, nt, False, tm)] + [_whole()] * 8,
        out_specs=[tile(D), tile(D)],
        out_shape=[S((t, D), F32), S((t, D), F32)],
        scratch_shapes=[pltpu.VMEM((tm, MIX), BF16), pltpu.VMEM((tm + HALO, BW), F32)],
        compiler_params=_params(("arbitrary",)),
    )(x, p, p, lng, lnb, ws, bmap, wc, kv, wout, pg)


def _even_bwd(dres, o, p, lng, lnb, ws, wst, bmap, wc, kv, wout, pg):
    t = dres.shape[0]
    tm = min(TM_BWD_EVEN, t)
    nt = t // tm

    def body(dres_ref, o_ref, p_ref, ph_ref, lng_ref, lnb_ref, ws_ref, wst_ref, bmap_ref, wc_ref, kv_ref, wout_ref,
             pg_ref, dp_ref, y_ref, do_ref, dpg_ref, dws_ref, dbs_ref, dlng_ref, dlnb_ref, dwc_ref, dkv_ref,
             dy, cbuf, gbuf, dconv, carry, dvn, dbmap, wacc):
        i = pl.program_id(0)
        ti = nt - 1 - i
        masks = _group_masks()

        @pl.when(i == 0)
        def _():
            for ref in (dpg_ref, dws_ref, dlng_ref, dlnb_ref, dkv_ref, dbmap, wacc):
                ref[...] = jnp.zeros_like(ref)

        do, dpg = _rms_post_bwd(dres_ref[...], o_ref[...], pg_ref[...])
        dpg_ref[...] += dpg
        do_b = do.astype(BF16)
        do_ref[...] = do_b
        dy[...] = _dot_nt(do_b, wout_ref[...])

        vh, rs = _ln_stats(p_ref[:, E_V:E_V + BW])
        vn = vh * lng_ref[...] + lnb_ref[...]
        for c in range(tm // CHUNK):
            sl = slice(c * CHUNK, (c + 1) * CHUNK)
            vn_b = vn[sl].astype(BF16)
            sg = _sgu_chunk(vn_b, ws_ref, bmap_ref)
            u = p_ref[sl, E_U:E_U + BW]
            gate, dgate = _silu_parts(p_ref[sl, E_GATE:E_GATE + BW])
            dyc = dy[sl, 0:BW]
            ya = u * sg
            y_ref[sl, 0:BW] = (ya * gate).astype(BF16)
            dp_ref[sl, E_GATE:E_GATE + BW] = (dyc * ya * dgate).astype(BF16)
            dya = dyc * gate
            dp_ref[sl, E_U:E_U + BW] = (dya * sg).astype(BF16)
            dsg = dya * u
            dbmap[...] += dsg
            dsg_b = dsg.astype(BF16)
            for h in range(NH):
                total = None
                for j, heads in enumerate(_TILE_GROUPS):
                    if h in heads:
                        d_t = dsg_b[:, _tile(j)]
                        if len(heads) == 2:
                            d_t = jnp.where(_low_lanes() == (h == heads[0]), d_t, jnp.zeros_like(d_t))
                        part = _dot_nt(d_t, vn_b[:, _tile(j)])
                        total = part if total is None else total + part
                dws_ref[h] += total
            dvn[sl, :] = _by_group(lambda h, j: _dot(wst_ref[h], dsg_b[:, _tile(j)]))
        dn = dvn[...]
        dlng_ref[...] += _colsum(dn * vh)
        dlnb_ref[...] += _colsum(dn)
        dp_ref[:, E_V:E_V + BW] = _ln_bwd(dn, vh, rs, lng_ref[...]).astype(BF16)

        cbuf[0:HALO] = jnp.where(ti > 0, ph_ref[:, E_CG:E_CG + BW] * ph_ref[:, E_XIN:E_XIN + BW], 0.0)
        cbuf[HALO:HALO + tm] = p_ref[:, E_CG:E_CG + BW] * p_ref[:, E_XIN:E_XIN + BW]
        for r0 in range(0, tm, 32):
            sl = slice(r0, r0 + 32)
            cv = _tap_sum(cbuf, None, wc_ref, r0, 3, True)
            gate, dgate = _silu_parts(p_ref[sl, E_GATE + BW:E_GATE + 2 * BW])
            bg = p_ref[sl, E_BG:E_BG + BW]
            dyc = dy[sl, BW:2 * BW]
            yb = bg * cv
            y_ref[sl, BW:2 * BW] = (yb * gate).astype(BF16)
            dp_ref[sl, E_GATE + BW:E_GATE + 2 * BW] = (dyc * yb * dgate).astype(BF16)
            dyb = dyc * gate
            dp_ref[sl, E_BG:E_BG + BW] = (dyb * cv).astype(BF16)
            dconv[sl, :] = dyb * bg
        gbuf[0:tm] = dconv[...]
        gbuf[tm:tm + HALO] = jnp.where(i > 0, carry[...], 0.0)
        carry[...] = dconv[0:HALO]
        for r0 in range(0, tm, 32):
            sl = slice(r0, r0 + 32)
            dc = _conv_back(gbuf, None, wc_ref, wacc, cbuf[HALO + r0:HALO + r0 + 32, :], r0, 3)
            dp_ref[sl, E_CG:E_CG + BW] = (dc * p_ref[sl, E_XIN:E_XIN + BW]).astype(BF16)
            dp_ref[sl, E_XIN:E_XIN + BW] = (dc * p_ref[sl, E_CG:E_CG + BW]).astype(BF16)

        for h in range(NH):
            qs = slice(E_Q + h * HD, E_Q + (h + 1) * HD)
            ks = slice(h * HD, (h + 1) * HD)
            vs = slice(XA + h * HD, XA + (h + 1) * HD)
            gs = slice(E_GATE + 2 * BW + h * HD, E_GATE + 2 * BW + (h + 1) * HD)
            ys = slice(2 * BW + h * HD, 2 * BW + (h + 1) * HD)
            q_b = p_ref[:, qs].astype(BF16)
            prob, yx = _attn_head(q_b, kv_ref[:, ks], kv_ref[:, vs])
            gate, dgate = _silu_parts(p_ref[:, gs])
            dyc = dy[:, ys]
            y_ref[:, ys] = (yx * gate).astype(BF16)
            dp_ref[:, gs] = (dyc * yx * dgate).astype(BF16)
            dyx_b = (dyc * gate).astype(BF16)
            dprob = _dot_nt(dyx_b, kv_ref[:, vs])
            dkv_ref[:, vs] += _dot_tn(prob.astype(BF16), dyx_b)
            ds_b = (prob * (dprob - jnp.sum(dprob * prob, axis=-1, keepdims=True)) * (HD ** -0.5)).astype(BF16)
            dp_ref[:, qs] = _dot(ds_b, kv_ref[:, ks]).astype(BF16)
            dkv_ref[:, ks] += _dot_tn(ds_b, q_b)

        @pl.when(i == nt - 1)
        def _():
            for h in range(NH):
                dbs_ref[:, h * HD:(h + 1) * HD] = jnp.broadcast_to(
                    jnp.sum(dbmap[...] * masks[h], axis=-1, keepdims=True), (CHUNK, HD))
            for k in range(3):
                dwc_ref[k:k + 1, :] = _colsum(wacc[k * 8:(k + 1) * 8, :])
            dwc_ref[3:8, :] = jnp.zeros((5, BW), F32)
            causal = (lax.broadcasted_iota(jnp.int32, (CHUNK, CHUNK), 0)
                      >= lax.broadcasted_iota(jnp.int32, (CHUNK, CHUNK), 1))
            for h in range(NH):
                dws_ref[h] = jnp.where(causal, dws_ref[h], 0.0)

    rtile = lambda n: pl.BlockSpec((tm, n), lambda i: (nt - 1 - i, 0))
    outs = [S((t, EVEN_IN), BF16), S((t, MIX), BF16), S((t, D), BF16), S((1, D), F32), S((NH, CHUNK, CHUNK), F32),
            S((CHUNK, NH * HD), F32), S((1, BW), F32), S((1, BW), F32), S((8, BW), F32), S((N_MEM, 2 * XA), F32)]
    return pl.pallas_call(
        body, name="even_bwd", grid=(nt,),
        in_specs=[rtile(D), rtile(D), rtile(EVEN_IN), _halo_spec(EVEN_IN, nt, True, tm)] + [_whole()] * 9,
        out_specs=[rtile(EVEN_IN), rtile(MIX), rtile(D)] + [_full(s.shape) for s in outs[3:]],
        out_shape=outs,
        scratch_shapes=[pltpu.VMEM((tm, MIX), F32), pltpu.VMEM((tm + HALO, BW), F32), pltpu.VMEM((tm + HALO, BW), F32),
                        pltpu.VMEM((tm, BW), F32), pltpu.VMEM((HALO, BW), F32), pltpu.VMEM((tm, BW), F32),
                        pltpu.VMEM((CHUNK, BW), F32), pltpu.VMEM((3 * 8, BW), F32)],
        compiler_params=_params(("arbitrary",)),
    )(dres, o, p, p, lng, lnb, ws, wst, bmap, wc, kv, wout, pg)


def _pool_causal_levels(za, zb, zc, zd, tm):
    n = tm + HALO
    zb[pl.ds(8, n - 8), :] = za[pl.ds(8, n - 8), :] + za[pl.ds(7, n - 8), :]
    zc[pl.ds(16, n - 16), :] = zb[pl.ds(16, n - 16), :] + zb[pl.ds(14, n - 16), :]
    zd[pl.ds(24, n - 24), :] = zc[pl.ds(24, n - 24), :] + zc[pl.ds(20, n - 24), :]


def _pool_causal(za, zb, zc, zd, tm):
    _pool_causal_levels(za, zb, zc, zd, tm)
    s16 = zd[pl.ds(HALO, tm), :] + zd[pl.ds(HALO - 8, tm), :]
    return zb[pl.ds(HALO, tm), :], zc[pl.ds(HALO, tm), :], zd[pl.ds(HALO, tm), :], s16


def _pool_anticausal_levels(ea, eb, ec, ed, tm):
    n = tm + HALO
    eb[pl.ds(0, n - 8), :] = ea[pl.ds(0, n - 8), :] + ea[pl.ds(1, n - 8), :]
    ec[pl.ds(0, n - 16), :] = eb[pl.ds(0, n - 16), :] + eb[pl.ds(2, n - 16), :]
    ed[pl.ds(0, n - 24), :] = ec[pl.ds(0, n - 24), :] + ec[pl.ds(4, n - 24), :]


def _pool_weights(t0, rows):
    tf = (t0 + lax.broadcasted_iota(jnp.int32, (rows, 1), 0) + 1).astype(F32)
    inv = [jnp.broadcast_to(1.0 / jnp.minimum(tf, float(win)), (rows, LANE)) for win in POOL_WINDOWS]
    return _by_group(lambda g, j: inv[g])


_HALVES = (slice(0, BW // 2), slice(BW // 2, BW))


def _mix4(parts):
    return _by_group(lambda g, j: parts[g][:, _tile(j)])


def _odd_fwd(x1, tgt, p, wbd, cscale, dww, dwb, lng, lnb, wpw, pwb, kv, wout, pg):
    t = x1.shape[0]
    tm = min(TM_FWD_ODD, t)
    nt = t // tm

    def body(x_ref, tgt_ref, p_ref, ph_ref, wbd_ref, cs_ref, dww_ref, dwb_ref, lng_ref, lnb_ref, wpw_ref, pwb_ref,
             kv_ref, wout_ref, pg_ref, o_ref, dres_ref, loss_ref, conv_ref, ybuf, za, zb, zc, zd, gbuf, lacc, gsh):
        i = pl.program_id(0)

        @pl.when(i == 0)
        def _():
            lacc[...] = jnp.zeros_like(lacc)

        z = p_ref[:, O_ZC:O_ZC + BW]
        za[0:HALO] = jnp.where(i > 0, ph_ref[:, O_ZC:O_ZC + BW], 0.0)
        za[HALO:HALO + tm] = z
        pooled = _mix4(_pool_causal(za, zb, zc, zd, tm)) * _pool_weights(i * tm, tm) - z
        pooled_b = pooled.astype(BF16)
        for hs in _HALVES:
            gate, _ = _silu_parts(p_ref[:, O_GATE + hs.start:O_GATE + hs.stop])
            ybuf[:, hs] = (_dot(pooled_b[:, hs], wbd_ref[hs, hs]) * cs_ref[:, hs] * gate).astype(BF16)

        gbuf[0:HALO] = jnp.where(i > 0, ph_ref[:, O_GA:O_GA + BW] * _sigmoid(ph_ref[:, O_GB:O_GB + BW]), 0.0)
        gbuf[HALO:HALO + tm] = p_ref[:, O_GA:O_GA + BW] * _sigmoid(p_ref[:, O_GB:O_GB + BW])
        _shift_copies(gbuf, gsh)
        def conv_rows(r0, carry):
            conv_ref[pl.ds(r0, 32), :] = _tap_sum(gbuf, gsh, dww_ref, r0, CONF_K, True) + dwb_ref[...]
            return carry

        _loop_rows(tm, 32, conv_rows)
        zh, _ = _ln_stats(conv_ref[...])
        zn = zh * lng_ref[...] + lnb_ref[...]
        yd = _dot((zn * _sigmoid(zn)).astype(BF16), wpw_ref[...]) + pwb_ref[...]
        gate, _ = _silu_parts(p_ref[:, O_GATE + BW:O_GATE + 2 * BW])
        ybuf[:, BW:2 * BW] = (yd * gate).astype(BF16)

        for h in range(NH):
            qs = slice(O_Q + h * HD, O_Q + (h + 1) * HD)
            _, yx = _attn_head(p_ref[:, qs].astype(BF16), kv_ref[:, h * HD:(h + 1) * HD],
                               kv_ref[:, XA + h * HD:XA + (h + 1) * HD])
            gs = slice(O_GATE + 2 * BW + h * HD, O_GATE + 2 * BW + (h + 1) * HD)
            gate, _ = _silu_parts(p_ref[:, gs])
            ybuf[:, 2 * BW + h * HD:2 * BW + (h + 1) * HD] = (yx * gate).astype(BF16)

        o = _dot(ybuf[...], wout_ref[...])
        o_ref[...] = o
        err = _rms_residual(x_ref[...], o, pg_ref[...]) - tgt_ref[...]
        lacc[...] += _colsum(err * err)
        dres_ref[...] = err * (1.0 / D)

        @pl.when(i == nt - 1)
        def _():
            loss_ref[...] = jnp.full((1, HD), jnp.sum(lacc[...]) * (0.5 / D), F32)

    tile = lambda n: pl.BlockSpec((tm, n), lambda i: (i, 0))
    ext = pltpu.VMEM((tm + HALO, BW), F32)
    return pl.pallas_call(
        body, name="odd_fwd", grid=(nt,),
        in_specs=[tile(D), tile(D), tile(ODD_IN), _halo_spec(ODD_IN, nt, False, tm)] + [_whole()] * 11,
        out_specs=[tile(D), tile(D), _full((1, HD)), tile(BW)],
        out_shape=[S((t, D), F32), S((t, D), F32), S((1, HD), F32), S((t, BW), F32)],
        scratch_shapes=[pltpu.VMEM((tm, MIX), BF16), ext, ext, ext, ext, ext,
                        pltpu.VMEM((1, D), F32), pltpu.VMEM((7, tm + HALO, BW), F32)],
        compiler_params=_params(("arbitrary",)),
    )(x1, tgt, p, p, wbd, cscale, dww, dwb, lng, lnb, wpw, pwb, kv, wout, pg)


def _odd_bwd(dres, o, p, conv, wbd, cscale, dww, dwb, lng, lnb, wpw, pwb, kv, wout, pg):
    t = dres.shape[0]
    tm = min(TM_BWD_ODD, t)
    nt = t // tm

    def body(dres_ref, o_ref, p_ref, ph_ref, conv_ref, wbd_ref, cs_ref, dww_ref, dwb_ref, lng_ref, lnb_ref, wpw_ref,
             pwb_ref, kv_ref, wout_ref, pg_ref, dp_ref, y_ref, do_ref, dpg_ref, dwbd_ref, dcs_ref, ddww_ref, ddwb_ref,
             dlng_ref, dlnb_ref, dwpw_ref, dpwb_ref, dkv_ref,
             dy, za, zb, zc, zd, carry_e, carry_d, wacc, shifted, t1, b1, b2):
        hbuf = zb
        i = pl.program_id(0)
        ti = nt - 1 - i

        @pl.when(i == 0)
        def _():
            for ref in (dpg_ref, dwbd_ref, dcs_ref, ddwb_ref, dlng_ref, dlnb_ref, dwpw_ref, dpwb_ref, dkv_ref, wacc):
                ref[...] = jnp.zeros_like(ref)

        def post_norm_rows(r0, acc):
            sl = pl.ds(r0, RB)
            ov, dv = o_ref[sl, :], dres_ref[sl, :]
            r = lax.rsqrt(_rowmean(ov * ov) + EPS)
            oh = ov * r
            doh = dv * pg_ref[...]
            do_ref[sl, :] = (r * (doh - oh * _rowmean(doh * oh))).astype(BF16)
            return acc + dv * oh

        dpg_ref[...] += _colsum(_loop_rows(tm, RB, post_norm_rows, jnp.zeros((RB, D), F32)))
        dy[...] = _dot_nt(do_ref[...], wout_ref[...])

        za[0:HALO] = jnp.where(ti > 0, ph_ref[:, O_ZC:O_ZC + BW], 0.0)
        za[HALO:HALO + tm] = p_ref[:, O_ZC:O_ZC + BW]
        _pool_causal_levels(za, zb, zc, zd, tm)

        def pooled_rows(r0, carry):
            sl = pl.ds(r0, RB)
            at = lambda ref, back=0: ref[pl.ds(HALO + r0 - back, RB), :]
            inv = _pool_weights(ti * tm + r0, RB)
            sums = (at(zb), at(zc), at(zd), at(zd) + at(zd, 8))
            b1[sl, :] = (_mix4(sums) * inv - p_ref[sl, O_ZC:O_ZC + BW]).astype(BF16)
            return carry

        _loop_rows(tm, RB, pooled_rows)
        for hs in _HALVES:
            t1[:, hs] = _dot(b1[:, hs], wbd_ref[hs, hs])

        def pool_gate_rows(r0, acc):
            sl = pl.ds(r0, RB)
            pm = t1[sl, :]
            gate, dgate = _silu_parts(p_ref[sl, O_GATE:O_GATE + BW])
            dyc = dy[sl, 0:BW]
            yc = pm * cs_ref[...]
            y_ref[sl, 0:BW] = (yc * gate).astype(BF16)
            dp_ref[sl, O_GATE:O_GATE + BW] = (dyc * yc * dgate).astype(BF16)
            dyc = dyc * gate
            b2[sl, :] = (dyc * cs_ref[...]).astype(BF16)
            return acc + dyc * pm

        dcs_ref[...] += _colsum(_loop_rows(tm, RB, pool_gate_rows, jnp.zeros((RB, BW), F32)))
        for hs in _HALVES:
            dwbd_ref[hs, hs] += _dot_tn(b1[:, hs], b2[:, hs])
            t1[:, hs] = _dot_nt(b2[:, hs], wbd_ref[hs, hs])

        def weighted_rows(r0, carry):
            sl = pl.ds(r0, RB)
            za[sl, :] = t1[sl, :] * _pool_weights(ti * tm + r0, RB)
            return carry

        _loop_rows(tm, RB, weighted_rows)
        za[tm:tm + HALO] = jnp.where(i > 0, carry_e[...], 0.0)
        carry_e[...] = za[0:HALO]
        _pool_anticausal_levels(za, zb, zc, zd, tm)

        def pool_back_rows(r0, carry):
            sl = pl.ds(r0, RB)
            ahead = lambda ref, fwd=0: ref[pl.ds(r0 + fwd, RB), :]
            sums = (ahead(zb), ahead(zc), ahead(zd), ahead(zd) + ahead(zd, 8))
            dp_ref[sl, O_ZC:O_ZC + BW] = (_mix4(sums) - t1[sl, :]).astype(BF16)
            return carry

        _loop_rows(tm, RB, pool_back_rows)

        def swish_rows(r0, carry):
            sl = pl.ds(r0, RB)
            zh, _ = _ln_stats(conv_ref[sl, :])
            zn = zh * lng_ref[...] + lnb_ref[...]
            b1[sl, :] = (zn * _sigmoid(zn)).astype(BF16)
            return carry

        _loop_rows(tm, RB, swish_rows)
        t1[...] = _dot(b1[...], wpw_ref[...])

        def conf_gate_rows(r0, acc):
            sl = pl.ds(r0, RB)
            yd = t1[sl, :] + pwb_ref[...]
            gate, dgate = _silu_parts(p_ref[sl, O_GATE + BW:O_GATE + 2 * BW])
            dyc = dy[sl, BW:2 * BW]
            y_ref[sl, BW:2 * BW] = (yd * gate).astype(BF16)
            dp_ref[sl, O_GATE + BW:O_GATE + 2 * BW] = (dyc * yd * dgate).astype(BF16)
            dyd = dyc * gate
            b2[sl, :] = dyd.astype(BF16)
            return acc + dyd

        dpwb_ref[...] += _colsum(_loop_rows(tm, RB, conf_gate_rows, jnp.zeros((RB, BW), F32)))
        dwpw_ref[...] += _dot_tn(b1[...], b2[...])
        t1[...] = _dot_nt(b2[...], wpw_ref[...])

        def norm_back_rows(r0, accs):
            sl = pl.ds(r0, RB)
            zh, rs = _ln_stats(conv_ref[sl, :])
            _, dsilu = _silu_parts(zh * lng_ref[...] + lnb_ref[...])
            dzn = t1[sl, :] * dsilu
            dzd = _ln_bwd(dzn, zh, rs, lng_ref[...])
            hbuf[sl, :] = dzd
            return accs[0] + dzn * zh, accs[1] + dzn, accs[2] + dzd

        zero = jnp.zeros((RB, BW), F32)
        acc_g, acc_b, acc_d = _loop_rows(tm, RB, norm_back_rows, (zero, zero, zero))
        dlng_ref[...] += _colsum(acc_g)
        dlnb_ref[...] += _colsum(acc_b)
        ddwb_ref[...] += _colsum(acc_d)
        hbuf[tm:tm + HALO] = jnp.where(i > 0, carry_d[...], 0.0)
        carry_d[...] = hbuf[0:HALO]
        _shift_copies(hbuf, shifted)

        def conv_back_rows(r0, carry):
            sl = pl.ds(r0, 32)
            sgb = _sigmoid(p_ref[sl, O_GB:O_GB + BW])
            ga = p_ref[sl, O_GA:O_GA + BW]
            dzg = _conv_back(hbuf, shifted, dww_ref, wacc, ga * sgb, r0, CONF_K)
            dp_ref[sl, O_GA:O_GA + BW] = (dzg * sgb).astype(BF16)
            dp_ref[sl, O_GB:O_GB + BW] = (dzg * ga * sgb * (1.0 - sgb)).astype(BF16)
            return carry

        _loop_rows(tm, 32, conv_back_rows, unrolled=False)

        for h in range(NH):
            qs = slice(O_Q + h * HD, O_Q + (h + 1) * HD)
            ks = slice(h * HD, (h + 1) * HD)
            vs = slice(XA + h * HD, XA + (h + 1) * HD)
            gs = slice(O_GATE + 2 * BW + h * HD, O_GATE + 2 * BW + (h + 1) * HD)
            ys = slice(2 * BW + h * HD, 2 * BW + (h + 1) * HD)
            q_b = p_ref[:, qs].astype(BF16)
            prob, yx = _attn_head(q_b, kv_ref[:, ks], kv_ref[:, vs])
            gate, dgate = _silu_parts(p_ref[:, gs])
            dyc = dy[:, ys]
            y_ref[:, ys] = (yx * gate).astype(BF16)
            dp_ref[:, gs] = (dyc * yx * dgate).astype(BF16)
            dyx_b = (dyc * gate).astype(BF16)
            dprob = _dot_nt(dyx_b, kv_ref[:, vs])
            dkv_ref[:, vs] += _dot_tn(prob.astype(BF16), dyx_b)
            ds_b = (prob * (dprob - jnp.sum(dprob * prob, axis=-1, keepdims=True)) * (HD ** -0.5)).astype(BF16)
            dp_ref[:, qs] = _dot(ds_b, kv_ref[:, ks]).astype(BF16)
            dkv_ref[:, ks] += _dot_tn(ds_b, q_b)

        @pl.when(i == nt - 1)
        def _():
            for k in range(CONF_K):
                ddww_ref[k:k + 1, :] = _colsum(wacc[k * 8:(k + 1) * 8, :])
            ddww_ref[CONF_K:CONF_K + 1, :] = jnp.zeros((1, BW), F32)

    rtile = lambda n: pl.BlockSpec((tm, n), lambda i: (nt - 1 - i, 0))
    outs = [S((t, ODD_IN), BF16), S((t, MIX), BF16), S((t, D), BF16), S((1, D), F32), S((BW, BW), F32),
            S((1, BW), F32), S((CONF_K + 1, BW), F32), S((1, BW), F32), S((1, BW), F32), S((1, BW), F32),
            S((BW, BW), F32), S((1, BW), F32), S((N_MEM, 2 * XA), F32)]
    ext = pltpu.VMEM((tm + HALO, BW), F32)
    return pl.pallas_call(
        body, name="odd_bwd", grid=(nt,),
        in_specs=[rtile(D), rtile(D), rtile(ODD_IN), _halo_spec(ODD_IN, nt, True, tm), rtile(BW)] + [_whole()] * 11,
        out_specs=[rtile(ODD_IN), rtile(MIX), rtile(D)] + [_full(s.shape) for s in outs[3:]],
        out_shape=outs,
        scratch_shapes=[pltpu.VMEM((tm, MIX), F32), ext, ext, ext, ext,
                        pltpu.VMEM((HALO, BW), F32), pltpu.VMEM((HALO, BW), F32), pltpu.VMEM((CONF_K * 8, BW), F32),
                        pltpu.VMEM((7, tm + HALO, BW), F32),
                        pltpu.VMEM((tm, BW), F32), pltpu.VMEM((tm, BW), BF16), pltpu.VMEM((tm, BW), BF16)],
        compiler_params=pltpu.CompilerParams(dimension_semantics=("arbitrary",),
                                             vmem_limit_bytes=VMEM_LIMIT_ODD_BWD_V7X),
    )(dres, o, p, p, conv, wbd, cscale, dww, dwb, lng, lnb, wpw, pwb, kv, wout, pg)


def _pick_rows(n):
    for rows in (3200, 2432, 1024, 768):
        if n % rows == 0:
            return rows
    return n


def _pad_rows(a, rows):
    return jnp.pad(a, ((0, rows - a.shape[0]), (0, 0)))


def _step(x, mem, tgt, ex):
    t = x.shape[0]
    tm = min(512, t)
    w, deps = ex.first()
    causal = jnp.tril(jnp.ones((CHUNK, CHUNK), bool))
    ws = jnp.where(causal[None], w["even_a_ws"], 0.0).astype(BF16)
    wst = jnp.transpose(ws, (0, 2, 1))
    bmap = jnp.repeat(w["even_a_bs"].T, GRP, axis=1)
    wc = _pad_rows(w["even_b_conv"], 8)
    wbd = jax.scipy.linalg.block_diag(*[w["odd_c_wgrp"][g] for g in range(NH)]).astype(BF16)
    dww = _pad_rows(w["odd_d_dw_w"], CONF_K + 1)
    tk = min(1024, t)
    zeros = jnp.zeros_like(mem)

    p_e, h_e = _rms_matmul(x, w["even_pre_g"], w["even_w_in"], tm=tm, name="in_even", transposed=True, deps=deps)
    w.update(ex.even_rest(h_e))
    kv_e, memn_e = _rms_matmul(mem, w["even_mem_g"], w["even_w_kv"], tm=N_MEM, name="kv_even", out_dtype=BF16)
    even_args = (w["even_a_ln_g"], w["even_a_ln_b"], ws)
    o_e, x1 = _even_fwd(x, p_e, *even_args, bmap, wc, kv_e, w["even_w_out"], w["even_post_g"])
    w.update(ex.odd(o_e))
    kv_o, memn_o = _rms_matmul(mem, w["odd_mem_g"], w["odd_w_kv"], tm=N_MEM, name="kv_odd", out_dtype=BF16)
    p_o, h_o = _rms_matmul(x1, w["odd_pre_g"], w["odd_w_in"], tm=tm, name="in_odd", transposed=True)
    odd_args = (wbd, w["odd_c_scale"], dww, w["odd_d_dw_b"], w["odd_d_ln_g"], w["odd_d_ln_b"], w["odd_d_pw_w"],
                w["odd_d_pw_b"], kv_o, w["odd_w_out"], w["odd_post_g"])
    o_o, dres, loss, conv_o = _odd_fwd(x1, tgt, p_o, *odd_args)

    g = {}
    (dp_o, y_o, do_o, post_g_o, dwbd, g["odd_c_scale"], ddww, g["odd_d_dw_b"], g["odd_d_ln_g"], g["odd_d_ln_b"],
     dwpw, g["odd_d_pw_b"], dkv_o) = _odd_bwd(dres, o_o, p_o, conv_o, *odd_args)
    g["odd_post_g"] = post_g_o
    g["odd_d_dw_w"] = ddww[:CONF_K]
    dkv_o = dkv_o.astype(BF16)
    deps = ex.send("odd", {
        "odd_w_in": _tn_matmul(dp_o, h_o, tmc=_pick_rows(ODD_IN), tk=tk, out_dtype=BF16, name="dw_in_odd"),
        "odd_w_out": _tn_matmul(y_o, do_o, tmc=MIX, tk=min(2 * tk, t), out_dtype=BF16, name="dw_out_odd"),
        "odd_w_kv": _tn_matmul(memn_o, dkv_o, tmc=D, tk=N_MEM, out_dtype=BF16, name="dw_kv_odd"),
        "odd_d_pw_w": dwpw.astype(BF16), "loss": loss,
        "odd_c_wgrp": jnp.concatenate([dwbd[i * GRP:(i + 1) * GRP, i * GRP:(i + 1) * GRP] for i in range(NH)])})
    dx1, g["odd_pre_g"] = _nt_matmul_rms_bwd(dp_o, w["odd_w_in"], x1, w["odd_pre_g"], dres, tm=tm,
                                             name="dx_odd", transposed=True, deps=deps)
    _, g["odd_mem_g"] = _nt_matmul_rms_bwd(dkv_o, w["odd_w_kv"], mem, w["odd_mem_g"], zeros, tm=N_MEM,
                                           name="dmem_odd")

    (dp_e, y_e, do_e, post_g_e, dws, dbs, ln_g_e, ln_b_e, dwc, dkv_e) = _even_bwd(
        dx1, o_e, p_e, *even_args, wst, bmap, wc, kv_e, w["even_w_out"], w["even_post_g"])
    g["even_b_conv"] = dwc[:3]
    dkv_e = dkv_e.astype(BF16)
    deps = ex.send("even_rest", {
        "even_w_out": _tn_matmul(y_e, do_e, tmc=MIX, tk=min(2 * tk, t), out_dtype=BF16, name="dw_out_even"),
        "even_w_kv": _tn_matmul(memn_e, dkv_e, tmc=D, tk=N_MEM, out_dtype=BF16, name="dw_kv_even"),
        "even_a_ln_g": ln_g_e, "even_a_ln_b": ln_b_e,
        "even_a_ws": dws.reshape(NH * CHUNK, CHUNK), "even_a_bs": dbs[:, ::HD].T})
    g["even_w_in"] = _tn_matmul(dp_e, h_e, tmc=_pick_rows(EVEN_IN), tk=tk, out_dtype=BF16, name="dw_in_even",
                                deps=deps)
    deps = ex.send("even_in", g)
    grad_x, pre_g_e = _nt_matmul_rms_bwd(dp_e, w["even_w_in"], x, w["even_pre_g"], dx1, tm=tm,
                                         name="dx_even", transposed=True, deps=deps)
    _, mem_g_e = _nt_matmul_rms_bwd(dkv_e, w["even_w_kv"], mem, w["even_mem_g"], zeros, tm=N_MEM, name="dmem_even",
                                    deps=(grad_x,))
    deps = ex.send("even_gains", {"even_pre_g": pre_g_e, "even_mem_g": mem_g_e, "even_post_g": post_g_e})
    return grad_x, deps


def _place():
    return lax.axis_index("x"), lax.axis_index("y"), lax.axis_index("c")


def _index(px, py, pc):
    return 4 * px + 2 * py + pc


_COPIES = N_DEV - 1


def _all_gather(arrs, name):
    n = len(arrs)

    def body(*refs):
        ins, outs = refs[:n], refs[n:2 * n]
        send_sems, recv_sems, local_sems = refs[2 * n:]
        x, y, c = _place()
        me, sibling = (x, y, c), (x, y, 1 - c)
        chips = [(1 - x, y), (x, 1 - y), (1 - x, 1 - y)]

        def copy(a, k, block, to, src=None):
            dst = outs[a].at[_index(*block)]
            return pltpu.make_async_remote_copy(
                src_ref=dst if src is None else src, dst_ref=dst, send_sem=send_sems.at[a * _COPIES + k],
                recv_sem=recv_sems.at[a * _COPIES + k], device_id=to, device_id_type=MESH)

        mine = [pltpu.make_async_copy(ins[a], outs[a].at[_index(*me)], local_sems.at[a]) for a in range(n)]
        first = []
        for a in range(n):
            mine[a].start()
            first.append(copy(a, 0, me, sibling, src=ins[a]))
            first += [copy(a, 1 + j, me, (*chip, c), src=ins[a]) for j, chip in enumerate(chips)]
        for cp in first:
            cp.start()
        passed = []
        for j, chip in enumerate(chips):
            for a in range(n):
                copy(a, 1 + j, (*chip, c), me).wait_recv()
                passed.append(copy(a, 4 + j, (*chip, c), sibling))
                passed[-1].start()
        for a in range(n):
            copy(a, 0, sibling, me).wait_recv()
            for j, chip in enumerate(chips):
                copy(a, 4 + j, (*chip, 1 - c), me).wait_recv()
        for cp in first + passed:
            cp.wait_send()
        for cp in mine:
            cp.wait()

    return pl.pallas_call(
        body, name=name, in_specs=[_ANY] * n, out_specs=[_ANY] * n,
        out_shape=[S((N_DEV,) + a.shape, a.dtype) for a in arrs],
        scratch_shapes=[pltpu.SemaphoreType.DMA((n * _COPIES,)), pltpu.SemaphoreType.DMA((n * _COPIES,)),
                        pltpu.SemaphoreType.DMA((n,))],
    )(*arrs)


_HBM = pl.BlockSpec(memory_space=pltpu.HBM)
_SEM = pl.BlockSpec(memory_space=pltpu.SEMAPHORE)
_EFFECT = pltpu.SideEffectType.DATAFLOW_SIDE_EFFECTING


_ALL_FLIPS = [(k >> 2 & 1, k >> 1 & 1, k & 1) for k in range(1, N_DEV)]
_CHIP_FLIPS = [(1, 0, 0), (0, 1, 0), (1, 1, 0)]
_FLIPS = {"gather": _ALL_FLIPS, "scatter": _ALL_FLIPS, "gather_chips": [(0, 0, 1)] + _CHIP_FLIPS,
          "scatter_chips": _CHIP_FLIPS}


def _landing_shape(kind, a):
    return (N_DEV,) + a.shape if kind.startswith("gather") else a.shape


def _exchange_copies(kinds, srcs, lands, send_sems, recv_sems, local_sems, arriving):
    x, y, c = _place()
    mine = _index(x, y, c)
    remote, local = [], []
    for a, kind in enumerate(kinds):
        by_chip = kind == "scatter_chips"
        here = 2 * x + y if by_chip else mine
        own = srcs[a] if kind.startswith("gather") else srcs[a].at[here]
        local.append(pltpu.make_async_copy(own, lands[a].at[here], local_sems.at[a]))
        for k, (fx, fy, fc) in enumerate(_FLIPS[kind]):
            peer = (1 - x if fx else x, 1 - y if fy else y, 1 - c if fc else c)
            there = 2 * peer[0] + peer[1] if by_chip else _index(*peer)
            remote.append(pltpu.make_async_remote_copy(
                src_ref=srcs[a] if kind.startswith("gather") else srcs[a].at[there],
                dst_ref=lands[a].at[there if arriving else here],
                send_sem=send_sems.at[a * _COPIES + k], recv_sem=recv_sems.at[a * _COPIES + k],
                device_id=peer, device_id_type=MESH))
    return remote, local


def _exchange_start(items, name, deps=()):
    kinds = [kind for kind, _ in items]
    srcs = [a for _, a in items]
    n = len(items)
    lands = [lax.empty(_landing_shape(kind, a), a.dtype) for kind, a in items]

    def body(*refs):
        send_sems, recv_sems, local_sems = refs[2 * n + len(deps):2 * n + len(deps) + 3]
        remote, local = _exchange_copies(kinds, refs[:n], refs[n:2 * n], send_sems, recv_sems, local_sems, False)
        for cp in local + remote:
            cp.start()
        refs[-1][...] = jnp.zeros_like(refs[-1])

    held = [pltpu.HBM(a.shape, a.dtype) for a in srcs + lands]
    res = pl.pallas_call(
        body, name=name,
        out_shape=(pltpu.SemaphoreType.DMA((n * _COPIES,)), pltpu.SemaphoreType.DMA((n * _COPIES,)),
                   pltpu.SemaphoreType.DMA((n,)), *held, S((8, 128), F32)),
        in_specs=[_HBM] * (2 * n) + [_ANY] * len(deps),
        out_specs=(_SEM, _SEM, _SEM, *[_HBM] * (2 * n), _whole()),
        input_output_aliases={i: 3 + i for i in range(2 * n)},
        compiler_params=pltpu.CompilerParams(has_side_effects=_EFFECT),
    )(*[pltpu.with_memory_space_constraint(a, pltpu.HBM) for a in srcs + lands], *deps)
    return (kinds, res[:3], res[3:3 + 2 * n]), res[-1]


def _exchange_wait(handle, after, name):
    kinds, sems, held = handle
    n = len(kinds)

    def body(*refs):
        send_sems, recv_sems, local_sems = refs[2 * n:2 * n + 3]
        remote, local = _exchange_copies(kinds, refs[:n], refs[n:2 * n], send_sems, recv_sems, local_sems, True)
        for cp in remote:
            cp.wait_send()
            cp.wait_recv()
        for cp in local:
            cp.wait()

    res = pl.pallas_call(
        body, name=name, out_shape=[pltpu.HBM(a.shape, a.dtype) for a in held],
        in_specs=[_HBM] * (2 * n) + [_SEM] * 3 + [_ANY] * len(after), out_specs=[_HBM] * (2 * n),
        input_output_aliases={i: i for i in range(2 * n)},
        compiler_params=pltpu.CompilerParams(has_side_effects=_EFFECT),
    )(*held, *sems, *after)
    return res[n:]


_CHIPS = [(0, 0), (0, 1), (1, 0), (1, 1)]
_N_CHIPS = len(_CHIPS)


def _sibling_forward(lands, name):
    n = len(lands)

    def body(*refs):
        ins, outs = refs[:n], refs[n:2 * n]
        send_sems, recv_sems = refs[2 * n:]
        x, y, c = _place()
        sent, arriving = [], []
        for a in range(n):
            for j, (fx, fy, _) in enumerate(_CHIP_FLIPS):
                chip = (1 - x if fx else x, 1 - y if fy else y)
                sems = dict(send_sem=send_sems.at[a * 3 + j], recv_sem=recv_sems.at[a * 3 + j],
                            device_id=(x, y, 1 - c), device_id_type=MESH)
                mine, theirs = _index(*chip, c), _index(*chip, 1 - c)
                sent.append(pltpu.make_async_remote_copy(src_ref=ins[a].at[mine], dst_ref=outs[a].at[mine], **sems))
                arriving.append(pltpu.make_async_remote_copy(src_ref=ins[a].at[theirs], dst_ref=outs[a].at[theirs],
                                                             **sems))
        for cp in sent:
            cp.start()
        for cp in sent:
            cp.wait_send()
        for cp in arriving:
            cp.wait_recv()

    return pl.pallas_call(
        body, name=name, in_specs=[_ANY] * n, out_specs=[_ANY] * n,
        out_shape=[S(a.shape, a.dtype) for a in lands], input_output_aliases={a: a for a in range(n)},
        scratch_shapes=[pltpu.SemaphoreType.DMA((3 * n,)), pltpu.SemaphoreType.DMA((3 * n,))],
    )(*lands)


def _sibling_swap(arrs, name):
    n = len(arrs)

    def body(*refs):
        ins, outs = refs[:n], refs[n:2 * n]
        send_sems, recv_sems = refs[2 * n:]
        x, y, c = _place()
        copies = []
        for a in range(n):
            for q, chip in enumerate(_CHIPS):
                copies.append(pltpu.make_async_remote_copy(
                    src_ref=ins[a].at[_index(*chip, 1 - c)], dst_ref=outs[a].at[q],
                    send_sem=send_sems.at[a * _N_CHIPS + q], recv_sem=recv_sems.at[a * _N_CHIPS + q],
                    device_id=(x, y, 1 - c), device_id_type=MESH))
        for cp in copies:
            cp.start()
        for cp in copies:
            cp.wait_send()
            cp.wait_recv()

    return pl.pallas_call(
        body, name=name, in_specs=[_ANY] * n, out_specs=[_ANY] * n,
        out_shape=[S((_N_CHIPS,) + a.shape[1:], a.dtype) for a in arrs],
        scratch_shapes=[pltpu.SemaphoreType.DMA((_N_CHIPS * n,)), pltpu.SemaphoreType.DMA((_N_CHIPS * n,))],
    )(*arrs)


def _add_partials(mine, theirs, *, tr, name):
    _, r, c = mine.shape

    def body(mine_ref, theirs_ref, out_ref):
        core = lax.axis_index("c")
        own = jnp.where(core == 0, mine_ref[0].astype(F32), mine_ref[1].astype(F32))
        out_ref[0] = (own + theirs_ref[0].astype(F32)).astype(out_ref.dtype)

    return pl.pallas_call(
        body, name=name, grid=(_N_CHIPS, r // tr),
        in_specs=[pl.BlockSpec((2, tr, c), lambda q, i: (q, i, 0)), pl.BlockSpec((1, tr, c), lambda q, i: (q, i, 0))],
        out_specs=pl.BlockSpec((1, tr, c), lambda q, i: (q, i, 0)),
        out_shape=S((_N_CHIPS, r, c), mine.dtype),
        compiler_params=_params(("arbitrary", "arbitrary")),
    )(mine, theirs)


def _adamw(w, g, m, v):
    m = ADAM_B1 * m + (1.0 - ADAM_B1) * g
    v = ADAM_B2 * v + (1.0 - ADAM_B2) * (g * g)
    m_hat = m / (1.0 - ADAM_B1 ** ADAM_STEP)
    v_hat = v / (1.0 - ADAM_B2 ** ADAM_STEP)
    return -ADAM_LR * (m_hat / (jnp.sqrt(v_hat) + ADAM_EPS) + ADAM_WD * w), m, v


def _sum_devices(ref, rows):
    total = ref[0, rows, :].astype(F32)
    for s in range(1, ref.shape[0]):
        total = total + ref[s, rows, :].astype(F32)
    return total


def _adam_big(recv, w, m, v, *, tr, name):
    r, c = w.shape

    def body(recv_ref, w_ref, m_ref, v_ref, g_ref, d_ref, m2_ref, v2_ref):
        g = _sum_devices(recv_ref, slice(None))
        g_ref[...] = g
        d_ref[...], m2_ref[...], v2_ref[...] = _adamw(w_ref[...], g, m_ref[...], v_ref[...])

    blk = pl.BlockSpec((tr, c), lambda i: (i, 0))
    return pl.pallas_call(
        body, name=name, grid=(r // tr,),
        in_specs=[pl.BlockSpec((recv.shape[0], tr, c), lambda i: (0, i, 0)), blk, blk, blk],
        out_specs=[blk] * 4, out_shape=[S((r, c), F32)] * 4,
        compiler_params=_params(("arbitrary",)),
    )(recv, w, m, v)


_REPLICATED = {"even_pre_g": (0, 0, 1), "even_mem_g": (0, 8, 1), "even_post_g": (0, 16, 1),
               "even_a_ln_g": (1, 0, 1), "even_a_ln_b": (1, 8, 1),
               "even_a_ws": (2, 0, NH * CHUNK), "even_a_bs": (2, NH * CHUNK, NH),
               "odd_c_wgrp": (3, 0, NH * GRP)}
_SHARDED = {"odd_pre_g": (4, 0, 1), "odd_mem_g": (4, 8, 1), "odd_post_g": (4, 16, 1),
            "even_b_conv": (5, 0, 3), "odd_c_scale": (5, 8, 1), "odd_d_dw_w": (5, 16, CONF_K),
            "odd_d_dw_b": (5, 48, 1), "odd_d_ln_g": (5, 56, 1), "odd_d_ln_b": (5, 64, 1), "odd_d_pw_b": (5, 72, 1)}
_SMALL = {**_REPLICATED, **_SHARDED}
_SMALL_ROWS = {0: 24, 1: 16, 2: NH * CHUNK + 8, 3: NH * GRP, 4: 24, 5: 80}


def _adam_small(sources, wmv):
    names = list(_SMALL)
    ns = len(sources)

    def body(*refs):
        src = refs[:ns]
        ins = refs[ns:ns + 3 * len(names)]
        outs = refs[ns + 3 * len(names):]
        outs[-1][...] = _sum_devices(src[-1], slice(0, 1))
        for i, nm in enumerate(names):
            a, row0, rows = _SMALL[nm]
            g = _sum_devices(src[a], slice(row0, row0 + rows))
            w_ref, m_ref, v_ref = ins[3 * i:3 * i + 3]
            g_ref, d_ref, m2_ref, v2_ref = outs[4 * i:4 * i + 4]
            g_ref[...] = g
            d_ref[...], m2_ref[...], v2_ref[...] = _adamw(w_ref[...], g, m_ref[...], v_ref[...])

    flat = [t for nm in names for t in wmv[nm]]
    out_shape = [S(wmv[nm][0].shape, F32) for nm in names for _ in range(4)] + [S((1, HD), F32)]
    res = pl.pallas_call(
        body, name="adam_small", in_specs=[_whole()] * (ns + len(flat)), out_specs=[_whole()] * len(out_shape),
        out_shape=out_shape, compiler_params=_params(),
    )(*sources, *flat)
    return {nm: tuple(res[4 * i:4 * i + 4]) for i, nm in enumerate(names)}, res[-1]


_WEIGHTS = ["even_pre_g", "even_w_in", "even_a_ln_g", "even_a_ln_b", "even_a_ws", "even_a_bs", "even_b_conv",
            "even_mem_g", "even_w_kv", "even_w_out", "even_post_g", "odd_pre_g", "odd_w_in", "odd_c_wgrp",
            "odd_c_scale", "odd_d_dw_w", "odd_d_dw_b", "odd_d_ln_g", "odd_d_ln_b", "odd_d_pw_w", "odd_d_pw_b",
            "odd_mem_g", "odd_w_kv", "odd_w_out", "odd_post_g"]
_TRANSPOSED = ["even_w_in", "odd_w_in"]
_BIG = _TRANSPOSED + ["even_w_kv", "even_w_out", "odd_w_kv", "odd_w_out", "odd_d_pw_w"]
_BIG_TILE_ROWS = {"even_w_in": 400, "odd_w_in": 304, "even_w_kv": 128, "even_w_out": 128, "odd_w_kv": 128,
                  "odd_w_out": 128, "odd_d_pw_w": 96}


def _view2d(a, transposed):
    a = a[0]
    if a.ndim == 1:
        return a[None]
    if transposed:
        return a.T
    return a.reshape(-1, a.shape[-1])


def _rows8(a):
    return _pad_rows(a, -(-a.shape[0] // 8) * 8)


def _pack_rows(parts):
    return jnp.concatenate([_rows8(p) for p in parts], axis=0)


def _unshard_cols(a):
    return jnp.transpose(a, (1, 0, 2)).reshape(a.shape[1], N_DEV * a.shape[2])


def _shard_cols(a):
    return jnp.transpose(a.reshape(a.shape[0], N_DEV, a.shape[1] // N_DEV), (1, 0, 2))


def _rows_of(a):
    return a.reshape(-1, a.shape[-1])


_GROUPS = {"odd": (["odd_w_in", "odd_w_out", "odd_w_kv", "odd_d_pw_w"], [3], []),
           "even_rest": (["even_w_out", "even_w_kv"], [1, 2], []),
           "even_in": (["even_w_in"], [], [4, 5]),
           "even_gains": ([], [0], [])}


_TWO_LEVEL = ("even_in",)


class _MeshExchange:
    def __init__(self, shard):
        self.shard = shard
        self.handles = {}

    def first(self):
        shard = self.shard
        packs = [_pack_rows([shard[nm] for nm in _SHARDED if _SHARDED[nm][0] == a]) for a in (4, 5)]
        w_in, p128, p96 = _all_gather([shard["even_w_in"].astype(BF16)] + packs, "gather_first")
        w = {nm: shard[nm] for nm in _REPLICATED}
        w["even_a_ws"] = w["even_a_ws"].reshape(NH, CHUNK, CHUNK)
        w["odd_c_wgrp"] = w["odd_c_wgrp"].reshape(NH, GRP, GRP)
        w["even_w_in"] = _rows_of(w_in)
        full_packs = {4: _unshard_cols(p128), 5: _unshard_cols(p96)}
        for nm, (a, row0, rows) in _SHARDED.items():
            w[nm] = full_packs[a][row0:row0 + rows]
        later = lambda names: [("gather_chips", shard[nm].astype(BF16)) for nm in names]
        self.handles["w_even"], token = _exchange_start(later(["even_w_kv", "even_w_out"]), "gather_even_start",
                                                        deps=(w_in,))
        self.handles["w_odd"], token = _exchange_start(later(["odd_w_in", "odd_w_kv", "odd_w_out", "odd_d_pw_w"]),
                                                       "gather_odd_start", deps=(token,))
        return w, (token,)

    def even_rest(self, after):
        landed = _exchange_wait(self.handles.pop("w_even"), (after,), "gather_even_wait")
        kv, out = _sibling_forward(landed, "forward_even")
        return {"even_w_kv": _rows_of(kv), "even_w_out": _rows_of(out)}

    def odd(self, after):
        landed = _exchange_wait(self.handles.pop("w_odd"), (after,), "gather_odd_wait")
        w_in, kv, out, pw = _sibling_forward(landed, "forward_odd")
        return {"odd_w_in": _rows_of(w_in), "odd_w_kv": _rows_of(kv), "odd_w_out": _rows_of(out),
                "odd_d_pw_w": _rows_of(pw)}

    def send(self, group, g):
        big, replicated, sharded = _GROUPS[group]
        by_owner = [g[nm].reshape(N_DEV, -1, g[nm].shape[-1]) for nm in big]
        if group in _TWO_LEVEL:
            theirs = _sibling_swap(by_owner, "swap_" + group)
            items = [("scatter_chips", _add_partials(a, b, tr=_BIG_TILE_ROWS[nm], name="chip_sum_" + nm))
                     for nm, a, b in zip(big, by_owner, theirs)]
        else:
            items = [("scatter", a) for a in by_owner]
        items += [("gather", _pack_rows([g[nm] for nm in _REPLICATED if _REPLICATED[nm][0] == a]))
                  for a in replicated]
        items += [("scatter", _shard_cols(_pack_rows([g[nm] for nm in _SHARDED if _SHARDED[nm][0] == a])))
                  for a in sharded]
        if group == "odd":
            items.append(("gather", _rows8(g["loss"])))
        self.handles[group], token = _exchange_start(items, "send_" + group + "_start")
        return (token,)

    def receive(self, group, after):
        after = after if isinstance(after, tuple) else (after,)
        return _exchange_wait(self.handles.pop(group), after, "send_" + group + "_wait")


def kernel(x, mem, even_pre_g, even_w_in, even_a_ln_g, even_a_ln_b, even_a_ws, even_a_bs, even_b_conv, even_mem_g, even_w_kv, even_w_out, even_post_g, odd_pre_g, odd_w_in, odd_c_wgrp, odd_c_scale, odd_d_dw_w, odd_d_dw_b, odd_d_ln_g, odd_d_ln_b, odd_d_pw_w, odd_d_pw_b, odd_mem_g, odd_w_kv, odd_w_out, odd_post_g, loss_target, m_even_pre_g, m_even_w_in, m_even_a_ln_g, m_even_a_ln_b, m_even_a_ws, m_even_a_bs, m_even_b_conv, m_even_mem_g, m_even_w_kv, m_even_w_out, m_even_post_g, m_odd_pre_g, m_odd_w_in, m_odd_c_wgrp, m_odd_c_scale, m_odd_d_dw_w, m_odd_d_dw_b, m_odd_d_ln_g, m_odd_d_ln_b, m_odd_d_pw_w, m_odd_d_pw_b, m_odd_mem_g, m_odd_w_kv, m_odd_w_out, m_odd_post_g, v_even_pre_g, v_even_w_in, v_even_a_ln_g, v_even_a_ln_b, v_even_a_ws, v_even_a_bs, v_even_b_conv, v_even_mem_g, v_even_w_kv, v_even_w_out, v_even_post_g, v_odd_pre_g, v_odd_w_in, v_odd_c_wgrp, v_odd_c_scale, v_odd_d_dw_w, v_odd_d_dw_b, v_odd_d_ln_g, v_odd_d_ln_b, v_odd_d_pw_w, v_odd_d_pw_b, v_odd_mem_g, v_odd_w_kv, v_odd_w_out, v_odd_post_g):
    given = dict(locals())
    view = lambda nm, kind: _view2d(given[kind + nm], nm in _TRANSPOSED)
    shard = {nm: view(nm, "") for nm in _WEIGHTS}
    wmv = {nm: (shard[nm], view(nm, "m_"), view(nm, "v_")) for nm in _WEIGHTS}

    ex = _MeshExchange(shard)
    grad_x, last = _step(x[0], mem[0], loss_target[0], ex)

    res = {}

    def update(group, after):
        names = _GROUPS[group][0]
        landed = ex.receive(group, after)
        for nm, recv in zip(names, landed):
            res[nm] = _adam_big(recv, *wmv[nm], tr=_BIG_TILE_ROWS[nm], name="adam_" + nm)
        return landed[len(names):]

    c192, losses = update("odd", last)
    c768, c128 = update("even_rest", res["odd_d_pw_w"][0])
    a128, a96 = update("even_in", res["even_w_kv"][0])
    (c1024,) = update("even_gains", res["even_w_in"][0])
    small, loss = _adam_small([c1024, c768, c128, c192, a128, a96, losses], {nm: wmv[nm] for nm in _SMALL})
    res.update(small)
    total = loss[0, 0]
    back = lambda nm, a: (a.T if nm in _TRANSPOSED else a).reshape(given[nm].shape)
    outs = [[back(nm, res[nm][i]) for nm in _WEIGHTS] for i in range(4)]
    return (total, grad_x[None], *outs[0], *outs[1], *outs[2], *outs[3])
```

```python
import functools

import jax
import jax.numpy as jnp
from jax import lax
from jax.experimental import pallas as pl
from jax.experimental.pallas import tpu as pltpu

F32 = jnp.float32
BF16 = jnp.bfloat16
S = jax.ShapeDtypeStruct
MESH = pl.DeviceIdType.MESH
AXES = ("x", "y", "c")
N_DEV = 8

D = 1024
BW = 768
XA = 512
HD = 128
NH = 4
MIX = 2048
CHUNK = 128
GRP = 192
N_MEM = 256
CONF_K = 31
EPS = 1e-6
HALO = 32
POOL_WINDOWS = (2, 4, 8, 16)
TM_FWD_EVEN = 512
TM_FWD_ODD = 256
TM_BWD_EVEN = 256
TM_BWD_ODD = 256
RB = 16

E_U, E_V, E_BG, E_CG, E_XIN, E_Q, E_GATE = 0, 768, 1536, 2304, 3072, 3840, 4352
EVEN_IN = 6400
O_ZC, O_GA, O_GB, O_Q, O_GATE = 0, 768, 1536, 2304, 2816
ODD_IN = 4864

ADAM_LR, ADAM_B1, ADAM_B2, ADAM_EPS, ADAM_WD, ADAM_STEP = 0.001, 0.9, 0.999, 1e-08, 0.01, 10

VMEM_LIMIT_V7X = 56 * 1024 * 1024
VMEM_LIMIT_ODD_BWD_V7X = 62 * 1024 * 1024


def _params(sem=None):
    return pltpu.CompilerParams(dimension_semantics=sem, vmem_limit_bytes=VMEM_LIMIT_V7X)


def _dot(a, b):
    return jnp.dot(a, b, preferred_element_type=F32)


def _dot_nt(a, b):
    return lax.dot_general(a, b, (((1,), (1,)), ((), ())), preferred_element_type=F32)


def _dot_tn(a, b):
    return lax.dot_general(a, b, (((0,), (0,)), ((), ())), preferred_element_type=F32)


def _sigmoid(z):
    return 1.0 / (1.0 + jnp.exp(-z))


def _rowmean(a):
    return jnp.mean(a, axis=-1, keepdims=True)


def _colsum(a):
    return jnp.sum(a, axis=0, keepdims=True)


def _ln_stats(v):
    mu = _rowmean(v)
    vc = v - mu
    rs = lax.rsqrt(_rowmean(vc * vc) + EPS)
    return vc * rs, rs


def _ln_bwd(dn, vh, rs, g):
    dvh = dn * g
    return rs * (dvh - _rowmean(dvh) - vh * _rowmean(dvh * vh))


def _group_masks():
    col = lax.broadcasted_iota(jnp.int32, (1, BW), 1)
    return [((col >= GRP * h) & (col < GRP * (h + 1))).astype(F32) for h in range(NH)]


def _full(shape):
    nd = len(shape)
    return pl.BlockSpec(shape, lambda *_: (0,) * nd)


def _whole():
    return pl.BlockSpec(memory_space=pltpu.VMEM)


_ANY = pl.BlockSpec(memory_space=pl.ANY)


def _after(body, n_in, deps):
    def ordered(*refs):
        return body(*refs[:n_in], *refs[n_in + len(deps):])
    return ordered


def _rms_matmul(x, g, w, *, tm, name, transposed=False, out_dtype=F32, deps=()):
    t, d = x.shape
    n = w.shape[0] if transposed else w.shape[1]

    def body(x_ref, g_ref, w_ref, p_ref, h_ref):
        xv = x_ref[...]
        r = lax.rsqrt(_rowmean(xv * xv) + EPS)
        h = (xv * r * g_ref[...]).astype(BF16)
        h_ref[...] = h
        p_ref[...] = (_dot_nt(h, w_ref[...]) if transposed else _dot(h, w_ref[...])).astype(out_dtype)

    return pl.pallas_call(
        _after(body, 3, deps), name=name, grid=(t // tm,),
        in_specs=[pl.BlockSpec((tm, d), lambda i: (i, 0)), _whole(), _whole()] + [_ANY] * len(deps),
        out_specs=[pl.BlockSpec((tm, n), lambda i: (i, 0)), pl.BlockSpec((tm, d), lambda i: (i, 0))],
        out_shape=[S((t, n), out_dtype), S((t, d), BF16)],
        compiler_params=_params(("arbitrary",)),
    )(x, g, w, *deps)


def _nt_matmul_rms_bwd(dp, w, x, g, dres, *, tm, name, transposed=False, deps=()):
    t, n = dp.shape
    d = x.shape[1]

    def body(dp_ref, w_ref, x_ref, g_ref, dres_ref, dx_ref, dg_ref):
        @pl.when(pl.program_id(0) == 0)
        def _():
            dg_ref[...] = jnp.zeros_like(dg_ref)

        dh = _dot(dp_ref[...], w_ref[...]) if transposed else _dot_nt(dp_ref[...], w_ref[...])
        xv = x_ref[...]
        r = lax.rsqrt(_rowmean(xv * xv) + EPS)
        xh = xv * r
        dg_ref[...] += _colsum(dh * xh)
        dxh = dh * g_ref[...]
        dx_ref[...] = dres_ref[...] + r * (dxh - xh * _rowmean(dxh * xh))

    return pl.pallas_call(
        _after(body, 5, deps), name=name, grid=(t // tm,),
        in_specs=[pl.BlockSpec((tm, n), lambda i: (i, 0)), _whole(), pl.BlockSpec((tm, d), lambda i: (i, 0)),
                  _whole(), pl.BlockSpec((tm, d), lambda i: (i, 0))] + [_ANY] * len(deps),
        out_specs=[pl.BlockSpec((tm, d), lambda i: (i, 0)), pl.BlockSpec((1, d), lambda i: (0, 0))],
        out_shape=[S((t, d), F32), S((1, d), F32)],
        compiler_params=_params(("arbitrary",)),
    )(dp, w, x, g, dres, *deps)


def _tn_matmul(a, b, *, tmc, tk, out_dtype, name, deps=()):
    t, m = a.shape
    n = b.shape[1]
    nk = t // tk

    def body(a_ref, b_ref, o_ref, acc_ref):
        k = pl.program_id(1)

        @pl.when(k == 0)
        def _():
            acc_ref[...] = jnp.zeros_like(acc_ref)

        acc_ref[...] += _dot_tn(a_ref[...], b_ref[...])

        @pl.when(k == nk - 1)
        def _():
            o_ref[...] = acc_ref[...].astype(out_dtype)

    return pl.pallas_call(
        _after(body, 2, deps), name=name, grid=(m // tmc, nk),
        in_specs=[pl.BlockSpec((tk, tmc), lambda j, k: (k, j)), pl.BlockSpec((tk, n), lambda j, k: (k, 0))]
        + [_ANY] * len(deps),
        out_specs=pl.BlockSpec((tmc, n), lambda j, k: (j, 0)),
        out_shape=S((m, n), out_dtype),
        scratch_shapes=[pltpu.VMEM((tmc, n), F32)],
        compiler_params=_params(("arbitrary", "arbitrary")),
    )(a, b, *deps)


def _kv_backward(memn, dkv, w, mem, *, name):
    d, n = w.shape

    def body(memn_ref, dkv_ref, w_ref, mem_ref, dw_ref, dg_ref):
        dw_ref[...] = _dot_tn(memn_ref[...], dkv_ref[...]).astype(BF16)
        dh = _dot_nt(dkv_ref[...], w_ref[...])
        xv = mem_ref[...]
        dg_ref[...] = _colsum(dh * (xv * lax.rsqrt(_rowmean(xv * xv) + EPS)))

    return pl.pallas_call(
        body, name=name, in_specs=[_whole()] * 4, out_specs=[_whole()] * 2,
        out_shape=[S((d, n), BF16), S((1, d), F32)], compiler_params=_params(),
    )(memn, dkv, w, mem)


def _silu_parts(gt):
    sg = _sigmoid(gt)
    return gt * sg, sg * (1.0 + gt * (1.0 - sg))


def _attn_head(q_b, k_b, v_b):
    s = _dot_nt(q_b, k_b) * (HD ** -0.5)
    e = jnp.exp(s - jnp.max(s, axis=-1, keepdims=True))
    prob = e / jnp.sum(e, axis=-1, keepdims=True)
    return prob, _dot(prob.astype(BF16), v_b)


def _rms_residual(x, o, g):
    r = lax.rsqrt(_rowmean(o * o) + EPS)
    return x + o * r * g


def _rms_post_bwd(dres, o, g):
    r = lax.rsqrt(_rowmean(o * o) + EPS)
    oh = o * r
    doh = dres * g
    return r * (doh - oh * _rowmean(doh * oh)), _colsum(dres * oh)


LANE = 128
_TILE_GROUPS = [sorted({LANE * j // GRP, (LANE * j + LANE - 1) // GRP}) for j in range(BW // LANE)]


def _tile(j):
    return slice(LANE * j, LANE * (j + 1))


def _low_lanes():
    return lax.broadcasted_iota(jnp.int32, (1, LANE), 1) < GRP - LANE


def _by_group(fn):
    tiles = []
    for j, groups in enumerate(_TILE_GROUPS):
        if len(groups) == 1:
            tiles.append(fn(groups[0], j))
        else:
            tiles.append(jnp.where(_low_lanes(), fn(groups[0], j), fn(groups[1], j)))
    return jnp.concatenate(tiles, axis=1)


def _sgu_chunk(vn_b, ws_ref, bmap_ref):
    return bmap_ref[...] + _by_group(lambda h, j: _dot(ws_ref[h], vn_b[:, _tile(j)]))


def _shift_copies(buf, sh):
    n = buf.shape[0] - 8
    for b in range(1, 8):
        sh[b - 1, pl.ds(0, n), :] = buf[pl.ds(b, n), :]


def _loop_rows(rows, step, fn, carry=0, unrolled=True):
    if unrolled:
        for r0 in range(0, rows, step):
            carry = fn(r0, carry)
        return carry

    def body(j, c):
        return fn(pl.multiple_of(j * step, step), c)
    return lax.fori_loop(0, rows // step, body, carry)


def _rows_at(buf, sh, r0, off):
    b = off % 8
    if b == 0 or sh is None:
        return buf[pl.ds(r0 + off, 32), :]
    return sh[b - 1, pl.ds(r0 + (off - b), 32), :]


def _tap_sum(buf, sh, w_ref, r0, taps, causal):
    acc = None
    for k in range(taps):
        off = HALO - (taps - 1 - k) if causal else taps - 1 - k
        term = w_ref[k:k + 1, :] * _rows_at(buf, sh, r0, off)
        acc = term if acc is None else acc + term
    return acc


def _fold8(a):
    return a[0:8] + a[8:16] + a[16:24] + a[24:32]


def _conv_back(buf, sh, w_ref, acc_ref, z, r0, taps):
    dz = None
    for k in range(taps):
        ahead = _rows_at(buf, sh, r0, taps - 1 - k)
        term = w_ref[k:k + 1, :] * ahead
        dz = term if dz is None else dz + term
        acc_ref[k * 8:(k + 1) * 8, :] += _fold8(z * ahead)
    return dz


def _halo_spec(n, nt, reverse, tm):
    per = tm // HALO
    if reverse:
        return pl.BlockSpec((HALO, n), lambda i: (jnp.maximum((nt - 1 - i) * per - 1, 0), 0))
    return pl.BlockSpec((HALO, n), lambda i: (jnp.maximum(i * per - 1, 0), 0))


def _even_fwd(x, p, lng, lnb, ws, bmap, wc, kv, wout, pg):
    t = x.shape[0]
    tm = min(TM_FWD_EVEN, t)
    nt = t // tm

    def body(x_ref, p_ref, ph_ref, lng_ref, lnb_ref, ws_ref, bmap_ref, wc_ref, kv_ref, wout_ref, pg_ref,
             o_ref, x1_ref, ybuf, cbuf):
        i = pl.program_id(0)
        vh, _ = _ln_stats(p_ref[:, E_V:E_V + BW])
        vn = vh * lng_ref[...] + lnb_ref[...]
        for c in range(tm // CHUNK):
            sl = slice(c * CHUNK, (c + 1) * CHUNK)
            sg = _sgu_chunk(vn[sl].astype(BF16), ws_ref, bmap_ref)
            gate, _ = _silu_parts(p_ref[sl, E_GATE:E_GATE + BW])
            ybuf[sl, 0:BW] = (p_ref[sl, E_U:E_U + BW] * sg * gate).astype(BF16)

        cbuf[0:HALO] = jnp.where(i > 0, ph_ref[:, E_CG:E_CG + BW] * ph_ref[:, E_XIN:E_XIN + BW], 0.0)
        cbuf[HALO:HALO + tm] = p_ref[:, E_CG:E_CG + BW] * p_ref[:, E_XIN:E_XIN + BW]
        for r0 in range(0, tm, 32):
            sl = slice(r0, r0 + 32)
            cv = _tap_sum(cbuf, None, wc_ref, r0, 3, True)
            gate, _ = _silu_parts(p_ref[sl, E_GATE + BW:E_GATE + 2 * BW])
            ybuf[sl, BW:2 * BW] = (p_ref[sl, E_BG:E_BG + BW] * cv * gate).astype(BF16)

        for h in range(NH):
            qs = slice(E_Q + h * HD, E_Q + (h + 1) * HD)
            _, yx = _attn_head(p_ref[:, qs].astype(BF16), kv_ref[:, h * HD:(h + 1) * HD],
                               kv_ref[:, XA + h * HD:XA + (h + 1) * HD])
            gs = slice(E_GATE + 2 * BW + h * HD, E_GATE + 2 * BW + (h + 1) * HD)
            gate, _ = _silu_parts(p_ref[:, gs])
            ybuf[:, 2 * BW + h * HD:2 * BW + (h + 1) * HD] = (yx * gate).astype(BF16)

        o = _dot(ybuf[...], wout_ref[...])
        o_ref[...] = o
        x1_ref[...] = _rms_residual(x_ref[...], o, pg_ref[...])

    tile = lambda n: pl.BlockSpec((tm, n), lambda i: (i, 0))
    return pl.pallas_call(
        body, name="even_fwd", grid=(nt,),
        in_specs=[tile(D), tile(EVEN_IN), _halo_spec(EVEN_IN, nt, False, tm)] + [_whole()] * 8,
        out_specs=[tile(D), tile(D)],
        out_shape=[S((t, D), F32), S((t, D), F32)],
        scratch_shapes=[pltpu.VMEM((tm, MIX), BF16), pltpu.VMEM((tm + HALO, BW), F32)],
        compiler_params=_params(("arbitrary",)),
    )(x, p, p, lng, lnb, ws, bmap, wc, kv, wout, pg)


def _even_bwd(dres, o, p, lng, lnb, ws, wst, bmap, wc, kv, wout, pg):
    t = dres.shape[0]
    tm = min(TM_BWD_EVEN, t)
    nt = t // tm

    def body(dres_ref, o_ref, p_ref, ph_ref, lng_ref, lnb_ref, ws_ref, wst_ref, bmap_ref, wc_ref, kv_ref, wout_ref,
             pg_ref, dp_ref, y_ref, do_ref, dpg_ref, dws_ref, dbs_ref, dlng_ref, dlnb_ref, dwc_ref, dkv_ref,
             dy, cbuf, gbuf, dconv, carry, dvn, dbmap, wacc):
        i = pl.program_id(0)
        ti = nt - 1 - i
        masks = _group_masks()

        @pl.when(i == 0)
        def _():
            for ref in (dpg_ref, dws_ref, dlng_ref, dlnb_ref, dkv_ref, dbmap, wacc):
                ref[...] = jnp.zeros_like(ref)

        do, dpg = _rms_post_bwd(dres_ref[...], o_ref[...], pg_ref[...])
        dpg_ref[...] += dpg
        do_b = do.astype(BF16)
        do_ref[...] = do_b
        dy[...] = _dot_nt(do_b, wout_ref[...])

        vh, rs = _ln_stats(p_ref[:, E_V:E_V + BW])
        vn = vh * lng_ref[...] + lnb_ref[...]
        for c in range(tm // CHUNK):
            sl = slice(c * CHUNK, (c + 1) * CHUNK)
            vn_b = vn[sl].astype(BF16)
            sg = _sgu_chunk(vn_b, ws_ref, bmap_ref)
            u = p_ref[sl, E_U:E_U + BW]
            gate, dgate = _silu_parts(p_ref[sl, E_GATE:E_GATE + BW])
            dyc = dy[sl, 0:BW]
            ya = u * sg
            y_ref[sl, 0:BW] = (ya * gate).astype(BF16)
            dp_ref[sl, E_GATE:E_GATE + BW] = (dyc * ya * dgate).astype(BF16)
            dya = dyc * gate
            dp_ref[sl, E_U:E_U + BW] = (dya * sg).astype(BF16)
            dsg = dya * u
            dbmap[...] += dsg
            dsg_b = dsg.astype(BF16)
            for h in range(NH):
                total = None
                for j, heads in enumerate(_TILE_GROUPS):
                    if h in heads:
                        d_t = dsg_b[:, _tile(j)]
                        if len(heads) == 2:
                            d_t = jnp.where(_low_lanes() == (h == heads[0]), d_t, jnp.zeros_like(d_t))
                        part = _dot_nt(d_t, vn_b[:, _tile(j)])
                        total = part if total is None else total + part
                dws_ref[h] += total
            dvn[sl, :] = _by_group(lambda h, j: _dot(wst_ref[h], dsg_b[:, _tile(j)]))
        dn = dvn[...]
        dlng_ref[...] += _colsum(dn * vh)
        dlnb_ref[...] += _colsum(dn)
        dp_ref[:, E_V:E_V + BW] = _ln_bwd(dn, vh, rs, lng_ref[...]).astype(BF16)

        cbuf[0:HALO] = jnp.where(ti > 0, ph_ref[:, E_CG:E_CG + BW] * ph_ref[:, E_XIN:E_XIN + BW], 0.0)
        cbuf[HALO:HALO + tm] = p_ref[:, E_CG:E_CG + BW] * p_ref[:, E_XIN:E_XIN + BW]
        for r0 in range(0, tm, 32):
            sl = slice(r0, r0 + 32)
            cv = _tap_sum(cbuf, None, wc_ref, r0, 3, True)
            gate, dgate = _silu_parts(p_ref[sl, E_GATE + BW:E_GATE + 2 * BW])
            bg = p_ref[sl, E_BG:E_BG + BW]
            dyc = dy[sl, BW:2 * BW]
            yb = bg * cv
            y_ref[sl, BW:2 * BW] = (yb * gate).astype(BF16)
            dp_ref[sl, E_GATE + BW:E_GATE + 2 * BW] = (dyc * yb * dgate).astype(BF16)
            dyb = dyc * gate
            dp_ref[sl, E_BG:E_BG + BW] = (dyb * cv).astype(BF16)
            dconv[sl, :] = dyb * bg
        gbuf[0:tm] = dconv[...]
        gbuf[tm:tm + HALO] = jnp.where(i > 0, carry[...], 0.0)
        carry[...] = dconv[0:HALO]
        for r0 in range(0, tm, 32):
            sl = slice(r0, r0 + 32)
            dc = _conv_back(gbuf, None, wc_ref, wacc, cbuf[HALO + r0:HALO + r0 + 32, :], r0, 3)
            dp_ref[sl, E_CG:E_CG + BW] = (dc * p_ref[sl, E_XIN:E_XIN + BW]).astype(BF16)
            dp_ref[sl, E_XIN:E_XIN + BW] = (dc * p_ref[sl, E_CG:E_CG + BW]).astype(BF16)

        for h in range(NH):
            qs = slice(E_Q + h * HD, E_Q + (h + 1) * HD)
            ks = slice(h * HD, (h + 1) * HD)
            vs = slice(XA + h * HD, XA + (h + 1) * HD)
            gs = slice(E_GATE + 2 * BW + h * HD, E_GATE + 2 * BW + (h + 1) * HD)
            ys = slice(2 * BW + h * HD, 2 * BW + (h + 1) * HD)
            q_b = p_ref[:, qs].astype(BF16)
            prob, yx = _attn_head(q_b, kv_ref[:, ks], kv_ref[:, vs])
            gate, dgate = _silu_parts(p_ref[:, gs])
            dyc = dy[:, ys]
            y_ref[:, ys] = (yx * gate).astype(BF16)
            dp_ref[:, gs] = (dyc * yx * dgate).astype(BF16)
            dyx_b = (dyc * gate).astype(BF16)
            dprob = _dot_nt(dyx_b, kv_ref[:, vs])
            dkv_ref[:, vs] += _dot_tn(prob.astype(BF16), dyx_b)
            ds_b = (prob * (dprob - jnp.sum(dprob * prob, axis=-1, keepdims=True)) * (HD ** -0.5)).astype(BF16)
            dp_ref[:, qs] = _dot(ds_b, kv_ref[:, ks]).astype(BF16)
            dkv_ref[:, ks] += _dot_tn(ds_b, q_b)

        @pl.when(i == nt - 1)
        def _():
            for h in range(NH):
                dbs_ref[:, h * HD:(h + 1) * HD] = jnp.broadcast_to(
                    jnp.sum(dbmap[...] * masks[h], axis=-1, keepdims=True), (CHUNK, HD))
            for k in range(3):
                dwc_ref[k:k + 1, :] = _colsum(wacc[k * 8:(k + 1) * 8, :])
            dwc_ref[3:8, :] = jnp.zeros((5, BW), F32)
            causal = (lax.broadcasted_iota(jnp.int32, (CHUNK, CHUNK), 0)
                      >= lax.broadcasted_iota(jnp.int32, (CHUNK, CHUNK), 1))
            for h in range(NH):
                dws_ref[h] = jnp.where(causal, dws_ref[h], 0.0)

    rtile = lambda n: pl.BlockSpec((tm, n), lambda i: (nt - 1 - i, 0))
    outs = [S((t, EVEN_IN), BF16), S((t, MIX), BF16), S((t, D), BF16), S((1, D), F32), S((NH, CHUNK, CHUNK), F32),
            S((CHUNK, NH * HD), F32), S((1, BW), F32), S((1, BW), F32), S((8, BW), F32), S((N_MEM, 2 * XA), F32)]
    return pl.pallas_call(
        body, name="even_bwd", grid=(nt,),
        in_specs=[rtile(D), rtile(D), rtile(EVEN_IN), _halo_spec(EVEN_IN, nt, True, tm)] + [_whole()] * 9,
        out_specs=[rtile(EVEN_IN), rtile(MIX), rtile(D)] + [_full(s.shape) for s in outs[3:]],
        out_shape=outs,
        scratch_shapes=[pltpu.VMEM((tm, MIX), F32), pltpu.VMEM((tm + HALO, BW), F32), pltpu.VMEM((tm + HALO, BW), F32),
                        pltpu.VMEM((tm, BW), F32), pltpu.VMEM((HALO, BW), F32), pltpu.VMEM((tm, BW), F32),
                        pltpu.VMEM((CHUNK, BW), F32), pltpu.VMEM((3 * 8, BW), F32)],
        compiler_params=_params(("arbitrary",)),
    )(dres, o, p, p, lng, lnb, ws, wst, bmap, wc, kv, wout, pg)


def _pool_causal_levels(za, zb, zc, zd, tm):
    n = tm + HALO
    zb[pl.ds(8, n - 8), :] = za[pl.ds(8, n - 8), :] + za[pl.ds(7, n - 8), :]
    zc[pl.ds(16, n - 16), :] = zb[pl.ds(16, n - 16), :] + zb[pl.ds(14, n - 16), :]
    zd[pl.ds(24, n - 24), :] = zc[pl.ds(24, n - 24), :] + zc[pl.ds(20, n - 24), :]


def _pool_causal(za, zb, zc, zd, tm):
    _pool_causal_levels(za, zb, zc, zd, tm)
    s16 = zd[pl.ds(HALO, tm), :] + zd[pl.ds(HALO - 8, tm), :]
    return zb[pl.ds(HALO, tm), :], zc[pl.ds(HALO, tm), :], zd[pl.ds(HALO, tm), :], s16


def _pool_anticausal_levels(ea, eb, ec, ed, tm):
    n = tm + HALO
    eb[pl.ds(0, n - 8), :] = ea[pl.ds(0, n - 8), :] + ea[pl.ds(1, n - 8), :]
    ec[pl.ds(0, n - 16), :] = eb[pl.ds(0, n - 16), :] + eb[pl.ds(2, n - 16), :]
    ed[pl.ds(0, n - 24), :] = ec[pl.ds(0, n - 24), :] + ec[pl.ds(4, n - 24), :]


def _pool_weights(t0, rows):
    tf = (t0 + lax.broadcasted_iota(jnp.int32, (rows, 1), 0) + 1).astype(F32)
    inv = [jnp.broadcast_to(1.0 / jnp.minimum(tf, float(win)), (rows, LANE)) for win in POOL_WINDOWS]
    return _by_group(lambda g, j: inv[g])


_HALVES = (slice(0, BW // 2), slice(BW // 2, BW))


def _mix4(parts):
    return _by_group(lambda g, j: parts[g][:, _tile(j)])


def _odd_fwd(x1, tgt, p, wbd, cscale, dww, dwb, lng, lnb, wpw, pwb, kv, wout, pg):
    t = x1.shape[0]
    tm = min(TM_FWD_ODD, t)
    nt = t // tm

    def body(x_ref, tgt_ref, p_ref, ph_ref, wbd_ref, cs_ref, dww_ref, dwb_ref, lng_ref, lnb_ref, wpw_ref, pwb_ref,
             kv_ref, wout_ref, pg_ref, o_ref, dres_ref, loss_ref, conv_ref, ybuf, za, zb, zc, zd, gbuf, lacc, gsh):
        i = pl.program_id(0)

        @pl.when(i == 0)
        def _():
            lacc[...] = jnp.zeros_like(lacc)

        z = p_ref[:, O_ZC:O_ZC + BW]
        za[0:HALO] = jnp.where(i > 0, ph_ref[:, O_ZC:O_ZC + BW], 0.0)
        za[HALO:HALO + tm] = z
        pooled = _mix4(_pool_causal(za, zb, zc, zd, tm)) * _pool_weights(i * tm, tm) - z
        pooled_b = pooled.astype(BF16)
        for hs in _HALVES:
            gate, _ = _silu_parts(p_ref[:, O_GATE + hs.start:O_GATE + hs.stop])
            ybuf[:, hs] = (_dot(pooled_b[:, hs], wbd_ref[hs, hs]) * cs_ref[:, hs] * gate).astype(BF16)

        gbuf[0:HALO] = jnp.where(i > 0, ph_ref[:, O_GA:O_GA + BW] * _sigmoid(ph_ref[:, O_GB:O_GB + BW]), 0.0)
        gbuf[HALO:HALO + tm] = p_ref[:, O_GA:O_GA + BW] * _sigmoid(p_ref[:, O_GB:O_GB + BW])
        _shift_copies(gbuf, gsh)
        def conv_rows(r0, carry):
            conv_ref[pl.ds(r0, 32), :] = _tap_sum(gbuf, gsh, dww_ref, r0, CONF_K, True) + dwb_ref[...]
            return carry

        _loop_rows(tm, 32, conv_rows)
        zh, _ = _ln_stats(conv_ref[...])
        zn = zh * lng_ref[...] + lnb_ref[...]
        yd = _dot((zn * _sigmoid(zn)).astype(BF16), wpw_ref[...]) + pwb_ref[...]
        gate, _ = _silu_parts(p_ref[:, O_GATE + BW:O_GATE + 2 * BW])
        ybuf[:, BW:2 * BW] = (yd * gate).astype(BF16)

        for h in range(NH):
            qs = slice(O_Q + h * HD, O_Q + (h + 1) * HD)
            _, yx = _attn_head(p_ref[:, qs].astype(BF16), kv_ref[:, h * HD:(h + 1) * HD],
                               kv_ref[:, XA + h * HD:XA + (h + 1) * HD])
            gs = slice(O_GATE + 2 * BW + h * HD, O_GATE + 2 * BW + (h + 1) * HD)
            gate, _ = _silu_parts(p_ref[:, gs])
            ybuf[:, 2 * BW + h * HD:2 * BW + (h + 1) * HD] = (yx * gate).astype(BF16)

        o = _dot(ybuf[...], wout_ref[...])
        o_ref[...] = o
        err = _rms_residual(x_ref[...], o, pg_ref[...]) - tgt_ref[...]
        lacc[...] += _colsum(err * err)
        dres_ref[...] = err * (1.0 / D)

        @pl.when(i == nt - 1)
        def _():
            loss_ref[...] = jnp.full((1, HD), jnp.sum(lacc[...]) * (0.5 / D), F32)

    tile = lambda n: pl.BlockSpec((tm, n), lambda i: (i, 0))
    ext = pltpu.VMEM((tm + HALO, BW), F32)
    return pl.pallas_call(
        body, name="odd_fwd", grid=(nt,),
        in_specs=[tile(D), tile(D), tile(ODD_IN), _halo_spec(ODD_IN, nt, False, tm)] + [_whole()] * 11,
        out_specs=[tile(D), tile(D), _full((1, HD)), tile(BW)],
        out_shape=[S((t, D), F32), S((t, D), F32), S((1, HD), F32), S((t, BW), F32)],
        scratch_shapes=[pltpu.VMEM((tm, MIX), BF16), ext, ext, ext, ext, ext,
                        pltpu.VMEM((1, D), F32), pltpu.VMEM((7, tm + HALO, BW), F32)],
        compiler_params=_params(("arbitrary",)),
    )(x1, tgt, p, p, wbd, cscale, dww, dwb, lng, lnb, wpw, pwb, kv, wout, pg)


def _odd_bwd(dres, o, p, conv, wbd, cscale, dww, dwb, lng, lnb, wpw, pwb, kv, wout, pg):
    t = dres.shape[0]
    tm = min(TM_BWD_ODD, t)
    nt = t // tm

    def body(dres_ref, o_ref, p_ref, ph_ref, conv_ref, wbd_ref, cs_ref, dww_ref, dwb_ref, lng_ref, lnb_ref, wpw_ref,
             pwb_ref, kv_ref, wout_ref, pg_ref, dp_ref, y_ref, do_ref, dpg_ref, dwbd_ref, dcs_ref, ddww_ref, ddwb_ref,
             dlng_ref, dlnb_ref, dwpw_ref, dpwb_ref, dkv_ref,
             dy, za, zb, zc, zd, carry_e, carry_d, wacc, shifted, t1, b1, b2):
        hbuf = zb
        i = pl.program_id(0)
        ti = nt - 1 - i

        @pl.when(i == 0)
        def _():
            for ref in (dpg_ref, dwbd_ref, dcs_ref, ddwb_ref, dlng_ref, dlnb_ref, dwpw_ref, dpwb_ref, dkv_ref, wacc):
                ref[...] = jnp.zeros_like(ref)

        def post_norm_rows(r0, acc):
            sl = pl.ds(r0, RB)
            ov, dv = o_ref[sl, :], dres_ref[sl, :]
            r = lax.rsqrt(_rowmean(ov * ov) + EPS)
            oh = ov * r
            doh = dv * pg_ref[...]
            do_ref[sl, :] = (r * (doh - oh * _rowmean(doh * oh))).astype(BF16)
            return acc + dv * oh

        dpg_ref[...] += _colsum(_loop_rows(tm, RB, post_norm_rows, jnp.zeros((RB, D), F32)))
        dy[...] = _dot_nt(do_ref[...], wout_ref[...])

        za[0:HALO] = jnp.where(ti > 0, ph_ref[:, O_ZC:O_ZC + BW], 0.0)
        za[HALO:HALO + tm] = p_ref[:, O_ZC:O_ZC + BW]
        _pool_causal_levels(za, zb, zc, zd, tm)

        def pooled_rows(r0, carry):
            sl = pl.ds(r0, RB)
            at = lambda ref, back=0: ref[pl.ds(HALO + r0 - back, RB), :]
            inv = _pool_weights(ti * tm + r0, RB)
            sums = (at(zb), at(zc), at(zd), at(zd) + at(zd, 8))
            b1[sl, :] = (_mix4(sums) * inv - p_ref[sl, O_ZC:O_ZC + BW]).astype(BF16)
            return carry

        _loop_rows(tm, RB, pooled_rows)
        for hs in _HALVES:
            t1[:, hs] = _dot(b1[:, hs], wbd_ref[hs, hs])

        def pool_gate_rows(r0, acc):
            sl = pl.ds(r0, RB)
            pm = t1[sl, :]
            gate, dgate = _silu_parts(p_ref[sl, O_GATE:O_GATE + BW])
            dyc = dy[sl, 0:BW]
            yc = pm * cs_ref[...]
            y_ref[sl, 0:BW] = (yc * gate).astype(BF16)
            dp_ref[sl, O_GATE:O_GATE + BW] = (dyc * yc * dgate).astype(BF16)
            dyc = dyc * gate
            b2[sl, :] = (dyc * cs_ref[...]).astype(BF16)
            return acc + dyc * pm

        dcs_ref[...] += _colsum(_loop_rows(tm, RB, pool_gate_rows, jnp.zeros((RB, BW), F32)))
        for hs in _HALVES:
            dwbd_ref[hs, hs] += _dot_tn(b1[:, hs], b2[:, hs])
            t1[:, hs] = _dot_nt(b2[:, hs], wbd_ref[hs, hs])

        def weighted_rows(r0, carry):
            sl = pl.ds(r0, RB)
            za[sl, :] = t1[sl, :] * _pool_weights(ti * tm + r0, RB)
            return carry

        _loop_rows(tm, RB, weighted_rows)
        za[tm:tm + HALO] = jnp.where(i > 0, carry_e[...], 0.0)
        carry_e[...] = za[0:HALO]
        _pool_anticausal_levels(za, zb, zc, zd, tm)

        def pool_back_rows(r0, carry):
            sl = pl.ds(r0, RB)
            ahead = lambda ref, fwd=0: ref[pl.ds(r0 + fwd, RB), :]
            sums = (ahead(zb), ahead(zc), ahead(zd), ahead(zd) + ahead(zd, 8))
            dp_ref[sl, O_ZC:O_ZC + BW] = (_mix4(sums) - t1[sl, :]).astype(BF16)
            return carry

        _loop_rows(tm, RB, pool_back_rows)

        def swish_rows(r0, carry):
            sl = pl.ds(r0, RB)
            zh, _ = _ln_stats(conv_ref[sl, :])
            zn = zh * lng_ref[...] + lnb_ref[...]
            b1[sl, :] = (zn * _sigmoid(zn)).astype(BF16)
            return carry

        _loop_rows(tm, RB, swish_rows)
        t1[...] = _dot(b1[...], wpw_ref[...])

        def conf_gate_rows(r0, acc):
            sl = pl.ds(r0, RB)
            yd = t1[sl, :] + pwb_ref[...]
            gate, dgate = _silu_parts(p_ref[sl, O_GATE + BW:O_GATE + 2 * BW])
            dyc = dy[sl, BW:2 * BW]
            y_ref[sl, BW:2 * BW] = (yd * gate).astype(BF16)
            dp_ref[sl, O_GATE + BW:O_GATE + 2 * BW] = (dyc * yd * dgate).astype(BF16)
            dyd = dyc * gate
            b2[sl, :] = dyd.astype(BF16)
            return acc + dyd

        dpwb_ref[...] += _colsum(_loop_rows(tm, RB, conf_gate_rows, jnp.zeros((RB, BW), F32)))
        dwpw_ref[...] += _dot_tn(b1[...], b2[...])
        t1[...] = _dot_nt(b2[...], wpw_ref[...])

        def norm_back_rows(r0, accs):
            sl = pl.ds(r0, RB)
            zh, rs = _ln_stats(conv_ref[sl, :])
            _, dsilu = _silu_parts(zh * lng_ref[...] + lnb_ref[...])
            dzn = t1[sl, :] * dsilu
            dzd = _ln_bwd(dzn, zh, rs, lng_ref[...])
            hbuf[sl, :] = dzd
            return accs[0] + dzn * zh, accs[1] + dzn, accs[2] + dzd

        zero = jnp.zeros((RB, BW), F32)
        acc_g, acc_b, acc_d = _loop_rows(tm, RB, norm_back_rows, (zero, zero, zero))
        dlng_ref[...] += _colsum(acc_g)
        dlnb_ref[...] += _colsum(acc_b)
        ddwb_ref[...] += _colsum(acc_d)
        hbuf[tm:tm + HALO] = jnp.where(i > 0, carry_d[...], 0.0)
        carry_d[...] = hbuf[0:HALO]
        _shift_copies(hbuf, shifted)

        def conv_back_rows(r0, carry):
            sl = pl.ds(r0, 32)
            sgb = _sigmoid(p_ref[sl, O_GB:O_GB + BW])
            ga = p_ref[sl, O_GA:O_GA + BW]
            dzg = _conv_back(hbuf, shifted, dww_ref, wacc, ga * sgb, r0, CONF_K)
            dp_ref[sl, O_GA:O_GA + BW] = (dzg * sgb).astype(BF16)
            dp_ref[sl, O_GB:O_GB + BW] = (dzg * ga * sgb * (1.0 - sgb)).astype(BF16)
            return carry

        _loop_rows(tm, 32, conv_back_rows, unrolled=False)

        for h in range(NH):
            qs = slice(O_Q + h * HD, O_Q + (h + 1) * HD)
            ks = slice(h * HD, (h + 1) * HD)
            vs = slice(XA + h * HD, XA + (h + 1) * HD)
            gs = slice(O_GATE + 2 * BW + h * HD, O_GATE + 2 * BW + (h + 1) * HD)
            ys = slice(2 * BW + h * HD, 2 * BW + (h + 1) * HD)
            q_b = p_ref[:, qs].astype(BF16)
            prob, yx = _attn_head(q_b, kv_ref[:, ks], kv_ref[:, vs])
            gate, dgate = _silu_parts(p_ref[:, gs])
            dyc = dy[:, ys]
            y_ref[:, ys] = (yx * gate).astype(BF16)
            dp_ref[:, gs] = (dyc * yx * dgate).astype(BF16)
            dyx_b = (dyc * gate).astype(BF16)
            dprob = _dot_nt(dyx_b, kv_ref[:, vs])
            dkv_ref[:, vs] += _dot_tn(prob.astype(BF16), dyx_b)
            ds_b = (prob * (dprob - jnp.sum(dprob * prob, axis=-1, keepdims=True)) * (HD ** -0.5)).astype(BF16)
            dp_ref[:, qs] = _dot(ds_b, kv_ref[:, ks]).astype(BF16)
            dkv_ref[:, ks] += _dot_tn(ds_b, q_b)

        @pl.when(i == nt - 1)
        def _():
            for k in range(CONF_K):
                ddww_ref[k:k + 1, :] = _colsum(wacc[k * 8:(k + 1) * 8, :])
            ddww_ref[CONF_K:CONF_K + 1, :] = jnp.zeros((1, BW), F32)

    rtile = lambda n: pl.BlockSpec((tm, n), lambda i: (nt - 1 - i, 0))
    outs = [S((t, ODD_IN), BF16), S((t, MIX), BF16), S((t, D), BF16), S((1, D), F32), S((BW, BW), F32),
            S((1, BW), F32), S((CONF_K + 1, BW), F32), S((1, BW), F32), S((1, BW), F32), S((1, BW), F32),
            S((BW, BW), F32), S((1, BW), F32), S((N_MEM, 2 * XA), F32)]
    ext = pltpu.VMEM((tm + HALO, BW), F32)
    return pl.pallas_call(
        body, name="odd_bwd", grid=(nt,),
        in_specs=[rtile(D), rtile(D), rtile(ODD_IN), _halo_spec(ODD_IN, nt, True, tm), rtile(BW)] + [_whole()] * 11,
        out_specs=[rtile(ODD_IN), rtile(MIX), rtile(D)] + [_full(s.shape) for s in outs[3:]],
        out_shape=outs,
        scratch_shapes=[pltpu.VMEM((tm, MIX), F32), ext, ext, ext, ext,
                        pltpu.VMEM((HALO, BW), F32), pltpu.VMEM((HALO, BW), F32), pltpu.VMEM((CONF_K * 8, BW), F32),
                        pltpu.VMEM((7, tm + HALO, BW), F32),
                        pltpu.VMEM((tm, BW), F32), pltpu.VMEM((tm, BW), BF16), pltpu.VMEM((tm, BW), BF16)],
        compiler_params=pltpu.CompilerParams(dimension_semantics=("arbitrary",),
                                             vmem_limit_bytes=VMEM_LIMIT_ODD_BWD_V7X),
    )(dres, o, p, p, conv, wbd, cscale, dww, dwb, lng, lnb, wpw, pwb, kv, wout, pg)


def _pick_rows(n):
    for rows in (3200, 2432, 1024, 768):
        if n % rows == 0:
            return rows
    return n


def _pad_rows(a, rows):
    return jnp.pad(a, ((0, rows - a.shape[0]), (0, 0)))


def _step(x, mem, tgt, ex):
    t = x.shape[0]
    tm = min(512, t)
    w, deps = ex.first()
    causal = jnp.tril(jnp.ones((CHUNK, CHUNK), bool))
    ws = jnp.where(causal[None], w["even_a_ws"], 0.0).astype(BF16)
    wst = jnp.transpose(ws, (0, 2, 1))
    bmap = jnp.repeat(w["even_a_bs"].T, GRP, axis=1)
    wc = _pad_rows(w["even_b_conv"], 8)
    wbd = jax.scipy.linalg.block_diag(*[w["odd_c_wgrp"][g] for g in range(NH)]).astype(BF16)
    dww = _pad_rows(w["odd_d_dw_w"], CONF_K + 1)
    tk = min(1024, t)

    p_e, h_e = _rms_matmul(x, w["even_pre_g"], w["even_w_in"], tm=tm, name="in_even", transposed=True, deps=deps)
    w.update(ex.even_rest(h_e))
    kv_e, memn_e = _rms_matmul(mem, w["even_mem_g"], w["even_w_kv"], tm=N_MEM, name="kv_even", out_dtype=BF16)
    even_args = (w["even_a_ln_g"], w["even_a_ln_b"], ws)
    o_e, x1 = _even_fwd(x, p_e, *even_args, bmap, wc, kv_e, w["even_w_out"], w["even_post_g"])
    w.update(ex.odd(o_e))
    kv_o, memn_o = _rms_matmul(mem, w["odd_mem_g"], w["odd_w_kv"], tm=N_MEM, name="kv_odd", out_dtype=BF16)
    p_o, h_o = _rms_matmul(x1, w["odd_pre_g"], w["odd_w_in"], tm=tm, name="in_odd", transposed=True)
    odd_args = (wbd, w["odd_c_scale"], dww, w["odd_d_dw_b"], w["odd_d_ln_g"], w["odd_d_ln_b"], w["odd_d_pw_w"],
                w["odd_d_pw_b"], kv_o, w["odd_w_out"], w["odd_post_g"])
    o_o, dres, loss, conv_o = _odd_fwd(x1, tgt, p_o, *odd_args)

    g = {}
    (dp_o, y_o, do_o, post_g_o, dwbd, g["odd_c_scale"], ddww, g["odd_d_dw_b"], g["odd_d_ln_g"], g["odd_d_ln_b"],
     dwpw, g["odd_d_pw_b"], dkv_o) = _odd_bwd(dres, o_o, p_o, conv_o, *odd_args)
    g["odd_post_g"] = post_g_o
    g["odd_d_dw_w"] = ddww[:CONF_K]
    dw_kv_o, g["odd_mem_g"] = _kv_backward(memn_o, dkv_o.astype(BF16), w["odd_w_kv"], mem, name="kv_bwd_odd")
    deps = ex.send("odd", {
        "odd_w_in": _tn_matmul(dp_o, h_o, tmc=_pick_rows(ODD_IN), tk=tk, out_dtype=BF16, name="dw_in_odd"),
        "odd_w_out": _tn_matmul(y_o, do_o, tmc=MIX, tk=tk, out_dtype=BF16, name="dw_out_odd"),
        "odd_w_kv": dw_kv_o, "odd_d_pw_w": dwpw.astype(BF16), "loss": loss,
        "odd_c_wgrp": jnp.concatenate([dwbd[i * GRP:(i + 1) * GRP, i * GRP:(i + 1) * GRP] for i in range(NH)])})
    dx1, g["odd_pre_g"] = _nt_matmul_rms_bwd(dp_o, w["odd_w_in"], x1, w["odd_pre_g"], dres, tm=tm,
                                             name="dx_odd", transposed=True, deps=deps)

    (dp_e, y_e, do_e, post_g_e, dws, dbs, ln_g_e, ln_b_e, dwc, dkv_e) = _even_bwd(
        dx1, o_e, p_e, *even_args, wst, bmap, wc, kv_e, w["even_w_out"], w["even_post_g"])
    g["even_b_conv"] = dwc[:3]
    dw_kv_e, mem_g_e = _kv_backward(memn_e, dkv_e.astype(BF16), w["even_w_kv"], mem, name="kv_bwd_even")
    deps = ex.send("even_rest", {
        "even_w_out": _tn_matmul(y_e, do_e, tmc=MIX, tk=tk, out_dtype=BF16, name="dw_out_even"),
        "even_w_kv": dw_kv_e, "even_a_ln_g": ln_g_e, "even_a_ln_b": ln_b_e,
        "even_a_ws": dws.reshape(NH * CHUNK, CHUNK), "even_a_bs": dbs[:, ::HD].T})
    g["even_w_in"] = _tn_matmul(dp_e, h_e, tmc=_pick_rows(EVEN_IN), tk=tk, out_dtype=BF16, name="dw_in_even",
                                deps=deps)
    deps = ex.send("even_in", g)
    grad_x, pre_g_e = _nt_matmul_rms_bwd(dp_e, w["even_w_in"], x, w["even_pre_g"], dx1, tm=tm,
                                         name="dx_even", transposed=True, deps=deps)
    deps = ex.send("even_gains", {"even_pre_g": pre_g_e, "even_mem_g": mem_g_e, "even_post_g": post_g_e})
    return grad_x, deps


def _place():
    return lax.axis_index("x"), lax.axis_index("y"), lax.axis_index("c")


def _index(px, py, pc):
    return 4 * px + 2 * py + pc


_COPIES = N_DEV - 1


def _all_gather(arrs, name):
    n = len(arrs)

    def body(*refs):
        ins, outs = refs[:n], refs[n:2 * n]
        send_sems, recv_sems, local_sems = refs[2 * n:]
        x, y, c = _place()
        me, sibling = (x, y, c), (x, y, 1 - c)
        chips = [(1 - x, y), (x, 1 - y), (1 - x, 1 - y)]

        def copy(a, k, block, to, src=None):
            dst = outs[a].at[_index(*block)]
            return pltpu.make_async_remote_copy(
                src_ref=dst if src is None else src, dst_ref=dst, send_sem=send_sems.at[a * _COPIES + k],
                recv_sem=recv_sems.at[a * _COPIES + k], device_id=to, device_id_type=MESH)

        mine = [pltpu.make_async_copy(ins[a], outs[a].at[_index(*me)], local_sems.at[a]) for a in range(n)]
        first = []
        for a in range(n):
            mine[a].start()
            first.append(copy(a, 0, me, sibling, src=ins[a]))
            first += [copy(a, 1 + j, me, (*chip, c), src=ins[a]) for j, chip in enumerate(chips)]
        for cp in first:
            cp.start()
        passed = []
        for j, chip in enumerate(chips):
            for a in range(n):
                copy(a, 1 + j, (*chip, c), me).wait_recv()
                passed.append(copy(a, 4 + j, (*chip, c), sibling))
                passed[-1].start()
        for a in range(n):
            copy(a, 0, sibling, me).wait_recv()
            for j, chip in enumerate(chips):
                copy(a, 4 + j, (*chip, 1 - c), me).wait_recv()
        for cp in first + passed:
            cp.wait_send()
        for cp in mine:
            cp.wait()

    return pl.pallas_call(
        body, name=name, in_specs=[_ANY] * n, out_specs=[_ANY] * n,
        out_shape=[S((N_DEV,) + a.shape, a.dtype) for a in arrs],
        scratch_shapes=[pltpu.SemaphoreType.DMA((n * _COPIES,)), pltpu.SemaphoreType.DMA((n * _COPIES,)),
                        pltpu.SemaphoreType.DMA((n,))],
    )(*arrs)


_HBM = pl.BlockSpec(memory_space=pltpu.HBM)
_SEM = pl.BlockSpec(memory_space=pltpu.SEMAPHORE)
_EFFECT = pltpu.SideEffectType.DATAFLOW_SIDE_EFFECTING


_ALL_FLIPS = [(k >> 2 & 1, k >> 1 & 1, k & 1) for k in range(1, N_DEV)]
_CHIP_FLIPS = [(1, 0, 0), (0, 1, 0), (1, 1, 0)]
_FLIPS = {"gather": _ALL_FLIPS, "scatter": _ALL_FLIPS, "gather_chips": [(0, 0, 1)] + _CHIP_FLIPS,
          "scatter_chips": _CHIP_FLIPS}


def _landing_shape(kind, a):
    return (N_DEV,) + a.shape if kind.startswith("gather") else a.shape


def _exchange_copies(kinds, srcs, lands, send_sems, recv_sems, local_sems, arriving):
    x, y, c = _place()
    mine = _index(x, y, c)
    remote, local = [], []
    for a, kind in enumerate(kinds):
        by_chip = kind == "scatter_chips"
        here = 2 * x + y if by_chip else mine
        own = srcs[a] if kind.startswith("gather") else srcs[a].at[here]
        local.append(pltpu.make_async_copy(own, lands[a].at[here], local_sems.at[a]))
        for k, (fx, fy, fc) in enumerate(_FLIPS[kind]):
            peer = (1 - x if fx else x, 1 - y if fy else y, 1 - c if fc else c)
            there = 2 * peer[0] + peer[1] if by_chip else _index(*peer)
            remote.append(pltpu.make_async_remote_copy(
                src_ref=srcs[a] if kind.startswith("gather") else srcs[a].at[there],
                dst_ref=lands[a].at[there if arriving else here],
                send_sem=send_sems.at[a * _COPIES + k], recv_sem=recv_sems.at[a * _COPIES + k],
                device_id=peer, device_id_type=MESH))
    return remote, local


def _exchange_start(items, name, deps=()):
    kinds = [kind for kind, _ in items]
    srcs = [a for _, a in items]
    n = len(items)
    lands = [lax.empty(_landing_shape(kind, a), a.dtype) for kind, a in items]

    def body(*refs):
        send_sems, recv_sems, local_sems = refs[2 * n + len(deps):2 * n + len(deps) + 3]
        remote, local = _exchange_copies(kinds, refs[:n], refs[n:2 * n], send_sems, recv_sems, local_sems, False)
        for cp in local + remote:
            cp.start()
        refs[-1][...] = jnp.zeros_like(refs[-1])

    held = [pltpu.HBM(a.shape, a.dtype) for a in srcs + lands]
    res = pl.pallas_call(
        body, name=name,
        out_shape=(pltpu.SemaphoreType.DMA((n * _COPIES,)), pltpu.SemaphoreType.DMA((n * _COPIES,)),
                   pltpu.SemaphoreType.DMA((n,)), *held, S((8, 128), F32)),
        in_specs=[_HBM] * (2 * n) + [_ANY] * len(deps),
        out_specs=(_SEM, _SEM, _SEM, *[_HBM] * (2 * n), _whole()),
        input_output_aliases={i: 3 + i for i in range(2 * n)},
        compiler_params=pltpu.CompilerParams(has_side_effects=_EFFECT),
    )(*[pltpu.with_memory_space_constraint(a, pltpu.HBM) for a in srcs + lands], *deps)
    return (kinds, res[:3], res[3:3 + 2 * n]), res[-1]


def _exchange_wait(handle, after, name):
    kinds, sems, held = handle
    n = len(kinds)

    def body(*refs):
        send_sems, recv_sems, local_sems = refs[2 * n:2 * n + 3]
        remote, local = _exchange_copies(kinds, refs[:n], refs[n:2 * n], send_sems, recv_sems, local_sems, True)
        for cp in remote:
            cp.wait_send()
            cp.wait_recv()
        for cp in local:
            cp.wait()

    res = pl.pallas_call(
        body, name=name, out_shape=[pltpu.HBM(a.shape, a.dtype) for a in held],
        in_specs=[_HBM] * (2 * n) + [_SEM] * 3 + [_ANY] * len(after), out_specs=[_HBM] * (2 * n),
        input_output_aliases={i: i for i in range(2 * n)},
        compiler_params=pltpu.CompilerParams(has_side_effects=_EFFECT),
    )(*held, *sems, *after)
    return res[n:]


_CHIPS = [(0, 0), (0, 1), (1, 0), (1, 1)]
_N_CHIPS = len(_CHIPS)


def _sibling_forward(lands, name):
    n = len(lands)

    def body(*refs):
        ins, outs = refs[:n], refs[n:2 * n]
        send_sems, recv_sems = refs[2 * n:]
        x, y, c = _place()
        sent, arriving = [], []
        for a in range(n):
            for j, (fx, fy, _) in enumerate(_CHIP_FLIPS):
                chip = (1 - x if fx else x, 1 - y if fy else y)
                sems = dict(send_sem=send_sems.at[a * 3 + j], recv_sem=recv_sems.at[a * 3 + j],
                            device_id=(x, y, 1 - c), device_id_type=MESH)
                mine, theirs = _index(*chip, c), _index(*chip, 1 - c)
                sent.append(pltpu.make_async_remote_copy(src_ref=ins[a].at[mine], dst_ref=outs[a].at[mine], **sems))
                arriving.append(pltpu.make_async_remote_copy(src_ref=ins[a].at[theirs], dst_ref=outs[a].at[theirs],
                                                             **sems))
        for cp in sent:
            cp.start()
        for cp in sent:
            cp.wait_send()
        for cp in arriving:
            cp.wait_recv()

    return pl.pallas_call(
        body, name=name, in_specs=[_ANY] * n, out_specs=[_ANY] * n,
        out_shape=[S(a.shape, a.dtype) for a in lands], input_output_aliases={a: a for a in range(n)},
        scratch_shapes=[pltpu.SemaphoreType.DMA((3 * n,)), pltpu.SemaphoreType.DMA((3 * n,))],
    )(*lands)


def _sibling_swap(arrs, name):
    n = len(arrs)

    def body(*refs):
        ins, outs = refs[:n], refs[n:2 * n]
        send_sems, recv_sems = refs[2 * n:]
        x, y, c = _place()
        copies = []
        for a in range(n):
            for q, chip in enumerate(_CHIPS):
                copies.append(pltpu.make_async_remote_copy(
                    src_ref=ins[a].at[_index(*chip, 1 - c)], dst_ref=outs[a].at[q],
                    send_sem=send_sems.at[a * _N_CHIPS + q], recv_sem=recv_sems.at[a * _N_CHIPS + q],
                    device_id=(x, y, 1 - c), device_id_type=MESH))
        for cp in copies:
            cp.start()
        for cp in copies:
            cp.wait_send()
            cp.wait_recv()

    return pl.pallas_call(
        body, name=name, in_specs=[_ANY] * n, out_specs=[_ANY] * n,
        out_shape=[S((_N_CHIPS,) + a.shape[1:], a.dtype) for a in arrs],
        scratch_shapes=[pltpu.SemaphoreType.DMA((_N_CHIPS * n,)), pltpu.SemaphoreType.DMA((_N_CHIPS * n,))],
    )(*arrs)


def _add_partials(mine, theirs, *, tr, name):
    _, r, c = mine.shape

    def body(mine_ref, theirs_ref, out_ref):
        core = lax.axis_index("c")
        own = jnp.where(core == 0, mine_ref[0].astype(F32), mine_ref[1].astype(F32))
        out_ref[0] = (own + theirs_ref[0].astype(F32)).astype(out_ref.dtype)

    return pl.pallas_call(
        body, name=name, grid=(_N_CHIPS, r // tr),
        in_specs=[pl.BlockSpec((2, tr, c), lambda q, i: (q, i, 0)), pl.BlockSpec((1, tr, c), lambda q, i: (q, i, 0))],
        out_specs=pl.BlockSpec((1, tr, c), lambda q, i: (q, i, 0)),
        out_shape=S((_N_CHIPS, r, c), mine.dtype),
        compiler_params=_params(("arbitrary", "arbitrary")),
    )(mine, theirs)


def _adamw(w, g, m, v):
    m = ADAM_B1 * m + (1.0 - ADAM_B1) * g
    v = ADAM_B2 * v + (1.0 - ADAM_B2) * (g * g)
    m_hat = m / (1.0 - ADAM_B1 ** ADAM_STEP)
    v_hat = v / (1.0 - ADAM_B2 ** ADAM_STEP)
    return -ADAM_LR * (m_hat / (jnp.sqrt(v_hat) + ADAM_EPS) + ADAM_WD * w), m, v


def _sum_devices(ref, rows):
    total = ref[0, rows, :].astype(F32)
    for s in range(1, ref.shape[0]):
        total = total + ref[s, rows, :].astype(F32)
    return total


def _adam_big(recv, w, m, v, *, tr, name):
    r, c = w.shape

    def body(recv_ref, w_ref, m_ref, v_ref, g_ref, d_ref, m2_ref, v2_ref):
        g = _sum_devices(recv_ref, slice(None))
        g_ref[...] = g
        d_ref[...], m2_ref[...], v2_ref[...] = _adamw(w_ref[...], g, m_ref[...], v_ref[...])

    blk = pl.BlockSpec((tr, c), lambda i: (i, 0))
    return pl.pallas_call(
        body, name=name, grid=(r // tr,),
        in_specs=[pl.BlockSpec((recv.shape[0], tr, c), lambda i: (0, i, 0)), blk, blk, blk],
        out_specs=[blk] * 4, out_shape=[S((r, c), F32)] * 4,
        compiler_params=_params(("arbitrary",)),
    )(recv, w, m, v)


_REPLICATED = {"even_pre_g": (0, 0, 1), "even_mem_g": (0, 8, 1), "even_post_g": (0, 16, 1),
               "even_a_ln_g": (1, 0, 1), "even_a_ln_b": (1, 8, 1),
               "even_a_ws": (2, 0, NH * CHUNK), "even_a_bs": (2, NH * CHUNK, NH),
               "odd_c_wgrp": (3, 0, NH * GRP)}
_SHARDED = {"odd_pre_g": (4, 0, 1), "odd_mem_g": (4, 8, 1), "odd_post_g": (4, 16, 1),
            "even_b_conv": (5, 0, 3), "odd_c_scale": (5, 8, 1), "odd_d_dw_w": (5, 16, CONF_K),
            "odd_d_dw_b": (5, 48, 1), "odd_d_ln_g": (5, 56, 1), "odd_d_ln_b": (5, 64, 1), "odd_d_pw_b": (5, 72, 1)}
_SMALL = {**_REPLICATED, **_SHARDED}
_SMALL_ROWS = {0: 24, 1: 16, 2: NH * CHUNK + 8, 3: NH * GRP, 4: 24, 5: 80}


def _adam_small(sources, wmv):
    names = list(_SMALL)
    ns = len(sources)

    def body(*refs):
        src = refs[:ns]
        ins = refs[ns:ns + 3 * len(names)]
        outs = refs[ns + 3 * len(names):]
        outs[-1][...] = _sum_devices(src[-1], slice(0, 1))
        for i, nm in enumerate(names):
            a, row0, rows = _SMALL[nm]
            g = _sum_devices(src[a], slice(row0, row0 + rows))
            w_ref, m_ref, v_ref = ins[3 * i:3 * i + 3]
            g_ref, d_ref, m2_ref, v2_ref = outs[4 * i:4 * i + 4]
            g_ref[...] = g
            d_ref[...], m2_ref[...], v2_ref[...] = _adamw(w_ref[...], g, m_ref[...], v_ref[...])

    flat = [t for nm in names for t in wmv[nm]]
    out_shape = [S(wmv[nm][0].shape, F32) for nm in names for _ in range(4)] + [S((1, HD), F32)]
    res = pl.pallas_call(
        body, name="adam_small", in_specs=[_whole()] * (ns + len(flat)), out_specs=[_whole()] * len(out_shape),
        out_shape=out_shape, compiler_params=_params(),
    )(*sources, *flat)
    return {nm: tuple(res[4 * i:4 * i + 4]) for i, nm in enumerate(names)}, res[-1]


_WEIGHTS = ["even_pre_g", "even_w_in", "even_a_ln_g", "even_a_ln_b", "even_a_ws", "even_a_bs", "even_b_conv",
            "even_mem_g", "even_w_kv", "even_w_out", "even_post_g", "odd_pre_g", "odd_w_in", "odd_c_wgrp",
            "odd_c_scale", "odd_d_dw_w", "odd_d_dw_b", "odd_d_ln_g", "odd_d_ln_b", "odd_d_pw_w", "odd_d_pw_b",
            "odd_mem_g", "odd_w_kv", "odd_w_out", "odd_post_g"]
_TRANSPOSED = ["even_w_in", "odd_w_in"]
_BIG = _TRANSPOSED + ["even_w_kv", "even_w_out", "odd_w_kv", "odd_w_out", "odd_d_pw_w"]
_BIG_TILE_ROWS = {"even_w_in": 400, "odd_w_in": 304, "even_w_kv": 128, "even_w_out": 128, "odd_w_kv": 128,
                  "odd_w_out": 128, "odd_d_pw_w": 96}


def _view2d(a, transposed):
    a = a[0]
    if a.ndim == 1:
        return a[None]
    if transposed:
        return a.T
    return a.reshape(-1, a.shape[-1])


def _rows8(a):
    return _pad_rows(a, -(-a.shape[0] // 8) * 8)


def _pack_rows(parts):
    return jnp.concatenate([_rows8(p) for p in parts], axis=0)


def _unshard_cols(a):
    return jnp.transpose(a, (1, 0, 2)).reshape(a.shape[1], N_DEV * a.shape[2])


def _shard_cols(a):
    return jnp.transpose(a.reshape(a.shape[0], N_DEV, a.shape[1] // N_DEV), (1, 0, 2))


def _rows_of(a):
    return a.reshape(-1, a.shape[-1])


_GROUPS = {"odd": (["odd_w_in", "odd_w_out", "odd_w_kv", "odd_d_pw_w"], [3], []),
           "even_rest": (["even_w_out", "even_w_kv"], [1, 2], []),
           "even_in": (["even_w_in"], [], [4, 5]),
           "even_gains": ([], [0], [])}


_TWO_LEVEL = ("even_in",)


class _MeshExchange:
    def __init__(self, shard):
        self.shard = shard
        self.handles = {}

    def first(self):
        shard = self.shard
        packs = [_pack_rows([shard[nm] for nm in _SHARDED if _SHARDED[nm][0] == a]) for a in (4, 5)]
        w_in, p128, p96 = _all_gather([shard["even_w_in"].astype(BF16)] + packs, "gather_first")
        w = {nm: shard[nm] for nm in _REPLICATED}
        w["even_a_ws"] = w["even_a_ws"].reshape(NH, CHUNK, CHUNK)
        w["odd_c_wgrp"] = w["odd_c_wgrp"].reshape(NH, GRP, GRP)
        w["even_w_in"] = _rows_of(w_in)
        full_packs = {4: _unshard_cols(p128), 5: _unshard_cols(p96)}
        for nm, (a, row0, rows) in _SHARDED.items():
            w[nm] = full_packs[a][row0:row0 + rows]
        later = lambda names: [("gather_chips", shard[nm].astype(BF16)) for nm in names]
        self.handles["w_even"], token = _exchange_start(later(["even_w_kv", "even_w_out"]), "gather_even_start",
                                                        deps=(w_in,))
        self.handles["w_odd"], token = _exchange_start(later(["odd_w_in", "odd_w_kv", "odd_w_out", "odd_d_pw_w"]),
                                                       "gather_odd_start", deps=(token,))
        return w, (token,)

    def even_rest(self, after):
        landed = _exchange_wait(self.handles.pop("w_even"), (after,), "gather_even_wait")
        kv, out = _sibling_forward(landed, "forward_even")
        return {"even_w_kv": _rows_of(kv), "even_w_out": _rows_of(out)}

    def odd(self, after):
        landed = _exchange_wait(self.handles.pop("w_odd"), (after,), "gather_odd_wait")
        w_in, kv, out, pw = _sibling_forward(landed, "forward_odd")
        return {"odd_w_in": _rows_of(w_in), "odd_w_kv": _rows_of(kv), "odd_w_out": _rows_of(out),
                "odd_d_pw_w": _rows_of(pw)}

    def send(self, group, g):
        big, replicated, sharded = _GROUPS[group]
        by_owner = [g[nm].reshape(N_DEV, -1, g[nm].shape[-1]) for nm in big]
        if group in _TWO_LEVEL:
            theirs = _sibling_swap(by_owner, "swap_" + group)
            items = [("scatter_chips", _add_partials(a, b, tr=_BIG_TILE_ROWS[nm], name="chip_sum_" + nm))
                     for nm, a, b in zip(big, by_owner, theirs)]
        else:
            items = [("scatter", a) for a in by_owner]
        items += [("gather", _pack_rows([g[nm] for nm in _REPLICATED if _REPLICATED[nm][0] == a]))
                  for a in replicated]
        items += [("scatter", _shard_cols(_pack_rows([g[nm] for nm in _SHARDED if _SHARDED[nm][0] == a])))
                  for a in sharded]
        if group == "odd":
            items.append(("gather", _rows8(g["loss"])))
        self.handles[group], token = _exchange_start(items, "send_" + group + "_start")
        return (token,)

    def receive(self, group, after):
        after = after if isinstance(after, tuple) else (after,)
        return _exchange_wait(self.handles.pop(group), after, "send_" + group + "_wait")


def kernel(x, mem, even_pre_g, even_w_in, even_a_ln_g, even_a_ln_b, even_a_ws, even_a_bs, even_b_conv, even_mem_g, even_w_kv, even_w_out, even_post_g, odd_pre_g, odd_w_in, odd_c_wgrp, odd_c_scale, odd_d_dw_w, odd_d_dw_b, odd_d_ln_g, odd_d_ln_b, odd_d_pw_w, odd_d_pw_b, odd_mem_g, odd_w_kv, odd_w_out, odd_post_g, loss_target, m_even_pre_g, m_even_w_in, m_even_a_ln_g, m_even_a_ln_b, m_even_a_ws, m_even_a_bs, m_even_b_conv, m_even_mem_g, m_even_w_kv, m_even_w_out, m_even_post_g, m_odd_pre_g, m_odd_w_in, m_odd_c_wgrp, m_odd_c_scale, m_odd_d_dw_w, m_odd_d_dw_b, m_odd_d_ln_g, m_odd_d_ln_b, m_odd_d_pw_w, m_odd_d_pw_b, m_odd_mem_g, m_odd_w_kv, m_odd_w_out, m_odd_post_g, v_even_pre_g, v_even_w_in, v_even_a_ln_g, v_even_a_ln_b, v_even_a_ws, v_even_a_bs, v_even_b_conv, v_even_mem_g, v_even_w_kv, v_even_w_out, v_even_post_g, v_odd_pre_g, v_odd_w_in, v_odd_c_wgrp, v_odd_c_scale, v_odd_d_dw_w, v_odd_d_dw_b, v_odd_d_ln_g, v_odd_d_ln_b, v_odd_d_pw_w, v_odd_d_pw_b, v_odd_mem_g, v_odd_w_kv, v_odd_w_out, v_odd_post_g):
    given = dict(locals())
    view = lambda nm, kind: _view2d(given[kind + nm], nm in _TRANSPOSED)
    shard = {nm: view(nm, "") for nm in _WEIGHTS}
    wmv = {nm: (shard[nm], view(nm, "m_"), view(nm, "v_")) for nm in _WEIGHTS}

    ex = _MeshExchange(shard)
    grad_x, last = _step(x[0], mem[0], loss_target[0], ex)

    res = {}

    def update(group, after):
        names = _GROUPS[group][0]
        landed = ex.receive(group, after)
        for nm, recv in zip(names, landed):
            res[nm] = _adam_big(recv, *wmv[nm], tr=_BIG_TILE_ROWS[nm], name="adam_" + nm)
        return landed[len(names):]

    c192, losses = update("odd", last)
    c768, c128 = update("even_rest", res["odd_d_pw_w"][0])
    a128, a96 = update("even_in", res["even_w_kv"][0])
    (c1024,) = update("even_gains", res["even_w_in"][0])
    small, loss = _adam_small([c1024, c768, c128, c192, a128, a96, losses], {nm: wmv[nm] for nm in _SMALL})
    res.update(small)
    total = loss[0, 0]
    back = lambda nm, a: (a.T if nm in _TRANSPOSED else a).reshape(given[nm].shape)
    outs = [[back(nm, res[nm][i]) for nm in _WEIGHTS] for i in range(4)]
    return (total, grad_x[None], *outs[0], *outs[1], *outs[2], *outs[3])
```

```python
import functools

import jax
import jax.numpy as jnp
from jax import lax
from jax.experimental import pallas as pl
from jax.experimental.pallas import tpu as pltpu

F32 = jnp.float32
BF16 = jnp.bfloat16
S = jax.ShapeDtypeStruct
MESH = pl.DeviceIdType.MESH
AXES = ("x", "y", "c")
N_DEV = 8

D = 1024
BW = 768
XA = 512
HD = 128
NH = 4
MIX = 2048
CHUNK = 128
GRP = 192
N_MEM = 256
CONF_K = 31
EPS = 1e-6
HALO = 32
POOL_WINDOWS = (2, 4, 8, 16)
TM_FWD_EVEN = 512
TM_FWD_ODD = 256
TM_BWD_EVEN = 256
TM_BWD_ODD = 256
RB = 16

E_U, E_V, E_BG, E_CG, E_XIN, E_Q, E_GATE = 0, 768, 1536, 2304, 3072, 3840, 4352
EVEN_IN = 6400
O_ZC, O_GA, O_GB, O_Q, O_GATE = 0, 768, 1536, 2304, 2816
ODD_IN = 4864

ADAM_LR, ADAM_B1, ADAM_B2, ADAM_EPS, ADAM_WD, ADAM_STEP = 0.001, 0.9, 0.999, 1e-08, 0.01, 10

VMEM_LIMIT_V7X = 56 * 1024 * 1024
VMEM_LIMIT_ODD_BWD_V7X = 62 * 1024 * 1024


def _params(sem=None):
    return pltpu.CompilerParams(dimension_semantics=sem, vmem_limit_bytes=VMEM_LIMIT_V7X)


def _dot(a, b):
    return jnp.dot(a, b, preferred_element_type=F32)


def _dot_nt(a, b):
    return lax.dot_general(a, b, (((1,), (1,)), ((), ())), preferred_element_type=F32)


def _dot_tn(a, b):
    return lax.dot_general(a, b, (((0,), (0,)), ((), ())), preferred_element_type=F32)


def _sigmoid(z):
    return 1.0 / (1.0 + jnp.exp(-z))


def _rowmean(a):
    return jnp.mean(a, axis=-1, keepdims=True)


def _colsum(a):
    return jnp.sum(a, axis=0, keepdims=True)


def _ln_stats(v):
    mu = _rowmean(v)
    vc = v - mu
    rs = lax.rsqrt(_rowmean(vc * vc) + EPS)
    return vc * rs, rs


def _ln_bwd(dn, vh, rs, g):
    dvh = dn * g
    return rs * (dvh - _rowmean(dvh) - vh * _rowmean(dvh * vh))


def _group_masks():
    col = lax.broadcasted_iota(jnp.int32, (1, BW), 1)
    return [((col >= GRP * h) & (col < GRP * (h + 1))).astype(F32) for h in range(NH)]


def _full(shape):
    nd = len(shape)
    return pl.BlockSpec(shape, lambda *_: (0,) * nd)


def _whole():
    return pl.BlockSpec(memory_space=pltpu.VMEM)


_ANY = pl.BlockSpec(memory_space=pl.ANY)


def _after(body, n_in, deps):
    def ordered(*refs):
        return body(*refs[:n_in], *refs[n_in + len(deps):])
    return ordered


def _rms_matmul(x, g, w, *, tm, name, transposed=False, out_dtype=F32, deps=()):
    t, d = x.shape
    n = w.shape[0] if transposed else w.shape[1]

    def body(x_ref, g_ref, w_ref, p_ref, h_ref):
        xv = x_ref[...]
        r = lax.rsqrt(_rowmean(xv * xv) + EPS)
        h = (xv * r * g_ref[...]).astype(BF16)
        h_ref[...] = h
        p_ref[...] = (_dot_nt(h, w_ref[...]) if transposed else _dot(h, w_ref[...])).astype(out_dtype)

    return pl.pallas_call(
        _after(body, 3, deps), name=name, grid=(t // tm,),
        in_specs=[pl.BlockSpec((tm, d), lambda i: (i, 0)), _whole(), _whole()] + [_ANY] * len(deps),
        out_specs=[pl.BlockSpec((tm, n), lambda i: (i, 0)), pl.BlockSpec((tm, d), lambda i: (i, 0))],
        out_shape=[S((t, n), out_dtype), S((t, d), BF16)],
        compiler_params=_params(("arbitrary",)),
    )(x, g, w, *deps)


def _nt_matmul_rms_bwd(dp, w, x, g, dres, *, tm, name, transposed=False, deps=()):
    t, n = dp.shape
    d = x.shape[1]

    def body(dp_ref, w_ref, x_ref, g_ref, dres_ref, dx_ref, dg_ref):
        @pl.when(pl.program_id(0) == 0)
        def _():
            dg_ref[...] = jnp.zeros_like(dg_ref)

        dh = _dot(dp_ref[...], w_ref[...]) if transposed else _dot_nt(dp_ref[...], w_ref[...])
        xv = x_ref[...]
        r = lax.rsqrt(_rowmean(xv * xv) + EPS)
        xh = xv * r
        dg_ref[...] += _colsum(dh * xh)
        dxh = dh * g_ref[...]
        dx_ref[...] = dres_ref[...] + r * (dxh - xh * _rowmean(dxh * xh))

    return pl.pallas_call(
        _after(body, 5, deps), name=name, grid=(t // tm,),
        in_specs=[pl.BlockSpec((tm, n), lambda i: (i, 0)), _whole(), pl.BlockSpec((tm, d), lambda i: (i, 0)),
                  _whole(), pl.BlockSpec((tm, d), lambda i: (i, 0))] + [_ANY] * len(deps),
        out_specs=[pl.BlockSpec((tm, d), lambda i: (i, 0)), pl.BlockSpec((1, d), lambda i: (0, 0))],
        out_shape=[S((t, d), F32), S((1, d), F32)],
        compiler_params=_params(("arbitrary",)),
    )(dp, w, x, g, dres, *deps)


def _tn_matmul(a, b, *, tmc, tk, out_dtype, name, deps=()):
    t, m = a.shape
    n = b.shape[1]
    nk = t // tk

    def body(a_ref, b_ref, o_ref, acc_ref):
        k = pl.program_id(1)

        @pl.when(k == 0)
        def _():
            acc_ref[...] = jnp.zeros_like(acc_ref)

        acc_ref[...] += _dot_tn(a_ref[...], b_ref[...])

        @pl.when(k == nk - 1)
        def _():
            o_ref[...] = acc_ref[...].astype(out_dtype)

    return pl.pallas_call(
        _after(body, 2, deps), name=name, grid=(m // tmc, nk),
        in_specs=[pl.BlockSpec((tk, tmc), lambda j, k: (k, j)), pl.BlockSpec((tk, n), lambda j, k: (k, 0))]
        + [_ANY] * len(deps),
        out_specs=pl.BlockSpec((tmc, n), lambda j, k: (j, 0)),
        out_shape=S((m, n), out_dtype),
        scratch_shapes=[pltpu.VMEM((tmc, n), F32)],
        compiler_params=_params(("arbitrary", "arbitrary")),
    )(a, b, *deps)


def _kv_backward(memn, dkv, w, mem, *, name):
    d, n = w.shape

    def body(memn_ref, dkv_ref, w_ref, mem_ref, dw_ref, dg_ref):
        dw_ref[...] = _dot_tn(memn_ref[...], dkv_ref[...]).astype(BF16)
        dh = _dot_nt(dkv_ref[...], w_ref[...])
        xv = mem_ref[...]
        dg_ref[...] = _colsum(dh * (xv * lax.rsqrt(_rowmean(xv * xv) + EPS)))

    return pl.pallas_call(
        body, name=name, in_specs=[_whole()] * 4, out_specs=[_whole()] * 2,
        out_shape=[S((d, n), BF16), S((1, d), F32)], compiler_params=_params(),
    )(memn, dkv, w, mem)


def _silu_parts(gt):
    sg = _sigmoid(gt)
    return gt * sg, sg * (1.0 + gt * (1.0 - sg))


def _attn_head(q_b, k_b, v_b):
    s = _dot_nt(q_b, k_b) * (HD ** -0.5)
    e = jnp.exp(s - jnp.max(s, axis=-1, keepdims=True))
    prob = e / jnp.sum(e, axis=-1, keepdims=True)
    return prob, _dot(prob.astype(BF16), v_b)


def _rms_residual(x, o, g):
    r = lax.rsqrt(_rowmean(o * o) + EPS)
    return x + o * r * g


def _rms_post_bwd(dres, o, g):
    r = lax.rsqrt(_rowmean(o * o) + EPS)
    oh = o * r
    doh = dres * g
    return r * (doh - oh * _rowmean(doh * oh)), _colsum(dres * oh)


LANE = 128
_TILE_GROUPS = [sorted({LANE * j // GRP, (LANE * j + LANE - 1) // GRP}) for j in range(BW // LANE)]


def _tile(j):
    return slice(LANE * j, LANE * (j + 1))


def _low_lanes():
    return lax.broadcasted_iota(jnp.int32, (1, LANE), 1) < GRP - LANE


def _by_group(fn):
    tiles = []
    for j, groups in enumerate(_TILE_GROUPS):
        if len(groups) == 1:
            tiles.append(fn(groups[0], j))
        else:
            tiles.append(jnp.where(_low_lanes(), fn(groups[0], j), fn(groups[1], j)))
    return jnp.concatenate(tiles, axis=1)


def _sgu_chunk(vn_b, ws_ref, bmap_ref):
    return bmap_ref[...] + _by_group(lambda h, j: _dot(ws_ref[h], vn_b[:, _tile(j)]))


def _shift_copies(buf, sh):
    n = buf.shape[0] - 8
    for b in range(1, 8):
        sh[b - 1, pl.ds(0, n), :] = buf[pl.ds(b, n), :]


def _loop_rows(rows, step, fn, carry=0, unrolled=True):
    if unrolled:
        for r0 in range(0, rows, step):
            carry = fn(r0, carry)
        return carry

    def body(j, c):
        return fn(pl.multiple_of(j * step, step), c)
    return lax.fori_loop(0, rows // step, body, carry)


def _rows_at(buf, sh, r0, off):
    b = off % 8
    if b == 0 or sh is None:
        return buf[pl.ds(r0 + off, 32), :]
    return sh[b - 1, pl.ds(r0 + (off - b), 32), :]


def _tap_sum(buf, sh, w_ref, r0, taps, causal):
    acc = None
    for k in range(taps):
        off = HALO - (taps - 1 - k) if causal else taps - 1 - k
        term = w_ref[k:k + 1, :] * _rows_at(buf, sh, r0, off)
        acc = term if acc is None else acc + term
    return acc


def _fold8(a):
    return a[0:8] + a[8:16] + a[16:24] + a[24:32]


def _conv_back(buf, sh, w_ref, acc_ref, z, r0, taps):
    dz = None
    for k in range(taps):
        ahead = _rows_at(buf, sh, r0, taps - 1 - k)
        term = w_ref[k:k + 1, :] * ahead
        dz = term if dz is None else dz + term
        acc_ref[k * 8:(k + 1) * 8, :] += _fold8(z * ahead)
    return dz


def _halo_spec(n, nt, reverse, tm):
    per = tm // HALO
    if reverse:
        return pl.BlockSpec((HALO, n), lambda i: (jnp.maximum((nt - 1 - i) * per - 1, 0), 0))
    return pl.BlockSpec((HALO, n), lambda i: (jnp.maximum(i * per - 1, 0), 0))


def _even_fwd(x, p, lng, lnb, ws, bmap, wc, kv, wout, pg):
    t = x.shape[0]
    tm = min(TM_FWD_EVEN, t)
    nt = t // tm

    def body(x_ref, p_ref, ph_ref, lng_ref, lnb_ref, ws_ref, bmap_ref, wc_ref, kv_ref, wout_ref, pg_ref,
             o_ref, x1_ref, ybuf, cbuf):
        i = pl.program_id(0)
        vh, _ = _ln_stats(p_ref[:, E_V:E_V + BW])
        vn = vh * lng_ref[...] + lnb_ref[...]
        for c in range(tm // CHUNK):
            sl = slice(c * CHUNK, (c + 1) * CHUNK)
            sg = _sgu_chunk(vn[sl].astype(BF16), ws_ref, bmap_ref)
            gate, _ = _silu_parts(p_ref[sl, E_GATE:E_GATE + BW])
            ybuf[sl, 0:BW] = (p_ref[sl, E_U:E_U + BW] * sg * gate).astype(BF16)

        cbuf[0:HALO] = jnp.where(i > 0, ph_ref[:, E_CG:E_CG + BW] * ph_ref[:, E_XIN:E_XIN + BW], 0.0)
        cbuf[HALO:HALO + tm] = p_ref[:, E_CG:E_CG + BW] * p_ref[:, E_XIN:E_XIN + BW]
        for r0 in range(0, tm, 32):
            sl = slice(r0, r0 + 32)
            cv = _tap_sum(cbuf, None, wc_ref, r0, 3, True)
            gate, _ = _silu_parts(p_ref[sl, E_GATE + BW:E_GATE + 2 * BW])
            ybuf[sl, BW:2 * BW] = (p_ref[sl, E_BG:E_BG + BW] * cv * gate).astype(BF16)

        for h in range(NH):
            qs = slice(E_Q + h * HD, E_Q + (h + 1) * HD)
            _, yx = _attn_head(p_ref[:, qs].astype(BF16), kv_ref[:, h * HD:(h + 1) * HD],
                               kv_ref[:, XA + h * HD:XA + (h + 1) * HD])
            gs = slice(E_GATE + 2 * BW + h * HD, E_GATE + 2 * BW + (h + 1) * HD)
            gate, _ = _silu_parts(p_ref[:, gs])
            ybuf[:, 2 * BW + h * HD:2 * BW + (h + 1) * HD] = (yx * gate).astype(BF16)

        o = _dot(ybuf[...], wout_ref[...])
        o_ref[...] = o
        x1_ref[...] = _rms_residual(x_ref[...], o, pg_ref[...])

    tile = lambda n: pl.BlockSpec((tm, n), lambda i: (i, 0))
    return pl.pallas_call(
        body, name="even_fwd", grid=(nt,),
        in_specs=[tile(D), tile(EVEN_IN), _halo_spec(EVEN_IN, nt, False, tm)] + [_whole()] * 8,
        out_specs=[tile(D), tile(D)],
        out_shape=[S((t, D), F32), S((t, D), F32)],
        scratch_shapes=[pltpu.VMEM((tm, MIX), BF16), pltpu.VMEM((tm + HALO, BW), F32)],
        compiler_params=_params(("arbitrary",)),
    )(x, p, p, lng, lnb, ws, bmap, wc, kv, wout, pg)


def _even_bwd(dres, o, p, lng, lnb, ws, wst, bmap, wc, kv, wout, pg):
    t = dres.shape[0]
    tm = min(TM_BWD_EVEN, t)
    nt = t // tm

    def body(dres_ref, o_ref, p_ref, ph_ref, lng_ref, lnb_ref, ws_ref, wst_ref, bmap_ref, wc_ref, kv_ref, wout_ref,
             pg_ref, dp_ref, y_ref, do_ref, dpg_ref, dws_ref, dbs_ref, dlng_ref, dlnb_ref, dwc_ref, dkv_ref,
             dy, cbuf, gbuf, dconv, carry, dvn, dbmap, wacc):
        i = pl.program_id(0)
        ti = nt - 1 - i
        masks = _group_masks()

        @pl.when(i == 0)
        def _():
            for ref in (dpg_ref, dws_ref, dlng_ref, dlnb_ref, dkv_ref, dbmap, wacc):
                ref[...] = jnp.zeros_like(ref)

        do, dpg = _rms_post_bwd(dres_ref[...], o_ref[...], pg_ref[...])
        dpg_ref[...] += dpg
        do_b = do.astype(BF16)
        do_ref[...] = do_b
        dy[...] = _dot_nt(do_b, wout_ref[...])

        vh, rs = _ln_stats(p_ref[:, E_V:E_V + BW])
        vn = vh * lng_ref[...] + lnb_ref[...]
        for c in range(tm // CHUNK):
            sl = slice(c * CHUNK, (c + 1) * CHUNK)
            vn_b = vn[sl].astype(BF16)
            sg = _sgu_chunk(vn_b, ws_ref, bmap_ref)
            u = p_ref[sl, E_U:E_U + BW]
            gate, dgate = _silu_parts(p_ref[sl, E_GATE:E_GATE + BW])
            dyc = dy[sl, 0:BW]
            ya = u * sg
            y_ref[sl, 0:BW] = (ya * gate).astype(BF16)
            dp_ref[sl, E_GATE:E_GATE + BW] = (dyc * ya * dgate).astype(BF16)
            dya = dyc * gate
            dp_ref[sl, E_U:E_U + BW] = (dya * sg).astype(BF16)
            dsg = dya * u
            dbmap[...] += dsg
            dsg_b = dsg.astype(BF16)
            for h in range(NH):
                total = None
                for j, heads in enumerate(_TILE_GROUPS):
                    if h in heads:
                        d_t = dsg_b[:, _tile(j)]
                        if len(heads) == 2:
                            d_t = jnp.where(_low_lanes() == (h == heads[0]), d_t, jnp.zeros_like(d_t))
                        part = _dot_nt(d_t, vn_b[:, _tile(j)])
                        total = part if total is None else total + part
                dws_ref[h] += total
            dvn[sl, :] = _by_group(lambda h, j: _dot(wst_ref[h], dsg_b[:, _tile(j)]))
        dn = dvn[...]
        dlng_ref[...] += _colsum(dn * vh)
        dlnb_ref[...] += _colsum(dn)
        dp_ref[:, E_V:E_V + BW] = _ln_bwd(dn, vh, rs, lng_ref[...]).astype(BF16)

        cbuf[0:HALO] = jnp.where(ti > 0, ph_ref[:, E_CG:E_CG + BW] * ph_ref[:, E_XIN:E_XIN + BW], 0.0)
        cbuf[HALO:HALO + tm] = p_ref[:, E_CG:E_CG + BW] * p_ref[:, E_XIN:E_XIN + BW]
        for r0 in range(0, tm, 32):
            sl = slice(r0, r0 + 32)
            cv = _tap_sum(cbuf, None, wc_ref, r0, 3, True)
            gate, dgate = _silu_parts(p_ref[sl, E_GATE + BW:E_GATE + 2 * BW])
            bg = p_ref[sl, E_BG:E_BG + BW]
            dyc = dy[sl, BW:2 * BW]
            yb = bg * cv
            y_ref[sl, BW:2 * BW] = (yb * gate).astype(BF16)
            dp_ref[sl, E_GATE + BW:E_GATE + 2 * BW] = (dyc * yb * dgate).astype(BF16)
            dyb = dyc * gate
            dp_ref[sl, E_BG:E_BG + BW] = (dyb * cv).astype(BF16)
            dconv[sl, :] = dyb * bg
        gbuf[0:tm] = dconv[...]
        gbuf[tm:tm + HALO] = jnp.where(i > 0, carry[...], 0.0)
        carry[...] = dconv[0:HALO]
        for r0 in range(0, tm, 32):
            sl = slice(r0, r0 + 32)
            dc = _conv_back(gbuf, None, wc_ref, wacc, cbuf[HALO + r0:HALO + r0 + 32, :], r0, 3)
            dp_ref[sl, E_CG:E_CG + BW] = (dc * p_ref[sl, E_XIN:E_XIN + BW]).astype(BF16)
            dp_ref[sl, E_XIN:E_XIN + BW] = (dc * p_ref[sl, E_CG:E_CG + BW]).astype(BF16)

        for h in range(NH):
            qs = slice(E_Q + h * HD, E_Q + (h + 1) * HD)
            ks = slice(h * HD, (h + 1) * HD)
            vs = slice(XA + h * HD, XA + (h + 1) * HD)
            gs = slice(E_GATE + 2 * BW + h * HD, E_GATE + 2 * BW + (h + 1) * HD)
            ys = slice(2 * BW + h * HD, 2 * BW + (h + 1) * HD)
            q_b = p_ref[:, qs].astype(BF16)
            prob, yx = _attn_head(q_b, kv_ref[:, ks], kv_ref[:, vs])
            gate, dgate = _silu_parts(p_ref[:, gs])
            dyc = dy[:, ys]
            y_ref[:, ys] = (yx * gate).astype(BF16)
            dp_ref[:, gs] = (dyc * yx * dgate).astype(BF16)
            dyx_b = (dyc * gate).astype(BF16)
            dprob = _dot_nt(dyx_b, kv_ref[:, vs])
            dkv_ref[:, vs] += _dot_tn(prob.astype(BF16), dyx_b)
            ds_b = (prob * (dprob - jnp.sum(dprob * prob, axis=-1, keepdims=True)) * (HD ** -0.5)).astype(BF16)
            dp_ref[:, qs] = _dot(ds_b, kv_ref[:, ks]).astype(BF16)
            dkv_ref[:, ks] += _dot_tn(ds_b, q_b)

        @pl.when(i == nt - 1)
        def _():
            for h in range(NH):
                dbs_ref[:, h * HD:(h + 1) * HD] = jnp.broadcast_to(
                    jnp.sum(dbmap[...] * masks[h], axis=-1, keepdims=True), (CHUNK, HD))
            for k in range(3):
                dwc_ref[k:k + 1, :] = _colsum(wacc[k * 8:(k + 1) * 8, :])
            dwc_ref[3:8, :] = jnp.zeros((5, BW), F32)
            causal = (lax.broadcasted_iota(jnp.int32, (CHUNK, CHUNK), 0)
                      >= lax.broadcasted_iota(jnp.int32, (CHUNK, CHUNK), 1))
            for h in range(NH):
                dws_ref[h] = jnp.where(causal, dws_ref[h], 0.0)

    rtile = lambda n: pl.BlockSpec((tm, n), lambda i: (nt - 1 - i, 0))
    outs = [S((t, EVEN_IN), BF16), S((t, MIX), BF16), S((t, D), BF16), S((1, D), F32), S((NH, CHUNK, CHUNK), F32),
            S((CHUNK, NH * HD), F32), S((1, BW), F32), S((1, BW), F32), S((8, BW), F32), S((N_MEM, 2 * XA), F32)]
    return pl.pallas_call(
        body, name="even_bwd", grid=(nt,),
        in_specs=[rtile(D), rtile(D), rtile(EVEN_IN), _halo_spec(EVEN_IN, nt, True, tm)] + [_whole()] * 9,
        out_specs=[rtile(EVEN_IN), rtile(MIX), rtile(D)] + [_full(s.shape) for s in outs[3:]],
        out_shape=outs,
        scratch_shapes=[pltpu.VMEM((tm, MIX), F32), pltpu.VMEM((tm + HALO, BW), F32), pltpu.VMEM((tm + HALO, BW), F32),
                        pltpu.VMEM((tm, BW), F32), pltpu.VMEM((HALO, BW), F32), pltpu.VMEM((tm, BW), F32),
                        pltpu.VMEM((CHUNK, BW), F32), pltpu.VMEM((3 * 8, BW), F32)],
        compiler_params=_params(("arbitrary",)),
    )(dres, o, p, p, lng, lnb, ws, wst, bmap, wc, kv, wout, pg)


def _pool_causal_levels(za, zb, zc, zd, tm):
    n = tm + HALO
    zb[pl.ds(8, n - 8), :] = za[pl.ds(8, n - 8), :] + za[pl.ds(7, n - 8), :]
    zc[pl.ds(16, n - 16), :] = zb[pl.ds(16, n - 16), :] + zb[pl.ds(14, n - 16), :]
    zd[pl.ds(24, n - 24), :] = zc[pl.ds(24, n - 24), :] + zc[pl.ds(20, n - 24), :]


def _pool_causal(za, zb, zc, zd, tm):
    _pool_causal_levels(za, zb, zc, zd, tm)
    s16 = zd[pl.ds(HALO, tm), :] + zd[pl.ds(HALO - 8, tm), :]
    return zb[pl.ds(HALO, tm), :], zc[pl.ds(HALO, tm), :], zd[pl.ds(HALO, tm), :], s16


def _pool_anticausal_levels(ea, eb, ec, ed, tm):
    n = tm + HALO
    eb[pl.ds(0, n - 8), :] = ea[pl.ds(0, n - 8), :] + ea[pl.ds(1, n - 8), :]
    ec[pl.ds(0, n - 16), :] = eb[pl.ds(0, n - 16), :] + eb[pl.ds(2, n - 16), :]
    ed[pl.ds(0, n - 24), :] = ec[pl.ds(0, n - 24), :] + ec[pl.ds(4, n - 24), :]


def _pool_weights(t0, rows):
    tf = (t0 + lax.broadcasted_iota(jnp.int32, (rows, 1), 0) + 1).astype(F32)
    inv = [jnp.broadcast_to(1.0 / jnp.minimum(tf, float(win)), (rows, LANE)) for win in POOL_WINDOWS]
    return _by_group(lambda g, j: inv[g])


_HALVES = (slice(0, BW // 2), slice(BW // 2, BW))


def _mix4(parts):
    return _by_group(lambda g, j: parts[g][:, _tile(j)])


def _odd_fwd(x1, tgt, p, wbd, cscale, dww, dwb, lng, lnb, wpw, pwb, kv, wout, pg):
    t = x1.shape[0]
    tm = min(TM_FWD_ODD, t)
    nt = t // tm

    def body(x_ref, tgt_ref, p_ref, ph_ref, wbd_ref, cs_ref, dww_ref, dwb_ref, lng_ref, lnb_ref, wpw_ref, pwb_ref,
             kv_ref, wout_ref, pg_ref, o_ref, dres_ref, loss_ref, conv_ref, ybuf, za, zb, zc, zd, gbuf, lacc, gsh):
        i = pl.program_id(0)

        @pl.when(i == 0)
        def _():
            lacc[...] = jnp.zeros_like(lacc)

        z = p_ref[:, O_ZC:O_ZC + BW]
        za[0:HALO] = jnp.where(i > 0, ph_ref[:, O_ZC:O_ZC + BW], 0.0)
        za[HALO:HALO + tm] = z
        pooled = _mix4(_pool_causal(za, zb, zc, zd, tm)) * _pool_weights(i * tm, tm) - z
        pooled_b = pooled.astype(BF16)
        for hs in _HALVES:
            gate, _ = _silu_parts(p_ref[:, O_GATE + hs.start:O_GATE + hs.stop])
            ybuf[:, hs] = (_dot(pooled_b[:, hs], wbd_ref[hs, hs]) * cs_ref[:, hs] * gate).astype(BF16)

        gbuf[0:HALO] = jnp.where(i > 0, ph_ref[:, O_GA:O_GA + BW] * _sigmoid(ph_ref[:, O_GB:O_GB + BW]), 0.0)
        gbuf[HALO:HALO + tm] = p_ref[:, O_GA:O_GA + BW] * _sigmoid(p_ref[:, O_GB:O_GB + BW])
        _shift_copies(gbuf, gsh)
        def conv_rows(r0, carry):
            conv_ref[pl.ds(r0, 32), :] = _tap_sum(gbuf, gsh, dww_ref, r0, CONF_K, True) + dwb_ref[...]
            return carry

        _loop_rows(tm, 32, conv_rows)
        zh, _ = _ln_stats(conv_ref[...])
        zn = zh * lng_ref[...] + lnb_ref[...]
        yd = _dot((zn * _sigmoid(zn)).astype(BF16), wpw_ref[...]) + pwb_ref[...]
        gate, _ = _silu_parts(p_ref[:, O_GATE + BW:O_GATE + 2 * BW])
        ybuf[:, BW:2 * BW] = (yd * gate).astype(BF16)

        for h in range(NH):
            qs = slice(O_Q + h * HD, O_Q + (h + 1) * HD)
            _, yx = _attn_head(p_ref[:, qs].astype(BF16), kv_ref[:, h * HD:(h + 1) * HD],
                               kv_ref[:, XA + h * HD:XA + (h + 1) * HD])
            gs = slice(O_GATE + 2 * BW + h * HD, O_GATE + 2 * BW + (h + 1) * HD)
            gate, _ = _silu_parts(p_ref[:, gs])
            ybuf[:, 2 * BW + h * HD:2 * BW + (h + 1) * HD] = (yx * gate).astype(BF16)

        o = _dot(ybuf[...], wout_ref[...])
        o_ref[...] = o
        err = _rms_residual(x_ref[...], o, pg_ref[...]) - tgt_ref[...]
        lacc[...] += _colsum(err * err)
        dres_ref[...] = err * (1.0 / D)

        @pl.when(i == nt - 1)
        def _():
            loss_ref[...] = jnp.full((1, HD), jnp.sum(lacc[...]) * (0.5 / D), F32)

    tile = lambda n: pl.BlockSpec((tm, n), lambda i: (i, 0))
    ext = pltpu.VMEM((tm + HALO, BW), F32)
    return pl.pallas_call(
        body, name="odd_fwd", grid=(nt,),
        in_specs=[tile(D), tile(D), tile(ODD_IN), _halo_spec(ODD_IN, nt, False, tm)] + [_whole()] * 11,
        out_specs=[tile(D), tile(D), _full((1, HD)), tile(BW)],
        out_shape=[S((t, D), F32), S((t, D), F32), S((1, HD), F32), S((t, BW), F32)],
        scratch_shapes=[pltpu.VMEM((tm, MIX), BF16), ext, ext, ext, ext, ext,
                        pltpu.VMEM((1, D), F32), pltpu.VMEM((7, tm + HALO, BW), F32)],
        compiler_params=_params(("arbitrary",)),
    )(x1, tgt, p, p, wbd, cscale, dww, dwb, lng, lnb, wpw, pwb, kv, wout, pg)


def _odd_bwd(dres, o, p, conv, wbd, cscale, dww, dwb, lng, lnb, wpw, pwb, kv, wout, pg):
    t = dres.shape[0]
    tm = min(TM_BWD_ODD, t)
    nt = t // tm

    def body(dres_ref, o_ref, p_ref, ph_ref, conv_ref, wbd_ref, cs_ref, dww_ref, dwb_ref, lng_ref, lnb_ref, wpw_ref,
             pwb_ref, kv_ref, wout_ref, pg_ref, dp_ref, y_ref, do_ref, dpg_ref, dwbd_ref, dcs_ref, ddww_ref, ddwb_ref,
             dlng_ref, dlnb_ref, dwpw_ref, dpwb_ref, dkv_ref,
             dy, za, zb, zc, zd, carry_e, carry_d, wacc, shifted, t1, b1, b2):
        hbuf = zb
        i = pl.program_id(0)
        ti = nt - 1 - i

        @pl.when(i == 0)
        def _():
            for ref in (dpg_ref, dwbd_ref, dcs_ref, ddwb_ref, dlng_ref, dlnb_ref, dwpw_ref, dpwb_ref, dkv_ref, wacc):
                ref[...] = jnp.zeros_like(ref)

        def post_norm_rows(r0, acc):
            sl = pl.ds(r0, RB)
            ov, dv = o_ref[sl, :], dres_ref[sl, :]
            r = lax.rsqrt(_rowmean(ov * ov) + EPS)
            oh = ov * r
            doh = dv * pg_ref[...]
            do_ref[sl, :] = (r * (doh - oh * _rowmean(doh * oh))).astype(BF16)
            return acc + dv * oh

        dpg_ref[...] += _colsum(_loop_rows(tm, RB, post_norm_rows, jnp.zeros((RB, D), F32)))
        dy[...] = _dot_nt(do_ref[...], wout_ref[...])

        za[0:HALO] = jnp.where(ti > 0, ph_ref[:, O_ZC:O_ZC + BW], 0.0)
        za[HALO:HALO + tm] = p_ref[:, O_ZC:O_ZC + BW]
        _pool_causal_levels(za, zb, zc, zd, tm)

        def pooled_rows(r0, carry):
            sl = pl.ds(r0, RB)
            at = lambda ref, back=0: ref[pl.ds(HALO + r0 - back, RB), :]
            inv = _pool_weights(ti * tm + r0, RB)
            sums = (at(zb), at(zc), at(zd), at(zd) + at(zd, 8))
            b1[sl, :] = (_mix4(sums) * inv - p_ref[sl, O_ZC:O_ZC + BW]).astype(BF16)
            return carry

        _loop_rows(tm, RB, pooled_rows)
        for hs in _HALVES:
            t1[:, hs] = _dot(b1[:, hs], wbd_ref[hs, hs])

        def pool_gate_rows(r0, acc):
            sl = pl.ds(r0, RB)
            pm = t1[sl, :]
            gate, dgate = _silu_parts(p_ref[sl, O_GATE:O_GATE + BW])
            dyc = dy[sl, 0:BW]
            yc = pm * cs_ref[...]
            y_ref[sl, 0:BW] = (yc * gate).astype(BF16)
            dp_ref[sl, O_GATE:O_GATE + BW] = (dyc * yc * dgate).astype(BF16)
            dyc = dyc * gate
            b2[sl, :] = (dyc * cs_ref[...]).astype(BF16)
            return acc + dyc * pm

        dcs_ref[...] += _colsum(_loop_rows(tm, RB, pool_gate_rows, jnp.zeros((RB, BW), F32)))
        for hs in _HALVES:
            dwbd_ref[hs, hs] += _dot_tn(b1[:, hs], b2[:, hs])
            t1[:, hs] = _dot_nt(b2[:, hs], wbd_ref[hs, hs])

        def weighted_rows(r0, carry):
            sl = pl.ds(r0, RB)
            za[sl, :] = t1[sl, :] * _pool_weights(ti * tm + r0, RB)
            return carry

        _loop_rows(tm, RB, weighted_rows)
        za[tm:tm + HALO] = jnp.where(i > 0, carry_e[...], 0.0)
        carry_e[...] = za[0:HALO]
        _pool_anticausal_levels(za, zb, zc, zd, tm)

        def pool_back_rows(r0, carry):
            sl = pl.ds(r0, RB)
            ahead = lambda ref, fwd=0: ref[pl.ds(r0 + fwd, RB), :]
            sums = (ahead(zb), ahead(zc), ahead(zd), ahead(zd) + ahead(zd, 8))
            dp_ref[sl, O_ZC:O_ZC + BW] = (_mix4(sums) - t1[sl, :]).astype(BF16)
            return carry

        _loop_rows(tm, RB, pool_back_rows)

        def swish_rows(r0, carry):
            sl = pl.ds(r0, RB)
            zh, _ = _ln_stats(conv_ref[sl, :])
            zn = zh * lng_ref[...] + lnb_ref[...]
            b1[sl, :] = (zn * _sigmoid(zn)).astype(BF16)
            return carry

        _loop_rows(tm, RB, swish_rows)
        t1[...] = _dot(b1[...], wpw_ref[...])

        def conf_gate_rows(r0, acc):
            sl = pl.ds(r0, RB)
            yd = t1[sl, :] + pwb_ref[...]
            gate, dgate = _silu_parts(p_ref[sl, O_GATE + BW:O_GATE + 2 * BW])
            dyc = dy[sl, BW:2 * BW]
            y_ref[sl, BW:2 * BW] = (yd * gate).astype(BF16)
            dp_ref[sl, O_GATE + BW:O_GATE + 2 * BW] = (dyc * yd * dgate).astype(BF16)
            dyd = dyc * gate
            b2[sl, :] = dyd.astype(BF16)
            return acc + dyd

        dpwb_ref[...] += _colsum(_loop_rows(tm, RB, conf_gate_rows, jnp.zeros((RB, BW), F32)))
        dwpw_ref[...] += _dot_tn(b1[...], b2[...])
        t1[...] = _dot_nt(b2[...], wpw_ref[...])

        def norm_back_rows(r0, accs):
            sl = pl.ds(r0, RB)
            zh, rs = _ln_stats(conv_ref[sl, :])
            _, dsilu = _silu_parts(zh * lng_ref[...] + lnb_ref[...])
            dzn = t1[sl, :] * dsilu
            dzd = _ln_bwd(dzn, zh, rs, lng_ref[...])
            hbuf[sl, :] = dzd
            return accs[0] + dzn * zh, accs[1] + dzn, accs[2] + dzd

        zero = jnp.zeros((RB, BW), F32)
        acc_g, acc_b, acc_d = _loop_rows(tm, RB, norm_back_rows, (zero, zero, zero))
        dlng_ref[...] += _colsum(acc_g)
        dlnb_ref[...] += _colsum(acc_b)
        ddwb_ref[...] += _colsum(acc_d)
        hbuf[tm:tm + HALO] = jnp.where(i > 0, carry_d[...], 0.0)
        carry_d[...] = hbuf[0:HALO]
        _shift_copies(hbuf, shifted)

        def conv_back_rows(r0, carry):
            sl = pl.ds(r0, 32)
            sgb = _sigmoid(p_ref[sl, O_GB:O_GB + BW])
            ga = p_ref[sl, O_GA:O_GA + BW]
            dzg = _conv_back(hbuf, shifted, dww_ref, wacc, ga * sgb, r0, CONF_K)
            dp_ref[sl, O_GA:O_GA + BW] = (dzg * sgb).astype(BF16)
            dp_ref[sl, O_GB:O_GB + BW] = (dzg * ga * sgb * (1.0 - sgb)).astype(BF16)
            return carry

        _loop_rows(tm, 32, conv_back_rows, unrolled=False)

        for h in range(NH):
            qs = slice(O_Q + h * HD, O_Q + (h + 1) * HD)
            ks = slice(h * HD, (h + 1) * HD)
            vs = slice(XA + h * HD, XA + (h + 1) * HD)
            gs = slice(O_GATE + 2 * BW + h * HD, O_GATE + 2 * BW + (h + 1) * HD)
            ys = slice(2 * BW + h * HD, 2 * BW + (h + 1) * HD)
            q_b = p_ref[:, qs].astype(BF16)
            prob, yx = _attn_head(q_b, kv_ref[:, ks], kv_ref[:, vs])
            gate, dgate = _silu_parts(p_ref[:, gs])
            dyc = dy[:, ys]
            y_ref[:, ys] = (yx * gate).astype(BF16)
            dp_ref[:, gs] = (dyc * yx * dgate).astype(BF16)
            dyx_b = (dyc * gate).astype(BF16)
            dprob = _dot_nt(dyx_b, kv_ref[:, vs])
            dkv_ref[:, vs] += _dot_tn(prob.astype(BF16), dyx_b)
            ds_b = (prob * (dprob - jnp.sum(dprob * prob, axis=-1, keepdims=True)) * (HD ** -0.5)).astype(BF16)
            dp_ref[:, qs] = _dot(ds_b, kv_ref[:, ks]).astype(BF16)
            dkv_ref[:, ks] += _dot_tn(ds_b, q_b)

        @pl.when(i == nt - 1)
        def _():
            for k in range(CONF_K):
                ddww_ref[k:k + 1, :] = _colsum(wacc[k * 8:(k + 1) * 8, :])
            ddww_ref[CONF_K:CONF_K + 1, :] = jnp.zeros((1, BW), F32)

    rtile = lambda n: pl.BlockSpec((tm, n), lambda i: (nt - 1 - i, 0))
    outs = [S((t, ODD_IN), BF16), S((t, MIX), BF16), S((t, D), BF16), S((1, D), F32), S((BW, BW), F32),
            S((1, BW), F32), S((CONF_K + 1, BW), F32), S((1, BW), F32), S((1, BW), F32), S((1, BW), F32),
            S((BW, BW), F32), S((1, BW), F32), S((N_MEM, 2 * XA), F32)]
    ext = pltpu.VMEM((tm + HALO, BW), F32)
    return pl.pallas_call(
        body, name="odd_bwd", grid=(nt,),
        in_specs=[rtile(D), rtile(D), rtile(ODD_IN), _halo_spec(ODD_IN, nt, True, tm), rtile(BW)] + [_whole()] * 11,
        out_specs=[rtile(ODD_IN), rtile(MIX), rtile(D)] + [_full(s.shape) for s in outs[3:]],
        out_shape=outs,
        scratch_shapes=[pltpu.VMEM((tm, MIX), F32), ext, ext, ext, ext,
                        pltpu.VMEM((HALO, BW), F32), pltpu.VMEM((HALO, BW), F32), pltpu.VMEM((CONF_K * 8, BW), F32),
                        pltpu.VMEM((7, tm + HALO, BW), F32),
                        pltpu.VMEM((tm, BW), F32), pltpu.VMEM((tm, BW), BF16), pltpu.VMEM((tm, BW), BF16)],
        compiler_params=pltpu.CompilerParams(dimension_semantics=("arbitrary",),
                                             vmem_limit_bytes=VMEM_LIMIT_ODD_BWD_V7X),
    )(dres, o, p, p, conv, wbd, cscale, dww, dwb, lng, lnb, wpw, pwb, kv, wout, pg)


def _pick_rows(n):
    for rows in (3200, 2432, 1024, 768):
        if n % rows == 0:
            return rows
    return n


def _pad_rows(a, rows):
    return jnp.pad(a, ((0, rows - a.shape[0]), (0, 0)))


def _step(x, mem, tgt, ex):
    t = x.shape[0]
    tm = min(512, t)
    w, deps = ex.first()
    causal = jnp.tril(jnp.ones((CHUNK, CHUNK), bool))
    ws = jnp.where(causal[None], w["even_a_ws"], 0.0).astype(BF16)
    wst = jnp.transpose(ws, (0, 2, 1))
    bmap = jnp.repeat(w["even_a_bs"].T, GRP, axis=1)
    wc = _pad_rows(w["even_b_conv"], 8)
    wbd = jax.scipy.linalg.block_diag(*[w["odd_c_wgrp"][g] for g in range(NH)]).astype(BF16)
    dww = _pad_rows(w["odd_d_dw_w"], CONF_K + 1)
    tk = min(1024, t)

    p_e, h_e = _rms_matmul(x, w["even_pre_g"], w["even_w_in"], tm=tm, name="in_even", transposed=True, deps=deps)
    w.update(ex.even_rest(h_e))
    kv_e, memn_e = _rms_matmul(mem, w["even_mem_g"], w["even_w_kv"], tm=N_MEM, name="kv_even", out_dtype=BF16)
    even_args = (w["even_a_ln_g"], w["even_a_ln_b"], ws)
    o_e, x1 = _even_fwd(x, p_e, *even_args, bmap, wc, kv_e, w["even_w_out"], w["even_post_g"])
    w.update(ex.odd(o_e))
    kv_o, memn_o = _rms_matmul(mem, w["odd_mem_g"], w["odd_w_kv"], tm=N_MEM, name="kv_odd", out_dtype=BF16)
    p_o, h_o = _rms_matmul(x1, w["odd_pre_g"], w["odd_w_in"], tm=tm, name="in_odd", transposed=True)
    odd_args = (wbd, w["odd_c_scale"], dww, w["odd_d_dw_b"], w["odd_d_ln_g"], w["odd_d_ln_b"], w["odd_d_pw_w"],
                w["odd_d_pw_b"], kv_o, w["odd_w_out"], w["odd_post_g"])
    o_o, dres, loss, conv_o = _odd_fwd(x1, tgt, p_o, *odd_args)

    g = {}
    (dp_o, y_o, do_o, post_g_o, dwbd, g["odd_c_scale"], ddww, g["odd_d_dw_b"], g["odd_d_ln_g"], g["odd_d_ln_b"],
     dwpw, g["odd_d_pw_b"], dkv_o) = _odd_bwd(dres, o_o, p_o, conv_o, *odd_args)
    g["odd_post_g"] = post_g_o
    g["odd_d_dw_w"] = ddww[:CONF_K]
    dw_kv_o, g["odd_mem_g"] = _kv_backward(memn_o, dkv_o.astype(BF16), w["odd_w_kv"], mem, name="kv_bwd_odd")
    deps = ex.send("odd", {
        "odd_w_in": _tn_matmul(dp_o, h_o, tmc=_pick_rows(ODD_IN), tk=tk, out_dtype=BF16, name="dw_in_odd"),
        "odd_w_out": _tn_matmul(y_o, do_o, tmc=MIX, tk=tk, out_dtype=BF16, name="dw_out_odd"),
        "odd_w_kv": dw_kv_o, "odd_d_pw_w": dwpw.astype(BF16), "loss": loss,
        "odd_c_wgrp": jnp.concatenate([dwbd[i * GRP:(i + 1) * GRP, i * GRP:(i + 1) * GRP] for i in range(NH)])})
    dx1, g["odd_pre_g"] = _nt_matmul_rms_bwd(dp_o, w["odd_w_in"], x1, w["odd_pre_g"], dres, tm=tm,
                                             name="dx_odd", transposed=True, deps=deps)

    (dp_e, y_e, do_e, post_g_e, dws, dbs, ln_g_e, ln_b_e, dwc, dkv_e) = _even_bwd(
        dx1, o_e, p_e, *even_args, wst, bmap, wc, kv_e, w["even_w_out"], w["even_post_g"])
    g["even_b_conv"] = dwc[:3]
    dw_kv_e, mem_g_e = _kv_backward(memn_e, dkv_e.astype(BF16), w["even_w_kv"], mem, name="kv_bwd_even")
    deps = ex.send("even_rest", {
        "even_w_out": _tn_matmul(y_e, do_e, tmc=MIX, tk=tk, out_dtype=BF16, name="dw_out_even"),
        "even_w_kv": dw_kv_e, "even_a_ln_g": ln_g_e, "even_a_ln_b": ln_b_e,
        "even_a_ws": dws.reshape(NH * CHUNK, CHUNK), "even_a_bs": dbs[:, ::HD].T})
    g["even_w_in"] = _tn_matmul(dp_e, h_e, tmc=_pick_rows(EVEN_IN), tk=tk, out_dtype=BF16, name="dw_in_even",
                                deps=deps)
    deps = ex.send("even_in", g)
    grad_x, pre_g_e = _nt_matmul_rms_bwd(dp_e, w["even_w_in"], x, w["even_pre_g"], dx1, tm=tm,
                                         name="dx_even", transposed=True, deps=deps)
    deps = ex.send("even_gains", {"even_pre_g": pre_g_e, "even_mem_g": mem_g_e, "even_post_g": post_g_e})
    return grad_x, deps


def _place():
    return lax.axis_index("x"), lax.axis_index("y"), lax.axis_index("c")


def _index(px, py, pc):
    return 4 * px + 2 * py + pc


_COPIES = N_DEV - 1


def _all_gather(arrs, name):
    n = len(arrs)

    def body(*refs):
        ins, outs = refs[:n], refs[n:2 * n]
        send_sems, recv_sems, local_sems = refs[2 * n:]
        x, y, c = _place()
        me, sibling = (x, y, c), (x, y, 1 - c)
        chips = [(1 - x, y), (x, 1 - y), (1 - x, 1 - y)]

        def copy(a, k, block, to, src=None):
            dst = outs[a].at[_index(*block)]
            return pltpu.make_async_remote_copy(
                src_ref=dst if src is None else src, dst_ref=dst, send_sem=send_sems.at[a * _COPIES + k],
                recv_sem=recv_sems.at[a * _COPIES + k], device_id=to, device_id_type=MESH)

        mine = [pltpu.make_async_copy(ins[a], outs[a].at[_index(*me)], local_sems.at[a]) for a in range(n)]
        first = []
        for a in range(n):
            mine[a].start()
            first.append(copy(a, 0, me, sibling, src=ins[a]))
            first += [copy(a, 1 + j, me, (*chip, c), src=ins[a]) for j, chip in enumerate(chips)]
        for cp in first:
            cp.start()
        passed = []
        for j, chip in enumerate(chips):
            for a in range(n):
                copy(a, 1 + j, (*chip, c), me).wait_recv()
                passed.append(copy(a, 4 + j, (*chip, c), sibling))
                passed[-1].start()
        for a in range(n):
            copy(a, 0, sibling, me).wait_recv()
            for j, chip in enumerate(chips):
                copy(a, 4 + j, (*chip, 1 - c), me).wait_recv()
        for cp in first + passed:
            cp.wait_send()
        for cp in mine:
            cp.wait()

    return pl.pallas_call(
        body, name=name, in_specs=[_ANY] * n, out_specs=[_ANY] * n,
        out_shape=[S((N_DEV,) + a.shape, a.dtype) for a in arrs],
        scratch_shapes=[pltpu.SemaphoreType.DMA((n * _COPIES,)), pltpu.SemaphoreType.DMA((n * _COPIES,)),
                        pltpu.SemaphoreType.DMA((n,))],
    )(*arrs)


_HBM = pl.BlockSpec(memory_space=pltpu.HBM)
_SEM = pl.BlockSpec(memory_space=pltpu.SEMAPHORE)
_EFFECT = pltpu.SideEffectType.DATAFLOW_SIDE_EFFECTING


_ALL_FLIPS = [(k >> 2 & 1, k >> 1 & 1, k & 1) for k in range(1, N_DEV)]
_CHIP_FLIPS = [(1, 0, 0), (0, 1, 0), (1, 1, 0)]
_FLIPS = {"gather": _ALL_FLIPS, "scatter": _ALL_FLIPS, "gather_chips": [(0, 0, 1)] + _CHIP_FLIPS,
          "scatter_chips": _CHIP_FLIPS}


def _landing_shape(kind, a):
    return (N_DEV,) + a.shape if kind.startswith("gather") else a.shape


def _exchange_copies(kinds, srcs, lands, send_sems, recv_sems, local_sems, arriving):
    x, y, c = _place()
    mine = _index(x, y, c)
    remote, local = [], []
    for a, kind in enumerate(kinds):
        by_chip = kind == "scatter_chips"
        here = 2 * x + y if by_chip else mine
        own = srcs[a] if kind.startswith("gather") else srcs[a].at[here]
        local.append(pltpu.make_async_copy(own, lands[a].at[here], local_sems.at[a]))
        for k, (fx, fy, fc) in enumerate(_FLIPS[kind]):
            peer = (1 - x if fx else x, 1 - y if fy else y, 1 - c if fc else c)
            there = 2 * peer[0] + peer[1] if by_chip else _index(*peer)
            remote.append(pltpu.make_async_remote_copy(
                src_ref=srcs[a] if kind.startswith("gather") else srcs[a].at[there],
                dst_ref=lands[a].at[there if arriving else here],
                send_sem=send_sems.at[a * _COPIES + k], recv_sem=recv_sems.at[a * _COPIES + k],
                device_id=peer, device_id_type=MESH))
    return remote, local


def _exchange_start(items, name, deps=()):
    kinds = [kind for kind, _ in items]
    srcs = [a for _, a in items]
    n = len(items)
    lands = [lax.empty(_landing_shape(kind, a), a.dtype) for kind, a in items]

    def body(*refs):
        send_sems, recv_sems, local_sems = refs[2 * n + len(deps):2 * n + len(deps) + 3]
        remote, local = _exchange_copies(kinds, refs[:n], refs[n:2 * n], send_sems, recv_sems, local_sems, False)
        for cp in local + remote:
            cp.start()
        refs[-1][...] = jnp.zeros_like(refs[-1])

    held = [pltpu.HBM(a.shape, a.dtype) for a in srcs + lands]
    res = pl.pallas_call(
        body, name=name,
        out_shape=(pltpu.SemaphoreType.DMA((n * _COPIES,)), pltpu.SemaphoreType.DMA((n * _COPIES,)),
                   pltpu.SemaphoreType.DMA((n,)), *held, S((8, 128), F32)),
        in_specs=[_HBM] * (2 * n) + [_ANY] * len(deps),
        out_specs=(_SEM, _SEM, _SEM, *[_HBM] * (2 * n), _whole()),
        input_output_aliases={i: 3 + i for i in range(2 * n)},
        compiler_params=pltpu.CompilerParams(has_side_effects=_EFFECT),
    )(*[pltpu.with_memory_space_constraint(a, pltpu.HBM) for a in srcs + lands], *deps)
    return (kinds, res[:3], res[3:3 + 2 * n]), res[-1]


def _exchange_wait(handle, after, name):
    kinds, sems, held = handle
    n = len(kinds)

    def body(*refs):
        send_sems, recv_sems, local_sems = refs[2 * n:2 * n + 3]
        remote, local = _exchange_copies(kinds, refs[:n], refs[n:2 * n], send_sems, recv_sems, local_sems, True)
        for cp in remote:
            cp.wait_send()
            cp.wait_recv()
        for cp in local:
            cp.wait()

    res = pl.pallas_call(
        body, name=name, out_shape=[pltpu.HBM(a.shape, a.dtype) for a in held],
        in_specs=[_HBM] * (2 * n) + [_SEM] * 3 + [_ANY] * len(after), out_specs=[_HBM] * (2 * n),
        input_output_aliases={i: i for i in range(2 * n)},
        compiler_params=pltpu.CompilerParams(has_side_effects=_EFFECT),
    )(*held, *sems, *after)
    return res[n:]


_CHIPS = [(0, 0), (0, 1), (1, 0), (1, 1)]
_N_CHIPS = len(_CHIPS)


def _sibling_forward(lands, name):
    n = len(lands)

    def body(*refs):
        ins, outs = refs[:n], refs[n:2 * n]
        send_sems, recv_sems = refs[2 * n:]
        x, y, c = _place()
        sent, arriving = [], []
        for a in range(n):
            for j, (fx, fy, _) in enumerate(_CHIP_FLIPS):
                chip = (1 - x if fx else x, 1 - y if fy else y)
                sems = dict(send_sem=send_sems.at[a * 3 + j], recv_sem=recv_sems.at[a * 3 + j],
                            device_id=(x, y, 1 - c), device_id_type=MESH)
                mine, theirs = _index(*chip, c), _index(*chip, 1 - c)
                sent.append(pltpu.make_async_remote_copy(src_ref=ins[a].at[mine], dst_ref=outs[a].at[mine], **sems))
                arriving.append(pltpu.make_async_remote_copy(src_ref=ins[a].at[theirs], dst_ref=outs[a].at[theirs],
                                                             **sems))
        for cp in sent:
            cp.start()
        for cp in sent:
            cp.wait_send()
        for cp in arriving:
            cp.wait_recv()

    return pl.pallas_call(
        body, name=name, in_specs=[_ANY] * n, out_specs=[_ANY] * n,
        out_shape=[S(a.shape, a.dtype) for a in lands], input_output_aliases={a: a for a in range(n)},
        scratch_shapes=[pltpu.SemaphoreType.DMA((3 * n,)), pltpu.SemaphoreType.DMA((3 * n,))],
    )(*lands)


def _sibling_swap(arrs, name):
    n = len(arrs)

    def body(*refs):
        ins, outs = refs[:n], refs[n:2 * n]
        send_sems, recv_sems = refs[2 * n:]
        x, y, c = _place()
        copies = []
        for a in range(n):
            for q, chip in enumerate(_CHIPS):
                copies.append(pltpu.make_async_remote_copy(
                    src_ref=ins[a].at[_index(*chip, 1 - c)], dst_ref=outs[a].at[q],
                    send_sem=send_sems.at[a * _N_CHIPS + q], recv_sem=recv_sems.at[a * _N_CHIPS + q],
                    device_id=(x, y, 1 - c), device_id_type=MESH))
        for cp in copies:
            cp.start()
        for cp in copies:
            cp.wait_send()
            cp.wait_recv()

    return pl.pallas_call(
        body, name=name, in_specs=[_ANY] * n, out_specs=[_ANY] * n,
        out_shape=[S((_N_CHIPS,) + a.shape[1:], a.dtype) for a in arrs],
        scratch_shapes=[pltpu.SemaphoreType.DMA((_N_CHIPS * n,)), pltpu.SemaphoreType.DMA((_N_CHIPS * n,))],
    )(*arrs)


def _add_partials(mine, theirs, *, tr, name):
    _, r, c = mine.shape

    def body(mine_ref, theirs_ref, out_ref):
        out_ref[0] = (mine_ref[0].astype(F32) + theirs_ref[0].astype(F32)).astype(out_ref.dtype)

    return pl.pallas_call(
        body, name=name, grid=(_N_CHIPS, r // tr),
        in_specs=[pl.BlockSpec((1, tr, c), lambda q, i: (2 * q + lax.axis_index("c"), i, 0)),
                  pl.BlockSpec((1, tr, c), lambda q, i: (q, i, 0))],
        out_specs=pl.BlockSpec((1, tr, c), lambda q, i: (q, i, 0)),
        out_shape=S((_N_CHIPS, r, c), mine.dtype),
        compiler_params=_params(("arbitrary", "arbitrary")),
    )(mine, theirs)


def _adamw(w, g, m, v):
    m = ADAM_B1 * m + (1.0 - ADAM_B1) * g
    v = ADAM_B2 * v + (1.0 - ADAM_B2) * (g * g)
    m_hat = m / (1.0 - ADAM_B1 ** ADAM_STEP)
    v_hat = v / (1.0 - ADAM_B2 ** ADAM_STEP)
    return -ADAM_LR * (m_hat / (jnp.sqrt(v_hat) + ADAM_EPS) + ADAM_WD * w), m, v


def _sum_devices(ref, rows):
    total = ref[0, rows, :].astype(F32)
    for s in range(1, ref.shape[0]):
        total = total + ref[s, rows, :].astype(F32)
    return total


def _adam_big(recv, w, m, v, *, tr, name):
    r, c = w.shape

    def body(recv_ref, w_ref, m_ref, v_ref, g_ref, d_ref, m2_ref, v2_ref):
        g = _sum_devices(recv_ref, slice(None))
        g_ref[...] = g
        d_ref[...], m2_ref[...], v2_ref[...] = _adamw(w_ref[...], g, m_ref[...], v_ref[...])

    blk = pl.BlockSpec((tr, c), lambda i: (i, 0))
    return pl.pallas_call(
        body, name=name, grid=(r // tr,),
        in_specs=[pl.BlockSpec((recv.shape[0], tr, c), lambda i: (0, i, 0)), blk, blk, blk],
        out_specs=[blk] * 4, out_shape=[S((r, c), F32)] * 4,
        compiler_params=_params(("arbitrary",)),
    )(recv, w, m, v)


_REPLICATED = {"even_pre_g": (0, 0, 1), "even_mem_g": (0, 8, 1), "even_post_g": (0, 16, 1),
               "even_a_ln_g": (1, 0, 1), "even_a_ln_b": (1, 8, 1),
               "even_a_ws": (2, 0, NH * CHUNK), "even_a_bs": (2, NH * CHUNK, NH),
               "odd_c_wgrp": (3, 0, NH * GRP)}
_SHARDED = {"odd_pre_g": (4, 0, 1), "odd_mem_g": (4, 8, 1), "odd_post_g": (4, 16, 1),
            "even_b_conv": (5, 0, 3), "odd_c_scale": (5, 8, 1), "odd_d_dw_w": (5, 16, CONF_K),
            "odd_d_dw_b": (5, 48, 1), "odd_d_ln_g": (5, 56, 1), "odd_d_ln_b": (5, 64, 1), "odd_d_pw_b": (5, 72, 1)}
_SMALL = {**_REPLICATED, **_SHARDED}
_SMALL_ROWS = {0: 24, 1: 16, 2: NH * CHUNK + 8, 3: NH * GRP, 4: 24, 5: 80}


def _adam_small(sources, wmv):
    names = list(_SMALL)
    ns = len(sources)

    def body(*refs):
        src = refs[:ns]
        ins = refs[ns:ns + 3 * len(names)]
        outs = refs[ns + 3 * len(names):]
        outs[-1][...] = _sum_devices(src[-1], slice(0, 1))
        for i, nm in enumerate(names):
            a, row0, rows = _SMALL[nm]
            g = _sum_devices(src[a], slice(row0, row0 + rows))
            w_ref, m_ref, v_ref = ins[3 * i:3 * i + 3]
            g_ref, d_ref, m2_ref, v2_ref = outs[4 * i:4 * i + 4]
            g_ref[...] = g
            d_ref[...], m2_ref[...], v2_ref[...] = _adamw(w_ref[...], g, m_ref[...], v_ref[...])

    flat = [t for nm in names for t in wmv[nm]]
    out_shape = [S(wmv[nm][0].shape, F32) for nm in names for _ in range(4)] + [S((1, HD), F32)]
    res = pl.pallas_call(
        body, name="adam_small", in_specs=[_whole()] * (ns + len(flat)), out_specs=[_whole()] * len(out_shape),
        out_shape=out_shape, compiler_params=_params(),
    )(*sources, *flat)
    return {nm: tuple(res[4 * i:4 * i + 4]) for i, nm in enumerate(names)}, res[-1]


_WEIGHTS = ["even_pre_g", "even_w_in", "even_a_ln_g", "even_a_ln_b", "even_a_ws", "even_a_bs", "even_b_conv",
            "even_mem_g", "even_w_kv", "even_w_out", "even_post_g", "odd_pre_g", "odd_w_in", "odd_c_wgrp",
            "odd_c_scale", "odd_d_dw_w", "odd_d_dw_b", "odd_d_ln_g", "odd_d_ln_b", "odd_d_pw_w", "odd_d_pw_b",
            "odd_mem_g", "odd_w_kv", "odd_w_out", "odd_post_g"]
_TRANSPOSED = ["even_w_in", "odd_w_in"]
_BIG = _TRANSPOSED + ["even_w_kv", "even_w_out", "odd_w_kv", "odd_w_out", "odd_d_pw_w"]
_BIG_TILE_ROWS = {"even_w_in": 400, "odd_w_in": 304, "even_w_kv": 128, "even_w_out": 128, "odd_w_kv": 128,
                  "odd_w_out": 128, "odd_d_pw_w": 96}


def _view2d(a, transposed):
    a = a[0]
    if a.ndim == 1:
        return a[None]
    if transposed:
        return a.T
    return a.reshape(-1, a.shape[-1])


def _rows8(a):
    return _pad_rows(a, -(-a.shape[0] // 8) * 8)


def _pack_rows(parts):
    return jnp.concatenate([_rows8(p) for p in parts], axis=0)


def _unshard_cols(a):
    return jnp.transpose(a, (1, 0, 2)).reshape(a.shape[1], N_DEV * a.shape[2])


def _shard_cols(a):
    return jnp.transpose(a.reshape(a.shape[0], N_DEV, a.shape[1] // N_DEV), (1, 0, 2))


def _rows_of(a):
    return a.reshape(-1, a.shape[-1])


_GROUPS = {"odd": (["odd_w_in", "odd_w_out", "odd_w_kv", "odd_d_pw_w"], [3], []),
           "even_rest": (["even_w_out", "even_w_kv"], [1, 2], []),
           "even_in": (["even_w_in"], [], [4, 5]),
           "even_gains": ([], [0], [])}


_TWO_LEVEL = ("even_in",)


class _MeshExchange:
    def __init__(self, shard):
        self.shard = shard
        self.handles = {}

    def first(self):
        shard = self.shard
        packs = [_pack_rows([shard[nm] for nm in _SHARDED if _SHARDED[nm][0] == a]) for a in (4, 5)]
        w_in, p128, p96 = _all_gather([shard["even_w_in"].astype(BF16)] + packs, "gather_first")
        w = {nm: shard[nm] for nm in _REPLICATED}
        w["even_a_ws"] = w["even_a_ws"].reshape(NH, CHUNK, CHUNK)
        w["odd_c_wgrp"] = w["odd_c_wgrp"].reshape(NH, GRP, GRP)
        w["even_w_in"] = _rows_of(w_in)
        full_packs = {4: _unshard_cols(p128), 5: _unshard_cols(p96)}
        for nm, (a, row0, rows) in _SHARDED.items():
            w[nm] = full_packs[a][row0:row0 + rows]
        later = lambda names: [("gather_chips", shard[nm].astype(BF16)) for nm in names]
        self.handles["w_even"], token = _exchange_start(later(["even_w_kv", "even_w_out"]), "gather_even_start",
                                                        deps=(w_in,))
        self.handles["w_odd"], token = _exchange_start(later(["odd_w_in", "odd_w_kv", "odd_w_out", "odd_d_pw_w"]),
                                                       "gather_odd_start", deps=(token,))
        return w, (token,)

    def even_rest(self, after):
        landed = _exchange_wait(self.handles.pop("w_even"), (after,), "gather_even_wait")
        kv, out = _sibling_forward(landed, "forward_even")
        return {"even_w_kv": _rows_of(kv), "even_w_out": _rows_of(out)}

    def odd(self, after):
        landed = _exchange_wait(self.handles.pop("w_odd"), (after,), "gather_odd_wait")
        w_in, kv, out, pw = _sibling_forward(landed, "forward_odd")
        return {"odd_w_in": _rows_of(w_in), "odd_w_kv": _rows_of(kv), "odd_w_out": _rows_of(out),
                "odd_d_pw_w": _rows_of(pw)}

    def send(self, group, g):
        big, replicated, sharded = _GROUPS[group]
        by_owner = [g[nm].reshape(N_DEV, -1, g[nm].shape[-1]) for nm in big]
        if group in _TWO_LEVEL:
            theirs = _sibling_swap(by_owner, "swap_" + group)
            items = [("scatter_chips", _add_partials(a, b, tr=_BIG_TILE_ROWS[nm], name="chip_sum_" + nm))
                     for nm, a, b in zip(big, by_owner, theirs)]
        else:
            items = [("scatter", a) for a in by_owner]
        items += [("gather", _pack_rows([g[nm] for nm in _REPLICATED if _REPLICATED[nm][0] == a]))
                  for a in replicated]
        items += [("scatter", _shard_cols(_pack_rows([g[nm] for nm in _SHARDED if _SHARDED[nm][0] == a])))
                  for a in sharded]
        if group == "odd":
            items.append(("gather", _rows8(g["loss"])))
        self.handles[group], token = _exchange_start(items, "send_" + group + "_start")
        return (token,)

    def receive(self, group, after):
        after = after if isinstance(after, tuple) else (after,)
        return _exchange_wait(self.handles.pop(group), after, "send_" + group + "_wait")


def kernel(x, mem, even_pre_g, even_w_in, even_a_ln_g, even_a_ln_b, even_a_ws, even_a_bs, even_b_conv, even_mem_g, even_w_kv, even_w_out, even_post_g, odd_pre_g, odd_w_in, odd_c_wgrp, odd_c_scale, odd_d_dw_w, odd_d_dw_b, odd_d_ln_g, odd_d_ln_b, odd_d_pw_w, odd_d_pw_b, odd_mem_g, odd_w_kv, odd_w_out, odd_post_g, loss_target, m_even_pre_g, m_even_w_in, m_even_a_ln_g, m_even_a_ln_b, m_even_a_ws, m_even_a_bs, m_even_b_conv, m_even_mem_g, m_even_w_kv, m_even_w_out, m_even_post_g, m_odd_pre_g, m_odd_w_in, m_odd_c_wgrp, m_odd_c_scale, m_odd_d_dw_w, m_odd_d_dw_b, m_odd_d_ln_g, m_odd_d_ln_b, m_odd_d_pw_w, m_odd_d_pw_b, m_odd_mem_g, m_odd_w_kv, m_odd_w_out, m_odd_post_g, v_even_pre_g, v_even_w_in, v_even_a_ln_g, v_even_a_ln_b, v_even_a_ws, v_even_a_bs, v_even_b_conv, v_even_mem_g, v_even_w_kv, v_even_w_out, v_even_post_g, v_odd_pre_g, v_odd_w_in, v_odd_c_wgrp, v_odd_c_scale, v_odd_d_dw_w, v_odd_d_dw_b, v_odd_d_ln_g, v_odd_d_ln_b, v_odd_d_pw_w, v_odd_d_pw_b, v_odd_mem_g, v_odd_w_kv, v_odd_w_out, v_odd_post_g):
    given = dict(locals())
    view = lambda nm, kind: _view2d(given[kind + nm], nm in _TRANSPOSED)
    shard = {nm: view(nm, "") for nm in _WEIGHTS}
    wmv = {nm: (shard[nm], view(nm, "m_"), view(nm, "v_")) for nm in _WEIGHTS}

    ex = _MeshExchange(shard)
    grad_x, last = _step(x[0], mem[0], loss_target[0], ex)

    res = {}

    def update(group, after):
        names = _GROUPS[group][0]
        landed = ex.receive(group, after)
        for nm, recv in zip(names, landed):
            res[nm] = _adam_big(recv, *wmv[nm], tr=_BIG_TILE_ROWS[nm], name="adam_" + nm)
        return landed[len(names):]

    c192, losses = update("odd", last)
    c768, c128 = update("even_rest", res["odd_d_pw_w"][0])
    a128, a96 = update("even_in", res["even_w_kv"][0])
    (c1024,) = update("even_gains", res["even_w_in"][0])
    small, loss = _adam_small([c1024, c768, c128, c192, a128, a96, losses], {nm: wmv[nm] for nm in _SMALL})
    res.update(small)
    total = loss[0, 0]
    back = lambda nm, a: (a.T if nm in _TRANSPOSED else a).reshape(given[nm].shape)
    outs = [[back(nm, res[nm][i]) for nm in _WEIGHTS] for i in range(4)]
    return (total, grad_x[None], *outs[0], *outs[1], *outs[2], *outs[3])
```
